```python
import jax, jax.numpy as jnp
from jax import lax
import numpy as np

D_MODEL = 1024
BATCH = 16
SEQ = 256
DEPTH = 4
DEC_BATCH = 2
DEC_SEQ = 2048
PAST_LEN = 256

GRID_W = 64
D_RNN = 1024
N_RNN_BLOCKS = 16
RNN_BLOCK = D_RNN // N_RNN_BLOCKS
CONV_W = 4
CONV_LEFT = 2
LRU_C = 8.0
N_HEADS = 8
N_KV_HEADS = 2
HEAD_DIM = 128
AXIS_DIM = HEAD_DIM // 2
ROPE_THETA = 10000.0
Q_BLOCK = 128
Q_DIM = N_HEADS * HEAD_DIM
KV_DIM = N_KV_HEADS * HEAD_DIM
D_FF = 2816
N_EXPERTS = 8
TOP_K = 2
D_EXPERT = 1408
N_DENSE = (DEPTH + 1) // 2
N_MOE = DEPTH // 2
EPS = 1e-6
D_IN = 2 * D_RNN + Q_DIM + 2 * KV_DIM + 2 * D_MODEL
SPLITS = [D_RNN, 2 * D_RNN, 2 * D_RNN + Q_DIM, 2 * D_RNN + Q_DIM + KV_DIM,
          2 * D_RNN + Q_DIM + 2 * KV_DIM, 2 * D_RNN + Q_DIM + 2 * KV_DIM + D_MODEL]

kernel_name = "hybrid_rglru_gqa_prefix_diffusion_step"

F32 = jnp.float32


def rms_norm(x, g):
    x32 = x.astype(F32)
    y = x32 * lax.rsqrt(jnp.mean(x32 * x32, axis=-1, keepdims=True) + EPS)
    return (y * g.astype(F32)).astype(x.dtype)


def adaln(cond, w, b):
    m = jax.nn.silu(cond) @ w + b
    return [t[:, None, :] for t in jnp.split(m, 6, axis=-1)]


def dw_conv(x, w, b):
    T = x.shape[1]
    xp = jnp.pad(x, ((0, 0), (CONV_LEFT, CONV_W - 1 - CONV_LEFT), (0, 0)))
    y = b
    for k in range(CONV_W):
        y = y + xp[:, k:k + T] * w[k]
    return y


def block_diag(x, w, b):
    B, T, _ = x.shape
    y = jnp.einsum('btnj,njk->btnk', x.reshape(B, T, N_RNN_BLOCKS, RNN_BLOCK), w.astype(F32))
    return y.reshape(B, T, D_RNN) + b.astype(F32)


def _lin_combine(left, right):
    a1, b1 = left
    a2, b2 = right
    return a1 * a2, a2 * b1 + b2


def rglru_scan(xc, wa, ba, wi, bi, lam, h0, reverse):
    x32 = xc.astype(F32)
    r = jax.nn.sigmoid(block_diag(x32, wa, ba))
    i = jax.nn.sigmoid(block_diag(x32, wi, bi))
    log_a = -LRU_C * r * jax.nn.softplus(-lam.astype(F32))
    a = jnp.exp(log_a)
    bterm = jnp.sqrt(-jnp.expm1(2.0 * log_a)) * (i * x32)
    if reverse:
        a, bterm = jnp.flip(a, 1), jnp.flip(bterm, 1)
    a_cum, b_cum = lax.associative_scan(_lin_combine, (a, bterm), axis=1)
    h = a_cum * h0.astype(F32)[:, None, :] + b_cum
    if reverse:
        h = jnp.flip(h, 1)
    return h


def rnn_branch(xc, yr, wa, ba, wi, bi, lam, h0f, h0b):
    hf = rglru_scan(xc, wa[0], ba[0], wi[0], bi[0], lam[0], h0f, False)
    hb = rglru_scan(xc, wa[1], ba[1], wi[1], bi[1], lam[1], h0b, True)
    out = (hf + hb) * jax.nn.gelu(yr.astype(F32))
    return out.astype(xc.dtype), hf, hb


def head_rms(x, g):
    return rms_norm(x, g)


def rope_angles(T):
    rows = T // GRID_W
    t = jnp.arange(rows * GRID_W)
    pos = jnp.stack([t // GRID_W, t % GRID_W], axis=-1).astype(F32)
    inv = ROPE_THETA ** (-jnp.arange(0, AXIS_DIM, 2, dtype=F32) / AXIS_DIM)
    ang = pos[..., None] * inv
    return jnp.cos(ang), jnp.sin(ang)


def apply_rope(x, cos, sin):
    B, T, H, _ = x.shape
    xr = x.astype(F32).reshape(B, T, H, 2, 2, AXIS_DIM // 2)
    x1, x2 = xr[..., 0, :], xr[..., 1, :]
    c, s = cos[:, None], sin[:, None]
    out = jnp.stack([x1 * c - x2 * s, x2 * c + x1 * s], axis=-2)
    return out.reshape(B, T, H, HEAD_DIM).astype(x.dtype)


def attend(q, k, v):
    B, T, H, Dh = q.shape
    KV = k.shape[2]
    G = H // KV
    scale = 1.0 / np.sqrt(Dh)
    qb = q.reshape(B, T // Q_BLOCK, Q_BLOCK, KV, G, Dh).transpose(1, 0, 2, 3, 4, 5)

    def one_block(qblk):
        s = jnp.einsum('bqkgd,bskd->bkgqs', qblk, k).astype(F32) * scale
        p = jax.nn.softmax(s, axis=-1).astype(v.dtype)
        return jnp.einsum('bkgqs,bskd->bqkgd', p, v)

    o = lax.map(one_block, qb)
    return o.transpose(1, 0, 2, 3, 4, 5).reshape(B, T, H * Dh)


def swiglu(h, wg, wu, wd):
    return (jax.nn.silu(h @ wg) * (h @ wu)) @ wd


def moe_swiglu(h, wr, br, wg, wu, wd):
    logits = (h @ wr).astype(F32) + br.astype(F32)
    top_v, top_i = lax.top_k(logits, TOP_K)
    w = jax.nn.softmax(top_v, axis=-1)
    gates = jnp.sum(jax.nn.one_hot(top_i, N_EXPERTS, dtype=F32) * w[..., None], axis=-2)
    act = jax.nn.silu(jnp.einsum('btd,edf->btef', h, wg)) * jnp.einsum('btd,edf->btef', h, wu)
    return jnp.einsum('btef,efd->btd', act * gates[..., None].astype(h.dtype), wd)


def mixer_block(h, W, l, rope, h0f, h0b, kv_ctx):
    B, T, _ = h.shape
    z = h @ W['w_in'][l]
    xr, yr, q, k, v, g_r, g_a = jnp.split(z, SPLITS, axis=-1)
    xc = dw_conv(xr, W['conv_w'][l], W['conv_b'][l])
    out_r, hf, hb = rnn_branch(xc, yr, W['lru_wa'][l], W['lru_ba'][l], W['lru_wi'][l],
                               W['lru_bi'][l], W['lru_lambda'][l], h0f, h0b)
    q = head_rms(q.reshape(B, T, N_HEADS, HEAD_DIM), W['q_norm'][l])
    k = head_rms(k.reshape(B, T, N_KV_HEADS, HEAD_DIM), W['k_norm'][l])
    v = v.reshape(B, T, N_KV_HEADS, HEAD_DIM)
    if rope is not None:
        q = apply_rope(q, rope[0], rope[1])
        k_lat = apply_rope(k, rope[0], rope[1])
        k_all = jnp.concatenate([kv_ctx[0].astype(k.dtype), k_lat], axis=1)
        v_all = jnp.concatenate([kv_ctx[1].astype(v.dtype), v], axis=1)
    else:
        k_all, v_all = k, v
    out_a = attend(q, k_all, v_all)
    merged = (jax.nn.sigmoid(g_r) * (out_r @ W['w_rnn_out'][l])
              + jax.nn.sigmoid(g_a) * (out_a @ W['w_attn_out'][l]))
    return merged @ W['w_out'][l], k, v, hf, hb


def channel_block(h, W, l):
    j = l // 2
    if l % 2 == 0:
        return swiglu(h, W['ff_gate'][j], W['ff_up'][j], W['ff_down'][j])
    return moe_swiglu(h, W['router_w'][j], W['router_b'][j], W['exp_gate'][j],
                      W['exp_up'][j], W['exp_down'][j])


def layer(x, cond, W, l, rope, h0f, h0b, kv_ctx):
    sh1, sc1, g1, sh2, sc2, g2 = adaln(cond, W['w_mod'][l], W['b_mod'][l])
    h = rms_norm(x, W['norm1'][l]) * (1 + sc1) + sh1
    mix, k, v, hf, hb = mixer_block(h, W, l, rope, h0f, h0b, kv_ctx)
    x = x + g1 * mix
    h = rms_norm(x, W['norm2'][l]) * (1 + sc2) + sh2
    x = x + g2 * channel_block(h, W, l)
    return x, k, v, hf, hb


def setup_inputs(seed: int = 0) -> dict:
    key = jax.random.key(seed)
    ks = jax.random.split(key, 40)
    n = lambda i, shape, s: jax.random.normal(ks[i], shape, F32) * s
    u = jax.random.uniform(ks[39], (DEPTH, 2, D_RNN), F32, 0.9, 0.999) ** (1.0 / LRU_C)
    return {
        "x_prompt": n(0, (BATCH, SEQ, D_MODEL), 1.0),
        "x_sample": n(1, (DEC_BATCH, DEC_SEQ, D_MODEL), 1.0),
        "cache_k": n(2, (DEC_BATCH, DEPTH, PAST_LEN, N_KV_HEADS, HEAD_DIM), 1.0),
        "cache_v": n(3, (DEC_BATCH, DEPTH, PAST_LEN, N_KV_HEADS, HEAD_DIM), 1.0),
        "state_rnn": n(4, (DEC_BATCH, DEPTH, 2, D_RNN), 0.5),
        "c": n(5, (DEC_BATCH, D_MODEL), 1.0),
        "c_ctx": n(6, (D_MODEL,), 1.0),
        "w_mod": n(7, (DEPTH, D_MODEL, 6 * D_MODEL), 0.5 * D_MODEL ** -0.5),
        "b_mod": n(8, (DEPTH, 6 * D_MODEL), 0.02),
        "norm1": 1.0 + n(9, (DEPTH, D_MODEL), 0.02),
        "norm2": 1.0 + n(10, (DEPTH, D_MODEL), 0.02),
        "w_in": n(11, (DEPTH, D_MODEL, D_IN), D_MODEL ** -0.5),
        "conv_w": n(12, (DEPTH, CONV_W, D_RNN), CONV_W ** -0.5),
        "conv_b": n(13, (DEPTH, D_RNN), 0.02),
        "lru_wa": n(14, (DEPTH, 2, N_RNN_BLOCKS, RNN_BLOCK, RNN_BLOCK), RNN_BLOCK ** -0.5),
        "lru_ba": n(15, (DEPTH, 2, D_RNN), 0.02),
        "lru_wi": n(16, (DEPTH, 2, N_RNN_BLOCKS, RNN_BLOCK, RNN_BLOCK), RNN_BLOCK ** -0.5),
        "lru_bi": n(17, (DEPTH, 2, D_RNN), 0.02),
        "lru_lambda": jnp.log(u / (1.0 - u)),
        "q_norm": 1.0 + n(18, (DEPTH, HEAD_DIM), 0.02),
        "k_norm": 1.0 + n(19, (DEPTH, HEAD_DIM), 0.02),
        "w_rnn_out": n(20, (DEPTH, D_RNN, D_MODEL), D_RNN ** -0.5),
        "w_attn_out": n(21, (DEPTH, Q_DIM, D_MODEL), Q_DIM ** -0.5),
        "w_out": n(22, (DEPTH, D_MODEL, D_MODEL), D_MODEL ** -0.5),
        "ff_gate": n(23, (N_DENSE, D_MODEL, D_FF), D_MODEL ** -0.5),
        "ff_up": n(24, (N_DENSE, D_MODEL, D_FF), D_MODEL ** -0.5),
        "ff_down": n(25, (N_DENSE, D_FF, D_MODEL), D_FF ** -0.5),
        "router_w": n(26, (N_MOE, D_MODEL, N_EXPERTS), D_MODEL ** -0.5),
        "router_b": n(27, (N_MOE, N_EXPERTS), 0.01),
        "exp_gate": n(28, (N_MOE, N_EXPERTS, D_MODEL, D_EXPERT), D_MODEL ** -0.5),
        "exp_up": n(29, (N_MOE, N_EXPERTS, D_MODEL, D_EXPERT), D_MODEL ** -0.5),
        "exp_down": n(30, (N_MOE, N_EXPERTS, D_EXPERT, D_MODEL), D_EXPERT ** -0.5),
        "final_norm": 1.0 + n(31, (D_MODEL,), 0.02),
    }


def reference(x_prompt, x_sample, cache_k, cache_v, state_rnn, c, c_ctx, w_mod, b_mod, norm1, norm2,
              w_in, conv_w, conv_b, lru_wa, lru_ba, lru_wi, lru_bi, lru_lambda, q_norm, k_norm,
              w_rnn_out, w_attn_out, w_out, ff_gate, ff_up, ff_down, router_w, router_b,
              exp_gate, exp_up, exp_down, final_norm):
    W = dict(w_mod=w_mod, b_mod=b_mod, norm1=norm1, norm2=norm2, w_in=w_in, conv_w=conv_w,
             conv_b=conv_b, lru_wa=lru_wa, lru_ba=lru_ba, lru_wi=lru_wi, lru_bi=lru_bi,
             lru_lambda=lru_lambda, q_norm=q_norm, k_norm=k_norm, w_rnn_out=w_rnn_out,
             w_attn_out=w_attn_out, w_out=w_out, ff_gate=ff_gate, ff_up=ff_up, ff_down=ff_down,
             router_w=router_w, router_b=router_b, exp_gate=exp_gate, exp_up=exp_up,
             exp_down=exp_down)

    x = x_prompt
    zero_state = jnp.zeros((x_prompt.shape[0], D_RNN), F32)
    ctx_cond = c_ctx[None, :]
    ks, vs, ss = [], [], []
    for l in range(DEPTH):
        x, k, v, hf, hb = layer(x, ctx_cond, W, l, None, zero_state, zero_state, None)
        ks.append(k)
        vs.append(v)
        ss.append(jnp.stack([hf[:, -1], hb[:, 0]], axis=1))
    y_prompt = rms_norm(x, final_norm)
    new_cache_k = jnp.stack(ks, axis=1)
    new_cache_v = jnp.stack(vs, axis=1)
    new_state_rnn = jnp.stack(ss, axis=1)

    rope = rope_angles(x_sample.shape[1])
    x = x_sample
    for l in range(DEPTH):
        x, _, _, _, _ = layer(x, c, W, l, rope, state_rnn[:, l, 0], state_rnn[:, l, 1],
                              (cache_k[:, l], cache_v[:, l]))
    y_sample = rms_norm(x, final_norm)

    return (y_prompt, y_sample, new_cache_k, new_cache_v, new_state_rnn)
```

```python
import functools

import jax
import jax.numpy as jnp
from jax import lax
from jax.experimental import pallas as pl
from jax.experimental.pallas import tpu as pltpu

F32 = jnp.float32
BF16 = jnp.bfloat16

D_MODEL = 1024
D_RNN = 1024
N_RNN_BLOCKS = 16
RNN_BLOCK = D_RNN // N_RNN_BLOCKS
CONV_W = 4
CONV_LEFT = 2
LRU_C = 8.0
N_HEADS = 8
N_KV_HEADS = 2
HEAD_DIM = 128
AXIS_DIM = HEAD_DIM // 2
ROPE_THETA = 10000.0
GRID_W = 64
Q_DIM = N_HEADS * HEAD_DIM
KV_DIM = N_KV_HEADS * HEAD_DIM
N_EXPERTS = 8
EPS = 1e-6
D_IN = 2 * D_RNN + Q_DIM + 2 * KV_DIM + 2 * D_MODEL
O_XR, O_YR, O_Q = 0, D_RNN, 2 * D_RNN
O_K = O_Q + Q_DIM
O_V = O_K + KV_DIM
O_GR = O_V + KV_DIM
O_GA = O_GR + D_MODEL

LANES = 128
SUBLANES = 8
MOD_ROWS = 8
VMEM_LIMIT = 56 * 1024 * 1024

TM = 256
RNN_CHUNK = 256
TQ = 256
TM_FF = 512


def _cparams(sem):
    return pltpu.CompilerParams(dimension_semantics=sem, vmem_limit_bytes=VMEM_LIMIT)


def _sigmoid(x):
    return 1.0 / (1.0 + jnp.exp(-x))


def _silu(x):
    return x * _sigmoid(x)


def _gelu_tanh(x):
    return 0.5 * x * (1.0 + jnp.tanh(0.7978845608028654 * (x + 0.044715 * (x * x * x))))


def _rms(x, g):
    ms = jnp.mean(x * x, axis=-1, keepdims=True)
    return x * lax.rsqrt(ms + EPS) * g


def _dot(a, b):
    return jnp.dot(a, b, preferred_element_type=F32)


def _mod_kernel(cond_ref, w_ref, b_ref, o_ref):
    s = _silu(cond_ref[...]).astype(BF16)
    o_ref[...] = _dot(s, w_ref[...].astype(BF16)) + b_ref[...]


def _modulation(cond8, w_mod, b_mod):
    depth = w_mod.shape[0]
    tn = 1536
    return pl.pallas_call(
        _mod_kernel,
        grid=(depth, 6 * D_MODEL // tn),
        in_specs=[
            pl.BlockSpec((MOD_ROWS, D_MODEL), lambda l, j: (0, 0)),
            pl.BlockSpec((None, D_MODEL, tn), lambda l, j: (l, 0, j)),
            pl.BlockSpec((None, 1, tn), lambda l, j: (l, 0, j)),
        ],
        out_specs=pl.BlockSpec((None, MOD_ROWS, tn), lambda l, j: (l, 0, j)),
        out_shape=jax.ShapeDtypeStruct((depth, MOD_ROWS, 6 * D_MODEL), F32),
        compiler_params=_cparams(("arbitrary", "arbitrary")),
        name="adaln_mod",
    )(cond8, w_mod, b_mod.reshape(depth, 1, 6 * D_MODEL))


def _swap_halves(x):
    lane = lax.broadcasted_iota(jnp.int32, x.shape, 1)
    return jnp.where((lane % AXIS_DIM) < AXIS_DIM // 2,
                     pltpu.roll(x, LANES - AXIS_DIM // 2, axis=1),
                     pltpu.roll(x, AXIS_DIM // 2, axis=1))


def _in_proj_kernel(*refs, rope):
    if rope:
        (x_ref, mod_ref, n1_ref, w_ref, qn_ref, kn_ref, c_ref, s_ref,
         xr_ref, yr_ref, q_ref, kno_ref, kro_ref, v_ref, gr_ref, ga_ref) = refs
    else:
        (x_ref, mod_ref, n1_ref, w_ref, qn_ref, kn_ref,
         xr_ref, yr_ref, q_ref, kno_ref, kro_ref, v_ref, gr_ref, ga_ref) = refs
    x = x_ref[...]
    h = _rms(x, n1_ref[...]) * (1.0 + mod_ref[1:2, :]) + mod_ref[0:1, :]
    hb = h.astype(BF16)

    xr_ref[...] = _dot(hb, w_ref[:, O_XR:O_XR + D_RNN])
    yr_ref[...] = _dot(hb, w_ref[:, O_YR:O_YR + D_RNN])
    gr_ref[...] = _dot(hb, w_ref[:, O_GR:O_GR + D_MODEL])
    ga_ref[...] = _dot(hb, w_ref[:, O_GA:O_GA + D_MODEL])
    v_ref[...] = _dot(hb, w_ref[:, O_V:O_V + KV_DIM])

    if rope:
        cs, sn = c_ref[...], s_ref[...]
    zq = _dot(hb, w_ref[:, O_Q:O_Q + Q_DIM])
    for hd in range(N_HEADS):
        t = _rms(zq[:, hd * HEAD_DIM:(hd + 1) * HEAD_DIM], qn_ref[...])
        if rope:
            t = t * cs + _swap_halves(t) * sn
        q_ref[:, hd * HEAD_DIM:(hd + 1) * HEAD_DIM] = t.astype(BF16)
    zk = _dot(hb, w_ref[:, O_K:O_K + KV_DIM])
    for hd in range(N_KV_HEADS):
        t = _rms(zk[:, hd * HEAD_DIM:(hd + 1) * HEAD_DIM], kn_ref[...])
        kno_ref[:, hd * HEAD_DIM:(hd + 1) * HEAD_DIM] = t
        if rope:
            t = t * cs + _swap_halves(t) * sn
        kro_ref[:, hd * HEAD_DIM:(hd + 1) * HEAD_DIM] = t.astype(BF16)


def _in_proj(x, mods, l, norm1, w_in_bf, q_norm, k_norm, rope_tabs, seq_len):
    n = x.shape[0]
    tiles_per_seq = seq_len // TM
    rope = rope_tabs is not None
    if rope:
        mod_idx = lambda i: (l, 1 + i // tiles_per_seq, 0, 0)
    else:
        mod_idx = lambda i: (l, 0, 0, 0)
    row = lambda i: (i, 0)
    in_specs = [
        pl.BlockSpec((TM, D_MODEL), row),
        pl.BlockSpec((None, None, 6, D_MODEL), mod_idx),
        pl.BlockSpec((None, 1, D_MODEL), lambda i: (l, 0, 0)),
        pl.BlockSpec((None, D_MODEL, D_IN), lambda i: (l, 0, 0)),
        pl.BlockSpec((None, 1, HEAD_DIM), lambda i: (l, 0, 0)),
        pl.BlockSpec((None, 1, HEAD_DIM), lambda i: (l, 0, 0)),
    ]
    args = [x, mods, norm1, w_in_bf, q_norm, k_norm]
    if rope:
        in_specs += [pl.BlockSpec((TM, HEAD_DIM), lambda i: (i % tiles_per_seq, 0))] * 2
        args += list(rope_tabs)
    widths = [(D_RNN, F32), (D_RNN, F32), (Q_DIM, BF16), (KV_DIM, F32), (KV_DIM, BF16),
              (KV_DIM, F32), (D_MODEL, F32), (D_MODEL, F32)]
    return pl.pallas_call(
        functools.partial(_in_proj_kernel, rope=rope),
        grid=(n // TM,),
        in_specs=in_specs,
        out_specs=[pl.BlockSpec((TM, w), row) for w, _ in widths],
        out_shape=[jax.ShapeDtypeStruct((n, w), dt) for w, dt in widths],
        compiler_params=_cparams(("arbitrary",)),
        name="in_proj_lat" if rope else "in_proj_ctx",
    )(*args)


def _rnn_kernel(*refs, reverse, chunks_per_seq, n_chunks, zero_init):
    refs = list(refs)
    x_ref, xp_ref, xn_ref, cw_ref, cb_ref, gw_ref, ba_ref, bi_ref, lam_ref = refs[:9]
    refs = refs[9:]
    h0_ref = None if zero_init else refs.pop(0)
    if reverse:
        h_out_ref, carry_out_ref = refs[:2]
        refs = refs[2:]
    else:
        yr_ref, hb_ref = refs[:2]
        o_ref, carry_out_ref = refs[2:4]
        refs = refs[4:]
    ext_ref, a_ref, b_ref, carry_ref = refs

    i = pl.program_id(0)
    blk = (n_chunks - 1 - i) if reverse else i
    pos = blk % chunks_per_seq
    seq_first = pos == 0
    seq_last = pos == chunks_per_seq - 1
    scan_start = seq_last if reverse else seq_first

    halo = (SUBLANES, D_RNN)
    ext_ref[0:SUBLANES, :] = jnp.where(seq_first, jnp.zeros(halo, F32), xp_ref[...])
    ext_ref[SUBLANES:SUBLANES + RNN_CHUNK, :] = x_ref[...]
    ext_ref[SUBLANES + RNN_CHUNK:, :] = jnp.where(seq_last, jnp.zeros(halo, F32), xn_ref[...])

    @pl.when(scan_start)
    def _():
        if zero_init:
            carry_ref[...] = jnp.zeros_like(carry_ref)
        else:
            carry_ref[...] = jnp.broadcast_to(h0_ref[...], carry_ref.shape)

    lam = lam_ref[...]
    neg_c_softplus = -LRU_C * (jnp.maximum(-lam, 0.0) + jnp.log(1.0 + jnp.exp(-jnp.abs(lam))))
    for g in range(D_RNN // LANES):
        sl = slice(g * LANES, (g + 1) * LANES)
        xc = cb_ref[:, sl]
        for k in range(CONV_W):
            off = SUBLANES - CONV_LEFT + k
            xc = xc + ext_ref[off:off + RNN_CHUNK, sl] * cw_ref[k:k + 1, sl]
        pre = _dot(xc.astype(BF16), gw_ref[g])
        r = _sigmoid(pre[:, :LANES] + ba_ref[:, sl])
        gi = _sigmoid(pre[:, LANES:] + bi_ref[:, sl])
        log_a = r * neg_c_softplus[:, sl]
        a_ref[:, sl] = jnp.exp(log_a)
        b_ref[:, sl] = jnp.sqrt(1.0 - jnp.exp(2.0 * log_a)) * (gi * xc)

    row = lax.broadcasted_iota(jnp.int32, (SUBLANES, D_RNN), 0)
    n_tiles = RNN_CHUNK // SUBLANES

    def tile_step(t, carry):
        tt = (n_tiles - 1 - t) if reverse else t
        rows = pl.ds(pl.multiple_of(tt * SUBLANES, SUBLANES), SUBLANES)
        a = a_ref[rows, :]
        b = b_ref[rows, :]
        for d in (1, 2, 4):
            if reverse:
                keep = row < SUBLANES - d
                shift = SUBLANES - d
            else:
                keep = row >= d
                shift = d
            a_sh = pltpu.roll(a, shift, axis=0)
            b_sh = pltpu.roll(b, shift, axis=0)
            b = jnp.where(keep, a * b_sh + b, b)
            a = jnp.where(keep, a * a_sh, a)
        h = a * carry + b
        b_ref[rows, :] = h
        last = h[0:1, :] if reverse else h[SUBLANES - 1:SUBLANES, :]
        return jnp.broadcast_to(last, (SUBLANES, D_RNN))

    carry = lax.fori_loop(0, n_tiles, tile_step, carry_ref[...])
    carry_ref[...] = carry
    carry_out_ref[...] = carry[0:1, :]

    if reverse:
        h_out_ref[...] = b_ref[...]
    else:
        o_ref[...] = ((b_ref[...] + hb_ref[...]) * _gelu_tanh(yr_ref[...])).astype(BF16)


def _rnn_pass(xr, yr, hb, l, direction, conv_w, conv_b, gate_w, lru_ba, lru_bi, lru_lambda,
              state5, seq_len):
    n = xr.shape[0]
    reverse = direction == 1
    n_chunks = n // RNN_CHUNK
    cps = seq_len // RNN_CHUNK
    zero_init = state5 is None
    tiles_per_chunk = RNN_CHUNK // SUBLANES
    n_tiles = n // SUBLANES

    def blk(i):
        return (n_chunks - 1 - i) if reverse else i

    main = lambda i: (blk(i), 0)
    prev = lambda i: (jnp.maximum(blk(i) * tiles_per_chunk - 1, 0), 0)
    nxt = lambda i: (jnp.minimum((blk(i) + 1) * tiles_per_chunk, n_tiles - 1), 0)
    vec = lambda i: (l, direction, 0, 0)
    in_specs = [
        pl.BlockSpec((RNN_CHUNK, D_RNN), main),
        pl.BlockSpec((SUBLANES, D_RNN), prev),
        pl.BlockSpec((SUBLANES, D_RNN), nxt),
        pl.BlockSpec((None, CONV_W, D_RNN), lambda i: (l, 0, 0)),
        pl.BlockSpec((None, 1, D_RNN), lambda i: (l, 0, 0)),
        pl.BlockSpec((None, None, D_RNN // LANES, LANES, 2 * LANES), lambda i: (l, direction, 0, 0, 0)),
        pl.BlockSpec((None, None, 1, D_RNN), vec),
        pl.BlockSpec((None, None, 1, D_RNN), vec),
        pl.BlockSpec((None, None, 1, D_RNN), vec),
    ]
    args = [xr, xr, xr, conv_w, conv_b, gate_w, lru_ba, lru_bi, lru_lambda]
    if not zero_init:
        in_specs.append(pl.BlockSpec((None, None, None, 1, D_RNN),
                                     lambda i: (blk(i) // cps, l, direction, 0, 0)))
        args.append(state5)
    carry_spec = pl.BlockSpec((None, 1, D_RNN), lambda i: (blk(i), 0, 0))
    carry_shape = jax.ShapeDtypeStruct((n_chunks, 1, D_RNN), F32)
    if reverse:
        out_specs = [pl.BlockSpec((RNN_CHUNK, D_RNN), main), carry_spec]
        out_shape = [jax.ShapeDtypeStruct((n, D_RNN), F32), carry_shape]
    else:
        in_specs += [pl.BlockSpec((RNN_CHUNK, D_RNN), main)] * 2
        args += [yr, hb]
        out_specs = [pl.BlockSpec((RNN_CHUNK, D_RNN), main), carry_spec]
        out_shape = [jax.ShapeDtypeStruct((n, D_RNN), BF16), carry_shape]
    return pl.pallas_call(
        functools.partial(_rnn_kernel, reverse=reverse, chunks_per_seq=cps, n_chunks=n_chunks,
                          zero_init=zero_init),
        grid=(n_chunks,),
        in_specs=in_specs,
        out_specs=out_specs,
        out_shape=out_shape,
        scratch_shapes=[
            pltpu.VMEM((RNN_CHUNK + 2 * SUBLANES, D_RNN), F32),
            pltpu.VMEM((RNN_CHUNK, D_RNN), F32),
            pltpu.VMEM((RNN_CHUNK, D_RNN), F32),
            pltpu.VMEM((SUBLANES, D_RNN), F32),
        ],
        compiler_params=_cparams(("arbitrary",)),
        name=("rnn_bwd" if reverse else "rnn_fwd") + ("_ctx" if zero_init else "_lat"),
    )(*args)


def _attend_heads(q_ref, k_all, v_all, o_ref):
    scale = 1.0 / (HEAD_DIM ** 0.5)
    group = N_HEADS // N_KV_HEADS
    for kh in range(N_KV_HEADS):
        k = k_all[:, kh * HEAD_DIM:(kh + 1) * HEAD_DIM]
        v = v_all[:, kh * HEAD_DIM:(kh + 1) * HEAD_DIM]
        for g in range(group):
            sl = slice((kh * group + g) * HEAD_DIM, (kh * group + g + 1) * HEAD_DIM)
            s = lax.dot_general(q_ref[:, sl], k, (((1,), (1,)), ((), ())),
                                preferred_element_type=F32) * scale
            m = jnp.max(s, axis=-1, keepdims=True)
            p = jnp.exp(s - m)
            denom = jnp.sum(p, axis=-1, keepdims=True)
            o = _dot(p.astype(BF16), v)
            o_ref[:, sl] = (o / denom).astype(BF16)


def _attn_ctx_kernel(q_ref, k_ref, v_ref, o_ref):
    _attend_heads(q_ref, k_ref[...], v_ref[...].astype(BF16), o_ref)


def _attn_ctx(q, kr, v, seq_len):
    n = q.shape[0]
    row = lambda b: (b, 0)
    return pl.pallas_call(
        _attn_ctx_kernel,
        grid=(n // seq_len,),
        in_specs=[pl.BlockSpec((seq_len, Q_DIM), row),
                  pl.BlockSpec((seq_len, KV_DIM), row),
                  pl.BlockSpec((seq_len, KV_DIM), row)],
        out_specs=pl.BlockSpec((seq_len, Q_DIM), row),
        out_shape=jax.ShapeDtypeStruct((n, Q_DIM), BF16),
        compiler_params=_cparams(("arbitrary",)),
        name="attn_ctx",
    )(q, kr, v)


def _attn_lat_kernel(q_ref, kc_ref, vc_ref, kl_ref, vl_ref, o_ref, k_all, v_all, *, past):
    @pl.when(pl.program_id(1) == 0)
    def _():
        k_all[0:past, :] = kc_ref[...].astype(BF16)
        k_all[past:, :] = kl_ref[...]
        v_all[0:past, :] = vc_ref[...].astype(BF16)
        v_all[past:, :] = vl_ref[...].astype(BF16)

    _attend_heads(q_ref, k_all[...], v_all[...], o_ref)


def _attn_lat(q, kr, v, cache_k4, cache_v4, l, seq_len):
    n = q.shape[0]
    past = cache_k4.shape[2]
    nq = seq_len // TQ
    return pl.pallas_call(
        functools.partial(_attn_lat_kernel, past=past),
        grid=(n // seq_len, nq),
        in_specs=[pl.BlockSpec((TQ, Q_DIM), lambda b, j: (b * nq + j, 0)),
                  pl.BlockSpec((None, None, past, KV_DIM), lambda b, j: (b, l, 0, 0)),
                  pl.BlockSpec((None, None, past, KV_DIM), lambda b, j: (b, l, 0, 0)),
                  pl.BlockSpec((seq_len, KV_DIM), lambda b, j: (b, 0)),
                  pl.BlockSpec((seq_len, KV_DIM), lambda b, j: (b, 0))],
        out_specs=pl.BlockSpec((TQ, Q_DIM), lambda b, j: (b * nq + j, 0)),
        out_shape=jax.ShapeDtypeStruct((n, Q_DIM), BF16),
        scratch_shapes=[pltpu.VMEM((past + seq_len, KV_DIM), BF16),
                        pltpu.VMEM((past + seq_len, KV_DIM), BF16)],
        compiler_params=_cparams(("arbitrary", "arbitrary")),
        name="attn_lat",
    )(q, cache_k4, cache_v4, kr, v)


def _mix_out_kernel(x_ref, r_ref, a_ref, gr_ref, ga_ref, mod_ref, n2_ref, wr_ref, wa_ref, wo_ref,
                    x1_ref, h2_ref):
    merged = (_sigmoid(gr_ref[...]) * _dot(r_ref[...], wr_ref[...])
              + _sigmoid(ga_ref[...]) * _dot(a_ref[...], wa_ref[...]))
    mix = _dot(merged.astype(BF16), wo_ref[...])
    x1 = x_ref[...] + mod_ref[2:3, :] * mix
    x1_ref[...] = x1
    h2 = _rms(x1, n2_ref[...]) * (1.0 + mod_ref[4:5, :]) + mod_ref[3:4, :]
    h2_ref[...] = h2.astype(h2_ref.dtype)


def _mix_out(x, out_r, out_a, gr, ga, mods, l, norm2, wr_bf, wa_bf, wo_bf, seq_len, lat, h2_dtype):
    n = x.shape[0]
    tiles_per_seq = seq_len // TM
    if lat:
        mod_idx = lambda i: (l, 1 + i // tiles_per_seq, 0, 0)
    else:
        mod_idx = lambda i: (l, 0, 0, 0)
    row = lambda i: (i, 0)
    wspec = pl.BlockSpec((None, D_MODEL, D_MODEL), lambda i: (l, 0, 0))
    return pl.pallas_call(
        _mix_out_kernel,
        grid=(n // TM,),
        in_specs=[pl.BlockSpec((TM, D_MODEL), row)] * 5 + [
            pl.BlockSpec((None, None, 6, D_MODEL), mod_idx),
            pl.BlockSpec((None, 1, D_MODEL), lambda i: (l, 0, 0)),
            wspec, wspec, wspec],
        out_specs=[pl.BlockSpec((TM, D_MODEL), row)] * 2,
        out_shape=[jax.ShapeDtypeStruct((n, D_MODEL), F32),
                   jax.ShapeDtypeStruct((n, D_MODEL), h2_dtype)],
        compiler_params=_cparams(("arbitrary",)),
        name="mix_out_lat" if lat else "mix_out_ctx",
    )(x, out_r, out_a, gr, ga, mods, norm2, wr_bf, wa_bf, wo_bf)


def _ffn_kernel(x_ref, h_ref, mod_ref, wg_ref, wu_ref, wd_ref, o_ref, *, f_chunk):
    h = h_ref[...]
    d_ff = wg_ref.shape[1]
    acc = jnp.zeros((h.shape[0], D_MODEL), F32)
    for c in range(d_ff // f_chunk):
        sl = slice(c * f_chunk, (c + 1) * f_chunk)
        act = _silu(_dot(h, wg_ref[:, sl])) * _dot(h, wu_ref[:, sl])
        acc = acc + _dot(act.astype(BF16), wd_ref[sl, :])
    o_ref[...] = x_ref[...] + mod_ref[5:6, :] * acc


def _ffn(x1, h2, mods, l, j, wg_bf, wu_bf, wd_bf, seq_len, lat):
    n = x1.shape[0]
    d_ff = wg_bf.shape[2]
    tiles_per_seq = seq_len // TM
    if lat:
        mod_idx = lambda i: (l, 1 + i // tiles_per_seq, 0, 0)
    else:
        mod_idx = lambda i: (l, 0, 0, 0)
    row = lambda i: (i, 0)
    return pl.pallas_call(
        functools.partial(_ffn_kernel, f_chunk=d_ff // 2),
        grid=(n // TM,),
        in_specs=[pl.BlockSpec((TM, D_MODEL), row), pl.BlockSpec((TM, D_MODEL), row),
                  pl.BlockSpec((None, None, 6, D_MODEL), mod_idx),
                  pl.BlockSpec((None, D_MODEL, d_ff), lambda i: (j, 0, 0)),
                  pl.BlockSpec((None, D_MODEL, d_ff), lambda i: (j, 0, 0)),
                  pl.BlockSpec((None, d_ff, D_MODEL), lambda i: (j, 0, 0))],
        out_specs=pl.BlockSpec((TM, D_MODEL), row),
        out_shape=jax.ShapeDtypeStruct((n, D_MODEL), F32),
        compiler_params=_cparams(("arbitrary",)),
        name="ffn_lat" if lat else "ffn_ctx",
    )(x1, h2, mods, wg_bf, wu_bf, wd_bf)


def _moe_kernel(x_ref, h_ref, mod_ref, wr_ref, br_ref, wg_ref, wu_ref, wd_ref, o_ref,
                hb_ref, gate_ref, acc_ref):
    e = pl.program_id(1)
    lane = lax.broadcasted_iota(jnp.int32, gate_ref.shape, 1)

    @pl.when(e == 0)
    def _():
        h = h_ref[...]
        hb_ref[...] = h.astype(BF16)
        logits = jnp.dot(h, wr_ref[...], preferred_element_type=F32,
                         precision=lax.Precision.HIGHEST) + br_ref[...]
        neg = jnp.float32(-jnp.inf)
        logits = jnp.where(lane < N_EXPERTS, logits, neg)
        v1 = jnp.max(logits, axis=-1, keepdims=True)
        i1 = jnp.min(jnp.where(logits == v1, lane, LANES), axis=-1, keepdims=True)
        rest = jnp.where(lane == i1, neg, logits)
        v2 = jnp.max(rest, axis=-1, keepdims=True)
        i2 = jnp.min(jnp.where(rest == v2, lane, LANES), axis=-1, keepdims=True)
        e2 = jnp.exp(v2 - v1)
        w1 = 1.0 / (1.0 + e2)
        w2 = e2 / (1.0 + e2)
        gate_ref[...] = jnp.where(lane == i1, w1, 0.0) + jnp.where(lane == i2, w2, 0.0)
        acc_ref[...] = jnp.zeros_like(acc_ref)

    gate = jnp.sum(jnp.where(lane == e, gate_ref[...], 0.0), axis=-1, keepdims=True)
    hb = hb_ref[...]
    act = _silu(_dot(hb, wg_ref[...])) * _dot(hb, wu_ref[...]) * gate
    acc_ref[...] += _dot(act.astype(BF16), wd_ref[...])

    @pl.when(e == N_EXPERTS - 1)
    def _():
        o_ref[...] = x_ref[...] + mod_ref[5:6, :] * acc_ref[...]


def _moe(x1, h2, mods, l, j, wr_pad, br_pad, wg_bf, wu_bf, wd_bf, seq_len, lat):
    n = x1.shape[0]
    d_e = wg_bf.shape[3]
    tm = TM_FF
    tiles_per_seq = seq_len // tm if seq_len >= tm else 1
    if lat:
        mod_idx = lambda i, e: (l, 1 + i // tiles_per_seq, 0, 0)
    else:
        mod_idx = lambda i, e: (l, 0, 0, 0)
    row = lambda i, e: (i, 0)
    return pl.pallas_call(
        _moe_kernel,
        grid=(n // tm, N_EXPERTS),
        in_specs=[pl.BlockSpec((tm, D_MODEL), row), pl.BlockSpec((tm, D_MODEL), row),
                  pl.BlockSpec((None, None, 6, D_MODEL), mod_idx),
                  pl.BlockSpec((None, D_MODEL, LANES), lambda i, e: (j, 0, 0)),
                  pl.BlockSpec((None, 1, LANES), lambda i, e: (j, 0, 0)),
                  pl.BlockSpec((None, None, D_MODEL, d_e), lambda i, e: (j, e, 0, 0)),
                  pl.BlockSpec((None, None, D_MODEL, d_e), lambda i, e: (j, e, 0, 0)),
                  pl.BlockSpec((None, None, d_e, D_MODEL), lambda i, e: (j, e, 0, 0))],
        out_specs=pl.BlockSpec((tm, D_MODEL), row),
        out_shape=jax.ShapeDtypeStruct((n, D_MODEL), F32),
        scratch_shapes=[pltpu.VMEM((tm, D_MODEL), BF16),
                        pltpu.VMEM((tm, LANES), F32),
                        pltpu.VMEM((tm, D_MODEL), F32)],
        compiler_params=_cparams(("arbitrary", "arbitrary")),
        name="moe_lat" if lat else "moe_ctx",
    )(x1, h2, mods, wr_pad, br_pad, wg_bf, wu_bf, wd_bf)


def _final_norm_kernel(x_ref, g_ref, o_ref):
    o_ref[...] = _rms(x_ref[...], g_ref[...])


def _final_norm(x, g):
    n = x.shape[0]
    row = lambda i: (i, 0)
    return pl.pallas_call(
        _final_norm_kernel,
        grid=(n // TM_FF,),
        in_specs=[pl.BlockSpec((TM_FF, D_MODEL), row), pl.BlockSpec((1, D_MODEL), lambda i: (0, 0))],
        out_specs=pl.BlockSpec((TM_FF, D_MODEL), row),
        out_shape=jax.ShapeDtypeStruct((n, D_MODEL), F32),
        compiler_params=_cparams(("arbitrary",)),
        name="final_norm",
    )(x, g)


def _rope_tables(t_len):
    t = jnp.arange(t_len)
    pos = jnp.stack([t // GRID_W, t % GRID_W], axis=-1).astype(F32)
    inv = ROPE_THETA ** (-jnp.arange(0, AXIS_DIM, 2, dtype=F32) / AXIS_DIM)
    ang = pos[..., None] * inv
    cos, sin = jnp.cos(ang), jnp.sin(ang)
    c = jnp.concatenate([cos[:, 0], cos[:, 0], cos[:, 1], cos[:, 1]], axis=-1)
    s = jnp.concatenate([-sin[:, 0], sin[:, 0], -sin[:, 1], sin[:, 1]], axis=-1)
    return c, s


def _gate_weights(wa, wi):
    depth = wa.shape[0]
    groups = D_RNN // LANES

    def pair(w):
        w = w.reshape(depth, 2, groups, 2, RNN_BLOCK, RNN_BLOCK)
        z = jnp.zeros_like(w[:, :, :, 0])
        top = jnp.concatenate([w[:, :, :, 0], z], axis=-1)
        bot = jnp.concatenate([z, w[:, :, :, 1]], axis=-1)
        return jnp.concatenate([top, bot], axis=-2)

    return jnp.concatenate([pair(wa), pair(wi)], axis=-1).astype(BF16)


def kernel(x_prompt, x_sample, cache_k, cache_v, state_rnn, c, c_ctx, w_mod, b_mod, norm1, norm2, w_in, conv_w, conv_b, lru_wa, lru_ba, lru_wi, lru_bi, lru_lambda, q_norm, k_norm, w_rnn_out, w_attn_out, w_out, ff_gate, ff_up, ff_down, router_w, router_b, exp_gate, exp_up, exp_down, final_norm):
    batch, seq, _ = x_prompt.shape
    dec_batch, dec_seq, _ = x_sample.shape
    depth = w_mod.shape[0]
    past = cache_k.shape[2]
    assert dec_batch + 1 <= MOD_ROWS

    cond8 = jnp.zeros((MOD_ROWS, D_MODEL), F32).at[0].set(c_ctx).at[1:1 + dec_batch].set(c)
    mods = _modulation(cond8, w_mod, b_mod).reshape(depth, MOD_ROWS, 6, D_MODEL)

    bf = lambda w: w.astype(BF16)
    w_in_bf, wr_bf, wa_bf, wo_bf = bf(w_in), bf(w_rnn_out), bf(w_attn_out), bf(w_out)
    ffg_bf, ffu_bf, ffd_bf = bf(ff_gate), bf(ff_up), bf(ff_down)
    eg_bf, eu_bf, ed_bf = bf(exp_gate), bf(exp_up), bf(exp_down)
    gate_w = _gate_weights(lru_wa, lru_wi)
    n_moe = router_w.shape[0]
    wr_pad = jnp.zeros((n_moe, D_MODEL, LANES), F32).at[:, :, :N_EXPERTS].set(router_w)
    br_pad = jnp.zeros((n_moe, 1, LANES), F32).at[:, 0, :N_EXPERTS].set(router_b)

    norm1_3 = norm1.reshape(depth, 1, D_MODEL)
    norm2_3 = norm2.reshape(depth, 1, D_MODEL)
    qn3 = q_norm.reshape(depth, 1, HEAD_DIM)
    kn3 = k_norm.reshape(depth, 1, HEAD_DIM)
    conv_b3 = conv_b.reshape(depth, 1, D_RNN)
    ba4 = lru_ba.reshape(depth, 2, 1, D_RNN)
    bi4 = lru_bi.reshape(depth, 2, 1, D_RNN)
    lam4 = lru_lambda.reshape(depth, 2, 1, D_RNN)
    state5 = state_rnn.reshape(dec_batch, depth, 2, 1, D_RNN)
    cache_k4 = cache_k.reshape(dec_batch, depth, past, KV_DIM)
    cache_v4 = cache_v.reshape(dec_batch, depth, past, KV_DIM)
    rope_tabs = _rope_tables(dec_seq)

    def run_stream(x, lat):
        seq_len = dec_seq if lat else seq
        ks, vs, ss = [], [], []
        for l in range(depth):
            xr, yr, q, kn, kr, v, gr, ga = _in_proj(
                x, mods, l, norm1_3, w_in_bf, qn3, kn3, rope_tabs if lat else None, seq_len)
            rnn_args = (conv_w, conv_b3, gate_w, ba4, bi4, lam4, state5 if lat else None, seq_len)
            hb, carry_b = _rnn_pass(xr, None, None, l, 1, *rnn_args)
            out_r, carry_f = _rnn_pass(xr, yr, hb, l, 0, *rnn_args)
            if lat:
                out_a = _attn_lat(q, kr, v, cache_k4, cache_v4, l, seq_len)
            else:
                out_a = _attn_ctx(q, kr, v, seq_len)
                ks.append(kn)
                vs.append(v)
                ss.append(jnp.concatenate([carry_f, carry_b], axis=1))
            moe_layer = l % 2 == 1
            x1, h2 = _mix_out(x, out_r, out_a, gr, ga, mods, l, norm2_3, wr_bf, wa_bf, wo_bf,
                              seq_len, lat, F32 if moe_layer else BF16)
            if moe_layer:
                x = _moe(x1, h2, mods, l, l // 2, wr_pad, br_pad, eg_bf, eu_bf, ed_bf, seq_len, lat)
            else:
                x = _ffn(x1, h2, mods, l, l // 2, ffg_bf, ffu_bf, ffd_bf, seq_len, lat)
        return _final_norm(x, final_norm.reshape(1, D_MODEL)), ks, vs, ss

    y_ctx, ks, vs, ss = run_stream(x_prompt.reshape(batch * seq, D_MODEL), False)
    y_lat, _, _, _ = run_stream(x_sample.reshape(dec_batch * dec_seq, D_MODEL), True)

    new_k = jnp.stack([k.reshape(batch, seq, N_KV_HEADS, HEAD_DIM) for k in ks], axis=1)
    new_v = jnp.stack([v.reshape(batch, seq, N_KV_HEADS, HEAD_DIM) for v in vs], axis=1)
    new_state = jnp.stack(ss, axis=1)
    return (y_ctx.reshape(batch, seq, D_MODEL), y_lat.reshape(dec_batch, dec_seq, D_MODEL),
            new_k, new_v, new_state)
```

```python
import functools

import jax
import jax.numpy as jnp
from jax import lax
from jax.experimental import pallas as pl
from jax.experimental.pallas import tpu as pltpu

F32 = jnp.float32
BF16 = jnp.bfloat16

D_MODEL = 1024
D_RNN = 1024
N_RNN_BLOCKS = 16
RNN_BLOCK = D_RNN // N_RNN_BLOCKS
CONV_W = 4
CONV_LEFT = 2
LRU_C = 8.0
N_HEADS = 8
N_KV_HEADS = 2
HEAD_DIM = 128
AXIS_DIM = HEAD_DIM // 2
ROPE_THETA = 10000.0
GRID_W = 64
Q_DIM = N_HEADS * HEAD_DIM
KV_DIM = N_KV_HEADS * HEAD_DIM
N_EXPERTS = 8
EPS = 1e-6
D_IN = 2 * D_RNN + Q_DIM + 2 * KV_DIM + 2 * D_MODEL
O_XR, O_YR, O_Q = 0, D_RNN, 2 * D_RNN
O_K = O_Q + Q_DIM
O_V = O_K + KV_DIM
O_GR = O_V + KV_DIM
O_GA = O_GR + D_MODEL

LANES = 128
SUBLANES = 8
MOD_ROWS = 8
VMEM_LIMIT = 56 * 1024 * 1024

TM = 256
RNN_CHUNK = 256
TQ = 256
TM_FF = 512
TM_GROUP = 256
DMA_CHUNK = 256
DMA_UNROLL = 8


def _cparams(sem):
    return pltpu.CompilerParams(dimension_semantics=sem, vmem_limit_bytes=VMEM_LIMIT)


def _sigmoid(x):
    return 1.0 / (1.0 + jnp.exp(-x))


def _silu(x):
    return x * _sigmoid(x)


def _gelu_tanh(x):
    return 0.5 * x * (1.0 + jnp.tanh(0.7978845608028654 * (x + 0.044715 * (x * x * x))))


def _rms(x, g):
    ms = jnp.mean(x * x, axis=-1, keepdims=True)
    return x * lax.rsqrt(ms + EPS) * g


def _dot(a, b):
    return jnp.dot(a, b, preferred_element_type=F32)


def _mod_kernel(cond_ref, w_ref, b_ref, o_ref):
    s = _silu(cond_ref[...]).astype(BF16)
    o_ref[...] = _dot(s, w_ref[...].astype(BF16)) + b_ref[...]


def _modulation(cond8, w_mod, b_mod):
    depth = w_mod.shape[0]
    tn = 1536
    return pl.pallas_call(
        _mod_kernel,
        grid=(depth, 6 * D_MODEL // tn),
        in_specs=[
            pl.BlockSpec((MOD_ROWS, D_MODEL), lambda l, j: (0, 0)),
            pl.BlockSpec((None, D_MODEL, tn), lambda l, j: (l, 0, j)),
            pl.BlockSpec((None, 1, tn), lambda l, j: (l, 0, j)),
        ],
        out_specs=pl.BlockSpec((None, MOD_ROWS, tn), lambda l, j: (l, 0, j)),
        out_shape=jax.ShapeDtypeStruct((depth, MOD_ROWS, 6 * D_MODEL), F32),
        compiler_params=_cparams(("arbitrary", "arbitrary")),
        name="adaln_mod",
    )(cond8, w_mod, b_mod.reshape(depth, 1, 6 * D_MODEL))


def _swap_halves(x):
    lane = lax.broadcasted_iota(jnp.int32, x.shape, 1)
    return jnp.where((lane % AXIS_DIM) < AXIS_DIM // 2,
                     pltpu.roll(x, LANES - AXIS_DIM // 2, axis=1),
                     pltpu.roll(x, AXIS_DIM // 2, axis=1))


def _in_proj_kernel(*refs, rope):
    if rope:
        (x_ref, mod_ref, n1_ref, w_ref, qn_ref, kn_ref, c_ref, s_ref,
         xr_ref, yr_ref, q_ref, kno_ref, kro_ref, v_ref, gr_ref, ga_ref) = refs
    else:
        (x_ref, mod_ref, n1_ref, w_ref, qn_ref, kn_ref,
         xr_ref, yr_ref, q_ref, kno_ref, kro_ref, v_ref, gr_ref, ga_ref) = refs
    x = x_ref[...]
    h = _rms(x, n1_ref[...]) * (1.0 + mod_ref[1:2, :]) + mod_ref[0:1, :]
    hb = h.astype(BF16)

    xr_ref[...] = _dot(hb, w_ref[:, O_XR:O_XR + D_RNN])
    yr_ref[...] = _dot(hb, w_ref[:, O_YR:O_YR + D_RNN])
    gr_ref[...] = _dot(hb, w_ref[:, O_GR:O_GR + D_MODEL])
    ga_ref[...] = _dot(hb, w_ref[:, O_GA:O_GA + D_MODEL])
    v_ref[...] = _dot(hb, w_ref[:, O_V:O_V + KV_DIM])

    if rope:
        cs, sn = c_ref[...], s_ref[...]
    zq = _dot(hb, w_ref[:, O_Q:O_Q + Q_DIM])
    for hd in range(N_HEADS):
        t = _rms(zq[:, hd * HEAD_DIM:(hd + 1) * HEAD_DIM], qn_ref[...])
        if rope:
            t = t * cs + _swap_halves(t) * sn
        q_ref[:, hd * HEAD_DIM:(hd + 1) * HEAD_DIM] = t.astype(BF16)
    zk = _dot(hb, w_ref[:, O_K:O_K + KV_DIM])
    for hd in range(N_KV_HEADS):
        t = _rms(zk[:, hd * HEAD_DIM:(hd + 1) * HEAD_DIM], kn_ref[...])
        kno_ref[:, hd * HEAD_DIM:(hd + 1) * HEAD_DIM] = t
        if rope:
            t = t * cs + _swap_halves(t) * sn
        kro_ref[:, hd * HEAD_DIM:(hd + 1) * HEAD_DIM] = t.astype(BF16)


def _in_proj(x, mods, l, norm1, w_in_bf, q_norm, k_norm, rope_tabs, seq_len):
    n = x.shape[0]
    tiles_per_seq = seq_len // TM
    rope = rope_tabs is not None
    if rope:
        mod_idx = lambda i: (l, 1 + i // tiles_per_seq, 0, 0)
    else:
        mod_idx = lambda i: (l, 0, 0, 0)
    row = lambda i: (i, 0)
    in_specs = [
        pl.BlockSpec((TM, D_MODEL), row),
        pl.BlockSpec((None, None, 6, D_MODEL), mod_idx),
        pl.BlockSpec((None, 1, D_MODEL), lambda i: (l, 0, 0)),
        pl.BlockSpec((None, D_MODEL, D_IN), lambda i: (l, 0, 0)),
        pl.BlockSpec((None, 1, HEAD_DIM), lambda i: (l, 0, 0)),
        pl.BlockSpec((None, 1, HEAD_DIM), lambda i: (l, 0, 0)),
    ]
    args = [x, mods, norm1, w_in_bf, q_norm, k_norm]
    if rope:
        in_specs += [pl.BlockSpec((TM, HEAD_DIM), lambda i: (i % tiles_per_seq, 0))] * 2
        args += list(rope_tabs)
    widths = [(D_RNN, F32), (D_RNN, F32), (Q_DIM, BF16), (KV_DIM, F32), (KV_DIM, BF16),
              (KV_DIM, F32), (D_MODEL, F32), (D_MODEL, F32)]
    return pl.pallas_call(
        functools.partial(_in_proj_kernel, rope=rope),
        grid=(n // TM,),
        in_specs=in_specs,
        out_specs=[pl.BlockSpec((TM, w), row) for w, _ in widths],
        out_shape=[jax.ShapeDtypeStruct((n, w), dt) for w, dt in widths],
        compiler_params=_cparams(("arbitrary",)),
        name="in_proj_lat" if rope else "in_proj_ctx",
    )(*args)


def _rnn_kernel(*refs, reverse, chunks_per_seq, n_chunks, zero_init):
    refs = list(refs)
    x_ref, xp_ref, xn_ref, cw_ref, cb_ref, gw_ref, ba_ref, bi_ref, lam_ref = refs[:9]
    refs = refs[9:]
    h0_ref = None if zero_init else refs.pop(0)
    if reverse:
        h_out_ref, carry_out_ref = refs[:2]
        refs = refs[2:]
    else:
        yr_ref, hb_ref = refs[:2]
        o_ref, carry_out_ref = refs[2:4]
        refs = refs[4:]
    ext_ref, a_ref, b_ref, carry_ref = refs

    i = pl.program_id(0)
    blk = (n_chunks - 1 - i) if reverse else i
    pos = blk % chunks_per_seq
    seq_first = pos == 0
    seq_last = pos == chunks_per_seq - 1
    scan_start = seq_last if reverse else seq_first

    halo = (SUBLANES, D_RNN)
    ext_ref[0:SUBLANES, :] = jnp.where(seq_first, jnp.zeros(halo, F32), xp_ref[...])
    ext_ref[SUBLANES:SUBLANES + RNN_CHUNK, :] = x_ref[...]
    ext_ref[SUBLANES + RNN_CHUNK:, :] = jnp.where(seq_last, jnp.zeros(halo, F32), xn_ref[...])

    @pl.when(scan_start)
    def _():
        if zero_init:
            carry_ref[...] = jnp.zeros_like(carry_ref)
        else:
            carry_ref[...] = jnp.broadcast_to(h0_ref[...], carry_ref.shape)

    lam = lam_ref[...]
    neg_c_softplus = -LRU_C * (jnp.maximum(-lam, 0.0) + jnp.log(1.0 + jnp.exp(-jnp.abs(lam))))
    for g in range(D_RNN // LANES):
        sl = slice(g * LANES, (g + 1) * LANES)
        xc = cb_ref[:, sl]
        for k in range(CONV_W):
            off = SUBLANES - CONV_LEFT + k
            xc = xc + ext_ref[off:off + RNN_CHUNK, sl] * cw_ref[k:k + 1, sl]
        pre = _dot(xc.astype(BF16), gw_ref[g])
        r = _sigmoid(pre[:, :LANES] + ba_ref[:, sl])
        gi = _sigmoid(pre[:, LANES:] + bi_ref[:, sl])
        log_a = r * neg_c_softplus[:, sl]
        a_ref[:, sl] = jnp.exp(log_a)
        b_ref[:, sl] = jnp.sqrt(1.0 - jnp.exp(2.0 * log_a)) * (gi * xc)

    row = lax.broadcasted_iota(jnp.int32, (SUBLANES, D_RNN), 0)
    n_tiles = RNN_CHUNK // SUBLANES

    def tile_step(t, carry):
        tt = (n_tiles - 1 - t) if reverse else t
        rows = pl.ds(pl.multiple_of(tt * SUBLANES, SUBLANES), SUBLANES)
        a = a_ref[rows, :]
        b = b_ref[rows, :]
        for d in (1, 2, 4):
            if reverse:
                keep = row < SUBLANES - d
                shift = SUBLANES - d
            else:
                keep = row >= d
                shift = d
            a_sh = pltpu.roll(a, shift, axis=0)
            b_sh = pltpu.roll(b, shift, axis=0)
            b = jnp.where(keep, a * b_sh + b, b)
            a = jnp.where(keep, a * a_sh, a)
        h = a * carry + b
        b_ref[rows, :] = h
        last = h[0:1, :] if reverse else h[SUBLANES - 1:SUBLANES, :]
        return jnp.broadcast_to(last, (SUBLANES, D_RNN))

    carry = lax.fori_loop(0, n_tiles, tile_step, carry_ref[...])
    carry_ref[...] = carry
    carry_out_ref[...] = carry[0:1, :]

    if reverse:
        h_out_ref[...] = b_ref[...]
    else:
        o_ref[...] = ((b_ref[...] + hb_ref[...]) * _gelu_tanh(yr_ref[...])).astype(BF16)


def _rnn_pass(xr, yr, hb, l, direction, conv_w, conv_b, gate_w, lru_ba, lru_bi, lru_lambda,
              state5, seq_len):
    n = xr.shape[0]
    reverse = direction == 1
    n_chunks = n // RNN_CHUNK
    cps = seq_len // RNN_CHUNK
    zero_init = state5 is None
    tiles_per_chunk = RNN_CHUNK // SUBLANES
    n_tiles = n // SUBLANES

    def blk(i):
        return (n_chunks - 1 - i) if reverse else i

    main = lambda i: (blk(i), 0)
    prev = lambda i: (jnp.maximum(blk(i) * tiles_per_chunk - 1, 0), 0)
    nxt = lambda i: (jnp.minimum((blk(i) + 1) * tiles_per_chunk, n_tiles - 1), 0)
    vec = lambda i: (l, direction, 0, 0)
    in_specs = [
        pl.BlockSpec((RNN_CHUNK, D_RNN), main),
        pl.BlockSpec((SUBLANES, D_RNN), prev),
        pl.BlockSpec((SUBLANES, D_RNN), nxt),
        pl.BlockSpec((None, CONV_W, D_RNN), lambda i: (l, 0, 0)),
        pl.BlockSpec((None, 1, D_RNN), lambda i: (l, 0, 0)),
        pl.BlockSpec((None, None, D_RNN // LANES, LANES, 2 * LANES), lambda i: (l, direction, 0, 0, 0)),
        pl.BlockSpec((None, None, 1, D_RNN), vec),
        pl.BlockSpec((None, None, 1, D_RNN), vec),
        pl.BlockSpec((None, None, 1, D_RNN), vec),
    ]
    args = [xr, xr, xr, conv_w, conv_b, gate_w, lru_ba, lru_bi, lru_lambda]
    if not zero_init:
        in_specs.append(pl.BlockSpec((None, None, None, 1, D_RNN),
                                     lambda i: (blk(i) // cps, l, direction, 0, 0)))
        args.append(state5)
    carry_spec = pl.BlockSpec((None, 1, D_RNN), lambda i: (blk(i), 0, 0))
    carry_shape = jax.ShapeDtypeStruct((n_chunks, 1, D_RNN), F32)
    if reverse:
        out_specs = [pl.BlockSpec((RNN_CHUNK, D_RNN), main), carry_spec]
        out_shape = [jax.ShapeDtypeStruct((n, D_RNN), F32), carry_shape]
    else:
        in_specs += [pl.BlockSpec((RNN_CHUNK, D_RNN), main)] * 2
        args += [yr, hb]
        out_specs = [pl.BlockSpec((RNN_CHUNK, D_RNN), main), carry_spec]
        out_shape = [jax.ShapeDtypeStruct((n, D_RNN), BF16), carry_shape]
    return pl.pallas_call(
        functools.partial(_rnn_kernel, reverse=reverse, chunks_per_seq=cps, n_chunks=n_chunks,
                          zero_init=zero_init),
        grid=(n_chunks,),
        in_specs=in_specs,
        out_specs=out_specs,
        out_shape=out_shape,
        scratch_shapes=[
            pltpu.VMEM((RNN_CHUNK + 2 * SUBLANES, D_RNN), F32),
            pltpu.VMEM((RNN_CHUNK, D_RNN), F32),
            pltpu.VMEM((RNN_CHUNK, D_RNN), F32),
            pltpu.VMEM((SUBLANES, D_RNN), F32),
        ],
        compiler_params=_cparams(("arbitrary",)),
        name=("rnn_bwd" if reverse else "rnn_fwd") + ("_ctx" if zero_init else "_lat"),
    )(*args)


def _attend_heads(q_ref, k_all, v_all, o_ref):
    scale = 1.0 / (HEAD_DIM ** 0.5)
    group = N_HEADS // N_KV_HEADS
    for kh in range(N_KV_HEADS):
        k = k_all[:, kh * HEAD_DIM:(kh + 1) * HEAD_DIM]
        v = v_all[:, kh * HEAD_DIM:(kh + 1) * HEAD_DIM]
        for g in range(group):
            sl = slice((kh * group + g) * HEAD_DIM, (kh * group + g + 1) * HEAD_DIM)
            s = lax.dot_general(q_ref[:, sl], k, (((1,), (1,)), ((), ())),
                                preferred_element_type=F32) * scale
            m = jnp.max(s, axis=-1, keepdims=True)
            p = jnp.exp(s - m)
            denom = jnp.sum(p, axis=-1, keepdims=True)
            o = _dot(p.astype(BF16), v)
            o_ref[:, sl] = (o / denom).astype(BF16)


def _attn_ctx_kernel(q_ref, k_ref, v_ref, o_ref):
    _attend_heads(q_ref, k_ref[...], v_ref[...].astype(BF16), o_ref)


def _attn_ctx(q, kr, v, seq_len):
    n = q.shape[0]
    row = lambda b: (b, 0)
    return pl.pallas_call(
        _attn_ctx_kernel,
        grid=(n // seq_len,),
        in_specs=[pl.BlockSpec((seq_len, Q_DIM), row),
                  pl.BlockSpec((seq_len, KV_DIM), row),
                  pl.BlockSpec((seq_len, KV_DIM), row)],
        out_specs=pl.BlockSpec((seq_len, Q_DIM), row),
        out_shape=jax.ShapeDtypeStruct((n, Q_DIM), BF16),
        compiler_params=_cparams(("arbitrary",)),
        name="attn_ctx",
    )(q, kr, v)


def _attn_lat_kernel(q_ref, kc_ref, vc_ref, kl_ref, vl_ref, o_ref, k_all, v_all, *, past):
    @pl.when(pl.program_id(1) == 0)
    def _():
        k_all[0:past, :] = kc_ref[...].astype(BF16)
        k_all[past:, :] = kl_ref[...]
        v_all[0:past, :] = vc_ref[...].astype(BF16)
        v_all[past:, :] = vl_ref[...].astype(BF16)

    _attend_heads(q_ref, k_all[...], v_all[...], o_ref)


def _attn_lat(q, kr, v, cache_k4, cache_v4, l, seq_len):
    n = q.shape[0]
    past = cache_k4.shape[2]
    nq = seq_len // TQ
    return pl.pallas_call(
        functools.partial(_attn_lat_kernel, past=past),
        grid=(n // seq_len, nq),
        in_specs=[pl.BlockSpec((TQ, Q_DIM), lambda b, j: (b * nq + j, 0)),
                  pl.BlockSpec((None, None, past, KV_DIM), lambda b, j: (b, l, 0, 0)),
                  pl.BlockSpec((None, None, past, KV_DIM), lambda b, j: (b, l, 0, 0)),
                  pl.BlockSpec((seq_len, KV_DIM), lambda b, j: (b, 0)),
                  pl.BlockSpec((seq_len, KV_DIM), lambda b, j: (b, 0))],
        out_specs=pl.BlockSpec((TQ, Q_DIM), lambda b, j: (b * nq + j, 0)),
        out_shape=jax.ShapeDtypeStruct((n, Q_DIM), BF16),
        scratch_shapes=[pltpu.VMEM((past + seq_len, KV_DIM), BF16),
                        pltpu.VMEM((past + seq_len, KV_DIM), BF16)],
        compiler_params=_cparams(("arbitrary", "arbitrary")),
        name="attn_lat",
    )(q, cache_k4, cache_v4, kr, v)


R_E1, R_E2, R_W1, R_W2, R_RANK1, R_RANK2 = range(6)


def _route(h2, wrt_ref, brt_ref, count_ref):
    shape = (h2.shape[0], LANES)
    lane = lax.broadcasted_iota(jnp.int32, shape, 1)
    neg = jnp.float32(-jnp.inf)
    logits = jnp.full(shape, neg, F32)
    for e in range(N_EXPERTS):
        prod = h2 * wrt_ref[e:e + 1, :]
        part = prod[:, 0:LANES]
        for c in range(1, D_MODEL // LANES):
            part = part + prod[:, c * LANES:(c + 1) * LANES]
        logits = jnp.where(lane == e, jnp.sum(part, axis=-1, keepdims=True), logits)
    logits = logits + brt_ref[...]
    v1 = jnp.max(logits, axis=-1, keepdims=True)
    i1 = jnp.min(jnp.where(logits == v1, lane, LANES), axis=-1, keepdims=True)
    rest = jnp.where(lane == i1, neg, logits)
    v2 = jnp.max(rest, axis=-1, keepdims=True)
    i2 = jnp.min(jnp.where(rest == v2, lane, LANES), axis=-1, keepdims=True)
    e2 = jnp.exp(v2 - v1)
    w1 = 1.0 / (1.0 + e2)
    w2 = e2 / (1.0 + e2)
    chosen = jnp.where((lane == i1) | (lane == i2), 1.0, 0.0)
    r_id = lax.broadcasted_iota(jnp.int32, (shape[0], shape[0]), 0)
    c_id = lax.broadcasted_iota(jnp.int32, (shape[0], shape[0]), 1)
    before = jnp.where(c_id < r_id, 1.0, 0.0).astype(BF16)
    rank = _dot(before, chosen.astype(BF16)) + count_ref[...]
    rank1 = jnp.sum(jnp.where(lane == i1, rank, 0.0), axis=-1, keepdims=True)
    rank2 = jnp.sum(jnp.where(lane == i2, rank, 0.0), axis=-1, keepdims=True)
    count_ref[...] += jnp.sum(chosen, axis=0, keepdims=True)
    rec = jnp.zeros(shape, F32)
    for k, val in ((R_E1, i1.astype(F32)), (R_E2, i2.astype(F32)), (R_W1, w1), (R_W2, w2),
                   (R_RANK1, rank1), (R_RANK2, rank2)):
        rec = jnp.where(lane == k, val, rec)
    return rec


def _mix_out_kernel(*refs, route):
    (x_ref, r_ref, a_ref, gr_ref, ga_ref, mod_ref, n2_ref, wr_ref, wa_ref, wo_ref) = refs[:10]
    if route:
        wrt_ref, brt_ref, x1_ref, h2_ref, rec_ref, cnt_ref, count_ref = refs[10:]
    else:
        x1_ref, h2_ref = refs[10:]
    merged = (_sigmoid(gr_ref[...]) * _dot(r_ref[...], wr_ref[...])
              + _sigmoid(ga_ref[...]) * _dot(a_ref[...], wa_ref[...]))
    mix = _dot(merged.astype(BF16), wo_ref[...])
    x1 = x_ref[...] + mod_ref[2:3, :] * mix
    x1_ref[...] = x1
    h2 = _rms(x1, n2_ref[...]) * (1.0 + mod_ref[4:5, :]) + mod_ref[3:4, :]
    if not route:
        h2_ref[...] = h2.astype(BF16)
        return

    @pl.when(pl.program_id(0) == 0)
    def _():
        count_ref[...] = jnp.zeros_like(count_ref)

    for c in range(D_MODEL // LANES):
        h2_ref[:, c, :] = h2[:, c * LANES:(c + 1) * LANES]
    rec_ref[...] = _route(h2, wrt_ref, brt_ref, count_ref)
    cnt_ref[...] = count_ref[...]


def _mix_out(x, out_r, out_a, gr, ga, mods, l, norm2, wr_bf, wa_bf, wo_bf, seq_len, lat,
             router=None):
    n = x.shape[0]
    tiles_per_seq = seq_len // TM
    if lat:
        mod_idx = lambda i: (l, 1 + i // tiles_per_seq, 0, 0)
    else:
        mod_idx = lambda i: (l, 0, 0, 0)
    row = lambda i: (i, 0)
    wspec = pl.BlockSpec((None, D_MODEL, D_MODEL), lambda i: (l, 0, 0))
    in_specs = [pl.BlockSpec((TM, D_MODEL), row)] * 5 + [
        pl.BlockSpec((None, None, 6, D_MODEL), mod_idx),
        pl.BlockSpec((None, 1, D_MODEL), lambda i: (l, 0, 0)),
        wspec, wspec, wspec]
    args = [x, out_r, out_a, gr, ga, mods, norm2, wr_bf, wa_bf, wo_bf]
    out_specs = [pl.BlockSpec((TM, D_MODEL), row)]
    out_shape = [jax.ShapeDtypeStruct((n, D_MODEL), F32)]
    scratch = []
    if router is None:
        out_specs.append(pl.BlockSpec((TM, D_MODEL), row))
        out_shape.append(jax.ShapeDtypeStruct((n, D_MODEL), BF16))
    else:
        j, wr_t, br_pad = router
        in_specs += [pl.BlockSpec((None, N_EXPERTS, D_MODEL), lambda i: (j, 0, 0)),
                     pl.BlockSpec((None, 1, LANES), lambda i: (j, 0, 0))]
        args += [wr_t, br_pad]
        out_specs += [pl.BlockSpec((TM, D_MODEL // LANES, LANES), lambda i: (i, 0, 0)),
                      pl.BlockSpec((TM, LANES), row),
                      pl.BlockSpec((1, LANES), lambda i: (0, 0))]
        out_shape += [jax.ShapeDtypeStruct((n, D_MODEL // LANES, LANES), F32),
                      jax.ShapeDtypeStruct((n, LANES), F32),
                      jax.ShapeDtypeStruct((1, LANES), F32)]
        scratch = [pltpu.VMEM((1, LANES), F32)]
    return pl.pallas_call(
        functools.partial(_mix_out_kernel, route=router is not None),
        grid=(n // TM,),
        in_specs=in_specs,
        out_specs=out_specs,
        out_shape=out_shape,
        scratch_shapes=scratch,
        compiler_params=_cparams(("arbitrary",)),
        name=("mix_route" if router is not None else "mix_out") + ("_lat" if lat else "_ctx"),
    )(*args)


def _ffn_kernel(x_ref, h_ref, mod_ref, wg_ref, wu_ref, wd_ref, o_ref, *, f_chunk):
    h = h_ref[...]
    d_ff = wg_ref.shape[1]
    acc = jnp.zeros((h.shape[0], D_MODEL), F32)
    for c in range(d_ff // f_chunk):
        sl = slice(c * f_chunk, (c + 1) * f_chunk)
        act = _silu(_dot(h, wg_ref[:, sl])) * _dot(h, wu_ref[:, sl])
        acc = acc + _dot(act.astype(BF16), wd_ref[sl, :])
    o_ref[...] = x_ref[...] + mod_ref[5:6, :] * acc


def _ffn(x1, h2, mods, l, j, wg_bf, wu_bf, wd_bf, seq_len, lat):
    n = x1.shape[0]
    d_ff = wg_bf.shape[2]
    tiles_per_seq = seq_len // TM
    if lat:
        mod_idx = lambda i: (l, 1 + i // tiles_per_seq, 0, 0)
    else:
        mod_idx = lambda i: (l, 0, 0, 0)
    row = lambda i: (i, 0)
    return pl.pallas_call(
        functools.partial(_ffn_kernel, f_chunk=d_ff // 2),
        grid=(n // TM,),
        in_specs=[pl.BlockSpec((TM, D_MODEL), row), pl.BlockSpec((TM, D_MODEL), row),
                  pl.BlockSpec((None, None, 6, D_MODEL), mod_idx),
                  pl.BlockSpec((None, D_MODEL, d_ff), lambda i: (j, 0, 0)),
                  pl.BlockSpec((None, D_MODEL, d_ff), lambda i: (j, 0, 0)),
                  pl.BlockSpec((None, d_ff, D_MODEL), lambda i: (j, 0, 0))],
        out_specs=pl.BlockSpec((TM, D_MODEL), row),
        out_shape=jax.ShapeDtypeStruct((n, D_MODEL), F32),
        compiler_params=_cparams(("arbitrary",)),
        name="ffn_lat" if lat else "ffn_ctx",
    )(x1, h2, mods, wg_bf, wu_bf, wd_bf)


def _group_layout(rec, counts, n_tiles):
    as_int = lambda k: rec[:, k].astype(jnp.int32)
    cnt = counts[0, :N_EXPERTS].astype(jnp.int32)
    tiles = (cnt + TM_GROUP - 1) // TM_GROUP
    cum = jnp.cumsum(tiles)
    start = (cum - tiles) * TM_GROUP
    experts = jnp.arange(N_EXPERTS, dtype=jnp.int32)

    def pos(e, rank):
        return jnp.sum(jnp.where(e[:, None] == experts[None, :], start[None, :], 0), axis=1) + rank

    pos2 = jnp.stack([pos(as_int(R_E1), as_int(R_RANK1)), pos(as_int(R_E2), as_int(R_RANK2))])
    n_active = cum[N_EXPERTS - 1]
    tile_expert = jnp.sum(jnp.arange(n_tiles, dtype=jnp.int32)[:, None] >= cum[None, :], axis=1)
    tile_expert = jnp.minimum(tile_expert, tile_expert[n_active - 1]).astype(jnp.int32)
    return pos2.astype(jnp.int32), tile_expert, n_active.reshape(1).astype(jnp.int32)


def _row_copy(src, dst, sem):
    return pltpu.make_async_copy(src, dst, sem)


def _dispatch_kernel(pos_ref, h_ref, xs_in_ref, xs_ref, sem):
    del xs_in_ref
    n_chunks = h_ref.shape[0] // DMA_CHUNK

    def drain_chunk():
        rows = pl.ds(0, 2 * DMA_CHUNK)
        _row_copy(xs_ref.at[rows], xs_ref.at[rows], sem).wait()

    def chunk(k, carry):
        def issue(r, c):
            t = k * DMA_CHUNK + r
            _row_copy(h_ref.at[t], xs_ref.at[pos_ref[0, t]], sem).start()
            _row_copy(h_ref.at[t], xs_ref.at[pos_ref[1, t]], sem).start()
            return c

        lax.fori_loop(0, DMA_CHUNK, issue, 0, unroll=DMA_UNROLL)

        @pl.when(k > 0)
        def _():
            drain_chunk()

        return carry

    lax.fori_loop(0, n_chunks, chunk, 0)
    drain_chunk()


def _dispatch(pos2, h2_rows, n_rows):
    tile = h2_rows.shape[1:]
    return pl.pallas_call(
        _dispatch_kernel,
        grid_spec=pltpu.PrefetchScalarGridSpec(
            num_scalar_prefetch=1, grid=(1,),
            in_specs=[pl.BlockSpec(memory_space=pl.ANY), pl.BlockSpec(memory_space=pl.ANY)],
            out_specs=pl.BlockSpec(memory_space=pl.ANY),
            scratch_shapes=[pltpu.SemaphoreType.DMA(())]),
        out_shape=jax.ShapeDtypeStruct((n_rows,) + tile, F32),
        input_output_aliases={2: 0},
        compiler_params=_cparams(("arbitrary",)),
        name="moe_dispatch",
    )(pos2, h2_rows, jnp.zeros((n_rows,) + tile, F32))


def _expert_kernel(te_ref, na_ref, x_ref, wg_ref, wu_ref, wd_ref, y_ref, xb_ref):
    del te_ref
    active = pl.program_id(0) < na_ref[0]

    @pl.when(jnp.logical_not(active))
    def _():
        y_ref[...] = jnp.zeros_like(y_ref)

    @pl.when(active)
    def _():
        for c in range(D_MODEL // LANES):
            xb_ref[:, c * LANES:(c + 1) * LANES] = x_ref[:, c, :].astype(BF16)
        xb = xb_ref[...]
        act = _silu(_dot(xb, wg_ref[...])) * _dot(xb, wu_ref[...])
        y = _dot(act.astype(BF16), wd_ref[...])
        for c in range(D_MODEL // LANES):
            y_ref[:, c, :] = y[:, c * LANES:(c + 1) * LANES]


def _experts(tile_expert, n_active, xs, j, wg_bf, wu_bf, wd_bf):
    n_rows = xs.shape[0]
    d_e = wg_bf.shape[3]
    tile = xs.shape[1:]
    rows = lambda i, te, na: (jnp.minimum(i, na[0] - 1), 0, 0)
    return pl.pallas_call(
        _expert_kernel,
        grid_spec=pltpu.PrefetchScalarGridSpec(
            num_scalar_prefetch=2, grid=(n_rows // TM_GROUP,),
            in_specs=[pl.BlockSpec((TM_GROUP,) + tile, rows),
                      pl.BlockSpec((None, None, D_MODEL, d_e), lambda i, te, na: (j, te[i], 0, 0)),
                      pl.BlockSpec((None, None, D_MODEL, d_e), lambda i, te, na: (j, te[i], 0, 0)),
                      pl.BlockSpec((None, None, d_e, D_MODEL), lambda i, te, na: (j, te[i], 0, 0))],
            out_specs=pl.BlockSpec((TM_GROUP,) + tile, lambda i, te, na: (i, 0, 0)),
            scratch_shapes=[pltpu.VMEM((TM_GROUP, D_MODEL), BF16)]),
        out_shape=jax.ShapeDtypeStruct(xs.shape, F32),
        compiler_params=_cparams(("arbitrary",)),
        name="moe_experts",
    )(tile_expert, n_active, xs, wg_bf, wu_bf, wd_bf)


def _combine_kernel(pos_ref, y_ref, x_ref, rec_ref, mod_ref, o_ref,
                    a1, a2, b1, b2, sem_a, sem_b):
    i = pl.program_id(0)

    def issue(half_idx, d1, d2, sem):
        base = half_idx * DMA_CHUNK

        def body(r, c):
            _row_copy(y_ref.at[pos_ref[0, base + r]], d1.at[r], sem).start()
            _row_copy(y_ref.at[pos_ref[1, base + r]], d2.at[r], sem).start()
            return c

        lax.fori_loop(0, DMA_CHUNK, body, 0, unroll=DMA_UNROLL)

    def drain(d1, d2, sem):
        rows = pl.ds(0, DMA_CHUNK)
        _row_copy(y_ref.at[rows], d1, sem).wait()
        _row_copy(y_ref.at[rows], d2, sem).wait()

    def finish(half, d1, d2):
        rows = slice(half * DMA_CHUNK, (half + 1) * DMA_CHUNK)
        w1 = rec_ref[rows, R_W1:R_W1 + 1]
        w2 = rec_ref[rows, R_W2:R_W2 + 1]
        for c in range(D_MODEL // LANES):
            sl = slice(c * LANES, (c + 1) * LANES)
            y = w1 * d1[:, c, :] + w2 * d2[:, c, :]
            o_ref[rows, sl] = x_ref[rows, sl] + mod_ref[5:6, sl] * y

    @pl.when(i == 0)
    def _():
        issue(0, a1, a2, sem_a)

    issue(2 * i + 1, b1, b2, sem_b)
    drain(a1, a2, sem_a)
    finish(0, a1, a2)

    @pl.when(i + 1 < pl.num_programs(0))
    def _():
        issue(2 * i + 2, a1, a2, sem_a)

    drain(b1, b2, sem_b)
    finish(1, b1, b2)


def _combine(pos2, ys, x1, rec, mods, l, seq_len, lat):
    n = x1.shape[0]
    tile = ys.shape[1:]
    tm = 2 * DMA_CHUNK
    tiles_per_seq = max(seq_len // tm, 1)
    if lat:
        mod_idx = lambda i, p: (l, 1 + i // tiles_per_seq, 0, 0)
    else:
        mod_idx = lambda i, p: (l, 0, 0, 0)
    row = lambda i, p: (i, 0)
    slot = pltpu.VMEM((DMA_CHUNK,) + tile, F32)
    return pl.pallas_call(
        _combine_kernel,
        grid_spec=pltpu.PrefetchScalarGridSpec(
            num_scalar_prefetch=1, grid=(n // tm,),
            in_specs=[pl.BlockSpec(memory_space=pl.ANY),
                      pl.BlockSpec((tm, D_MODEL), row),
                      pl.BlockSpec((tm, LANES), row),
                      pl.BlockSpec((None, None, 6, D_MODEL), mod_idx)],
            out_specs=pl.BlockSpec((tm, D_MODEL), row),
            scratch_shapes=[slot, slot, slot, slot,
                            pltpu.SemaphoreType.DMA(()), pltpu.SemaphoreType.DMA(())]),
        out_shape=jax.ShapeDtypeStruct((n, D_MODEL), F32),
        compiler_params=_cparams(("arbitrary",)),
        name="moe_combine",
    )(pos2, ys, x1, rec, mods)


def _moe(x1, h2_rows, rec, counts, mods, l, j, wg_bf, wu_bf, wd_bf, seq_len, lat):
    n = x1.shape[0]
    n_tiles = 2 * n // TM_GROUP + N_EXPERTS
    pos2, tile_expert, n_active = _group_layout(rec, counts, n_tiles)
    xs = _dispatch(pos2, h2_rows, n_tiles * TM_GROUP)
    ys = _experts(tile_expert, n_active, xs, j, wg_bf, wu_bf, wd_bf)
    return _combine(pos2, ys, x1, rec, mods, l, seq_len, lat)


def _final_norm_kernel(x_ref, g_ref, o_ref):
    o_ref[...] = _rms(x_ref[...], g_ref[...])


def _final_norm(x, g):
    n = x.shape[0]
    row = lambda i: (i, 0)
    return pl.pallas_call(
        _final_norm_kernel,
        grid=(n // TM_FF,),
        in_specs=[pl.BlockSpec((TM_FF, D_MODEL), row), pl.BlockSpec((1, D_MODEL), lambda i: (0, 0))],
        out_specs=pl.BlockSpec((TM_FF, D_MODEL), row),
        out_shape=jax.ShapeDtypeStruct((n, D_MODEL), F32),
        compiler_params=_cparams(("arbitrary",)),
        name="final_norm",
    )(x, g)


def _rope_tables(t_len):
    t = jnp.arange(t_len)
    pos = jnp.stack([t // GRID_W, t % GRID_W], axis=-1).astype(F32)
    inv = ROPE_THETA ** (-jnp.arange(0, AXIS_DIM, 2, dtype=F32) / AXIS_DIM)
    ang = pos[..., None] * inv
    cos, sin = jnp.cos(ang), jnp.sin(ang)
    c = jnp.concatenate([cos[:, 0], cos[:, 0], cos[:, 1], cos[:, 1]], axis=-1)
    s = jnp.concatenate([-sin[:, 0], sin[:, 0], -sin[:, 1], sin[:, 1]], axis=-1)
    return c, s


def _gate_weights(wa, wi):
    depth = wa.shape[0]
    groups = D_RNN // LANES

    def pair(w):
        w = w.reshape(depth, 2, groups, 2, RNN_BLOCK, RNN_BLOCK)
        z = jnp.zeros_like(w[:, :, :, 0])
        top = jnp.concatenate([w[:, :, :, 0], z], axis=-1)
        bot = jnp.concatenate([z, w[:, :, :, 1]], axis=-1)
        return jnp.concatenate([top, bot], axis=-2)

    return jnp.concatenate([pair(wa), pair(wi)], axis=-1).astype(BF16)


def kernel(x_prompt, x_sample, cache_k, cache_v, state_rnn, c, c_ctx, w_mod, b_mod, norm1, norm2, w_in, conv_w, conv_b, lru_wa, lru_ba, lru_wi, lru_bi, lru_lambda, q_norm, k_norm, w_rnn_out, w_attn_out, w_out, ff_gate, ff_up, ff_down, router_w, router_b, exp_gate, exp_up, exp_down, final_norm):
    batch, seq, _ = x_prompt.shape
    dec_batch, dec_seq, _ = x_sample.shape
    depth = w_mod.shape[0]
    past = cache_k.shape[2]
    assert dec_batch + 1 <= MOD_ROWS

    cond8 = jnp.zeros((MOD_ROWS, D_MODEL), F32).at[0].set(c_ctx).at[1:1 + dec_batch].set(c)
    mods = _modulation(cond8, w_mod, b_mod).reshape(depth, MOD_ROWS, 6, D_MODEL)

    bf = lambda w: w.astype(BF16)
    w_in_bf, wr_bf, wa_bf, wo_bf = bf(w_in), bf(w_rnn_out), bf(w_attn_out), bf(w_out)
    ffg_bf, ffu_bf, ffd_bf = bf(ff_gate), bf(ff_up), bf(ff_down)
    eg_bf, eu_bf, ed_bf = bf(exp_gate), bf(exp_up), bf(exp_down)
    gate_w = _gate_weights(lru_wa, lru_wi)
    n_moe = router_w.shape[0]
    wr_t = jnp.swapaxes(router_w, 1, 2)
    br_pad = jnp.zeros((n_moe, 1, LANES), F32).at[:, 0, :N_EXPERTS].set(router_b)

    norm1_3 = norm1.reshape(depth, 1, D_MODEL)
    norm2_3 = norm2.reshape(depth, 1, D_MODEL)
    qn3 = q_norm.reshape(depth, 1, HEAD_DIM)
    kn3 = k_norm.reshape(depth, 1, HEAD_DIM)
    conv_b3 = conv_b.reshape(depth, 1, D_RNN)
    ba4 = lru_ba.reshape(depth, 2, 1, D_RNN)
    bi4 = lru_bi.reshape(depth, 2, 1, D_RNN)
    lam4 = lru_lambda.reshape(depth, 2, 1, D_RNN)
    state5 = state_rnn.reshape(dec_batch, depth, 2, 1, D_RNN)
    cache_k4 = cache_k.reshape(dec_batch, depth, past, KV_DIM)
    cache_v4 = cache_v.reshape(dec_batch, depth, past, KV_DIM)
    rope_tabs = _rope_tables(dec_seq)

    def run_stream(x, lat):
        seq_len = dec_seq if lat else seq
        ks, vs, ss = [], [], []
        for l in range(depth):
            xr, yr, q, kn, kr, v, gr, ga = _in_proj(
                x, mods, l, norm1_3, w_in_bf, qn3, kn3, rope_tabs if lat else None, seq_len)
            rnn_args = (conv_w, conv_b3, gate_w, ba4, bi4, lam4, state5 if lat else None, seq_len)
            hb, carry_b = _rnn_pass(xr, None, None, l, 1, *rnn_args)
            out_r, carry_f = _rnn_pass(xr, yr, hb, l, 0, *rnn_args)
            if lat:
                out_a = _attn_lat(q, kr, v, cache_k4, cache_v4, l, seq_len)
            else:
                out_a = _attn_ctx(q, kr, v, seq_len)
                ks.append(kn)
                vs.append(v)
                ss.append(jnp.concatenate([carry_f, carry_b], axis=1))
            moe_layer = l % 2 == 1
            mix_args = (x, out_r, out_a, gr, ga, mods, l, norm2_3, wr_bf, wa_bf, wo_bf, seq_len, lat)
            if moe_layer:
                x1, h2_rows, rec, counts = _mix_out(*mix_args, router=(l // 2, wr_t, br_pad))
                x = _moe(x1, h2_rows, rec, counts, mods, l, l // 2, eg_bf, eu_bf, ed_bf,
                         seq_len, lat)
            else:
                x1, h2 = _mix_out(*mix_args)
                x = _ffn(x1, h2, mods, l, l // 2, ffg_bf, ffu_bf, ffd_bf, seq_len, lat)
        return _final_norm(x, final_norm.reshape(1, D_MODEL)), ks, vs, ss

    y_ctx, ks, vs, ss = run_stream(x_prompt.reshape(batch * seq, D_MODEL), False)
    y_lat, _, _, _ = run_stream(x_sample.reshape(dec_batch * dec_seq, D_MODEL), True)

    new_k = jnp.stack([k.reshape(batch, seq, N_KV_HEADS, HEAD_DIM) for k in ks], axis=1)
    new_v = jnp.stack([v.reshape(batch, seq, N_KV_HEADS, HEAD_DIM) for v in vs], axis=1)
    new_state = jnp.stack(ss, axis=1)
    return (y_ctx.reshape(batch, seq, D_MODEL), y_lat.reshape(dec_batch, dec_seq, D_MODEL),
            new_k, new_v, new_state)
```

```python
import functools

import jax
import jax.numpy as jnp
from jax import lax
from jax.experimental import pallas as pl
from jax.experimental.pallas import tpu as pltpu

F32 = jnp.float32
BF16 = jnp.bfloat16

D_MODEL = 1024
D_RNN = 1024
N_RNN_BLOCKS = 16
RNN_BLOCK = D_RNN // N_RNN_BLOCKS
CONV_W = 4
CONV_LEFT = 2
LRU_C = 8.0
N_HEADS = 8
N_KV_HEADS = 2
HEAD_DIM = 128
AXIS_DIM = HEAD_DIM // 2
ROPE_THETA = 10000.0
GRID_W = 64
Q_DIM = N_HEADS * HEAD_DIM
KV_DIM = N_KV_HEADS * HEAD_DIM
N_EXPERTS = 8
EPS = 1e-6
D_IN = 2 * D_RNN + Q_DIM + 2 * KV_DIM + 2 * D_MODEL
O_XR, O_YR, O_Q = 0, D_RNN, 2 * D_RNN
O_K = O_Q + Q_DIM
O_V = O_K + KV_DIM
O_GR = O_V + KV_DIM
O_GA = O_GR + D_MODEL

LANES = 128
SUBLANES = 8
MOD_ROWS = 8
VMEM_LIMIT = 56 * 1024 * 1024

TM = 256
RNN_CHUNK = 256
TQ = 256
TM_FF = 512
TM_GROUP = 256
DMA_CHUNK = 256
DMA_UNROLL = 8


def _cparams(sem):
    return pltpu.CompilerParams(dimension_semantics=sem, vmem_limit_bytes=VMEM_LIMIT)


def _sigmoid(x):
    return 1.0 / (1.0 + jnp.exp(-x))


def _silu(x):
    return x * _sigmoid(x)


def _gelu_tanh(x):
    return 0.5 * x * (1.0 + jnp.tanh(0.7978845608028654 * (x + 0.044715 * (x * x * x))))


def _rms(x, g):
    ms = jnp.mean(x * x, axis=-1, keepdims=True)
    return x * lax.rsqrt(ms + EPS) * g


def _dot(a, b):
    return jnp.dot(a, b, preferred_element_type=F32)


def _mod_kernel(cond_ref, w_ref, b_ref, o_ref):
    s = _silu(cond_ref[...]).astype(BF16)
    o_ref[...] = _dot(s, w_ref[...].astype(BF16)) + b_ref[...]


def _modulation(cond8, w_mod, b_mod):
    depth = w_mod.shape[0]
    tn = 1536
    return pl.pallas_call(
        _mod_kernel,
        grid=(depth, 6 * D_MODEL // tn),
        in_specs=[
            pl.BlockSpec((MOD_ROWS, D_MODEL), lambda l, j: (0, 0)),
            pl.BlockSpec((None, D_MODEL, tn), lambda l, j: (l, 0, j)),
            pl.BlockSpec((None, 1, tn), lambda l, j: (l, 0, j)),
        ],
        out_specs=pl.BlockSpec((None, MOD_ROWS, tn), lambda l, j: (l, 0, j)),
        out_shape=jax.ShapeDtypeStruct((depth, MOD_ROWS, 6 * D_MODEL), F32),
        compiler_params=_cparams(("arbitrary", "arbitrary")),
        name="adaln_mod",
    )(cond8, w_mod, b_mod.reshape(depth, 1, 6 * D_MODEL))


def _swap_halves(x):
    lane = lax.broadcasted_iota(jnp.int32, x.shape, 1)
    return jnp.where((lane % AXIS_DIM) < AXIS_DIM // 2,
                     pltpu.roll(x, LANES - AXIS_DIM // 2, axis=1),
                     pltpu.roll(x, AXIS_DIM // 2, axis=1))


def _in_proj_kernel(*refs, rope):
    if rope:
        (x_ref, mod_ref, n1_ref, w_ref, qn_ref, kn_ref, c_ref, s_ref,
         xr_ref, yr_ref, q_ref, kno_ref, kro_ref, v_ref, gr_ref, ga_ref) = refs
    else:
        (x_ref, mod_ref, n1_ref, w_ref, qn_ref, kn_ref,
         xr_ref, yr_ref, q_ref, kno_ref, kro_ref, v_ref, gr_ref, ga_ref) = refs
    x = x_ref[...]
    h = _rms(x, n1_ref[...]) * (1.0 + mod_ref[1:2, :]) + mod_ref[0:1, :]
    hb = h.astype(BF16)

    xr_ref[...] = _dot(hb, w_ref[:, O_XR:O_XR + D_RNN])
    yr_ref[...] = _dot(hb, w_ref[:, O_YR:O_YR + D_RNN])
    gr_ref[...] = _dot(hb, w_ref[:, O_GR:O_GR + D_MODEL])
    ga_ref[...] = _dot(hb, w_ref[:, O_GA:O_GA + D_MODEL])
    v_ref[...] = _dot(hb, w_ref[:, O_V:O_V + KV_DIM])

    if rope:
        cs, sn = c_ref[...], s_ref[...]
    zq = _dot(hb, w_ref[:, O_Q:O_Q + Q_DIM])
    for hd in range(N_HEADS):
        t = _rms(zq[:, hd * HEAD_DIM:(hd + 1) * HEAD_DIM], qn_ref[...])
        if rope:
            t = t * cs + _swap_halves(t) * sn
        q_ref[:, hd * HEAD_DIM:(hd + 1) * HEAD_DIM] = t.astype(BF16)
    zk = _dot(hb, w_ref[:, O_K:O_K + KV_DIM])
    for hd in range(N_KV_HEADS):
        t = _rms(zk[:, hd * HEAD_DIM:(hd + 1) * HEAD_DIM], kn_ref[...])
        kno_ref[:, hd * HEAD_DIM:(hd + 1) * HEAD_DIM] = t
        if rope:
            t = t * cs + _swap_halves(t) * sn
        kro_ref[:, hd * HEAD_DIM:(hd + 1) * HEAD_DIM] = t.astype(BF16)


def _in_proj(x, mods, l, norm1, w_in_bf, q_norm, k_norm, rope_tabs, seq_len):
    n = x.shape[0]
    tiles_per_seq = seq_len // TM
    rope = rope_tabs is not None
    if rope:
        mod_idx = lambda i: (l, 1 + i // tiles_per_seq, 0, 0)
    else:
        mod_idx = lambda i: (l, 0, 0, 0)
    row = lambda i: (i, 0)
    in_specs = [
        pl.BlockSpec((TM, D_MODEL), row),
        pl.BlockSpec((None, None, 6, D_MODEL), mod_idx),
        pl.BlockSpec((None, 1, D_MODEL), lambda i: (l, 0, 0)),
        pl.BlockSpec((None, D_MODEL, D_IN), lambda i: (l, 0, 0)),
        pl.BlockSpec((None, 1, HEAD_DIM), lambda i: (l, 0, 0)),
        pl.BlockSpec((None, 1, HEAD_DIM), lambda i: (l, 0, 0)),
    ]
    args = [x, mods, norm1, w_in_bf, q_norm, k_norm]
    if rope:
        in_specs += [pl.BlockSpec((TM, HEAD_DIM), lambda i: (i % tiles_per_seq, 0))] * 2
        args += list(rope_tabs)
    widths = [(D_RNN, F32), (D_RNN, F32), (Q_DIM, BF16), (KV_DIM, F32), (KV_DIM, BF16),
              (KV_DIM, F32), (D_MODEL, F32), (D_MODEL, F32)]
    return pl.pallas_call(
        functools.partial(_in_proj_kernel, rope=rope),
        grid=(n // TM,),
        in_specs=in_specs,
        out_specs=[pl.BlockSpec((TM, w), row) for w, _ in widths],
        out_shape=[jax.ShapeDtypeStruct((n, w), dt) for w, dt in widths],
        compiler_params=_cparams(("arbitrary",)),
        name="in_proj_lat" if rope else "in_proj_ctx",
    )(*args)


def _rnn_kernel(*refs, reverse, chunks_per_seq, n_chunks, zero_init):
    refs = list(refs)
    x_ref, xp_ref, xn_ref, cw_ref, cb_ref, gw_ref, ba_ref, bi_ref, lam_ref = refs[:9]
    refs = refs[9:]
    h0_ref = None if zero_init else refs.pop(0)
    if reverse:
        h_out_ref, carry_out_ref = refs[:2]
        refs = refs[2:]
    else:
        yr_ref, hb_ref = refs[:2]
        o_ref, carry_out_ref = refs[2:4]
        refs = refs[4:]
    ext_ref, a_ref, b_ref, carry_ref = refs

    i = pl.program_id(0)
    blk = (n_chunks - 1 - i) if reverse else i
    pos = blk % chunks_per_seq
    seq_first = pos == 0
    seq_last = pos == chunks_per_seq - 1
    scan_start = seq_last if reverse else seq_first

    halo = (SUBLANES, D_RNN)
    ext_ref[0:SUBLANES, :] = jnp.where(seq_first, jnp.zeros(halo, F32), xp_ref[...])
    ext_ref[SUBLANES:SUBLANES + RNN_CHUNK, :] = x_ref[...]
    ext_ref[SUBLANES + RNN_CHUNK:, :] = jnp.where(seq_last, jnp.zeros(halo, F32), xn_ref[...])

    @pl.when(scan_start)
    def _():
        if zero_init:
            carry_ref[...] = jnp.zeros_like(carry_ref)
        else:
            carry_ref[...] = jnp.broadcast_to(h0_ref[...], carry_ref.shape)

    lam = lam_ref[...]
    neg_c_softplus = -LRU_C * (jnp.maximum(-lam, 0.0) + jnp.log(1.0 + jnp.exp(-jnp.abs(lam))))
    for g in range(D_RNN // LANES):
        sl = slice(g * LANES, (g + 1) * LANES)
        xc = cb_ref[:, sl]
        for k in range(CONV_W):
            off = SUBLANES - CONV_LEFT + k
            xc = xc + ext_ref[off:off + RNN_CHUNK, sl] * cw_ref[k:k + 1, sl]
        pre = _dot(xc.astype(BF16), gw_ref[g])
        r = _sigmoid(pre[:, :LANES] + ba_ref[:, sl])
        gi = _sigmoid(pre[:, LANES:] + bi_ref[:, sl])
        log_a = r * neg_c_softplus[:, sl]
        a_ref[:, sl] = jnp.exp(log_a)
        b_ref[:, sl] = jnp.sqrt(1.0 - jnp.exp(2.0 * log_a)) * (gi * xc)

    row = lax.broadcasted_iota(jnp.int32, (SUBLANES, D_RNN), 0)
    n_tiles = RNN_CHUNK // SUBLANES

    def tile_step(t, carry):
        tt = (n_tiles - 1 - t) if reverse else t
        rows = pl.ds(pl.multiple_of(tt * SUBLANES, SUBLANES), SUBLANES)
        a = a_ref[rows, :]
        b = b_ref[rows, :]
        for d in (1, 2, 4):
            if reverse:
                keep = row < SUBLANES - d
                shift = SUBLANES - d
            else:
                keep = row >= d
                shift = d
            a_sh = pltpu.roll(a, shift, axis=0)
            b_sh = pltpu.roll(b, shift, axis=0)
            b = jnp.where(keep, a * b_sh + b, b)
            a = jnp.where(keep, a * a_sh, a)
        h = a * carry + b
        b_ref[rows, :] = h
        last = h[0:1, :] if reverse else h[SUBLANES - 1:SUBLANES, :]
        return jnp.broadcast_to(last, (SUBLANES, D_RNN))

    carry = lax.fori_loop(0, n_tiles, tile_step, carry_ref[...])
    carry_ref[...] = carry
    carry_out_ref[...] = carry[0:1, :]

    if reverse:
        h_out_ref[...] = b_ref[...]
    else:
        o_ref[...] = ((b_ref[...] + hb_ref[...]) * _gelu_tanh(yr_ref[...])).astype(BF16)


def _rnn_pass(xr, yr, hb, l, direction, conv_w, conv_b, gate_w, lru_ba, lru_bi, lru_lambda,
              state5, seq_len):
    n = xr.shape[0]
    reverse = direction == 1
    n_chunks = n // RNN_CHUNK
    cps = seq_len // RNN_CHUNK
    zero_init = state5 is None
    tiles_per_chunk = RNN_CHUNK // SUBLANES
    n_tiles = n // SUBLANES

    def blk(i):
        return (n_chunks - 1 - i) if reverse else i

    main = lambda i: (blk(i), 0)
    prev = lambda i: (jnp.maximum(blk(i) * tiles_per_chunk - 1, 0), 0)
    nxt = lambda i: (jnp.minimum((blk(i) + 1) * tiles_per_chunk, n_tiles - 1), 0)
    vec = lambda i: (l, direction, 0, 0)
    in_specs = [
        pl.BlockSpec((RNN_CHUNK, D_RNN), main),
        pl.BlockSpec((SUBLANES, D_RNN), prev),
        pl.BlockSpec((SUBLANES, D_RNN), nxt),
        pl.BlockSpec((None, CONV_W, D_RNN), lambda i: (l, 0, 0)),
        pl.BlockSpec((None, 1, D_RNN), lambda i: (l, 0, 0)),
        pl.BlockSpec((None, None, D_RNN // LANES, LANES, 2 * LANES), lambda i: (l, direction, 0, 0, 0)),
        pl.BlockSpec((None, None, 1, D_RNN), vec),
        pl.BlockSpec((None, None, 1, D_RNN), vec),
        pl.BlockSpec((None, None, 1, D_RNN), vec),
    ]
    args = [xr, xr, xr, conv_w, conv_b, gate_w, lru_ba, lru_bi, lru_lambda]
    if not zero_init:
        in_specs.append(pl.BlockSpec((None, None, None, 1, D_RNN),
                                     lambda i: (blk(i) // cps, l, direction, 0, 0)))
        args.append(state5)
    carry_spec = pl.BlockSpec((None, 1, D_RNN), lambda i: (blk(i), 0, 0))
    carry_shape = jax.ShapeDtypeStruct((n_chunks, 1, D_RNN), F32)
    if reverse:
        out_specs = [pl.BlockSpec((RNN_CHUNK, D_RNN), main), carry_spec]
        out_shape = [jax.ShapeDtypeStruct((n, D_RNN), F32), carry_shape]
    else:
        in_specs += [pl.BlockSpec((RNN_CHUNK, D_RNN), main)] * 2
        args += [yr, hb]
        out_specs = [pl.BlockSpec((RNN_CHUNK, D_RNN), main), carry_spec]
        out_shape = [jax.ShapeDtypeStruct((n, D_RNN), BF16), carry_shape]
    return pl.pallas_call(
        functools.partial(_rnn_kernel, reverse=reverse, chunks_per_seq=cps, n_chunks=n_chunks,
                          zero_init=zero_init),
        grid=(n_chunks,),
        in_specs=in_specs,
        out_specs=out_specs,
        out_shape=out_shape,
        scratch_shapes=[
            pltpu.VMEM((RNN_CHUNK + 2 * SUBLANES, D_RNN), F32),
            pltpu.VMEM((RNN_CHUNK, D_RNN), F32),
            pltpu.VMEM((RNN_CHUNK, D_RNN), F32),
            pltpu.VMEM((SUBLANES, D_RNN), F32),
        ],
        compiler_params=_cparams(("arbitrary",)),
        name=("rnn_bwd" if reverse else "rnn_fwd") + ("_ctx" if zero_init else "_lat"),
    )(*args)


def _attend_heads(q_ref, k_all, v_all, o_ref):
    scale = 1.0 / (HEAD_DIM ** 0.5)
    group = N_HEADS // N_KV_HEADS
    for kh in range(N_KV_HEADS):
        k = k_all[:, kh * HEAD_DIM:(kh + 1) * HEAD_DIM]
        v = v_all[:, kh * HEAD_DIM:(kh + 1) * HEAD_DIM]
        for g in range(group):
            sl = slice((kh * group + g) * HEAD_DIM, (kh * group + g + 1) * HEAD_DIM)
            s = lax.dot_general(q_ref[:, sl], k, (((1,), (1,)), ((), ())),
                                preferred_element_type=F32) * scale
            m = jnp.max(s, axis=-1, keepdims=True)
            p = jnp.exp(s - m)
            denom = jnp.sum(p, axis=-1, keepdims=True)
            o = _dot(p.astype(BF16), v)
            o_ref[:, sl] = (o / denom).astype(BF16)


def _attn_ctx_kernel(q_ref, k_ref, v_ref, o_ref):
    _attend_heads(q_ref, k_ref[...], v_ref[...].astype(BF16), o_ref)


def _attn_ctx(q, kr, v, seq_len):
    n = q.shape[0]
    row = lambda b: (b, 0)
    return pl.pallas_call(
        _attn_ctx_kernel,
        grid=(n // seq_len,),
        in_specs=[pl.BlockSpec((seq_len, Q_DIM), row),
                  pl.BlockSpec((seq_len, KV_DIM), row),
                  pl.BlockSpec((seq_len, KV_DIM), row)],
        out_specs=pl.BlockSpec((seq_len, Q_DIM), row),
        out_shape=jax.ShapeDtypeStruct((n, Q_DIM), BF16),
        compiler_params=_cparams(("arbitrary",)),
        name="attn_ctx",
    )(q, kr, v)


def _attn_lat_kernel(q_ref, kc_ref, vc_ref, kl_ref, vl_ref, o_ref, k_all, v_all, *, past):
    @pl.when(pl.program_id(1) == 0)
    def _():
        k_all[0:past, :] = kc_ref[...].astype(BF16)
        k_all[past:, :] = kl_ref[...]
        v_all[0:past, :] = vc_ref[...].astype(BF16)
        v_all[past:, :] = vl_ref[...].astype(BF16)

    _attend_heads(q_ref, k_all[...], v_all[...], o_ref)


def _attn_lat(q, kr, v, cache_k4, cache_v4, l, seq_len):
    n = q.shape[0]
    past = cache_k4.shape[2]
    nq = seq_len // TQ
    return pl.pallas_call(
        functools.partial(_attn_lat_kernel, past=past),
        grid=(n // seq_len, nq),
        in_specs=[pl.BlockSpec((TQ, Q_DIM), lambda b, j: (b * nq + j, 0)),
                  pl.BlockSpec((None, None, past, KV_DIM), lambda b, j: (b, l, 0, 0)),
                  pl.BlockSpec((None, None, past, KV_DIM), lambda b, j: (b, l, 0, 0)),
                  pl.BlockSpec((seq_len, KV_DIM), lambda b, j: (b, 0)),
                  pl.BlockSpec((seq_len, KV_DIM), lambda b, j: (b, 0))],
        out_specs=pl.BlockSpec((TQ, Q_DIM), lambda b, j: (b * nq + j, 0)),
        out_shape=jax.ShapeDtypeStruct((n, Q_DIM), BF16),
        scratch_shapes=[pltpu.VMEM((past + seq_len, KV_DIM), BF16),
                        pltpu.VMEM((past + seq_len, KV_DIM), BF16)],
        compiler_params=_cparams(("arbitrary", "arbitrary")),
        name="attn_lat",
    )(q, cache_k4, cache_v4, kr, v)


R_E1, R_E2, R_W1, R_W2, R_RANK1, R_RANK2 = range(6)


def _route(h2, wrt_ref, brt_ref, count_ref):
    shape = (h2.shape[0], LANES)
    lane = lax.broadcasted_iota(jnp.int32, shape, 1)
    neg = jnp.float32(-jnp.inf)
    logits = jnp.full(shape, neg, F32)
    for e in range(N_EXPERTS):
        prod = h2 * wrt_ref[e:e + 1, :]
        part = prod[:, 0:LANES]
        for c in range(1, D_MODEL // LANES):
            part = part + prod[:, c * LANES:(c + 1) * LANES]
        logits = jnp.where(lane == e, jnp.sum(part, axis=-1, keepdims=True), logits)
    logits = logits + brt_ref[...]
    v1 = jnp.max(logits, axis=-1, keepdims=True)
    i1 = jnp.min(jnp.where(logits == v1, lane, LANES), axis=-1, keepdims=True)
    rest = jnp.where(lane == i1, neg, logits)
    v2 = jnp.max(rest, axis=-1, keepdims=True)
    i2 = jnp.min(jnp.where(rest == v2, lane, LANES), axis=-1, keepdims=True)
    e2 = jnp.exp(v2 - v1)
    w1 = 1.0 / (1.0 + e2)
    w2 = e2 / (1.0 + e2)
    chosen = jnp.where((lane == i1) | (lane == i2), 1.0, 0.0)
    r_id = lax.broadcasted_iota(jnp.int32, (shape[0], shape[0]), 0)
    c_id = lax.broadcasted_iota(jnp.int32, (shape[0], shape[0]), 1)
    before = jnp.where(c_id < r_id, 1.0, 0.0).astype(BF16)
    rank = _dot(before, chosen.astype(BF16)) + count_ref[...]
    rank1 = jnp.sum(jnp.where(lane == i1, rank, 0.0), axis=-1, keepdims=True)
    rank2 = jnp.sum(jnp.where(lane == i2, rank, 0.0), axis=-1, keepdims=True)
    count_ref[...] += jnp.sum(chosen, axis=0, keepdims=True)
    rec = jnp.zeros(shape, F32)
    for k, val in ((R_E1, i1.astype(F32)), (R_E2, i2.astype(F32)), (R_W1, w1), (R_W2, w2),
                   (R_RANK1, rank1), (R_RANK2, rank2)):
        rec = jnp.where(lane == k, val, rec)
    return rec


def _mix_out_kernel(*refs, route):
    (x_ref, r_ref, a_ref, gr_ref, ga_ref, mod_ref, n2_ref, wr_ref, wa_ref, wo_ref) = refs[:10]
    if route:
        wrt_ref, brt_ref, x1_ref, h2_ref, rec_ref, cnt_ref, count_ref = refs[10:]
    else:
        x1_ref, h2_ref = refs[10:]
    merged = (_sigmoid(gr_ref[...]) * _dot(r_ref[...], wr_ref[...])
              + _sigmoid(ga_ref[...]) * _dot(a_ref[...], wa_ref[...]))
    mix = _dot(merged.astype(BF16), wo_ref[...])
    x1 = x_ref[...] + mod_ref[2:3, :] * mix
    x1_ref[...] = x1
    h2 = _rms(x1, n2_ref[...]) * (1.0 + mod_ref[4:5, :]) + mod_ref[3:4, :]
    if not route:
        h2_ref[...] = h2.astype(BF16)
        return

    @pl.when(pl.program_id(0) == 0)
    def _():
        count_ref[...] = jnp.zeros_like(count_ref)

    h2_ref[...] = h2
    rec_ref[...] = _route(h2, wrt_ref, brt_ref, count_ref)
    cnt_ref[...] = count_ref[...]


def _mix_out(x, out_r, out_a, gr, ga, mods, l, norm2, wr_bf, wa_bf, wo_bf, seq_len, lat,
             router=None):
    n = x.shape[0]
    tiles_per_seq = seq_len // TM
    if lat:
        mod_idx = lambda i: (l, 1 + i // tiles_per_seq, 0, 0)
    else:
        mod_idx = lambda i: (l, 0, 0, 0)
    row = lambda i: (i, 0)
    wspec = pl.BlockSpec((None, D_MODEL, D_MODEL), lambda i: (l, 0, 0))
    in_specs = [pl.BlockSpec((TM, D_MODEL), row)] * 5 + [
        pl.BlockSpec((None, None, 6, D_MODEL), mod_idx),
        pl.BlockSpec((None, 1, D_MODEL), lambda i: (l, 0, 0)),
        wspec, wspec, wspec]
    args = [x, out_r, out_a, gr, ga, mods, norm2, wr_bf, wa_bf, wo_bf]
    out_specs = [pl.BlockSpec((TM, D_MODEL), row)]
    out_shape = [jax.ShapeDtypeStruct((n, D_MODEL), F32)]
    scratch = []
    if router is None:
        out_specs.append(pl.BlockSpec((TM, D_MODEL), row))
        out_shape.append(jax.ShapeDtypeStruct((n, D_MODEL), BF16))
    else:
        j, wr_t, br_pad = router
        in_specs += [pl.BlockSpec((None, N_EXPERTS, D_MODEL), lambda i: (j, 0, 0)),
                     pl.BlockSpec((None, 1, LANES), lambda i: (j, 0, 0))]
        args += [wr_t, br_pad]
        out_specs += [pl.BlockSpec((TM, D_MODEL), row),
                      pl.BlockSpec((TM, LANES), row),
                      pl.BlockSpec((1, LANES), lambda i: (0, 0))]
        out_shape += [jax.ShapeDtypeStruct((n, D_MODEL), F32),
                      jax.ShapeDtypeStruct((n, LANES), F32),
                      jax.ShapeDtypeStruct((1, LANES), F32)]
        scratch = [pltpu.VMEM((1, LANES), F32)]
    return pl.pallas_call(
        functools.partial(_mix_out_kernel, route=router is not None),
        grid=(n // TM,),
        in_specs=in_specs,
        out_specs=out_specs,
        out_shape=out_shape,
        scratch_shapes=scratch,
        compiler_params=_cparams(("arbitrary",)),
        name=("mix_route" if router is not None else "mix_out") + ("_lat" if lat else "_ctx"),
    )(*args)


def _ffn_kernel(x_ref, h_ref, mod_ref, wg_ref, wu_ref, wd_ref, o_ref, *, f_chunk):
    h = h_ref[...]
    d_ff = wg_ref.shape[1]
    acc = jnp.zeros((h.shape[0], D_MODEL), F32)
    for c in range(d_ff // f_chunk):
        sl = slice(c * f_chunk, (c + 1) * f_chunk)
        act = _silu(_dot(h, wg_ref[:, sl])) * _dot(h, wu_ref[:, sl])
        acc = acc + _dot(act.astype(BF16), wd_ref[sl, :])
    o_ref[...] = x_ref[...] + mod_ref[5:6, :] * acc


def _ffn(x1, h2, mods, l, j, wg_bf, wu_bf, wd_bf, seq_len, lat):
    n = x1.shape[0]
    d_ff = wg_bf.shape[2]
    tiles_per_seq = seq_len // TM
    if lat:
        mod_idx = lambda i: (l, 1 + i // tiles_per_seq, 0, 0)
    else:
        mod_idx = lambda i: (l, 0, 0, 0)
    row = lambda i: (i, 0)
    return pl.pallas_call(
        functools.partial(_ffn_kernel, f_chunk=d_ff // 2),
        grid=(n // TM,),
        in_specs=[pl.BlockSpec((TM, D_MODEL), row), pl.BlockSpec((TM, D_MODEL), row),
                  pl.BlockSpec((None, None, 6, D_MODEL), mod_idx),
                  pl.BlockSpec((None, D_MODEL, d_ff), lambda i: (j, 0, 0)),
                  pl.BlockSpec((None, D_MODEL, d_ff), lambda i: (j, 0, 0)),
                  pl.BlockSpec((None, d_ff, D_MODEL), lambda i: (j, 0, 0))],
        out_specs=pl.BlockSpec((TM, D_MODEL), row),
        out_shape=jax.ShapeDtypeStruct((n, D_MODEL), F32),
        compiler_params=_cparams(("arbitrary",)),
        name="ffn_lat" if lat else "ffn_ctx",
    )(x1, h2, mods, wg_bf, wu_bf, wd_bf)


def _group_layout(rec, counts, n_tiles):
    as_int = lambda k: rec[:, k].astype(jnp.int32)
    cnt = counts[0, :N_EXPERTS].astype(jnp.int32)
    tiles = (cnt + TM_GROUP - 1) // TM_GROUP
    cum = jnp.cumsum(tiles)
    start = (cum - tiles) * TM_GROUP
    experts = jnp.arange(N_EXPERTS, dtype=jnp.int32)

    def pos(e, rank):
        return jnp.sum(jnp.where(e[:, None] == experts[None, :], start[None, :], 0), axis=1) + rank

    pos2 = jnp.stack([pos(as_int(R_E1), as_int(R_RANK1)), pos(as_int(R_E2), as_int(R_RANK2))])
    n_active = cum[N_EXPERTS - 1]
    tile_expert = jnp.sum(jnp.arange(n_tiles, dtype=jnp.int32)[:, None] >= cum[None, :], axis=1)
    tile_expert = jnp.minimum(tile_expert, tile_expert[n_active - 1]).astype(jnp.int32)
    return pos2.astype(jnp.int32), tile_expert, n_active.reshape(1).astype(jnp.int32)


def _row_copy(src, dst, sem):
    return pltpu.make_async_copy(src, dst, sem)


def _dispatch_kernel(pos_ref, h_ref, xs_in_ref, xs_ref, sem):
    del xs_in_ref
    n_chunks = h_ref.shape[0] // DMA_CHUNK

    def drain_chunk():
        rows = pl.ds(0, 2 * DMA_CHUNK)
        _row_copy(xs_ref.at[rows], xs_ref.at[rows], sem).wait()

    def chunk(k, carry):
        def issue(r, c):
            t = k * DMA_CHUNK + r
            _row_copy(h_ref.at[t], xs_ref.at[pos_ref[0, t]], sem).start()
            _row_copy(h_ref.at[t], xs_ref.at[pos_ref[1, t]], sem).start()
            return c

        lax.fori_loop(0, DMA_CHUNK, issue, 0, unroll=DMA_UNROLL)

        @pl.when(k > 0)
        def _():
            drain_chunk()

        return carry

    lax.fori_loop(0, n_chunks, chunk, 0)
    drain_chunk()


def _dispatch(pos2, h2_rows, n_rows):
    tile = h2_rows.shape[1:]
    return pl.pallas_call(
        _dispatch_kernel,
        grid_spec=pltpu.PrefetchScalarGridSpec(
            num_scalar_prefetch=1, grid=(1,),
            in_specs=[pl.BlockSpec(h2_rows.shape, lambda i, p: (0, 0, 0)),
                      pl.BlockSpec(memory_space=pl.ANY)],
            out_specs=pl.BlockSpec(memory_space=pl.ANY),
            scratch_shapes=[pltpu.SemaphoreType.DMA(())]),
        out_shape=jax.ShapeDtypeStruct((n_rows,) + tile, F32),
        input_output_aliases={2: 0},
        compiler_params=_cparams(("arbitrary",)),
        name="moe_dispatch",
    )(pos2, h2_rows, jnp.zeros((n_rows,) + tile, F32))


def _expert_kernel(te_ref, na_ref, xs_ref, wg_ref, wu_ref, wd_ref, ys_ref,
                   xbuf, ybuf, sem_in, sem_out):
    del te_ref
    i = pl.program_id(0)
    last = pl.num_programs(0) - 1
    n_active = na_ref[0]
    slot = i % 2
    n_col = D_MODEL // LANES

    def in_copies(tile, s):
        rows = pl.ds(tile * TM_GROUP, TM_GROUP)
        return [pltpu.make_async_copy(xs_ref.at[rows, c], xbuf.at[s, :, c * LANES:(c + 1) * LANES],
                                      sem_in.at[s]) for c in range(n_col)]

    def out_copies(tile, s):
        rows = pl.ds(tile * TM_GROUP, TM_GROUP)
        return [pltpu.make_async_copy(ybuf.at[s, :, c * LANES:(c + 1) * LANES], ys_ref.at[rows, c],
                                      sem_out.at[s]) for c in range(n_col)]

    @pl.when(i == 0)
    def _():
        for cp in in_copies(0, 0):
            cp.start()

    @pl.when(i + 1 < n_active)
    def _():
        for cp in in_copies(i + 1, 1 - slot):
            cp.start()

    @pl.when(i >= 2)
    def _():
        for cp in out_copies(i - 2, slot):
            cp.wait()

    @pl.when(i < n_active)
    def _():
        for cp in in_copies(i, slot):
            cp.wait()
        xb = xbuf[slot].astype(BF16)
        act = _silu(_dot(xb, wg_ref[...])) * _dot(xb, wu_ref[...])
        ybuf[slot] = _dot(act.astype(BF16), wd_ref[...])

    @pl.when(i >= n_active)
    def _():
        ybuf[slot] = jnp.zeros((TM_GROUP, D_MODEL), F32)

    for cp in out_copies(i, slot):
        cp.start()

    @pl.when(i == last)
    def _():
        for cp in out_copies(i, slot):
            cp.wait()

        @pl.when(i >= 1)
        def _():
            for cp in out_copies(i - 1, 1 - slot):
                cp.wait()


def _experts(tile_expert, n_active, xs, j, wg_bf, wu_bf, wd_bf):
    n_rows = xs.shape[0]
    d_e = wg_bf.shape[3]
    wspec = lambda shape: pl.BlockSpec((None, None) + shape, lambda i, te, na: (j, te[i], 0, 0))
    return pl.pallas_call(
        _expert_kernel,
        grid_spec=pltpu.PrefetchScalarGridSpec(
            num_scalar_prefetch=2, grid=(n_rows // TM_GROUP,),
            in_specs=[pl.BlockSpec(memory_space=pl.ANY),
                      wspec((D_MODEL, d_e)), wspec((D_MODEL, d_e)), wspec((d_e, D_MODEL))],
            out_specs=pl.BlockSpec(memory_space=pl.ANY),
            scratch_shapes=[pltpu.VMEM((2, TM_GROUP, D_MODEL), F32),
                            pltpu.VMEM((2, TM_GROUP, D_MODEL), F32),
                            pltpu.SemaphoreType.DMA((2,)), pltpu.SemaphoreType.DMA((2,))]),
        out_shape=jax.ShapeDtypeStruct(xs.shape, F32),
        compiler_params=_cparams(("arbitrary",)),
        name="moe_experts",
    )(tile_expert, n_active, xs, wg_bf, wu_bf, wd_bf)


def _combine_kernel(pos_ref, y_ref, x_ref, rec_ref, mod_ref, o_ref,
                    a1, a2, b1, b2, sem_a, sem_b):
    i = pl.program_id(0)

    def issue(half_idx, d1, d2, sem):
        base = half_idx * DMA_CHUNK

        def body(r, c):
            _row_copy(y_ref.at[pos_ref[0, base + r]], d1.at[r], sem).start()
            _row_copy(y_ref.at[pos_ref[1, base + r]], d2.at[r], sem).start()
            return c

        lax.fori_loop(0, DMA_CHUNK, body, 0, unroll=DMA_UNROLL)

    def drain(d1, d2, sem):
        rows = pl.ds(0, DMA_CHUNK)
        _row_copy(y_ref.at[rows], d1, sem).wait()
        _row_copy(y_ref.at[rows], d2, sem).wait()

    def finish(half, d1, d2):
        rows = slice(half * DMA_CHUNK, (half + 1) * DMA_CHUNK)
        w1 = rec_ref[rows, R_W1:R_W1 + 1]
        w2 = rec_ref[rows, R_W2:R_W2 + 1]
        for c in range(D_MODEL // LANES):
            sl = slice(c * LANES, (c + 1) * LANES)
            y = w1 * d1[:, c, :] + w2 * d2[:, c, :]
            o_ref[rows, sl] = x_ref[rows, sl] + mod_ref[5:6, sl] * y

    @pl.when(i == 0)
    def _():
        issue(0, a1, a2, sem_a)

    issue(2 * i + 1, b1, b2, sem_b)
    drain(a1, a2, sem_a)
    finish(0, a1, a2)

    @pl.when(i + 1 < pl.num_programs(0))
    def _():
        issue(2 * i + 2, a1, a2, sem_a)

    drain(b1, b2, sem_b)
    finish(1, b1, b2)


def _combine(pos2, ys, x1, rec, mods, l, seq_len, lat):
    n = x1.shape[0]
    tile = ys.shape[1:]
    tm = 2 * DMA_CHUNK
    tiles_per_seq = max(seq_len // tm, 1)
    if lat:
        mod_idx = lambda i, p: (l, 1 + i // tiles_per_seq, 0, 0)
    else:
        mod_idx = lambda i, p: (l, 0, 0, 0)
    row = lambda i, p: (i, 0)
    slot = pltpu.VMEM((DMA_CHUNK,) + tile, F32)
    return pl.pallas_call(
        _combine_kernel,
        grid_spec=pltpu.PrefetchScalarGridSpec(
            num_scalar_prefetch=1, grid=(n // tm,),
            in_specs=[pl.BlockSpec(memory_space=pl.ANY),
                      pl.BlockSpec((tm, D_MODEL), row),
                      pl.BlockSpec((tm, LANES), row),
                      pl.BlockSpec((None, None, 6, D_MODEL), mod_idx)],
            out_specs=pl.BlockSpec((tm, D_MODEL), row),
            scratch_shapes=[slot, slot, slot, slot,
                            pltpu.SemaphoreType.DMA(()), pltpu.SemaphoreType.DMA(())]),
        out_shape=jax.ShapeDtypeStruct((n, D_MODEL), F32),
        compiler_params=_cparams(("arbitrary",)),
        name="moe_combine",
    )(pos2, ys, x1, rec, mods)


def _moe(x1, h2_rows, rec, counts, mods, l, j, wg_bf, wu_bf, wd_bf, seq_len, lat):
    n = x1.shape[0]
    n_tiles = 2 * n // TM_GROUP + N_EXPERTS
    pos2, tile_expert, n_active = _group_layout(rec, counts, n_tiles)
    h2_rows = h2_rows.reshape(n, D_MODEL // LANES, LANES)
    xs = _dispatch(pos2, h2_rows, n_tiles * TM_GROUP)
    ys = _experts(tile_expert, n_active, xs, j, wg_bf, wu_bf, wd_bf)
    return _combine(pos2, ys, x1, rec, mods, l, seq_len, lat)


def _final_norm_kernel(x_ref, g_ref, o_ref):
    o_ref[...] = _rms(x_ref[...], g_ref[...])


def _final_norm(x, g):
    n = x.shape[0]
    row = lambda i: (i, 0)
    return pl.pallas_call(
        _final_norm_kernel,
        grid=(n // TM_FF,),
        in_specs=[pl.BlockSpec((TM_FF, D_MODEL), row), pl.BlockSpec((1, D_MODEL), lambda i: (0, 0))],
        out_specs=pl.BlockSpec((TM_FF, D_MODEL), row),
        out_shape=jax.ShapeDtypeStruct((n, D_MODEL), F32),
        compiler_params=_cparams(("arbitrary",)),
        name="final_norm",
    )(x, g)


def _rope_tables(t_len):
    t = jnp.arange(t_len)
    pos = jnp.stack([t // GRID_W, t % GRID_W], axis=-1).astype(F32)
    inv = ROPE_THETA ** (-jnp.arange(0, AXIS_DIM, 2, dtype=F32) / AXIS_DIM)
    ang = pos[..., None] * inv
    cos, sin = jnp.cos(ang), jnp.sin(ang)
    c = jnp.concatenate([cos[:, 0], cos[:, 0], cos[:, 1], cos[:, 1]], axis=-1)
    s = jnp.concatenate([-sin[:, 0], sin[:, 0], -sin[:, 1], sin[:, 1]], axis=-1)
    return c, s


def _gate_weights(wa, wi):
    depth = wa.shape[0]
    groups = D_RNN // LANES

    def pair(w):
        w = w.reshape(depth, 2, groups, 2, RNN_BLOCK, RNN_BLOCK)
        z = jnp.zeros_like(w[:, :, :, 0])
        top = jnp.concatenate([w[:, :, :, 0], z], axis=-1)
        bot = jnp.concatenate([z, w[:, :, :, 1]], axis=-1)
        return jnp.concatenate([top, bot], axis=-2)

    return jnp.concatenate([pair(wa), pair(wi)], axis=-1).astype(BF16)


def kernel(x_prompt, x_sample, cache_k, cache_v, state_rnn, c, c_ctx, w_mod, b_mod, norm1, norm2, w_in, conv_w, conv_b, lru_wa, lru_ba, lru_wi, lru_bi, lru_lambda, q_norm, k_norm, w_rnn_out, w_attn_out, w_out, ff_gate, ff_up, ff_down, router_w, router_b, exp_gate, exp_up, exp_down, final_norm):
    batch, seq, _ = x_prompt.shape
    dec_batch, dec_seq, _ = x_sample.shape
    depth = w_mod.shape[0]
    past = cache_k.shape[2]
    assert dec_batch + 1 <= MOD_ROWS

    cond8 = jnp.zeros((MOD_ROWS, D_MODEL), F32).at[0].set(c_ctx).at[1:1 + dec_batch].set(c)
    mods = _modulation(cond8, w_mod, b_mod).reshape(depth, MOD_ROWS, 6, D_MODEL)

    bf = lambda w: w.astype(BF16)
    w_in_bf, wr_bf, wa_bf, wo_bf = bf(w_in), bf(w_rnn_out), bf(w_attn_out), bf(w_out)
    ffg_bf, ffu_bf, ffd_bf = bf(ff_gate), bf(ff_up), bf(ff_down)
    eg_bf, eu_bf, ed_bf = bf(exp_gate), bf(exp_up), bf(exp_down)
    gate_w = _gate_weights(lru_wa, lru_wi)
    n_moe = router_w.shape[0]
    wr_t = jnp.swapaxes(router_w, 1, 2)
    br_pad = jnp.zeros((n_moe, 1, LANES), F32).at[:, 0, :N_EXPERTS].set(router_b)

    norm1_3 = norm1.reshape(depth, 1, D_MODEL)
    norm2_3 = norm2.reshape(depth, 1, D_MODEL)
    qn3 = q_norm.reshape(depth, 1, HEAD_DIM)
    kn3 = k_norm.reshape(depth, 1, HEAD_DIM)
    conv_b3 = conv_b.reshape(depth, 1, D_RNN)
    ba4 = lru_ba.reshape(depth, 2, 1, D_RNN)
    bi4 = lru_bi.reshape(depth, 2, 1, D_RNN)
    lam4 = lru_lambda.reshape(depth, 2, 1, D_RNN)
    state5 = state_rnn.reshape(dec_batch, depth, 2, 1, D_RNN)
    cache_k4 = cache_k.reshape(dec_batch, depth, past, KV_DIM)
    cache_v4 = cache_v.reshape(dec_batch, depth, past, KV_DIM)
    rope_tabs = _rope_tables(dec_seq)

    def run_stream(x, lat):
        seq_len = dec_seq if lat else seq
        ks, vs, ss = [], [], []
        for l in range(depth):
            xr, yr, q, kn, kr, v, gr, ga = _in_proj(
                x, mods, l, norm1_3, w_in_bf, qn3, kn3, rope_tabs if lat else None, seq_len)
            rnn_args = (conv_w, conv_b3, gate_w, ba4, bi4, lam4, state5 if lat else None, seq_len)
            hb, carry_b = _rnn_pass(xr, None, None, l, 1, *rnn_args)
            out_r, carry_f = _rnn_pass(xr, yr, hb, l, 0, *rnn_args)
            if lat:
                out_a = _attn_lat(q, kr, v, cache_k4, cache_v4, l, seq_len)
            else:
                out_a = _attn_ctx(q, kr, v, seq_len)
                ks.append(kn)
                vs.append(v)
                ss.append(jnp.concatenate([carry_f, carry_b], axis=1))
            moe_layer = l % 2 == 1
            mix_args = (x, out_r, out_a, gr, ga, mods, l, norm2_3, wr_bf, wa_bf, wo_bf, seq_len, lat)
            if moe_layer:
                x1, h2_rows, rec, counts = _mix_out(*mix_args, router=(l // 2, wr_t, br_pad))
                x = _moe(x1, h2_rows, rec, counts, mods, l, l // 2, eg_bf, eu_bf, ed_bf,
                         seq_len, lat)
            else:
                x1, h2 = _mix_out(*mix_args)
                x = _ffn(x1, h2, mods, l, l // 2, ffg_bf, ffu_bf, ffd_bf, seq_len, lat)
        return _final_norm(x, final_norm.reshape(1, D_MODEL)), ks, vs, ss

    y_ctx, ks, vs, ss = run_stream(x_prompt.reshape(batch * seq, D_MODEL), False)
    y_lat, _, _, _ = run_stream(x_sample.reshape(dec_batch * dec_seq, D_MODEL), True)

    new_k = jnp.stack([k.reshape(batch, seq, N_KV_HEADS, HEAD_DIM) for k in ks], axis=1)
    new_v = jnp.stack([v.reshape(batch, seq, N_KV_HEADS, HEAD_DIM) for v in vs], axis=1)
    new_state = jnp.stack(ss, axis=1)
    return (y_ctx.reshape(batch, seq, D_MODEL), y_lat.reshape(dec_batch, dec_seq, D_MODEL),
            new_k, new_v, new_state)
```

```python
import functools

import jax
import jax.numpy as jnp
from jax import lax
from jax.experimental import pallas as pl
from jax.experimental.pallas import tpu as pltpu

F32 = jnp.float32
BF16 = jnp.bfloat16

D_MODEL = 1024
D_RNN = 1024
N_RNN_BLOCKS = 16
RNN_BLOCK = D_RNN // N_RNN_BLOCKS
CONV_W = 4
CONV_LEFT = 2
LRU_C = 8.0
N_HEADS = 8
N_KV_HEADS = 2
HEAD_DIM = 128
AXIS_DIM = HEAD_DIM // 2
ROPE_THETA = 10000.0
GRID_W = 64
Q_DIM = N_HEADS * HEAD_DIM
KV_DIM = N_KV_HEADS * HEAD_DIM
N_EXPERTS = 8
EPS = 1e-6
LOG2_E = 1.4426950408889634
D_IN = 2 * D_RNN + Q_DIM + 2 * KV_DIM + 2 * D_MODEL
O_XR, O_YR, O_Q = 0, D_RNN, 2 * D_RNN
O_K = O_Q + Q_DIM
O_V = O_K + KV_DIM
O_GR = O_V + KV_DIM
O_GA = O_GR + D_MODEL

LANES = 128
SUBLANES = 8
MOD_ROWS = 8
VMEM_LIMIT = 56 * 1024 * 1024

TM = 256
RNN_T = 256
RNN_SEG = SUBLANES
RNN_C = 256
RNN_TCH = 32
RNN_SCAN_UNROLL = 8
TQ = 256
TM_FF = 512
TM_GROUP = 256
DMA_CHUNK = 256
DMA_UNROLL = 8


def _cparams(sem):
    return pltpu.CompilerParams(dimension_semantics=sem, vmem_limit_bytes=VMEM_LIMIT)


def _sigmoid(x):
    return 1.0 / (1.0 + jnp.exp(-x))


def _silu(x):
    return x * _sigmoid(x)


def _gelu_tanh(x):
    return 0.5 * x * (1.0 + jnp.tanh(0.7978845608028654 * (x + 0.044715 * (x * x * x))))


def _rms(x, g):
    ms = jnp.mean(x * x, axis=-1, keepdims=True)
    return x * lax.rsqrt(ms + EPS) * g


def _dot(a, b):
    return jnp.dot(a, b, preferred_element_type=F32)


def _mod_kernel(cond_ref, w_ref, b_ref, o_ref):
    s = _silu(cond_ref[...]).astype(BF16)
    o_ref[...] = _dot(s, w_ref[...].astype(BF16)) + b_ref[...]


def _modulation(cond8, w_mod, b_mod):
    depth = w_mod.shape[0]
    tn = 1536
    return pl.pallas_call(
        _mod_kernel,
        grid=(depth, 6 * D_MODEL // tn),
        in_specs=[
            pl.BlockSpec((MOD_ROWS, D_MODEL), lambda l, j: (0, 0)),
            pl.BlockSpec((None, D_MODEL, tn), lambda l, j: (l, 0, j)),
            pl.BlockSpec((None, 1, tn), lambda l, j: (l, 0, j)),
        ],
        out_specs=pl.BlockSpec((None, MOD_ROWS, tn), lambda l, j: (l, 0, j)),
        out_shape=jax.ShapeDtypeStruct((depth, MOD_ROWS, 6 * D_MODEL), F32),
        compiler_params=_cparams(("arbitrary", "arbitrary")),
        name="adaln_mod",
    )(cond8, w_mod, b_mod.reshape(depth, 1, 6 * D_MODEL))


def _swap_halves(x):
    lane = lax.broadcasted_iota(jnp.int32, x.shape, 1)
    return jnp.where((lane % AXIS_DIM) < AXIS_DIM // 2,
                     pltpu.roll(x, LANES - AXIS_DIM // 2, axis=1),
                     pltpu.roll(x, AXIS_DIM // 2, axis=1))


def _in_proj_kernel(*refs, rope):
    if rope:
        (x_ref, mod_ref, n1_ref, w_ref, qn_ref, kn_ref, c_ref, s_ref,
         xr_ref, yr_ref, q_ref, kno_ref, kro_ref, v_ref, gr_ref, ga_ref) = refs
    else:
        (x_ref, mod_ref, n1_ref, w_ref, qn_ref, kn_ref,
         xr_ref, yr_ref, q_ref, kno_ref, kro_ref, v_ref, gr_ref, ga_ref) = refs
    x = x_ref[...]
    h = _rms(x, n1_ref[...]) * (1.0 + mod_ref[1:2, :]) + mod_ref[0:1, :]
    hb = h.astype(BF16)

    xr_ref[...] = _dot(hb, w_ref[:, O_XR:O_XR + D_RNN])
    yr_ref[...] = _dot(hb, w_ref[:, O_YR:O_YR + D_RNN])
    gr_ref[...] = _dot(hb, w_ref[:, O_GR:O_GR + D_MODEL])
    ga_ref[...] = _dot(hb, w_ref[:, O_GA:O_GA + D_MODEL])
    v_ref[...] = _dot(hb, w_ref[:, O_V:O_V + KV_DIM])

    if rope:
        cs, sn = c_ref[...], s_ref[...]
    zq = _dot(hb, w_ref[:, O_Q:O_Q + Q_DIM])
    for hd in range(N_HEADS):
        t = _rms(zq[:, hd * HEAD_DIM:(hd + 1) * HEAD_DIM], qn_ref[...])
        if rope:
            t = t * cs + _swap_halves(t) * sn
        q_ref[:, hd * HEAD_DIM:(hd + 1) * HEAD_DIM] = t.astype(BF16)
    zk = _dot(hb, w_ref[:, O_K:O_K + KV_DIM])
    for hd in range(N_KV_HEADS):
        t = _rms(zk[:, hd * HEAD_DIM:(hd + 1) * HEAD_DIM], kn_ref[...])
        kno_ref[:, hd * HEAD_DIM:(hd + 1) * HEAD_DIM] = t
        if rope:
            t = t * cs + _swap_halves(t) * sn
        kro_ref[:, hd * HEAD_DIM:(hd + 1) * HEAD_DIM] = t.astype(BF16)


def _in_proj(x, mods, l, norm1, w_in_bf, q_norm, k_norm, rope_tabs, seq_len):
    n = x.shape[0]
    tiles_per_seq = seq_len // TM
    rope = rope_tabs is not None
    if rope:
        mod_idx = lambda i: (l, 1 + i // tiles_per_seq, 0, 0)
    else:
        mod_idx = lambda i: (l, 0, 0, 0)
    row = lambda i: (i, 0)
    in_specs = [
        pl.BlockSpec((TM, D_MODEL), row),
        pl.BlockSpec((None, None, 6, D_MODEL), mod_idx),
        pl.BlockSpec((None, 1, D_MODEL), lambda i: (l, 0, 0)),
        pl.BlockSpec((None, D_MODEL, D_IN), lambda i: (l, 0, 0)),
        pl.BlockSpec((None, 1, HEAD_DIM), lambda i: (l, 0, 0)),
        pl.BlockSpec((None, 1, HEAD_DIM), lambda i: (l, 0, 0)),
    ]
    args = [x, mods, norm1, w_in_bf, q_norm, k_norm]
    if rope:
        in_specs += [pl.BlockSpec((TM, HEAD_DIM), lambda i: (i % tiles_per_seq, 0))] * 2
        args += list(rope_tabs)
    widths = [(D_RNN, F32), (D_RNN, F32), (Q_DIM, BF16), (KV_DIM, F32), (KV_DIM, BF16),
              (KV_DIM, F32), (D_MODEL, F32), (D_MODEL, F32)]
    return pl.pallas_call(
        functools.partial(_in_proj_kernel, rope=rope),
        grid=(n // TM,),
        in_specs=in_specs,
        out_specs=[pl.BlockSpec((TM, w), row) for w, _ in widths],
        out_shape=[jax.ShapeDtypeStruct((n, w), dt) for w, dt in widths],
        compiler_params=_cparams(("arbitrary",)),
        name="in_proj_lat" if rope else "in_proj_ctx",
    )(*args)


def _rnn_kernel(*refs, chained):
    refs = list(refs)
    xr_ref, yr_ref, cw_ref, cb_ref, gw_ref, ba_ref, bi_ref, lam_ref = refs[:8]
    refs = refs[8:]
    h0_ref = refs.pop(0) if chained else None
    out_ref, sf_ref, sb_ref, xext, ybuf, obuf, a_f, b_f, a_b, b_b, sem_in, sem_out = refs

    g = pl.program_id(0)
    cb = pl.program_id(1)
    n_cb = pl.num_programs(1)
    step = g * n_cb + cb
    n_steps = pl.num_programs(0) * n_cb
    slot = step % 2

    def hbm_window(kk, s):
        rows = pl.ds(pl.multiple_of(((kk // n_cb) * RNN_SEG + s) * RNN_T, RNN_T), RNN_T)
        cols = pl.ds(pl.multiple_of((kk % n_cb) * RNN_C, RNN_C), RNN_C)
        return rows, cols

    def in_copies(kk, sl_):
        cps = []
        for s in range(RNN_SEG):
            win = hbm_window(kk, s)
            cps.append(pltpu.make_async_copy(
                xr_ref.at[win], xext.at[sl_, pl.ds(CONV_LEFT, RNN_T), s], sem_in.at[sl_]))
            cps.append(pltpu.make_async_copy(yr_ref.at[win], ybuf.at[sl_, :, s], sem_in.at[sl_]))
        return cps

    def out_copies(kk, sl_):
        return [pltpu.make_async_copy(obuf.at[sl_, :, s], out_ref.at[hbm_window(kk, s)],
                                      sem_out.at[sl_]) for s in range(RNN_SEG)]

    @pl.when(step == 0)
    def _():
        for cp in in_copies(0, 0):
            cp.start()

    @pl.when(step + 1 < n_steps)
    def _():
        for cp in in_copies(step + 1, 1 - slot):
            cp.start()

    for cp in in_copies(step, slot):
        cp.wait()

    seg = lax.broadcasted_iota(jnp.int32, (RNN_SEG, RNN_C), 0)
    zero_row = jnp.zeros((RNN_SEG, RNN_C), F32)
    if chained:
        xext[slot, 0] = jnp.where(seg == 0, zero_row, pltpu.roll(xext[slot, RNN_T], 1, axis=0))
        xext[slot, 1] = jnp.where(seg == 0, zero_row, pltpu.roll(xext[slot, RNN_T + 1], 1, axis=0))
        xext[slot, RNN_T + CONV_LEFT] = jnp.where(
            seg == RNN_SEG - 1, zero_row, pltpu.roll(xext[slot, CONV_LEFT], RNN_SEG - 1, axis=0))
    else:
        xext[slot, 0] = zero_row
        xext[slot, 1] = zero_row
        xext[slot, RNN_T + CONV_LEFT] = zero_row

    scans = ((a_f, b_f), (a_b, b_b))
    rows2d = RNN_TCH * RNN_SEG

    def gate_chunk(tc, carry):
        t0 = pl.multiple_of(tc * RNN_TCH, RNN_TCH)
        for j in range(RNN_C // LANES):
            sl = slice(j * LANES, (j + 1) * LANES)
            xc = jnp.broadcast_to(cb_ref[:, sl].reshape(1, 1, LANES), (RNN_TCH, RNN_SEG, LANES))
            for k in range(CONV_W):
                xc = xc + (xext[slot, pl.ds(t0 + k, RNN_TCH), :, sl]
                           * cw_ref[k:k + 1, sl].reshape(1, 1, LANES))
            xc = xc.reshape(rows2d, LANES)
            pre = _dot(xc.astype(BF16), gw_ref[j])
            for d, (a_s, b_s) in enumerate(scans):
                lam = lam_ref[d, :, sl]
                log2_decay = (-LRU_C * LOG2_E) * (jnp.maximum(-lam, 0.0)
                                                  + jnp.log(1.0 + jnp.exp(-jnp.abs(lam))))
                r = 0.5 * jnp.tanh(pre[:, 2 * d * LANES:(2 * d + 1) * LANES]
                                   + ba_ref[d, :, sl]) + 0.5
                gi = 0.5 * jnp.tanh(pre[:, (2 * d + 1) * LANES:(2 * d + 2) * LANES]
                                    + bi_ref[d, :, sl]) + 0.5
                a = jnp.exp2(r * log2_decay)
                b = jnp.sqrt(1.0 - a * a) * (gi * xc)
                a_s[pl.ds(t0, RNN_TCH), :, sl] = a.reshape(RNN_TCH, RNN_SEG, LANES)
                b_s[pl.ds(t0, RNN_TCH), :, sl] = b.reshape(RNN_TCH, RNN_SEG, LANES)
        return carry

    lax.fori_loop(0, RNN_T // RNN_TCH, gate_chunk, 0)

    def scan_step(t, carry):
        hf, hb, pf, pb = carry
        tb = RNN_T - 1 - t
        af_t = a_f[t]
        hf = af_t * hf + b_f[t]
        b_f[t] = hf
        ab_t = a_b[tb]
        hb = ab_t * hb + b_b[tb]
        b_b[tb] = hb
        if chained:
            pf = af_t * pf
            a_f[t] = pf
            pb = ab_t * pb
            a_b[tb] = pb
        return hf, hb, pf, pb

    ones = jnp.ones((RNN_SEG, RNN_C), F32)
    hf, hb, pf, pb = lax.fori_loop(0, RNN_T, scan_step, (zero_row, zero_row, ones, ones),
                                   unroll=RNN_SCAN_UNROLL)

    if chained:
        st_f = jnp.broadcast_to(h0_ref[0], (RNN_SEG, RNN_C))
        st_b = jnp.broadcast_to(h0_ref[1], (RNN_SEG, RNN_C))
        in_f, in_b = st_f, st_b
        for _ in range(RNN_SEG - 1):
            in_f = jnp.where(seg == 0, st_f, pltpu.roll(pf * in_f + hf, 1, axis=0))
            in_b = jnp.where(seg == RNN_SEG - 1, st_b,
                             pltpu.roll(pb * in_b + hb, RNN_SEG - 1, axis=0))
        sf_ref[...] = pf * in_f + hf
        sb_ref[...] = pb * in_b + hb
    else:
        sf_ref[...] = hf
        sb_ref[...] = hb

    @pl.when(step >= 2)
    def _():
        for cp in out_copies(step - 2, slot):
            cp.wait()

    def finish_chunk(tc, carry):
        rows = pl.ds(pl.multiple_of(tc * RNN_TCH, RNN_TCH), RNN_TCH)
        h_f, h_b = b_f[rows], b_b[rows]
        if chained:
            h_f = a_f[rows] * in_f.reshape(1, RNN_SEG, RNN_C) + h_f
            h_b = a_b[rows] * in_b.reshape(1, RNN_SEG, RNN_C) + h_b
        obuf[slot, rows] = (h_f + h_b) * _gelu_tanh(ybuf[slot, rows])
        return carry

    lax.fori_loop(0, RNN_T // RNN_TCH, finish_chunk, 0)

    for cp in out_copies(step, slot):
        cp.start()

    @pl.when(step == n_steps - 1)
    def _():
        for cp in out_copies(step, slot):
            cp.wait()

        @pl.when(step >= 1)
        def _():
            for cp in out_copies(step - 1, 1 - slot):
                cp.wait()


def _rnn(xr, yr, l, conv_w, conv_b, gate_w, lru_ba, lru_bi, lru_lambda, state5, seq_len):
    n = xr.shape[0]
    chained = state5 is not None
    group_len = RNN_SEG * RNN_T
    assert n % group_len == 0 and seq_len in ((group_len,) if chained else (RNN_T,))
    n_groups = n // group_len
    lanes_per_step = RNN_C // LANES
    vec = pl.BlockSpec((None, 2, 1, RNN_C), lambda g, c: (l, 0, 0, c))
    in_specs = [
        pl.BlockSpec(memory_space=pl.ANY),
        pl.BlockSpec(memory_space=pl.ANY),
        pl.BlockSpec((None, CONV_W, RNN_C), lambda g, c: (l, 0, c)),
        pl.BlockSpec((None, 1, RNN_C), lambda g, c: (l, 0, c)),
        pl.BlockSpec((None, lanes_per_step, LANES, 4 * LANES), lambda g, c: (l, c, 0, 0)),
        vec, vec, vec,
    ]
    args = [xr, yr, conv_w, conv_b, gate_w, lru_ba, lru_bi, lru_lambda]
    if chained:
        in_specs.append(pl.BlockSpec((None, None, 2, 1, RNN_C), lambda g, c: (g, l, 0, 0, c)))
        args.append(state5)
    state_spec = pl.BlockSpec((RNN_SEG, RNN_C), lambda g, c: (g, c))
    state_shape = jax.ShapeDtypeStruct((n_groups * RNN_SEG, D_RNN), F32)
    work = lambda t: pltpu.VMEM((t, RNN_SEG, RNN_C), F32)
    return pl.pallas_call(
        functools.partial(_rnn_kernel, chained=chained),
        grid=(n_groups, D_RNN // RNN_C),
        in_specs=in_specs,
        out_specs=[pl.BlockSpec(memory_space=pl.ANY), state_spec, state_spec],
        out_shape=[jax.ShapeDtypeStruct((n, D_RNN), F32), state_shape, state_shape],
        scratch_shapes=[
            pltpu.VMEM((2, RNN_T + CONV_W - 1, RNN_SEG, RNN_C), F32),
            pltpu.VMEM((2, RNN_T, RNN_SEG, RNN_C), F32),
            pltpu.VMEM((2, RNN_T, RNN_SEG, RNN_C), F32),
            work(RNN_T), work(RNN_T), work(RNN_T), work(RNN_T),
            pltpu.SemaphoreType.DMA((2,)), pltpu.SemaphoreType.DMA((2,)),
        ],
        compiler_params=_cparams(("arbitrary", "arbitrary")),
        name="rnn_lat" if chained else "rnn_ctx",
    )(*args)


def _attend_heads(q_ref, k_all, v_all, o_ref):
    exp2_scale = (1.0 / (HEAD_DIM ** 0.5)) * 1.4426950408889634
    group = N_HEADS // N_KV_HEADS
    for kh in range(N_KV_HEADS):
        k = k_all[:, kh * HEAD_DIM:(kh + 1) * HEAD_DIM]
        v = v_all[:, kh * HEAD_DIM:(kh + 1) * HEAD_DIM]
        for g in range(group):
            sl = slice((kh * group + g) * HEAD_DIM, (kh * group + g + 1) * HEAD_DIM)
            s = lax.dot_general(q_ref[:, sl], k, (((1,), (1,)), ((), ())),
                                preferred_element_type=F32)
            m = jnp.max(s, axis=-1, keepdims=True)
            p = jnp.exp2((s - m) * exp2_scale)
            denom = jnp.sum(p, axis=-1, keepdims=True)
            o = _dot(p.astype(BF16), v)
            o_ref[:, sl] = (o / denom).astype(BF16)


def _attn_ctx_kernel(q_ref, k_ref, v_ref, o_ref):
    _attend_heads(q_ref, k_ref[...], v_ref[...].astype(BF16), o_ref)


def _attn_ctx(q, kr, v, seq_len):
    n = q.shape[0]
    row = lambda b: (b, 0)
    return pl.pallas_call(
        _attn_ctx_kernel,
        grid=(n // seq_len,),
        in_specs=[pl.BlockSpec((seq_len, Q_DIM), row),
                  pl.BlockSpec((seq_len, KV_DIM), row),
                  pl.BlockSpec((seq_len, KV_DIM), row)],
        out_specs=pl.BlockSpec((seq_len, Q_DIM), row),
        out_shape=jax.ShapeDtypeStruct((n, Q_DIM), BF16),
        compiler_params=_cparams(("arbitrary",)),
        name="attn_ctx",
    )(q, kr, v)


def _attn_lat_kernel(q_ref, kc_ref, vc_ref, kl_ref, vl_ref, o_ref, k_all, v_all, *, past):
    @pl.when(pl.program_id(1) == 0)
    def _():
        k_all[0:past, :] = kc_ref[...].astype(BF16)
        k_all[past:, :] = kl_ref[...]
        v_all[0:past, :] = vc_ref[...].astype(BF16)
        v_all[past:, :] = vl_ref[...].astype(BF16)

    _attend_heads(q_ref, k_all[...], v_all[...], o_ref)


def _attn_lat(q, kr, v, cache_k4, cache_v4, l, seq_len):
    n = q.shape[0]
    past = cache_k4.shape[2]
    nq = seq_len // TQ
    return pl.pallas_call(
        functools.partial(_attn_lat_kernel, past=past),
        grid=(n // seq_len, nq),
        in_specs=[pl.BlockSpec((TQ, Q_DIM), lambda b, j: (b * nq + j, 0)),
                  pl.BlockSpec((None, None, past, KV_DIM), lambda b, j: (b, l, 0, 0)),
                  pl.BlockSpec((None, None, past, KV_DIM), lambda b, j: (b, l, 0, 0)),
                  pl.BlockSpec((seq_len, KV_DIM), lambda b, j: (b, 0)),
                  pl.BlockSpec((seq_len, KV_DIM), lambda b, j: (b, 0))],
        out_specs=pl.BlockSpec((TQ, Q_DIM), lambda b, j: (b * nq + j, 0)),
        out_shape=jax.ShapeDtypeStruct((n, Q_DIM), BF16),
        scratch_shapes=[pltpu.VMEM((past + seq_len, KV_DIM), BF16),
                        pltpu.VMEM((past + seq_len, KV_DIM), BF16)],
        compiler_params=_cparams(("arbitrary", "arbitrary")),
        name="attn_lat",
    )(q, cache_k4, cache_v4, kr, v)


R_E1, R_E2, R_W1, R_W2, R_RANK1, R_RANK2 = range(6)


def _route(h2, wrt_ref, brt_ref, count_ref):
    shape = (h2.shape[0], LANES)
    lane = lax.broadcasted_iota(jnp.int32, shape, 1)
    neg = jnp.float32(-jnp.inf)
    logits = jnp.full(shape, neg, F32)
    for e in range(N_EXPERTS):
        prod = h2 * wrt_ref[e:e + 1, :]
        part = prod[:, 0:LANES]
        for c in range(1, D_MODEL // LANES):
            part = part + prod[:, c * LANES:(c + 1) * LANES]
        logits = jnp.where(lane == e, jnp.sum(part, axis=-1, keepdims=True), logits)
    logits = logits + brt_ref[...]
    v1 = jnp.max(logits, axis=-1, keepdims=True)
    i1 = jnp.min(jnp.where(logits == v1, lane, LANES), axis=-1, keepdims=True)
    rest = jnp.where(lane == i1, neg, logits)
    v2 = jnp.max(rest, axis=-1, keepdims=True)
    i2 = jnp.min(jnp.where(rest == v2, lane, LANES), axis=-1, keepdims=True)
    e2 = jnp.exp(v2 - v1)
    w1 = 1.0 / (1.0 + e2)
    w2 = e2 / (1.0 + e2)
    chosen = jnp.where((lane == i1) | (lane == i2), 1.0, 0.0)
    r_id = lax.broadcasted_iota(jnp.int32, (shape[0], shape[0]), 0)
    c_id = lax.broadcasted_iota(jnp.int32, (shape[0], shape[0]), 1)
    before = jnp.where(c_id < r_id, 1.0, 0.0).astype(BF16)
    rank = _dot(before, chosen.astype(BF16)) + count_ref[...]
    rank1 = jnp.sum(jnp.where(lane == i1, rank, 0.0), axis=-1, keepdims=True)
    rank2 = jnp.sum(jnp.where(lane == i2, rank, 0.0), axis=-1, keepdims=True)
    count_ref[...] += jnp.sum(chosen, axis=0, keepdims=True)
    rec = jnp.zeros(shape, F32)
    for k, val in ((R_E1, i1.astype(F32)), (R_E2, i2.astype(F32)), (R_W1, w1), (R_W2, w2),
                   (R_RANK1, rank1), (R_RANK2, rank2)):
        rec = jnp.where(lane == k, val, rec)
    return rec


def _mix_out_kernel(*refs, route):
    (x_ref, r_ref, a_ref, gr_ref, ga_ref, mod_ref, n2_ref, wr_ref, wa_ref, wo_ref) = refs[:10]
    if route:
        wrt_ref, brt_ref, x1_ref, h2_ref, rec_ref, cnt_ref, count_ref = refs[10:]
    else:
        x1_ref, h2_ref = refs[10:]
    merged = (_sigmoid(gr_ref[...]) * _dot(r_ref[...].astype(BF16), wr_ref[...])
              + _sigmoid(ga_ref[...]) * _dot(a_ref[...], wa_ref[...]))
    mix = _dot(merged.astype(BF16), wo_ref[...])
    x1 = x_ref[...] + mod_ref[2:3, :] * mix
    x1_ref[...] = x1
    h2 = _rms(x1, n2_ref[...]) * (1.0 + mod_ref[4:5, :]) + mod_ref[3:4, :]
    if not route:
        h2_ref[...] = h2.astype(BF16)
        return

    @pl.when(pl.program_id(0) == 0)
    def _():
        count_ref[...] = jnp.zeros_like(count_ref)

    h2_ref[...] = h2
    rec_ref[...] = _route(h2, wrt_ref, brt_ref, count_ref)
    cnt_ref[...] = count_ref[...]


def _mix_out(x, out_r, out_a, gr, ga, mods, l, norm2, wr_bf, wa_bf, wo_bf, seq_len, lat,
             router=None):
    n = x.shape[0]
    tiles_per_seq = seq_len // TM
    if lat:
        mod_idx = lambda i: (l, 1 + i // tiles_per_seq, 0, 0)
    else:
        mod_idx = lambda i: (l, 0, 0, 0)
    row = lambda i: (i, 0)
    wspec = pl.BlockSpec((None, D_MODEL, D_MODEL), lambda i: (l, 0, 0))
    in_specs = [pl.BlockSpec((TM, D_MODEL), row)] * 5 + [
        pl.BlockSpec((None, None, 6, D_MODEL), mod_idx),
        pl.BlockSpec((None, 1, D_MODEL), lambda i: (l, 0, 0)),
        wspec, wspec, wspec]
    args = [x, out_r, out_a, gr, ga, mods, norm2, wr_bf, wa_bf, wo_bf]
    out_specs = [pl.BlockSpec((TM, D_MODEL), row)]
    out_shape = [jax.ShapeDtypeStruct((n, D_MODEL), F32)]
    scratch = []
    if router is None:
        out_specs.append(pl.BlockSpec((TM, D_MODEL), row))
        out_shape.append(jax.ShapeDtypeStruct((n, D_MODEL), BF16))
    else:
        j, wr_t, br_pad = router
        in_specs += [pl.BlockSpec((None, N_EXPERTS, D_MODEL), lambda i: (j, 0, 0)),
                     pl.BlockSpec((None, 1, LANES), lambda i: (j, 0, 0))]
        args += [wr_t, br_pad]
        out_specs += [pl.BlockSpec((TM, D_MODEL), row),
                      pl.BlockSpec((TM, LANES), row),
                      pl.BlockSpec((1, LANES), lambda i: (0, 0))]
        out_shape += [jax.ShapeDtypeStruct((n, D_MODEL), F32),
                      jax.ShapeDtypeStruct((n, LANES), F32),
                      jax.ShapeDtypeStruct((1, LANES), F32)]
        scratch = [pltpu.VMEM((1, LANES), F32)]
    return pl.pallas_call(
        functools.partial(_mix_out_kernel, route=router is not None),
        grid=(n // TM,),
        in_specs=in_specs,
        out_specs=out_specs,
        out_shape=out_shape,
        scratch_shapes=scratch,
        compiler_params=_cparams(("arbitrary",)),
        name=("mix_route" if router is not None else "mix_out") + ("_lat" if lat else "_ctx"),
    )(*args)


def _ffn_kernel(x_ref, h_ref, mod_ref, wg_ref, wu_ref, wd_ref, o_ref, *, f_chunk):
    h = h_ref[...]
    d_ff = wg_ref.shape[1]
    acc = jnp.zeros((h.shape[0], D_MODEL), F32)
    for c in range(d_ff // f_chunk):
        sl = slice(c * f_chunk, (c + 1) * f_chunk)
        act = _silu(_dot(h, wg_ref[:, sl])) * _dot(h, wu_ref[:, sl])
        acc = acc + _dot(act.astype(BF16), wd_ref[sl, :])
    o_ref[...] = x_ref[...] + mod_ref[5:6, :] * acc


def _ffn(x1, h2, mods, l, j, wg_bf, wu_bf, wd_bf, seq_len, lat):
    n = x1.shape[0]
    d_ff = wg_bf.shape[2]
    tiles_per_seq = seq_len // TM
    if lat:
        mod_idx = lambda i: (l, 1 + i // tiles_per_seq, 0, 0)
    else:
        mod_idx = lambda i: (l, 0, 0, 0)
    row = lambda i: (i, 0)
    return pl.pallas_call(
        functools.partial(_ffn_kernel, f_chunk=d_ff // 2),
        grid=(n // TM,),
        in_specs=[pl.BlockSpec((TM, D_MODEL), row), pl.BlockSpec((TM, D_MODEL), row),
                  pl.BlockSpec((None, None, 6, D_MODEL), mod_idx),
                  pl.BlockSpec((None, D_MODEL, d_ff), lambda i: (j, 0, 0)),
                  pl.BlockSpec((None, D_MODEL, d_ff), lambda i: (j, 0, 0)),
                  pl.BlockSpec((None, d_ff, D_MODEL), lambda i: (j, 0, 0))],
        out_specs=pl.BlockSpec((TM, D_MODEL), row),
        out_shape=jax.ShapeDtypeStruct((n, D_MODEL), F32),
        compiler_params=_cparams(("arbitrary",)),
        name="ffn_lat" if lat else "ffn_ctx",
    )(x1, h2, mods, wg_bf, wu_bf, wd_bf)


def _group_layout(rec, counts, n_tiles):
    as_int = lambda k: rec[:, k].astype(jnp.int32)
    cnt = counts[0, :N_EXPERTS].astype(jnp.int32)
    tiles = (cnt + TM_GROUP - 1) // TM_GROUP
    cum = jnp.cumsum(tiles)
    start = (cum - tiles) * TM_GROUP
    experts = jnp.arange(N_EXPERTS, dtype=jnp.int32)

    def pos(e, rank):
        return jnp.sum(jnp.where(e[:, None] == experts[None, :], start[None, :], 0), axis=1) + rank

    pos2 = jnp.stack([pos(as_int(R_E1), as_int(R_RANK1)), pos(as_int(R_E2), as_int(R_RANK2))])
    n_active = cum[N_EXPERTS - 1]
    tile_expert = jnp.sum(jnp.arange(n_tiles, dtype=jnp.int32)[:, None] >= cum[None, :], axis=1)
    tile_expert = jnp.minimum(tile_expert, tile_expert[n_active - 1]).astype(jnp.int32)
    return pos2.astype(jnp.int32), tile_expert, n_active.reshape(1).astype(jnp.int32)


def _row_copy(src, dst, sem):
    return pltpu.make_async_copy(src, dst, sem)


def _dispatch_kernel(pos_ref, h_ref, xs_in_ref, xs_ref, sem):
    del xs_in_ref
    n_chunks = h_ref.shape[0] // DMA_CHUNK

    def drain_chunk():
        rows = pl.ds(0, 2 * DMA_CHUNK)
        _row_copy(xs_ref.at[rows], xs_ref.at[rows], sem).wait()

    def chunk(k, carry):
        def issue(r, c):
            t = k * DMA_CHUNK + r
            _row_copy(h_ref.at[t], xs_ref.at[pos_ref[0, t]], sem).start()
            _row_copy(h_ref.at[t], xs_ref.at[pos_ref[1, t]], sem).start()
            return c

        lax.fori_loop(0, DMA_CHUNK, issue, 0, unroll=DMA_UNROLL)

        @pl.when(k > 0)
        def _():
            drain_chunk()

        return carry

    lax.fori_loop(0, n_chunks, chunk, 0)
    drain_chunk()


def _dispatch(pos2, h2_rows, n_rows):
    tile = h2_rows.shape[1:]
    return pl.pallas_call(
        _dispatch_kernel,
        grid_spec=pltpu.PrefetchScalarGridSpec(
            num_scalar_prefetch=1, grid=(1,),
            in_specs=[pl.BlockSpec(h2_rows.shape, lambda i, p: (0, 0, 0)),
                      pl.BlockSpec(memory_space=pl.ANY)],
            out_specs=pl.BlockSpec(memory_space=pl.ANY),
            scratch_shapes=[pltpu.SemaphoreType.DMA(())]),
        out_shape=jax.ShapeDtypeStruct((n_rows,) + tile, F32),
        input_output_aliases={2: 0},
        compiler_params=_cparams(("arbitrary",)),
        name="moe_dispatch",
    )(pos2, h2_rows, jnp.zeros((n_rows,) + tile, F32))


def _expert_kernel(te_ref, na_ref, xs_ref, wg_ref, wu_ref, wd_ref, ys_ref,
                   xbuf, ybuf, sem_in, sem_out):
    del te_ref
    i = pl.program_id(0)
    last = pl.num_programs(0) - 1
    n_active = na_ref[0]
    slot = i % 2
    n_col = D_MODEL // LANES

    def in_copies(tile, s):
        rows = pl.ds(tile * TM_GROUP, TM_GROUP)
        return [pltpu.make_async_copy(xs_ref.at[rows, c], xbuf.at[s, :, c * LANES:(c + 1) * LANES],
                                      sem_in.at[s]) for c in range(n_col)]

    def out_copies(tile, s):
        rows = pl.ds(tile * TM_GROUP, TM_GROUP)
        return [pltpu.make_async_copy(ybuf.at[s, :, c * LANES:(c + 1) * LANES], ys_ref.at[rows, c],
                                      sem_out.at[s]) for c in range(n_col)]

    @pl.when(i == 0)
    def _():
        for cp in in_copies(0, 0):
            cp.start()

    @pl.when(i + 1 < n_active)
    def _():
        for cp in in_copies(i + 1, 1 - slot):
            cp.start()

    @pl.when(i >= 2)
    def _():
        for cp in out_copies(i - 2, slot):
            cp.wait()

    @pl.when(i < n_active)
    def _():
        for cp in in_copies(i, slot):
            cp.wait()
        xb = xbuf[slot].astype(BF16)
        act = _silu(_dot(xb, wg_ref[...])) * _dot(xb, wu_ref[...])
        ybuf[slot] = _dot(act.astype(BF16), wd_ref[...])

    @pl.when(i >= n_active)
    def _():
        ybuf[slot] = jnp.zeros((TM_GROUP, D_MODEL), F32)

    for cp in out_copies(i, slot):
        cp.start()

    @pl.when(i == last)
    def _():
        for cp in out_copies(i, slot):
            cp.wait()

        @pl.when(i >= 1)
        def _():
            for cp in out_copies(i - 1, 1 - slot):
                cp.wait()


def _experts(tile_expert, n_active, xs, j, wg_bf, wu_bf, wd_bf):
    n_rows = xs.shape[0]
    d_e = wg_bf.shape[3]
    wspec = lambda shape: pl.BlockSpec((None, None) + shape, lambda i, te, na: (j, te[i], 0, 0))
    return pl.pallas_call(
        _expert_kernel,
        grid_spec=pltpu.PrefetchScalarGridSpec(
            num_scalar_prefetch=2, grid=(n_rows // TM_GROUP,),
            in_specs=[pl.BlockSpec(memory_space=pl.ANY),
                      wspec((D_MODEL, d_e)), wspec((D_MODEL, d_e)), wspec((d_e, D_MODEL))],
            out_specs=pl.BlockSpec(memory_space=pl.ANY),
            scratch_shapes=[pltpu.VMEM((2, TM_GROUP, D_MODEL), F32),
                            pltpu.VMEM((2, TM_GROUP, D_MODEL), F32),
                            pltpu.SemaphoreType.DMA((2,)), pltpu.SemaphoreType.DMA((2,))]),
        out_shape=jax.ShapeDtypeStruct(xs.shape, F32),
        compiler_params=_cparams(("arbitrary",)),
        name="moe_experts",
    )(tile_expert, n_active, xs, wg_bf, wu_bf, wd_bf)


def _combine_kernel(pos_ref, y_ref, x_ref, rec_ref, mod_ref, o_ref,
                    a1, a2, b1, b2, sem_a, sem_b):
    i = pl.program_id(0)

    def issue(half_idx, d1, d2, sem):
        base = half_idx * DMA_CHUNK

        def body(r, c):
            _row_copy(y_ref.at[pos_ref[0, base + r]], d1.at[r], sem).start()
            _row_copy(y_ref.at[pos_ref[1, base + r]], d2.at[r], sem).start()
            return c

        lax.fori_loop(0, DMA_CHUNK, body, 0, unroll=DMA_UNROLL)

    def drain(d1, d2, sem):
        rows = pl.ds(0, DMA_CHUNK)
        _row_copy(y_ref.at[rows], d1, sem).wait()
        _row_copy(y_ref.at[rows], d2, sem).wait()

    def finish(half, d1, d2):
        rows = slice(half * DMA_CHUNK, (half + 1) * DMA_CHUNK)
        w1 = rec_ref[rows, R_W1:R_W1 + 1]
        w2 = rec_ref[rows, R_W2:R_W2 + 1]
        for c in range(D_MODEL // LANES):
            sl = slice(c * LANES, (c + 1) * LANES)
            y = w1 * d1[:, c, :] + w2 * d2[:, c, :]
            o_ref[rows, sl] = x_ref[rows, sl] + mod_ref[5:6, sl] * y

    @pl.when(i == 0)
    def _():
        issue(0, a1, a2, sem_a)

    issue(2 * i + 1, b1, b2, sem_b)
    drain(a1, a2, sem_a)
    finish(0, a1, a2)

    @pl.when(i + 1 < pl.num_programs(0))
    def _():
        issue(2 * i + 2, a1, a2, sem_a)

    drain(b1, b2, sem_b)
    finish(1, b1, b2)


def _combine(pos2, ys, x1, rec, mods, l, seq_len, lat):
    n = x1.shape[0]
    tile = ys.shape[1:]
    tm = 2 * DMA_CHUNK
    tiles_per_seq = max(seq_len // tm, 1)
    if lat:
        mod_idx = lambda i, p: (l, 1 + i // tiles_per_seq, 0, 0)
    else:
        mod_idx = lambda i, p: (l, 0, 0, 0)
    row = lambda i, p: (i, 0)
    slot = pltpu.VMEM((DMA_CHUNK,) + tile, F32)
    return pl.pallas_call(
        _combine_kernel,
        grid_spec=pltpu.PrefetchScalarGridSpec(
            num_scalar_prefetch=1, grid=(n // tm,),
            in_specs=[pl.BlockSpec(memory_space=pl.ANY),
                      pl.BlockSpec((tm, D_MODEL), row),
                      pl.BlockSpec((tm, LANES), row),
                      pl.BlockSpec((None, None, 6, D_MODEL), mod_idx)],
            out_specs=pl.BlockSpec((tm, D_MODEL), row),
            scratch_shapes=[slot, slot, slot, slot,
                            pltpu.SemaphoreType.DMA(()), pltpu.SemaphoreType.DMA(())]),
        out_shape=jax.ShapeDtypeStruct((n, D_MODEL), F32),
        compiler_params=_cparams(("arbitrary",)),
        name="moe_combine",
    )(pos2, ys, x1, rec, mods)


def _moe(x1, h2_rows, rec, counts, mods, l, j, wg_bf, wu_bf, wd_bf, seq_len, lat):
    n = x1.shape[0]
    n_tiles = 2 * n // TM_GROUP + N_EXPERTS
    pos2, tile_expert, n_active = _group_layout(rec, counts, n_tiles)
    h2_rows = h2_rows.reshape(n, D_MODEL // LANES, LANES)
    xs = _dispatch(pos2, h2_rows, n_tiles * TM_GROUP)
    ys = _experts(tile_expert, n_active, xs, j, wg_bf, wu_bf, wd_bf)
    return _combine(pos2, ys, x1, rec, mods, l, seq_len, lat)


def _final_norm_kernel(x_ref, g_ref, o_ref):
    o_ref[...] = _rms(x_ref[...], g_ref[...])


def _final_norm(x, g):
    n = x.shape[0]
    row = lambda i: (i, 0)
    return pl.pallas_call(
        _final_norm_kernel,
        grid=(n // TM_FF,),
        in_specs=[pl.BlockSpec((TM_FF, D_MODEL), row), pl.BlockSpec((1, D_MODEL), lambda i: (0, 0))],
        out_specs=pl.BlockSpec((TM_FF, D_MODEL), row),
        out_shape=jax.ShapeDtypeStruct((n, D_MODEL), F32),
        compiler_params=_cparams(("arbitrary",)),
        name="final_norm",
    )(x, g)


def _rope_tables(t_len):
    t = jnp.arange(t_len)
    pos = jnp.stack([t // GRID_W, t % GRID_W], axis=-1).astype(F32)
    inv = ROPE_THETA ** (-jnp.arange(0, AXIS_DIM, 2, dtype=F32) / AXIS_DIM)
    ang = pos[..., None] * inv
    cos, sin = jnp.cos(ang), jnp.sin(ang)
    c = jnp.concatenate([cos[:, 0], cos[:, 0], cos[:, 1], cos[:, 1]], axis=-1)
    s = jnp.concatenate([-sin[:, 0], sin[:, 0], -sin[:, 1], sin[:, 1]], axis=-1)
    return c, s


def _gate_weights(wa, wi):
    depth = wa.shape[0]
    groups = D_RNN // LANES

    def pair(w):
        w = w.reshape(depth, 2, groups, 2, RNN_BLOCK, RNN_BLOCK)
        z = jnp.zeros_like(w[:, :, :, 0])
        top = jnp.concatenate([w[:, :, :, 0], z], axis=-1)
        bot = jnp.concatenate([z, w[:, :, :, 1]], axis=-1)
        return jnp.concatenate([top, bot], axis=-2)

    w = jnp.concatenate([pair(wa), pair(wi)], axis=-1)
    return (0.5 * jnp.concatenate([w[:, 0], w[:, 1]], axis=-1)).astype(BF16)


def kernel(x_prompt, x_sample, cache_k, cache_v, state_rnn, c, c_ctx, w_mod, b_mod, norm1, norm2, w_in, conv_w, conv_b, lru_wa, lru_ba, lru_wi, lru_bi, lru_lambda, q_norm, k_norm, w_rnn_out, w_attn_out, w_out, ff_gate, ff_up, ff_down, router_w, router_b, exp_gate, exp_up, exp_down, final_norm):
    batch, seq, _ = x_prompt.shape
    dec_batch, dec_seq, _ = x_sample.shape
    depth = w_mod.shape[0]
    past = cache_k.shape[2]
    assert dec_batch + 1 <= MOD_ROWS

    cond8 = jnp.zeros((MOD_ROWS, D_MODEL), F32).at[0].set(c_ctx).at[1:1 + dec_batch].set(c)
    mods = _modulation(cond8, w_mod, b_mod).reshape(depth, MOD_ROWS, 6, D_MODEL)

    bf = lambda w: w.astype(BF16)
    w_in_bf, wr_bf, wa_bf, wo_bf = bf(w_in), bf(w_rnn_out), bf(w_attn_out), bf(w_out)
    ffg_bf, ffu_bf, ffd_bf = bf(ff_gate), bf(ff_up), bf(ff_down)
    eg_bf, eu_bf, ed_bf = bf(exp_gate), bf(exp_up), bf(exp_down)
    gate_w = _gate_weights(lru_wa, lru_wi)
    n_moe = router_w.shape[0]
    wr_t = jnp.swapaxes(router_w, 1, 2)
    br_pad = jnp.zeros((n_moe, 1, LANES), F32).at[:, 0, :N_EXPERTS].set(router_b)

    norm1_3 = norm1.reshape(depth, 1, D_MODEL)
    norm2_3 = norm2.reshape(depth, 1, D_MODEL)
    qn3 = q_norm.reshape(depth, 1, HEAD_DIM)
    kn3 = k_norm.reshape(depth, 1, HEAD_DIM)
    conv_b3 = conv_b.reshape(depth, 1, D_RNN)
    ba4 = (0.5 * lru_ba).reshape(depth, 2, 1, D_RNN)
    bi4 = (0.5 * lru_bi).reshape(depth, 2, 1, D_RNN)
    lam4 = lru_lambda.reshape(depth, 2, 1, D_RNN)
    state5 = state_rnn.reshape(dec_batch, depth, 2, 1, D_RNN)
    cache_k4 = cache_k.reshape(dec_batch, depth, past, KV_DIM)
    cache_v4 = cache_v.reshape(dec_batch, depth, past, KV_DIM)
    rope_tabs = _rope_tables(dec_seq)

    def run_stream(x, lat):
        seq_len = dec_seq if lat else seq
        ks, vs, ss = [], [], []
        for l in range(depth):
            xr, yr, q, kn, kr, v, gr, ga = _in_proj(
                x, mods, l, norm1_3, w_in_bf, qn3, kn3, rope_tabs if lat else None, seq_len)
            out_r, end_f, end_b = _rnn(xr, yr, l, conv_w, conv_b3, gate_w, ba4, bi4, lam4,
                                       state5 if lat else None, seq_len)
            if lat:
                out_a = _attn_lat(q, kr, v, cache_k4, cache_v4, l, seq_len)
            else:
                out_a = _attn_ctx(q, kr, v, seq_len)
                ks.append(kn)
                vs.append(v)
                ss.append(jnp.stack([end_f, end_b], axis=1))
            moe_layer = l % 2 == 1
            mix_args = (x, out_r, out_a, gr, ga, mods, l, norm2_3, wr_bf, wa_bf, wo_bf, seq_len, lat)
            if moe_layer:
                x1, h2_rows, rec, counts = _mix_out(*mix_args, router=(l // 2, wr_t, br_pad))
                x = _moe(x1, h2_rows, rec, counts, mods, l, l // 2, eg_bf, eu_bf, ed_bf,
                         seq_len, lat)
            else:
                x1, h2 = _mix_out(*mix_args)
                x = _ffn(x1, h2, mods, l, l // 2, ffg_bf, ffu_bf, ffd_bf, seq_len, lat)
        return _final_norm(x, final_norm.reshape(1, D_MODEL)), ks, vs, ss

    y_ctx, ks, vs, ss = run_stream(x_prompt.reshape(batch * seq, D_MODEL), False)
    y_lat, _, _, _ = run_stream(x_sample.reshape(dec_batch * dec_seq, D_MODEL), True)

    new_k = jnp.stack([k.reshape(batch, seq, N_KV_HEADS, HEAD_DIM) for k in ks], axis=1)
    new_v = jnp.stack([v.reshape(batch, seq, N_KV_HEADS, HEAD_DIM) for v in vs], axis=1)
    new_state = jnp.stack(ss, axis=1)
    return (y_ctx.reshape(batch, seq, D_MODEL), y_lat.reshape(dec_batch, dec_seq, D_MODEL),
            new_k, new_v, new_state)
```

```python
import functools

import jax
import jax.numpy as jnp
from jax import lax
from jax.experimental import pallas as pl
from jax.experimental.pallas import tpu as pltpu

F32 = jnp.float32
BF16 = jnp.bfloat16

D_MODEL = 1024
D_RNN = 1024
N_RNN_BLOCKS = 16
RNN_BLOCK = D_RNN // N_RNN_BLOCKS
CONV_W = 4
CONV_LEFT = 2
LRU_C = 8.0
N_HEADS = 8
N_KV_HEADS = 2
HEAD_DIM = 128
AXIS_DIM = HEAD_DIM // 2
ROPE_THETA = 10000.0
GRID_W = 64
Q_DIM = N_HEADS * HEAD_DIM
KV_DIM = N_KV_HEADS * HEAD_DIM
N_EXPERTS = 8
EPS = 1e-6
LOG2_E = 1.4426950408889634
RSQRT_FLOOR = 1e-30
D_IN = 2 * D_RNN + Q_DIM + 2 * KV_DIM + 2 * D_MODEL
O_XR, O_YR, O_Q = 0, D_RNN, 2 * D_RNN
O_K = O_Q + Q_DIM
O_V = O_K + KV_DIM
O_GR = O_V + KV_DIM
O_GA = O_GR + D_MODEL

LANES = 128
SUBLANES = 8
MOD_ROWS = 8
VMEM_LIMIT = 56 * 1024 * 1024

TM = 256
RNN_T = 256
RNN_SEG = SUBLANES
RNN_C = 256
RNN_TCH = 32
RNN_SCAN_UNROLL = 8
TQ = 256
TM_FF = 512
TM_GROUP = 256
DMA_CHUNK = 256
DMA_UNROLL = 8


def _cparams(sem):
    return pltpu.CompilerParams(dimension_semantics=sem, vmem_limit_bytes=VMEM_LIMIT)


def _sigmoid(x):
    return 1.0 / (1.0 + jnp.exp(-x))


def _silu(x):
    return x * _sigmoid(x)


def _gelu_tanh(x):
    return 0.5 * x * (1.0 + jnp.tanh(0.7978845608028654 * (x + 0.044715 * (x * x * x))))


def _rms(x, g):
    ms = jnp.mean(x * x, axis=-1, keepdims=True)
    return x * lax.rsqrt(ms + EPS) * g


def _dot(a, b):
    return jnp.dot(a, b, preferred_element_type=F32)


def _mod_kernel(cond_ref, w_ref, b_ref, o_ref):
    s = _silu(cond_ref[...]).astype(BF16)
    o_ref[...] = _dot(s, w_ref[...].astype(BF16)) + b_ref[...]


def _modulation(cond8, w_mod, b_mod):
    depth = w_mod.shape[0]
    tn = 1536
    return pl.pallas_call(
        _mod_kernel,
        grid=(depth, 6 * D_MODEL // tn),
        in_specs=[
            pl.BlockSpec((MOD_ROWS, D_MODEL), lambda l, j: (0, 0)),
            pl.BlockSpec((None, D_MODEL, tn), lambda l, j: (l, 0, j)),
            pl.BlockSpec((None, 1, tn), lambda l, j: (l, 0, j)),
        ],
        out_specs=pl.BlockSpec((None, MOD_ROWS, tn), lambda l, j: (l, 0, j)),
        out_shape=jax.ShapeDtypeStruct((depth, MOD_ROWS, 6 * D_MODEL), F32),
        compiler_params=_cparams(("arbitrary", "arbitrary")),
        name="adaln_mod",
    )(cond8, w_mod, b_mod.reshape(depth, 1, 6 * D_MODEL))


def _swap_halves(x):
    lane = lax.broadcasted_iota(jnp.int32, x.shape, 1)
    return jnp.where((lane % AXIS_DIM) < AXIS_DIM // 2,
                     pltpu.roll(x, LANES - AXIS_DIM // 2, axis=1),
                     pltpu.roll(x, AXIS_DIM // 2, axis=1))


def _in_proj_kernel(*refs, rope):
    if rope:
        (x_ref, mod_ref, n1_ref, w_ref, qn_ref, kn_ref, c_ref, s_ref,
         xr_ref, yr_ref, q_ref, kno_ref, kro_ref, v_ref, gr_ref, ga_ref) = refs
    else:
        (x_ref, mod_ref, n1_ref, w_ref, qn_ref, kn_ref,
         xr_ref, yr_ref, q_ref, kno_ref, kro_ref, v_ref, gr_ref, ga_ref) = refs
    x = x_ref[...]
    h = _rms(x, n1_ref[...]) * (1.0 + mod_ref[1:2, :]) + mod_ref[0:1, :]
    hb = h.astype(BF16)

    xr_ref[...] = _dot(hb, w_ref[:, O_XR:O_XR + D_RNN])
    yr_ref[...] = _dot(hb, w_ref[:, O_YR:O_YR + D_RNN])
    gr_ref[...] = _dot(hb, w_ref[:, O_GR:O_GR + D_MODEL])
    ga_ref[...] = _dot(hb, w_ref[:, O_GA:O_GA + D_MODEL])
    v_ref[...] = _dot(hb, w_ref[:, O_V:O_V + KV_DIM])

    if rope:
        cs, sn = c_ref[...], s_ref[...]
    zq = _dot(hb, w_ref[:, O_Q:O_Q + Q_DIM])
    for hd in range(N_HEADS):
        t = _rms(zq[:, hd * HEAD_DIM:(hd + 1) * HEAD_DIM], qn_ref[...])
        if rope:
            t = t * cs + _swap_halves(t) * sn
        q_ref[:, hd * HEAD_DIM:(hd + 1) * HEAD_DIM] = t.astype(BF16)
    zk = _dot(hb, w_ref[:, O_K:O_K + KV_DIM])
    for hd in range(N_KV_HEADS):
        t = _rms(zk[:, hd * HEAD_DIM:(hd + 1) * HEAD_DIM], kn_ref[...])
        kno_ref[:, hd * HEAD_DIM:(hd + 1) * HEAD_DIM] = t
        if rope:
            t = t * cs + _swap_halves(t) * sn
        kro_ref[:, hd * HEAD_DIM:(hd + 1) * HEAD_DIM] = t.astype(BF16)


def _in_proj(x, mods, l, norm1, w_in_bf, q_norm, k_norm, rope_tabs, seq_len):
    n = x.shape[0]
    tiles_per_seq = seq_len // TM
    rope = rope_tabs is not None
    if rope:
        mod_idx = lambda i: (l, 1 + i // tiles_per_seq, 0, 0)
    else:
        mod_idx = lambda i: (l, 0, 0, 0)
    row = lambda i: (i, 0)
    in_specs = [
        pl.BlockSpec((TM, D_MODEL), row),
        pl.BlockSpec((None, None, 6, D_MODEL), mod_idx),
        pl.BlockSpec((None, 1, D_MODEL), lambda i: (l, 0, 0)),
        pl.BlockSpec((None, D_MODEL, D_IN), lambda i: (l, 0, 0)),
        pl.BlockSpec((None, 1, HEAD_DIM), lambda i: (l, 0, 0)),
        pl.BlockSpec((None, 1, HEAD_DIM), lambda i: (l, 0, 0)),
    ]
    args = [x, mods, norm1, w_in_bf, q_norm, k_norm]
    if rope:
        in_specs += [pl.BlockSpec((TM, HEAD_DIM), lambda i: (i % tiles_per_seq, 0))] * 2
        args += list(rope_tabs)
    widths = [(D_RNN, F32), (D_RNN, F32), (Q_DIM, BF16), (KV_DIM, F32), (KV_DIM, BF16),
              (KV_DIM, F32), (D_MODEL, F32), (D_MODEL, F32)]
    return pl.pallas_call(
        functools.partial(_in_proj_kernel, rope=rope),
        grid=(n // TM,),
        in_specs=in_specs,
        out_specs=[pl.BlockSpec((TM, w), row) for w, _ in widths],
        out_shape=[jax.ShapeDtypeStruct((n, w), dt) for w, dt in widths],
        compiler_params=_cparams(("arbitrary",)),
        name="in_proj_lat" if rope else "in_proj_ctx",
    )(*args)


def _rnn_kernel(*refs, chained):
    refs = list(refs)
    xr_ref, yr_ref, cw_ref, cb_ref, gw_ref, ba_ref, bi_ref, lam_ref = refs[:8]
    refs = refs[8:]
    h0_ref = refs.pop(0) if chained else None
    out_ref, sf_ref, sb_ref, xext, ybuf, obuf, a_f, b_f, a_b, b_b, sem_in, sem_out = refs

    g = pl.program_id(0)
    cb = pl.program_id(1)
    n_cb = pl.num_programs(1)
    step = g * n_cb + cb
    n_steps = pl.num_programs(0) * n_cb
    slot = step % 2

    def hbm_window(kk, s):
        rows = pl.ds(pl.multiple_of(((kk // n_cb) * RNN_SEG + s) * RNN_T, RNN_T), RNN_T)
        cols = pl.ds(pl.multiple_of((kk % n_cb) * RNN_C, RNN_C), RNN_C)
        return rows, cols

    def in_copies(kk, sl_):
        cps = []
        for s in range(RNN_SEG):
            win = hbm_window(kk, s)
            cps.append(pltpu.make_async_copy(
                xr_ref.at[win], xext.at[sl_, pl.ds(CONV_LEFT, RNN_T), s], sem_in.at[sl_]))
            cps.append(pltpu.make_async_copy(yr_ref.at[win], ybuf.at[sl_, :, s], sem_in.at[sl_]))
        return cps

    def out_copies(kk, sl_):
        return [pltpu.make_async_copy(obuf.at[sl_, :, s], out_ref.at[hbm_window(kk, s)],
                                      sem_out.at[sl_]) for s in range(RNN_SEG)]

    @pl.when(step == 0)
    def _():
        for cp in in_copies(0, 0):
            cp.start()

    @pl.when(step + 1 < n_steps)
    def _():
        for cp in in_copies(step + 1, 1 - slot):
            cp.start()

    for cp in in_copies(step, slot):
        cp.wait()

    seg = lax.broadcasted_iota(jnp.int32, (RNN_SEG, RNN_C), 0)
    zero_row = jnp.zeros((RNN_SEG, RNN_C), F32)
    if chained:
        xext[slot, 0] = jnp.where(seg == 0, zero_row, pltpu.roll(xext[slot, RNN_T], 1, axis=0))
        xext[slot, 1] = jnp.where(seg == 0, zero_row, pltpu.roll(xext[slot, RNN_T + 1], 1, axis=0))
        xext[slot, RNN_T + CONV_LEFT] = jnp.where(
            seg == RNN_SEG - 1, zero_row, pltpu.roll(xext[slot, CONV_LEFT], RNN_SEG - 1, axis=0))
    else:
        xext[slot, 0] = zero_row
        xext[slot, 1] = zero_row
        xext[slot, RNN_T + CONV_LEFT] = zero_row

    scans = ((a_f, b_f), (a_b, b_b))
    rows2d = RNN_TCH * RNN_SEG

    def gate_chunk(tc, carry):
        t0 = pl.multiple_of(tc * RNN_TCH, RNN_TCH)
        for j in range(RNN_C // LANES):
            sl = slice(j * LANES, (j + 1) * LANES)
            xc = jnp.broadcast_to(cb_ref[:, sl].reshape(1, 1, LANES), (RNN_TCH, RNN_SEG, LANES))
            for k in range(CONV_W):
                xc = xc + (xext[slot, pl.ds(t0 + k, RNN_TCH), :, sl]
                           * cw_ref[k:k + 1, sl].reshape(1, 1, LANES))
            xc = xc.reshape(rows2d, LANES)
            pre = _dot(xc.astype(BF16), gw_ref[j])
            for d, (a_s, b_s) in enumerate(scans):
                lam = lam_ref[d, :, sl]
                log2_decay = (-LRU_C * LOG2_E) * (jnp.maximum(-lam, 0.0)
                                                  + jnp.log(1.0 + jnp.exp(-jnp.abs(lam))))
                r = 0.5 * jnp.tanh(pre[:, 2 * d * LANES:(2 * d + 1) * LANES]
                                   + ba_ref[d, :, sl]) + 0.5
                gi = 0.5 * jnp.tanh(pre[:, (2 * d + 1) * LANES:(2 * d + 2) * LANES]
                                    + bi_ref[d, :, sl]) + 0.5
                a = jnp.exp2(r * log2_decay)
                y = 1.0 - a * a
                b = (y * lax.rsqrt(jnp.maximum(y, RSQRT_FLOOR))) * (gi * xc)
                a_s[pl.ds(t0, RNN_TCH), :, sl] = a.reshape(RNN_TCH, RNN_SEG, LANES)
                b_s[pl.ds(t0, RNN_TCH), :, sl] = b.reshape(RNN_TCH, RNN_SEG, LANES)
        return carry

    lax.fori_loop(0, RNN_T // RNN_TCH, gate_chunk, 0)

    def scan_step(t, carry):
        hf, hb, pf, pb = carry
        tb = RNN_T - 1 - t
        af_t = a_f[t]
        hf = af_t * hf + b_f[t]
        b_f[t] = hf
        ab_t = a_b[tb]
        hb = ab_t * hb + b_b[tb]
        b_b[tb] = hb
        if chained:
            pf = af_t * pf
            a_f[t] = pf
            pb = ab_t * pb
            a_b[tb] = pb
        return hf, hb, pf, pb

    ones = jnp.ones((RNN_SEG, RNN_C), F32)
    hf, hb, pf, pb = lax.fori_loop(0, RNN_T, scan_step, (zero_row, zero_row, ones, ones),
                                   unroll=RNN_SCAN_UNROLL)

    if chained:
        st_f = jnp.broadcast_to(h0_ref[0], (RNN_SEG, RNN_C))
        st_b = jnp.broadcast_to(h0_ref[1], (RNN_SEG, RNN_C))
        in_f, in_b = st_f, st_b
        for _ in range(RNN_SEG - 1):
            in_f = jnp.where(seg == 0, st_f, pltpu.roll(pf * in_f + hf, 1, axis=0))
            in_b = jnp.where(seg == RNN_SEG - 1, st_b,
                             pltpu.roll(pb * in_b + hb, RNN_SEG - 1, axis=0))
        sf_ref[...] = pf * in_f + hf
        sb_ref[...] = pb * in_b + hb
    else:
        sf_ref[...] = hf
        sb_ref[...] = hb

    @pl.when(step >= 2)
    def _():
        for cp in out_copies(step - 2, slot):
            cp.wait()

    def finish_chunk(tc, carry):
        rows = pl.ds(pl.multiple_of(tc * RNN_TCH, RNN_TCH), RNN_TCH)
        h_f, h_b = b_f[rows], b_b[rows]
        if chained:
            h_f = a_f[rows] * in_f.reshape(1, RNN_SEG, RNN_C) + h_f
            h_b = a_b[rows] * in_b.reshape(1, RNN_SEG, RNN_C) + h_b
        obuf[slot, rows] = (h_f + h_b) * _gelu_tanh(ybuf[slot, rows])
        return carry

    lax.fori_loop(0, RNN_T // RNN_TCH, finish_chunk, 0)

    for cp in out_copies(step, slot):
        cp.start()

    @pl.when(step == n_steps - 1)
    def _():
        for cp in out_copies(step, slot):
            cp.wait()

        @pl.when(step >= 1)
        def _():
            for cp in out_copies(step - 1, 1 - slot):
                cp.wait()


def _rnn(xr, yr, l, conv_w, conv_b, gate_w, lru_ba, lru_bi, lru_lambda, state5, seq_len):
    n = xr.shape[0]
    chained = state5 is not None
    group_len = RNN_SEG * RNN_T
    assert n % group_len == 0 and seq_len in ((group_len,) if chained else (RNN_T,))
    n_groups = n // group_len
    lanes_per_step = RNN_C // LANES
    vec = pl.BlockSpec((None, 2, 1, RNN_C), lambda g, c: (l, 0, 0, c))
    in_specs = [
        pl.BlockSpec(memory_space=pl.ANY),
        pl.BlockSpec(memory_space=pl.ANY),
        pl.BlockSpec((None, CONV_W, RNN_C), lambda g, c: (l, 0, c)),
        pl.BlockSpec((None, 1, RNN_C), lambda g, c: (l, 0, c)),
        pl.BlockSpec((None, lanes_per_step, LANES, 4 * LANES), lambda g, c: (l, c, 0, 0)),
        vec, vec, vec,
    ]
    args = [xr, yr, conv_w, conv_b, gate_w, lru_ba, lru_bi, lru_lambda]
    if chained:
        in_specs.append(pl.BlockSpec((None, None, 2, 1, RNN_C), lambda g, c: (g, l, 0, 0, c)))
        args.append(state5)
    state_spec = pl.BlockSpec((RNN_SEG, RNN_C), lambda g, c: (g, c))
    state_shape = jax.ShapeDtypeStruct((n_groups * RNN_SEG, D_RNN), F32)
    work = lambda t: pltpu.VMEM((t, RNN_SEG, RNN_C), F32)
    return pl.pallas_call(
        functools.partial(_rnn_kernel, chained=chained),
        grid=(n_groups, D_RNN // RNN_C),
        in_specs=in_specs,
        out_specs=[pl.BlockSpec(memory_space=pl.ANY), state_spec, state_spec],
        out_shape=[jax.ShapeDtypeStruct((n, D_RNN), F32), state_shape, state_shape],
        scratch_shapes=[
            pltpu.VMEM((2, RNN_T + CONV_W - 1, RNN_SEG, RNN_C), F32),
            pltpu.VMEM((2, RNN_T, RNN_SEG, RNN_C), F32),
            pltpu.VMEM((2, RNN_T, RNN_SEG, RNN_C), F32),
            work(RNN_T), work(RNN_T), work(RNN_T), work(RNN_T),
            pltpu.SemaphoreType.DMA((2,)), pltpu.SemaphoreType.DMA((2,)),
        ],
        compiler_params=_cparams(("arbitrary", "arbitrary")),
        name="rnn_lat" if chained else "rnn_ctx",
    )(*args)


def _attend_heads(q_ref, k_all, v_all, o_ref, stack_heads):
    exp2_scale = (1.0 / (HEAD_DIM ** 0.5)) * LOG2_E
    group = N_HEADS // N_KV_HEADS
    tq = q_ref.shape[0]
    for kh in range(N_KV_HEADS):
        k = k_all[:, kh * HEAD_DIM:(kh + 1) * HEAD_DIM]
        v = v_all[:, kh * HEAD_DIM:(kh + 1) * HEAD_DIM]
        heads = [slice((kh * group + g) * HEAD_DIM, (kh * group + g + 1) * HEAD_DIM)
                 for g in range(group)]
        stacks = [heads] if stack_heads else [[sl] for sl in heads]
        for stack in stacks:
            q = jnp.concatenate([q_ref[:, sl] for sl in stack], axis=0)
            s = lax.dot_general(q, k, (((1,), (1,)), ((), ())), preferred_element_type=F32)
            m = jnp.max(s, axis=-1, keepdims=True)
            p = jnp.exp2((s - m) * exp2_scale)
            denom = jnp.sum(p, axis=-1, keepdims=True)
            o = (_dot(p.astype(BF16), v) / denom).astype(BF16)
            for g, sl in enumerate(stack):
                o_ref[:, sl] = o[g * tq:(g + 1) * tq]


def _attn_ctx_kernel(q_ref, k_ref, v_ref, o_ref):
    _attend_heads(q_ref, k_ref[...], v_ref[...].astype(BF16), o_ref, stack_heads=True)


def _attn_ctx(q, kr, v, seq_len):
    n = q.shape[0]
    row = lambda b: (b, 0)
    return pl.pallas_call(
        _attn_ctx_kernel,
        grid=(n // seq_len,),
        in_specs=[pl.BlockSpec((seq_len, Q_DIM), row),
                  pl.BlockSpec((seq_len, KV_DIM), row),
                  pl.BlockSpec((seq_len, KV_DIM), row)],
        out_specs=pl.BlockSpec((seq_len, Q_DIM), row),
        out_shape=jax.ShapeDtypeStruct((n, Q_DIM), BF16),
        compiler_params=_cparams(("arbitrary",)),
        name="attn_ctx",
    )(q, kr, v)


def _attn_lat_kernel(q_ref, kc_ref, vc_ref, kl_ref, vl_ref, o_ref, k_all, v_all, *, past):
    @pl.when(pl.program_id(1) == 0)
    def _():
        k_all[0:past, :] = kc_ref[...].astype(BF16)
        k_all[past:, :] = kl_ref[...]
        v_all[0:past, :] = vc_ref[...].astype(BF16)
        v_all[past:, :] = vl_ref[...].astype(BF16)

    _attend_heads(q_ref, k_all[...], v_all[...], o_ref, stack_heads=False)


def _attn_lat(q, kr, v, cache_k4, cache_v4, l, seq_len):
    n = q.shape[0]
    past = cache_k4.shape[2]
    nq = seq_len // TQ
    return pl.pallas_call(
        functools.partial(_attn_lat_kernel, past=past),
        grid=(n // seq_len, nq),
        in_specs=[pl.BlockSpec((TQ, Q_DIM), lambda b, j: (b * nq + j, 0)),
                  pl.BlockSpec((None, None, past, KV_DIM), lambda b, j: (b, l, 0, 0)),
                  pl.BlockSpec((None, None, past, KV_DIM), lambda b, j: (b, l, 0, 0)),
                  pl.BlockSpec((seq_len, KV_DIM), lambda b, j: (b, 0)),
                  pl.BlockSpec((seq_len, KV_DIM), lambda b, j: (b, 0))],
        out_specs=pl.BlockSpec((TQ, Q_DIM), lambda b, j: (b * nq + j, 0)),
        out_shape=jax.ShapeDtypeStruct((n, Q_DIM), BF16),
        scratch_shapes=[pltpu.VMEM((past + seq_len, KV_DIM), BF16),
                        pltpu.VMEM((past + seq_len, KV_DIM), BF16)],
        compiler_params=_cparams(("arbitrary", "arbitrary")),
        name="attn_lat",
    )(q, cache_k4, cache_v4, kr, v)


R_E1, R_E2, R_W1, R_W2, R_RANK1, R_RANK2 = range(6)


def _route(h2, wrt_ref, brt_ref, count_ref):
    shape = (h2.shape[0], LANES)
    lane = lax.broadcasted_iota(jnp.int32, shape, 1)
    neg = jnp.float32(-jnp.inf)
    logits = jnp.full(shape, neg, F32)
    for e in range(N_EXPERTS):
        prod = h2 * wrt_ref[e:e + 1, :]
        part = prod[:, 0:LANES]
        for c in range(1, D_MODEL // LANES):
            part = part + prod[:, c * LANES:(c + 1) * LANES]
        logits = jnp.where(lane == e, jnp.sum(part, axis=-1, keepdims=True), logits)
    logits = logits + brt_ref[...]
    v1 = jnp.max(logits, axis=-1, keepdims=True)
    i1 = jnp.min(jnp.where(logits == v1, lane, LANES), axis=-1, keepdims=True)
    rest = jnp.where(lane == i1, neg, logits)
    v2 = jnp.max(rest, axis=-1, keepdims=True)
    i2 = jnp.min(jnp.where(rest == v2, lane, LANES), axis=-1, keepdims=True)
    e2 = jnp.exp(v2 - v1)
    w1 = 1.0 / (1.0 + e2)
    w2 = e2 / (1.0 + e2)
    chosen = jnp.where((lane == i1) | (lane == i2), 1.0, 0.0)
    r_id = lax.broadcasted_iota(jnp.int32, (shape[0], shape[0]), 0)
    c_id = lax.broadcasted_iota(jnp.int32, (shape[0], shape[0]), 1)
    before = jnp.where(c_id < r_id, 1.0, 0.0).astype(BF16)
    rank = _dot(before, chosen.astype(BF16)) + count_ref[...]
    rank1 = jnp.sum(jnp.where(lane == i1, rank, 0.0), axis=-1, keepdims=True)
    rank2 = jnp.sum(jnp.where(lane == i2, rank, 0.0), axis=-1, keepdims=True)
    count_ref[...] += jnp.sum(chosen, axis=0, keepdims=True)
    rec = jnp.zeros(shape, F32)
    for k, val in ((R_E1, i1.astype(F32)), (R_E2, i2.astype(F32)), (R_W1, w1), (R_W2, w2),
                   (R_RANK1, rank1), (R_RANK2, rank2)):
        rec = jnp.where(lane == k, val, rec)
    return rec


def _mix_out_kernel(*refs, route):
    (x_ref, r_ref, a_ref, gr_ref, ga_ref, mod_ref, n2_ref, wr_ref, wa_ref, wo_ref) = refs[:10]
    if route:
        wrt_ref, brt_ref, x1_ref, h2_ref, rec_ref, cnt_ref, count_ref, hbuf, hsem = refs[10:]
    else:
        x1_ref, h2_ref = refs[10:]
    merged = (_sigmoid(gr_ref[...]) * _dot(r_ref[...].astype(BF16), wr_ref[...])
              + _sigmoid(ga_ref[...]) * _dot(a_ref[...], wa_ref[...]))
    mix = _dot(merged.astype(BF16), wo_ref[...])
    x1 = x_ref[...] + mod_ref[2:3, :] * mix
    x1_ref[...] = x1
    h2 = _rms(x1, n2_ref[...]) * (1.0 + mod_ref[4:5, :]) + mod_ref[3:4, :]
    if not route:
        h2_ref[...] = h2.astype(BF16)
        return

    @pl.when(pl.program_id(0) == 0)
    def _():
        count_ref[...] = jnp.zeros_like(count_ref)

    rec_ref[...] = _route(h2, wrt_ref, brt_ref, count_ref)
    cnt_ref[...] = count_ref[...]

    step = pl.program_id(0)
    slot = step % 2

    def h_copies(k, s):
        rows = pl.ds(k * TM, TM)
        return [pltpu.make_async_copy(hbuf.at[s, :, c * LANES:(c + 1) * LANES], h2_ref.at[rows, c],
                                      hsem.at[s]) for c in range(D_MODEL // LANES)]

    @pl.when(step >= 2)
    def _():
        for cp in h_copies(step - 2, slot):
            cp.wait()

    hbuf[slot] = h2
    for cp in h_copies(step, slot):
        cp.start()

    @pl.when(step == pl.num_programs(0) - 1)
    def _():
        for cp in h_copies(step, slot):
            cp.wait()

        @pl.when(step >= 1)
        def _():
            for cp in h_copies(step - 1, 1 - slot):
                cp.wait()


def _mix_out(x, out_r, out_a, gr, ga, mods, l, norm2, wr_bf, wa_bf, wo_bf, seq_len, lat,
             router=None):
    n = x.shape[0]
    tiles_per_seq = seq_len // TM
    if lat:
        mod_idx = lambda i: (l, 1 + i // tiles_per_seq, 0, 0)
    else:
        mod_idx = lambda i: (l, 0, 0, 0)
    row = lambda i: (i, 0)
    wspec = pl.BlockSpec((None, D_MODEL, D_MODEL), lambda i: (l, 0, 0))
    in_specs = [pl.BlockSpec((TM, D_MODEL), row)] * 5 + [
        pl.BlockSpec((None, None, 6, D_MODEL), mod_idx),
        pl.BlockSpec((None, 1, D_MODEL), lambda i: (l, 0, 0)),
        wspec, wspec, wspec]
    args = [x, out_r, out_a, gr, ga, mods, norm2, wr_bf, wa_bf, wo_bf]
    out_specs = [pl.BlockSpec((TM, D_MODEL), row)]
    out_shape = [jax.ShapeDtypeStruct((n, D_MODEL), F32)]
    scratch = []
    if router is None:
        out_specs.append(pl.BlockSpec((TM, D_MODEL), row))
        out_shape.append(jax.ShapeDtypeStruct((n, D_MODEL), BF16))
    else:
        j, wr_t, br_pad = router
        in_specs += [pl.BlockSpec((None, N_EXPERTS, D_MODEL), lambda i: (j, 0, 0)),
                     pl.BlockSpec((None, 1, LANES), lambda i: (j, 0, 0))]
        args += [wr_t, br_pad]
        out_specs += [pl.BlockSpec(memory_space=pl.ANY),
                      pl.BlockSpec((TM, LANES), row),
                      pl.BlockSpec((1, LANES), lambda i: (0, 0))]
        out_shape += [jax.ShapeDtypeStruct((n, D_MODEL // LANES, LANES), F32),
                      jax.ShapeDtypeStruct((n, LANES), F32),
                      jax.ShapeDtypeStruct((1, LANES), F32)]
        scratch = [pltpu.VMEM((1, LANES), F32), pltpu.VMEM((2, TM, D_MODEL), F32),
                   pltpu.SemaphoreType.DMA((2,))]
    return pl.pallas_call(
        functools.partial(_mix_out_kernel, route=router is not None),
        grid=(n // TM,),
        in_specs=in_specs,
        out_specs=out_specs,
        out_shape=out_shape,
        scratch_shapes=scratch,
        compiler_params=_cparams(("arbitrary",)),
        name=("mix_route" if router is not None else "mix_out") + ("_lat" if lat else "_ctx"),
    )(*args)


def _ffn_kernel(x_ref, h_ref, mod_ref, wg_ref, wu_ref, wd_ref, o_ref, *, f_chunk):
    h = h_ref[...]
    d_ff = wg_ref.shape[1]
    acc = jnp.zeros((h.shape[0], D_MODEL), F32)
    for c in range(d_ff // f_chunk):
        sl = slice(c * f_chunk, (c + 1) * f_chunk)
        act = _silu(_dot(h, wg_ref[:, sl])) * _dot(h, wu_ref[:, sl])
        acc = acc + _dot(act.astype(BF16), wd_ref[sl, :])
    o_ref[...] = x_ref[...] + mod_ref[5:6, :] * acc


def _ffn(x1, h2, mods, l, j, wg_bf, wu_bf, wd_bf, seq_len, lat):
    n = x1.shape[0]
    d_ff = wg_bf.shape[2]
    tiles_per_seq = seq_len // TM
    if lat:
        mod_idx = lambda i: (l, 1 + i // tiles_per_seq, 0, 0)
    else:
        mod_idx = lambda i: (l, 0, 0, 0)
    row = lambda i: (i, 0)
    return pl.pallas_call(
        functools.partial(_ffn_kernel, f_chunk=d_ff // 2),
        grid=(n // TM,),
        in_specs=[pl.BlockSpec((TM, D_MODEL), row), pl.BlockSpec((TM, D_MODEL), row),
                  pl.BlockSpec((None, None, 6, D_MODEL), mod_idx),
                  pl.BlockSpec((None, D_MODEL, d_ff), lambda i: (j, 0, 0)),
                  pl.BlockSpec((None, D_MODEL, d_ff), lambda i: (j, 0, 0)),
                  pl.BlockSpec((None, d_ff, D_MODEL), lambda i: (j, 0, 0))],
        out_specs=pl.BlockSpec((TM, D_MODEL), row),
        out_shape=jax.ShapeDtypeStruct((n, D_MODEL), F32),
        compiler_params=_cparams(("arbitrary",)),
        name="ffn_lat" if lat else "ffn_ctx",
    )(x1, h2, mods, wg_bf, wu_bf, wd_bf)


def _group_layout(rec, counts, n_tiles):
    as_int = lambda k: rec[:, k].astype(jnp.int32)
    cnt = counts[0, :N_EXPERTS].astype(jnp.int32)
    tiles = (cnt + TM_GROUP - 1) // TM_GROUP
    cum = jnp.cumsum(tiles)
    start = (cum - tiles) * TM_GROUP
    experts = jnp.arange(N_EXPERTS, dtype=jnp.int32)

    def pos(e, rank):
        return jnp.sum(jnp.where(e[:, None] == experts[None, :], start[None, :], 0), axis=1) + rank

    pos2 = jnp.stack([pos(as_int(R_E1), as_int(R_RANK1)), pos(as_int(R_E2), as_int(R_RANK2))])
    n_active = cum[N_EXPERTS - 1]
    tile_expert = jnp.sum(jnp.arange(n_tiles, dtype=jnp.int32)[:, None] >= cum[None, :], axis=1)
    tile_expert = jnp.minimum(tile_expert, tile_expert[n_active - 1]).astype(jnp.int32)
    tile_ids = jnp.arange(n_tiles, dtype=jnp.int32)
    group_last = jnp.any((tile_ids[:, None] == cum[None, :] - 1) & (tiles[None, :] > 0), axis=1)
    zero_tile = (group_last | (tile_ids >= n_active)).astype(jnp.int32)
    return (pos2.astype(jnp.int32), tile_expert, n_active.reshape(1).astype(jnp.int32), zero_tile)


def _row_copy(src, dst, sem):
    return pltpu.make_async_copy(src, dst, sem)


def _dispatch_kernel(pos_ref, zt_ref, h_ref, xs_ref, zbuf, sem, zsem):
    n_chunks = h_ref.shape[0] // DMA_CHUNK
    n_tiles = xs_ref.shape[0] // TM_GROUP

    zbuf[...] = jnp.zeros_like(zbuf)

    def zero_fill(j):
        return pltpu.make_async_copy(zbuf, xs_ref.at[pl.ds(j * TM_GROUP, TM_GROUP)], zsem)

    def zero_start(j, carry):
        @pl.when(zt_ref[j] != 0)
        def _():
            zero_fill(j).start()
        return carry

    def zero_wait(j, carry):
        @pl.when(zt_ref[j] != 0)
        def _():
            zero_fill(j).wait()
        return carry

    lax.fori_loop(0, n_tiles, zero_start, 0)
    lax.fori_loop(0, n_tiles, zero_wait, 0)

    def drain_chunk():
        rows = pl.ds(0, 2 * DMA_CHUNK)
        _row_copy(xs_ref.at[rows], xs_ref.at[rows], sem).wait()

    def chunk(k, carry):
        def issue(r, c):
            t = k * DMA_CHUNK + r
            _row_copy(h_ref.at[t], xs_ref.at[pos_ref[0, t]], sem).start()
            _row_copy(h_ref.at[t], xs_ref.at[pos_ref[1, t]], sem).start()
            return c

        lax.fori_loop(0, DMA_CHUNK, issue, 0, unroll=DMA_UNROLL)

        @pl.when(k > 0)
        def _():
            drain_chunk()

        return carry

    lax.fori_loop(0, n_chunks, chunk, 0)
    drain_chunk()


def _dispatch(pos2, zero_tile, h2_rows, n_rows):
    tile = h2_rows.shape[1:]
    return pl.pallas_call(
        _dispatch_kernel,
        grid_spec=pltpu.PrefetchScalarGridSpec(
            num_scalar_prefetch=2, grid=(1,),
            in_specs=[pl.BlockSpec(h2_rows.shape, lambda i, p, z: (0, 0, 0))],
            out_specs=pl.BlockSpec(memory_space=pl.ANY),
            scratch_shapes=[pltpu.VMEM((TM_GROUP,) + tile, F32),
                            pltpu.SemaphoreType.DMA(()), pltpu.SemaphoreType.DMA(())]),
        out_shape=jax.ShapeDtypeStruct((n_rows,) + tile, F32),
        compiler_params=_cparams(("arbitrary",)),
        name="moe_dispatch",
    )(pos2, zero_tile, h2_rows)


def _expert_kernel(te_ref, na_ref, xs_ref, wg_ref, wu_ref, wd_ref, ys_ref,
                   xbuf, ybuf, sem_in, sem_out):
    del te_ref
    i = pl.program_id(0)
    last = pl.num_programs(0) - 1
    n_active = na_ref[0]
    slot = i % 2
    n_col = D_MODEL // LANES

    def in_copies(tile, s):
        rows = pl.ds(tile * TM_GROUP, TM_GROUP)
        return [pltpu.make_async_copy(xs_ref.at[rows, c], xbuf.at[s, :, c * LANES:(c + 1) * LANES],
                                      sem_in.at[s]) for c in range(n_col)]

    def out_copies(tile, s):
        rows = pl.ds(tile * TM_GROUP, TM_GROUP)
        return [pltpu.make_async_copy(ybuf.at[s, :, c * LANES:(c + 1) * LANES], ys_ref.at[rows, c],
                                      sem_out.at[s]) for c in range(n_col)]

    @pl.when(i == 0)
    def _():
        for cp in in_copies(0, 0):
            cp.start()

    @pl.when(i + 1 < n_active)
    def _():
        for cp in in_copies(i + 1, 1 - slot):
            cp.start()

    @pl.when(i >= 2)
    def _():
        for cp in out_copies(i - 2, slot):
            cp.wait()

    @pl.when(i < n_active)
    def _():
        for cp in in_copies(i, slot):
            cp.wait()
        xb = xbuf[slot].astype(BF16)
        act = _silu(_dot(xb, wg_ref[...])) * _dot(xb, wu_ref[...])
        ybuf[slot] = _dot(act.astype(BF16), wd_ref[...])

    @pl.when(i >= n_active)
    def _():
        ybuf[slot] = jnp.zeros((TM_GROUP, D_MODEL), F32)

    for cp in out_copies(i, slot):
        cp.start()

    @pl.when(i == last)
    def _():
        for cp in out_copies(i, slot):
            cp.wait()

        @pl.when(i >= 1)
        def _():
            for cp in out_copies(i - 1, 1 - slot):
                cp.wait()


def _experts(tile_expert, n_active, xs, j, wg_bf, wu_bf, wd_bf):
    n_rows = xs.shape[0]
    d_e = wg_bf.shape[3]
    wspec = lambda shape: pl.BlockSpec((None, None) + shape, lambda i, te, na: (j, te[i], 0, 0))
    return pl.pallas_call(
        _expert_kernel,
        grid_spec=pltpu.PrefetchScalarGridSpec(
            num_scalar_prefetch=2, grid=(n_rows // TM_GROUP,),
            in_specs=[pl.BlockSpec(memory_space=pl.ANY),
                      wspec((D_MODEL, d_e)), wspec((D_MODEL, d_e)), wspec((d_e, D_MODEL))],
            out_specs=pl.BlockSpec(memory_space=pl.ANY),
            scratch_shapes=[pltpu.VMEM((2, TM_GROUP, D_MODEL), F32),
                            pltpu.VMEM((2, TM_GROUP, D_MODEL), F32),
                            pltpu.SemaphoreType.DMA((2,)), pltpu.SemaphoreType.DMA((2,))]),
        out_shape=jax.ShapeDtypeStruct(xs.shape, F32),
        compiler_params=_cparams(("arbitrary",)),
        name="moe_experts",
    )(tile_expert, n_active, xs, wg_bf, wu_bf, wd_bf)


def _combine_kernel(*refs, final):
    pos_ref, y_ref, x_ref, rec_ref, mod_ref = refs[:5]
    g_ref = refs[5] if final else None
    o_ref, a1, a2, b1, b2, sem_a, sem_b = refs[5 + int(final):]
    i = pl.program_id(0)

    def issue(half_idx, d1, d2, sem):
        base = half_idx * DMA_CHUNK

        def body(r, c):
            _row_copy(y_ref.at[pos_ref[0, base + r]], d1.at[r], sem).start()
            _row_copy(y_ref.at[pos_ref[1, base + r]], d2.at[r], sem).start()
            return c

        lax.fori_loop(0, DMA_CHUNK, body, 0, unroll=DMA_UNROLL)

    def drain(d1, d2, sem):
        rows = pl.ds(0, DMA_CHUNK)
        _row_copy(y_ref.at[rows], d1, sem).wait()
        _row_copy(y_ref.at[rows], d2, sem).wait()

    def finish(half, d1, d2):
        rows = slice(half * DMA_CHUNK, (half + 1) * DMA_CHUNK)
        w1 = rec_ref[rows, R_W1:R_W1 + 1]
        w2 = rec_ref[rows, R_W2:R_W2 + 1]
        for c in range(D_MODEL // LANES):
            sl = slice(c * LANES, (c + 1) * LANES)
            y = w1 * d1[:, c, :] + w2 * d2[:, c, :]
            o_ref[rows, sl] = x_ref[rows, sl] + mod_ref[5:6, sl] * y
        if final:
            o_ref[rows, :] = _rms(o_ref[rows, :], g_ref[...])

    @pl.when(i == 0)
    def _():
        issue(0, a1, a2, sem_a)

    issue(2 * i + 1, b1, b2, sem_b)
    drain(a1, a2, sem_a)
    finish(0, a1, a2)

    @pl.when(i + 1 < pl.num_programs(0))
    def _():
        issue(2 * i + 2, a1, a2, sem_a)

    drain(b1, b2, sem_b)
    finish(1, b1, b2)


def _combine(pos2, ys, x1, rec, mods, l, seq_len, lat, final_g):
    n = x1.shape[0]
    final = final_g is not None
    tile = ys.shape[1:]
    tm = 2 * DMA_CHUNK
    tiles_per_seq = max(seq_len // tm, 1)
    if lat:
        mod_idx = lambda i, p: (l, 1 + i // tiles_per_seq, 0, 0)
    else:
        mod_idx = lambda i, p: (l, 0, 0, 0)
    row = lambda i, p: (i, 0)
    slot = pltpu.VMEM((DMA_CHUNK,) + tile, F32)
    in_specs = [pl.BlockSpec(memory_space=pl.ANY),
                pl.BlockSpec((tm, D_MODEL), row),
                pl.BlockSpec((tm, LANES), row),
                pl.BlockSpec((None, None, 6, D_MODEL), mod_idx)]
    args = [pos2, ys, x1, rec, mods]
    if final:
        in_specs.append(pl.BlockSpec((1, D_MODEL), lambda i, p: (0, 0)))
        args.append(final_g)
    return pl.pallas_call(
        functools.partial(_combine_kernel, final=final),
        grid_spec=pltpu.PrefetchScalarGridSpec(
            num_scalar_prefetch=1, grid=(n // tm,),
            in_specs=in_specs,
            out_specs=pl.BlockSpec((tm, D_MODEL), row),
            scratch_shapes=[slot, slot, slot, slot,
                            pltpu.SemaphoreType.DMA(()), pltpu.SemaphoreType.DMA(())]),
        out_shape=jax.ShapeDtypeStruct((n, D_MODEL), F32),
        compiler_params=_cparams(("arbitrary",)),
        name="moe_combine_final" if final else "moe_combine",
    )(*args)


def _moe(x1, h2_rows, rec, counts, mods, l, j, wg_bf, wu_bf, wd_bf, seq_len, lat, final_g):
    n = x1.shape[0]
    n_tiles = 2 * n // TM_GROUP + N_EXPERTS
    pos2, tile_expert, n_active, zero_tile = _group_layout(rec, counts, n_tiles)
    xs = _dispatch(pos2, zero_tile, h2_rows, n_tiles * TM_GROUP)
    ys = _experts(tile_expert, n_active, xs, j, wg_bf, wu_bf, wd_bf)
    return _combine(pos2, ys, x1, rec, mods, l, seq_len, lat, final_g)


def _final_norm_kernel(x_ref, g_ref, o_ref):
    o_ref[...] = _rms(x_ref[...], g_ref[...])


def _final_norm(x, g):
    n = x.shape[0]
    row = lambda i: (i, 0)
    return pl.pallas_call(
        _final_norm_kernel,
        grid=(n // TM_FF,),
        in_specs=[pl.BlockSpec((TM_FF, D_MODEL), row), pl.BlockSpec((1, D_MODEL), lambda i: (0, 0))],
        out_specs=pl.BlockSpec((TM_FF, D_MODEL), row),
        out_shape=jax.ShapeDtypeStruct((n, D_MODEL), F32),
        compiler_params=_cparams(("arbitrary",)),
        name="final_norm",
    )(x, g)


def _rope_tables(t_len):
    t = jnp.arange(t_len)
    pos = jnp.stack([t // GRID_W, t % GRID_W], axis=-1).astype(F32)
    inv = ROPE_THETA ** (-jnp.arange(0, AXIS_DIM, 2, dtype=F32) / AXIS_DIM)
    ang = pos[..., None] * inv
    cos, sin = jnp.cos(ang), jnp.sin(ang)
    c = jnp.concatenate([cos[:, 0], cos[:, 0], cos[:, 1], cos[:, 1]], axis=-1)
    s = jnp.concatenate([-sin[:, 0], sin[:, 0], -sin[:, 1], sin[:, 1]], axis=-1)
    return c, s


def _gate_weights(wa, wi):
    depth = wa.shape[0]
    groups = D_RNN // LANES

    def pair(w):
        w = w.reshape(depth, 2, groups, 2, RNN_BLOCK, RNN_BLOCK)
        z = jnp.zeros_like(w[:, :, :, 0])
        top = jnp.concatenate([w[:, :, :, 0], z], axis=-1)
        bot = jnp.concatenate([z, w[:, :, :, 1]], axis=-1)
        return jnp.concatenate([top, bot], axis=-2)

    w = jnp.concatenate([pair(wa), pair(wi)], axis=-1)
    return (0.5 * jnp.concatenate([w[:, 0], w[:, 1]], axis=-1)).astype(BF16)


def kernel(x_prompt, x_sample, cache_k, cache_v, state_rnn, c, c_ctx, w_mod, b_mod, norm1, norm2, w_in, conv_w, conv_b, lru_wa, lru_ba, lru_wi, lru_bi, lru_lambda, q_norm, k_norm, w_rnn_out, w_attn_out, w_out, ff_gate, ff_up, ff_down, router_w, router_b, exp_gate, exp_up, exp_down, final_norm):
    batch, seq, _ = x_prompt.shape
    dec_batch, dec_seq, _ = x_sample.shape
    depth = w_mod.shape[0]
    past = cache_k.shape[2]
    assert dec_batch + 1 <= MOD_ROWS

    cond8 = jnp.zeros((MOD_ROWS, D_MODEL), F32).at[0].set(c_ctx).at[1:1 + dec_batch].set(c)
    mods = _modulation(cond8, w_mod, b_mod).reshape(depth, MOD_ROWS, 6, D_MODEL)

    bf = lambda w: w.astype(BF16)
    w_in_bf, wr_bf, wa_bf, wo_bf = bf(w_in), bf(w_rnn_out), bf(w_attn_out), bf(w_out)
    ffg_bf, ffu_bf, ffd_bf = bf(ff_gate), bf(ff_up), bf(ff_down)
    eg_bf, eu_bf, ed_bf = bf(exp_gate), bf(exp_up), bf(exp_down)
    gate_w = _gate_weights(lru_wa, lru_wi)
    n_moe = router_w.shape[0]
    wr_t = jnp.swapaxes(router_w, 1, 2)
    br_pad = jnp.zeros((n_moe, 1, LANES), F32).at[:, 0, :N_EXPERTS].set(router_b)

    norm1_3 = norm1.reshape(depth, 1, D_MODEL)
    norm2_3 = norm2.reshape(depth, 1, D_MODEL)
    qn3 = q_norm.reshape(depth, 1, HEAD_DIM)
    kn3 = k_norm.reshape(depth, 1, HEAD_DIM)
    conv_b3 = conv_b.reshape(depth, 1, D_RNN)
    ba4 = (0.5 * lru_ba).reshape(depth, 2, 1, D_RNN)
    bi4 = (0.5 * lru_bi).reshape(depth, 2, 1, D_RNN)
    lam4 = lru_lambda.reshape(depth, 2, 1, D_RNN)
    state5 = state_rnn.reshape(dec_batch, depth, 2, 1, D_RNN)
    cache_k4 = cache_k.reshape(dec_batch, depth, past, KV_DIM)
    cache_v4 = cache_v.reshape(dec_batch, depth, past, KV_DIM)
    rope_tabs = _rope_tables(dec_seq)
    final_g = final_norm.reshape(1, D_MODEL)

    def run_stream(x, lat):
        seq_len = dec_seq if lat else seq
        ks, vs, ss = [], [], []
        for l in range(depth):
            xr, yr, q, kn, kr, v, gr, ga = _in_proj(
                x, mods, l, norm1_3, w_in_bf, qn3, kn3, rope_tabs if lat else None, seq_len)
            out_r, end_f, end_b = _rnn(xr, yr, l, conv_w, conv_b3, gate_w, ba4, bi4, lam4,
                                       state5 if lat else None, seq_len)
            if lat:
                out_a = _attn_lat(q, kr, v, cache_k4, cache_v4, l, seq_len)
            else:
                out_a = _attn_ctx(q, kr, v, seq_len)
                ks.append(kn)
                vs.append(v)
                ss.append(jnp.stack([end_f, end_b], axis=1))
            moe_layer = l % 2 == 1
            mix_args = (x, out_r, out_a, gr, ga, mods, l, norm2_3, wr_bf, wa_bf, wo_bf, seq_len, lat)
            if moe_layer:
                x1, h2_rows, rec, counts = _mix_out(*mix_args, router=(l // 2, wr_t, br_pad))
                x = _moe(x1, h2_rows, rec, counts, mods, l, l // 2, eg_bf, eu_bf, ed_bf,
                         seq_len, lat, final_g if l == depth - 1 else None)
            else:
                x1, h2 = _mix_out(*mix_args)
                x = _ffn(x1, h2, mods, l, l // 2, ffg_bf, ffu_bf, ffd_bf, seq_len, lat)
                if l == depth - 1:
                    x = _final_norm(x, final_g)
        return x, ks, vs, ss

    y_ctx, ks, vs, ss = run_stream(x_prompt.reshape(batch * seq, D_MODEL), False)
    y_lat, _, _, _ = run_stream(x_sample.reshape(dec_batch * dec_seq, D_MODEL), True)

    new_k = jnp.stack([k.reshape(batch, seq, N_KV_HEADS, HEAD_DIM) for k in ks], axis=1)
    new_v = jnp.stack([v.reshape(batch, seq, N_KV_HEADS, HEAD_DIM) for v in vs], axis=1)
    new_state = jnp.stack(ss, axis=1)
    return (y_ctx.reshape(batch, seq, D_MODEL), y_lat.reshape(dec_batch, dec_seq, D_MODEL),
            new_k, new_v, new_state)
```

```python
import functools

import jax
import jax.numpy as jnp
from jax import lax
from jax.experimental import pallas as pl
from jax.experimental.pallas import tpu as pltpu

F32 = jnp.float32
BF16 = jnp.bfloat16

D_MODEL = 1024
D_RNN = 1024
N_RNN_BLOCKS = 16
RNN_BLOCK = D_RNN // N_RNN_BLOCKS
CONV_W = 4
CONV_LEFT = 2
LRU_C = 8.0
N_HEADS = 8
N_KV_HEADS = 2
HEAD_DIM = 128
AXIS_DIM = HEAD_DIM // 2
ROPE_THETA = 10000.0
GRID_W = 64
Q_DIM = N_HEADS * HEAD_DIM
KV_DIM = N_KV_HEADS * HEAD_DIM
N_EXPERTS = 8
EPS = 1e-6
LOG2_E = 1.4426950408889634
RSQRT_FLOOR = 1e-30
D_IN = 2 * D_RNN + Q_DIM + 2 * KV_DIM + 2 * D_MODEL
O_XR, O_YR, O_Q = 0, D_RNN, 2 * D_RNN
O_K = O_Q + Q_DIM
O_V = O_K + KV_DIM
O_GR = O_V + KV_DIM
O_GA = O_GR + D_MODEL

LANES = 128
SUBLANES = 8
MOD_ROWS = 8
VMEM_LIMIT = 56 * 1024 * 1024

TM = 512
TM_ROPE = 256
RESIDENT = pl.Buffered(1)
RNN_T = 256
RNN_SEG = SUBLANES
RNN_C = 256
RNN_TCH = 32
RNN_SCAN_UNROLL = 8
TQ = 256
TM_FF = 512
TM_GROUP = 256
DMA_CHUNK = 256
DMA_UNROLL = True


def _cparams(sem):
    return pltpu.CompilerParams(dimension_semantics=sem, vmem_limit_bytes=VMEM_LIMIT)


def _sigmoid(x):
    return 1.0 / (1.0 + jnp.exp(-x))


def _silu(x):
    return x * _sigmoid(x)


def _gelu_tanh(x):
    return 0.5 * x * (1.0 + jnp.tanh(0.7978845608028654 * (x + 0.044715 * (x * x * x))))


def _rms(x, g):
    ms = jnp.mean(x * x, axis=-1, keepdims=True)
    return x * lax.rsqrt(ms + EPS) * g


def _dot(a, b):
    return jnp.dot(a, b, preferred_element_type=F32)


def _mod_kernel(cond_ref, w_ref, b_ref, o_ref):
    s = _silu(cond_ref[...]).astype(BF16)
    o_ref[...] = _dot(s, w_ref[...].astype(BF16)) + b_ref[...]


def _modulation(cond8, w_mod, b_mod):
    depth = w_mod.shape[0]
    tn = 1536
    return pl.pallas_call(
        _mod_kernel,
        grid=(depth, 6 * D_MODEL // tn),
        in_specs=[
            pl.BlockSpec((MOD_ROWS, D_MODEL), lambda l, j: (0, 0)),
            pl.BlockSpec((None, D_MODEL, tn), lambda l, j: (l, 0, j)),
            pl.BlockSpec((None, 1, tn), lambda l, j: (l, 0, j)),
        ],
        out_specs=pl.BlockSpec((None, MOD_ROWS, tn), lambda l, j: (l, 0, j)),
        out_shape=jax.ShapeDtypeStruct((depth, MOD_ROWS, 6 * D_MODEL), F32),
        compiler_params=_cparams(("arbitrary", "arbitrary")),
        name="adaln_mod",
    )(cond8, w_mod, b_mod.reshape(depth, 1, 6 * D_MODEL))


def _swap_halves(x):
    lane = lax.broadcasted_iota(jnp.int32, x.shape, 1)
    return jnp.where((lane % AXIS_DIM) < AXIS_DIM // 2,
                     pltpu.roll(x, LANES - AXIS_DIM // 2, axis=1),
                     pltpu.roll(x, AXIS_DIM // 2, axis=1))


def _in_proj_kernel(*refs, rope):
    if rope:
        (x_ref, mod_ref, n1_ref, w_ref, qn_ref, kn_ref, c_ref, s_ref,
         xr_ref, yr_ref, q_ref, kno_ref, kro_ref, v_ref, gr_ref, ga_ref) = refs
    else:
        (x_ref, mod_ref, n1_ref, w_ref, qn_ref, kn_ref,
         xr_ref, yr_ref, q_ref, kno_ref, kro_ref, v_ref, gr_ref, ga_ref) = refs
    x = x_ref[...]
    h = _rms(x, n1_ref[...]) * (1.0 + mod_ref[1:2, :]) + mod_ref[0:1, :]
    hb = h.astype(BF16)

    xr_ref[...] = _dot(hb, w_ref[:, O_XR:O_XR + D_RNN])
    yr_ref[...] = _dot(hb, w_ref[:, O_YR:O_YR + D_RNN])
    gr_ref[...] = _dot(hb, w_ref[:, O_GR:O_GR + D_MODEL])
    ga_ref[...] = _dot(hb, w_ref[:, O_GA:O_GA + D_MODEL])
    v_ref[...] = _dot(hb, w_ref[:, O_V:O_V + KV_DIM])

    if rope:
        cs, sn = c_ref[...], s_ref[...]
    zq = _dot(hb, w_ref[:, O_Q:O_Q + Q_DIM])
    for hd in range(N_HEADS):
        t = _rms(zq[:, hd * HEAD_DIM:(hd + 1) * HEAD_DIM], qn_ref[...])
        if rope:
            t = t * cs + _swap_halves(t) * sn
        q_ref[:, hd * HEAD_DIM:(hd + 1) * HEAD_DIM] = t.astype(BF16)
    zk = _dot(hb, w_ref[:, O_K:O_K + KV_DIM])
    for hd in range(N_KV_HEADS):
        t = _rms(zk[:, hd * HEAD_DIM:(hd + 1) * HEAD_DIM], kn_ref[...])
        kno_ref[:, hd * HEAD_DIM:(hd + 1) * HEAD_DIM] = t
        if rope:
            t = t * cs + _swap_halves(t) * sn
        kro_ref[:, hd * HEAD_DIM:(hd + 1) * HEAD_DIM] = t.astype(BF16)


def _in_proj(x, mods, l, norm1, w_in_bf, q_norm, k_norm, rope_tabs, seq_len):
    n = x.shape[0]
    rope = rope_tabs is not None
    tm = TM_ROPE if rope else TM
    tiles_per_seq = seq_len // tm
    if rope:
        mod_idx = lambda i: (l, 1 + i // tiles_per_seq, 0, 0)
    else:
        mod_idx = lambda i: (l, 0, 0, 0)
    row = lambda i: (i, 0)
    in_specs = [
        pl.BlockSpec((tm, D_MODEL), row),
        pl.BlockSpec((None, None, 6, D_MODEL), mod_idx),
        pl.BlockSpec((None, 1, D_MODEL), lambda i: (l, 0, 0)),
        pl.BlockSpec((None, D_MODEL, D_IN), lambda i: (l, 0, 0), pipeline_mode=RESIDENT),
        pl.BlockSpec((None, 1, HEAD_DIM), lambda i: (l, 0, 0)),
        pl.BlockSpec((None, 1, HEAD_DIM), lambda i: (l, 0, 0)),
    ]
    args = [x, mods, norm1, w_in_bf, q_norm, k_norm]
    if rope:
        in_specs += [pl.BlockSpec((tm, HEAD_DIM), lambda i: (i % tiles_per_seq, 0))] * 2
        args += list(rope_tabs)
    widths = [(D_RNN, F32), (D_RNN, F32), (Q_DIM, BF16), (KV_DIM, F32), (KV_DIM, BF16),
              (KV_DIM, F32), (D_MODEL, F32), (D_MODEL, F32)]
    return pl.pallas_call(
        functools.partial(_in_proj_kernel, rope=rope),
        grid=(n // tm,),
        in_specs=in_specs,
        out_specs=[pl.BlockSpec((tm, w), row) for w, _ in widths],
        out_shape=[jax.ShapeDtypeStruct((n, w), dt) for w, dt in widths],
        compiler_params=_cparams(("arbitrary",)),
        name="in_proj_lat" if rope else "in_proj_ctx",
    )(*args)


def _rnn_kernel(*refs, chained):
    refs = list(refs)
    xr_ref, yr_ref, cw_ref, cb_ref, gw_ref, ba_ref, bi_ref, lam_ref = refs[:8]
    refs = refs[8:]
    h0_ref = refs.pop(0) if chained else None
    out_ref, sf_ref, sb_ref, xext, ybuf, obuf, a_f, b_f, a_b, b_b, sem_in, sem_out = refs

    g = pl.program_id(0)
    cb = pl.program_id(1)
    n_cb = pl.num_programs(1)
    step = g * n_cb + cb
    n_steps = pl.num_programs(0) * n_cb
    slot = step % 2

    def hbm_window(kk, s):
        rows = pl.ds(pl.multiple_of(((kk // n_cb) * RNN_SEG + s) * RNN_T, RNN_T), RNN_T)
        cols = pl.ds(pl.multiple_of((kk % n_cb) * RNN_C, RNN_C), RNN_C)
        return rows, cols

    def in_copies(kk, sl_):
        cps = []
        for s in range(RNN_SEG):
            win = hbm_window(kk, s)
            cps.append(pltpu.make_async_copy(
                xr_ref.at[win], xext.at[sl_, pl.ds(CONV_LEFT, RNN_T), s], sem_in.at[sl_]))
            cps.append(pltpu.make_async_copy(yr_ref.at[win], ybuf.at[sl_, :, s], sem_in.at[sl_]))
        return cps

    def out_copies(kk, sl_):
        return [pltpu.make_async_copy(obuf.at[sl_, :, s], out_ref.at[hbm_window(kk, s)],
                                      sem_out.at[sl_]) for s in range(RNN_SEG)]

    @pl.when(step == 0)
    def _():
        for cp in in_copies(0, 0):
            cp.start()

    @pl.when(step + 1 < n_steps)
    def _():
        for cp in in_copies(step + 1, 1 - slot):
            cp.start()

    for cp in in_copies(step, slot):
        cp.wait()

    seg = lax.broadcasted_iota(jnp.int32, (RNN_SEG, RNN_C), 0)
    zero_row = jnp.zeros((RNN_SEG, RNN_C), F32)
    if chained:
        xext[slot, 0] = jnp.where(seg == 0, zero_row, pltpu.roll(xext[slot, RNN_T], 1, axis=0))
        xext[slot, 1] = jnp.where(seg == 0, zero_row, pltpu.roll(xext[slot, RNN_T + 1], 1, axis=0))
        xext[slot, RNN_T + CONV_LEFT] = jnp.where(
            seg == RNN_SEG - 1, zero_row, pltpu.roll(xext[slot, CONV_LEFT], RNN_SEG - 1, axis=0))
    else:
        xext[slot, 0] = zero_row
        xext[slot, 1] = zero_row
        xext[slot, RNN_T + CONV_LEFT] = zero_row

    scans = ((a_f, b_f), (a_b, b_b))
    rows2d = RNN_TCH * RNN_SEG

    def gate_chunk(tc, carry):
        t0 = pl.multiple_of(tc * RNN_TCH, RNN_TCH)
        for j in range(RNN_C // LANES):
            sl = slice(j * LANES, (j + 1) * LANES)
            xc = jnp.broadcast_to(cb_ref[:, sl].reshape(1, 1, LANES), (RNN_TCH, RNN_SEG, LANES))
            for k in range(CONV_W):
                xc = xc + (xext[slot, pl.ds(t0 + k, RNN_TCH), :, sl]
                           * cw_ref[k:k + 1, sl].reshape(1, 1, LANES))
            xc = xc.reshape(rows2d, LANES)
            pre = _dot(xc.astype(BF16), gw_ref[j])
            for d, (a_s, b_s) in enumerate(scans):
                lam = lam_ref[d, :, sl]
                log2_decay = (-LRU_C * LOG2_E) * (jnp.maximum(-lam, 0.0)
                                                  + jnp.log(1.0 + jnp.exp(-jnp.abs(lam))))
                r = 0.5 * jnp.tanh(pre[:, 2 * d * LANES:(2 * d + 1) * LANES]
                                   + ba_ref[d, :, sl]) + 0.5
                gi = 0.5 * jnp.tanh(pre[:, (2 * d + 1) * LANES:(2 * d + 2) * LANES]
                                    + bi_ref[d, :, sl]) + 0.5
                a = jnp.exp2(r * log2_decay)
                y = 1.0 - a * a
                b = (y * lax.rsqrt(jnp.maximum(y, RSQRT_FLOOR))) * (gi * xc)
                a_s[pl.ds(t0, RNN_TCH), :, sl] = a.reshape(RNN_TCH, RNN_SEG, LANES)
                b_s[pl.ds(t0, RNN_TCH), :, sl] = b.reshape(RNN_TCH, RNN_SEG, LANES)
        return carry

    lax.fori_loop(0, RNN_T // RNN_TCH, gate_chunk, 0)

    def scan_step(t, carry):
        hf, hb, pf, pb = carry
        tb = RNN_T - 1 - t
        af_t = a_f[t]
        hf = af_t * hf + b_f[t]
        b_f[t] = hf
        ab_t = a_b[tb]
        hb = ab_t * hb + b_b[tb]
        b_b[tb] = hb
        if chained:
            pf = af_t * pf
            a_f[t] = pf
            pb = ab_t * pb
            a_b[tb] = pb
        return hf, hb, pf, pb

    ones = jnp.ones((RNN_SEG, RNN_C), F32)
    hf, hb, pf, pb = lax.fori_loop(0, RNN_T, scan_step, (zero_row, zero_row, ones, ones),
                                   unroll=RNN_SCAN_UNROLL)

    if chained:
        st_f = jnp.broadcast_to(h0_ref[0], (RNN_SEG, RNN_C))
        st_b = jnp.broadcast_to(h0_ref[1], (RNN_SEG, RNN_C))
        in_f, in_b = st_f, st_b
        for _ in range(RNN_SEG - 1):
            in_f = jnp.where(seg == 0, st_f, pltpu.roll(pf * in_f + hf, 1, axis=0))
            in_b = jnp.where(seg == RNN_SEG - 1, st_b,
                             pltpu.roll(pb * in_b + hb, RNN_SEG - 1, axis=0))
        sf_ref[...] = pf * in_f + hf
        sb_ref[...] = pb * in_b + hb
    else:
        sf_ref[...] = hf
        sb_ref[...] = hb

    @pl.when(step >= 2)
    def _():
        for cp in out_copies(step - 2, slot):
            cp.wait()

    def finish_chunk(tc, carry):
        rows = pl.ds(pl.multiple_of(tc * RNN_TCH, RNN_TCH), RNN_TCH)
        h_f, h_b = b_f[rows], b_b[rows]
        if chained:
            h_f = a_f[rows] * in_f.reshape(1, RNN_SEG, RNN_C) + h_f
            h_b = a_b[rows] * in_b.reshape(1, RNN_SEG, RNN_C) + h_b
        obuf[slot, rows] = (h_f + h_b) * _gelu_tanh(ybuf[slot, rows])
        return carry

    lax.fori_loop(0, RNN_T // RNN_TCH, finish_chunk, 0)

    for cp in out_copies(step, slot):
        cp.start()

    @pl.when(step == n_steps - 1)
    def _():
        for cp in out_copies(step, slot):
            cp.wait()

        @pl.when(step >= 1)
        def _():
            for cp in out_copies(step - 1, 1 - slot):
                cp.wait()


def _rnn(xr, yr, l, conv_w, conv_b, gate_w, lru_ba, lru_bi, lru_lambda, state5, seq_len):
    n = xr.shape[0]
    chained = state5 is not None
    group_len = RNN_SEG * RNN_T
    assert n % group_len == 0 and seq_len in ((group_len,) if chained else (RNN_T,))
    n_groups = n // group_len
    lanes_per_step = RNN_C // LANES
    vec = pl.BlockSpec((None, 2, 1, RNN_C), lambda g, c: (l, 0, 0, c))
    in_specs = [
        pl.BlockSpec(memory_space=pl.ANY),
        pl.BlockSpec(memory_space=pl.ANY),
        pl.BlockSpec((None, CONV_W, RNN_C), lambda g, c: (l, 0, c)),
        pl.BlockSpec((None, 1, RNN_C), lambda g, c: (l, 0, c)),
        pl.BlockSpec((None, lanes_per_step, LANES, 4 * LANES), lambda g, c: (l, c, 0, 0)),
        vec, vec, vec,
    ]
    args = [xr, yr, conv_w, conv_b, gate_w, lru_ba, lru_bi, lru_lambda]
    if chained:
        in_specs.append(pl.BlockSpec((None, None, 2, 1, RNN_C), lambda g, c: (g, l, 0, 0, c)))
        args.append(state5)
    state_spec = pl.BlockSpec((RNN_SEG, RNN_C), lambda g, c: (g, c))
    state_shape = jax.ShapeDtypeStruct((n_groups * RNN_SEG, D_RNN), F32)
    work = lambda t: pltpu.VMEM((t, RNN_SEG, RNN_C), F32)
    return pl.pallas_call(
        functools.partial(_rnn_kernel, chained=chained),
        grid=(n_groups, D_RNN // RNN_C),
        in_specs=in_specs,
        out_specs=[pl.BlockSpec(memory_space=pl.ANY), state_spec, state_spec],
        out_shape=[jax.ShapeDtypeStruct((n, D_RNN), F32), state_shape, state_shape],
        scratch_shapes=[
            pltpu.VMEM((2, RNN_T + CONV_W - 1, RNN_SEG, RNN_C), F32),
            pltpu.VMEM((2, RNN_T, RNN_SEG, RNN_C), F32),
            pltpu.VMEM((2, RNN_T, RNN_SEG, RNN_C), F32),
            work(RNN_T), work(RNN_T), work(RNN_T), work(RNN_T),
            pltpu.SemaphoreType.DMA((2,)), pltpu.SemaphoreType.DMA((2,)),
        ],
        compiler_params=_cparams(("arbitrary", "arbitrary")),
        name="rnn_lat" if chained else "rnn_ctx",
    )(*args)


def _attend_heads(q_ref, k_all, v_all, o_ref, stack_heads):
    exp2_scale = (1.0 / (HEAD_DIM ** 0.5)) * LOG2_E
    group = N_HEADS // N_KV_HEADS
    tq = q_ref.shape[0]
    for kh in range(N_KV_HEADS):
        k = k_all[:, kh * HEAD_DIM:(kh + 1) * HEAD_DIM]
        v = v_all[:, kh * HEAD_DIM:(kh + 1) * HEAD_DIM]
        heads = [slice((kh * group + g) * HEAD_DIM, (kh * group + g + 1) * HEAD_DIM)
                 for g in range(group)]
        stacks = [heads] if stack_heads else [[sl] for sl in heads]
        for stack in stacks:
            q = jnp.concatenate([q_ref[:, sl] for sl in stack], axis=0)
            s = lax.dot_general(q, k, (((1,), (1,)), ((), ())), preferred_element_type=F32)
            m = jnp.max(s, axis=-1, keepdims=True)
            p = jnp.exp2((s - m) * exp2_scale)
            denom = jnp.sum(p, axis=-1, keepdims=True)
            o = (_dot(p.astype(BF16), v) / denom).astype(BF16)
            for g, sl in enumerate(stack):
                o_ref[:, sl] = o[g * tq:(g + 1) * tq]


def _attn_ctx_kernel(q_ref, k_ref, v_ref, o_ref):
    _attend_heads(q_ref, k_ref[...], v_ref[...].astype(BF16), o_ref, stack_heads=True)


def _attn_ctx(q, kr, v, seq_len):
    n = q.shape[0]
    row = lambda b: (b, 0)
    return pl.pallas_call(
        _attn_ctx_kernel,
        grid=(n // seq_len,),
        in_specs=[pl.BlockSpec((seq_len, Q_DIM), row),
                  pl.BlockSpec((seq_len, KV_DIM), row),
                  pl.BlockSpec((seq_len, KV_DIM), row)],
        out_specs=pl.BlockSpec((seq_len, Q_DIM), row),
        out_shape=jax.ShapeDtypeStruct((n, Q_DIM), BF16),
        compiler_params=_cparams(("arbitrary",)),
        name="attn_ctx",
    )(q, kr, v)


def _attn_lat_kernel(q_ref, kc_ref, vc_ref, kl_ref, vl_ref, o_ref, k_all, v_all, *, past):
    @pl.when(pl.program_id(1) == 0)
    def _():
        k_all[0:past, :] = kc_ref[...].astype(BF16)
        k_all[past:, :] = kl_ref[...]
        v_all[0:past, :] = vc_ref[...].astype(BF16)
        v_all[past:, :] = vl_ref[...].astype(BF16)

    _attend_heads(q_ref, k_all[...], v_all[...], o_ref, stack_heads=False)


def _attn_lat(q, kr, v, cache_k4, cache_v4, l, seq_len):
    n = q.shape[0]
    past = cache_k4.shape[2]
    nq = seq_len // TQ
    return pl.pallas_call(
        functools.partial(_attn_lat_kernel, past=past),
        grid=(n // seq_len, nq),
        in_specs=[pl.BlockSpec((TQ, Q_DIM), lambda b, j: (b * nq + j, 0)),
                  pl.BlockSpec((None, None, past, KV_DIM), lambda b, j: (b, l, 0, 0)),
                  pl.BlockSpec((None, None, past, KV_DIM), lambda b, j: (b, l, 0, 0)),
                  pl.BlockSpec((seq_len, KV_DIM), lambda b, j: (b, 0)),
                  pl.BlockSpec((seq_len, KV_DIM), lambda b, j: (b, 0))],
        out_specs=pl.BlockSpec((TQ, Q_DIM), lambda b, j: (b * nq + j, 0)),
        out_shape=jax.ShapeDtypeStruct((n, Q_DIM), BF16),
        scratch_shapes=[pltpu.VMEM((past + seq_len, KV_DIM), BF16),
                        pltpu.VMEM((past + seq_len, KV_DIM), BF16)],
        compiler_params=_cparams(("arbitrary", "arbitrary")),
        name="attn_lat",
    )(q, cache_k4, cache_v4, kr, v)


R_E1, R_E2, R_W1, R_W2, R_RANK1, R_RANK2 = range(6)


def _route(h2, wrt_ref, brt_ref, count_ref):
    shape = (h2.shape[0], LANES)
    lane = lax.broadcasted_iota(jnp.int32, shape, 1)
    neg = jnp.float32(-jnp.inf)
    logits = jnp.full(shape, neg, F32)
    for e in range(N_EXPERTS):
        prod = h2 * wrt_ref[e:e + 1, :]
        part = prod[:, 0:LANES]
        for c in range(1, D_MODEL // LANES):
            part = part + prod[:, c * LANES:(c + 1) * LANES]
        logits = jnp.where(lane == e, jnp.sum(part, axis=-1, keepdims=True), logits)
    logits = logits + brt_ref[...]
    v1 = jnp.max(logits, axis=-1, keepdims=True)
    i1 = jnp.min(jnp.where(logits == v1, lane, LANES), axis=-1, keepdims=True)
    rest = jnp.where(lane == i1, neg, logits)
    v2 = jnp.max(rest, axis=-1, keepdims=True)
    i2 = jnp.min(jnp.where(rest == v2, lane, LANES), axis=-1, keepdims=True)
    e2 = jnp.exp(v2 - v1)
    w1 = 1.0 / (1.0 + e2)
    w2 = e2 / (1.0 + e2)
    chosen = jnp.where((lane == i1) | (lane == i2), 1.0, 0.0)
    r_id = lax.broadcasted_iota(jnp.int32, (shape[0], shape[0]), 0)
    c_id = lax.broadcasted_iota(jnp.int32, (shape[0], shape[0]), 1)
    before = jnp.where(c_id < r_id, 1.0, 0.0).astype(BF16)
    rank = _dot(before, chosen.astype(BF16)) + count_ref[...]
    rank1 = jnp.sum(jnp.where(lane == i1, rank, 0.0), axis=-1, keepdims=True)
    rank2 = jnp.sum(jnp.where(lane == i2, rank, 0.0), axis=-1, keepdims=True)
    count_ref[...] += jnp.sum(chosen, axis=0, keepdims=True)
    rec = jnp.zeros(shape, F32)
    for k, val in ((R_E1, i1.astype(F32)), (R_E2, i2.astype(F32)), (R_W1, w1), (R_W2, w2),
                   (R_RANK1, rank1), (R_RANK2, rank2)):
        rec = jnp.where(lane == k, val, rec)
    return rec


def _mix_out_kernel(*refs, route):
    (x_ref, r_ref, a_ref, gr_ref, ga_ref, mod_ref, n2_ref, wr_ref, wa_ref, wo_ref) = refs[:10]
    if route:
        wrt_ref, brt_ref, x1_ref, h2_ref, rec_ref, cnt_ref, count_ref, hbuf, hsem = refs[10:]
    else:
        x1_ref, h2_ref = refs[10:]
    merged = (_sigmoid(gr_ref[...]) * _dot(r_ref[...].astype(BF16), wr_ref[...])
              + _sigmoid(ga_ref[...]) * _dot(a_ref[...], wa_ref[...]))
    mix = _dot(merged.astype(BF16), wo_ref[...])
    x1 = x_ref[...] + mod_ref[2:3, :] * mix
    x1_ref[...] = x1
    h2 = _rms(x1, n2_ref[...]) * (1.0 + mod_ref[4:5, :]) + mod_ref[3:4, :]
    if not route:
        h2_ref[...] = h2.astype(BF16)
        return

    @pl.when(pl.program_id(0) == 0)
    def _():
        count_ref[...] = jnp.zeros_like(count_ref)

    rec_ref[...] = _route(h2, wrt_ref, brt_ref, count_ref)
    cnt_ref[...] = count_ref[...]

    step = pl.program_id(0)
    slot = step % 2

    def h_copies(k, s):
        rows = pl.ds(k * TM, TM)
        return [pltpu.make_async_copy(hbuf.at[s, :, c * LANES:(c + 1) * LANES], h2_ref.at[rows, c],
                                      hsem.at[s]) for c in range(D_MODEL // LANES)]

    @pl.when(step >= 2)
    def _():
        for cp in h_copies(step - 2, slot):
            cp.wait()

    hbuf[slot] = h2
    for cp in h_copies(step, slot):
        cp.start()

    @pl.when(step == pl.num_programs(0) - 1)
    def _():
        for cp in h_copies(step, slot):
            cp.wait()

        @pl.when(step >= 1)
        def _():
            for cp in h_copies(step - 1, 1 - slot):
                cp.wait()


def _mix_out(x, out_r, out_a, gr, ga, mods, l, norm2, wr_bf, wa_bf, wo_bf, seq_len, lat,
             router=None):
    n = x.shape[0]
    tiles_per_seq = seq_len // TM
    if lat:
        mod_idx = lambda i: (l, 1 + i // tiles_per_seq, 0, 0)
    else:
        mod_idx = lambda i: (l, 0, 0, 0)
    row = lambda i: (i, 0)
    wspec = pl.BlockSpec((None, D_MODEL, D_MODEL), lambda i: (l, 0, 0), pipeline_mode=RESIDENT)
    in_specs = [pl.BlockSpec((TM, D_MODEL), row)] * 5 + [
        pl.BlockSpec((None, None, 6, D_MODEL), mod_idx),
        pl.BlockSpec((None, 1, D_MODEL), lambda i: (l, 0, 0)),
        wspec, wspec, wspec]
    args = [x, out_r, out_a, gr, ga, mods, norm2, wr_bf, wa_bf, wo_bf]
    out_specs = [pl.BlockSpec((TM, D_MODEL), row)]
    out_shape = [jax.ShapeDtypeStruct((n, D_MODEL), F32)]
    scratch = []
    if router is None:
        out_specs.append(pl.BlockSpec((TM, D_MODEL), row))
        out_shape.append(jax.ShapeDtypeStruct((n, D_MODEL), BF16))
    else:
        j, wr_t, br_pad = router
        in_specs += [pl.BlockSpec((None, N_EXPERTS, D_MODEL), lambda i: (j, 0, 0)),
                     pl.BlockSpec((None, 1, LANES), lambda i: (j, 0, 0))]
        args += [wr_t, br_pad]
        out_specs += [pl.BlockSpec(memory_space=pl.ANY),
                      pl.BlockSpec((TM, LANES), row),
                      pl.BlockSpec((1, LANES), lambda i: (0, 0))]
        out_shape += [jax.ShapeDtypeStruct((n, D_MODEL // LANES, LANES), F32),
                      jax.ShapeDtypeStruct((n, LANES), F32),
                      jax.ShapeDtypeStruct((1, LANES), F32)]
        scratch = [pltpu.VMEM((1, LANES), F32), pltpu.VMEM((2, TM, D_MODEL), F32),
                   pltpu.SemaphoreType.DMA((2,))]
    return pl.pallas_call(
        functools.partial(_mix_out_kernel, route=router is not None),
        grid=(n // TM,),
        in_specs=in_specs,
        out_specs=out_specs,
        out_shape=out_shape,
        scratch_shapes=scratch,
        compiler_params=_cparams(("arbitrary",)),
        name=("mix_route" if router is not None else "mix_out") + ("_lat" if lat else "_ctx"),
    )(*args)


def _ffn_kernel(x_ref, h_ref, mod_ref, wg_ref, wu_ref, wd_ref, o_ref, *, f_chunk):
    h = h_ref[...]
    d_ff = wg_ref.shape[1]
    acc = jnp.zeros((h.shape[0], D_MODEL), F32)
    for c in range(d_ff // f_chunk):
        sl = slice(c * f_chunk, (c + 1) * f_chunk)
        act = _silu(_dot(h, wg_ref[:, sl])) * _dot(h, wu_ref[:, sl])
        acc = acc + _dot(act.astype(BF16), wd_ref[sl, :])
    o_ref[...] = x_ref[...] + mod_ref[5:6, :] * acc


def _ffn(x1, h2, mods, l, j, wg_bf, wu_bf, wd_bf, seq_len, lat):
    n = x1.shape[0]
    d_ff = wg_bf.shape[2]
    tiles_per_seq = seq_len // TM
    if lat:
        mod_idx = lambda i: (l, 1 + i // tiles_per_seq, 0, 0)
    else:
        mod_idx = lambda i: (l, 0, 0, 0)
    row = lambda i: (i, 0)
    return pl.pallas_call(
        functools.partial(_ffn_kernel, f_chunk=d_ff // 2),
        grid=(n // TM,),
        in_specs=[pl.BlockSpec((TM, D_MODEL), row), pl.BlockSpec((TM, D_MODEL), row),
                  pl.BlockSpec((None, None, 6, D_MODEL), mod_idx),
                  pl.BlockSpec((None, D_MODEL, d_ff), lambda i: (j, 0, 0), pipeline_mode=RESIDENT),
                  pl.BlockSpec((None, D_MODEL, d_ff), lambda i: (j, 0, 0), pipeline_mode=RESIDENT),
                  pl.BlockSpec((None, d_ff, D_MODEL), lambda i: (j, 0, 0), pipeline_mode=RESIDENT)],
        out_specs=pl.BlockSpec((TM, D_MODEL), row),
        out_shape=jax.ShapeDtypeStruct((n, D_MODEL), F32),
        compiler_params=_cparams(("arbitrary",)),
        name="ffn_lat" if lat else "ffn_ctx",
    )(x1, h2, mods, wg_bf, wu_bf, wd_bf)


def _group_layout(rec, counts, n_tiles):
    as_int = lambda k: rec[:, k].astype(jnp.int32)
    cnt = counts[0, :N_EXPERTS].astype(jnp.int32)
    tiles = (cnt + TM_GROUP - 1) // TM_GROUP
    cum = jnp.cumsum(tiles)
    start = (cum - tiles) * TM_GROUP
    experts = jnp.arange(N_EXPERTS, dtype=jnp.int32)

    def pos(e, rank):
        return jnp.sum(jnp.where(e[:, None] == experts[None, :], start[None, :], 0), axis=1) + rank

    pos2 = jnp.stack([pos(as_int(R_E1), as_int(R_RANK1)), pos(as_int(R_E2), as_int(R_RANK2))])
    n_active = cum[N_EXPERTS - 1]
    tile_expert = jnp.sum(jnp.arange(n_tiles, dtype=jnp.int32)[:, None] >= cum[None, :], axis=1)
    tile_expert = jnp.minimum(tile_expert, tile_expert[n_active - 1]).astype(jnp.int32)
    tile_ids = jnp.arange(n_tiles, dtype=jnp.int32)
    group_last = jnp.any((tile_ids[:, None] == cum[None, :] - 1) & (tiles[None, :] > 0), axis=1)
    zero_tile = (group_last | (tile_ids >= n_active)).astype(jnp.int32)
    return (pos2.astype(jnp.int32), tile_expert, n_active.reshape(1).astype(jnp.int32), zero_tile)


def _row_copy(src, dst, sem):
    return pltpu.make_async_copy(src, dst, sem)


def _dispatch_kernel(pos_ref, zt_ref, h_ref, xs_ref, zbuf, sem, zsem):
    n_chunks = h_ref.shape[0] // DMA_CHUNK
    n_tiles = xs_ref.shape[0] // TM_GROUP

    zbuf[...] = jnp.zeros_like(zbuf)

    def zero_fill(j):
        return pltpu.make_async_copy(zbuf, xs_ref.at[pl.ds(j * TM_GROUP, TM_GROUP)], zsem)

    def zero_start(j, carry):
        @pl.when(zt_ref[j] != 0)
        def _():
            zero_fill(j).start()
        return carry

    def zero_wait(j, carry):
        @pl.when(zt_ref[j] != 0)
        def _():
            zero_fill(j).wait()
        return carry

    lax.fori_loop(0, n_tiles, zero_start, 0)
    lax.fori_loop(0, n_tiles, zero_wait, 0)

    def drain_chunk():
        rows = pl.ds(0, 2 * DMA_CHUNK)
        _row_copy(xs_ref.at[rows], xs_ref.at[rows], sem).wait()

    def chunk(k, carry):
        def issue(r, c):
            t = k * DMA_CHUNK + r
            _row_copy(h_ref.at[t], xs_ref.at[pos_ref[0, t]], sem).start()
            _row_copy(h_ref.at[t], xs_ref.at[pos_ref[1, t]], sem).start()
            return c

        lax.fori_loop(0, DMA_CHUNK, issue, 0, unroll=DMA_UNROLL)

        @pl.when(k > 0)
        def _():
            drain_chunk()

        return carry

    lax.fori_loop(0, n_chunks, chunk, 0)
    drain_chunk()


def _dispatch(pos2, zero_tile, h2_rows, n_rows):
    tile = h2_rows.shape[1:]
    return pl.pallas_call(
        _dispatch_kernel,
        grid_spec=pltpu.PrefetchScalarGridSpec(
            num_scalar_prefetch=2, grid=(1,),
            in_specs=[pl.BlockSpec(h2_rows.shape, lambda i, p, z: (0, 0, 0))],
            out_specs=pl.BlockSpec(memory_space=pl.ANY),
            scratch_shapes=[pltpu.VMEM((TM_GROUP,) + tile, F32),
                            pltpu.SemaphoreType.DMA(()), pltpu.SemaphoreType.DMA(())]),
        out_shape=jax.ShapeDtypeStruct((n_rows,) + tile, F32),
        compiler_params=_cparams(("arbitrary",)),
        name="moe_dispatch",
    )(pos2, zero_tile, h2_rows)


def _expert_kernel(te_ref, na_ref, xs_ref, wg_ref, wu_ref, wd_ref, ys_ref,
                   xbuf, ybuf, sem_in, sem_out):
    del te_ref
    i = pl.program_id(0)
    last = pl.num_programs(0) - 1
    n_active = na_ref[0]
    slot = i % 2
    n_col = D_MODEL // LANES

    def in_copies(tile, s):
        rows = pl.ds(tile * TM_GROUP, TM_GROUP)
        return [pltpu.make_async_copy(xs_ref.at[rows, c], xbuf.at[s, :, c * LANES:(c + 1) * LANES],
                                      sem_in.at[s]) for c in range(n_col)]

    def out_copies(tile, s):
        rows = pl.ds(tile * TM_GROUP, TM_GROUP)
        return [pltpu.make_async_copy(ybuf.at[s, :, c * LANES:(c + 1) * LANES], ys_ref.at[rows, c],
                                      sem_out.at[s]) for c in range(n_col)]

    @pl.when(i == 0)
    def _():
        for cp in in_copies(0, 0):
            cp.start()

    @pl.when(i + 1 < n_active)
    def _():
        for cp in in_copies(i + 1, 1 - slot):
            cp.start()

    @pl.when(i >= 2)
    def _():
        for cp in out_copies(i - 2, slot):
            cp.wait()

    @pl.when(i < n_active)
    def _():
        for cp in in_copies(i, slot):
            cp.wait()
        xb = xbuf[slot].astype(BF16)
        act = _silu(_dot(xb, wg_ref[...])) * _dot(xb, wu_ref[...])
        ybuf[slot] = _dot(act.astype(BF16), wd_ref[...])

    @pl.when(i >= n_active)
    def _():
        ybuf[slot] = jnp.zeros((TM_GROUP, D_MODEL), F32)

    for cp in out_copies(i, slot):
        cp.start()

    @pl.when(i == last)
    def _():
        for cp in out_copies(i, slot):
            cp.wait()

        @pl.when(i >= 1)
        def _():
            for cp in out_copies(i - 1, 1 - slot):
                cp.wait()


def _experts(tile_expert, n_active, xs, j, wg_bf, wu_bf, wd_bf):
    n_rows = xs.shape[0]
    d_e = wg_bf.shape[3]
    wspec = lambda shape: pl.BlockSpec((None, None) + shape, lambda i, te, na: (j, te[i], 0, 0))
    return pl.pallas_call(
        _expert_kernel,
        grid_spec=pltpu.PrefetchScalarGridSpec(
            num_scalar_prefetch=2, grid=(n_rows // TM_GROUP,),
            in_specs=[pl.BlockSpec(memory_space=pl.ANY),
                      wspec((D_MODEL, d_e)), wspec((D_MODEL, d_e)), wspec((d_e, D_MODEL))],
            out_specs=pl.BlockSpec(memory_space=pl.ANY),
            scratch_shapes=[pltpu.VMEM((2, TM_GROUP, D_MODEL), F32),
                            pltpu.VMEM((2, TM_GROUP, D_MODEL), F32),
                            pltpu.SemaphoreType.DMA((2,)), pltpu.SemaphoreType.DMA((2,))]),
        out_shape=jax.ShapeDtypeStruct(xs.shape, F32),
        compiler_params=_cparams(("arbitrary",)),
        name="moe_experts",
    )(tile_expert, n_active, xs, wg_bf, wu_bf, wd_bf)


def _combine_kernel(*refs, final):
    pos_ref, y_ref, x_ref, rec_ref, mod_ref = refs[:5]
    g_ref = refs[5] if final else None
    o_ref, a1, a2, b1, b2, sem_a, sem_b = refs[5 + int(final):]
    i = pl.program_id(0)

    def issue(half_idx, d1, d2, sem):
        base = half_idx * DMA_CHUNK

        def body(r, c):
            _row_copy(y_ref.at[pos_ref[0, base + r]], d1.at[r], sem).start()
            _row_copy(y_ref.at[pos_ref[1, base + r]], d2.at[r], sem).start()
            return c

        lax.fori_loop(0, DMA_CHUNK, body, 0, unroll=DMA_UNROLL)

    def drain(d1, d2, sem):
        rows = pl.ds(0, DMA_CHUNK)
        _row_copy(y_ref.at[rows], d1, sem).wait()
        _row_copy(y_ref.at[rows], d2, sem).wait()

    def finish(half, d1, d2):
        rows = slice(half * DMA_CHUNK, (half + 1) * DMA_CHUNK)
        w1 = rec_ref[rows, R_W1:R_W1 + 1]
        w2 = rec_ref[rows, R_W2:R_W2 + 1]
        for c in range(D_MODEL // LANES):
            sl = slice(c * LANES, (c + 1) * LANES)
            y = w1 * d1[:, c, :] + w2 * d2[:, c, :]
            o_ref[rows, sl] = x_ref[rows, sl] + mod_ref[5:6, sl] * y
        if final:
            o_ref[rows, :] = _rms(o_ref[rows, :], g_ref[...])

    @pl.when(i == 0)
    def _():
        issue(0, a1, a2, sem_a)

    issue(2 * i + 1, b1, b2, sem_b)
    drain(a1, a2, sem_a)
    finish(0, a1, a2)

    @pl.when(i + 1 < pl.num_programs(0))
    def _():
        issue(2 * i + 2, a1, a2, sem_a)

    drain(b1, b2, sem_b)
    finish(1, b1, b2)


def _combine(pos2, ys, x1, rec, mods, l, seq_len, lat, final_g):
    n = x1.shape[0]
    final = final_g is not None
    tile = ys.shape[1:]
    tm = 2 * DMA_CHUNK
    tiles_per_seq = max(seq_len // tm, 1)
    if lat:
        mod_idx = lambda i, p: (l, 1 + i // tiles_per_seq, 0, 0)
    else:
        mod_idx = lambda i, p: (l, 0, 0, 0)
    row = lambda i, p: (i, 0)
    slot = pltpu.VMEM((DMA_CHUNK,) + tile, F32)
    in_specs = [pl.BlockSpec(memory_space=pl.ANY),
                pl.BlockSpec((tm, D_MODEL), row),
                pl.BlockSpec((tm, LANES), row),
                pl.BlockSpec((None, None, 6, D_MODEL), mod_idx)]
    args = [pos2, ys, x1, rec, mods]
    if final:
        in_specs.append(pl.BlockSpec((1, D_MODEL), lambda i, p: (0, 0)))
        args.append(final_g)
    return pl.pallas_call(
        functools.partial(_combine_kernel, final=final),
        grid_spec=pltpu.PrefetchScalarGridSpec(
            num_scalar_prefetch=1, grid=(n // tm,),
            in_specs=in_specs,
            out_specs=pl.BlockSpec((tm, D_MODEL), row),
            scratch_shapes=[slot, slot, slot, slot,
                            pltpu.SemaphoreType.DMA(()), pltpu.SemaphoreType.DMA(())]),
        out_shape=jax.ShapeDtypeStruct((n, D_MODEL), F32),
        compiler_params=_cparams(("arbitrary",)),
        name="moe_combine_final" if final else "moe_combine",
    )(*args)


def _moe(x1, h2_rows, rec, counts, mods, l, j, wg_bf, wu_bf, wd_bf, seq_len, lat, final_g):
    n = x1.shape[0]
    n_tiles = 2 * n // TM_GROUP + N_EXPERTS
    pos2, tile_expert, n_active, zero_tile = _group_layout(rec, counts, n_tiles)
    xs = _dispatch(pos2, zero_tile, h2_rows, n_tiles * TM_GROUP)
    ys = _experts(tile_expert, n_active, xs, j, wg_bf, wu_bf, wd_bf)
    return _combine(pos2, ys, x1, rec, mods, l, seq_len, lat, final_g)


def _final_norm_kernel(x_ref, g_ref, o_ref):
    o_ref[...] = _rms(x_ref[...], g_ref[...])


def _final_norm(x, g):
    n = x.shape[0]
    row = lambda i: (i, 0)
    return pl.pallas_call(
        _final_norm_kernel,
        grid=(n // TM_FF,),
        in_specs=[pl.BlockSpec((TM_FF, D_MODEL), row), pl.BlockSpec((1, D_MODEL), lambda i: (0, 0))],
        out_specs=pl.BlockSpec((TM_FF, D_MODEL), row),
        out_shape=jax.ShapeDtypeStruct((n, D_MODEL), F32),
        compiler_params=_cparams(("arbitrary",)),
        name="final_norm",
    )(x, g)


def _rope_tables(t_len):
    t = jnp.arange(t_len)
    pos = jnp.stack([t // GRID_W, t % GRID_W], axis=-1).astype(F32)
    inv = ROPE_THETA ** (-jnp.arange(0, AXIS_DIM, 2, dtype=F32) / AXIS_DIM)
    ang = pos[..., None] * inv
    cos, sin = jnp.cos(ang), jnp.sin(ang)
    c = jnp.concatenate([cos[:, 0], cos[:, 0], cos[:, 1], cos[:, 1]], axis=-1)
    s = jnp.concatenate([-sin[:, 0], sin[:, 0], -sin[:, 1], sin[:, 1]], axis=-1)
    return c, s


def _gate_weights(wa, wi):
    depth = wa.shape[0]
    groups = D_RNN // LANES

    def pair(w):
        w = w.reshape(depth, 2, groups, 2, RNN_BLOCK, RNN_BLOCK)
        z = jnp.zeros_like(w[:, :, :, 0])
        top = jnp.concatenate([w[:, :, :, 0], z], axis=-1)
        bot = jnp.concatenate([z, w[:, :, :, 1]], axis=-1)
        return jnp.concatenate([top, bot], axis=-2)

    w = jnp.concatenate([pair(wa), pair(wi)], axis=-1)
    return (0.5 * jnp.concatenate([w[:, 0], w[:, 1]], axis=-1)).astype(BF16)


def kernel(x_prompt, x_sample, cache_k, cache_v, state_rnn, c, c_ctx, w_mod, b_mod, norm1, norm2, w_in, conv_w, conv_b, lru_wa, lru_ba, lru_wi, lru_bi, lru_lambda, q_norm, k_norm, w_rnn_out, w_attn_out, w_out, ff_gate, ff_up, ff_down, router_w, router_b, exp_gate, exp_up, exp_down, final_norm):
    batch, seq, _ = x_prompt.shape
    dec_batch, dec_seq, _ = x_sample.shape
    depth = w_mod.shape[0]
    past = cache_k.shape[2]
    assert dec_batch + 1 <= MOD_ROWS

    cond8 = jnp.zeros((MOD_ROWS, D_MODEL), F32).at[0].set(c_ctx).at[1:1 + dec_batch].set(c)
    mods = _modulation(cond8, w_mod, b_mod).reshape(depth, MOD_ROWS, 6, D_MODEL)

    bf = lambda w: w.astype(BF16)
    w_in_bf, wr_bf, wa_bf, wo_bf = bf(w_in), bf(w_rnn_out), bf(w_attn_out), bf(w_out)
    ffg_bf, ffu_bf, ffd_bf = bf(ff_gate), bf(ff_up), bf(ff_down)
    eg_bf, eu_bf, ed_bf = bf(exp_gate), bf(exp_up), bf(exp_down)
    gate_w = _gate_weights(lru_wa, lru_wi)
    n_moe = router_w.shape[0]
    wr_t = jnp.swapaxes(router_w, 1, 2)
    br_pad = jnp.zeros((n_moe, 1, LANES), F32).at[:, 0, :N_EXPERTS].set(router_b)

    norm1_3 = norm1.reshape(depth, 1, D_MODEL)
    norm2_3 = norm2.reshape(depth, 1, D_MODEL)
    qn3 = q_norm.reshape(depth, 1, HEAD_DIM)
    kn3 = k_norm.reshape(depth, 1, HEAD_DIM)
    conv_b3 = conv_b.reshape(depth, 1, D_RNN)
    ba4 = (0.5 * lru_ba).reshape(depth, 2, 1, D_RNN)
    bi4 = (0.5 * lru_bi).reshape(depth, 2, 1, D_RNN)
    lam4 = lru_lambda.reshape(depth, 2, 1, D_RNN)
    state5 = state_rnn.reshape(dec_batch, depth, 2, 1, D_RNN)
    cache_k4 = cache_k.reshape(dec_batch, depth, past, KV_DIM)
    cache_v4 = cache_v.reshape(dec_batch, depth, past, KV_DIM)
    rope_tabs = _rope_tables(dec_seq)
    final_g = final_norm.reshape(1, D_MODEL)

    def run_stream(x, lat):
        seq_len = dec_seq if lat else seq
        ks, vs, ss = [], [], []
        for l in range(depth):
            xr, yr, q, kn, kr, v, gr, ga = _in_proj(
                x, mods, l, norm1_3, w_in_bf, qn3, kn3, rope_tabs if lat else None, seq_len)
            out_r, end_f, end_b = _rnn(xr, yr, l, conv_w, conv_b3, gate_w, ba4, bi4, lam4,
                                       state5 if lat else None, seq_len)
            if lat:
                out_a = _attn_lat(q, kr, v, cache_k4, cache_v4, l, seq_len)
            else:
                out_a = _attn_ctx(q, kr, v, seq_len)
                ks.append(kn)
                vs.append(v)
                ss.append(jnp.stack([end_f, end_b], axis=1))
            moe_layer = l % 2 == 1
            mix_args = (x, out_r, out_a, gr, ga, mods, l, norm2_3, wr_bf, wa_bf, wo_bf, seq_len, lat)
            if moe_layer:
                x1, h2_rows, rec, counts = _mix_out(*mix_args, router=(l // 2, wr_t, br_pad))
                x = _moe(x1, h2_rows, rec, counts, mods, l, l // 2, eg_bf, eu_bf, ed_bf,
                         seq_len, lat, final_g if l == depth - 1 else None)
            else:
                x1, h2 = _mix_out(*mix_args)
                x = _ffn(x1, h2, mods, l, l // 2, ffg_bf, ffu_bf, ffd_bf, seq_len, lat)
                if l == depth - 1:
                    x = _final_norm(x, final_g)
        return x, ks, vs, ss

    y_ctx, ks, vs, ss = run_stream(x_prompt.reshape(batch * seq, D_MODEL), False)
    y_lat, _, _, _ = run_stream(x_sample.reshape(dec_batch * dec_seq, D_MODEL), True)

    new_k = jnp.stack([k.reshape(batch, seq, N_KV_HEADS, HEAD_DIM) for k in ks], axis=1)
    new_v = jnp.stack([v.reshape(batch, seq, N_KV_HEADS, HEAD_DIM) for v in vs], axis=1)
    new_state = jnp.stack(ss, axis=1)
    return (y_ctx.reshape(batch, seq, D_MODEL), y_lat.reshape(dec_batch, dec_seq, D_MODEL),
            new_k, new_v, new_state)
```

```python
import functools

import jax
import jax.numpy as jnp
from jax import lax
from jax.experimental import pallas as pl
from jax.experimental.pallas import tpu as pltpu

F32 = jnp.float32
BF16 = jnp.bfloat16

D_MODEL = 1024
D_RNN = 1024
N_RNN_BLOCKS = 16
RNN_BLOCK = D_RNN // N_RNN_BLOCKS
CONV_W = 4
CONV_LEFT = 2
LRU_C = 8.0
N_HEADS = 8
N_KV_HEADS = 2
HEAD_DIM = 128
AXIS_DIM = HEAD_DIM // 2
ROPE_THETA = 10000.0
GRID_W = 64
Q_DIM = N_HEADS * HEAD_DIM
KV_DIM = N_KV_HEADS * HEAD_DIM
N_EXPERTS = 8
EPS = 1e-6
LOG2_E = 1.4426950408889634
RSQRT_FLOOR = 1e-30
D_IN = 2 * D_RNN + Q_DIM + 2 * KV_DIM + 2 * D_MODEL
O_XR, O_YR, O_Q = 0, D_RNN, 2 * D_RNN
O_K = O_Q + Q_DIM
O_V = O_K + KV_DIM
O_GR = O_V + KV_DIM
O_GA = O_GR + D_MODEL

LANES = 128
SUBLANES = 8
MOD_ROWS = 8
VMEM_LIMIT = 56 * 1024 * 1024

TM = 512
TM_ROPE = 256
RESIDENT = pl.Buffered(1)
RNN_T = 256
RNN_SEG = SUBLANES
RNN_C = 256
RNN_TCH = 32
RNN_SCAN_UNROLL = 8
TQ = 256
TM_FF = 512
TM_GROUP = 256
DMA_CHUNK = 256


def _cparams(sem):
    return pltpu.CompilerParams(dimension_semantics=sem, vmem_limit_bytes=VMEM_LIMIT)


def _sigmoid(x):
    return 1.0 / (1.0 + jnp.exp(-x))


def _silu(x):
    return x * _sigmoid(x)


def _gelu_tanh(x):
    return 0.5 * x * (1.0 + jnp.tanh(0.7978845608028654 * (x + 0.044715 * (x * x * x))))


def _rms(x, g):
    ms = jnp.mean(x * x, axis=-1, keepdims=True)
    return x * lax.rsqrt(ms + EPS) * g


def _dot(a, b):
    return jnp.dot(a, b, preferred_element_type=F32)


def _mod_kernel(cond_ref, w_ref, b_ref, o_ref):
    s = _silu(cond_ref[...]).astype(BF16)
    o_ref[...] = _dot(s, w_ref[...].astype(BF16)) + b_ref[...]


def _modulation(cond8, w_mod, b_mod):
    depth = w_mod.shape[0]
    tn = 1536
    return pl.pallas_call(
        _mod_kernel,
        grid=(depth, 6 * D_MODEL // tn),
        in_specs=[
            pl.BlockSpec((MOD_ROWS, D_MODEL), lambda l, j: (0, 0)),
            pl.BlockSpec((None, D_MODEL, tn), lambda l, j: (l, 0, j)),
            pl.BlockSpec((None, 1, tn), lambda l, j: (l, 0, j)),
        ],
        out_specs=pl.BlockSpec((None, MOD_ROWS, tn), lambda l, j: (l, 0, j)),
        out_shape=jax.ShapeDtypeStruct((depth, MOD_ROWS, 6 * D_MODEL), F32),
        compiler_params=_cparams(("arbitrary", "arbitrary")),
        name="adaln_mod",
    )(cond8, w_mod, b_mod.reshape(depth, 1, 6 * D_MODEL))


def _swap_halves(x):
    lane = lax.broadcasted_iota(jnp.int32, x.shape, 1)
    return jnp.where((lane % AXIS_DIM) < AXIS_DIM // 2,
                     pltpu.roll(x, LANES - AXIS_DIM // 2, axis=1),
                     pltpu.roll(x, AXIS_DIM // 2, axis=1))


def _in_proj_kernel(*refs, rope):
    if rope:
        (x_ref, mod_ref, n1_ref, w_ref, qn_ref, kn_ref, c_ref, s_ref,
         xr_ref, yr_ref, q_ref, kno_ref, kro_ref, v_ref, gr_ref, ga_ref) = refs
    else:
        (x_ref, mod_ref, n1_ref, w_ref, qn_ref, kn_ref,
         xr_ref, yr_ref, q_ref, kno_ref, kro_ref, v_ref, gr_ref, ga_ref) = refs
    x = x_ref[...]
    h = _rms(x, n1_ref[...]) * (1.0 + mod_ref[1:2, :]) + mod_ref[0:1, :]
    hb = h.astype(BF16)

    xr_ref[...] = _dot(hb, w_ref[:, O_XR:O_XR + D_RNN])
    yr_ref[...] = _dot(hb, w_ref[:, O_YR:O_YR + D_RNN])
    gr_ref[...] = _dot(hb, w_ref[:, O_GR:O_GR + D_MODEL])
    ga_ref[...] = _dot(hb, w_ref[:, O_GA:O_GA + D_MODEL])
    v_ref[...] = _dot(hb, w_ref[:, O_V:O_V + KV_DIM])

    if rope:
        cs, sn = c_ref[...], s_ref[...]
    zq = _dot(hb, w_ref[:, O_Q:O_Q + Q_DIM])
    for hd in range(N_HEADS):
        t = _rms(zq[:, hd * HEAD_DIM:(hd + 1) * HEAD_DIM], qn_ref[...])
        if rope:
            t = t * cs + _swap_halves(t) * sn
        q_ref[:, hd * HEAD_DIM:(hd + 1) * HEAD_DIM] = t.astype(BF16)
    zk = _dot(hb, w_ref[:, O_K:O_K + KV_DIM])
    for hd in range(N_KV_HEADS):
        t = _rms(zk[:, hd * HEAD_DIM:(hd + 1) * HEAD_DIM], kn_ref[...])
        kno_ref[:, hd * HEAD_DIM:(hd + 1) * HEAD_DIM] = t
        if rope:
            t = t * cs + _swap_halves(t) * sn
        kro_ref[:, hd * HEAD_DIM:(hd + 1) * HEAD_DIM] = t.astype(BF16)


def _in_proj(x, mods, l, norm1, w_in_bf, q_norm, k_norm, rope_tabs, seq_len):
    n = x.shape[0]
    rope = rope_tabs is not None
    tm = TM_ROPE if rope else TM
    tiles_per_seq = seq_len // tm
    if rope:
        mod_idx = lambda i: (l, 1 + i // tiles_per_seq, 0, 0)
    else:
        mod_idx = lambda i: (l, 0, 0, 0)
    row = lambda i: (i, 0)
    in_specs = [
        pl.BlockSpec((tm, D_MODEL), row),
        pl.BlockSpec((None, None, 6, D_MODEL), mod_idx),
        pl.BlockSpec((None, 1, D_MODEL), lambda i: (l, 0, 0)),
        pl.BlockSpec((None, D_MODEL, D_IN), lambda i: (l, 0, 0), pipeline_mode=RESIDENT),
        pl.BlockSpec((None, 1, HEAD_DIM), lambda i: (l, 0, 0)),
        pl.BlockSpec((None, 1, HEAD_DIM), lambda i: (l, 0, 0)),
    ]
    args = [x, mods, norm1, w_in_bf, q_norm, k_norm]
    if rope:
        in_specs += [pl.BlockSpec((tm, HEAD_DIM), lambda i: (i % tiles_per_seq, 0))] * 2
        args += list(rope_tabs)
    widths = [(D_RNN, F32), (D_RNN, F32), (Q_DIM, BF16), (KV_DIM, F32), (KV_DIM, BF16),
              (KV_DIM, F32), (D_MODEL, F32), (D_MODEL, F32)]
    return pl.pallas_call(
        functools.partial(_in_proj_kernel, rope=rope),
        grid=(n // tm,),
        in_specs=in_specs,
        out_specs=[pl.BlockSpec((tm, w), row) for w, _ in widths],
        out_shape=[jax.ShapeDtypeStruct((n, w), dt) for w, dt in widths],
        compiler_params=_cparams(("arbitrary",)),
        name="in_proj_lat" if rope else "in_proj_ctx",
    )(*args)


def _rnn_kernel(*refs, chained):
    refs = list(refs)
    xr_ref, yr_ref, cw_ref, cb_ref, gw_ref, ba_ref, bi_ref, lam_ref = refs[:8]
    refs = refs[8:]
    h0_ref = refs.pop(0) if chained else None
    out_ref, sf_ref, sb_ref, xext, ybuf, obuf, a_f, b_f, a_b, b_b, sem_in, sem_out = refs

    g = pl.program_id(0)
    cb = pl.program_id(1)
    n_cb = pl.num_programs(1)
    step = g * n_cb + cb
    n_steps = pl.num_programs(0) * n_cb
    slot = step % 2

    def hbm_window(kk, s):
        rows = pl.ds(pl.multiple_of(((kk // n_cb) * RNN_SEG + s) * RNN_T, RNN_T), RNN_T)
        cols = pl.ds(pl.multiple_of((kk % n_cb) * RNN_C, RNN_C), RNN_C)
        return rows, cols

    def in_copies(kk, sl_):
        cps = []
        for s in range(RNN_SEG):
            win = hbm_window(kk, s)
            cps.append(pltpu.make_async_copy(
                xr_ref.at[win], xext.at[sl_, pl.ds(CONV_LEFT, RNN_T), s], sem_in.at[sl_]))
            cps.append(pltpu.make_async_copy(yr_ref.at[win], ybuf.at[sl_, :, s], sem_in.at[sl_]))
        return cps

    def out_copies(kk, sl_):
        return [pltpu.make_async_copy(obuf.at[sl_, :, s], out_ref.at[hbm_window(kk, s)],
                                      sem_out.at[sl_]) for s in range(RNN_SEG)]

    @pl.when(step == 0)
    def _():
        for cp in in_copies(0, 0):
            cp.start()

    @pl.when(step + 1 < n_steps)
    def _():
        for cp in in_copies(step + 1, 1 - slot):
            cp.start()

    for cp in in_copies(step, slot):
        cp.wait()

    seg = lax.broadcasted_iota(jnp.int32, (RNN_SEG, RNN_C), 0)
    zero_row = jnp.zeros((RNN_SEG, RNN_C), F32)
    if chained:
        xext[slot, 0] = jnp.where(seg == 0, zero_row, pltpu.roll(xext[slot, RNN_T], 1, axis=0))
        xext[slot, 1] = jnp.where(seg == 0, zero_row, pltpu.roll(xext[slot, RNN_T + 1], 1, axis=0))
        xext[slot, RNN_T + CONV_LEFT] = jnp.where(
            seg == RNN_SEG - 1, zero_row, pltpu.roll(xext[slot, CONV_LEFT], RNN_SEG - 1, axis=0))
    else:
        xext[slot, 0] = zero_row
        xext[slot, 1] = zero_row
        xext[slot, RNN_T + CONV_LEFT] = zero_row

    scans = ((a_f, b_f), (a_b, b_b))
    rows2d = RNN_TCH * RNN_SEG

    def gate_chunk(tc, carry):
        t0 = pl.multiple_of(tc * RNN_TCH, RNN_TCH)
        for j in range(RNN_C // LANES):
            sl = slice(j * LANES, (j + 1) * LANES)
            xc = jnp.broadcast_to(cb_ref[:, sl].reshape(1, 1, LANES), (RNN_TCH, RNN_SEG, LANES))
            for k in range(CONV_W):
                xc = xc + (xext[slot, pl.ds(t0 + k, RNN_TCH), :, sl]
                           * cw_ref[k:k + 1, sl].reshape(1, 1, LANES))
            xc = xc.reshape(rows2d, LANES)
            pre = _dot(xc.astype(BF16), gw_ref[j])
            for d, (a_s, b_s) in enumerate(scans):
                lam = lam_ref[d, :, sl]
                log2_decay = (-LRU_C * LOG2_E) * (jnp.maximum(-lam, 0.0)
                                                  + jnp.log(1.0 + jnp.exp(-jnp.abs(lam))))
                r = 0.5 * jnp.tanh(pre[:, 2 * d * LANES:(2 * d + 1) * LANES]
                                   + ba_ref[d, :, sl]) + 0.5
                gi = 0.5 * jnp.tanh(pre[:, (2 * d + 1) * LANES:(2 * d + 2) * LANES]
                                    + bi_ref[d, :, sl]) + 0.5
                a = jnp.exp2(r * log2_decay)
                y = 1.0 - a * a
                b = (y * lax.rsqrt(jnp.maximum(y, RSQRT_FLOOR))) * (gi * xc)
                a_s[pl.ds(t0, RNN_TCH), :, sl] = a.reshape(RNN_TCH, RNN_SEG, LANES)
                b_s[pl.ds(t0, RNN_TCH), :, sl] = b.reshape(RNN_TCH, RNN_SEG, LANES)
        return carry

    lax.fori_loop(0, RNN_T // RNN_TCH, gate_chunk, 0)

    def scan_step(t, carry):
        hf, hb, pf, pb = carry
        tb = RNN_T - 1 - t
        af_t = a_f[t]
        hf = af_t * hf + b_f[t]
        b_f[t] = hf
        ab_t = a_b[tb]
        hb = ab_t * hb + b_b[tb]
        b_b[tb] = hb
        if chained:
            pf = af_t * pf
            a_f[t] = pf
            pb = ab_t * pb
            a_b[tb] = pb
        return hf, hb, pf, pb

    ones = jnp.ones((RNN_SEG, RNN_C), F32)
    hf, hb, pf, pb = lax.fori_loop(0, RNN_T, scan_step, (zero_row, zero_row, ones, ones),
                                   unroll=RNN_SCAN_UNROLL)

    if chained:
        st_f = jnp.broadcast_to(h0_ref[0], (RNN_SEG, RNN_C))
        st_b = jnp.broadcast_to(h0_ref[1], (RNN_SEG, RNN_C))
        in_f, in_b = st_f, st_b
        for _ in range(RNN_SEG - 1):
            in_f = jnp.where(seg == 0, st_f, pltpu.roll(pf * in_f + hf, 1, axis=0))
            in_b = jnp.where(seg == RNN_SEG - 1, st_b,
                             pltpu.roll(pb * in_b + hb, RNN_SEG - 1, axis=0))
        sf_ref[...] = pf * in_f + hf
        sb_ref[...] = pb * in_b + hb
    else:
        sf_ref[...] = hf
        sb_ref[...] = hb

    @pl.when(step >= 2)
    def _():
        for cp in out_copies(step - 2, slot):
            cp.wait()

    def finish_chunk(tc, carry):
        rows = pl.ds(pl.multiple_of(tc * RNN_TCH, RNN_TCH), RNN_TCH)
        h_f, h_b = b_f[rows], b_b[rows]
        if chained:
            h_f = a_f[rows] * in_f.reshape(1, RNN_SEG, RNN_C) + h_f
            h_b = a_b[rows] * in_b.reshape(1, RNN_SEG, RNN_C) + h_b
        obuf[slot, rows] = (h_f + h_b) * _gelu_tanh(ybuf[slot, rows])
        return carry

    lax.fori_loop(0, RNN_T // RNN_TCH, finish_chunk, 0)

    for cp in out_copies(step, slot):
        cp.start()

    @pl.when(step == n_steps - 1)
    def _():
        for cp in out_copies(step, slot):
            cp.wait()

        @pl.when(step >= 1)
        def _():
            for cp in out_copies(step - 1, 1 - slot):
                cp.wait()


def _rnn(xr, yr, l, conv_w, conv_b, gate_w, lru_ba, lru_bi, lru_lambda, state5, seq_len):
    n = xr.shape[0]
    chained = state5 is not None
    group_len = RNN_SEG * RNN_T
    assert n % group_len == 0 and seq_len in ((group_len,) if chained else (RNN_T,))
    n_groups = n // group_len
    lanes_per_step = RNN_C // LANES
    vec = pl.BlockSpec((None, 2, 1, RNN_C), lambda g, c: (l, 0, 0, c))
    in_specs = [
        pl.BlockSpec(memory_space=pl.ANY),
        pl.BlockSpec(memory_space=pl.ANY),
        pl.BlockSpec((None, CONV_W, RNN_C), lambda g, c: (l, 0, c)),
        pl.BlockSpec((None, 1, RNN_C), lambda g, c: (l, 0, c)),
        pl.BlockSpec((None, lanes_per_step, LANES, 4 * LANES), lambda g, c: (l, c, 0, 0)),
        vec, vec, vec,
    ]
    args = [xr, yr, conv_w, conv_b, gate_w, lru_ba, lru_bi, lru_lambda]
    if chained:
        in_specs.append(pl.BlockSpec((None, None, 2, 1, RNN_C), lambda g, c: (g, l, 0, 0, c)))
        args.append(state5)
    state_spec = pl.BlockSpec((RNN_SEG, RNN_C), lambda g, c: (g, c))
    state_shape = jax.ShapeDtypeStruct((n_groups * RNN_SEG, D_RNN), F32)
    work = lambda t: pltpu.VMEM((t, RNN_SEG, RNN_C), F32)
    return pl.pallas_call(
        functools.partial(_rnn_kernel, chained=chained),
        grid=(n_groups, D_RNN // RNN_C),
        in_specs=in_specs,
        out_specs=[pl.BlockSpec(memory_space=pl.ANY), state_spec, state_spec],
        out_shape=[jax.ShapeDtypeStruct((n, D_RNN), F32), state_shape, state_shape],
        scratch_shapes=[
            pltpu.VMEM((2, RNN_T + CONV_W - 1, RNN_SEG, RNN_C), F32),
            pltpu.VMEM((2, RNN_T, RNN_SEG, RNN_C), F32),
            pltpu.VMEM((2, RNN_T, RNN_SEG, RNN_C), F32),
            work(RNN_T), work(RNN_T), work(RNN_T), work(RNN_T),
            pltpu.SemaphoreType.DMA((2,)), pltpu.SemaphoreType.DMA((2,)),
        ],
        compiler_params=_cparams(("arbitrary", "arbitrary")),
        name="rnn_lat" if chained else "rnn_ctx",
    )(*args)


def _attend_heads(q_ref, k_all, v_all, o_ref, stack_heads):
    exp2_scale = (1.0 / (HEAD_DIM ** 0.5)) * LOG2_E
    group = N_HEADS // N_KV_HEADS
    tq = q_ref.shape[0]
    for kh in range(N_KV_HEADS):
        k = k_all[:, kh * HEAD_DIM:(kh + 1) * HEAD_DIM]
        v = v_all[:, kh * HEAD_DIM:(kh + 1) * HEAD_DIM]
        heads = [slice((kh * group + g) * HEAD_DIM, (kh * group + g + 1) * HEAD_DIM)
                 for g in range(group)]
        stacks = [heads] if stack_heads else [[sl] for sl in heads]
        for stack in stacks:
            q = jnp.concatenate([q_ref[:, sl] for sl in stack], axis=0)
            s = lax.dot_general(q, k, (((1,), (1,)), ((), ())), preferred_element_type=F32)
            m = jnp.max(s, axis=-1, keepdims=True)
            p = jnp.exp2((s - m) * exp2_scale)
            denom = jnp.sum(p, axis=-1, keepdims=True)
            o = (_dot(p.astype(BF16), v) / denom).astype(BF16)
            for g, sl in enumerate(stack):
                o_ref[:, sl] = o[g * tq:(g + 1) * tq]


def _attn_ctx_kernel(q_ref, k_ref, v_ref, o_ref):
    _attend_heads(q_ref, k_ref[...], v_ref[...].astype(BF16), o_ref, stack_heads=True)


def _attn_ctx(q, kr, v, seq_len):
    n = q.shape[0]
    row = lambda b: (b, 0)
    return pl.pallas_call(
        _attn_ctx_kernel,
        grid=(n // seq_len,),
        in_specs=[pl.BlockSpec((seq_len, Q_DIM), row),
                  pl.BlockSpec((seq_len, KV_DIM), row),
                  pl.BlockSpec((seq_len, KV_DIM), row)],
        out_specs=pl.BlockSpec((seq_len, Q_DIM), row),
        out_shape=jax.ShapeDtypeStruct((n, Q_DIM), BF16),
        compiler_params=_cparams(("arbitrary",)),
        name="attn_ctx",
    )(q, kr, v)


def _attn_lat_kernel(q_ref, kc_ref, vc_ref, kl_ref, vl_ref, o_ref, k_all, v_all, *, past):
    @pl.when(pl.program_id(1) == 0)
    def _():
        k_all[0:past, :] = kc_ref[...].astype(BF16)
        k_all[past:, :] = kl_ref[...]
        v_all[0:past, :] = vc_ref[...].astype(BF16)
        v_all[past:, :] = vl_ref[...].astype(BF16)

    _attend_heads(q_ref, k_all[...], v_all[...], o_ref, stack_heads=False)


def _attn_lat(q, kr, v, cache_k4, cache_v4, l, seq_len):
    n = q.shape[0]
    past = cache_k4.shape[2]
    nq = seq_len // TQ
    return pl.pallas_call(
        functools.partial(_attn_lat_kernel, past=past),
        grid=(n // seq_len, nq),
        in_specs=[pl.BlockSpec((TQ, Q_DIM), lambda b, j: (b * nq + j, 0)),
                  pl.BlockSpec((None, None, past, KV_DIM), lambda b, j: (b, l, 0, 0)),
                  pl.BlockSpec((None, None, past, KV_DIM), lambda b, j: (b, l, 0, 0)),
                  pl.BlockSpec((seq_len, KV_DIM), lambda b, j: (b, 0)),
                  pl.BlockSpec((seq_len, KV_DIM), lambda b, j: (b, 0))],
        out_specs=pl.BlockSpec((TQ, Q_DIM), lambda b, j: (b * nq + j, 0)),
        out_shape=jax.ShapeDtypeStruct((n, Q_DIM), BF16),
        scratch_shapes=[pltpu.VMEM((past + seq_len, KV_DIM), BF16),
                        pltpu.VMEM((past + seq_len, KV_DIM), BF16)],
        compiler_params=_cparams(("arbitrary", "arbitrary")),
        name="attn_lat",
    )(q, cache_k4, cache_v4, kr, v)


R_E1, R_E2, R_W1, R_W2, R_RANK1, R_RANK2 = range(6)


def _route(h2, wrt_ref, brt_ref, count_ref):
    shape = (h2.shape[0], LANES)
    lane = lax.broadcasted_iota(jnp.int32, shape, 1)
    neg = jnp.float32(-jnp.inf)
    logits = jnp.full(shape, neg, F32)
    for e in range(N_EXPERTS):
        prod = h2 * wrt_ref[e:e + 1, :]
        part = prod[:, 0:LANES]
        for c in range(1, D_MODEL // LANES):
            part = part + prod[:, c * LANES:(c + 1) * LANES]
        logits = jnp.where(lane == e, jnp.sum(part, axis=-1, keepdims=True), logits)
    logits = logits + brt_ref[...]
    v1 = jnp.max(logits, axis=-1, keepdims=True)
    i1 = jnp.min(jnp.where(logits == v1, lane, LANES), axis=-1, keepdims=True)
    rest = jnp.where(lane == i1, neg, logits)
    v2 = jnp.max(rest, axis=-1, keepdims=True)
    i2 = jnp.min(jnp.where(rest == v2, lane, LANES), axis=-1, keepdims=True)
    e2 = jnp.exp(v2 - v1)
    w1 = 1.0 / (1.0 + e2)
    w2 = e2 / (1.0 + e2)
    chosen = jnp.where((lane == i1) | (lane == i2), 1.0, 0.0)
    r_id = lax.broadcasted_iota(jnp.int32, (shape[0], shape[0]), 0)
    c_id = lax.broadcasted_iota(jnp.int32, (shape[0], shape[0]), 1)
    before = jnp.where(c_id < r_id, 1.0, 0.0).astype(BF16)
    rank = _dot(before, chosen.astype(BF16)) + count_ref[...]
    rank1 = jnp.sum(jnp.where(lane == i1, rank, 0.0), axis=-1, keepdims=True)
    rank2 = jnp.sum(jnp.where(lane == i2, rank, 0.0), axis=-1, keepdims=True)
    count_ref[...] += jnp.sum(chosen, axis=0, keepdims=True)
    rec = jnp.zeros(shape, F32)
    for k, val in ((R_E1, i1.astype(F32)), (R_E2, i2.astype(F32)), (R_W1, w1), (R_W2, w2),
                   (R_RANK1, rank1), (R_RANK2, rank2)):
        rec = jnp.where(lane == k, val, rec)
    return rec


def _mix_out_kernel(*refs, route):
    (x_ref, r_ref, a_ref, gr_ref, ga_ref, mod_ref, n2_ref, wr_ref, wa_ref, wo_ref) = refs[:10]
    if route:
        wrt_ref, brt_ref, x1_ref, h2_ref, rec_ref, cnt_ref, count_ref, hbuf, hsem = refs[10:]
    else:
        x1_ref, h2_ref = refs[10:]
    merged = (_sigmoid(gr_ref[...]) * _dot(r_ref[...].astype(BF16), wr_ref[...])
              + _sigmoid(ga_ref[...]) * _dot(a_ref[...], wa_ref[...]))
    mix = _dot(merged.astype(BF16), wo_ref[...])
    x1 = x_ref[...] + mod_ref[2:3, :] * mix
    x1_ref[...] = x1
    h2 = _rms(x1, n2_ref[...]) * (1.0 + mod_ref[4:5, :]) + mod_ref[3:4, :]
    if not route:
        h2_ref[...] = h2.astype(BF16)
        return

    @pl.when(pl.program_id(0) == 0)
    def _():
        count_ref[...] = jnp.zeros_like(count_ref)

    rec_ref[...] = _route(h2, wrt_ref, brt_ref, count_ref)
    cnt_ref[...] = count_ref[...]

    step = pl.program_id(0)
    slot = step % 2

    def h_copies(k, s):
        rows = pl.ds(k * TM, TM)
        return [pltpu.make_async_copy(hbuf.at[s, :, c * LANES:(c + 1) * LANES], h2_ref.at[rows, c],
                                      hsem.at[s]) for c in range(D_MODEL // LANES)]

    @pl.when(step >= 2)
    def _():
        for cp in h_copies(step - 2, slot):
            cp.wait()

    hbuf[slot] = h2
    for cp in h_copies(step, slot):
        cp.start()

    @pl.when(step == pl.num_programs(0) - 1)
    def _():
        for cp in h_copies(step, slot):
            cp.wait()

        @pl.when(step >= 1)
        def _():
            for cp in h_copies(step - 1, 1 - slot):
                cp.wait()


def _mix_out(x, out_r, out_a, gr, ga, mods, l, norm2, wr_bf, wa_bf, wo_bf, seq_len, lat,
             router=None):
    n = x.shape[0]
    tiles_per_seq = seq_len // TM
    if lat:
        mod_idx = lambda i: (l, 1 + i // tiles_per_seq, 0, 0)
    else:
        mod_idx = lambda i: (l, 0, 0, 0)
    row = lambda i: (i, 0)
    wspec = pl.BlockSpec((None, D_MODEL, D_MODEL), lambda i: (l, 0, 0), pipeline_mode=RESIDENT)
    in_specs = [pl.BlockSpec((TM, D_MODEL), row)] * 5 + [
        pl.BlockSpec((None, None, 6, D_MODEL), mod_idx),
        pl.BlockSpec((None, 1, D_MODEL), lambda i: (l, 0, 0)),
        wspec, wspec, wspec]
    args = [x, out_r, out_a, gr, ga, mods, norm2, wr_bf, wa_bf, wo_bf]
    out_specs = [pl.BlockSpec((TM, D_MODEL), row)]
    out_shape = [jax.ShapeDtypeStruct((n, D_MODEL), F32)]
    scratch = []
    if router is None:
        out_specs.append(pl.BlockSpec((TM, D_MODEL), row))
        out_shape.append(jax.ShapeDtypeStruct((n, D_MODEL), BF16))
    else:
        j, wr_t, br_pad = router
        in_specs += [pl.BlockSpec((None, N_EXPERTS, D_MODEL), lambda i: (j, 0, 0)),
                     pl.BlockSpec((None, 1, LANES), lambda i: (j, 0, 0))]
        args += [wr_t, br_pad]
        out_specs += [pl.BlockSpec(memory_space=pl.ANY),
                      pl.BlockSpec((TM, LANES), row),
                      pl.BlockSpec((1, LANES), lambda i: (0, 0))]
        out_shape += [jax.ShapeDtypeStruct((n, D_MODEL // LANES, LANES), F32),
                      jax.ShapeDtypeStruct((n, LANES), F32),
                      jax.ShapeDtypeStruct((1, LANES), F32)]
        scratch = [pltpu.VMEM((1, LANES), F32), pltpu.VMEM((2, TM, D_MODEL), F32),
                   pltpu.SemaphoreType.DMA((2,))]
    return pl.pallas_call(
        functools.partial(_mix_out_kernel, route=router is not None),
        grid=(n // TM,),
        in_specs=in_specs,
        out_specs=out_specs,
        out_shape=out_shape,
        scratch_shapes=scratch,
        compiler_params=_cparams(("arbitrary",)),
        name=("mix_route" if router is not None else "mix_out") + ("_lat" if lat else "_ctx"),
    )(*args)


def _ffn_kernel(x_ref, h_ref, mod_ref, wg_ref, wu_ref, wd_ref, o_ref, *, f_chunk):
    h = h_ref[...]
    d_ff = wg_ref.shape[1]
    acc = jnp.zeros((h.shape[0], D_MODEL), F32)
    for c in range(d_ff // f_chunk):
        sl = slice(c * f_chunk, (c + 1) * f_chunk)
        act = _silu(_dot(h, wg_ref[:, sl])) * _dot(h, wu_ref[:, sl])
        acc = acc + _dot(act.astype(BF16), wd_ref[sl, :])
    o_ref[...] = x_ref[...] + mod_ref[5:6, :] * acc


def _ffn(x1, h2, mods, l, j, wg_bf, wu_bf, wd_bf, seq_len, lat):
    n = x1.shape[0]
    d_ff = wg_bf.shape[2]
    tiles_per_seq = seq_len // TM
    if lat:
        mod_idx = lambda i: (l, 1 + i // tiles_per_seq, 0, 0)
    else:
        mod_idx = lambda i: (l, 0, 0, 0)
    row = lambda i: (i, 0)
    return pl.pallas_call(
        functools.partial(_ffn_kernel, f_chunk=d_ff // 2),
        grid=(n // TM,),
        in_specs=[pl.BlockSpec((TM, D_MODEL), row), pl.BlockSpec((TM, D_MODEL), row),
                  pl.BlockSpec((None, None, 6, D_MODEL), mod_idx),
                  pl.BlockSpec((None, D_MODEL, d_ff), lambda i: (j, 0, 0), pipeline_mode=RESIDENT),
                  pl.BlockSpec((None, D_MODEL, d_ff), lambda i: (j, 0, 0), pipeline_mode=RESIDENT),
                  pl.BlockSpec((None, d_ff, D_MODEL), lambda i: (j, 0, 0), pipeline_mode=RESIDENT)],
        out_specs=pl.BlockSpec((TM, D_MODEL), row),
        out_shape=jax.ShapeDtypeStruct((n, D_MODEL), F32),
        compiler_params=_cparams(("arbitrary",)),
        name="ffn_lat" if lat else "ffn_ctx",
    )(x1, h2, mods, wg_bf, wu_bf, wd_bf)


def _group_layout(rec, counts, n_tiles):
    as_int = lambda k: rec[:, k].astype(jnp.int32)
    cnt = counts[0, :N_EXPERTS].astype(jnp.int32)
    tiles = (cnt + TM_GROUP - 1) // TM_GROUP
    cum = jnp.cumsum(tiles)
    start = (cum - tiles) * TM_GROUP
    experts = jnp.arange(N_EXPERTS, dtype=jnp.int32)

    def pos(e, rank):
        return jnp.sum(jnp.where(e[:, None] == experts[None, :], start[None, :], 0), axis=1) + rank

    pos2 = jnp.stack([pos(as_int(R_E1), as_int(R_RANK1)), pos(as_int(R_E2), as_int(R_RANK2))])
    n_active = cum[N_EXPERTS - 1]
    tile_expert = jnp.sum(jnp.arange(n_tiles, dtype=jnp.int32)[:, None] >= cum[None, :], axis=1)
    tile_expert = jnp.minimum(tile_expert, tile_expert[n_active - 1]).astype(jnp.int32)
    tile_ids = jnp.arange(n_tiles, dtype=jnp.int32)
    group_last = jnp.any((tile_ids[:, None] == cum[None, :] - 1) & (tiles[None, :] > 0), axis=1)
    zero_tile = (group_last | (tile_ids >= n_active)).astype(jnp.int32)
    return (pos2.astype(jnp.int32), tile_expert, n_active.reshape(1).astype(jnp.int32), zero_tile)


def _row_copy(src, dst, sem):
    return pltpu.make_async_copy(src, dst, sem)


def _dispatch_kernel(pos_ref, zt_ref, h_ref, xs_ref, zbuf, sem, zsem):
    n_chunks = h_ref.shape[0] // DMA_CHUNK
    n_tiles = xs_ref.shape[0] // TM_GROUP

    zbuf[...] = jnp.zeros_like(zbuf)

    def zero_fill(j):
        return pltpu.make_async_copy(zbuf, xs_ref.at[pl.ds(j * TM_GROUP, TM_GROUP)], zsem)

    def zero_start(j, carry):
        @pl.when(zt_ref[j] != 0)
        def _():
            zero_fill(j).start()
        return carry

    def zero_wait(j, carry):
        @pl.when(zt_ref[j] != 0)
        def _():
            zero_fill(j).wait()
        return carry

    lax.fori_loop(0, n_tiles, zero_start, 0)
    lax.fori_loop(0, n_tiles, zero_wait, 0)

    def drain_chunk():
        rows = pl.ds(0, 2 * DMA_CHUNK)
        _row_copy(xs_ref.at[rows], xs_ref.at[rows], sem).wait()

    def chunk(k, carry):
        for r in range(DMA_CHUNK):
            t = k * DMA_CHUNK + r
            _row_copy(h_ref.at[t], xs_ref.at[pos_ref[0, t]], sem).start(priority=0)
            _row_copy(h_ref.at[t], xs_ref.at[pos_ref[1, t]], sem).start(priority=1)

        @pl.when(k > 0)
        def _():
            drain_chunk()

        return carry

    lax.fori_loop(0, n_chunks, chunk, 0)
    drain_chunk()


def _dispatch(pos2, zero_tile, h2_rows, n_rows):
    tile = h2_rows.shape[1:]
    return pl.pallas_call(
        _dispatch_kernel,
        grid_spec=pltpu.PrefetchScalarGridSpec(
            num_scalar_prefetch=2, grid=(1,),
            in_specs=[pl.BlockSpec(h2_rows.shape, lambda i, p, z: (0, 0, 0))],
            out_specs=pl.BlockSpec(memory_space=pl.ANY),
            scratch_shapes=[pltpu.VMEM((TM_GROUP,) + tile, F32),
                            pltpu.SemaphoreType.DMA(()), pltpu.SemaphoreType.DMA(())]),
        out_shape=jax.ShapeDtypeStruct((n_rows,) + tile, F32),
        compiler_params=_cparams(("arbitrary",)),
        name="moe_dispatch",
    )(pos2, zero_tile, h2_rows)


def _expert_kernel(te_ref, na_ref, xs_ref, wg_ref, wu_ref, wd_ref, ys_ref,
                   xbuf, ybuf, sem_in, sem_out):
    del te_ref
    i = pl.program_id(0)
    last = pl.num_programs(0) - 1
    n_active = na_ref[0]
    slot = i % 2
    n_col = D_MODEL // LANES

    def in_copies(tile, s):
        rows = pl.ds(tile * TM_GROUP, TM_GROUP)
        return [pltpu.make_async_copy(xs_ref.at[rows, c], xbuf.at[s, :, c * LANES:(c + 1) * LANES],
                                      sem_in.at[s]) for c in range(n_col)]

    def out_copies(tile, s):
        rows = pl.ds(tile * TM_GROUP, TM_GROUP)
        return [pltpu.make_async_copy(ybuf.at[s, :, c * LANES:(c + 1) * LANES], ys_ref.at[rows, c],
                                      sem_out.at[s]) for c in range(n_col)]

    @pl.when(i == 0)
    def _():
        for cp in in_copies(0, 0):
            cp.start()

    @pl.when(i + 1 < n_active)
    def _():
        for cp in in_copies(i + 1, 1 - slot):
            cp.start()

    @pl.when(i >= 2)
    def _():
        for cp in out_copies(i - 2, slot):
            cp.wait()

    @pl.when(i < n_active)
    def _():
        for cp in in_copies(i, slot):
            cp.wait()
        xb = xbuf[slot].astype(BF16)
        act = _silu(_dot(xb, wg_ref[...])) * _dot(xb, wu_ref[...])
        ybuf[slot] = _dot(act.astype(BF16), wd_ref[...])

    @pl.when(i >= n_active)
    def _():
        ybuf[slot] = jnp.zeros((TM_GROUP, D_MODEL), F32)

    for cp in out_copies(i, slot):
        cp.start()

    @pl.when(i == last)
    def _():
        for cp in out_copies(i, slot):
            cp.wait()

        @pl.when(i >= 1)
        def _():
            for cp in out_copies(i - 1, 1 - slot):
                cp.wait()


def _experts(tile_expert, n_active, xs, j, wg_bf, wu_bf, wd_bf):
    n_rows = xs.shape[0]
    d_e = wg_bf.shape[3]
    wspec = lambda shape: pl.BlockSpec((None, None) + shape, lambda i, te, na: (j, te[i], 0, 0))
    return pl.pallas_call(
        _expert_kernel,
        grid_spec=pltpu.PrefetchScalarGridSpec(
            num_scalar_prefetch=2, grid=(n_rows // TM_GROUP,),
            in_specs=[pl.BlockSpec(memory_space=pl.ANY),
                      wspec((D_MODEL, d_e)), wspec((D_MODEL, d_e)), wspec((d_e, D_MODEL))],
            out_specs=pl.BlockSpec(memory_space=pl.ANY),
            scratch_shapes=[pltpu.VMEM((2, TM_GROUP, D_MODEL), F32),
                            pltpu.VMEM((2, TM_GROUP, D_MODEL), F32),
                            pltpu.SemaphoreType.DMA((2,)), pltpu.SemaphoreType.DMA((2,))]),
        out_shape=jax.ShapeDtypeStruct(xs.shape, F32),
        compiler_params=_cparams(("arbitrary",)),
        name="moe_experts",
    )(tile_expert, n_active, xs, wg_bf, wu_bf, wd_bf)


def _combine_kernel(*refs, final):
    pos_ref, y_ref, x_ref, rec_ref, mod_ref = refs[:5]
    g_ref = refs[5] if final else None
    o_ref, a1, a2, b1, b2, sem_a, sem_b = refs[5 + int(final):]
    i = pl.program_id(0)

    def issue(half_idx, d1, d2, sem):
        base = half_idx * DMA_CHUNK

        for r in range(DMA_CHUNK):
            _row_copy(y_ref.at[pos_ref[0, base + r]], d1.at[r], sem).start(priority=0)
            _row_copy(y_ref.at[pos_ref[1, base + r]], d2.at[r], sem).start(priority=1)

    def drain(d1, d2, sem):
        rows = pl.ds(0, DMA_CHUNK)
        _row_copy(y_ref.at[rows], d1, sem).wait()
        _row_copy(y_ref.at[rows], d2, sem).wait()

    def finish(half, d1, d2):
        rows = slice(half * DMA_CHUNK, (half + 1) * DMA_CHUNK)
        w1 = rec_ref[rows, R_W1:R_W1 + 1]
        w2 = rec_ref[rows, R_W2:R_W2 + 1]
        for c in range(D_MODEL // LANES):
            sl = slice(c * LANES, (c + 1) * LANES)
            y = w1 * d1[:, c, :] + w2 * d2[:, c, :]
            o_ref[rows, sl] = x_ref[rows, sl] + mod_ref[5:6, sl] * y
        if final:
            o_ref[rows, :] = _rms(o_ref[rows, :], g_ref[...])

    @pl.when(i == 0)
    def _():
        issue(0, a1, a2, sem_a)

    issue(2 * i + 1, b1, b2, sem_b)
    drain(a1, a2, sem_a)
    finish(0, a1, a2)

    @pl.when(i + 1 < pl.num_programs(0))
    def _():
        issue(2 * i + 2, a1, a2, sem_a)

    drain(b1, b2, sem_b)
    finish(1, b1, b2)


def _combine(pos2, ys, x1, rec, mods, l, seq_len, lat, final_g):
    n = x1.shape[0]
    final = final_g is not None
    tile = ys.shape[1:]
    tm = 2 * DMA_CHUNK
    tiles_per_seq = max(seq_len // tm, 1)
    if lat:
        mod_idx = lambda i, p: (l, 1 + i // tiles_per_seq, 0, 0)
    else:
        mod_idx = lambda i, p: (l, 0, 0, 0)
    row = lambda i, p: (i, 0)
    slot = pltpu.VMEM((DMA_CHUNK,) + tile, F32)
    in_specs = [pl.BlockSpec(memory_space=pl.ANY),
                pl.BlockSpec((tm, D_MODEL), row),
                pl.BlockSpec((tm, LANES), row),
                pl.BlockSpec((None, None, 6, D_MODEL), mod_idx)]
    args = [pos2, ys, x1, rec, mods]
    if final:
        in_specs.append(pl.BlockSpec((1, D_MODEL), lambda i, p: (0, 0)))
        args.append(final_g)
    return pl.pallas_call(
        functools.partial(_combine_kernel, final=final),
        grid_spec=pltpu.PrefetchScalarGridSpec(
            num_scalar_prefetch=1, grid=(n // tm,),
            in_specs=in_specs,
            out_specs=pl.BlockSpec((tm, D_MODEL), row),
            scratch_shapes=[slot, slot, slot, slot,
                            pltpu.SemaphoreType.DMA(()), pltpu.SemaphoreType.DMA(())]),
        out_shape=jax.ShapeDtypeStruct((n, D_MODEL), F32),
        compiler_params=_cparams(("arbitrary",)),
        name="moe_combine_final" if final else "moe_combine",
    )(*args)


def _moe(x1, h2_rows, rec, counts, mods, l, j, wg_bf, wu_bf, wd_bf, seq_len, lat, final_g):
    n = x1.shape[0]
    n_tiles = 2 * n // TM_GROUP + N_EXPERTS
    pos2, tile_expert, n_active, zero_tile = _group_layout(rec, counts, n_tiles)
    xs = _dispatch(pos2, zero_tile, h2_rows, n_tiles * TM_GROUP)
    ys = _experts(tile_expert, n_active, xs, j, wg_bf, wu_bf, wd_bf)
    return _combine(pos2, ys, x1, rec, mods, l, seq_len, lat, final_g)


def _final_norm_kernel(x_ref, g_ref, o_ref):
    o_ref[...] = _rms(x_ref[...], g_ref[...])


def _final_norm(x, g):
    n = x.shape[0]
    row = lambda i: (i, 0)
    return pl.pallas_call(
        _final_norm_kernel,
        grid=(n // TM_FF,),
        in_specs=[pl.BlockSpec((TM_FF, D_MODEL), row), pl.BlockSpec((1, D_MODEL), lambda i: (0, 0))],
        out_specs=pl.BlockSpec((TM_FF, D_MODEL), row),
        out_shape=jax.ShapeDtypeStruct((n, D_MODEL), F32),
        compiler_params=_cparams(("arbitrary",)),
        name="final_norm",
    )(x, g)


def _rope_tables(t_len):
    t = jnp.arange(t_len)
    pos = jnp.stack([t // GRID_W, t % GRID_W], axis=-1).astype(F32)
    inv = ROPE_THETA ** (-jnp.arange(0, AXIS_DIM, 2, dtype=F32) / AXIS_DIM)
    ang = pos[..., None] * inv
    cos, sin = jnp.cos(ang), jnp.sin(ang)
    c = jnp.concatenate([cos[:, 0], cos[:, 0], cos[:, 1], cos[:, 1]], axis=-1)
    s = jnp.concatenate([-sin[:, 0], sin[:, 0], -sin[:, 1], sin[:, 1]], axis=-1)
    return c, s


def _gate_weights(wa, wi):
    depth = wa.shape[0]
    groups = D_RNN // LANES

    def pair(w):
        w = w.reshape(depth, 2, groups, 2, RNN_BLOCK, RNN_BLOCK)
        z = jnp.zeros_like(w[:, :, :, 0])
        top = jnp.concatenate([w[:, :, :, 0], z], axis=-1)
        bot = jnp.concatenate([z, w[:, :, :, 1]], axis=-1)
        return jnp.concatenate([top, bot], axis=-2)

    w = jnp.concatenate([pair(wa), pair(wi)], axis=-1)
    return (0.5 * jnp.concatenate([w[:, 0], w[:, 1]], axis=-1)).astype(BF16)


def kernel(x_prompt, x_sample, cache_k, cache_v, state_rnn, c, c_ctx, w_mod, b_mod, norm1, norm2, w_in, conv_w, conv_b, lru_wa, lru_ba, lru_wi, lru_bi, lru_lambda, q_norm, k_norm, w_rnn_out, w_attn_out, w_out, ff_gate, ff_up, ff_down, router_w, router_b, exp_gate, exp_up, exp_down, final_norm):
    batch, seq, _ = x_prompt.shape
    dec_batch, dec_seq, _ = x_sample.shape
    depth = w_mod.shape[0]
    past = cache_k.shape[2]
    assert dec_batch + 1 <= MOD_ROWS

    cond8 = jnp.zeros((MOD_ROWS, D_MODEL), F32).at[0].set(c_ctx).at[1:1 + dec_batch].set(c)
    mods = _modulation(cond8, w_mod, b_mod).reshape(depth, MOD_ROWS, 6, D_MODEL)

    bf = lambda w: w.astype(BF16)
    w_in_bf, wr_bf, wa_bf, wo_bf = bf(w_in), bf(w_rnn_out), bf(w_attn_out), bf(w_out)
    ffg_bf, ffu_bf, ffd_bf = bf(ff_gate), bf(ff_up), bf(ff_down)
    eg_bf, eu_bf, ed_bf = bf(exp_gate), bf(exp_up), bf(exp_down)
    gate_w = _gate_weights(lru_wa, lru_wi)
    n_moe = router_w.shape[0]
    wr_t = jnp.swapaxes(router_w, 1, 2)
    br_pad = jnp.zeros((n_moe, 1, LANES), F32).at[:, 0, :N_EXPERTS].set(router_b)

    norm1_3 = norm1.reshape(depth, 1, D_MODEL)
    norm2_3 = norm2.reshape(depth, 1, D_MODEL)
    qn3 = q_norm.reshape(depth, 1, HEAD_DIM)
    kn3 = k_norm.reshape(depth, 1, HEAD_DIM)
    conv_b3 = conv_b.reshape(depth, 1, D_RNN)
    ba4 = (0.5 * lru_ba).reshape(depth, 2, 1, D_RNN)
    bi4 = (0.5 * lru_bi).reshape(depth, 2, 1, D_RNN)
    lam4 = lru_lambda.reshape(depth, 2, 1, D_RNN)
    state5 = state_rnn.reshape(dec_batch, depth, 2, 1, D_RNN)
    cache_k4 = cache_k.reshape(dec_batch, depth, past, KV_DIM)
    cache_v4 = cache_v.reshape(dec_batch, depth, past, KV_DIM)
    rope_tabs = _rope_tables(dec_seq)
    final_g = final_norm.reshape(1, D_MODEL)

    def run_stream(x, lat):
        seq_len = dec_seq if lat else seq
        ks, vs, ss = [], [], []
        for l in range(depth):
            xr, yr, q, kn, kr, v, gr, ga = _in_proj(
                x, mods, l, norm1_3, w_in_bf, qn3, kn3, rope_tabs if lat else None, seq_len)
            out_r, end_f, end_b = _rnn(xr, yr, l, conv_w, conv_b3, gate_w, ba4, bi4, lam4,
                                       state5 if lat else None, seq_len)
            if lat:
                out_a = _attn_lat(q, kr, v, cache_k4, cache_v4, l, seq_len)
            else:
                out_a = _attn_ctx(q, kr, v, seq_len)
                ks.append(kn)
                vs.append(v)
                ss.append(jnp.stack([end_f, end_b], axis=1))
            moe_layer = l % 2 == 1
            mix_args = (x, out_r, out_a, gr, ga, mods, l, norm2_3, wr_bf, wa_bf, wo_bf, seq_len, lat)
            if moe_layer:
                x1, h2_rows, rec, counts = _mix_out(*mix_args, router=(l // 2, wr_t, br_pad))
                x = _moe(x1, h2_rows, rec, counts, mods, l, l // 2, eg_bf, eu_bf, ed_bf,
                         seq_len, lat, final_g if l == depth - 1 else None)
            else:
                x1, h2 = _mix_out(*mix_args)
                x = _ffn(x1, h2, mods, l, l // 2, ffg_bf, ffu_bf, ffd_bf, seq_len, lat)
                if l == depth - 1:
                    x = _final_norm(x, final_g)
        return x, ks, vs, ss

    y_ctx, ks, vs, ss = run_stream(x_prompt.reshape(batch * seq, D_MODEL), False)
    y_lat, _, _, _ = run_stream(x_sample.reshape(dec_batch * dec_seq, D_MODEL), True)

    new_k = jnp.stack([k.reshape(batch, seq, N_KV_HEADS, HEAD_DIM) for k in ks], axis=1)
    new_v = jnp.stack([v.reshape(batch, seq, N_KV_HEADS, HEAD_DIM) for v in vs], axis=1)
    new_state = jnp.stack(ss, axis=1)
    return (y_ctx.reshape(batch, seq, D_MODEL), y_lat.reshape(dec_batch, dec_seq, D_MODEL),
            new_k, new_v, new_state)
```

```python
import functools

import jax
import jax.numpy as jnp
from jax import lax
from jax.experimental import pallas as pl
from jax.experimental.pallas import tpu as pltpu

F32 = jnp.float32
BF16 = jnp.bfloat16

D_MODEL = 1024
D_RNN = 1024
N_RNN_BLOCKS = 16
RNN_BLOCK = D_RNN // N_RNN_BLOCKS
CONV_W = 4
CONV_LEFT = 2
LRU_C = 8.0
N_HEADS = 8
N_KV_HEADS = 2
HEAD_DIM = 128
AXIS_DIM = HEAD_DIM // 2
ROPE_THETA = 10000.0
GRID_W = 64
Q_DIM = N_HEADS * HEAD_DIM
KV_DIM = N_KV_HEADS * HEAD_DIM
N_EXPERTS = 8
EPS = 1e-6
LOG2_E = 1.4426950408889634
RSQRT_FLOOR = 1e-30
D_IN = 2 * D_RNN + Q_DIM + 2 * KV_DIM + 2 * D_MODEL
O_XR, O_YR, O_Q = 0, D_RNN, 2 * D_RNN
O_K = O_Q + Q_DIM
O_V = O_K + KV_DIM
O_GR = O_V + KV_DIM
O_GA = O_GR + D_MODEL

LANES = 128
SUBLANES = 8
MOD_ROWS = 8
VMEM_LIMIT = 56 * 1024 * 1024

TM = 512
TM_ROPE = 256
RESIDENT = pl.Buffered(1)
RNN_T = 256
RNN_SEG = SUBLANES
RNN_C = 256
RNN_TCH = 32
RNN_SCAN_UNROLL = 8
TQ = 256
TM_FF = 512
TM_GROUP = 256
DMA_CHUNK = 256


def _cparams(sem):
    return pltpu.CompilerParams(dimension_semantics=sem, vmem_limit_bytes=VMEM_LIMIT)


def _sigmoid(x):
    return 1.0 / (1.0 + jnp.exp(-x))


def _silu(x):
    return x * _sigmoid(x)


def _gelu_tanh(x):
    return 0.5 * x * (1.0 + jnp.tanh(0.7978845608028654 * (x + 0.044715 * (x * x * x))))


def _rms(x, g):
    ms = jnp.mean(x * x, axis=-1, keepdims=True)
    return x * lax.rsqrt(ms + EPS) * g


def _dot(a, b):
    return jnp.dot(a, b, preferred_element_type=F32)


def _mod_kernel(cond_ref, w_ref, b_ref, o_ref):
    s = _silu(cond_ref[...]).astype(BF16)
    o_ref[...] = _dot(s, w_ref[...].astype(BF16)) + b_ref[...]


def _modulation(cond8, w_mod, b_mod):
    depth = w_mod.shape[0]
    tn = 1536
    return pl.pallas_call(
        _mod_kernel,
        grid=(depth, 6 * D_MODEL // tn),
        in_specs=[
            pl.BlockSpec((MOD_ROWS, D_MODEL), lambda l, j: (0, 0)),
            pl.BlockSpec((None, D_MODEL, tn), lambda l, j: (l, 0, j)),
            pl.BlockSpec((None, 1, tn), lambda l, j: (l, 0, j)),
        ],
        out_specs=pl.BlockSpec((None, MOD_ROWS, tn), lambda l, j: (l, 0, j)),
        out_shape=jax.ShapeDtypeStruct((depth, MOD_ROWS, 6 * D_MODEL), F32),
        compiler_params=_cparams(("arbitrary", "arbitrary")),
        name="adaln_mod",
    )(cond8, w_mod, b_mod.reshape(depth, 1, 6 * D_MODEL))


def _swap_halves(x):
    lane = lax.broadcasted_iota(jnp.int32, x.shape, 1)
    return jnp.where((lane % AXIS_DIM) < AXIS_DIM // 2,
                     pltpu.roll(x, LANES - AXIS_DIM // 2, axis=1),
                     pltpu.roll(x, AXIS_DIM // 2, axis=1))


class _Rider:
    def __init__(self, w2d, rows, first):
        self.w2d, self.rows, self.first = w2d, rows, first


def _rider_specs(riders, n_steps, step_of):
    in_specs, out_specs, out_shape = [], [], []
    for r in riders:
        cols = r.w2d.shape[1]
        assert r.rows % 16 == 0 and (r.first + n_steps) * r.rows <= r.w2d.shape[0]
        in_specs.append(pl.BlockSpec((r.rows, cols), lambda *g, r=r: (r.first + step_of(*g), 0)))
        out_specs.append(pl.BlockSpec((r.rows, cols), lambda *g: (step_of(*g), 0)))
        out_shape.append(jax.ShapeDtypeStruct((r.rows * n_steps, cols), BF16))
    return in_specs, out_specs, out_shape


def _convert_riders(in_refs, out_refs):
    for w_ref, o_ref in zip(in_refs, out_refs):
        o_ref[...] = w_ref[...].astype(BF16)


def _in_proj_kernel(*refs, rope, n_riders):
    n_in = 8 if rope else 6
    _convert_riders(refs[n_in:n_in + n_riders], refs[n_in + n_riders + 8:])
    refs = refs[:n_in] + refs[n_in + n_riders:n_in + n_riders + 8]
    if rope:
        (x_ref, mod_ref, n1_ref, w_ref, qn_ref, kn_ref, c_ref, s_ref,
         xr_ref, yr_ref, q_ref, kno_ref, kro_ref, v_ref, gr_ref, ga_ref) = refs
    else:
        (x_ref, mod_ref, n1_ref, w_ref, qn_ref, kn_ref,
         xr_ref, yr_ref, q_ref, kno_ref, kro_ref, v_ref, gr_ref, ga_ref) = refs
    x = x_ref[...]
    h = _rms(x, n1_ref[...]) * (1.0 + mod_ref[1:2, :]) + mod_ref[0:1, :]
    hb = h.astype(BF16)

    xr_ref[...] = _dot(hb, w_ref[:, O_XR:O_XR + D_RNN])
    yr_ref[...] = _dot(hb, w_ref[:, O_YR:O_YR + D_RNN])
    gr_ref[...] = _dot(hb, w_ref[:, O_GR:O_GR + D_MODEL])
    ga_ref[...] = _dot(hb, w_ref[:, O_GA:O_GA + D_MODEL])
    v_ref[...] = _dot(hb, w_ref[:, O_V:O_V + KV_DIM])

    if rope:
        cs, sn = c_ref[...], s_ref[...]
    zq = _dot(hb, w_ref[:, O_Q:O_Q + Q_DIM])
    for hd in range(N_HEADS):
        t = _rms(zq[:, hd * HEAD_DIM:(hd + 1) * HEAD_DIM], qn_ref[...])
        if rope:
            t = t * cs + _swap_halves(t) * sn
        q_ref[:, hd * HEAD_DIM:(hd + 1) * HEAD_DIM] = t.astype(BF16)
    zk = _dot(hb, w_ref[:, O_K:O_K + KV_DIM])
    for hd in range(N_KV_HEADS):
        t = _rms(zk[:, hd * HEAD_DIM:(hd + 1) * HEAD_DIM], kn_ref[...])
        kno_ref[:, hd * HEAD_DIM:(hd + 1) * HEAD_DIM] = t
        if rope:
            t = t * cs + _swap_halves(t) * sn
        kro_ref[:, hd * HEAD_DIM:(hd + 1) * HEAD_DIM] = t.astype(BF16)


def _in_proj(x, mods, l, norm1, w_in_bf, wl, q_norm, k_norm, rope_tabs, seq_len, riders=()):
    n = x.shape[0]
    rope = rope_tabs is not None
    tm = TM_ROPE if rope else TM
    tiles_per_seq = seq_len // tm
    if rope:
        mod_idx = lambda i: (l, 1 + i // tiles_per_seq, 0, 0)
    else:
        mod_idx = lambda i: (l, 0, 0, 0)
    row = lambda i: (i, 0)
    in_specs = [
        pl.BlockSpec((tm, D_MODEL), row),
        pl.BlockSpec((None, None, 6, D_MODEL), mod_idx),
        pl.BlockSpec((None, 1, D_MODEL), lambda i: (l, 0, 0)),
        pl.BlockSpec((None, D_MODEL, D_IN), lambda i: (wl, 0, 0), pipeline_mode=RESIDENT),
        pl.BlockSpec((None, 1, HEAD_DIM), lambda i: (l, 0, 0)),
        pl.BlockSpec((None, 1, HEAD_DIM), lambda i: (l, 0, 0)),
    ]
    args = [x, mods, norm1, w_in_bf, q_norm, k_norm]
    if rope:
        in_specs += [pl.BlockSpec((tm, HEAD_DIM), lambda i: (i % tiles_per_seq, 0))] * 2
        args += list(rope_tabs)
    widths = [(D_RNN, F32), (D_RNN, F32), (Q_DIM, BF16), (KV_DIM, F32), (KV_DIM, BF16),
              (KV_DIM, F32), (D_MODEL, F32), (D_MODEL, F32)]
    r_in, r_out, r_shape = _rider_specs(riders, n // tm, lambda i: i)
    return pl.pallas_call(
        functools.partial(_in_proj_kernel, rope=rope, n_riders=len(riders)),
        grid=(n // tm,),
        in_specs=in_specs + r_in,
        out_specs=[pl.BlockSpec((tm, w), row) for w, _ in widths] + r_out,
        out_shape=[jax.ShapeDtypeStruct((n, w), dt) for w, dt in widths] + r_shape,
        compiler_params=_cparams(("arbitrary",)),
        name="in_proj_lat" if rope else "in_proj_ctx",
    )(*args, *[r.w2d for r in riders])


def _rnn_kernel(*refs, chained):
    refs = list(refs)
    xr_ref, yr_ref, cw_ref, cb_ref, gw_ref, ba_ref, bi_ref, lam_ref = refs[:8]
    refs = refs[8:]
    h0_ref = refs.pop(0) if chained else None
    out_ref, sf_ref, sb_ref, xext, ybuf, obuf, a_f, b_f, a_b, b_b, sem_in, sem_out = refs

    g = pl.program_id(0)
    cb = pl.program_id(1)
    n_cb = pl.num_programs(1)
    step = g * n_cb + cb
    n_steps = pl.num_programs(0) * n_cb
    slot = step % 2

    def hbm_window(kk, s):
        rows = pl.ds(pl.multiple_of(((kk // n_cb) * RNN_SEG + s) * RNN_T, RNN_T), RNN_T)
        cols = pl.ds(pl.multiple_of((kk % n_cb) * RNN_C, RNN_C), RNN_C)
        return rows, cols

    def in_copies(kk, sl_):
        cps = []
        for s in range(RNN_SEG):
            win = hbm_window(kk, s)
            cps.append(pltpu.make_async_copy(
                xr_ref.at[win], xext.at[sl_, pl.ds(CONV_LEFT, RNN_T), s], sem_in.at[sl_]))
            cps.append(pltpu.make_async_copy(yr_ref.at[win], ybuf.at[sl_, :, s], sem_in.at[sl_]))
        return cps

    def out_copies(kk, sl_):
        return [pltpu.make_async_copy(obuf.at[sl_, :, s], out_ref.at[hbm_window(kk, s)],
                                      sem_out.at[sl_]) for s in range(RNN_SEG)]

    @pl.when(step == 0)
    def _():
        for cp in in_copies(0, 0):
            cp.start()

    @pl.when(step + 1 < n_steps)
    def _():
        for cp in in_copies(step + 1, 1 - slot):
            cp.start()

    for cp in in_copies(step, slot):
        cp.wait()

    seg = lax.broadcasted_iota(jnp.int32, (RNN_SEG, RNN_C), 0)
    zero_row = jnp.zeros((RNN_SEG, RNN_C), F32)
    if chained:
        xext[slot, 0] = jnp.where(seg == 0, zero_row, pltpu.roll(xext[slot, RNN_T], 1, axis=0))
        xext[slot, 1] = jnp.where(seg == 0, zero_row, pltpu.roll(xext[slot, RNN_T + 1], 1, axis=0))
        xext[slot, RNN_T + CONV_LEFT] = jnp.where(
            seg == RNN_SEG - 1, zero_row, pltpu.roll(xext[slot, CONV_LEFT], RNN_SEG - 1, axis=0))
    else:
        xext[slot, 0] = zero_row
        xext[slot, 1] = zero_row
        xext[slot, RNN_T + CONV_LEFT] = zero_row

    scans = ((a_f, b_f), (a_b, b_b))
    rows2d = RNN_TCH * RNN_SEG

    def gate_chunk(tc, carry):
        t0 = pl.multiple_of(tc * RNN_TCH, RNN_TCH)
        for j in range(RNN_C // LANES):
            sl = slice(j * LANES, (j + 1) * LANES)
            xc = jnp.broadcast_to(cb_ref[:, sl].reshape(1, 1, LANES), (RNN_TCH, RNN_SEG, LANES))
            for k in range(CONV_W):
                xc = xc + (xext[slot, pl.ds(t0 + k, RNN_TCH), :, sl]
                           * cw_ref[k:k + 1, sl].reshape(1, 1, LANES))
            xc = xc.reshape(rows2d, LANES)
            pre = _dot(xc.astype(BF16), gw_ref[j])
            for d, (a_s, b_s) in enumerate(scans):
                lam = lam_ref[d, :, sl]
                log2_decay = (-LRU_C * LOG2_E) * (jnp.maximum(-lam, 0.0)
                                                  + jnp.log(1.0 + jnp.exp(-jnp.abs(lam))))
                r = 0.5 * jnp.tanh(pre[:, 2 * d * LANES:(2 * d + 1) * LANES]
                                   + ba_ref[d, :, sl]) + 0.5
                gi = 0.5 * jnp.tanh(pre[:, (2 * d + 1) * LANES:(2 * d + 2) * LANES]
                                    + bi_ref[d, :, sl]) + 0.5
                a = jnp.exp2(r * log2_decay)
                y = 1.0 - a * a
                b = (y * lax.rsqrt(jnp.maximum(y, RSQRT_FLOOR))) * (gi * xc)
                a_s[pl.ds(t0, RNN_TCH), :, sl] = a.reshape(RNN_TCH, RNN_SEG, LANES)
                b_s[pl.ds(t0, RNN_TCH), :, sl] = b.reshape(RNN_TCH, RNN_SEG, LANES)
        return carry

    lax.fori_loop(0, RNN_T // RNN_TCH, gate_chunk, 0)

    def scan_step(t, carry):
        hf, hb, pf, pb = carry
        tb = RNN_T - 1 - t
        af_t = a_f[t]
        hf = af_t * hf + b_f[t]
        b_f[t] = hf
        ab_t = a_b[tb]
        hb = ab_t * hb + b_b[tb]
        b_b[tb] = hb
        if chained:
            pf = af_t * pf
            a_f[t] = pf
            pb = ab_t * pb
            a_b[tb] = pb
        return hf, hb, pf, pb

    ones = jnp.ones((RNN_SEG, RNN_C), F32)
    hf, hb, pf, pb = lax.fori_loop(0, RNN_T, scan_step, (zero_row, zero_row, ones, ones),
                                   unroll=RNN_SCAN_UNROLL)

    if chained:
        st_f = jnp.broadcast_to(h0_ref[0], (RNN_SEG, RNN_C))
        st_b = jnp.broadcast_to(h0_ref[1], (RNN_SEG, RNN_C))
        in_f, in_b = st_f, st_b
        for _ in range(RNN_SEG - 1):
            in_f = jnp.where(seg == 0, st_f, pltpu.roll(pf * in_f + hf, 1, axis=0))
            in_b = jnp.where(seg == RNN_SEG - 1, st_b,
                             pltpu.roll(pb * in_b + hb, RNN_SEG - 1, axis=0))
        sf_ref[...] = pf * in_f + hf
        sb_ref[...] = pb * in_b + hb
    else:
        sf_ref[...] = hf
        sb_ref[...] = hb

    @pl.when(step >= 2)
    def _():
        for cp in out_copies(step - 2, slot):
            cp.wait()

    def finish_chunk(tc, carry):
        rows = pl.ds(pl.multiple_of(tc * RNN_TCH, RNN_TCH), RNN_TCH)
        h_f, h_b = b_f[rows], b_b[rows]
        if chained:
            h_f = a_f[rows] * in_f.reshape(1, RNN_SEG, RNN_C) + h_f
            h_b = a_b[rows] * in_b.reshape(1, RNN_SEG, RNN_C) + h_b
        obuf[slot, rows] = (h_f + h_b) * _gelu_tanh(ybuf[slot, rows])
        return carry

    lax.fori_loop(0, RNN_T // RNN_TCH, finish_chunk, 0)

    for cp in out_copies(step, slot):
        cp.start()

    @pl.when(step == n_steps - 1)
    def _():
        for cp in out_copies(step, slot):
            cp.wait()

        @pl.when(step >= 1)
        def _():
            for cp in out_copies(step - 1, 1 - slot):
                cp.wait()


def _rnn(xr, yr, l, conv_w, conv_b, gate_w, lru_ba, lru_bi, lru_lambda, state5, seq_len):
    n = xr.shape[0]
    chained = state5 is not None
    group_len = RNN_SEG * RNN_T
    assert n % group_len == 0 and seq_len in ((group_len,) if chained else (RNN_T,))
    n_groups = n // group_len
    lanes_per_step = RNN_C // LANES
    vec = pl.BlockSpec((None, 2, 1, RNN_C), lambda g, c: (l, 0, 0, c))
    in_specs = [
        pl.BlockSpec(memory_space=pl.ANY),
        pl.BlockSpec(memory_space=pl.ANY),
        pl.BlockSpec((None, CONV_W, RNN_C), lambda g, c: (l, 0, c)),
        pl.BlockSpec((None, 1, RNN_C), lambda g, c: (l, 0, c)),
        pl.BlockSpec((None, lanes_per_step, LANES, 4 * LANES), lambda g, c: (l, c, 0, 0)),
        vec, vec, vec,
    ]
    args = [xr, yr, conv_w, conv_b, gate_w, lru_ba, lru_bi, lru_lambda]
    if chained:
        in_specs.append(pl.BlockSpec((None, None, 2, 1, RNN_C), lambda g, c: (g, l, 0, 0, c)))
        args.append(state5)
    state_spec = pl.BlockSpec((RNN_SEG, RNN_C), lambda g, c: (g, c))
    state_shape = jax.ShapeDtypeStruct((n_groups * RNN_SEG, D_RNN), F32)
    work = lambda t: pltpu.VMEM((t, RNN_SEG, RNN_C), F32)
    return pl.pallas_call(
        functools.partial(_rnn_kernel, chained=chained),
        grid=(n_groups, D_RNN // RNN_C),
        in_specs=in_specs,
        out_specs=[pl.BlockSpec(memory_space=pl.ANY), state_spec, state_spec],
        out_shape=[jax.ShapeDtypeStruct((n, D_RNN), F32), state_shape, state_shape],
        scratch_shapes=[
            pltpu.VMEM((2, RNN_T + CONV_W - 1, RNN_SEG, RNN_C), F32),
            pltpu.VMEM((2, RNN_T, RNN_SEG, RNN_C), F32),
            pltpu.VMEM((2, RNN_T, RNN_SEG, RNN_C), F32),
            work(RNN_T), work(RNN_T), work(RNN_T), work(RNN_T),
            pltpu.SemaphoreType.DMA((2,)), pltpu.SemaphoreType.DMA((2,)),
        ],
        compiler_params=_cparams(("arbitrary", "arbitrary")),
        name="rnn_lat" if chained else "rnn_ctx",
    )(*args)


def _attend_heads(q_ref, k_all, v_all, o_ref, stack_heads):
    exp2_scale = (1.0 / (HEAD_DIM ** 0.5)) * LOG2_E
    group = N_HEADS // N_KV_HEADS
    tq = q_ref.shape[0]
    for kh in range(N_KV_HEADS):
        k = k_all[:, kh * HEAD_DIM:(kh + 1) * HEAD_DIM]
        v = v_all[:, kh * HEAD_DIM:(kh + 1) * HEAD_DIM]
        heads = [slice((kh * group + g) * HEAD_DIM, (kh * group + g + 1) * HEAD_DIM)
                 for g in range(group)]
        stacks = [heads] if stack_heads else [[sl] for sl in heads]
        for stack in stacks:
            q = jnp.concatenate([q_ref[:, sl] for sl in stack], axis=0)
            s = lax.dot_general(q, k, (((1,), (1,)), ((), ())), preferred_element_type=F32)
            m = jnp.max(s, axis=-1, keepdims=True)
            p = jnp.exp2((s - m) * exp2_scale)
            denom = jnp.sum(p, axis=-1, keepdims=True)
            o = (_dot(p.astype(BF16), v) / denom).astype(BF16)
            for g, sl in enumerate(stack):
                o_ref[:, sl] = o[g * tq:(g + 1) * tq]


def _attn_ctx_kernel(q_ref, k_ref, v_ref, o_ref):
    _attend_heads(q_ref, k_ref[...], v_ref[...].astype(BF16), o_ref, stack_heads=True)


def _attn_ctx(q, kr, v, seq_len):
    n = q.shape[0]
    row = lambda b: (b, 0)
    return pl.pallas_call(
        _attn_ctx_kernel,
        grid=(n // seq_len,),
        in_specs=[pl.BlockSpec((seq_len, Q_DIM), row),
                  pl.BlockSpec((seq_len, KV_DIM), row),
                  pl.BlockSpec((seq_len, KV_DIM), row)],
        out_specs=pl.BlockSpec((seq_len, Q_DIM), row),
        out_shape=jax.ShapeDtypeStruct((n, Q_DIM), BF16),
        compiler_params=_cparams(("arbitrary",)),
        name="attn_ctx",
    )(q, kr, v)


def _attn_lat_kernel(*refs, past, n_riders):
    _convert_riders(refs[5:5 + n_riders], refs[6 + n_riders:6 + 2 * n_riders])
    q_ref, kc_ref, vc_ref, kl_ref, vl_ref = refs[:5]
    o_ref = refs[5 + n_riders]
    k_all, v_all = refs[6 + 2 * n_riders:]

    @pl.when(pl.program_id(1) == 0)
    def _():
        k_all[0:past, :] = kc_ref[...].astype(BF16)
        k_all[past:, :] = kl_ref[...]
        v_all[0:past, :] = vc_ref[...].astype(BF16)
        v_all[past:, :] = vl_ref[...].astype(BF16)

    _attend_heads(q_ref, k_all[...], v_all[...], o_ref, stack_heads=False)


def _attn_lat(q, kr, v, cache_k4, cache_v4, l, seq_len, riders=()):
    n = q.shape[0]
    past = cache_k4.shape[2]
    nq = seq_len // TQ
    step_of = lambda b, j: b * nq + j
    r_in, r_out, r_shape = _rider_specs(riders, (n // seq_len) * nq, step_of)
    return pl.pallas_call(
        functools.partial(_attn_lat_kernel, past=past, n_riders=len(riders)),
        grid=(n // seq_len, nq),
        in_specs=[pl.BlockSpec((TQ, Q_DIM), lambda b, j: (b * nq + j, 0)),
                  pl.BlockSpec((None, None, past, KV_DIM), lambda b, j: (b, l, 0, 0)),
                  pl.BlockSpec((None, None, past, KV_DIM), lambda b, j: (b, l, 0, 0)),
                  pl.BlockSpec((seq_len, KV_DIM), lambda b, j: (b, 0)),
                  pl.BlockSpec((seq_len, KV_DIM), lambda b, j: (b, 0))] + r_in,
        out_specs=[pl.BlockSpec((TQ, Q_DIM), lambda b, j: (b * nq + j, 0))] + r_out,
        out_shape=[jax.ShapeDtypeStruct((n, Q_DIM), BF16)] + r_shape,
        scratch_shapes=[pltpu.VMEM((past + seq_len, KV_DIM), BF16),
                        pltpu.VMEM((past + seq_len, KV_DIM), BF16)],
        compiler_params=_cparams(("arbitrary", "arbitrary")),
        name="attn_lat",
    )(q, cache_k4, cache_v4, kr, v, *[r.w2d for r in riders])


R_E1, R_E2, R_W1, R_W2, R_RANK1, R_RANK2 = range(6)


def _route(h2, wrt_ref, brt_ref, count_ref):
    shape = (h2.shape[0], LANES)
    lane = lax.broadcasted_iota(jnp.int32, shape, 1)
    neg = jnp.float32(-jnp.inf)
    logits = jnp.full(shape, neg, F32)
    for e in range(N_EXPERTS):
        prod = h2 * wrt_ref[e:e + 1, :]
        part = prod[:, 0:LANES]
        for c in range(1, D_MODEL // LANES):
            part = part + prod[:, c * LANES:(c + 1) * LANES]
        logits = jnp.where(lane == e, jnp.sum(part, axis=-1, keepdims=True), logits)
    logits = logits + brt_ref[...]
    v1 = jnp.max(logits, axis=-1, keepdims=True)
    i1 = jnp.min(jnp.where(logits == v1, lane, LANES), axis=-1, keepdims=True)
    rest = jnp.where(lane == i1, neg, logits)
    v2 = jnp.max(rest, axis=-1, keepdims=True)
    i2 = jnp.min(jnp.where(rest == v2, lane, LANES), axis=-1, keepdims=True)
    e2 = jnp.exp(v2 - v1)
    w1 = 1.0 / (1.0 + e2)
    w2 = e2 / (1.0 + e2)
    chosen = jnp.where((lane == i1) | (lane == i2), 1.0, 0.0)
    r_id = lax.broadcasted_iota(jnp.int32, (shape[0], shape[0]), 0)
    c_id = lax.broadcasted_iota(jnp.int32, (shape[0], shape[0]), 1)
    before = jnp.where(c_id < r_id, 1.0, 0.0).astype(BF16)
    rank = _dot(before, chosen.astype(BF16)) + count_ref[...]
    rank1 = jnp.sum(jnp.where(lane == i1, rank, 0.0), axis=-1, keepdims=True)
    rank2 = jnp.sum(jnp.where(lane == i2, rank, 0.0), axis=-1, keepdims=True)
    count_ref[...] += jnp.sum(chosen, axis=0, keepdims=True)
    rec = jnp.zeros(shape, F32)
    for k, val in ((R_E1, i1.astype(F32)), (R_E2, i2.astype(F32)), (R_W1, w1), (R_W2, w2),
                   (R_RANK1, rank1), (R_RANK2, rank2)):
        rec = jnp.where(lane == k, val, rec)
    return rec


def _mix_out_kernel(*refs, route):
    (x_ref, r_ref, a_ref, gr_ref, ga_ref, mod_ref, n2_ref, wr_ref, wa_ref, wo_ref) = refs[:10]
    if route:
        wrt_ref, brt_ref, x1_ref, h2_ref, rec_ref, cnt_ref, count_ref, hbuf, hsem = refs[10:]
    else:
        x1_ref, h2_ref = refs[10:]
    merged = (_sigmoid(gr_ref[...]) * _dot(r_ref[...].astype(BF16), wr_ref[...])
              + _sigmoid(ga_ref[...]) * _dot(a_ref[...], wa_ref[...]))
    mix = _dot(merged.astype(BF16), wo_ref[...])
    x1 = x_ref[...] + mod_ref[2:3, :] * mix
    x1_ref[...] = x1
    h2 = _rms(x1, n2_ref[...]) * (1.0 + mod_ref[4:5, :]) + mod_ref[3:4, :]
    if not route:
        h2_ref[...] = h2.astype(BF16)
        return

    @pl.when(pl.program_id(0) == 0)
    def _():
        count_ref[...] = jnp.zeros_like(count_ref)

    rec_ref[...] = _route(h2, wrt_ref, brt_ref, count_ref)
    cnt_ref[...] = count_ref[...]

    step = pl.program_id(0)
    slot = step % 2

    def h_copies(k, s):
        rows = pl.ds(k * TM, TM)
        return [pltpu.make_async_copy(hbuf.at[s, :, c * LANES:(c + 1) * LANES], h2_ref.at[rows, c],
                                      hsem.at[s]) for c in range(D_MODEL // LANES)]

    @pl.when(step >= 2)
    def _():
        for cp in h_copies(step - 2, slot):
            cp.wait()

    hbuf[slot] = h2
    for cp in h_copies(step, slot):
        cp.start()

    @pl.when(step == pl.num_programs(0) - 1)
    def _():
        for cp in h_copies(step, slot):
            cp.wait()

        @pl.when(step >= 1)
        def _():
            for cp in h_copies(step - 1, 1 - slot):
                cp.wait()


def _mix_out(x, out_r, out_a, gr, ga, mods, l, norm2, wr_bf, wa_bf, wo_bf, wl, seq_len, lat,
             router=None):
    n = x.shape[0]
    tiles_per_seq = seq_len // TM
    if lat:
        mod_idx = lambda i: (l, 1 + i // tiles_per_seq, 0, 0)
    else:
        mod_idx = lambda i: (l, 0, 0, 0)
    row = lambda i: (i, 0)
    wspec = pl.BlockSpec((None, D_MODEL, D_MODEL), lambda i: (wl, 0, 0), pipeline_mode=RESIDENT)
    in_specs = [pl.BlockSpec((TM, D_MODEL), row)] * 5 + [
        pl.BlockSpec((None, None, 6, D_MODEL), mod_idx),
        pl.BlockSpec((None, 1, D_MODEL), lambda i: (l, 0, 0)),
        wspec, wspec, wspec]
    args = [x, out_r, out_a, gr, ga, mods, norm2, wr_bf, wa_bf, wo_bf]
    out_specs = [pl.BlockSpec((TM, D_MODEL), row)]
    out_shape = [jax.ShapeDtypeStruct((n, D_MODEL), F32)]
    scratch = []
    if router is None:
        out_specs.append(pl.BlockSpec((TM, D_MODEL), row))
        out_shape.append(jax.ShapeDtypeStruct((n, D_MODEL), BF16))
    else:
        j, wr_t, br_pad = router
        in_specs += [pl.BlockSpec((None, N_EXPERTS, D_MODEL), lambda i: (j, 0, 0)),
                     pl.BlockSpec((None, 1, LANES), lambda i: (j, 0, 0))]
        args += [wr_t, br_pad]
        out_specs += [pl.BlockSpec(memory_space=pl.ANY),
                      pl.BlockSpec((TM, LANES), row),
                      pl.BlockSpec((1, LANES), lambda i: (0, 0))]
        out_shape += [jax.ShapeDtypeStruct((n, D_MODEL // LANES, LANES), F32),
                      jax.ShapeDtypeStruct((n, LANES), F32),
                      jax.ShapeDtypeStruct((1, LANES), F32)]
        scratch = [pltpu.VMEM((1, LANES), F32), pltpu.VMEM((2, TM, D_MODEL), F32),
                   pltpu.SemaphoreType.DMA((2,))]
    return pl.pallas_call(
        functools.partial(_mix_out_kernel, route=router is not None),
        grid=(n // TM,),
        in_specs=in_specs,
        out_specs=out_specs,
        out_shape=out_shape,
        scratch_shapes=scratch,
        compiler_params=_cparams(("arbitrary",)),
        name=("mix_route" if router is not None else "mix_out") + ("_lat" if lat else "_ctx"),
    )(*args)


def _ffn_kernel(x_ref, h_ref, mod_ref, wg_ref, wu_ref, wd_ref, o_ref, *, f_chunk):
    h = h_ref[...]
    d_ff = wg_ref.shape[1]
    acc = jnp.zeros((h.shape[0], D_MODEL), F32)
    for c in range(d_ff // f_chunk):
        sl = slice(c * f_chunk, (c + 1) * f_chunk)
        act = _silu(_dot(h, wg_ref[:, sl])) * _dot(h, wu_ref[:, sl])
        acc = acc + _dot(act.astype(BF16), wd_ref[sl, :])
    o_ref[...] = x_ref[...] + mod_ref[5:6, :] * acc


def _ffn(x1, h2, mods, l, j, wg_bf, wu_bf, wd_bf, seq_len, lat):
    n = x1.shape[0]
    d_ff = wg_bf.shape[2]
    tiles_per_seq = seq_len // TM
    if lat:
        mod_idx = lambda i: (l, 1 + i // tiles_per_seq, 0, 0)
    else:
        mod_idx = lambda i: (l, 0, 0, 0)
    row = lambda i: (i, 0)
    return pl.pallas_call(
        functools.partial(_ffn_kernel, f_chunk=d_ff // 2),
        grid=(n // TM,),
        in_specs=[pl.BlockSpec((TM, D_MODEL), row), pl.BlockSpec((TM, D_MODEL), row),
                  pl.BlockSpec((None, None, 6, D_MODEL), mod_idx),
                  pl.BlockSpec((None, D_MODEL, d_ff), lambda i: (j, 0, 0), pipeline_mode=RESIDENT),
                  pl.BlockSpec((None, D_MODEL, d_ff), lambda i: (j, 0, 0), pipeline_mode=RESIDENT),
                  pl.BlockSpec((None, d_ff, D_MODEL), lambda i: (j, 0, 0), pipeline_mode=RESIDENT)],
        out_specs=pl.BlockSpec((TM, D_MODEL), row),
        out_shape=jax.ShapeDtypeStruct((n, D_MODEL), F32),
        compiler_params=_cparams(("arbitrary",)),
        name="ffn_lat" if lat else "ffn_ctx",
    )(x1, h2, mods, wg_bf, wu_bf, wd_bf)


def _group_layout(rec, counts, n_tiles):
    as_int = lambda k: rec[:, k].astype(jnp.int32)
    cnt = counts[0, :N_EXPERTS].astype(jnp.int32)
    tiles = (cnt + TM_GROUP - 1) // TM_GROUP
    cum = jnp.cumsum(tiles)
    start = (cum - tiles) * TM_GROUP
    experts = jnp.arange(N_EXPERTS, dtype=jnp.int32)

    def pos(e, rank):
        return jnp.sum(jnp.where(e[:, None] == experts[None, :], start[None, :], 0), axis=1) + rank

    pos2 = jnp.stack([pos(as_int(R_E1), as_int(R_RANK1)), pos(as_int(R_E2), as_int(R_RANK2))])
    n_active = cum[N_EXPERTS - 1]
    tile_expert = jnp.sum(jnp.arange(n_tiles, dtype=jnp.int32)[:, None] >= cum[None, :], axis=1)
    tile_expert = jnp.minimum(tile_expert, tile_expert[n_active - 1]).astype(jnp.int32)
    tile_ids = jnp.arange(n_tiles, dtype=jnp.int32)
    group_last = jnp.any((tile_ids[:, None] == cum[None, :] - 1) & (tiles[None, :] > 0), axis=1)
    zero_tile = (group_last | (tile_ids >= n_active)).astype(jnp.int32)
    return (pos2.astype(jnp.int32), tile_expert, n_active.reshape(1).astype(jnp.int32), zero_tile)


def _row_copy(src, dst, sem):
    return pltpu.make_async_copy(src, dst, sem)


def _dispatch_kernel(pos_ref, zt_ref, h_ref, xs_ref, zbuf, sem, zsem):
    n_chunks = h_ref.shape[0] // DMA_CHUNK
    n_tiles = xs_ref.shape[0] // TM_GROUP

    zbuf[...] = jnp.zeros_like(zbuf)

    def zero_fill(j):
        return pltpu.make_async_copy(zbuf, xs_ref.at[pl.ds(j * TM_GROUP, TM_GROUP)], zsem)

    def zero_start(j, carry):
        @pl.when(zt_ref[j] != 0)
        def _():
            zero_fill(j).start()
        return carry

    def zero_wait(j, carry):
        @pl.when(zt_ref[j] != 0)
        def _():
            zero_fill(j).wait()
        return carry

    lax.fori_loop(0, n_tiles, zero_start, 0)
    lax.fori_loop(0, n_tiles, zero_wait, 0)

    def drain_chunk():
        rows = pl.ds(0, 2 * DMA_CHUNK)
        _row_copy(xs_ref.at[rows], xs_ref.at[rows], sem).wait()

    def chunk(k, carry):
        for r in range(DMA_CHUNK):
            t = k * DMA_CHUNK + r
            _row_copy(h_ref.at[t], xs_ref.at[pos_ref[0, t]], sem).start(priority=0)
            _row_copy(h_ref.at[t], xs_ref.at[pos_ref[1, t]], sem).start(priority=1)

        @pl.when(k > 0)
        def _():
            drain_chunk()

        return carry

    lax.fori_loop(0, n_chunks, chunk, 0)
    drain_chunk()


def _dispatch(pos2, zero_tile, h2_rows, n_rows):
    tile = h2_rows.shape[1:]
    return pl.pallas_call(
        _dispatch_kernel,
        grid_spec=pltpu.PrefetchScalarGridSpec(
            num_scalar_prefetch=2, grid=(1,),
            in_specs=[pl.BlockSpec(h2_rows.shape, lambda i, p, z: (0, 0, 0))],
            out_specs=pl.BlockSpec(memory_space=pl.ANY),
            scratch_shapes=[pltpu.VMEM((TM_GROUP,) + tile, F32),
                            pltpu.SemaphoreType.DMA(()), pltpu.SemaphoreType.DMA(())]),
        out_shape=jax.ShapeDtypeStruct((n_rows,) + tile, F32),
        compiler_params=_cparams(("arbitrary",)),
        name="moe_dispatch",
    )(pos2, zero_tile, h2_rows)


def _expert_kernel(te_ref, na_ref, xs_ref, wg_ref, wu_ref, wd_ref, ys_ref,
                   xbuf, ybuf, sem_in, sem_out):
    del te_ref
    i = pl.program_id(0)
    last = pl.num_programs(0) - 1
    n_active = na_ref[0]
    slot = i % 2
    n_col = D_MODEL // LANES

    def in_copies(tile, s):
        rows = pl.ds(tile * TM_GROUP, TM_GROUP)
        return [pltpu.make_async_copy(xs_ref.at[rows, c], xbuf.at[s, :, c * LANES:(c + 1) * LANES],
                                      sem_in.at[s]) for c in range(n_col)]

    def out_copies(tile, s):
        rows = pl.ds(tile * TM_GROUP, TM_GROUP)
        return [pltpu.make_async_copy(ybuf.at[s, :, c * LANES:(c + 1) * LANES], ys_ref.at[rows, c],
                                      sem_out.at[s]) for c in range(n_col)]

    @pl.when(i == 0)
    def _():
        for cp in in_copies(0, 0):
            cp.start()

    @pl.when(i + 1 < n_active)
    def _():
        for cp in in_copies(i + 1, 1 - slot):
            cp.start()

    @pl.when(i >= 2)
    def _():
        for cp in out_copies(i - 2, slot):
            cp.wait()

    @pl.when(i < n_active)
    def _():
        for cp in in_copies(i, slot):
            cp.wait()
        xb = xbuf[slot].astype(BF16)
        act = _silu(_dot(xb, wg_ref[...])) * _dot(xb, wu_ref[...])
        ybuf[slot] = _dot(act.astype(BF16), wd_ref[...])

    @pl.when(i >= n_active)
    def _():
        ybuf[slot] = jnp.zeros((TM_GROUP, D_MODEL), F32)

    for cp in out_copies(i, slot):
        cp.start()

    @pl.when(i == last)
    def _():
        for cp in out_copies(i, slot):
            cp.wait()

        @pl.when(i >= 1)
        def _():
            for cp in out_copies(i - 1, 1 - slot):
                cp.wait()


def _experts(tile_expert, n_active, xs, j, wg_bf, wu_bf, wd_bf):
    n_rows = xs.shape[0]
    d_e = wg_bf.shape[3]
    wspec = lambda shape: pl.BlockSpec((None, None) + shape, lambda i, te, na: (j, te[i], 0, 0))
    return pl.pallas_call(
        _expert_kernel,
        grid_spec=pltpu.PrefetchScalarGridSpec(
            num_scalar_prefetch=2, grid=(n_rows // TM_GROUP,),
            in_specs=[pl.BlockSpec(memory_space=pl.ANY),
                      wspec((D_MODEL, d_e)), wspec((D_MODEL, d_e)), wspec((d_e, D_MODEL))],
            out_specs=pl.BlockSpec(memory_space=pl.ANY),
            scratch_shapes=[pltpu.VMEM((2, TM_GROUP, D_MODEL), F32),
                            pltpu.VMEM((2, TM_GROUP, D_MODEL), F32),
                            pltpu.SemaphoreType.DMA((2,)), pltpu.SemaphoreType.DMA((2,))]),
        out_shape=jax.ShapeDtypeStruct(xs.shape, F32),
        compiler_params=_cparams(("arbitrary",)),
        name="moe_experts",
    )(tile_expert, n_active, xs, wg_bf, wu_bf, wd_bf)


def _combine_kernel(*refs, final):
    pos_ref, y_ref, x_ref, rec_ref, mod_ref = refs[:5]
    g_ref = refs[5] if final else None
    o_ref, a1, a2, b1, b2, sem_a, sem_b = refs[5 + int(final):]
    i = pl.program_id(0)

    def issue(half_idx, d1, d2, sem):
        base = half_idx * DMA_CHUNK

        for r in range(DMA_CHUNK):
            _row_copy(y_ref.at[pos_ref[0, base + r]], d1.at[r], sem).start(priority=0)
            _row_copy(y_ref.at[pos_ref[1, base + r]], d2.at[r], sem).start(priority=1)

    def drain(d1, d2, sem):
        rows = pl.ds(0, DMA_CHUNK)
        _row_copy(y_ref.at[rows], d1, sem).wait()
        _row_copy(y_ref.at[rows], d2, sem).wait()

    def finish(half, d1, d2):
        rows = slice(half * DMA_CHUNK, (half + 1) * DMA_CHUNK)
        w1 = rec_ref[rows, R_W1:R_W1 + 1]
        w2 = rec_ref[rows, R_W2:R_W2 + 1]
        for c in range(D_MODEL // LANES):
            sl = slice(c * LANES, (c + 1) * LANES)
            y = w1 * d1[:, c, :] + w2 * d2[:, c, :]
            o_ref[rows, sl] = x_ref[rows, sl] + mod_ref[5:6, sl] * y
        if final:
            o_ref[rows, :] = _rms(o_ref[rows, :], g_ref[...])

    @pl.when(i == 0)
    def _():
        issue(0, a1, a2, sem_a)

    issue(2 * i + 1, b1, b2, sem_b)
    drain(a1, a2, sem_a)
    finish(0, a1, a2)

    @pl.when(i + 1 < pl.num_programs(0))
    def _():
        issue(2 * i + 2, a1, a2, sem_a)

    drain(b1, b2, sem_b)
    finish(1, b1, b2)


def _combine(pos2, ys, x1, rec, mods, l, seq_len, lat, final_g):
    n = x1.shape[0]
    final = final_g is not None
    tile = ys.shape[1:]
    tm = 2 * DMA_CHUNK
    tiles_per_seq = max(seq_len // tm, 1)
    if lat:
        mod_idx = lambda i, p: (l, 1 + i // tiles_per_seq, 0, 0)
    else:
        mod_idx = lambda i, p: (l, 0, 0, 0)
    row = lambda i, p: (i, 0)
    slot = pltpu.VMEM((DMA_CHUNK,) + tile, F32)
    in_specs = [pl.BlockSpec(memory_space=pl.ANY),
                pl.BlockSpec((tm, D_MODEL), row),
                pl.BlockSpec((tm, LANES), row),
                pl.BlockSpec((None, None, 6, D_MODEL), mod_idx)]
    args = [pos2, ys, x1, rec, mods]
    if final:
        in_specs.append(pl.BlockSpec((1, D_MODEL), lambda i, p: (0, 0)))
        args.append(final_g)
    return pl.pallas_call(
        functools.partial(_combine_kernel, final=final),
        grid_spec=pltpu.PrefetchScalarGridSpec(
            num_scalar_prefetch=1, grid=(n // tm,),
            in_specs=in_specs,
            out_specs=pl.BlockSpec((tm, D_MODEL), row),
            scratch_shapes=[slot, slot, slot, slot,
                            pltpu.SemaphoreType.DMA(()), pltpu.SemaphoreType.DMA(())]),
        out_shape=jax.ShapeDtypeStruct((n, D_MODEL), F32),
        compiler_params=_cparams(("arbitrary",)),
        name="moe_combine_final" if final else "moe_combine",
    )(*args)


def _moe(x1, h2_rows, rec, counts, mods, l, j, wg_bf, wu_bf, wd_bf, seq_len, lat, final_g):
    n = x1.shape[0]
    n_tiles = 2 * n // TM_GROUP + N_EXPERTS
    pos2, tile_expert, n_active, zero_tile = _group_layout(rec, counts, n_tiles)
    xs = _dispatch(pos2, zero_tile, h2_rows, n_tiles * TM_GROUP)
    ys = _experts(tile_expert, n_active, xs, j, wg_bf, wu_bf, wd_bf)
    return _combine(pos2, ys, x1, rec, mods, l, seq_len, lat, final_g)


def _final_norm_kernel(x_ref, g_ref, o_ref):
    o_ref[...] = _rms(x_ref[...], g_ref[...])


def _final_norm(x, g):
    n = x.shape[0]
    row = lambda i: (i, 0)
    return pl.pallas_call(
        _final_norm_kernel,
        grid=(n // TM_FF,),
        in_specs=[pl.BlockSpec((TM_FF, D_MODEL), row), pl.BlockSpec((1, D_MODEL), lambda i: (0, 0))],
        out_specs=pl.BlockSpec((TM_FF, D_MODEL), row),
        out_shape=jax.ShapeDtypeStruct((n, D_MODEL), F32),
        compiler_params=_cparams(("arbitrary",)),
        name="final_norm",
    )(x, g)


def _rope_tables(t_len):
    t = jnp.arange(t_len)
    pos = jnp.stack([t // GRID_W, t % GRID_W], axis=-1).astype(F32)
    inv = ROPE_THETA ** (-jnp.arange(0, AXIS_DIM, 2, dtype=F32) / AXIS_DIM)
    ang = pos[..., None] * inv
    cos, sin = jnp.cos(ang), jnp.sin(ang)
    c = jnp.concatenate([cos[:, 0], cos[:, 0], cos[:, 1], cos[:, 1]], axis=-1)
    s = jnp.concatenate([-sin[:, 0], sin[:, 0], -sin[:, 1], sin[:, 1]], axis=-1)
    return c, s


def _gate_weights(wa, wi):
    depth = wa.shape[0]
    groups = D_RNN // LANES

    def pair(w):
        w = w.reshape(depth, 2, groups, 2, RNN_BLOCK, RNN_BLOCK)
        z = jnp.zeros_like(w[:, :, :, 0])
        top = jnp.concatenate([w[:, :, :, 0], z], axis=-1)
        bot = jnp.concatenate([z, w[:, :, :, 1]], axis=-1)
        return jnp.concatenate([top, bot], axis=-2)

    w = jnp.concatenate([pair(wa), pair(wi)], axis=-1)
    return (0.5 * jnp.concatenate([w[:, 0], w[:, 1]], axis=-1)).astype(BF16)


def kernel(x_prompt, x_sample, cache_k, cache_v, state_rnn, c, c_ctx, w_mod, b_mod, norm1, norm2, w_in, conv_w, conv_b, lru_wa, lru_ba, lru_wi, lru_bi, lru_lambda, q_norm, k_norm, w_rnn_out, w_attn_out, w_out, ff_gate, ff_up, ff_down, router_w, router_b, exp_gate, exp_up, exp_down, final_norm):
    batch, seq, _ = x_prompt.shape
    dec_batch, dec_seq, _ = x_sample.shape
    depth = w_mod.shape[0]
    past = cache_k.shape[2]
    assert dec_batch + 1 <= MOD_ROWS

    cond8 = jnp.zeros((MOD_ROWS, D_MODEL), F32).at[0].set(c_ctx).at[1:1 + dec_batch].set(c)
    mods = _modulation(cond8, w_mod, b_mod).reshape(depth, MOD_ROWS, 6, D_MODEL)

    gate_w = _gate_weights(lru_wa, lru_wi)
    n_moe = router_w.shape[0]
    wr_t = jnp.swapaxes(router_w, 1, 2)
    br_pad = jnp.zeros((n_moe, 1, LANES), F32).at[:, 0, :N_EXPERTS].set(router_b)

    norm1_3 = norm1.reshape(depth, 1, D_MODEL)
    norm2_3 = norm2.reshape(depth, 1, D_MODEL)
    qn3 = q_norm.reshape(depth, 1, HEAD_DIM)
    kn3 = k_norm.reshape(depth, 1, HEAD_DIM)
    conv_b3 = conv_b.reshape(depth, 1, D_RNN)
    ba4 = (0.5 * lru_ba).reshape(depth, 2, 1, D_RNN)
    bi4 = (0.5 * lru_bi).reshape(depth, 2, 1, D_RNN)
    lam4 = lru_lambda.reshape(depth, 2, 1, D_RNN)
    state5 = state_rnn.reshape(dec_batch, depth, 2, 1, D_RNN)
    cache_k4 = cache_k.reshape(dec_batch, depth, past, KV_DIM)
    cache_v4 = cache_v.reshape(dec_batch, depth, past, KV_DIM)
    rope_tabs = _rope_tables(dec_seq)
    final_g = final_norm.reshape(1, D_MODEL)

    n_steps = dec_batch * dec_seq // TQ
    assert n_steps == dec_batch * dec_seq // TM_ROPE

    def flat(w):
        return w.reshape(-1, w.shape[-1])

    def channel_names(l):
        return ("exp_gate", "exp_up", "exp_down") if l % 2 == 1 else ("ff_gate", "ff_up", "ff_down")

    stacked = dict(w_in=w_in, w_rnn_out=w_rnn_out, w_attn_out=w_attn_out, w_out=w_out,
                   ff_gate=ff_gate, ff_up=ff_up, ff_down=ff_down,
                   exp_gate=exp_gate, exp_up=exp_up, exp_down=exp_down)

    def layer_shape(name):
        return (1,) + stacked[name].shape[1:]

    def layer_index(name, l):
        return l if name.startswith("w_") else l // 2

    def rider(name, l):
        w = stacked[name]
        rows = flat(w).shape[0] // w.shape[0] // n_steps
        return _Rider(flat(w), rows, layer_index(name, l) * n_steps)

    def names_of(l):
        return ("w_in", "w_rnn_out", "w_attn_out", "w_out") + channel_names(l)

    weights = {0: {name: stacked[name][layer_index(name, 0)][None].astype(BF16)
                   for name in names_of(0)}}

    def run_layer(x, l, lat):
        seq_len = dec_seq if lat else seq
        wts = weights[l]
        convert_next = lat and l + 1 < depth
        proj_names = names_of(l + 1)[:4] + names_of(l + 1)[6:] if convert_next else ()
        attn_names = names_of(l + 1)[4:6] if convert_next else ()
        outs = _in_proj(x, mods, l, norm1_3, wts["w_in"], 0, qn3, kn3,
                        rope_tabs if lat else None, seq_len,
                        riders=[rider(name, l + 1) for name in proj_names])
        xr, yr, q, kn, kr, v, gr, ga = outs[:8]
        converted = dict(zip(proj_names, outs[8:]))
        out_r, end_f, end_b = _rnn(xr, yr, l, conv_w, conv_b3, gate_w, ba4, bi4, lam4,
                                   state5 if lat else None, seq_len)
        if lat:
            outs = _attn_lat(q, kr, v, cache_k4, cache_v4, l, seq_len,
                             riders=[rider(name, l + 1) for name in attn_names])
            out_a = outs[0]
            converted.update(zip(attn_names, outs[1:]))
        else:
            out_a = _attn_ctx(q, kr, v, seq_len)
        if convert_next:
            weights[l + 1] = {name: converted[name].reshape(layer_shape(name))
                              for name in names_of(l + 1)}
        mix_args = (x, out_r, out_a, gr, ga, mods, l, norm2_3, wts["w_rnn_out"],
                    wts["w_attn_out"], wts["w_out"], 0, seq_len, lat)
        if l % 2 == 1:
            x1, h2_rows, rec, counts = _mix_out(*mix_args, router=(l // 2, wr_t, br_pad))
            x = _moe(x1, h2_rows, rec, counts, mods, l, 0, wts["exp_gate"], wts["exp_up"],
                     wts["exp_down"], seq_len, lat, final_g if l == depth - 1 else None)
        else:
            x1, h2 = _mix_out(*mix_args)
            x = _ffn(x1, h2, mods, l, 0, wts["ff_gate"], wts["ff_up"], wts["ff_down"],
                     seq_len, lat)
            if l == depth - 1:
                x = _final_norm(x, final_g)
        return x, kn, v, jnp.stack([end_f, end_b], axis=1)

    y_ctx = x_prompt.reshape(batch * seq, D_MODEL)
    y_lat = x_sample.reshape(dec_batch * dec_seq, D_MODEL)
    ks, vs, ss = [], [], []
    for l in range(depth):
        y_ctx, kn, v, st = run_layer(y_ctx, l, False)
        ks.append(kn)
        vs.append(v)
        ss.append(st)
        y_lat, _, _, _ = run_layer(y_lat, l, True)

    new_k = jnp.stack([k.reshape(batch, seq, N_KV_HEADS, HEAD_DIM) for k in ks], axis=1)
    new_v = jnp.stack([v.reshape(batch, seq, N_KV_HEADS, HEAD_DIM) for v in vs], axis=1)
    new_state = jnp.stack(ss, axis=1)
    return (y_ctx.reshape(batch, seq, D_MODEL), y_lat.reshape(dec_batch, dec_seq, D_MODEL),
            new_k, new_v, new_state)
```

```python
import functools

import jax
import jax.numpy as jnp
from jax import lax
from jax.experimental import pallas as pl
from jax.experimental.pallas import tpu as pltpu

F32 = jnp.float32
BF16 = jnp.bfloat16

D_MODEL = 1024
D_RNN = 1024
N_RNN_BLOCKS = 16
RNN_BLOCK = D_RNN // N_RNN_BLOCKS
CONV_W = 4
CONV_LEFT = 2
LRU_C = 8.0
N_HEADS = 8
N_KV_HEADS = 2
HEAD_DIM = 128
AXIS_DIM = HEAD_DIM // 2
ROPE_THETA = 10000.0
GRID_W = 64
Q_DIM = N_HEADS * HEAD_DIM
KV_DIM = N_KV_HEADS * HEAD_DIM
N_EXPERTS = 8
EPS = 1e-6
LOG2_E = 1.4426950408889634
RSQRT_FLOOR = 1e-30
D_IN = 2 * D_RNN + Q_DIM + 2 * KV_DIM + 2 * D_MODEL
O_XR, O_YR, O_Q = 0, D_RNN, 2 * D_RNN
O_K = O_Q + Q_DIM
O_V = O_K + KV_DIM
O_GR = O_V + KV_DIM
O_GA = O_GR + D_MODEL

LANES = 128
SUBLANES = 8
MOD_ROWS = 8
VMEM_LIMIT = 56 * 1024 * 1024

TM = 512
TM_ROPE = 256
RESIDENT = pl.Buffered(1)
RNN_T = 256
RNN_SEG = SUBLANES
RNN_C = 512
RNN_TCH = 32
RNN_SCAN_UNROLL = 8
TQ = 256
TM_FF = 512
TM_GROUP = 256
DMA_CHUNK = 256


def _cparams(sem):
    return pltpu.CompilerParams(dimension_semantics=sem, vmem_limit_bytes=VMEM_LIMIT)


def _sigmoid(x):
    return 1.0 / (1.0 + jnp.exp(-x))


def _silu(x):
    return x * _sigmoid(x)


def _gelu_tanh(x):
    return 0.5 * x * (1.0 + jnp.tanh(0.7978845608028654 * (x + 0.044715 * (x * x * x))))


def _rms(x, g):
    ms = jnp.mean(x * x, axis=-1, keepdims=True)
    return x * lax.rsqrt(ms + EPS) * g


def _dot(a, b):
    return jnp.dot(a, b, preferred_element_type=F32)


def _mod_kernel(cond_ref, w_ref, b_ref, o_ref):
    s = _silu(cond_ref[...]).astype(BF16)
    o_ref[...] = _dot(s, w_ref[...].astype(BF16)) + b_ref[...]


def _modulation(cond8, w_mod, b_mod):
    depth = w_mod.shape[0]
    tn = 1536
    return pl.pallas_call(
        _mod_kernel,
        grid=(depth, 6 * D_MODEL // tn),
        in_specs=[
            pl.BlockSpec((MOD_ROWS, D_MODEL), lambda l, j: (0, 0)),
            pl.BlockSpec((None, D_MODEL, tn), lambda l, j: (l, 0, j)),
            pl.BlockSpec((None, 1, tn), lambda l, j: (l, 0, j)),
        ],
        out_specs=pl.BlockSpec((None, MOD_ROWS, tn), lambda l, j: (l, 0, j)),
        out_shape=jax.ShapeDtypeStruct((depth, MOD_ROWS, 6 * D_MODEL), F32),
        compiler_params=_cparams(("arbitrary", "arbitrary")),
        name="adaln_mod",
    )(cond8, w_mod, b_mod.reshape(depth, 1, 6 * D_MODEL))


def _swap_halves(x):
    lane = lax.broadcasted_iota(jnp.int32, x.shape, 1)
    return jnp.where((lane % AXIS_DIM) < AXIS_DIM // 2,
                     pltpu.roll(x, LANES - AXIS_DIM // 2, axis=1),
                     pltpu.roll(x, AXIS_DIM // 2, axis=1))


class _Rider:
    def __init__(self, w2d, rows, first):
        self.w2d, self.rows, self.first = w2d, rows, first


def _rider_specs(riders, n_steps, step_of):
    in_specs, out_specs, out_shape = [], [], []
    for r in riders:
        cols = r.w2d.shape[1]
        assert r.rows % 16 == 0 and (r.first + n_steps) * r.rows <= r.w2d.shape[0]
        in_specs.append(pl.BlockSpec((r.rows, cols), lambda *g, r=r: (r.first + step_of(*g), 0)))
        out_specs.append(pl.BlockSpec((r.rows, cols), lambda *g: (step_of(*g), 0)))
        out_shape.append(jax.ShapeDtypeStruct((r.rows * n_steps, cols), BF16))
    return in_specs, out_specs, out_shape


def _convert_riders(in_refs, out_refs):
    for w_ref, o_ref in zip(in_refs, out_refs):
        o_ref[...] = w_ref[...].astype(BF16)


def _convert_kernel(*refs):
    _convert_riders(refs[:len(refs) // 2], refs[len(refs) // 2:])


def _convert(riders, n_steps):
    r_in, r_out, r_shape = _rider_specs(riders, n_steps, lambda i: i)
    return pl.pallas_call(
        _convert_kernel,
        grid=(n_steps,),
        in_specs=r_in,
        out_specs=r_out,
        out_shape=r_shape,
        compiler_params=_cparams(("arbitrary",)),
        name="convert_weights",
    )(*[r.w2d for r in riders])


def _in_proj_kernel(*refs, rope, n_riders):
    n_in = 8 if rope else 6
    _convert_riders(refs[n_in:n_in + n_riders], refs[n_in + n_riders + 8:])
    refs = refs[:n_in] + refs[n_in + n_riders:n_in + n_riders + 8]
    if rope:
        (x_ref, mod_ref, n1_ref, w_ref, qn_ref, kn_ref, c_ref, s_ref,
         xr_ref, yr_ref, q_ref, kno_ref, kro_ref, v_ref, gr_ref, ga_ref) = refs
    else:
        (x_ref, mod_ref, n1_ref, w_ref, qn_ref, kn_ref,
         xr_ref, yr_ref, q_ref, kno_ref, kro_ref, v_ref, gr_ref, ga_ref) = refs
    x = x_ref[...]
    h = _rms(x, n1_ref[...]) * (1.0 + mod_ref[1:2, :]) + mod_ref[0:1, :]
    hb = h.astype(BF16)

    xr_ref[...] = _dot(hb, w_ref[:, O_XR:O_XR + D_RNN])
    yr_ref[...] = _dot(hb, w_ref[:, O_YR:O_YR + D_RNN])
    gr_ref[...] = _dot(hb, w_ref[:, O_GR:O_GR + D_MODEL])
    ga_ref[...] = _dot(hb, w_ref[:, O_GA:O_GA + D_MODEL])
    v_ref[...] = _dot(hb, w_ref[:, O_V:O_V + KV_DIM])

    if rope:
        cs, sn = c_ref[...], s_ref[...]
    zq = _dot(hb, w_ref[:, O_Q:O_Q + Q_DIM])
    for hd in range(N_HEADS):
        t = _rms(zq[:, hd * HEAD_DIM:(hd + 1) * HEAD_DIM], qn_ref[...])
        if rope:
            t = t * cs + _swap_halves(t) * sn
        q_ref[:, hd * HEAD_DIM:(hd + 1) * HEAD_DIM] = t.astype(BF16)
    zk = _dot(hb, w_ref[:, O_K:O_K + KV_DIM])
    for hd in range(N_KV_HEADS):
        t = _rms(zk[:, hd * HEAD_DIM:(hd + 1) * HEAD_DIM], kn_ref[...])
        kno_ref[:, hd * HEAD_DIM:(hd + 1) * HEAD_DIM] = t
        if rope:
            t = t * cs + _swap_halves(t) * sn
        kro_ref[:, hd * HEAD_DIM:(hd + 1) * HEAD_DIM] = t.astype(BF16)


def _in_proj(x, mods, l, norm1, w_in_bf, wl, q_norm, k_norm, rope_tabs, seq_len, riders=()):
    n = x.shape[0]
    rope = rope_tabs is not None
    tm = TM_ROPE if rope else TM
    tiles_per_seq = seq_len // tm
    if rope:
        mod_idx = lambda i: (l, 1 + i // tiles_per_seq, 0, 0)
    else:
        mod_idx = lambda i: (l, 0, 0, 0)
    row = lambda i: (i, 0)
    in_specs = [
        pl.BlockSpec((tm, D_MODEL), row),
        pl.BlockSpec((None, None, 6, D_MODEL), mod_idx),
        pl.BlockSpec((None, 1, D_MODEL), lambda i: (l, 0, 0)),
        pl.BlockSpec((None, D_MODEL, D_IN), lambda i: (wl, 0, 0), pipeline_mode=RESIDENT),
        pl.BlockSpec((None, 1, HEAD_DIM), lambda i: (l, 0, 0)),
        pl.BlockSpec((None, 1, HEAD_DIM), lambda i: (l, 0, 0)),
    ]
    args = [x, mods, norm1, w_in_bf, q_norm, k_norm]
    if rope:
        in_specs += [pl.BlockSpec((tm, HEAD_DIM), lambda i: (i % tiles_per_seq, 0))] * 2
        args += list(rope_tabs)
    widths = [(D_RNN, F32), (D_RNN, F32), (Q_DIM, BF16), (KV_DIM, F32), (KV_DIM, BF16),
              (KV_DIM, F32), (D_MODEL, F32), (D_MODEL, F32)]
    r_in, r_out, r_shape = _rider_specs(riders, n // tm, lambda i: i)
    return pl.pallas_call(
        functools.partial(_in_proj_kernel, rope=rope, n_riders=len(riders)),
        grid=(n // tm,),
        in_specs=in_specs + r_in,
        out_specs=[pl.BlockSpec((tm, w), row) for w, _ in widths] + r_out,
        out_shape=[jax.ShapeDtypeStruct((n, w), dt) for w, dt in widths] + r_shape,
        compiler_params=_cparams(("arbitrary",)),
        name="in_proj_lat" if rope else "in_proj_ctx",
    )(*args, *[r.w2d for r in riders])


def _rnn_kernel(*refs, chained):
    refs = list(refs)
    xr_ref, yr_ref, cw_ref, cb_ref, gw_ref, ba_ref, bi_ref, lam_ref = refs[:8]
    refs = refs[8:]
    h0_ref = refs.pop(0) if chained else None
    out_ref, sf_ref, sb_ref, xext, ybuf, obuf, a_f, b_f, a_b, b_b, sem_in, sem_out = refs

    g = pl.program_id(0)
    cb = pl.program_id(1)
    n_cb = pl.num_programs(1)
    step = g * n_cb + cb
    n_steps = pl.num_programs(0) * n_cb
    slot = step % 2

    def hbm_window(kk, s):
        rows = pl.ds(pl.multiple_of(((kk // n_cb) * RNN_SEG + s) * RNN_T, RNN_T), RNN_T)
        cols = pl.ds(pl.multiple_of((kk % n_cb) * RNN_C, RNN_C), RNN_C)
        return rows, cols

    def in_copies(kk, sl_):
        cps = []
        for s in range(RNN_SEG):
            win = hbm_window(kk, s)
            cps.append(pltpu.make_async_copy(
                xr_ref.at[win], xext.at[sl_, pl.ds(CONV_LEFT, RNN_T), s], sem_in.at[sl_]))
            cps.append(pltpu.make_async_copy(yr_ref.at[win], ybuf.at[sl_, :, s], sem_in.at[sl_]))
        return cps

    def out_copies(kk, sl_):
        return [pltpu.make_async_copy(obuf.at[sl_, :, s], out_ref.at[hbm_window(kk, s)],
                                      sem_out.at[sl_]) for s in range(RNN_SEG)]

    @pl.when(step == 0)
    def _():
        for cp in in_copies(0, 0):
            cp.start()

    @pl.when(step + 1 < n_steps)
    def _():
        for cp in in_copies(step + 1, 1 - slot):
            cp.start()

    for cp in in_copies(step, slot):
        cp.wait()

    seg = lax.broadcasted_iota(jnp.int32, (RNN_SEG, RNN_C), 0)
    zero_row = jnp.zeros((RNN_SEG, RNN_C), F32)
    if chained:
        xext[slot, 0] = jnp.where(seg == 0, zero_row, pltpu.roll(xext[slot, RNN_T], 1, axis=0))
        xext[slot, 1] = jnp.where(seg == 0, zero_row, pltpu.roll(xext[slot, RNN_T + 1], 1, axis=0))
        xext[slot, RNN_T + CONV_LEFT] = jnp.where(
            seg == RNN_SEG - 1, zero_row, pltpu.roll(xext[slot, CONV_LEFT], RNN_SEG - 1, axis=0))
    else:
        xext[slot, 0] = zero_row
        xext[slot, 1] = zero_row
        xext[slot, RNN_T + CONV_LEFT] = zero_row

    scans = ((a_f, b_f), (a_b, b_b))
    rows2d = RNN_TCH * RNN_SEG

    def gate_chunk(tc, carry):
        t0 = pl.multiple_of(tc * RNN_TCH, RNN_TCH)
        for j in range(RNN_C // LANES):
            sl = slice(j * LANES, (j + 1) * LANES)
            xc = jnp.broadcast_to(cb_ref[:, sl].reshape(1, 1, LANES), (RNN_TCH, RNN_SEG, LANES))
            for k in range(CONV_W):
                xc = xc + (xext[slot, pl.ds(t0 + k, RNN_TCH), :, sl]
                           * cw_ref[k:k + 1, sl].reshape(1, 1, LANES))
            xc = xc.reshape(rows2d, LANES)
            pre = _dot(xc.astype(BF16), gw_ref[j])
            for d, (a_s, b_s) in enumerate(scans):
                lam = lam_ref[d, :, sl]
                log2_decay = (-LRU_C * LOG2_E) * (jnp.maximum(-lam, 0.0)
                                                  + jnp.log(1.0 + jnp.exp(-jnp.abs(lam))))
                r = 0.5 * jnp.tanh(pre[:, 2 * d * LANES:(2 * d + 1) * LANES]
                                   + ba_ref[d, :, sl]) + 0.5
                gi = 0.5 * jnp.tanh(pre[:, (2 * d + 1) * LANES:(2 * d + 2) * LANES]
                                    + bi_ref[d, :, sl]) + 0.5
                a = jnp.exp2(r * log2_decay)
                y = 1.0 - a * a
                b = (y * lax.rsqrt(jnp.maximum(y, RSQRT_FLOOR))) * (gi * xc)
                a_s[pl.ds(t0, RNN_TCH), :, sl] = a.reshape(RNN_TCH, RNN_SEG, LANES)
                b_s[pl.ds(t0, RNN_TCH), :, sl] = b.reshape(RNN_TCH, RNN_SEG, LANES)
        return carry

    lax.fori_loop(0, RNN_T // RNN_TCH, gate_chunk, 0)

    def scan_step(t, carry):
        hf, hb, pf, pb = carry
        tb = RNN_T - 1 - t
        af_t = a_f[t]
        hf = af_t * hf + b_f[t]
        b_f[t] = hf
        ab_t = a_b[tb]
        hb = ab_t * hb + b_b[tb]
        b_b[tb] = hb
        if chained:
            pf = af_t * pf
            a_f[t] = pf
            pb = ab_t * pb
            a_b[tb] = pb
        return hf, hb, pf, pb

    ones = jnp.ones((RNN_SEG, RNN_C), F32)
    hf, hb, pf, pb = lax.fori_loop(0, RNN_T, scan_step, (zero_row, zero_row, ones, ones),
                                   unroll=RNN_SCAN_UNROLL)

    if chained:
        st_f = jnp.broadcast_to(h0_ref[0], (RNN_SEG, RNN_C))
        st_b = jnp.broadcast_to(h0_ref[1], (RNN_SEG, RNN_C))
        in_f, in_b = st_f, st_b
        for _ in range(RNN_SEG - 1):
            in_f = jnp.where(seg == 0, st_f, pltpu.roll(pf * in_f + hf, 1, axis=0))
            in_b = jnp.where(seg == RNN_SEG - 1, st_b,
                             pltpu.roll(pb * in_b + hb, RNN_SEG - 1, axis=0))
        sf_ref[...] = pf * in_f + hf
        sb_ref[...] = pb * in_b + hb
    else:
        sf_ref[...] = hf
        sb_ref[...] = hb

    @pl.when(step >= 2)
    def _():
        for cp in out_copies(step - 2, slot):
            cp.wait()

    def finish_chunk(tc, carry):
        rows = pl.ds(pl.multiple_of(tc * RNN_TCH, RNN_TCH), RNN_TCH)
        h_f, h_b = b_f[rows], b_b[rows]
        if chained:
            h_f = a_f[rows] * in_f.reshape(1, RNN_SEG, RNN_C) + h_f
            h_b = a_b[rows] * in_b.reshape(1, RNN_SEG, RNN_C) + h_b
        obuf[slot, rows] = (h_f + h_b) * _gelu_tanh(ybuf[slot, rows])
        return carry

    lax.fori_loop(0, RNN_T // RNN_TCH, finish_chunk, 0)

    for cp in out_copies(step, slot):
        cp.start()

    @pl.when(step == n_steps - 1)
    def _():
        for cp in out_copies(step, slot):
            cp.wait()

        @pl.when(step >= 1)
        def _():
            for cp in out_copies(step - 1, 1 - slot):
                cp.wait()


def _rnn(xr, yr, l, conv_w, conv_b, gate_w, lru_ba, lru_bi, lru_lambda, state5, seq_len):
    n = xr.shape[0]
    chained = state5 is not None
    group_len = RNN_SEG * RNN_T
    assert n % group_len == 0 and seq_len in ((group_len,) if chained else (RNN_T,))
    n_groups = n // group_len
    lanes_per_step = RNN_C // LANES
    vec = pl.BlockSpec((None, 2, 1, RNN_C), lambda g, c: (l, 0, 0, c))
    in_specs = [
        pl.BlockSpec(memory_space=pl.ANY),
        pl.BlockSpec(memory_space=pl.ANY),
        pl.BlockSpec((None, CONV_W, RNN_C), lambda g, c: (l, 0, c)),
        pl.BlockSpec((None, 1, RNN_C), lambda g, c: (l, 0, c)),
        pl.BlockSpec((None, lanes_per_step, LANES, 4 * LANES), lambda g, c: (l, c, 0, 0)),
        vec, vec, vec,
    ]
    args = [xr, yr, conv_w, conv_b, gate_w, lru_ba, lru_bi, lru_lambda]
    if chained:
        in_specs.append(pl.BlockSpec((None, None, 2, 1, RNN_C), lambda g, c: (g, l, 0, 0, c)))
        args.append(state5)
    state_spec = pl.BlockSpec((RNN_SEG, RNN_C), lambda g, c: (g, c))
    state_shape = jax.ShapeDtypeStruct((n_groups * RNN_SEG, D_RNN), F32)
    work = lambda t: pltpu.VMEM((t, RNN_SEG, RNN_C), F32)
    return pl.pallas_call(
        functools.partial(_rnn_kernel, chained=chained),
        grid=(n_groups, D_RNN // RNN_C),
        in_specs=in_specs,
        out_specs=[pl.BlockSpec(memory_space=pl.ANY), state_spec, state_spec],
        out_shape=[jax.ShapeDtypeStruct((n, D_RNN), F32), state_shape, state_shape],
        scratch_shapes=[
            pltpu.VMEM((2, RNN_T + CONV_W - 1, RNN_SEG, RNN_C), F32),
            pltpu.VMEM((2, RNN_T, RNN_SEG, RNN_C), F32),
            pltpu.VMEM((2, RNN_T, RNN_SEG, RNN_C), F32),
            work(RNN_T), work(RNN_T), work(RNN_T), work(RNN_T),
            pltpu.SemaphoreType.DMA((2,)), pltpu.SemaphoreType.DMA((2,)),
        ],
        compiler_params=_cparams(("arbitrary", "arbitrary")),
        name="rnn_lat" if chained else "rnn_ctx",
    )(*args)


def _attend_heads(q_ref, k_all, v_all, o_ref, stack_heads):
    exp2_scale = (1.0 / (HEAD_DIM ** 0.5)) * LOG2_E
    group = N_HEADS // N_KV_HEADS
    tq = q_ref.shape[0]
    for kh in range(N_KV_HEADS):
        k = k_all[:, kh * HEAD_DIM:(kh + 1) * HEAD_DIM]
        v = v_all[:, kh * HEAD_DIM:(kh + 1) * HEAD_DIM]
        heads = [slice((kh * group + g) * HEAD_DIM, (kh * group + g + 1) * HEAD_DIM)
                 for g in range(group)]
        stacks = [heads] if stack_heads else [[sl] for sl in heads]
        for stack in stacks:
            q = jnp.concatenate([q_ref[:, sl] for sl in stack], axis=0)
            s = lax.dot_general(q, k, (((1,), (1,)), ((), ())), preferred_element_type=F32)
            m = jnp.max(s, axis=-1, keepdims=True)
            p = jnp.exp2((s - m) * exp2_scale)
            denom = jnp.sum(p, axis=-1, keepdims=True)
            o = (_dot(p.astype(BF16), v) / denom).astype(BF16)
            for g, sl in enumerate(stack):
                o_ref[:, sl] = o[g * tq:(g + 1) * tq]


def _attn_ctx_kernel(q_ref, k_ref, v_ref, o_ref):
    _attend_heads(q_ref, k_ref[...], v_ref[...].astype(BF16), o_ref, stack_heads=True)


def _attn_ctx(q, kr, v, seq_len):
    n = q.shape[0]
    row = lambda b: (b, 0)
    return pl.pallas_call(
        _attn_ctx_kernel,
        grid=(n // seq_len,),
        in_specs=[pl.BlockSpec((seq_len, Q_DIM), row),
                  pl.BlockSpec((seq_len, KV_DIM), row),
                  pl.BlockSpec((seq_len, KV_DIM), row)],
        out_specs=pl.BlockSpec((seq_len, Q_DIM), row),
        out_shape=jax.ShapeDtypeStruct((n, Q_DIM), BF16),
        compiler_params=_cparams(("arbitrary",)),
        name="attn_ctx",
    )(q, kr, v)


def _attn_lat_kernel(*refs, past, n_riders):
    _convert_riders(refs[5:5 + n_riders], refs[6 + n_riders:6 + 2 * n_riders])
    q_ref, kc_ref, vc_ref, kl_ref, vl_ref = refs[:5]
    o_ref = refs[5 + n_riders]
    k_all, v_all = refs[6 + 2 * n_riders:]

    @pl.when(pl.program_id(1) == 0)
    def _():
        k_all[0:past, :] = kc_ref[...].astype(BF16)
        k_all[past:, :] = kl_ref[...]
        v_all[0:past, :] = vc_ref[...].astype(BF16)
        v_all[past:, :] = vl_ref[...].astype(BF16)

    _attend_heads(q_ref, k_all[...], v_all[...], o_ref, stack_heads=False)


def _attn_lat(q, kr, v, cache_k4, cache_v4, l, seq_len, riders=()):
    n = q.shape[0]
    past = cache_k4.shape[2]
    nq = seq_len // TQ
    step_of = lambda b, j: b * nq + j
    r_in, r_out, r_shape = _rider_specs(riders, (n // seq_len) * nq, step_of)
    return pl.pallas_call(
        functools.partial(_attn_lat_kernel, past=past, n_riders=len(riders)),
        grid=(n // seq_len, nq),
        in_specs=[pl.BlockSpec((TQ, Q_DIM), lambda b, j: (b * nq + j, 0)),
                  pl.BlockSpec((None, None, past, KV_DIM), lambda b, j: (b, l, 0, 0)),
                  pl.BlockSpec((None, None, past, KV_DIM), lambda b, j: (b, l, 0, 0)),
                  pl.BlockSpec((seq_len, KV_DIM), lambda b, j: (b, 0)),
                  pl.BlockSpec((seq_len, KV_DIM), lambda b, j: (b, 0))] + r_in,
        out_specs=[pl.BlockSpec((TQ, Q_DIM), lambda b, j: (b * nq + j, 0))] + r_out,
        out_shape=[jax.ShapeDtypeStruct((n, Q_DIM), BF16)] + r_shape,
        scratch_shapes=[pltpu.VMEM((past + seq_len, KV_DIM), BF16),
                        pltpu.VMEM((past + seq_len, KV_DIM), BF16)],
        compiler_params=_cparams(("arbitrary", "arbitrary")),
        name="attn_lat",
    )(q, cache_k4, cache_v4, kr, v, *[r.w2d for r in riders])


R_E1, R_E2, R_W1, R_W2, R_RANK1, R_RANK2 = range(6)


def _route(h2, wrt_ref, brt_ref, count_ref):
    shape = (h2.shape[0], LANES)
    lane = lax.broadcasted_iota(jnp.int32, shape, 1)
    neg = jnp.float32(-jnp.inf)
    logits = jnp.full(shape, neg, F32)
    for e in range(N_EXPERTS):
        prod = h2 * wrt_ref[e:e + 1, :]
        part = prod[:, 0:LANES]
        for c in range(1, D_MODEL // LANES):
            part = part + prod[:, c * LANES:(c + 1) * LANES]
        logits = jnp.where(lane == e, jnp.sum(part, axis=-1, keepdims=True), logits)
    logits = logits + brt_ref[...]
    v1 = jnp.max(logits, axis=-1, keepdims=True)
    i1 = jnp.min(jnp.where(logits == v1, lane, LANES), axis=-1, keepdims=True)
    rest = jnp.where(lane == i1, neg, logits)
    v2 = jnp.max(rest, axis=-1, keepdims=True)
    i2 = jnp.min(jnp.where(rest == v2, lane, LANES), axis=-1, keepdims=True)
    e2 = jnp.exp(v2 - v1)
    w1 = 1.0 / (1.0 + e2)
    w2 = e2 / (1.0 + e2)
    chosen = jnp.where((lane == i1) | (lane == i2), 1.0, 0.0)
    r_id = lax.broadcasted_iota(jnp.int32, (shape[0], shape[0]), 0)
    c_id = lax.broadcasted_iota(jnp.int32, (shape[0], shape[0]), 1)
    before = jnp.where(c_id < r_id, 1.0, 0.0).astype(BF16)
    rank = _dot(before, chosen.astype(BF16)) + count_ref[...]
    rank1 = jnp.sum(jnp.where(lane == i1, rank, 0.0), axis=-1, keepdims=True)
    rank2 = jnp.sum(jnp.where(lane == i2, rank, 0.0), axis=-1, keepdims=True)
    count_ref[...] += jnp.sum(chosen, axis=0, keepdims=True)
    rec = jnp.zeros(shape, F32)
    for k, val in ((R_E1, i1.astype(F32)), (R_E2, i2.astype(F32)), (R_W1, w1), (R_W2, w2),
                   (R_RANK1, rank1), (R_RANK2, rank2)):
        rec = jnp.where(lane == k, val, rec)
    return rec


def _mix_out_kernel(*refs, route):
    (x_ref, r_ref, a_ref, gr_ref, ga_ref, mod_ref, n2_ref, wr_ref, wa_ref, wo_ref) = refs[:10]
    if route:
        wrt_ref, brt_ref, x1_ref, h2_ref, rec_ref, cnt_ref, count_ref, hbuf, hsem = refs[10:]
    else:
        x1_ref, h2_ref = refs[10:]
    merged = (_sigmoid(gr_ref[...]) * _dot(r_ref[...].astype(BF16), wr_ref[...])
              + _sigmoid(ga_ref[...]) * _dot(a_ref[...], wa_ref[...]))
    mix = _dot(merged.astype(BF16), wo_ref[...])
    x1 = x_ref[...] + mod_ref[2:3, :] * mix
    x1_ref[...] = x1
    h2 = _rms(x1, n2_ref[...]) * (1.0 + mod_ref[4:5, :]) + mod_ref[3:4, :]
    if not route:
        h2_ref[...] = h2.astype(BF16)
        return

    @pl.when(pl.program_id(0) == 0)
    def _():
        count_ref[...] = jnp.zeros_like(count_ref)

    rec_ref[...] = _route(h2, wrt_ref, brt_ref, count_ref)
    cnt_ref[...] = count_ref[...]

    step = pl.program_id(0)
    slot = step % 2

    def h_copies(k, s):
        rows = pl.ds(k * TM, TM)
        return [pltpu.make_async_copy(hbuf.at[s, :, c * LANES:(c + 1) * LANES], h2_ref.at[rows, c],
                                      hsem.at[s]) for c in range(D_MODEL // LANES)]

    @pl.when(step >= 2)
    def _():
        for cp in h_copies(step - 2, slot):
            cp.wait()

    hbuf[slot] = h2
    for cp in h_copies(step, slot):
        cp.start()

    @pl.when(step == pl.num_programs(0) - 1)
    def _():
        for cp in h_copies(step, slot):
            cp.wait()

        @pl.when(step >= 1)
        def _():
            for cp in h_copies(step - 1, 1 - slot):
                cp.wait()


def _mix_out(x, out_r, out_a, gr, ga, mods, l, norm2, wr_bf, wa_bf, wo_bf, wl, seq_len, lat,
             router=None):
    n = x.shape[0]
    tiles_per_seq = seq_len // TM
    if lat:
        mod_idx = lambda i: (l, 1 + i // tiles_per_seq, 0, 0)
    else:
        mod_idx = lambda i: (l, 0, 0, 0)
    row = lambda i: (i, 0)
    wspec = pl.BlockSpec((None, D_MODEL, D_MODEL), lambda i: (wl, 0, 0), pipeline_mode=RESIDENT)
    in_specs = [pl.BlockSpec((TM, D_MODEL), row)] * 5 + [
        pl.BlockSpec((None, None, 6, D_MODEL), mod_idx),
        pl.BlockSpec((None, 1, D_MODEL), lambda i: (l, 0, 0)),
        wspec, wspec, wspec]
    args = [x, out_r, out_a, gr, ga, mods, norm2, wr_bf, wa_bf, wo_bf]
    out_specs = [pl.BlockSpec((TM, D_MODEL), row)]
    out_shape = [jax.ShapeDtypeStruct((n, D_MODEL), F32)]
    scratch = []
    if router is None:
        out_specs.append(pl.BlockSpec((TM, D_MODEL), row))
        out_shape.append(jax.ShapeDtypeStruct((n, D_MODEL), BF16))
    else:
        j, wr_t, br_pad = router
        in_specs += [pl.BlockSpec((None, N_EXPERTS, D_MODEL), lambda i: (j, 0, 0)),
                     pl.BlockSpec((None, 1, LANES), lambda i: (j, 0, 0))]
        args += [wr_t, br_pad]
        out_specs += [pl.BlockSpec(memory_space=pl.ANY),
                      pl.BlockSpec((TM, LANES), row),
                      pl.BlockSpec((1, LANES), lambda i: (0, 0))]
        out_shape += [jax.ShapeDtypeStruct((n, D_MODEL // LANES, LANES), F32),
                      jax.ShapeDtypeStruct((n, LANES), F32),
                      jax.ShapeDtypeStruct((1, LANES), F32)]
        scratch = [pltpu.VMEM((1, LANES), F32), pltpu.VMEM((2, TM, D_MODEL), F32),
                   pltpu.SemaphoreType.DMA((2,))]
    return pl.pallas_call(
        functools.partial(_mix_out_kernel, route=router is not None),
        grid=(n // TM,),
        in_specs=in_specs,
        out_specs=out_specs,
        out_shape=out_shape,
        scratch_shapes=scratch,
        compiler_params=_cparams(("arbitrary",)),
        name=("mix_route" if router is not None else "mix_out") + ("_lat" if lat else "_ctx"),
    )(*args)


def _ffn_kernel(x_ref, h_ref, mod_ref, wg_ref, wu_ref, wd_ref, o_ref, *, f_chunk):
    h = h_ref[...]
    d_ff = wg_ref.shape[1]
    acc = jnp.zeros((h.shape[0], D_MODEL), F32)
    for c in range(d_ff // f_chunk):
        sl = slice(c * f_chunk, (c + 1) * f_chunk)
        act = _silu(_dot(h, wg_ref[:, sl])) * _dot(h, wu_ref[:, sl])
        acc = acc + _dot(act.astype(BF16), wd_ref[sl, :])
    o_ref[...] = x_ref[...] + mod_ref[5:6, :] * acc


def _ffn(x1, h2, mods, l, j, wg_bf, wu_bf, wd_bf, seq_len, lat):
    n = x1.shape[0]
    d_ff = wg_bf.shape[2]
    tiles_per_seq = seq_len // TM
    if lat:
        mod_idx = lambda i: (l, 1 + i // tiles_per_seq, 0, 0)
    else:
        mod_idx = lambda i: (l, 0, 0, 0)
    row = lambda i: (i, 0)
    return pl.pallas_call(
        functools.partial(_ffn_kernel, f_chunk=d_ff // 2),
        grid=(n // TM,),
        in_specs=[pl.BlockSpec((TM, D_MODEL), row), pl.BlockSpec((TM, D_MODEL), row),
                  pl.BlockSpec((None, None, 6, D_MODEL), mod_idx),
                  pl.BlockSpec((None, D_MODEL, d_ff), lambda i: (j, 0, 0), pipeline_mode=RESIDENT),
                  pl.BlockSpec((None, D_MODEL, d_ff), lambda i: (j, 0, 0), pipeline_mode=RESIDENT),
                  pl.BlockSpec((None, d_ff, D_MODEL), lambda i: (j, 0, 0), pipeline_mode=RESIDENT)],
        out_specs=pl.BlockSpec((TM, D_MODEL), row),
        out_shape=jax.ShapeDtypeStruct((n, D_MODEL), F32),
        compiler_params=_cparams(("arbitrary",)),
        name="ffn_lat" if lat else "ffn_ctx",
    )(x1, h2, mods, wg_bf, wu_bf, wd_bf)


def _group_layout(rec, counts, n_tiles):
    as_int = lambda k: rec[:, k].astype(jnp.int32)
    cnt = counts[0, :N_EXPERTS].astype(jnp.int32)
    tiles = (cnt + TM_GROUP - 1) // TM_GROUP
    cum = jnp.cumsum(tiles)
    start = (cum - tiles) * TM_GROUP
    experts = jnp.arange(N_EXPERTS, dtype=jnp.int32)

    def pos(e, rank):
        return jnp.sum(jnp.where(e[:, None] == experts[None, :], start[None, :], 0), axis=1) + rank

    pos2 = jnp.stack([pos(as_int(R_E1), as_int(R_RANK1)), pos(as_int(R_E2), as_int(R_RANK2))])
    n_active = cum[N_EXPERTS - 1]
    tile_expert = jnp.sum(jnp.arange(n_tiles, dtype=jnp.int32)[:, None] >= cum[None, :], axis=1)
    tile_expert = jnp.minimum(tile_expert, tile_expert[n_active - 1]).astype(jnp.int32)
    tile_ids = jnp.arange(n_tiles, dtype=jnp.int32)
    group_last = jnp.any((tile_ids[:, None] == cum[None, :] - 1) & (tiles[None, :] > 0), axis=1)
    zero_tile = (group_last | (tile_ids >= n_active)).astype(jnp.int32)
    return (pos2.astype(jnp.int32), tile_expert, n_active.reshape(1).astype(jnp.int32), zero_tile)


def _row_copy(src, dst, sem):
    return pltpu.make_async_copy(src, dst, sem)


def _dispatch_kernel(pos_ref, zt_ref, h_ref, xs_ref, zbuf, sem, zsem):
    n_chunks = h_ref.shape[0] // DMA_CHUNK
    n_tiles = xs_ref.shape[0] // TM_GROUP

    zbuf[...] = jnp.zeros_like(zbuf)

    def zero_fill(j):
        return pltpu.make_async_copy(zbuf, xs_ref.at[pl.ds(j * TM_GROUP, TM_GROUP)], zsem)

    def zero_start(j, carry):
        @pl.when(zt_ref[j] != 0)
        def _():
            zero_fill(j).start()
        return carry

    def zero_wait(j, carry):
        @pl.when(zt_ref[j] != 0)
        def _():
            zero_fill(j).wait()
        return carry

    lax.fori_loop(0, n_tiles, zero_start, 0)
    lax.fori_loop(0, n_tiles, zero_wait, 0)

    def drain_chunk():
        rows = pl.ds(0, 2 * DMA_CHUNK)
        _row_copy(xs_ref.at[rows], xs_ref.at[rows], sem).wait()

    def chunk(k, carry):
        for r in range(DMA_CHUNK):
            t = k * DMA_CHUNK + r
            _row_copy(h_ref.at[t], xs_ref.at[pos_ref[0, t]], sem).start(priority=0)
            _row_copy(h_ref.at[t], xs_ref.at[pos_ref[1, t]], sem).start(priority=1)

        @pl.when(k > 0)
        def _():
            drain_chunk()

        return carry

    lax.fori_loop(0, n_chunks, chunk, 0)
    drain_chunk()


def _dispatch(pos2, zero_tile, h2_rows, n_rows):
    tile = h2_rows.shape[1:]
    return pl.pallas_call(
        _dispatch_kernel,
        grid_spec=pltpu.PrefetchScalarGridSpec(
            num_scalar_prefetch=2, grid=(1,),
            in_specs=[pl.BlockSpec(h2_rows.shape, lambda i, p, z: (0, 0, 0))],
            out_specs=pl.BlockSpec(memory_space=pl.ANY),
            scratch_shapes=[pltpu.VMEM((TM_GROUP,) + tile, F32),
                            pltpu.SemaphoreType.DMA(()), pltpu.SemaphoreType.DMA(())]),
        out_shape=jax.ShapeDtypeStruct((n_rows,) + tile, F32),
        compiler_params=_cparams(("arbitrary",)),
        name="moe_dispatch",
    )(pos2, zero_tile, h2_rows)


def _expert_kernel(te_ref, na_ref, xs_ref, wg_ref, wu_ref, wd_ref, ys_ref,
                   xbuf, ybuf, sem_in, sem_out):
    del te_ref
    i = pl.program_id(0)
    last = pl.num_programs(0) - 1
    n_active = na_ref[0]
    slot = i % 2
    n_col = D_MODEL // LANES

    def in_copies(tile, s):
        rows = pl.ds(tile * TM_GROUP, TM_GROUP)
        return [pltpu.make_async_copy(xs_ref.at[rows, c], xbuf.at[s, :, c * LANES:(c + 1) * LANES],
                                      sem_in.at[s]) for c in range(n_col)]

    def out_copies(tile, s):
        rows = pl.ds(tile * TM_GROUP, TM_GROUP)
        return [pltpu.make_async_copy(ybuf.at[s, :, c * LANES:(c + 1) * LANES], ys_ref.at[rows, c],
                                      sem_out.at[s]) for c in range(n_col)]

    @pl.when(i == 0)
    def _():
        for cp in in_copies(0, 0):
            cp.start()

    @pl.when(i + 1 < n_active)
    def _():
        for cp in in_copies(i + 1, 1 - slot):
            cp.start()

    @pl.when(i >= 2)
    def _():
        for cp in out_copies(i - 2, slot):
            cp.wait()

    @pl.when(i < n_active)
    def _():
        for cp in in_copies(i, slot):
            cp.wait()
        xb = xbuf[slot].astype(BF16)
        act = _silu(_dot(xb, wg_ref[...])) * _dot(xb, wu_ref[...])
        ybuf[slot] = _dot(act.astype(BF16), wd_ref[...])

    @pl.when(i >= n_active)
    def _():
        ybuf[slot] = jnp.zeros((TM_GROUP, D_MODEL), F32)

    for cp in out_copies(i, slot):
        cp.start()

    @pl.when(i == last)
    def _():
        for cp in out_copies(i, slot):
            cp.wait()

        @pl.when(i >= 1)
        def _():
            for cp in out_copies(i - 1, 1 - slot):
                cp.wait()


def _experts(tile_expert, n_active, xs, j, wg_bf, wu_bf, wd_bf):
    n_rows = xs.shape[0]
    d_e = wg_bf.shape[3]
    wspec = lambda shape: pl.BlockSpec((None, None) + shape, lambda i, te, na: (j, te[i], 0, 0))
    return pl.pallas_call(
        _expert_kernel,
        grid_spec=pltpu.PrefetchScalarGridSpec(
            num_scalar_prefetch=2, grid=(n_rows // TM_GROUP,),
            in_specs=[pl.BlockSpec(memory_space=pl.ANY),
                      wspec((D_MODEL, d_e)), wspec((D_MODEL, d_e)), wspec((d_e, D_MODEL))],
            out_specs=pl.BlockSpec(memory_space=pl.ANY),
            scratch_shapes=[pltpu.VMEM((2, TM_GROUP, D_MODEL), F32),
                            pltpu.VMEM((2, TM_GROUP, D_MODEL), F32),
                            pltpu.SemaphoreType.DMA((2,)), pltpu.SemaphoreType.DMA((2,))]),
        out_shape=jax.ShapeDtypeStruct(xs.shape, F32),
        compiler_params=_cparams(("arbitrary",)),
        name="moe_experts",
    )(tile_expert, n_active, xs, wg_bf, wu_bf, wd_bf)


def _combine_kernel(*refs, final):
    pos_ref, y_ref, x_ref, rec_ref, mod_ref = refs[:5]
    g_ref = refs[5] if final else None
    o_ref, a1, a2, b1, b2, sem_a, sem_b = refs[5 + int(final):]
    i = pl.program_id(0)

    def issue(half_idx, d1, d2, sem):
        base = half_idx * DMA_CHUNK

        for r in range(DMA_CHUNK):
            _row_copy(y_ref.at[pos_ref[0, base + r]], d1.at[r], sem).start(priority=0)
            _row_copy(y_ref.at[pos_ref[1, base + r]], d2.at[r], sem).start(priority=1)

    def drain(d1, d2, sem):
        rows = pl.ds(0, DMA_CHUNK)
        _row_copy(y_ref.at[rows], d1, sem).wait()
        _row_copy(y_ref.at[rows], d2, sem).wait()

    def finish(half, d1, d2):
        rows = slice(half * DMA_CHUNK, (half + 1) * DMA_CHUNK)
        w1 = rec_ref[rows, R_W1:R_W1 + 1]
        w2 = rec_ref[rows, R_W2:R_W2 + 1]
        for c in range(D_MODEL // LANES):
            sl = slice(c * LANES, (c + 1) * LANES)
            y = w1 * d1[:, c, :] + w2 * d2[:, c, :]
            o_ref[rows, sl] = x_ref[rows, sl] + mod_ref[5:6, sl] * y
        if final:
            o_ref[rows, :] = _rms(o_ref[rows, :], g_ref[...])

    @pl.when(i == 0)
    def _():
        issue(0, a1, a2, sem_a)

    issue(2 * i + 1, b1, b2, sem_b)
    drain(a1, a2, sem_a)
    finish(0, a1, a2)

    @pl.when(i + 1 < pl.num_programs(0))
    def _():
        issue(2 * i + 2, a1, a2, sem_a)

    drain(b1, b2, sem_b)
    finish(1, b1, b2)


def _combine(pos2, ys, x1, rec, mods, l, seq_len, lat, final_g):
    n = x1.shape[0]
    final = final_g is not None
    tile = ys.shape[1:]
    tm = 2 * DMA_CHUNK
    tiles_per_seq = max(seq_len // tm, 1)
    if lat:
        mod_idx = lambda i, p: (l, 1 + i // tiles_per_seq, 0, 0)
    else:
        mod_idx = lambda i, p: (l, 0, 0, 0)
    row = lambda i, p: (i, 0)
    slot = pltpu.VMEM((DMA_CHUNK,) + tile, F32)
    in_specs = [pl.BlockSpec(memory_space=pl.ANY),
                pl.BlockSpec((tm, D_MODEL), row),
                pl.BlockSpec((tm, LANES), row),
                pl.BlockSpec((None, None, 6, D_MODEL), mod_idx)]
    args = [pos2, ys, x1, rec, mods]
    if final:
        in_specs.append(pl.BlockSpec((1, D_MODEL), lambda i, p: (0, 0)))
        args.append(final_g)
    return pl.pallas_call(
        functools.partial(_combine_kernel, final=final),
        grid_spec=pltpu.PrefetchScalarGridSpec(
            num_scalar_prefetch=1, grid=(n // tm,),
            in_specs=in_specs,
            out_specs=pl.BlockSpec((tm, D_MODEL), row),
            scratch_shapes=[slot, slot, slot, slot,
                            pltpu.SemaphoreType.DMA(()), pltpu.SemaphoreType.DMA(())]),
        out_shape=jax.ShapeDtypeStruct((n, D_MODEL), F32),
        compiler_params=_cparams(("arbitrary",)),
        name="moe_combine_final" if final else "moe_combine",
    )(*args)


def _moe(x1, h2_rows, rec, counts, mods, l, j, wg_bf, wu_bf, wd_bf, seq_len, lat, final_g):
    n = x1.shape[0]
    n_tiles = 2 * n // TM_GROUP + N_EXPERTS
    pos2, tile_expert, n_active, zero_tile = _group_layout(rec, counts, n_tiles)
    xs = _dispatch(pos2, zero_tile, h2_rows, n_tiles * TM_GROUP)
    ys = _experts(tile_expert, n_active, xs, j, wg_bf, wu_bf, wd_bf)
    return _combine(pos2, ys, x1, rec, mods, l, seq_len, lat, final_g)


def _final_norm_kernel(x_ref, g_ref, o_ref):
    o_ref[...] = _rms(x_ref[...], g_ref[...])


def _final_norm(x, g):
    n = x.shape[0]
    row = lambda i: (i, 0)
    return pl.pallas_call(
        _final_norm_kernel,
        grid=(n // TM_FF,),
        in_specs=[pl.BlockSpec((TM_FF, D_MODEL), row), pl.BlockSpec((1, D_MODEL), lambda i: (0, 0))],
        out_specs=pl.BlockSpec((TM_FF, D_MODEL), row),
        out_shape=jax.ShapeDtypeStruct((n, D_MODEL), F32),
        compiler_params=_cparams(("arbitrary",)),
        name="final_norm",
    )(x, g)


def _rope_tables(t_len):
    t = jnp.arange(t_len)
    pos = jnp.stack([t // GRID_W, t % GRID_W], axis=-1).astype(F32)
    inv = ROPE_THETA ** (-jnp.arange(0, AXIS_DIM, 2, dtype=F32) / AXIS_DIM)
    ang = pos[..., None] * inv
    cos, sin = jnp.cos(ang), jnp.sin(ang)
    c = jnp.concatenate([cos[:, 0], cos[:, 0], cos[:, 1], cos[:, 1]], axis=-1)
    s = jnp.concatenate([-sin[:, 0], sin[:, 0], -sin[:, 1], sin[:, 1]], axis=-1)
    return c, s


def _gate_weights(wa, wi):
    depth = wa.shape[0]
    groups = D_RNN // LANES
    per_group = LANES // RNN_BLOCK
    w = jnp.stack([wa[:, 0], wi[:, 0], wa[:, 1], wi[:, 1]], axis=1)
    w = w.reshape(depth, 4, groups, per_group, RNN_BLOCK, RNN_BLOCK)
    w = w.transpose(0, 2, 3, 4, 1, 5)
    row_blk = jnp.arange(per_group).reshape(1, 1, per_group, 1, 1, 1, 1)
    col_blk = jnp.arange(per_group).reshape(1, 1, 1, 1, 1, per_group, 1)
    w = jnp.where(row_blk == col_blk, w[:, :, :, :, :, None, :], 0.0)
    return (0.5 * w).reshape(depth, groups, LANES, 4 * LANES).astype(BF16)


def kernel(x_prompt, x_sample, cache_k, cache_v, state_rnn, c, c_ctx, w_mod, b_mod, norm1, norm2, w_in, conv_w, conv_b, lru_wa, lru_ba, lru_wi, lru_bi, lru_lambda, q_norm, k_norm, w_rnn_out, w_attn_out, w_out, ff_gate, ff_up, ff_down, router_w, router_b, exp_gate, exp_up, exp_down, final_norm):
    batch, seq, _ = x_prompt.shape
    dec_batch, dec_seq, _ = x_sample.shape
    depth = w_mod.shape[0]
    past = cache_k.shape[2]
    assert dec_batch + 1 <= MOD_ROWS

    cond8 = jnp.zeros((MOD_ROWS, D_MODEL), F32).at[0].set(c_ctx).at[1:1 + dec_batch].set(c)
    mods = _modulation(cond8, w_mod, b_mod).reshape(depth, MOD_ROWS, 6, D_MODEL)

    gate_w = _gate_weights(lru_wa, lru_wi)
    n_moe = router_w.shape[0]
    wr_t = jnp.swapaxes(router_w, 1, 2)
    br_pad = jnp.zeros((n_moe, 1, LANES), F32).at[:, 0, :N_EXPERTS].set(router_b)

    norm1_3 = norm1.reshape(depth, 1, D_MODEL)
    norm2_3 = norm2.reshape(depth, 1, D_MODEL)
    qn3 = q_norm.reshape(depth, 1, HEAD_DIM)
    kn3 = k_norm.reshape(depth, 1, HEAD_DIM)
    conv_b3 = conv_b.reshape(depth, 1, D_RNN)
    ba4 = (0.5 * lru_ba).reshape(depth, 2, 1, D_RNN)
    bi4 = (0.5 * lru_bi).reshape(depth, 2, 1, D_RNN)
    lam4 = lru_lambda.reshape(depth, 2, 1, D_RNN)
    state5 = state_rnn.reshape(dec_batch, depth, 2, 1, D_RNN)
    cache_k4 = cache_k.reshape(dec_batch, depth, past, KV_DIM)
    cache_v4 = cache_v.reshape(dec_batch, depth, past, KV_DIM)
    rope_tabs = _rope_tables(dec_seq)
    final_g = final_norm.reshape(1, D_MODEL)

    n_steps = dec_batch * dec_seq // TQ
    assert n_steps == dec_batch * dec_seq // TM_ROPE

    def flat(w):
        return w.reshape(-1, w.shape[-1])

    def channel_names(l):
        return ("exp_gate", "exp_up", "exp_down") if l % 2 == 1 else ("ff_gate", "ff_up", "ff_down")

    stacked = dict(w_in=w_in, w_rnn_out=w_rnn_out, w_attn_out=w_attn_out, w_out=w_out,
                   ff_gate=ff_gate, ff_up=ff_up, ff_down=ff_down,
                   exp_gate=exp_gate, exp_up=exp_up, exp_down=exp_down)

    def layer_shape(name):
        return (1,) + stacked[name].shape[1:]

    def layer_index(name, l):
        return l if name.startswith("w_") else l // 2

    def rider(name, l, steps):
        w = stacked[name]
        rows = flat(w).shape[0] // w.shape[0] // steps
        return _Rider(flat(w), rows, layer_index(name, l) * steps)

    def names_of(l):
        return ("w_in", "w_rnn_out", "w_attn_out", "w_out") + channel_names(l)

    weights = {l: {} for l in range(depth)}

    def keep(l, names, arrays):
        for name, arr in zip(names, arrays):
            weights[l][name] = arr.reshape(layer_shape(name))

    keep(0, ("w_in",), _convert([rider("w_in", 0, n_steps)], n_steps))
    ctx_steps = batch * seq // TM

    def run_layer(x, l, lat):
        seq_len = dec_seq if lat else seq
        wts = weights[l]
        convert_next = lat and l + 1 < depth
        if convert_next:
            proj_names = names_of(l + 1)[:4] + names_of(l + 1)[6:]
            attn_names = names_of(l + 1)[4:6]
            proj_riders = [rider(name, l + 1, n_steps) for name in proj_names]
            attn_riders = [rider(name, l + 1, n_steps) for name in attn_names]
            target = l + 1
        elif l == 0 and not lat:
            proj_names, attn_names, attn_riders, target = names_of(0)[1:], (), [], 0
            proj_riders = [rider(name, 0, ctx_steps) for name in proj_names]
        else:
            proj_names, attn_names, proj_riders, attn_riders, target = (), (), [], [], None
        outs = _in_proj(x, mods, l, norm1_3, wts["w_in"], 0, qn3, kn3,
                        rope_tabs if lat else None, seq_len, riders=proj_riders)
        xr, yr, q, kn, kr, v, gr, ga = outs[:8]
        if target is not None:
            keep(target, proj_names, outs[8:])
        out_r, end_f, end_b = _rnn(xr, yr, l, conv_w, conv_b3, gate_w, ba4, bi4, lam4,
                                   state5 if lat else None, seq_len)
        if lat:
            outs = _attn_lat(q, kr, v, cache_k4, cache_v4, l, seq_len, riders=attn_riders)
            out_a = outs[0]
            if target is not None:
                keep(target, attn_names, outs[1:])
        else:
            out_a = _attn_ctx(q, kr, v, seq_len)
        mix_args = (x, out_r, out_a, gr, ga, mods, l, norm2_3, wts["w_rnn_out"],
                    wts["w_attn_out"], wts["w_out"], 0, seq_len, lat)
        if l % 2 == 1:
            x1, h2_rows, rec, counts = _mix_out(*mix_args, router=(l // 2, wr_t, br_pad))
            x = _moe(x1, h2_rows, rec, counts, mods, l, 0, wts["exp_gate"], wts["exp_up"],
                     wts["exp_down"], seq_len, lat, final_g if l == depth - 1 else None)
        else:
            x1, h2 = _mix_out(*mix_args)
            x = _ffn(x1, h2, mods, l, 0, wts["ff_gate"], wts["ff_up"], wts["ff_down"],
                     seq_len, lat)
            if l == depth - 1:
                x = _final_norm(x, final_g)
        return x, kn, v, jnp.stack([end_f, end_b], axis=1)

    y_ctx = x_prompt.reshape(batch * seq, D_MODEL)
    y_lat = x_sample.reshape(dec_batch * dec_seq, D_MODEL)
    ks, vs, ss = [], [], []
    for l in range(depth):
        y_ctx, kn, v, st = run_layer(y_ctx, l, False)
        ks.append(kn)
        vs.append(v)
        ss.append(st)
        y_lat, _, _, _ = run_layer(y_lat, l, True)

    new_k = jnp.stack([k.reshape(batch, seq, N_KV_HEADS, HEAD_DIM) for k in ks], axis=1)
    new_v = jnp.stack([v.reshape(batch, seq, N_KV_HEADS, HEAD_DIM) for v in vs], axis=1)
    new_state = jnp.stack(ss, axis=1)
    return (y_ctx.reshape(batch, seq, D_MODEL), y_lat.reshape(dec_batch, dec_seq, D_MODEL),
            new_k, new_v, new_state)
```

```python
import functools

import jax
import jax.numpy as jnp
from jax import lax
from jax.experimental import pallas as pl
from jax.experimental.pallas import tpu as pltpu

F32 = jnp.float32
BF16 = jnp.bfloat16

D_MODEL = 1024
D_RNN = 1024
N_RNN_BLOCKS = 16
RNN_BLOCK = D_RNN // N_RNN_BLOCKS
CONV_W = 4
CONV_LEFT = 2
LRU_C = 8.0
N_HEADS = 8
N_KV_HEADS = 2
HEAD_DIM = 128
AXIS_DIM = HEAD_DIM // 2
ROPE_THETA = 10000.0
GRID_W = 64
Q_DIM = N_HEADS * HEAD_DIM
KV_DIM = N_KV_HEADS * HEAD_DIM
N_EXPERTS = 8
EPS = 1e-6
LOG2_E = 1.4426950408889634
RSQRT_FLOOR = 1e-30
D_IN = 2 * D_RNN + Q_DIM + 2 * KV_DIM + 2 * D_MODEL
O_XR, O_YR, O_Q = 0, D_RNN, 2 * D_RNN
O_K = O_Q + Q_DIM
O_V = O_K + KV_DIM
O_GR = O_V + KV_DIM
O_GA = O_GR + D_MODEL

LANES = 128
SUBLANES = 8
MOD_ROWS = 8
VMEM_LIMIT = 56 * 1024 * 1024

TM = 512
TM_ROPE = 256
RESIDENT = pl.Buffered(1)
RNN_T = 256
RNN_SEG = SUBLANES
RNN_C = 512
RNN_TCH = 32
RNN_SCAN_UNROLL = 8
TQ = 256
TM_FF = 512
TM_GROUP = 256
DMA_CHUNK = 256


def _cparams(sem):
    return pltpu.CompilerParams(dimension_semantics=sem, vmem_limit_bytes=VMEM_LIMIT)


def _sigmoid(x):
    return 1.0 / (1.0 + jnp.exp(-x))


def _silu(x):
    return x * _sigmoid(x)


def _gelu_tanh(x):
    return 0.5 * x * (1.0 + jnp.tanh(0.7978845608028654 * (x + 0.044715 * (x * x * x))))


def _rms(x, g):
    ms = jnp.mean(x * x, axis=-1, keepdims=True)
    return x * lax.rsqrt(ms + EPS) * g


def _dot(a, b):
    return jnp.dot(a, b, preferred_element_type=F32)


def _mod_kernel(cond_ref, w_ref, b_ref, o_ref):
    s = _silu(cond_ref[...]).astype(BF16)
    o_ref[...] = _dot(s, w_ref[...].astype(BF16)) + b_ref[...]


def _modulation(cond8, w_mod, b_mod):
    depth = w_mod.shape[0]
    tn = 1536
    return pl.pallas_call(
        _mod_kernel,
        grid=(depth, 6 * D_MODEL // tn),
        in_specs=[
            pl.BlockSpec((MOD_ROWS, D_MODEL), lambda l, j: (0, 0)),
            pl.BlockSpec((None, D_MODEL, tn), lambda l, j: (l, 0, j)),
            pl.BlockSpec((None, 1, tn), lambda l, j: (l, 0, j)),
        ],
        out_specs=pl.BlockSpec((None, MOD_ROWS, tn), lambda l, j: (l, 0, j)),
        out_shape=jax.ShapeDtypeStruct((depth, MOD_ROWS, 6 * D_MODEL), F32),
        compiler_params=_cparams(("arbitrary", "arbitrary")),
        name="adaln_mod",
    )(cond8, w_mod, b_mod.reshape(depth, 1, 6 * D_MODEL))


def _swap_halves(x):
    lane = lax.broadcasted_iota(jnp.int32, x.shape, 1)
    return jnp.where((lane % AXIS_DIM) < AXIS_DIM // 2,
                     pltpu.roll(x, LANES - AXIS_DIM // 2, axis=1),
                     pltpu.roll(x, AXIS_DIM // 2, axis=1))


class _Rider:
    def __init__(self, w2d, rows, first):
        self.w2d, self.rows, self.first = w2d, rows, first


def _rider_specs(riders, n_steps, step_of):
    in_specs, out_specs, out_shape = [], [], []
    for r in riders:
        cols = r.w2d.shape[1]
        assert r.rows % 16 == 0 and (r.first + n_steps) * r.rows <= r.w2d.shape[0]
        in_specs.append(pl.BlockSpec((r.rows, cols), lambda *g, r=r: (r.first + step_of(*g), 0)))
        out_specs.append(pl.BlockSpec((r.rows, cols), lambda *g: (step_of(*g), 0)))
        out_shape.append(jax.ShapeDtypeStruct((r.rows * n_steps, cols), BF16))
    return in_specs, out_specs, out_shape


def _convert_riders(in_refs, out_refs):
    for w_ref, o_ref in zip(in_refs, out_refs):
        o_ref[...] = w_ref[...].astype(BF16)


def _convert_kernel(*refs):
    _convert_riders(refs[:len(refs) // 2], refs[len(refs) // 2:])


def _convert(riders, n_steps):
    r_in, r_out, r_shape = _rider_specs(riders, n_steps, lambda i: i)
    return pl.pallas_call(
        _convert_kernel,
        grid=(n_steps,),
        in_specs=r_in,
        out_specs=r_out,
        out_shape=r_shape,
        compiler_params=_cparams(("arbitrary",)),
        name="convert_weights",
    )(*[r.w2d for r in riders])


def _in_proj_kernel(*refs, rope, n_riders, cache_layer, seq_len):
    n_in = 8 if rope else 6
    n_cache = 0 if cache_layer is None else 2
    first_out = n_in + n_riders + n_cache
    _convert_riders(refs[n_in:n_in + n_riders], refs[first_out + 8:first_out + 8 + n_riders])
    cache_refs = refs[first_out + 8 + n_riders:]
    refs = refs[:n_in] + refs[first_out:first_out + 8]
    if rope:
        (x_ref, mod_ref, n1_ref, w_ref, qn_ref, kn_ref, c_ref, s_ref,
         xr_ref, yr_ref, q_ref, kno_ref, kro_ref, v_ref, gr_ref, ga_ref) = refs
    else:
        (x_ref, mod_ref, n1_ref, w_ref, qn_ref, kn_ref,
         xr_ref, yr_ref, q_ref, kno_ref, kro_ref, v_ref, gr_ref, ga_ref) = refs
    x = x_ref[...]
    h = _rms(x, n1_ref[...]) * (1.0 + mod_ref[1:2, :]) + mod_ref[0:1, :]
    hb = h.astype(BF16)

    xr_ref[...] = _dot(hb, w_ref[:, O_XR:O_XR + D_RNN])
    yr_ref[...] = _dot(hb, w_ref[:, O_YR:O_YR + D_RNN])
    gr_ref[...] = _dot(hb, w_ref[:, O_GR:O_GR + D_MODEL])
    ga_ref[...] = _dot(hb, w_ref[:, O_GA:O_GA + D_MODEL])
    v_ref[...] = _dot(hb, w_ref[:, O_V:O_V + KV_DIM])

    if rope:
        cs, sn = c_ref[...], s_ref[...]
    zq = _dot(hb, w_ref[:, O_Q:O_Q + Q_DIM])
    for hd in range(N_HEADS):
        t = _rms(zq[:, hd * HEAD_DIM:(hd + 1) * HEAD_DIM], qn_ref[...])
        if rope:
            t = t * cs + _swap_halves(t) * sn
        q_ref[:, hd * HEAD_DIM:(hd + 1) * HEAD_DIM] = t.astype(BF16)
    zk = _dot(hb, w_ref[:, O_K:O_K + KV_DIM])
    for hd in range(N_KV_HEADS):
        t = _rms(zk[:, hd * HEAD_DIM:(hd + 1) * HEAD_DIM], kn_ref[...])
        kno_ref[:, hd * HEAD_DIM:(hd + 1) * HEAD_DIM] = t
        if rope:
            t = t * cs + _swap_halves(t) * sn
        kro_ref[:, hd * HEAD_DIM:(hd + 1) * HEAD_DIM] = t.astype(BF16)

    if cache_layer is None:
        return
    newk_ref, newv_ref, kbuf, vbuf, csem = cache_refs
    step = pl.program_id(0)
    slot = step % 2
    seqs_per_tile = x_ref.shape[0] // seq_len

    def cache_copies(k, s):
        cps = []
        for sq in range(seqs_per_tile):
            rows = slice(sq * seq_len, (sq + 1) * seq_len)
            for hd in range(N_KV_HEADS):
                cols = slice(hd * HEAD_DIM, (hd + 1) * HEAD_DIM)
                dst = (k * seqs_per_tile + sq, cache_layer, slice(None), hd, slice(None))
                cps.append(pltpu.make_async_copy(kbuf.at[s, rows, cols], newk_ref.at[dst], csem.at[s]))
                cps.append(pltpu.make_async_copy(vbuf.at[s, rows, cols], newv_ref.at[dst], csem.at[s]))
        return cps

    @pl.when(step >= 2)
    def _():
        for cp in cache_copies(step - 2, slot):
            cp.wait()

    kbuf[slot] = kno_ref[...]
    vbuf[slot] = v_ref[...]
    for cp in cache_copies(step, slot):
        cp.start()

    @pl.when(step == pl.num_programs(0) - 1)
    def _():
        for cp in cache_copies(step, slot):
            cp.wait()

        @pl.when(step >= 1)
        def _():
            for cp in cache_copies(step - 1, 1 - slot):
                cp.wait()


def _in_proj(x, mods, l, norm1, w_in_bf, wl, q_norm, k_norm, rope_tabs, seq_len, riders=(),
             caches=None):
    n = x.shape[0]
    rope = rope_tabs is not None
    tm = TM_ROPE if rope else TM
    tiles_per_seq = seq_len // tm
    if rope:
        mod_idx = lambda i: (l, 1 + i // tiles_per_seq, 0, 0)
    else:
        mod_idx = lambda i: (l, 0, 0, 0)
    row = lambda i: (i, 0)
    in_specs = [
        pl.BlockSpec((tm, D_MODEL), row),
        pl.BlockSpec((None, None, 6, D_MODEL), mod_idx),
        pl.BlockSpec((None, 1, D_MODEL), lambda i: (l, 0, 0)),
        pl.BlockSpec((None, D_MODEL, D_IN), lambda i: (wl, 0, 0), pipeline_mode=RESIDENT),
        pl.BlockSpec((None, 1, HEAD_DIM), lambda i: (l, 0, 0)),
        pl.BlockSpec((None, 1, HEAD_DIM), lambda i: (l, 0, 0)),
    ]
    args = [x, mods, norm1, w_in_bf, q_norm, k_norm]
    if rope:
        in_specs += [pl.BlockSpec((tm, HEAD_DIM), lambda i: (i % tiles_per_seq, 0))] * 2
        args += list(rope_tabs)
    widths = [(D_RNN, F32), (D_RNN, F32), (Q_DIM, BF16), (KV_DIM, F32), (KV_DIM, BF16),
              (KV_DIM, F32), (D_MODEL, F32), (D_MODEL, F32)]
    r_in, r_out, r_shape = _rider_specs(riders, n // tm, lambda i: i)
    in_specs = in_specs + r_in
    out_specs = [pl.BlockSpec((tm, w), row) for w, _ in widths] + r_out
    out_shape = [jax.ShapeDtypeStruct((n, w), dt) for w, dt in widths] + r_shape
    args = args + [r.w2d for r in riders]
    scratch, aliases = [], {}
    if caches is not None:
        assert tm % seq_len == 0
        for c in caches:
            aliases[len(args)] = len(out_shape)
            args.append(c)
            in_specs.append(pl.BlockSpec(memory_space=pl.ANY))
            out_specs.append(pl.BlockSpec(memory_space=pl.ANY))
            out_shape.append(jax.ShapeDtypeStruct(c.shape, c.dtype))
        scratch = [pltpu.VMEM((2, tm, KV_DIM), F32), pltpu.VMEM((2, tm, KV_DIM), F32),
                   pltpu.SemaphoreType.DMA((2,))]
    return pl.pallas_call(
        functools.partial(_in_proj_kernel, rope=rope, n_riders=len(riders),
                          cache_layer=None if caches is None else l, seq_len=seq_len),
        grid=(n // tm,),
        in_specs=in_specs,
        out_specs=out_specs,
        out_shape=out_shape,
        scratch_shapes=scratch,
        input_output_aliases=aliases,
        compiler_params=_cparams(("arbitrary",)),
        name="in_proj_lat" if rope else "in_proj_ctx",
    )(*args)


def _rnn_kernel(*refs, chained):
    refs = list(refs)
    xr_ref, yr_ref, cw_ref, cb_ref, gw_ref, ba_ref, bi_ref, lam_ref = refs[:8]
    refs = refs[8:]
    h0_ref = refs.pop(0) if chained else None
    out_ref, sf_ref, sb_ref, xext, ybuf, obuf, a_f, b_f, a_b, b_b, sem_in, sem_out = refs

    g = pl.program_id(0)
    cb = pl.program_id(1)
    n_cb = pl.num_programs(1)
    step = g * n_cb + cb
    n_steps = pl.num_programs(0) * n_cb
    slot = step % 2

    def hbm_window(kk, s):
        rows = pl.ds(pl.multiple_of(((kk // n_cb) * RNN_SEG + s) * RNN_T, RNN_T), RNN_T)
        cols = pl.ds(pl.multiple_of((kk % n_cb) * RNN_C, RNN_C), RNN_C)
        return rows, cols

    def in_copies(kk, sl_):
        cps = []
        for s in range(RNN_SEG):
            win = hbm_window(kk, s)
            cps.append(pltpu.make_async_copy(
                xr_ref.at[win], xext.at[sl_, pl.ds(CONV_LEFT, RNN_T), s], sem_in.at[sl_]))
            cps.append(pltpu.make_async_copy(yr_ref.at[win], ybuf.at[sl_, :, s], sem_in.at[sl_]))
        return cps

    def out_copies(kk, sl_):
        return [pltpu.make_async_copy(obuf.at[sl_, :, s], out_ref.at[hbm_window(kk, s)],
                                      sem_out.at[sl_]) for s in range(RNN_SEG)]

    @pl.when(step == 0)
    def _():
        for cp in in_copies(0, 0):
            cp.start()

    @pl.when(step + 1 < n_steps)
    def _():
        for cp in in_copies(step + 1, 1 - slot):
            cp.start()

    for cp in in_copies(step, slot):
        cp.wait()

    seg = lax.broadcasted_iota(jnp.int32, (RNN_SEG, RNN_C), 0)
    zero_row = jnp.zeros((RNN_SEG, RNN_C), F32)
    if chained:
        xext[slot, 0] = jnp.where(seg == 0, zero_row, pltpu.roll(xext[slot, RNN_T], 1, axis=0))
        xext[slot, 1] = jnp.where(seg == 0, zero_row, pltpu.roll(xext[slot, RNN_T + 1], 1, axis=0))
        xext[slot, RNN_T + CONV_LEFT] = jnp.where(
            seg == RNN_SEG - 1, zero_row, pltpu.roll(xext[slot, CONV_LEFT], RNN_SEG - 1, axis=0))
    else:
        xext[slot, 0] = zero_row
        xext[slot, 1] = zero_row
        xext[slot, RNN_T + CONV_LEFT] = zero_row

    scans = ((a_f, b_f), (a_b, b_b))
    rows2d = RNN_TCH * RNN_SEG

    def gate_chunk(tc, carry):
        t0 = pl.multiple_of(tc * RNN_TCH, RNN_TCH)
        for j in range(RNN_C // LANES):
            sl = slice(j * LANES, (j + 1) * LANES)
            xc = jnp.broadcast_to(cb_ref[:, sl].reshape(1, 1, LANES), (RNN_TCH, RNN_SEG, LANES))
            for k in range(CONV_W):
                xc = xc + (xext[slot, pl.ds(t0 + k, RNN_TCH), :, sl]
                           * cw_ref[k:k + 1, sl].reshape(1, 1, LANES))
            xc = xc.reshape(rows2d, LANES)
            pre = _dot(xc.astype(BF16), gw_ref[j])
            for d, (a_s, b_s) in enumerate(scans):
                lam = lam_ref[d, :, sl]
                log2_decay = (-LRU_C * LOG2_E) * (jnp.maximum(-lam, 0.0)
                                                  + jnp.log(1.0 + jnp.exp(-jnp.abs(lam))))
                r = 0.5 * jnp.tanh(pre[:, 2 * d * LANES:(2 * d + 1) * LANES]
                                   + ba_ref[d, :, sl]) + 0.5
                gi = 0.5 * jnp.tanh(pre[:, (2 * d + 1) * LANES:(2 * d + 2) * LANES]
                                    + bi_ref[d, :, sl]) + 0.5
                a = jnp.exp2(r * log2_decay)
                y = 1.0 - a * a
                b = (y * lax.rsqrt(jnp.maximum(y, RSQRT_FLOOR))) * (gi * xc)
                a_s[pl.ds(t0, RNN_TCH), :, sl] = a.reshape(RNN_TCH, RNN_SEG, LANES)
                b_s[pl.ds(t0, RNN_TCH), :, sl] = b.reshape(RNN_TCH, RNN_SEG, LANES)
        return carry

    lax.fori_loop(0, RNN_T // RNN_TCH, gate_chunk, 0)

    def scan_step(t, carry):
        hf, hb, pf, pb = carry
        tb = RNN_T - 1 - t
        af_t = a_f[t]
        hf = af_t * hf + b_f[t]
        b_f[t] = hf
        ab_t = a_b[tb]
        hb = ab_t * hb + b_b[tb]
        b_b[tb] = hb
        if chained:
            pf = af_t * pf
            a_f[t] = pf
            pb = ab_t * pb
            a_b[tb] = pb
        return hf, hb, pf, pb

    ones = jnp.ones((RNN_SEG, RNN_C), F32)
    hf, hb, pf, pb = lax.fori_loop(0, RNN_T, scan_step, (zero_row, zero_row, ones, ones),
                                   unroll=RNN_SCAN_UNROLL)

    if chained:
        st_f = jnp.broadcast_to(h0_ref[0], (RNN_SEG, RNN_C))
        st_b = jnp.broadcast_to(h0_ref[1], (RNN_SEG, RNN_C))
        in_f, in_b = st_f, st_b
        for _ in range(RNN_SEG - 1):
            in_f = jnp.where(seg == 0, st_f, pltpu.roll(pf * in_f + hf, 1, axis=0))
            in_b = jnp.where(seg == RNN_SEG - 1, st_b,
                             pltpu.roll(pb * in_b + hb, RNN_SEG - 1, axis=0))
        sf_ref[...] = pf * in_f + hf
        sb_ref[...] = pb * in_b + hb
    else:
        sf_ref[...] = hf
        sb_ref[...] = hb

    @pl.when(step >= 2)
    def _():
        for cp in out_copies(step - 2, slot):
            cp.wait()

    def finish_chunk(tc, carry):
        rows = pl.ds(pl.multiple_of(tc * RNN_TCH, RNN_TCH), RNN_TCH)
        h_f, h_b = b_f[rows], b_b[rows]
        if chained:
            h_f = a_f[rows] * in_f.reshape(1, RNN_SEG, RNN_C) + h_f
            h_b = a_b[rows] * in_b.reshape(1, RNN_SEG, RNN_C) + h_b
        obuf[slot, rows] = (h_f + h_b) * _gelu_tanh(ybuf[slot, rows])
        return carry

    lax.fori_loop(0, RNN_T // RNN_TCH, finish_chunk, 0)

    for cp in out_copies(step, slot):
        cp.start()

    @pl.when(step == n_steps - 1)
    def _():
        for cp in out_copies(step, slot):
            cp.wait()

        @pl.when(step >= 1)
        def _():
            for cp in out_copies(step - 1, 1 - slot):
                cp.wait()


def _rnn(xr, yr, l, conv_w, conv_b, gate_w, lru_ba, lru_bi, lru_lambda, state5, seq_len):
    n = xr.shape[0]
    chained = state5 is not None
    group_len = RNN_SEG * RNN_T
    assert n % group_len == 0 and seq_len in ((group_len,) if chained else (RNN_T,))
    n_groups = n // group_len
    lanes_per_step = RNN_C // LANES
    vec = pl.BlockSpec((None, 2, 1, RNN_C), lambda g, c: (l, 0, 0, c))
    in_specs = [
        pl.BlockSpec(memory_space=pl.ANY),
        pl.BlockSpec(memory_space=pl.ANY),
        pl.BlockSpec((None, CONV_W, RNN_C), lambda g, c: (l, 0, c)),
        pl.BlockSpec((None, 1, RNN_C), lambda g, c: (l, 0, c)),
        pl.BlockSpec((None, lanes_per_step, LANES, 4 * LANES), lambda g, c: (l, c, 0, 0)),
        vec, vec, vec,
    ]
    args = [xr, yr, conv_w, conv_b, gate_w, lru_ba, lru_bi, lru_lambda]
    if chained:
        in_specs.append(pl.BlockSpec((None, None, 2, 1, RNN_C), lambda g, c: (g, l, 0, 0, c)))
        args.append(state5)
    state_spec = pl.BlockSpec((RNN_SEG, RNN_C), lambda g, c: (g, c))
    state_shape = jax.ShapeDtypeStruct((n_groups * RNN_SEG, D_RNN), F32)
    work = lambda t: pltpu.VMEM((t, RNN_SEG, RNN_C), F32)
    return pl.pallas_call(
        functools.partial(_rnn_kernel, chained=chained),
        grid=(n_groups, D_RNN // RNN_C),
        in_specs=in_specs,
        out_specs=[pl.BlockSpec(memory_space=pl.ANY), state_spec, state_spec],
        out_shape=[jax.ShapeDtypeStruct((n, D_RNN), F32), state_shape, state_shape],
        scratch_shapes=[
            pltpu.VMEM((2, RNN_T + CONV_W - 1, RNN_SEG, RNN_C), F32),
            pltpu.VMEM((2, RNN_T, RNN_SEG, RNN_C), F32),
            pltpu.VMEM((2, RNN_T, RNN_SEG, RNN_C), F32),
            work(RNN_T), work(RNN_T), work(RNN_T), work(RNN_T),
            pltpu.SemaphoreType.DMA((2,)), pltpu.SemaphoreType.DMA((2,)),
        ],
        compiler_params=_cparams(("arbitrary", "arbitrary")),
        name="rnn_lat" if chained else "rnn_ctx",
    )(*args)


def _attend_heads(q_ref, k_all, v_all, o_ref, stack_heads):
    exp2_scale = (1.0 / (HEAD_DIM ** 0.5)) * LOG2_E
    group = N_HEADS // N_KV_HEADS
    tq = q_ref.shape[0]
    for kh in range(N_KV_HEADS):
        k = k_all[:, kh * HEAD_DIM:(kh + 1) * HEAD_DIM]
        v = v_all[:, kh * HEAD_DIM:(kh + 1) * HEAD_DIM]
        heads = [slice((kh * group + g) * HEAD_DIM, (kh * group + g + 1) * HEAD_DIM)
                 for g in range(group)]
        stacks = [heads] if stack_heads else [[sl] for sl in heads]
        for stack in stacks:
            q = jnp.concatenate([q_ref[:, sl] for sl in stack], axis=0)
            s = lax.dot_general(q, k, (((1,), (1,)), ((), ())), preferred_element_type=F32)
            m = jnp.max(s, axis=-1, keepdims=True)
            p = jnp.exp2((s - m) * exp2_scale)
            denom = jnp.sum(p, axis=-1, keepdims=True)
            o = (_dot(p.astype(BF16), v) / denom).astype(BF16)
            for g, sl in enumerate(stack):
                o_ref[:, sl] = o[g * tq:(g + 1) * tq]


def _attn_ctx_kernel(q_ref, k_ref, v_ref, o_ref):
    _attend_heads(q_ref, k_ref[...], v_ref[...].astype(BF16), o_ref, stack_heads=True)


def _attn_ctx(q, kr, v, seq_len):
    n = q.shape[0]
    row = lambda b: (b, 0)
    return pl.pallas_call(
        _attn_ctx_kernel,
        grid=(n // seq_len,),
        in_specs=[pl.BlockSpec((seq_len, Q_DIM), row),
                  pl.BlockSpec((seq_len, KV_DIM), row),
                  pl.BlockSpec((seq_len, KV_DIM), row)],
        out_specs=pl.BlockSpec((seq_len, Q_DIM), row),
        out_shape=jax.ShapeDtypeStruct((n, Q_DIM), BF16),
        compiler_params=_cparams(("arbitrary",)),
        name="attn_ctx",
    )(q, kr, v)


def _attn_lat_kernel(*refs, past, n_riders):
    _convert_riders(refs[5:5 + n_riders], refs[6 + n_riders:6 + 2 * n_riders])
    q_ref, kc_ref, vc_ref, kl_ref, vl_ref = refs[:5]
    o_ref = refs[5 + n_riders]
    k_all, v_all = refs[6 + 2 * n_riders:]

    @pl.when(pl.program_id(1) == 0)
    def _():
        k_all[0:past, :] = kc_ref[...].astype(BF16)
        k_all[past:, :] = kl_ref[...]
        v_all[0:past, :] = vc_ref[...].astype(BF16)
        v_all[past:, :] = vl_ref[...].astype(BF16)

    _attend_heads(q_ref, k_all[...], v_all[...], o_ref, stack_heads=False)


def _attn_lat(q, kr, v, cache_k4, cache_v4, l, seq_len, riders=()):
    n = q.shape[0]
    past = cache_k4.shape[2]
    nq = seq_len // TQ
    step_of = lambda b, j: b * nq + j
    r_in, r_out, r_shape = _rider_specs(riders, (n // seq_len) * nq, step_of)
    return pl.pallas_call(
        functools.partial(_attn_lat_kernel, past=past, n_riders=len(riders)),
        grid=(n // seq_len, nq),
        in_specs=[pl.BlockSpec((TQ, Q_DIM), lambda b, j: (b * nq + j, 0)),
                  pl.BlockSpec((None, None, past, KV_DIM), lambda b, j: (b, l, 0, 0)),
                  pl.BlockSpec((None, None, past, KV_DIM), lambda b, j: (b, l, 0, 0)),
                  pl.BlockSpec((seq_len, KV_DIM), lambda b, j: (b, 0)),
                  pl.BlockSpec((seq_len, KV_DIM), lambda b, j: (b, 0))] + r_in,
        out_specs=[pl.BlockSpec((TQ, Q_DIM), lambda b, j: (b * nq + j, 0))] + r_out,
        out_shape=[jax.ShapeDtypeStruct((n, Q_DIM), BF16)] + r_shape,
        scratch_shapes=[pltpu.VMEM((past + seq_len, KV_DIM), BF16),
                        pltpu.VMEM((past + seq_len, KV_DIM), BF16)],
        compiler_params=_cparams(("arbitrary", "arbitrary")),
        name="attn_lat",
    )(q, cache_k4, cache_v4, kr, v, *[r.w2d for r in riders])


R_E1, R_E2, R_W1, R_W2, R_RANK1, R_RANK2 = range(6)


def _route(h2, wrt_ref, brt_ref, count_ref):
    shape = (h2.shape[0], LANES)
    lane = lax.broadcasted_iota(jnp.int32, shape, 1)
    neg = jnp.float32(-jnp.inf)
    logits = jnp.full(shape, neg, F32)
    for e in range(N_EXPERTS):
        prod = h2 * wrt_ref[e:e + 1, :]
        part = prod[:, 0:LANES]
        for c in range(1, D_MODEL // LANES):
            part = part + prod[:, c * LANES:(c + 1) * LANES]
        logits = jnp.where(lane == e, jnp.sum(part, axis=-1, keepdims=True), logits)
    logits = logits + brt_ref[...]
    v1 = jnp.max(logits, axis=-1, keepdims=True)
    i1 = jnp.min(jnp.where(logits == v1, lane, LANES), axis=-1, keepdims=True)
    rest = jnp.where(lane == i1, neg, logits)
    v2 = jnp.max(rest, axis=-1, keepdims=True)
    i2 = jnp.min(jnp.where(rest == v2, lane, LANES), axis=-1, keepdims=True)
    e2 = jnp.exp(v2 - v1)
    w1 = 1.0 / (1.0 + e2)
    w2 = e2 / (1.0 + e2)
    chosen = jnp.where((lane == i1) | (lane == i2), 1.0, 0.0)
    r_id = lax.broadcasted_iota(jnp.int32, (shape[0], shape[0]), 0)
    c_id = lax.broadcasted_iota(jnp.int32, (shape[0], shape[0]), 1)
    before = jnp.where(c_id < r_id, 1.0, 0.0).astype(BF16)
    rank = _dot(before, chosen.astype(BF16)) + count_ref[...]
    rank1 = jnp.sum(jnp.where(lane == i1, rank, 0.0), axis=-1, keepdims=True)
    rank2 = jnp.sum(jnp.where(lane == i2, rank, 0.0), axis=-1, keepdims=True)
    count_ref[...] += jnp.sum(chosen, axis=0, keepdims=True)
    rec = jnp.zeros(shape, F32)
    for k, val in ((R_E1, i1.astype(F32)), (R_E2, i2.astype(F32)), (R_W1, w1), (R_W2, w2),
                   (R_RANK1, rank1), (R_RANK2, rank2)):
        rec = jnp.where(lane == k, val, rec)
    return rec


def _mix_out_kernel(*refs, route):
    (x_ref, r_ref, a_ref, gr_ref, ga_ref, mod_ref, n2_ref, wr_ref, wa_ref, wo_ref) = refs[:10]
    if route:
        wrt_ref, brt_ref, x1_ref, h2_ref, rec_ref, cnt_ref, count_ref, hbuf, hsem = refs[10:]
    else:
        x1_ref, h2_ref = refs[10:]
    merged = (_sigmoid(gr_ref[...]) * _dot(r_ref[...].astype(BF16), wr_ref[...])
              + _sigmoid(ga_ref[...]) * _dot(a_ref[...], wa_ref[...]))
    mix = _dot(merged.astype(BF16), wo_ref[...])
    x1 = x_ref[...] + mod_ref[2:3, :] * mix
    x1_ref[...] = x1
    h2 = _rms(x1, n2_ref[...]) * (1.0 + mod_ref[4:5, :]) + mod_ref[3:4, :]
    if not route:
        h2_ref[...] = h2.astype(BF16)
        return

    @pl.when(pl.program_id(0) == 0)
    def _():
        count_ref[...] = jnp.zeros_like(count_ref)

    rec_ref[...] = _route(h2, wrt_ref, brt_ref, count_ref)
    cnt_ref[...] = count_ref[...]

    step = pl.program_id(0)
    slot = step % 2

    def h_copies(k, s):
        rows = pl.ds(k * TM, TM)
        return [pltpu.make_async_copy(hbuf.at[s, :, c * LANES:(c + 1) * LANES], h2_ref.at[rows, c],
                                      hsem.at[s]) for c in range(D_MODEL // LANES)]

    @pl.when(step >= 2)
    def _():
        for cp in h_copies(step - 2, slot):
            cp.wait()

    hbuf[slot] = h2
    for cp in h_copies(step, slot):
        cp.start()

    @pl.when(step == pl.num_programs(0) - 1)
    def _():
        for cp in h_copies(step, slot):
            cp.wait()

        @pl.when(step >= 1)
        def _():
            for cp in h_copies(step - 1, 1 - slot):
                cp.wait()


def _mix_out(x, out_r, out_a, gr, ga, mods, l, norm2, wr_bf, wa_bf, wo_bf, wl, seq_len, lat,
             router=None):
    n = x.shape[0]
    tiles_per_seq = seq_len // TM
    if lat:
        mod_idx = lambda i: (l, 1 + i // tiles_per_seq, 0, 0)
    else:
        mod_idx = lambda i: (l, 0, 0, 0)
    row = lambda i: (i, 0)
    wspec = pl.BlockSpec((None, D_MODEL, D_MODEL), lambda i: (wl, 0, 0), pipeline_mode=RESIDENT)
    in_specs = [pl.BlockSpec((TM, D_MODEL), row)] * 5 + [
        pl.BlockSpec((None, None, 6, D_MODEL), mod_idx),
        pl.BlockSpec((None, 1, D_MODEL), lambda i: (l, 0, 0)),
        wspec, wspec, wspec]
    args = [x, out_r, out_a, gr, ga, mods, norm2, wr_bf, wa_bf, wo_bf]
    out_specs = [pl.BlockSpec((TM, D_MODEL), row)]
    out_shape = [jax.ShapeDtypeStruct((n, D_MODEL), F32)]
    scratch = []
    if router is None:
        out_specs.append(pl.BlockSpec((TM, D_MODEL), row))
        out_shape.append(jax.ShapeDtypeStruct((n, D_MODEL), BF16))
    else:
        j, wr_t, br_pad = router
        in_specs += [pl.BlockSpec((None, N_EXPERTS, D_MODEL), lambda i: (j, 0, 0)),
                     pl.BlockSpec((None, 1, LANES), lambda i: (j, 0, 0))]
        args += [wr_t, br_pad]
        out_specs += [pl.BlockSpec(memory_space=pl.ANY),
                      pl.BlockSpec((TM, LANES), row),
                      pl.BlockSpec((1, LANES), lambda i: (0, 0))]
        out_shape += [jax.ShapeDtypeStruct((n, D_MODEL // LANES, LANES), F32),
                      jax.ShapeDtypeStruct((n, LANES), F32),
                      jax.ShapeDtypeStruct((1, LANES), F32)]
        scratch = [pltpu.VMEM((1, LANES), F32), pltpu.VMEM((2, TM, D_MODEL), F32),
                   pltpu.SemaphoreType.DMA((2,))]
    return pl.pallas_call(
        functools.partial(_mix_out_kernel, route=router is not None),
        grid=(n // TM,),
        in_specs=in_specs,
        out_specs=out_specs,
        out_shape=out_shape,
        scratch_shapes=scratch,
        compiler_params=_cparams(("arbitrary",)),
        name=("mix_route" if router is not None else "mix_out") + ("_lat" if lat else "_ctx"),
    )(*args)


def _ffn_kernel(x_ref, h_ref, mod_ref, wg_ref, wu_ref, wd_ref, o_ref, *, f_chunk):
    h = h_ref[...]
    d_ff = wg_ref.shape[1]
    acc = jnp.zeros((h.shape[0], D_MODEL), F32)
    for c in range(d_ff // f_chunk):
        sl = slice(c * f_chunk, (c + 1) * f_chunk)
        act = _silu(_dot(h, wg_ref[:, sl])) * _dot(h, wu_ref[:, sl])
        acc = acc + _dot(act.astype(BF16), wd_ref[sl, :])
    o_ref[...] = x_ref[...] + mod_ref[5:6, :] * acc


def _ffn(x1, h2, mods, l, j, wg_bf, wu_bf, wd_bf, seq_len, lat):
    n = x1.shape[0]
    d_ff = wg_bf.shape[2]
    tiles_per_seq = seq_len // TM
    if lat:
        mod_idx = lambda i: (l, 1 + i // tiles_per_seq, 0, 0)
    else:
        mod_idx = lambda i: (l, 0, 0, 0)
    row = lambda i: (i, 0)
    return pl.pallas_call(
        functools.partial(_ffn_kernel, f_chunk=d_ff // 2),
        grid=(n // TM,),
        in_specs=[pl.BlockSpec((TM, D_MODEL), row), pl.BlockSpec((TM, D_MODEL), row),
                  pl.BlockSpec((None, None, 6, D_MODEL), mod_idx),
                  pl.BlockSpec((None, D_MODEL, d_ff), lambda i: (j, 0, 0), pipeline_mode=RESIDENT),
                  pl.BlockSpec((None, D_MODEL, d_ff), lambda i: (j, 0, 0), pipeline_mode=RESIDENT),
                  pl.BlockSpec((None, d_ff, D_MODEL), lambda i: (j, 0, 0), pipeline_mode=RESIDENT)],
        out_specs=pl.BlockSpec((TM, D_MODEL), row),
        out_shape=jax.ShapeDtypeStruct((n, D_MODEL), F32),
        compiler_params=_cparams(("arbitrary",)),
        name="ffn_lat" if lat else "ffn_ctx",
    )(x1, h2, mods, wg_bf, wu_bf, wd_bf)


def _group_layout(rec, counts, n_tiles):
    as_int = lambda k: rec[:, k].astype(jnp.int32)
    cnt = counts[0, :N_EXPERTS].astype(jnp.int32)
    tiles = (cnt + TM_GROUP - 1) // TM_GROUP
    cum = jnp.cumsum(tiles)
    start = (cum - tiles) * TM_GROUP
    experts = jnp.arange(N_EXPERTS, dtype=jnp.int32)

    def pos(e, rank):
        return jnp.sum(jnp.where(e[:, None] == experts[None, :], start[None, :], 0), axis=1) + rank

    pos2 = jnp.stack([pos(as_int(R_E1), as_int(R_RANK1)), pos(as_int(R_E2), as_int(R_RANK2))])
    n_active = cum[N_EXPERTS - 1]
    tile_expert = jnp.sum(jnp.arange(n_tiles, dtype=jnp.int32)[:, None] >= cum[None, :], axis=1)
    tile_expert = jnp.minimum(tile_expert, tile_expert[n_active - 1]).astype(jnp.int32)
    tile_ids = jnp.arange(n_tiles, dtype=jnp.int32)
    group_last = jnp.any((tile_ids[:, None] == cum[None, :] - 1) & (tiles[None, :] > 0), axis=1)
    zero_tile = (group_last | (tile_ids >= n_active)).astype(jnp.int32)
    return (pos2.astype(jnp.int32), tile_expert, n_active.reshape(1).astype(jnp.int32), zero_tile)


def _row_copy(src, dst, sem):
    return pltpu.make_async_copy(src, dst, sem)


def _dispatch_kernel(pos_ref, zt_ref, h_ref, xs_ref, zbuf, sem, zsem):
    n_chunks = h_ref.shape[0] // DMA_CHUNK
    n_tiles = xs_ref.shape[0] // TM_GROUP

    zbuf[...] = jnp.zeros_like(zbuf)

    def zero_fill(j):
        return pltpu.make_async_copy(zbuf, xs_ref.at[pl.ds(j * TM_GROUP, TM_GROUP)], zsem)

    def zero_start(j, carry):
        @pl.when(zt_ref[j] != 0)
        def _():
            zero_fill(j).start()
        return carry

    def zero_wait(j, carry):
        @pl.when(zt_ref[j] != 0)
        def _():
            zero_fill(j).wait()
        return carry

    lax.fori_loop(0, n_tiles, zero_start, 0)
    lax.fori_loop(0, n_tiles, zero_wait, 0)

    def drain_chunk():
        rows = pl.ds(0, 2 * DMA_CHUNK)
        _row_copy(xs_ref.at[rows], xs_ref.at[rows], sem).wait()

    def chunk(k, carry):
        for r in range(DMA_CHUNK):
            t = k * DMA_CHUNK + r
            _row_copy(h_ref.at[t], xs_ref.at[pos_ref[0, t]], sem).start(priority=0)
            _row_copy(h_ref.at[t], xs_ref.at[pos_ref[1, t]], sem).start(priority=1)

        @pl.when(k > 0)
        def _():
            drain_chunk()

        return carry

    lax.fori_loop(0, n_chunks, chunk, 0)
    drain_chunk()


def _dispatch(pos2, zero_tile, h2_rows, n_rows):
    tile = h2_rows.shape[1:]
    return pl.pallas_call(
        _dispatch_kernel,
        grid_spec=pltpu.PrefetchScalarGridSpec(
            num_scalar_prefetch=2, grid=(1,),
            in_specs=[pl.BlockSpec(h2_rows.shape, lambda i, p, z: (0, 0, 0))],
            out_specs=pl.BlockSpec(memory_space=pl.ANY),
            scratch_shapes=[pltpu.VMEM((TM_GROUP,) + tile, F32),
                            pltpu.SemaphoreType.DMA(()), pltpu.SemaphoreType.DMA(())]),
        out_shape=jax.ShapeDtypeStruct((n_rows,) + tile, F32),
        compiler_params=_cparams(("arbitrary",)),
        name="moe_dispatch",
    )(pos2, zero_tile, h2_rows)


def _expert_kernel(te_ref, na_ref, xs_ref, wg_ref, wu_ref, wd_ref, ys_ref,
                   xbuf, ybuf, sem_in, sem_out):
    del te_ref
    i = pl.program_id(0)
    last = pl.num_programs(0) - 1
    n_active = na_ref[0]
    slot = i % 2
    n_col = D_MODEL // LANES

    def in_copies(tile, s):
        rows = pl.ds(tile * TM_GROUP, TM_GROUP)
        return [pltpu.make_async_copy(xs_ref.at[rows, c], xbuf.at[s, :, c * LANES:(c + 1) * LANES],
                                      sem_in.at[s]) for c in range(n_col)]

    def out_copies(tile, s):
        rows = pl.ds(tile * TM_GROUP, TM_GROUP)
        return [pltpu.make_async_copy(ybuf.at[s, :, c * LANES:(c + 1) * LANES], ys_ref.at[rows, c],
                                      sem_out.at[s]) for c in range(n_col)]

    @pl.when(i == 0)
    def _():
        for cp in in_copies(0, 0):
            cp.start()

    @pl.when(i + 1 < n_active)
    def _():
        for cp in in_copies(i + 1, 1 - slot):
            cp.start()

    @pl.when(i >= 2)
    def _():
        for cp in out_copies(i - 2, slot):
            cp.wait()

    @pl.when(i < n_active)
    def _():
        for cp in in_copies(i, slot):
            cp.wait()
        xb = xbuf[slot].astype(BF16)
        act = _silu(_dot(xb, wg_ref[...])) * _dot(xb, wu_ref[...])
        ybuf[slot] = _dot(act.astype(BF16), wd_ref[...])

    @pl.when(i >= n_active)
    def _():
        ybuf[slot] = jnp.zeros((TM_GROUP, D_MODEL), F32)

    for cp in out_copies(i, slot):
        cp.start()

    @pl.when(i == last)
    def _():
        for cp in out_copies(i, slot):
            cp.wait()

        @pl.when(i >= 1)
        def _():
            for cp in out_copies(i - 1, 1 - slot):
                cp.wait()


def _experts(tile_expert, n_active, xs, j, wg_bf, wu_bf, wd_bf):
    n_rows = xs.shape[0]
    d_e = wg_bf.shape[3]
    wspec = lambda shape: pl.BlockSpec((None, None) + shape, lambda i, te, na: (j, te[i], 0, 0))
    return pl.pallas_call(
        _expert_kernel,
        grid_spec=pltpu.PrefetchScalarGridSpec(
            num_scalar_prefetch=2, grid=(n_rows // TM_GROUP,),
            in_specs=[pl.BlockSpec(memory_space=pl.ANY),
                      wspec((D_MODEL, d_e)), wspec((D_MODEL, d_e)), wspec((d_e, D_MODEL))],
            out_specs=pl.BlockSpec(memory_space=pl.ANY),
            scratch_shapes=[pltpu.VMEM((2, TM_GROUP, D_MODEL), F32),
                            pltpu.VMEM((2, TM_GROUP, D_MODEL), F32),
                            pltpu.SemaphoreType.DMA((2,)), pltpu.SemaphoreType.DMA((2,))]),
        out_shape=jax.ShapeDtypeStruct(xs.shape, F32),
        compiler_params=_cparams(("arbitrary",)),
        name="moe_experts",
    )(tile_expert, n_active, xs, wg_bf, wu_bf, wd_bf)


def _combine_kernel(*refs, final):
    pos_ref, y_ref, x_ref, rec_ref, mod_ref = refs[:5]
    g_ref = refs[5] if final else None
    o_ref, a1, a2, b1, b2, sem_a, sem_b = refs[5 + int(final):]
    i = pl.program_id(0)

    def issue(half_idx, d1, d2, sem):
        base = half_idx * DMA_CHUNK

        for r in range(DMA_CHUNK):
            _row_copy(y_ref.at[pos_ref[0, base + r]], d1.at[r], sem).start(priority=0)
            _row_copy(y_ref.at[pos_ref[1, base + r]], d2.at[r], sem).start(priority=1)

    def drain(d1, d2, sem):
        rows = pl.ds(0, DMA_CHUNK)
        _row_copy(y_ref.at[rows], d1, sem).wait()
        _row_copy(y_ref.at[rows], d2, sem).wait()

    def finish(half, d1, d2):
        rows = slice(half * DMA_CHUNK, (half + 1) * DMA_CHUNK)
        w1 = rec_ref[rows, R_W1:R_W1 + 1]
        w2 = rec_ref[rows, R_W2:R_W2 + 1]
        for c in range(D_MODEL // LANES):
            sl = slice(c * LANES, (c + 1) * LANES)
            y = w1 * d1[:, c, :] + w2 * d2[:, c, :]
            o_ref[rows, sl] = x_ref[rows, sl] + mod_ref[5:6, sl] * y
        if final:
            o_ref[rows, :] = _rms(o_ref[rows, :], g_ref[...])

    @pl.when(i == 0)
    def _():
        issue(0, a1, a2, sem_a)

    issue(2 * i + 1, b1, b2, sem_b)
    drain(a1, a2, sem_a)
    finish(0, a1, a2)

    @pl.when(i + 1 < pl.num_programs(0))
    def _():
        issue(2 * i + 2, a1, a2, sem_a)

    drain(b1, b2, sem_b)
    finish(1, b1, b2)


def _combine(pos2, ys, x1, rec, mods, l, seq_len, lat, final_g):
    n = x1.shape[0]
    final = final_g is not None
    tile = ys.shape[1:]
    tm = 2 * DMA_CHUNK
    tiles_per_seq = max(seq_len // tm, 1)
    if lat:
        mod_idx = lambda i, p: (l, 1 + i // tiles_per_seq, 0, 0)
    else:
        mod_idx = lambda i, p: (l, 0, 0, 0)
    row = lambda i, p: (i, 0)
    slot = pltpu.VMEM((DMA_CHUNK,) + tile, F32)
    in_specs = [pl.BlockSpec(memory_space=pl.ANY),
                pl.BlockSpec((tm, D_MODEL), row),
                pl.BlockSpec((tm, LANES), row),
                pl.BlockSpec((None, None, 6, D_MODEL), mod_idx)]
    args = [pos2, ys, x1, rec, mods]
    if final:
        in_specs.append(pl.BlockSpec((1, D_MODEL), lambda i, p: (0, 0)))
        args.append(final_g)
    return pl.pallas_call(
        functools.partial(_combine_kernel, final=final),
        grid_spec=pltpu.PrefetchScalarGridSpec(
            num_scalar_prefetch=1, grid=(n // tm,),
            in_specs=in_specs,
            out_specs=pl.BlockSpec((tm, D_MODEL), row),
            scratch_shapes=[slot, slot, slot, slot,
                            pltpu.SemaphoreType.DMA(()), pltpu.SemaphoreType.DMA(())]),
        out_shape=jax.ShapeDtypeStruct((n, D_MODEL), F32),
        compiler_params=_cparams(("arbitrary",)),
        name="moe_combine_final" if final else "moe_combine",
    )(*args)


def _moe(x1, h2_rows, rec, counts, mods, l, j, wg_bf, wu_bf, wd_bf, seq_len, lat, final_g):
    n = x1.shape[0]
    n_tiles = 2 * n // TM_GROUP + N_EXPERTS
    pos2, tile_expert, n_active, zero_tile = _group_layout(rec, counts, n_tiles)
    xs = _dispatch(pos2, zero_tile, h2_rows, n_tiles * TM_GROUP)
    ys = _experts(tile_expert, n_active, xs, j, wg_bf, wu_bf, wd_bf)
    return _combine(pos2, ys, x1, rec, mods, l, seq_len, lat, final_g)


def _final_norm_kernel(x_ref, g_ref, o_ref):
    o_ref[...] = _rms(x_ref[...], g_ref[...])


def _final_norm(x, g):
    n = x.shape[0]
    row = lambda i: (i, 0)
    return pl.pallas_call(
        _final_norm_kernel,
        grid=(n // TM_FF,),
        in_specs=[pl.BlockSpec((TM_FF, D_MODEL), row), pl.BlockSpec((1, D_MODEL), lambda i: (0, 0))],
        out_specs=pl.BlockSpec((TM_FF, D_MODEL), row),
        out_shape=jax.ShapeDtypeStruct((n, D_MODEL), F32),
        compiler_params=_cparams(("arbitrary",)),
        name="final_norm",
    )(x, g)


def _rope_tables(t_len):
    t = jnp.arange(t_len)
    pos = jnp.stack([t // GRID_W, t % GRID_W], axis=-1).astype(F32)
    inv = ROPE_THETA ** (-jnp.arange(0, AXIS_DIM, 2, dtype=F32) / AXIS_DIM)
    ang = pos[..., None] * inv
    cos, sin = jnp.cos(ang), jnp.sin(ang)
    c = jnp.concatenate([cos[:, 0], cos[:, 0], cos[:, 1], cos[:, 1]], axis=-1)
    s = jnp.concatenate([-sin[:, 0], sin[:, 0], -sin[:, 1], sin[:, 1]], axis=-1)
    return c, s


def _gate_weights(wa, wi):
    depth = wa.shape[0]
    groups = D_RNN // LANES

    def pair(w):
        w = w.reshape(depth, 2, groups, 2, RNN_BLOCK, RNN_BLOCK)
        z = jnp.zeros_like(w[:, :, :, 0])
        top = jnp.concatenate([w[:, :, :, 0], z], axis=-1)
        bot = jnp.concatenate([z, w[:, :, :, 1]], axis=-1)
        return jnp.concatenate([top, bot], axis=-2)

    w = jnp.concatenate([pair(wa), pair(wi)], axis=-1)
    return (0.5 * jnp.concatenate([w[:, 0], w[:, 1]], axis=-1)).astype(BF16)


def kernel(x_prompt, x_sample, cache_k, cache_v, state_rnn, c, c_ctx, w_mod, b_mod, norm1, norm2, w_in, conv_w, conv_b, lru_wa, lru_ba, lru_wi, lru_bi, lru_lambda, q_norm, k_norm, w_rnn_out, w_attn_out, w_out, ff_gate, ff_up, ff_down, router_w, router_b, exp_gate, exp_up, exp_down, final_norm):
    batch, seq, _ = x_prompt.shape
    dec_batch, dec_seq, _ = x_sample.shape
    depth = w_mod.shape[0]
    past = cache_k.shape[2]
    assert dec_batch + 1 <= MOD_ROWS

    cond8 = jnp.zeros((MOD_ROWS, D_MODEL), F32).at[0].set(c_ctx).at[1:1 + dec_batch].set(c)
    mods = _modulation(cond8, w_mod, b_mod).reshape(depth, MOD_ROWS, 6, D_MODEL)

    gate_w = _gate_weights(lru_wa, lru_wi)
    n_moe = router_w.shape[0]
    wr_t = jnp.swapaxes(router_w, 1, 2)
    br_pad = jnp.zeros((n_moe, 1, LANES), F32).at[:, 0, :N_EXPERTS].set(router_b)

    norm1_3 = norm1.reshape(depth, 1, D_MODEL)
    norm2_3 = norm2.reshape(depth, 1, D_MODEL)
    qn3 = q_norm.reshape(depth, 1, HEAD_DIM)
    kn3 = k_norm.reshape(depth, 1, HEAD_DIM)
    conv_b3 = conv_b.reshape(depth, 1, D_RNN)
    ba4 = (0.5 * lru_ba).reshape(depth, 2, 1, D_RNN)
    bi4 = (0.5 * lru_bi).reshape(depth, 2, 1, D_RNN)
    lam4 = lru_lambda.reshape(depth, 2, 1, D_RNN)
    state5 = state_rnn.reshape(dec_batch, depth, 2, 1, D_RNN)
    cache_k4 = cache_k.reshape(dec_batch, depth, past, KV_DIM)
    cache_v4 = cache_v.reshape(dec_batch, depth, past, KV_DIM)
    rope_tabs = _rope_tables(dec_seq)
    final_g = final_norm.reshape(1, D_MODEL)

    n_steps = dec_batch * dec_seq // TQ
    assert n_steps == dec_batch * dec_seq // TM_ROPE

    def flat(w):
        return w.reshape(-1, w.shape[-1])

    def channel_names(l):
        return ("exp_gate", "exp_up", "exp_down") if l % 2 == 1 else ("ff_gate", "ff_up", "ff_down")

    stacked = dict(w_in=w_in, w_rnn_out=w_rnn_out, w_attn_out=w_attn_out, w_out=w_out,
                   ff_gate=ff_gate, ff_up=ff_up, ff_down=ff_down,
                   exp_gate=exp_gate, exp_up=exp_up, exp_down=exp_down)

    def layer_shape(name):
        return (1,) + stacked[name].shape[1:]

    def layer_index(name, l):
        return l if name.startswith("w_") else l // 2

    def rider(name, l, steps):
        w = stacked[name]
        rows = flat(w).shape[0] // w.shape[0] // steps
        return _Rider(flat(w), rows, layer_index(name, l) * steps)

    def names_of(l):
        return ("w_in", "w_rnn_out", "w_attn_out", "w_out") + channel_names(l)

    weights = {l: {} for l in range(depth)}

    def keep(l, names, arrays):
        for name, arr in zip(names, arrays):
            weights[l][name] = arr.reshape(layer_shape(name))

    keep(0, ("w_in",), _convert([rider("w_in", 0, n_steps)], n_steps))
    ctx_steps = batch * seq // TM

    def run_layer(x, l, lat):
        seq_len = dec_seq if lat else seq
        wts = weights[l]
        convert_next = lat and l + 1 < depth
        if convert_next:
            proj_names = names_of(l + 1)[:4] + names_of(l + 1)[6:]
            attn_names = names_of(l + 1)[4:6]
            proj_riders = [rider(name, l + 1, n_steps) for name in proj_names]
            attn_riders = [rider(name, l + 1, n_steps) for name in attn_names]
            target = l + 1
        elif l == 0 and not lat:
            proj_names, attn_names, attn_riders, target = names_of(0)[1:], (), [], 0
            proj_riders = [rider(name, 0, ctx_steps) for name in proj_names]
        else:
            proj_names, attn_names, proj_riders, attn_riders, target = (), (), [], [], None
        outs = _in_proj(x, mods, l, norm1_3, wts["w_in"], 0, qn3, kn3,
                        rope_tabs if lat else None, seq_len, riders=proj_riders,
                        caches=None if lat else new_kv)
        xr, yr, q, _, kr, v, gr, ga = outs[:8]
        if target is not None:
            keep(target, proj_names, outs[8:8 + len(proj_names)])
        if not lat:
            new_kv[:] = outs[8 + len(proj_names):]
        out_r, end_f, end_b = _rnn(xr, yr, l, conv_w, conv_b3, gate_w, ba4, bi4, lam4,
                                   state5 if lat else None, seq_len)
        if lat:
            outs = _attn_lat(q, kr, v, cache_k4, cache_v4, l, seq_len, riders=attn_riders)
            out_a = outs[0]
            if target is not None:
                keep(target, attn_names, outs[1:])
        else:
            out_a = _attn_ctx(q, kr, v, seq_len)
        mix_args = (x, out_r, out_a, gr, ga, mods, l, norm2_3, wts["w_rnn_out"],
                    wts["w_attn_out"], wts["w_out"], 0, seq_len, lat)
        if l % 2 == 1:
            x1, h2_rows, rec, counts = _mix_out(*mix_args, router=(l // 2, wr_t, br_pad))
            x = _moe(x1, h2_rows, rec, counts, mods, l, 0, wts["exp_gate"], wts["exp_up"],
                     wts["exp_down"], seq_len, lat, final_g if l == depth - 1 else None)
        else:
            x1, h2 = _mix_out(*mix_args)
            x = _ffn(x1, h2, mods, l, 0, wts["ff_gate"], wts["ff_up"], wts["ff_down"],
                     seq_len, lat)
            if l == depth - 1:
                x = _final_norm(x, final_g)
        return x, jnp.stack([end_f, end_b], axis=1)

    new_kv = [jnp.zeros((batch, depth, seq, N_KV_HEADS, HEAD_DIM), F32) for _ in range(2)]
    y_ctx = x_prompt.reshape(batch * seq, D_MODEL)
    y_lat = x_sample.reshape(dec_batch * dec_seq, D_MODEL)
    ss = []
    for l in range(depth):
        y_ctx, st = run_layer(y_ctx, l, False)
        ss.append(st)
        y_lat, _ = run_layer(y_lat, l, True)

    new_state = jnp.stack(ss, axis=1)
    return (y_ctx.reshape(batch, seq, D_MODEL), y_lat.reshape(dec_batch, dec_seq, D_MODEL),
            new_kv[0], new_kv[1], new_state)
```

```python
import functools

import jax
import jax.numpy as jnp
from jax import lax
from jax.experimental import pallas as pl
from jax.experimental.pallas import tpu as pltpu

F32 = jnp.float32
BF16 = jnp.bfloat16

D_MODEL = 1024
D_RNN = 1024
N_RNN_BLOCKS = 16
RNN_BLOCK = D_RNN // N_RNN_BLOCKS
CONV_W = 4
CONV_LEFT = 2
LRU_C = 8.0
N_HEADS = 8
N_KV_HEADS = 2
HEAD_DIM = 128
AXIS_DIM = HEAD_DIM // 2
ROPE_THETA = 10000.0
GRID_W = 64
Q_DIM = N_HEADS * HEAD_DIM
KV_DIM = N_KV_HEADS * HEAD_DIM
N_EXPERTS = 8
EPS = 1e-6
LOG2_E = 1.4426950408889634
RSQRT_FLOOR = 1e-30
D_IN = 2 * D_RNN + Q_DIM + 2 * KV_DIM + 2 * D_MODEL
O_XR, O_YR, O_Q = 0, D_RNN, 2 * D_RNN
O_K = O_Q + Q_DIM
O_V = O_K + KV_DIM
O_GR = O_V + KV_DIM
O_GA = O_GR + D_MODEL

LANES = 128
SUBLANES = 8
MOD_ROWS = 8
VMEM_LIMIT = 56 * 1024 * 1024

TM = 512
TM_ROPE = 256
RESIDENT = pl.Buffered(1)
RNN_T = 256
RNN_SEG = SUBLANES
RNN_C = 512
RNN_TCH = 32
RNN_SCAN_UNROLL = 8
TQ = 256
TM_FF = 512
TM_GROUP = 256
DMA_CHUNK = 256


def _cparams(sem):
    return pltpu.CompilerParams(dimension_semantics=sem, vmem_limit_bytes=VMEM_LIMIT)


def _sigmoid(x):
    return 1.0 / (1.0 + jnp.exp(-x))


def _silu(x):
    return x * _sigmoid(x)


def _gelu_tanh(x):
    return 0.5 * x * (1.0 + jnp.tanh(0.7978845608028654 * (x + 0.044715 * (x * x * x))))


def _rms(x, g):
    ms = jnp.mean(x * x, axis=-1, keepdims=True)
    return x * lax.rsqrt(ms + EPS) * g


def _dot(a, b):
    return jnp.dot(a, b, preferred_element_type=F32)


def _mod_kernel(cond_ref, w_ref, b_ref, o_ref):
    s = _silu(cond_ref[...]).astype(BF16)
    o_ref[...] = _dot(s, w_ref[...].astype(BF16)) + b_ref[...]


def _modulation(cond8, w_mod, b_mod):
    depth = w_mod.shape[0]
    tn = 1536
    return pl.pallas_call(
        _mod_kernel,
        grid=(depth, 6 * D_MODEL // tn),
        in_specs=[
            pl.BlockSpec((MOD_ROWS, D_MODEL), lambda l, j: (0, 0)),
            pl.BlockSpec((None, D_MODEL, tn), lambda l, j: (l, 0, j)),
            pl.BlockSpec((None, 1, tn), lambda l, j: (l, 0, j)),
        ],
        out_specs=pl.BlockSpec((None, MOD_ROWS, tn), lambda l, j: (l, 0, j)),
        out_shape=jax.ShapeDtypeStruct((depth, MOD_ROWS, 6 * D_MODEL), F32),
        compiler_params=_cparams(("arbitrary", "arbitrary")),
        name="adaln_mod",
    )(cond8, w_mod, b_mod.reshape(depth, 1, 6 * D_MODEL))


def _swap_halves(x):
    lane = lax.broadcasted_iota(jnp.int32, x.shape, 1)
    return jnp.where((lane % AXIS_DIM) < AXIS_DIM // 2,
                     pltpu.roll(x, LANES - AXIS_DIM // 2, axis=1),
                     pltpu.roll(x, AXIS_DIM // 2, axis=1))


class _Rider:
    def __init__(self, w2d, rows, first):
        self.w2d, self.rows, self.first = w2d, rows, first


def _rider_specs(riders, n_steps, step_of):
    in_specs, out_specs, out_shape = [], [], []
    for r in riders:
        cols = r.w2d.shape[1]
        assert r.rows % 16 == 0 and (r.first + n_steps) * r.rows <= r.w2d.shape[0]
        in_specs.append(pl.BlockSpec((r.rows, cols), lambda *g, r=r: (r.first + step_of(*g), 0)))
        out_specs.append(pl.BlockSpec((r.rows, cols), lambda *g: (step_of(*g), 0)))
        out_shape.append(jax.ShapeDtypeStruct((r.rows * n_steps, cols), BF16))
    return in_specs, out_specs, out_shape


def _convert_riders(in_refs, out_refs):
    for w_ref, o_ref in zip(in_refs, out_refs):
        o_ref[...] = w_ref[...].astype(BF16)


def _convert_kernel(*refs):
    _convert_riders(refs[:len(refs) // 2], refs[len(refs) // 2:])


def _convert(riders, n_steps):
    r_in, r_out, r_shape = _rider_specs(riders, n_steps, lambda i: i)
    return pl.pallas_call(
        _convert_kernel,
        grid=(n_steps,),
        in_specs=r_in,
        out_specs=r_out,
        out_shape=r_shape,
        compiler_params=_cparams(("arbitrary",)),
        name="convert_weights",
    )(*[r.w2d for r in riders])


def _in_proj_kernel(*refs, rope, n_riders, cache_layer, seq_len):
    n_in = 8 if rope else 6
    n_cache = 0 if cache_layer is None else 2
    first_out = n_in + n_riders + n_cache
    _convert_riders(refs[n_in:n_in + n_riders], refs[first_out + 8:first_out + 8 + n_riders])
    cache_refs = refs[first_out + 8 + n_riders:]
    refs = refs[:n_in] + refs[first_out:first_out + 8]
    if rope:
        (x_ref, mod_ref, n1_ref, w_ref, qn_ref, kn_ref, c_ref, s_ref,
         xr_ref, yr_ref, q_ref, kno_ref, kro_ref, v_ref, gr_ref, ga_ref) = refs
    else:
        (x_ref, mod_ref, n1_ref, w_ref, qn_ref, kn_ref,
         xr_ref, yr_ref, q_ref, kno_ref, kro_ref, v_ref, gr_ref, ga_ref) = refs
    x = x_ref[...]
    h = _rms(x, n1_ref[...]) * (1.0 + mod_ref[1:2, :]) + mod_ref[0:1, :]
    hb = h.astype(BF16)

    xr_ref[...] = _dot(hb, w_ref[:, O_XR:O_XR + D_RNN])
    yr_ref[...] = _dot(hb, w_ref[:, O_YR:O_YR + D_RNN])
    gr_ref[...] = _dot(hb, w_ref[:, O_GR:O_GR + D_MODEL])
    ga_ref[...] = _dot(hb, w_ref[:, O_GA:O_GA + D_MODEL])
    v_ref[...] = _dot(hb, w_ref[:, O_V:O_V + KV_DIM])

    if rope:
        cs, sn = c_ref[...], s_ref[...]
    zq = _dot(hb, w_ref[:, O_Q:O_Q + Q_DIM])
    for hd in range(N_HEADS):
        t = _rms(zq[:, hd * HEAD_DIM:(hd + 1) * HEAD_DIM], qn_ref[...])
        if rope:
            t = t * cs + _swap_halves(t) * sn
        q_ref[:, hd * HEAD_DIM:(hd + 1) * HEAD_DIM] = t.astype(BF16)
    zk = _dot(hb, w_ref[:, O_K:O_K + KV_DIM])
    for hd in range(N_KV_HEADS):
        t = _rms(zk[:, hd * HEAD_DIM:(hd + 1) * HEAD_DIM], kn_ref[...])
        kno_ref[:, hd * HEAD_DIM:(hd + 1) * HEAD_DIM] = t
        if rope:
            t = t * cs + _swap_halves(t) * sn
        kro_ref[:, hd * HEAD_DIM:(hd + 1) * HEAD_DIM] = t.astype(BF16)

    if cache_layer is None:
        return
    newk_ref, newv_ref, kbuf, vbuf, csem = cache_refs
    step = pl.program_id(0)
    slot = step % 2
    seqs_per_tile = x_ref.shape[0] // seq_len

    def cache_copies(k, s):
        cps = []
        for sq in range(seqs_per_tile):
            rows = slice(sq * seq_len, (sq + 1) * seq_len)
            for hd in range(N_KV_HEADS):
                cols = slice(hd * HEAD_DIM, (hd + 1) * HEAD_DIM)
                dst = (k * seqs_per_tile + sq, cache_layer, slice(None), hd, slice(None))
                cps.append(pltpu.make_async_copy(kbuf.at[s, rows, cols], newk_ref.at[dst], csem.at[s]))
                cps.append(pltpu.make_async_copy(vbuf.at[s, rows, cols], newv_ref.at[dst], csem.at[s]))
        return cps

    @pl.when(step >= 2)
    def _():
        for cp in cache_copies(step - 2, slot):
            cp.wait()

    kbuf[slot] = kno_ref[...]
    vbuf[slot] = v_ref[...]
    for cp in cache_copies(step, slot):
        cp.start()

    @pl.when(step == pl.num_programs(0) - 1)
    def _():
        for cp in cache_copies(step, slot):
            cp.wait()

        @pl.when(step >= 1)
        def _():
            for cp in cache_copies(step - 1, 1 - slot):
                cp.wait()


def _in_proj(x, mods, l, norm1, w_in_bf, wl, q_norm, k_norm, rope_tabs, seq_len, riders=(),
             caches=None):
    n = x.shape[0]
    rope = rope_tabs is not None
    tm = TM_ROPE if rope else TM
    tiles_per_seq = seq_len // tm
    if rope:
        mod_idx = lambda i: (l, 1 + i // tiles_per_seq, 0, 0)
    else:
        mod_idx = lambda i: (l, 0, 0, 0)
    row = lambda i: (i, 0)
    in_specs = [
        pl.BlockSpec((tm, D_MODEL), row),
        pl.BlockSpec((None, None, 6, D_MODEL), mod_idx),
        pl.BlockSpec((None, 1, D_MODEL), lambda i: (l, 0, 0)),
        pl.BlockSpec((None, D_MODEL, D_IN), lambda i: (wl, 0, 0), pipeline_mode=RESIDENT),
        pl.BlockSpec((None, 1, HEAD_DIM), lambda i: (l, 0, 0)),
        pl.BlockSpec((None, 1, HEAD_DIM), lambda i: (l, 0, 0)),
    ]
    args = [x, mods, norm1, w_in_bf, q_norm, k_norm]
    if rope:
        in_specs += [pl.BlockSpec((tm, HEAD_DIM), lambda i: (i % tiles_per_seq, 0))] * 2
        args += list(rope_tabs)
    widths = [(D_RNN, F32), (D_RNN, F32), (Q_DIM, BF16), (KV_DIM, F32), (KV_DIM, BF16),
              (KV_DIM, F32), (D_MODEL, F32), (D_MODEL, F32)]
    r_in, r_out, r_shape = _rider_specs(riders, n // tm, lambda i: i)
    in_specs = in_specs + r_in
    out_specs = [pl.BlockSpec((tm, w), row) for w, _ in widths] + r_out
    out_shape = [jax.ShapeDtypeStruct((n, w), dt) for w, dt in widths] + r_shape
    args = args + [r.w2d for r in riders]
    scratch, aliases = [], {}
    if caches is not None:
        assert tm % seq_len == 0
        for c in caches:
            aliases[len(args)] = len(out_shape)
            args.append(c)
            in_specs.append(pl.BlockSpec(memory_space=pl.ANY))
            out_specs.append(pl.BlockSpec(memory_space=pl.ANY))
            out_shape.append(jax.ShapeDtypeStruct(c.shape, c.dtype))
        scratch = [pltpu.VMEM((2, tm, KV_DIM), F32), pltpu.VMEM((2, tm, KV_DIM), F32),
                   pltpu.SemaphoreType.DMA((2,))]
    return pl.pallas_call(
        functools.partial(_in_proj_kernel, rope=rope, n_riders=len(riders),
                          cache_layer=None if caches is None else l, seq_len=seq_len),
        grid=(n // tm,),
        in_specs=in_specs,
        out_specs=out_specs,
        out_shape=out_shape,
        scratch_shapes=scratch,
        input_output_aliases=aliases,
        compiler_params=_cparams(("arbitrary",)),
        name="in_proj_lat" if rope else "in_proj_ctx",
    )(*args)


def _rnn_kernel(*refs, chained):
    refs = list(refs)
    xr_ref, yr_ref, cw_ref, cb_ref, gw_ref, ba_ref, bi_ref, lam_ref = refs[:8]
    refs = refs[8:]
    h0_ref = refs.pop(0) if chained else None
    out_ref, sf_ref, sb_ref, xext, ybuf, obuf, a_f, b_f, a_b, b_b, sem_in, sem_out = refs

    g = pl.program_id(0)
    cb = pl.program_id(1)
    n_cb = pl.num_programs(1)
    step = g * n_cb + cb
    n_steps = pl.num_programs(0) * n_cb
    slot = step % 2

    def hbm_window(kk, s):
        rows = pl.ds(pl.multiple_of(((kk // n_cb) * RNN_SEG + s) * RNN_T, RNN_T), RNN_T)
        cols = pl.ds(pl.multiple_of((kk % n_cb) * RNN_C, RNN_C), RNN_C)
        return rows, cols

    def in_copies(kk, sl_):
        cps = []
        for s in range(RNN_SEG):
            win = hbm_window(kk, s)
            cps.append(pltpu.make_async_copy(
                xr_ref.at[win], xext.at[sl_, pl.ds(CONV_LEFT, RNN_T), s], sem_in.at[sl_]))
            cps.append(pltpu.make_async_copy(yr_ref.at[win], ybuf.at[sl_, :, s], sem_in.at[sl_]))
        return cps

    def out_copies(kk, sl_):
        return [pltpu.make_async_copy(obuf.at[sl_, :, s], out_ref.at[hbm_window(kk, s)],
                                      sem_out.at[sl_]) for s in range(RNN_SEG)]

    @pl.when(step == 0)
    def _():
        for cp in in_copies(0, 0):
            cp.start()

    @pl.when(step + 1 < n_steps)
    def _():
        for cp in in_copies(step + 1, 1 - slot):
            cp.start()

    for cp in in_copies(step, slot):
        cp.wait()

    seg = lax.broadcasted_iota(jnp.int32, (RNN_SEG, RNN_C), 0)
    zero_row = jnp.zeros((RNN_SEG, RNN_C), F32)
    if chained:
        xext[slot, 0] = jnp.where(seg == 0, zero_row, pltpu.roll(xext[slot, RNN_T], 1, axis=0))
        xext[slot, 1] = jnp.where(seg == 0, zero_row, pltpu.roll(xext[slot, RNN_T + 1], 1, axis=0))
        xext[slot, RNN_T + CONV_LEFT] = jnp.where(
            seg == RNN_SEG - 1, zero_row, pltpu.roll(xext[slot, CONV_LEFT], RNN_SEG - 1, axis=0))
    else:
        xext[slot, 0] = zero_row
        xext[slot, 1] = zero_row
        xext[slot, RNN_T + CONV_LEFT] = zero_row

    scans = ((a_f, b_f), (a_b, b_b))
    rows2d = RNN_TCH * RNN_SEG

    def gate_chunk(tc, carry):
        t0 = pl.multiple_of(tc * RNN_TCH, RNN_TCH)
        for j in range(RNN_C // LANES):
            sl = slice(j * LANES, (j + 1) * LANES)
            xc = jnp.broadcast_to(cb_ref[:, sl].reshape(1, 1, LANES), (RNN_TCH, RNN_SEG, LANES))
            for k in range(CONV_W):
                xc = xc + (xext[slot, pl.ds(t0 + k, RNN_TCH), :, sl]
                           * cw_ref[k:k + 1, sl].reshape(1, 1, LANES))
            xc = xc.reshape(rows2d, LANES)
            pre = _dot(xc.astype(BF16), gw_ref[j])
            for d, (a_s, b_s) in enumerate(scans):
                lam = lam_ref[d, :, sl]
                log2_decay = (-LRU_C * LOG2_E) * (jnp.maximum(-lam, 0.0)
                                                  + jnp.log(1.0 + jnp.exp(-jnp.abs(lam))))
                r = 0.5 * jnp.tanh(pre[:, 2 * d * LANES:(2 * d + 1) * LANES]
                                   + ba_ref[d, :, sl]) + 0.5
                gi = 0.5 * jnp.tanh(pre[:, (2 * d + 1) * LANES:(2 * d + 2) * LANES]
                                    + bi_ref[d, :, sl]) + 0.5
                a = jnp.exp2(r * log2_decay)
                y = 1.0 - a * a
                b = (y * lax.rsqrt(jnp.maximum(y, RSQRT_FLOOR))) * (gi * xc)
                a_s[pl.ds(t0, RNN_TCH), :, sl] = a.reshape(RNN_TCH, RNN_SEG, LANES)
                b_s[pl.ds(t0, RNN_TCH), :, sl] = b.reshape(RNN_TCH, RNN_SEG, LANES)
        return carry

    lax.fori_loop(0, RNN_T // RNN_TCH, gate_chunk, 0)

    def scan_step(t, carry):
        hf, hb, pf, pb = carry
        tb = RNN_T - 1 - t
        af_t = a_f[t]
        hf = af_t * hf + b_f[t]
        b_f[t] = hf
        ab_t = a_b[tb]
        hb = ab_t * hb + b_b[tb]
        b_b[tb] = hb
        if chained:
            pf = af_t * pf
            a_f[t] = pf
            pb = ab_t * pb
            a_b[tb] = pb
        return hf, hb, pf, pb

    ones = jnp.ones((RNN_SEG, RNN_C), F32)
    hf, hb, pf, pb = lax.fori_loop(0, RNN_T, scan_step, (zero_row, zero_row, ones, ones),
                                   unroll=RNN_SCAN_UNROLL)

    if chained:
        st_f = jnp.broadcast_to(h0_ref[0], (RNN_SEG, RNN_C))
        st_b = jnp.broadcast_to(h0_ref[1], (RNN_SEG, RNN_C))
        in_f, in_b = st_f, st_b
        for _ in range(RNN_SEG - 1):
            in_f = jnp.where(seg == 0, st_f, pltpu.roll(pf * in_f + hf, 1, axis=0))
            in_b = jnp.where(seg == RNN_SEG - 1, st_b,
                             pltpu.roll(pb * in_b + hb, RNN_SEG - 1, axis=0))
        sf_ref[...] = pf * in_f + hf
        sb_ref[...] = pb * in_b + hb
    else:
        sf_ref[...] = hf
        sb_ref[...] = hb

    @pl.when(step >= 2)
    def _():
        for cp in out_copies(step - 2, slot):
            cp.wait()

    def finish_chunk(tc, carry):
        rows = pl.ds(pl.multiple_of(tc * RNN_TCH, RNN_TCH), RNN_TCH)
        h_f, h_b = b_f[rows], b_b[rows]
        if chained:
            h_f = a_f[rows] * in_f.reshape(1, RNN_SEG, RNN_C) + h_f
            h_b = a_b[rows] * in_b.reshape(1, RNN_SEG, RNN_C) + h_b
        obuf[slot, rows] = (h_f + h_b) * _gelu_tanh(ybuf[slot, rows])
        return carry

    lax.fori_loop(0, RNN_T // RNN_TCH, finish_chunk, 0)

    for cp in out_copies(step, slot):
        cp.start()

    @pl.when(step == n_steps - 1)
    def _():
        for cp in out_copies(step, slot):
            cp.wait()

        @pl.when(step >= 1)
        def _():
            for cp in out_copies(step - 1, 1 - slot):
                cp.wait()


def _rnn(xr, yr, l, conv_w, conv_b, gate_w, lru_ba, lru_bi, lru_lambda, state5, seq_len):
    n = xr.shape[0]
    chained = state5 is not None
    group_len = RNN_SEG * RNN_T
    assert n % group_len == 0 and seq_len in ((group_len,) if chained else (RNN_T,))
    n_groups = n // group_len
    lanes_per_step = RNN_C // LANES
    vec = pl.BlockSpec((None, 2, 1, RNN_C), lambda g, c: (l, 0, 0, c))
    in_specs = [
        pl.BlockSpec(memory_space=pl.ANY),
        pl.BlockSpec(memory_space=pl.ANY),
        pl.BlockSpec((None, CONV_W, RNN_C), lambda g, c: (l, 0, c)),
        pl.BlockSpec((None, 1, RNN_C), lambda g, c: (l, 0, c)),
        pl.BlockSpec((None, lanes_per_step, LANES, 4 * LANES), lambda g, c: (l, c, 0, 0)),
        vec, vec, vec,
    ]
    args = [xr, yr, conv_w, conv_b, gate_w, lru_ba, lru_bi, lru_lambda]
    if chained:
        in_specs.append(pl.BlockSpec((None, None, 2, 1, RNN_C), lambda g, c: (g, l, 0, 0, c)))
        args.append(state5)
    state_spec = pl.BlockSpec((RNN_SEG, RNN_C), lambda g, c: (g, c))
    state_shape = jax.ShapeDtypeStruct((n_groups * RNN_SEG, D_RNN), F32)
    work = lambda t: pltpu.VMEM((t, RNN_SEG, RNN_C), F32)
    return pl.pallas_call(
        functools.partial(_rnn_kernel, chained=chained),
        grid=(n_groups, D_RNN // RNN_C),
        in_specs=in_specs,
        out_specs=[pl.BlockSpec(memory_space=pl.ANY), state_spec, state_spec],
        out_shape=[jax.ShapeDtypeStruct((n, D_RNN), F32), state_shape, state_shape],
        scratch_shapes=[
            pltpu.VMEM((2, RNN_T + CONV_W - 1, RNN_SEG, RNN_C), F32),
            pltpu.VMEM((2, RNN_T, RNN_SEG, RNN_C), F32),
            pltpu.VMEM((2, RNN_T, RNN_SEG, RNN_C), F32),
            work(RNN_T), work(RNN_T), work(RNN_T), work(RNN_T),
            pltpu.SemaphoreType.DMA((2,)), pltpu.SemaphoreType.DMA((2,)),
        ],
        compiler_params=_cparams(("arbitrary", "arbitrary")),
        name="rnn_lat" if chained else "rnn_ctx",
    )(*args)


def _attend_heads(q_ref, k_all, v_all, o_ref, stack_heads):
    exp2_scale = (1.0 / (HEAD_DIM ** 0.5)) * LOG2_E
    group = N_HEADS // N_KV_HEADS
    tq = q_ref.shape[0]
    for kh in range(N_KV_HEADS):
        k = k_all[:, kh * HEAD_DIM:(kh + 1) * HEAD_DIM]
        v = v_all[:, kh * HEAD_DIM:(kh + 1) * HEAD_DIM]
        heads = [slice((kh * group + g) * HEAD_DIM, (kh * group + g + 1) * HEAD_DIM)
                 for g in range(group)]
        stacks = [heads] if stack_heads else [[sl] for sl in heads]
        for stack in stacks:
            q = jnp.concatenate([q_ref[:, sl] for sl in stack], axis=0)
            s = lax.dot_general(q, k, (((1,), (1,)), ((), ())), preferred_element_type=F32)
            m = jnp.max(s, axis=-1, keepdims=True)
            p = jnp.exp2((s - m) * exp2_scale)
            denom = jnp.sum(p, axis=-1, keepdims=True)
            o = (_dot(p.astype(BF16), v) / denom).astype(BF16)
            for g, sl in enumerate(stack):
                o_ref[:, sl] = o[g * tq:(g + 1) * tq]


def _attn_ctx_kernel(q_ref, k_ref, v_ref, o_ref):
    _attend_heads(q_ref, k_ref[...], v_ref[...].astype(BF16), o_ref, stack_heads=True)


def _attn_ctx(q, kr, v, seq_len):
    n = q.shape[0]
    row = lambda b: (b, 0)
    return pl.pallas_call(
        _attn_ctx_kernel,
        grid=(n // seq_len,),
        in_specs=[pl.BlockSpec((seq_len, Q_DIM), row),
                  pl.BlockSpec((seq_len, KV_DIM), row),
                  pl.BlockSpec((seq_len, KV_DIM), row)],
        out_specs=pl.BlockSpec((seq_len, Q_DIM), row),
        out_shape=jax.ShapeDtypeStruct((n, Q_DIM), BF16),
        compiler_params=_cparams(("arbitrary",)),
        name="attn_ctx",
    )(q, kr, v)


def _attn_lat_kernel(*refs, past, n_riders):
    _convert_riders(refs[5:5 + n_riders], refs[6 + n_riders:6 + 2 * n_riders])
    q_ref, kc_ref, vc_ref, kl_ref, vl_ref = refs[:5]
    o_ref = refs[5 + n_riders]
    k_all, v_all = refs[6 + 2 * n_riders:]

    @pl.when(pl.program_id(1) == 0)
    def _():
        k_all[0:past, :] = kc_ref[...].astype(BF16)
        k_all[past:, :] = kl_ref[...]
        v_all[0:past, :] = vc_ref[...].astype(BF16)
        v_all[past:, :] = vl_ref[...].astype(BF16)

    _attend_heads(q_ref, k_all[...], v_all[...], o_ref, stack_heads=False)


def _attn_lat(q, kr, v, cache_k4, cache_v4, l, seq_len, riders=()):
    n = q.shape[0]
    past = cache_k4.shape[2]
    nq = seq_len // TQ
    step_of = lambda b, j: b * nq + j
    r_in, r_out, r_shape = _rider_specs(riders, (n // seq_len) * nq, step_of)
    return pl.pallas_call(
        functools.partial(_attn_lat_kernel, past=past, n_riders=len(riders)),
        grid=(n // seq_len, nq),
        in_specs=[pl.BlockSpec((TQ, Q_DIM), lambda b, j: (b * nq + j, 0)),
                  pl.BlockSpec((None, None, past, KV_DIM), lambda b, j: (b, l, 0, 0)),
                  pl.BlockSpec((None, None, past, KV_DIM), lambda b, j: (b, l, 0, 0)),
                  pl.BlockSpec((seq_len, KV_DIM), lambda b, j: (b, 0)),
                  pl.BlockSpec((seq_len, KV_DIM), lambda b, j: (b, 0))] + r_in,
        out_specs=[pl.BlockSpec((TQ, Q_DIM), lambda b, j: (b * nq + j, 0))] + r_out,
        out_shape=[jax.ShapeDtypeStruct((n, Q_DIM), BF16)] + r_shape,
        scratch_shapes=[pltpu.VMEM((past + seq_len, KV_DIM), BF16),
                        pltpu.VMEM((past + seq_len, KV_DIM), BF16)],
        compiler_params=_cparams(("arbitrary", "arbitrary")),
        name="attn_lat",
    )(q, cache_k4, cache_v4, kr, v, *[r.w2d for r in riders])


R_E1, R_E2, R_W1, R_W2, R_RANK1, R_RANK2 = range(6)


def _route(h2, wrt_ref, brt_ref, count_ref):
    shape = (h2.shape[0], LANES)
    lane = lax.broadcasted_iota(jnp.int32, shape, 1)
    neg = jnp.float32(-jnp.inf)
    h_hi = h2.astype(BF16)
    h_lo = (h2 - h_hi.astype(F32)).astype(BF16)
    z_hi = _dot(h_hi, wrt_ref[...])
    z_lo = _dot(h_lo, wrt_ref[...])
    logits = ((z_hi[:, :LANES] + z_hi[:, LANES:]) + (z_lo[:, :LANES] + z_lo[:, LANES:])
              + brt_ref[...])
    logits = jnp.where(lane < N_EXPERTS, logits, neg)
    v1 = jnp.max(logits, axis=-1, keepdims=True)
    i1 = jnp.min(jnp.where(logits == v1, lane, LANES), axis=-1, keepdims=True)
    rest = jnp.where(lane == i1, neg, logits)
    v2 = jnp.max(rest, axis=-1, keepdims=True)
    i2 = jnp.min(jnp.where(rest == v2, lane, LANES), axis=-1, keepdims=True)
    e2 = jnp.exp(v2 - v1)
    w1 = 1.0 / (1.0 + e2)
    w2 = e2 / (1.0 + e2)
    chosen = jnp.where((lane == i1) | (lane == i2), 1.0, 0.0)
    r_id = lax.broadcasted_iota(jnp.int32, (shape[0], shape[0]), 0)
    c_id = lax.broadcasted_iota(jnp.int32, (shape[0], shape[0]), 1)
    before = jnp.where(c_id < r_id, 1.0, 0.0).astype(BF16)
    rank = _dot(before, chosen.astype(BF16)) + count_ref[...]
    rank1 = jnp.sum(jnp.where(lane == i1, rank, 0.0), axis=-1, keepdims=True)
    rank2 = jnp.sum(jnp.where(lane == i2, rank, 0.0), axis=-1, keepdims=True)
    count_ref[...] += jnp.sum(chosen, axis=0, keepdims=True)
    rec = jnp.zeros(shape, F32)
    for k, val in ((R_E1, i1.astype(F32)), (R_E2, i2.astype(F32)), (R_W1, w1), (R_W2, w2),
                   (R_RANK1, rank1), (R_RANK2, rank2)):
        rec = jnp.where(lane == k, val, rec)
    return rec


def _mix_out_kernel(*refs, route):
    (x_ref, r_ref, a_ref, gr_ref, ga_ref, mod_ref, n2_ref, wr_ref, wa_ref, wo_ref) = refs[:10]
    if route:
        wrt_ref, brt_ref, x1_ref, h2_ref, rec_ref, cnt_ref, count_ref, hbuf, hsem = refs[10:]
    else:
        x1_ref, h2_ref = refs[10:]
    merged = (_sigmoid(gr_ref[...]) * _dot(r_ref[...].astype(BF16), wr_ref[...])
              + _sigmoid(ga_ref[...]) * _dot(a_ref[...], wa_ref[...]))
    mix = _dot(merged.astype(BF16), wo_ref[...])
    x1 = x_ref[...] + mod_ref[2:3, :] * mix
    x1_ref[...] = x1
    h2 = _rms(x1, n2_ref[...]) * (1.0 + mod_ref[4:5, :]) + mod_ref[3:4, :]
    if not route:
        h2_ref[...] = h2.astype(BF16)
        return

    @pl.when(pl.program_id(0) == 0)
    def _():
        count_ref[...] = jnp.zeros_like(count_ref)

    rec_ref[...] = _route(h2, wrt_ref, brt_ref, count_ref)
    cnt_ref[...] = count_ref[...]

    step = pl.program_id(0)
    slot = step % 2

    def h_copies(k, s):
        rows = pl.ds(k * TM, TM)
        return [pltpu.make_async_copy(hbuf.at[s, :, c * LANES:(c + 1) * LANES], h2_ref.at[rows, c],
                                      hsem.at[s]) for c in range(D_MODEL // LANES)]

    @pl.when(step >= 2)
    def _():
        for cp in h_copies(step - 2, slot):
            cp.wait()

    hbuf[slot] = h2
    for cp in h_copies(step, slot):
        cp.start()

    @pl.when(step == pl.num_programs(0) - 1)
    def _():
        for cp in h_copies(step, slot):
            cp.wait()

        @pl.when(step >= 1)
        def _():
            for cp in h_copies(step - 1, 1 - slot):
                cp.wait()


def _mix_out(x, out_r, out_a, gr, ga, mods, l, norm2, wr_bf, wa_bf, wo_bf, wl, seq_len, lat,
             router=None):
    n = x.shape[0]
    tiles_per_seq = seq_len // TM
    if lat:
        mod_idx = lambda i: (l, 1 + i // tiles_per_seq, 0, 0)
    else:
        mod_idx = lambda i: (l, 0, 0, 0)
    row = lambda i: (i, 0)
    wspec = pl.BlockSpec((None, D_MODEL, D_MODEL), lambda i: (wl, 0, 0), pipeline_mode=RESIDENT)
    in_specs = [pl.BlockSpec((TM, D_MODEL), row)] * 5 + [
        pl.BlockSpec((None, None, 6, D_MODEL), mod_idx),
        pl.BlockSpec((None, 1, D_MODEL), lambda i: (l, 0, 0)),
        wspec, wspec, wspec]
    args = [x, out_r, out_a, gr, ga, mods, norm2, wr_bf, wa_bf, wo_bf]
    out_specs = [pl.BlockSpec((TM, D_MODEL), row)]
    out_shape = [jax.ShapeDtypeStruct((n, D_MODEL), F32)]
    scratch = []
    if router is None:
        out_specs.append(pl.BlockSpec((TM, D_MODEL), row))
        out_shape.append(jax.ShapeDtypeStruct((n, D_MODEL), BF16))
    else:
        j, wr_t, br_pad = router
        in_specs += [pl.BlockSpec((None, D_MODEL, 2 * LANES), lambda i: (j, 0, 0)),
                     pl.BlockSpec((None, 1, LANES), lambda i: (j, 0, 0))]
        args += [wr_t, br_pad]
        out_specs += [pl.BlockSpec(memory_space=pl.ANY),
                      pl.BlockSpec((TM, LANES), row),
                      pl.BlockSpec((1, LANES), lambda i: (0, 0))]
        out_shape += [jax.ShapeDtypeStruct((n, D_MODEL // LANES, LANES), F32),
                      jax.ShapeDtypeStruct((n, LANES), F32),
                      jax.ShapeDtypeStruct((1, LANES), F32)]
        scratch = [pltpu.VMEM((1, LANES), F32), pltpu.VMEM((2, TM, D_MODEL), F32),
                   pltpu.SemaphoreType.DMA((2,))]
    return pl.pallas_call(
        functools.partial(_mix_out_kernel, route=router is not None),
        grid=(n // TM,),
        in_specs=in_specs,
        out_specs=out_specs,
        out_shape=out_shape,
        scratch_shapes=scratch,
        compiler_params=_cparams(("arbitrary",)),
        name=("mix_route" if router is not None else "mix_out") + ("_lat" if lat else "_ctx"),
    )(*args)


def _ffn_kernel(x_ref, h_ref, mod_ref, wg_ref, wu_ref, wd_ref, o_ref, *, f_chunk):
    h = h_ref[...]
    d_ff = wg_ref.shape[1]
    acc = jnp.zeros((h.shape[0], D_MODEL), F32)
    for c in range(d_ff // f_chunk):
        sl = slice(c * f_chunk, (c + 1) * f_chunk)
        act = _silu(_dot(h, wg_ref[:, sl])) * _dot(h, wu_ref[:, sl])
        acc = acc + _dot(act.astype(BF16), wd_ref[sl, :])
    o_ref[...] = x_ref[...] + mod_ref[5:6, :] * acc


def _ffn(x1, h2, mods, l, j, wg_bf, wu_bf, wd_bf, seq_len, lat):
    n = x1.shape[0]
    d_ff = wg_bf.shape[2]
    tiles_per_seq = seq_len // TM
    if lat:
        mod_idx = lambda i: (l, 1 + i // tiles_per_seq, 0, 0)
    else:
        mod_idx = lambda i: (l, 0, 0, 0)
    row = lambda i: (i, 0)
    return pl.pallas_call(
        functools.partial(_ffn_kernel, f_chunk=d_ff // 2),
        grid=(n // TM,),
        in_specs=[pl.BlockSpec((TM, D_MODEL), row), pl.BlockSpec((TM, D_MODEL), row),
                  pl.BlockSpec((None, None, 6, D_MODEL), mod_idx),
                  pl.BlockSpec((None, D_MODEL, d_ff), lambda i: (j, 0, 0), pipeline_mode=RESIDENT),
                  pl.BlockSpec((None, D_MODEL, d_ff), lambda i: (j, 0, 0), pipeline_mode=RESIDENT),
                  pl.BlockSpec((None, d_ff, D_MODEL), lambda i: (j, 0, 0), pipeline_mode=RESIDENT)],
        out_specs=pl.BlockSpec((TM, D_MODEL), row),
        out_shape=jax.ShapeDtypeStruct((n, D_MODEL), F32),
        compiler_params=_cparams(("arbitrary",)),
        name="ffn_lat" if lat else "ffn_ctx",
    )(x1, h2, mods, wg_bf, wu_bf, wd_bf)


def _group_layout(rec, counts, n_tiles):
    as_int = lambda k: rec[:, k].astype(jnp.int32)
    cnt = counts[0, :N_EXPERTS].astype(jnp.int32)
    tiles = (cnt + TM_GROUP - 1) // TM_GROUP
    cum = jnp.cumsum(tiles)
    start = (cum - tiles) * TM_GROUP
    experts = jnp.arange(N_EXPERTS, dtype=jnp.int32)

    def pos(e, rank):
        return jnp.sum(jnp.where(e[:, None] == experts[None, :], start[None, :], 0), axis=1) + rank

    pos2 = jnp.stack([pos(as_int(R_E1), as_int(R_RANK1)), pos(as_int(R_E2), as_int(R_RANK2))])
    n_active = cum[N_EXPERTS - 1]
    tile_expert = jnp.sum(jnp.arange(n_tiles, dtype=jnp.int32)[:, None] >= cum[None, :], axis=1)
    tile_expert = jnp.minimum(tile_expert, tile_expert[n_active - 1]).astype(jnp.int32)
    tile_ids = jnp.arange(n_tiles, dtype=jnp.int32)
    group_last = jnp.any((tile_ids[:, None] == cum[None, :] - 1) & (tiles[None, :] > 0), axis=1)
    zero_tile = (group_last | (tile_ids >= n_active)).astype(jnp.int32)
    return (pos2.astype(jnp.int32), tile_expert, n_active.reshape(1).astype(jnp.int32), zero_tile)


def _row_copy(src, dst, sem):
    return pltpu.make_async_copy(src, dst, sem)


def _dispatch_kernel(pos_ref, zt_ref, h_ref, xs_ref, zbuf, sem, zsem):
    n_chunks = h_ref.shape[0] // DMA_CHUNK
    n_tiles = xs_ref.shape[0] // TM_GROUP

    zbuf[...] = jnp.zeros_like(zbuf)

    def zero_fill(j):
        return pltpu.make_async_copy(zbuf, xs_ref.at[pl.ds(j * TM_GROUP, TM_GROUP)], zsem)

    def zero_start(j, carry):
        @pl.when(zt_ref[j] != 0)
        def _():
            zero_fill(j).start()
        return carry

    def zero_wait(j, carry):
        @pl.when(zt_ref[j] != 0)
        def _():
            zero_fill(j).wait()
        return carry

    lax.fori_loop(0, n_tiles, zero_start, 0)
    lax.fori_loop(0, n_tiles, zero_wait, 0)

    def drain_chunk():
        rows = pl.ds(0, 2 * DMA_CHUNK)
        _row_copy(xs_ref.at[rows], xs_ref.at[rows], sem).wait()

    def chunk(k, carry):
        for r in range(DMA_CHUNK):
            t = k * DMA_CHUNK + r
            _row_copy(h_ref.at[t], xs_ref.at[pos_ref[0, t]], sem).start(priority=0)
            _row_copy(h_ref.at[t], xs_ref.at[pos_ref[1, t]], sem).start(priority=1)

        @pl.when(k > 0)
        def _():
            drain_chunk()

        return carry

    lax.fori_loop(0, n_chunks, chunk, 0)
    drain_chunk()


def _dispatch(pos2, zero_tile, h2_rows, n_rows):
    tile = h2_rows.shape[1:]
    return pl.pallas_call(
        _dispatch_kernel,
        grid_spec=pltpu.PrefetchScalarGridSpec(
            num_scalar_prefetch=2, grid=(1,),
            in_specs=[pl.BlockSpec(h2_rows.shape, lambda i, p, z: (0, 0, 0))],
            out_specs=pl.BlockSpec(memory_space=pl.ANY),
            scratch_shapes=[pltpu.VMEM((TM_GROUP,) + tile, F32),
                            pltpu.SemaphoreType.DMA(()), pltpu.SemaphoreType.DMA(())]),
        out_shape=jax.ShapeDtypeStruct((n_rows,) + tile, F32),
        compiler_params=_cparams(("arbitrary",)),
        name="moe_dispatch",
    )(pos2, zero_tile, h2_rows)


def _expert_kernel(te_ref, na_ref, xs_ref, wg_ref, wu_ref, wd_ref, ys_ref,
                   xbuf, ybuf, sem_in, sem_out):
    del te_ref
    i = pl.program_id(0)
    last = pl.num_programs(0) - 1
    n_active = na_ref[0]
    slot = i % 2
    n_col = D_MODEL // LANES

    def in_copies(tile, s):
        rows = pl.ds(tile * TM_GROUP, TM_GROUP)
        return [pltpu.make_async_copy(xs_ref.at[rows, c], xbuf.at[s, :, c * LANES:(c + 1) * LANES],
                                      sem_in.at[s]) for c in range(n_col)]

    def out_copies(tile, s):
        rows = pl.ds(tile * TM_GROUP, TM_GROUP)
        return [pltpu.make_async_copy(ybuf.at[s, :, c * LANES:(c + 1) * LANES], ys_ref.at[rows, c],
                                      sem_out.at[s]) for c in range(n_col)]

    @pl.when(i == 0)
    def _():
        for cp in in_copies(0, 0):
            cp.start()

    @pl.when(i + 1 < n_active)
    def _():
        for cp in in_copies(i + 1, 1 - slot):
            cp.start()

    @pl.when(i >= 2)
    def _():
        for cp in out_copies(i - 2, slot):
            cp.wait()

    @pl.when(i < n_active)
    def _():
        for cp in in_copies(i, slot):
            cp.wait()
        xb = xbuf[slot].astype(BF16)
        act = _silu(_dot(xb, wg_ref[...])) * _dot(xb, wu_ref[...])
        ybuf[slot] = _dot(act.astype(BF16), wd_ref[...])

    @pl.when(i >= n_active)
    def _():
        ybuf[slot] = jnp.zeros((TM_GROUP, D_MODEL), F32)

    for cp in out_copies(i, slot):
        cp.start()

    @pl.when(i == last)
    def _():
        for cp in out_copies(i, slot):
            cp.wait()

        @pl.when(i >= 1)
        def _():
            for cp in out_copies(i - 1, 1 - slot):
                cp.wait()


def _experts(tile_expert, n_active, xs, j, wg_bf, wu_bf, wd_bf):
    n_rows = xs.shape[0]
    d_e = wg_bf.shape[3]
    wspec = lambda shape: pl.BlockSpec((None, None) + shape, lambda i, te, na: (j, te[i], 0, 0))
    return pl.pallas_call(
        _expert_kernel,
        grid_spec=pltpu.PrefetchScalarGridSpec(
            num_scalar_prefetch=2, grid=(n_rows // TM_GROUP,),
            in_specs=[pl.BlockSpec(memory_space=pl.ANY),
                      wspec((D_MODEL, d_e)), wspec((D_MODEL, d_e)), wspec((d_e, D_MODEL))],
            out_specs=pl.BlockSpec(memory_space=pl.ANY),
            scratch_shapes=[pltpu.VMEM((2, TM_GROUP, D_MODEL), F32),
                            pltpu.VMEM((2, TM_GROUP, D_MODEL), F32),
                            pltpu.SemaphoreType.DMA((2,)), pltpu.SemaphoreType.DMA((2,))]),
        out_shape=jax.ShapeDtypeStruct(xs.shape, F32),
        compiler_params=_cparams(("arbitrary",)),
        name="moe_experts",
    )(tile_expert, n_active, xs, wg_bf, wu_bf, wd_bf)


def _combine_kernel(*refs, final):
    pos_ref, y_ref, x_ref, rec_ref, mod_ref = refs[:5]
    g_ref = refs[5] if final else None
    o_ref, a1, a2, b1, b2, sem_a, sem_b = refs[5 + int(final):]
    i = pl.program_id(0)

    def issue(half_idx, d1, d2, sem):
        base = half_idx * DMA_CHUNK

        for r in range(DMA_CHUNK):
            _row_copy(y_ref.at[pos_ref[0, base + r]], d1.at[r], sem).start(priority=0)
            _row_copy(y_ref.at[pos_ref[1, base + r]], d2.at[r], sem).start(priority=1)

    def drain(d1, d2, sem):
        rows = pl.ds(0, DMA_CHUNK)
        _row_copy(y_ref.at[rows], d1, sem).wait()
        _row_copy(y_ref.at[rows], d2, sem).wait()

    def finish(half, d1, d2):
        rows = slice(half * DMA_CHUNK, (half + 1) * DMA_CHUNK)
        w1 = rec_ref[rows, R_W1:R_W1 + 1]
        w2 = rec_ref[rows, R_W2:R_W2 + 1]
        for c in range(D_MODEL // LANES):
            sl = slice(c * LANES, (c + 1) * LANES)
            y = w1 * d1[:, c, :] + w2 * d2[:, c, :]
            o_ref[rows, sl] = x_ref[rows, sl] + mod_ref[5:6, sl] * y
        if final:
            o_ref[rows, :] = _rms(o_ref[rows, :], g_ref[...])

    @pl.when(i == 0)
    def _():
        issue(0, a1, a2, sem_a)

    issue(2 * i + 1, b1, b2, sem_b)
    drain(a1, a2, sem_a)
    finish(0, a1, a2)

    @pl.when(i + 1 < pl.num_programs(0))
    def _():
        issue(2 * i + 2, a1, a2, sem_a)

    drain(b1, b2, sem_b)
    finish(1, b1, b2)


def _combine(pos2, ys, x1, rec, mods, l, seq_len, lat, final_g):
    n = x1.shape[0]
    final = final_g is not None
    tile = ys.shape[1:]
    tm = 2 * DMA_CHUNK
    tiles_per_seq = max(seq_len // tm, 1)
    if lat:
        mod_idx = lambda i, p: (l, 1 + i // tiles_per_seq, 0, 0)
    else:
        mod_idx = lambda i, p: (l, 0, 0, 0)
    row = lambda i, p: (i, 0)
    slot = pltpu.VMEM((DMA_CHUNK,) + tile, F32)
    in_specs = [pl.BlockSpec(memory_space=pl.ANY),
                pl.BlockSpec((tm, D_MODEL), row),
                pl.BlockSpec((tm, LANES), row),
                pl.BlockSpec((None, None, 6, D_MODEL), mod_idx)]
    args = [pos2, ys, x1, rec, mods]
    if final:
        in_specs.append(pl.BlockSpec((1, D_MODEL), lambda i, p: (0, 0)))
        args.append(final_g)
    return pl.pallas_call(
        functools.partial(_combine_kernel, final=final),
        grid_spec=pltpu.PrefetchScalarGridSpec(
            num_scalar_prefetch=1, grid=(n // tm,),
            in_specs=in_specs,
            out_specs=pl.BlockSpec((tm, D_MODEL), row),
            scratch_shapes=[slot, slot, slot, slot,
                            pltpu.SemaphoreType.DMA(()), pltpu.SemaphoreType.DMA(())]),
        out_shape=jax.ShapeDtypeStruct((n, D_MODEL), F32),
        compiler_params=_cparams(("arbitrary",)),
        name="moe_combine_final" if final else "moe_combine",
    )(*args)


def _moe(x1, h2_rows, rec, counts, mods, l, j, wg_bf, wu_bf, wd_bf, seq_len, lat, final_g):
    n = x1.shape[0]
    n_tiles = 2 * n // TM_GROUP + N_EXPERTS
    pos2, tile_expert, n_active, zero_tile = _group_layout(rec, counts, n_tiles)
    xs = _dispatch(pos2, zero_tile, h2_rows, n_tiles * TM_GROUP)
    ys = _experts(tile_expert, n_active, xs, j, wg_bf, wu_bf, wd_bf)
    return _combine(pos2, ys, x1, rec, mods, l, seq_len, lat, final_g)


def _final_norm_kernel(x_ref, g_ref, o_ref):
    o_ref[...] = _rms(x_ref[...], g_ref[...])


def _final_norm(x, g):
    n = x.shape[0]
    row = lambda i: (i, 0)
    return pl.pallas_call(
        _final_norm_kernel,
        grid=(n // TM_FF,),
        in_specs=[pl.BlockSpec((TM_FF, D_MODEL), row), pl.BlockSpec((1, D_MODEL), lambda i: (0, 0))],
        out_specs=pl.BlockSpec((TM_FF, D_MODEL), row),
        out_shape=jax.ShapeDtypeStruct((n, D_MODEL), F32),
        compiler_params=_cparams(("arbitrary",)),
        name="final_norm",
    )(x, g)


def _rope_tables(t_len):
    t = jnp.arange(t_len)
    pos = jnp.stack([t // GRID_W, t % GRID_W], axis=-1).astype(F32)
    inv = ROPE_THETA ** (-jnp.arange(0, AXIS_DIM, 2, dtype=F32) / AXIS_DIM)
    ang = pos[..., None] * inv
    cos, sin = jnp.cos(ang), jnp.sin(ang)
    c = jnp.concatenate([cos[:, 0], cos[:, 0], cos[:, 1], cos[:, 1]], axis=-1)
    s = jnp.concatenate([-sin[:, 0], sin[:, 0], -sin[:, 1], sin[:, 1]], axis=-1)
    return c, s


def _gate_weights(wa, wi):
    depth = wa.shape[0]
    groups = D_RNN // LANES

    def pair(w):
        w = w.reshape(depth, 2, groups, 2, RNN_BLOCK, RNN_BLOCK)
        z = jnp.zeros_like(w[:, :, :, 0])
        top = jnp.concatenate([w[:, :, :, 0], z], axis=-1)
        bot = jnp.concatenate([z, w[:, :, :, 1]], axis=-1)
        return jnp.concatenate([top, bot], axis=-2)

    w = jnp.concatenate([pair(wa), pair(wi)], axis=-1)
    return (0.5 * jnp.concatenate([w[:, 0], w[:, 1]], axis=-1)).astype(BF16)


def kernel(x_prompt, x_sample, cache_k, cache_v, state_rnn, c, c_ctx, w_mod, b_mod, norm1, norm2, w_in, conv_w, conv_b, lru_wa, lru_ba, lru_wi, lru_bi, lru_lambda, q_norm, k_norm, w_rnn_out, w_attn_out, w_out, ff_gate, ff_up, ff_down, router_w, router_b, exp_gate, exp_up, exp_down, final_norm):
    batch, seq, _ = x_prompt.shape
    dec_batch, dec_seq, _ = x_sample.shape
    depth = w_mod.shape[0]
    past = cache_k.shape[2]
    assert dec_batch + 1 <= MOD_ROWS

    cond8 = jnp.zeros((MOD_ROWS, D_MODEL), F32).at[0].set(c_ctx).at[1:1 + dec_batch].set(c)
    mods = _modulation(cond8, w_mod, b_mod).reshape(depth, MOD_ROWS, 6, D_MODEL)

    gate_w = _gate_weights(lru_wa, lru_wi)
    n_moe = router_w.shape[0]
    wr_pad = jnp.zeros((n_moe, D_MODEL, LANES), F32).at[:, :, :N_EXPERTS].set(router_w)
    wr_head = wr_pad.astype(BF16)
    wr_t = jnp.concatenate([wr_head, (wr_pad - wr_head.astype(F32)).astype(BF16)], axis=-1)
    br_pad = jnp.zeros((n_moe, 1, LANES), F32).at[:, 0, :N_EXPERTS].set(router_b)

    norm1_3 = norm1.reshape(depth, 1, D_MODEL)
    norm2_3 = norm2.reshape(depth, 1, D_MODEL)
    qn3 = q_norm.reshape(depth, 1, HEAD_DIM)
    kn3 = k_norm.reshape(depth, 1, HEAD_DIM)
    conv_b3 = conv_b.reshape(depth, 1, D_RNN)
    ba4 = (0.5 * lru_ba).reshape(depth, 2, 1, D_RNN)
    bi4 = (0.5 * lru_bi).reshape(depth, 2, 1, D_RNN)
    lam4 = lru_lambda.reshape(depth, 2, 1, D_RNN)
    state5 = state_rnn.reshape(dec_batch, depth, 2, 1, D_RNN)
    cache_k4 = cache_k.reshape(dec_batch, depth, past, KV_DIM)
    cache_v4 = cache_v.reshape(dec_batch, depth, past, KV_DIM)
    rope_tabs = _rope_tables(dec_seq)
    final_g = final_norm.reshape(1, D_MODEL)

    n_steps = dec_batch * dec_seq // TQ
    assert n_steps == dec_batch * dec_seq // TM_ROPE

    def flat(w):
        return w.reshape(-1, w.shape[-1])

    def channel_names(l):
        return ("exp_gate", "exp_up", "exp_down") if l % 2 == 1 else ("ff_gate", "ff_up", "ff_down")

    stacked = dict(w_in=w_in, w_rnn_out=w_rnn_out, w_attn_out=w_attn_out, w_out=w_out,
                   ff_gate=ff_gate, ff_up=ff_up, ff_down=ff_down,
                   exp_gate=exp_gate, exp_up=exp_up, exp_down=exp_down)

    def layer_shape(name):
        return (1,) + stacked[name].shape[1:]

    def layer_index(name, l):
        return l if name.startswith("w_") else l // 2

    def rider(name, l, steps):
        w = stacked[name]
        rows = flat(w).shape[0] // w.shape[0] // steps
        return _Rider(flat(w), rows, layer_index(name, l) * steps)

    def names_of(l):
        return ("w_in", "w_rnn_out", "w_attn_out", "w_out") + channel_names(l)

    weights = {l: {} for l in range(depth)}

    def keep(l, names, arrays):
        for name, arr in zip(names, arrays):
            weights[l][name] = arr.reshape(layer_shape(name))

    keep(0, ("w_in",), _convert([rider("w_in", 0, n_steps)], n_steps))
    ctx_steps = batch * seq // TM

    def run_layer(x, l, lat):
        seq_len = dec_seq if lat else seq
        wts = weights[l]
        convert_next = lat and l + 1 < depth
        if convert_next:
            proj_names = names_of(l + 1)[:4]
            attn_names = names_of(l + 1)[4:]
            proj_riders = [rider(name, l + 1, n_steps) for name in proj_names]
            attn_riders = [rider(name, l + 1, n_steps) for name in attn_names]
            target = l + 1
        elif l == 0 and not lat:
            proj_names, attn_names, attn_riders, target = names_of(0)[1:], (), [], 0
            proj_riders = [rider(name, 0, ctx_steps) for name in proj_names]
        else:
            proj_names, attn_names, proj_riders, attn_riders, target = (), (), [], [], None
        outs = _in_proj(x, mods, l, norm1_3, wts["w_in"], 0, qn3, kn3,
                        rope_tabs if lat else None, seq_len, riders=proj_riders,
                        caches=None if lat else new_kv)
        xr, yr, q, _, kr, v, gr, ga = outs[:8]
        if target is not None:
            keep(target, proj_names, outs[8:8 + len(proj_names)])
        if not lat:
            new_kv[:] = outs[8 + len(proj_names):]
        out_r, end_f, end_b = _rnn(xr, yr, l, conv_w, conv_b3, gate_w, ba4, bi4, lam4,
                                   state5 if lat else None, seq_len)
        if lat:
            outs = _attn_lat(q, kr, v, cache_k4, cache_v4, l, seq_len, riders=attn_riders)
            out_a = outs[0]
            if target is not None:
                keep(target, attn_names, outs[1:])
        else:
            out_a = _attn_ctx(q, kr, v, seq_len)
        mix_args = (x, out_r, out_a, gr, ga, mods, l, norm2_3, wts["w_rnn_out"],
                    wts["w_attn_out"], wts["w_out"], 0, seq_len, lat)
        if l % 2 == 1:
            x1, h2_rows, rec, counts = _mix_out(*mix_args, router=(l // 2, wr_t, br_pad))
            x = _moe(x1, h2_rows, rec, counts, mods, l, 0, wts["exp_gate"], wts["exp_up"],
                     wts["exp_down"], seq_len, lat, final_g if l == depth - 1 else None)
        else:
            x1, h2 = _mix_out(*mix_args)
            x = _ffn(x1, h2, mods, l, 0, wts["ff_gate"], wts["ff_up"], wts["ff_down"],
                     seq_len, lat)
            if l == depth - 1:
                x = _final_norm(x, final_g)
        return x, jnp.stack([end_f, end_b], axis=1)

    new_kv = [jnp.zeros((batch, depth, seq, N_KV_HEADS, HEAD_DIM), F32) for _ in range(2)]
    y_ctx = x_prompt.reshape(batch * seq, D_MODEL)
    y_lat = x_sample.reshape(dec_batch * dec_seq, D_MODEL)
    ss = []
    for l in range(depth):
        y_ctx, st = run_layer(y_ctx, l, False)
        ss.append(st)
        y_lat, _ = run_layer(y_lat, l, True)

    new_state = jnp.stack(ss, axis=1)
    return (y_ctx.reshape(batch, seq, D_MODEL), y_lat.reshape(dec_batch, dec_seq, D_MODEL),
            new_kv[0], new_kv[1], new_state)
```

```python
import functools

import jax
import jax.numpy as jnp
from jax import lax
from jax.experimental import pallas as pl
from jax.experimental.pallas import tpu as pltpu

F32 = jnp.float32
BF16 = jnp.bfloat16

D_MODEL = 1024
D_RNN = 1024
N_RNN_BLOCKS = 16
RNN_BLOCK = D_RNN // N_RNN_BLOCKS
CONV_W = 4
CONV_LEFT = 2
LRU_C = 8.0
N_HEADS = 8
N_KV_HEADS = 2
HEAD_DIM = 128
AXIS_DIM = HEAD_DIM // 2
ROPE_THETA = 10000.0
GRID_W = 64
Q_DIM = N_HEADS * HEAD_DIM
KV_DIM = N_KV_HEADS * HEAD_DIM
N_EXPERTS = 8
EPS = 1e-6
LOG2_E = 1.4426950408889634
RSQRT_FLOOR = 1e-30
D_IN = 2 * D_RNN + Q_DIM + 2 * KV_DIM + 2 * D_MODEL
O_XR, O_YR, O_Q = 0, D_RNN, 2 * D_RNN
O_K = O_Q + Q_DIM
O_V = O_K + KV_DIM
O_GR = O_V + KV_DIM
O_GA = O_GR + D_MODEL

LANES = 128
SUBLANES = 8
MOD_ROWS = 8
VMEM_LIMIT = 56 * 1024 * 1024

TM = 512
TM_ROPE = 256
RESIDENT = pl.Buffered(1)
RNN_T = 256
RNN_SEG = SUBLANES
RNN_C = 512
RNN_TCH = 32
RNN_SCAN_UNROLL = 8
TQ = 256
TM_FF = 512
TM_GROUP = 256
DMA_CHUNK = 256


def _cparams(sem):
    return pltpu.CompilerParams(dimension_semantics=sem, vmem_limit_bytes=VMEM_LIMIT)


def _sigmoid(x):
    return 1.0 / (1.0 + jnp.exp(-x))


def _silu(x):
    return x * _sigmoid(x)


def _gelu_tanh(x):
    return 0.5 * x * (1.0 + jnp.tanh(0.7978845608028654 * (x + 0.044715 * (x * x * x))))


def _rms(x, g):
    ms = jnp.mean(x * x, axis=-1, keepdims=True)
    return x * lax.rsqrt(ms + EPS) * g


def _dot(a, b):
    return jnp.dot(a, b, preferred_element_type=F32)


def _mod_kernel(cond_ref, w_ref, b_ref, o_ref):
    s = _silu(cond_ref[...]).astype(BF16)
    o_ref[...] = _dot(s, w_ref[...].astype(BF16)) + b_ref[...]


def _modulation(cond8, w_mod, b_mod):
    depth = w_mod.shape[0]
    tn = 1536
    return pl.pallas_call(
        _mod_kernel,
        grid=(depth, 6 * D_MODEL // tn),
        in_specs=[
            pl.BlockSpec((MOD_ROWS, D_MODEL), lambda l, j: (0, 0)),
            pl.BlockSpec((None, D_MODEL, tn), lambda l, j: (l, 0, j)),
            pl.BlockSpec((None, 1, tn), lambda l, j: (l, 0, j)),
        ],
        out_specs=pl.BlockSpec((None, MOD_ROWS, tn), lambda l, j: (l, 0, j)),
        out_shape=jax.ShapeDtypeStruct((depth, MOD_ROWS, 6 * D_MODEL), F32),
        compiler_params=_cparams(("arbitrary", "arbitrary")),
        name="adaln_mod",
    )(cond8, w_mod, b_mod.reshape(depth, 1, 6 * D_MODEL))


def _swap_halves(x):
    lane = lax.broadcasted_iota(jnp.int32, x.shape, 1)
    return jnp.where((lane % AXIS_DIM) < AXIS_DIM // 2,
                     pltpu.roll(x, LANES - AXIS_DIM // 2, axis=1),
                     pltpu.roll(x, AXIS_DIM // 2, axis=1))


class _Rider:
    def __init__(self, w2d, rows, first):
        self.w2d, self.rows, self.first = w2d, rows, first


def _rider_specs(riders, n_steps, step_of):
    in_specs, out_specs, out_shape = [], [], []
    for r in riders:
        cols = r.w2d.shape[1]
        assert r.rows % 16 == 0 and (r.first + n_steps) * r.rows <= r.w2d.shape[0]
        in_specs.append(pl.BlockSpec((r.rows, cols), lambda *g, r=r: (r.first + step_of(*g), 0)))
        out_specs.append(pl.BlockSpec((r.rows, cols), lambda *g: (step_of(*g), 0)))
        out_shape.append(jax.ShapeDtypeStruct((r.rows * n_steps, cols), BF16))
    return in_specs, out_specs, out_shape


def _convert_riders(in_refs, out_refs):
    for w_ref, o_ref in zip(in_refs, out_refs):
        o_ref[...] = w_ref[...].astype(BF16)


def _convert_kernel(*refs):
    _convert_riders(refs[:len(refs) // 2], refs[len(refs) // 2:])


def _convert(riders, n_steps):
    r_in, r_out, r_shape = _rider_specs(riders, n_steps, lambda i: i)
    return pl.pallas_call(
        _convert_kernel,
        grid=(n_steps,),
        in_specs=r_in,
        out_specs=r_out,
        out_shape=r_shape,
        compiler_params=_cparams(("arbitrary",)),
        name="convert_weights",
    )(*[r.w2d for r in riders])


def _in_proj_kernel(*refs, rope, n_riders, cache_layer, seq_len):
    n_in = 8 if rope else 6
    n_cache = 0 if cache_layer is None else 2
    first_out = n_in + n_riders + n_cache
    _convert_riders(refs[n_in:n_in + n_riders], refs[first_out + 8:first_out + 8 + n_riders])
    cache_refs = refs[first_out + 8 + n_riders:]
    refs = refs[:n_in] + refs[first_out:first_out + 8]
    if rope:
        (x_ref, mod_ref, n1_ref, w_ref, qn_ref, kn_ref, c_ref, s_ref,
         xr_ref, yr_ref, q_ref, kno_ref, kro_ref, v_ref, gr_ref, ga_ref) = refs
    else:
        (x_ref, mod_ref, n1_ref, w_ref, qn_ref, kn_ref,
         xr_ref, yr_ref, q_ref, kno_ref, kro_ref, v_ref, gr_ref, ga_ref) = refs
    x = x_ref[...]
    h = _rms(x, n1_ref[...]) * (1.0 + mod_ref[1:2, :]) + mod_ref[0:1, :]
    hb = h.astype(BF16)

    xr_ref[...] = _dot(hb, w_ref[:, O_XR:O_XR + D_RNN])
    yr_ref[...] = _dot(hb, w_ref[:, O_YR:O_YR + D_RNN])
    gr_ref[...] = _dot(hb, w_ref[:, O_GR:O_GR + D_MODEL]).astype(BF16)
    ga_ref[...] = _dot(hb, w_ref[:, O_GA:O_GA + D_MODEL]).astype(BF16)
    v_ref[...] = _dot(hb, w_ref[:, O_V:O_V + KV_DIM])

    if rope:
        cs, sn = c_ref[...], s_ref[...]
    zq = _dot(hb, w_ref[:, O_Q:O_Q + Q_DIM])
    for hd in range(N_HEADS):
        t = _rms(zq[:, hd * HEAD_DIM:(hd + 1) * HEAD_DIM], qn_ref[...])
        if rope:
            t = t * cs + _swap_halves(t) * sn
        q_ref[:, hd * HEAD_DIM:(hd + 1) * HEAD_DIM] = t.astype(BF16)
    zk = _dot(hb, w_ref[:, O_K:O_K + KV_DIM])
    for hd in range(N_KV_HEADS):
        t = _rms(zk[:, hd * HEAD_DIM:(hd + 1) * HEAD_DIM], kn_ref[...])
        kno_ref[:, hd * HEAD_DIM:(hd + 1) * HEAD_DIM] = t
        if rope:
            t = t * cs + _swap_halves(t) * sn
        kro_ref[:, hd * HEAD_DIM:(hd + 1) * HEAD_DIM] = t.astype(BF16)

    if cache_layer is None:
        return
    newk_ref, newv_ref, kbuf, vbuf, csem = cache_refs
    step = pl.program_id(0)
    slot = step % 2
    seqs_per_tile = x_ref.shape[0] // seq_len

    def cache_copies(k, s):
        cps = []
        for sq in range(seqs_per_tile):
            rows = slice(sq * seq_len, (sq + 1) * seq_len)
            for hd in range(N_KV_HEADS):
                cols = slice(hd * HEAD_DIM, (hd + 1) * HEAD_DIM)
                dst = (k * seqs_per_tile + sq, cache_layer, slice(None), hd, slice(None))
                cps.append(pltpu.make_async_copy(kbuf.at[s, rows, cols], newk_ref.at[dst], csem.at[s]))
                cps.append(pltpu.make_async_copy(vbuf.at[s, rows, cols], newv_ref.at[dst], csem.at[s]))
        return cps

    @pl.when(step >= 2)
    def _():
        for cp in cache_copies(step - 2, slot):
            cp.wait()

    kbuf[slot] = kno_ref[...]
    vbuf[slot] = v_ref[...]
    for cp in cache_copies(step, slot):
        cp.start()

    @pl.when(step == pl.num_programs(0) - 1)
    def _():
        for cp in cache_copies(step, slot):
            cp.wait()

        @pl.when(step >= 1)
        def _():
            for cp in cache_copies(step - 1, 1 - slot):
                cp.wait()


def _in_proj(x, mods, l, norm1, w_in_bf, wl, q_norm, k_norm, rope_tabs, seq_len, riders=(),
             caches=None):
    n = x.shape[0]
    rope = rope_tabs is not None
    tm = TM_ROPE if rope else TM
    tiles_per_seq = seq_len // tm
    if rope:
        mod_idx = lambda i: (l, 1 + i // tiles_per_seq, 0, 0)
    else:
        mod_idx = lambda i: (l, 0, 0, 0)
    row = lambda i: (i, 0)
    in_specs = [
        pl.BlockSpec((tm, D_MODEL), row),
        pl.BlockSpec((None, None, 6, D_MODEL), mod_idx),
        pl.BlockSpec((None, 1, D_MODEL), lambda i: (l, 0, 0)),
        pl.BlockSpec((None, D_MODEL, D_IN), lambda i: (wl, 0, 0), pipeline_mode=RESIDENT),
        pl.BlockSpec((None, 1, HEAD_DIM), lambda i: (l, 0, 0)),
        pl.BlockSpec((None, 1, HEAD_DIM), lambda i: (l, 0, 0)),
    ]
    args = [x, mods, norm1, w_in_bf, q_norm, k_norm]
    if rope:
        in_specs += [pl.BlockSpec((tm, HEAD_DIM), lambda i: (i % tiles_per_seq, 0))] * 2
        args += list(rope_tabs)
    widths = [(D_RNN, F32), (D_RNN, F32), (Q_DIM, BF16), (KV_DIM, F32), (KV_DIM, BF16),
              (KV_DIM, F32), (D_MODEL, BF16), (D_MODEL, BF16)]
    r_in, r_out, r_shape = _rider_specs(riders, n // tm, lambda i: i)
    in_specs = in_specs + r_in
    out_specs = [pl.BlockSpec((tm, w), row) for w, _ in widths] + r_out
    out_shape = [jax.ShapeDtypeStruct((n, w), dt) for w, dt in widths] + r_shape
    args = args + [r.w2d for r in riders]
    scratch, aliases = [], {}
    if caches is not None:
        assert tm % seq_len == 0
        for c in caches:
            aliases[len(args)] = len(out_shape)
            args.append(c)
            in_specs.append(pl.BlockSpec(memory_space=pl.ANY))
            out_specs.append(pl.BlockSpec(memory_space=pl.ANY))
            out_shape.append(jax.ShapeDtypeStruct(c.shape, c.dtype))
        scratch = [pltpu.VMEM((2, tm, KV_DIM), F32), pltpu.VMEM((2, tm, KV_DIM), F32),
                   pltpu.SemaphoreType.DMA((2,))]
    return pl.pallas_call(
        functools.partial(_in_proj_kernel, rope=rope, n_riders=len(riders),
                          cache_layer=None if caches is None else l, seq_len=seq_len),
        grid=(n // tm,),
        in_specs=in_specs,
        out_specs=out_specs,
        out_shape=out_shape,
        scratch_shapes=scratch,
        input_output_aliases=aliases,
        compiler_params=_cparams(("arbitrary",)),
        name="in_proj_lat" if rope else "in_proj_ctx",
    )(*args)


def _rnn_kernel(*refs, chained):
    refs = list(refs)
    xr_ref, yr_ref, cw_ref, cb_ref, gw_ref, ba_ref, bi_ref, lam_ref = refs[:8]
    refs = refs[8:]
    h0_ref = refs.pop(0) if chained else None
    out_ref, sf_ref, sb_ref, xext, ybuf, obuf, a_f, b_f, a_b, b_b, sem_in, sem_out = refs

    g = pl.program_id(0)
    cb = pl.program_id(1)
    n_cb = pl.num_programs(1)
    step = g * n_cb + cb
    n_steps = pl.num_programs(0) * n_cb
    slot = step % 2

    def hbm_window(kk, s):
        rows = pl.ds(pl.multiple_of(((kk // n_cb) * RNN_SEG + s) * RNN_T, RNN_T), RNN_T)
        cols = pl.ds(pl.multiple_of((kk % n_cb) * RNN_C, RNN_C), RNN_C)
        return rows, cols

    def in_copies(kk, sl_):
        cps = []
        for s in range(RNN_SEG):
            win = hbm_window(kk, s)
            cps.append(pltpu.make_async_copy(
                xr_ref.at[win], xext.at[sl_, pl.ds(CONV_LEFT, RNN_T), s], sem_in.at[sl_]))
            cps.append(pltpu.make_async_copy(yr_ref.at[win], ybuf.at[sl_, :, s], sem_in.at[sl_]))
        return cps

    def out_copies(kk, sl_):
        return [pltpu.make_async_copy(obuf.at[sl_, :, s], out_ref.at[hbm_window(kk, s)],
                                      sem_out.at[sl_]) for s in range(RNN_SEG)]

    @pl.when(step == 0)
    def _():
        for cp in in_copies(0, 0):
            cp.start()

    @pl.when(step + 1 < n_steps)
    def _():
        for cp in in_copies(step + 1, 1 - slot):
            cp.start()

    for cp in in_copies(step, slot):
        cp.wait()

    seg = lax.broadcasted_iota(jnp.int32, (RNN_SEG, RNN_C), 0)
    zero_row = jnp.zeros((RNN_SEG, RNN_C), F32)
    if chained:
        xext[slot, 0] = jnp.where(seg == 0, zero_row, pltpu.roll(xext[slot, RNN_T], 1, axis=0))
        xext[slot, 1] = jnp.where(seg == 0, zero_row, pltpu.roll(xext[slot, RNN_T + 1], 1, axis=0))
        xext[slot, RNN_T + CONV_LEFT] = jnp.where(
            seg == RNN_SEG - 1, zero_row, pltpu.roll(xext[slot, CONV_LEFT], RNN_SEG - 1, axis=0))
    else:
        xext[slot, 0] = zero_row
        xext[slot, 1] = zero_row
        xext[slot, RNN_T + CONV_LEFT] = zero_row

    scans = ((a_f, b_f), (a_b, b_b))
    rows2d = RNN_TCH * RNN_SEG

    def gate_chunk(tc, carry):
        t0 = pl.multiple_of(tc * RNN_TCH, RNN_TCH)
        for j in range(RNN_C // LANES):
            sl = slice(j * LANES, (j + 1) * LANES)
            xc = jnp.broadcast_to(cb_ref[:, sl].reshape(1, 1, LANES), (RNN_TCH, RNN_SEG, LANES))
            for k in range(CONV_W):
                xc = xc + (xext[slot, pl.ds(t0 + k, RNN_TCH), :, sl]
                           * cw_ref[k:k + 1, sl].reshape(1, 1, LANES))
            xc = xc.reshape(rows2d, LANES)
            pre = _dot(xc.astype(BF16), gw_ref[j])
            for d, (a_s, b_s) in enumerate(scans):
                lam = lam_ref[d, :, sl]
                log2_decay = (-LRU_C * LOG2_E) * (jnp.maximum(-lam, 0.0)
                                                  + jnp.log(1.0 + jnp.exp(-jnp.abs(lam))))
                r = 0.5 * jnp.tanh(pre[:, 2 * d * LANES:(2 * d + 1) * LANES]
                                   + ba_ref[d, :, sl]) + 0.5
                gi = 0.5 * jnp.tanh(pre[:, (2 * d + 1) * LANES:(2 * d + 2) * LANES]
                                    + bi_ref[d, :, sl]) + 0.5
                a = jnp.exp2(r * log2_decay)
                y = 1.0 - a * a
                b = (y * lax.rsqrt(jnp.maximum(y, RSQRT_FLOOR))) * (gi * xc)
                a_s[pl.ds(t0, RNN_TCH), :, sl] = a.reshape(RNN_TCH, RNN_SEG, LANES)
                b_s[pl.ds(t0, RNN_TCH), :, sl] = b.reshape(RNN_TCH, RNN_SEG, LANES)
        return carry

    lax.fori_loop(0, RNN_T // RNN_TCH, gate_chunk, 0)

    def scan_step(t, carry):
        hf, hb, pf, pb = carry
        tb = RNN_T - 1 - t
        af_t = a_f[t]
        hf = af_t * hf + b_f[t]
        b_f[t] = hf
        ab_t = a_b[tb]
        hb = ab_t * hb + b_b[tb]
        b_b[tb] = hb
        if chained:
            pf = af_t * pf
            a_f[t] = pf
            pb = ab_t * pb
            a_b[tb] = pb
        return hf, hb, pf, pb

    ones = jnp.ones((RNN_SEG, RNN_C), F32)
    hf, hb, pf, pb = lax.fori_loop(0, RNN_T, scan_step, (zero_row, zero_row, ones, ones),
                                   unroll=RNN_SCAN_UNROLL)

    if chained:
        st_f = jnp.broadcast_to(h0_ref[0], (RNN_SEG, RNN_C))
        st_b = jnp.broadcast_to(h0_ref[1], (RNN_SEG, RNN_C))
        in_f, in_b = st_f, st_b
        for _ in range(RNN_SEG - 1):
            in_f = jnp.where(seg == 0, st_f, pltpu.roll(pf * in_f + hf, 1, axis=0))
            in_b = jnp.where(seg == RNN_SEG - 1, st_b,
                             pltpu.roll(pb * in_b + hb, RNN_SEG - 1, axis=0))
        sf_ref[...] = pf * in_f + hf
        sb_ref[...] = pb * in_b + hb
    else:
        sf_ref[...] = hf
        sb_ref[...] = hb

    @pl.when(step >= 2)
    def _():
        for cp in out_copies(step - 2, slot):
            cp.wait()

    def finish_chunk(tc, carry):
        rows = pl.ds(pl.multiple_of(tc * RNN_TCH, RNN_TCH), RNN_TCH)
        h_f, h_b = b_f[rows], b_b[rows]
        if chained:
            h_f = a_f[rows] * in_f.reshape(1, RNN_SEG, RNN_C) + h_f
            h_b = a_b[rows] * in_b.reshape(1, RNN_SEG, RNN_C) + h_b
        obuf[slot, rows] = (h_f + h_b) * _gelu_tanh(ybuf[slot, rows])
        return carry

    lax.fori_loop(0, RNN_T // RNN_TCH, finish_chunk, 0)

    for cp in out_copies(step, slot):
        cp.start()

    @pl.when(step == n_steps - 1)
    def _():
        for cp in out_copies(step, slot):
            cp.wait()

        @pl.when(step >= 1)
        def _():
            for cp in out_copies(step - 1, 1 - slot):
                cp.wait()


def _rnn(xr, yr, l, conv_w, conv_b, gate_w, lru_ba, lru_bi, lru_lambda, state5, seq_len):
    n = xr.shape[0]
    chained = state5 is not None
    group_len = RNN_SEG * RNN_T
    assert n % group_len == 0 and seq_len in ((group_len,) if chained else (RNN_T,))
    n_groups = n // group_len
    lanes_per_step = RNN_C // LANES
    vec = pl.BlockSpec((None, 2, 1, RNN_C), lambda g, c: (l, 0, 0, c))
    in_specs = [
        pl.BlockSpec(memory_space=pl.ANY),
        pl.BlockSpec(memory_space=pl.ANY),
        pl.BlockSpec((None, CONV_W, RNN_C), lambda g, c: (l, 0, c)),
        pl.BlockSpec((None, 1, RNN_C), lambda g, c: (l, 0, c)),
        pl.BlockSpec((None, lanes_per_step, LANES, 4 * LANES), lambda g, c: (l, c, 0, 0)),
        vec, vec, vec,
    ]
    args = [xr, yr, conv_w, conv_b, gate_w, lru_ba, lru_bi, lru_lambda]
    if chained:
        in_specs.append(pl.BlockSpec((None, None, 2, 1, RNN_C), lambda g, c: (g, l, 0, 0, c)))
        args.append(state5)
    state_spec = pl.BlockSpec((RNN_SEG, RNN_C), lambda g, c: (g, c))
    state_shape = jax.ShapeDtypeStruct((n_groups * RNN_SEG, D_RNN), F32)
    work = lambda t: pltpu.VMEM((t, RNN_SEG, RNN_C), F32)
    return pl.pallas_call(
        functools.partial(_rnn_kernel, chained=chained),
        grid=(n_groups, D_RNN // RNN_C),
        in_specs=in_specs,
        out_specs=[pl.BlockSpec(memory_space=pl.ANY), state_spec, state_spec],
        out_shape=[jax.ShapeDtypeStruct((n, D_RNN), F32), state_shape, state_shape],
        scratch_shapes=[
            pltpu.VMEM((2, RNN_T + CONV_W - 1, RNN_SEG, RNN_C), F32),
            pltpu.VMEM((2, RNN_T, RNN_SEG, RNN_C), F32),
            pltpu.VMEM((2, RNN_T, RNN_SEG, RNN_C), F32),
            work(RNN_T), work(RNN_T), work(RNN_T), work(RNN_T),
            pltpu.SemaphoreType.DMA((2,)), pltpu.SemaphoreType.DMA((2,)),
        ],
        compiler_params=_cparams(("arbitrary", "arbitrary")),
        name="rnn_lat" if chained else "rnn_ctx",
    )(*args)


def _attend_heads(q_ref, k_all, v_all, o_ref, stack_heads):
    exp2_scale = (1.0 / (HEAD_DIM ** 0.5)) * LOG2_E
    group = N_HEADS // N_KV_HEADS
    tq = q_ref.shape[0]
    for kh in range(N_KV_HEADS):
        k = k_all[:, kh * HEAD_DIM:(kh + 1) * HEAD_DIM]
        v = v_all[:, kh * HEAD_DIM:(kh + 1) * HEAD_DIM]
        heads = [slice((kh * group + g) * HEAD_DIM, (kh * group + g + 1) * HEAD_DIM)
                 for g in range(group)]
        stacks = [heads] if stack_heads else [[sl] for sl in heads]
        for stack in stacks:
            q = jnp.concatenate([q_ref[:, sl] for sl in stack], axis=0)
            s = lax.dot_general(q, k, (((1,), (1,)), ((), ())), preferred_element_type=F32)
            m = jnp.max(s, axis=-1, keepdims=True)
            p = jnp.exp2((s - m) * exp2_scale)
            denom = jnp.sum(p, axis=-1, keepdims=True)
            o = (_dot(p.astype(BF16), v) / denom).astype(BF16)
            for g, sl in enumerate(stack):
                o_ref[:, sl] = o[g * tq:(g + 1) * tq]


def _attn_ctx_kernel(q_ref, k_ref, v_ref, o_ref):
    _attend_heads(q_ref, k_ref[...], v_ref[...].astype(BF16), o_ref, stack_heads=True)


def _attn_ctx(q, kr, v, seq_len):
    n = q.shape[0]
    row = lambda b: (b, 0)
    return pl.pallas_call(
        _attn_ctx_kernel,
        grid=(n // seq_len,),
        in_specs=[pl.BlockSpec((seq_len, Q_DIM), row),
                  pl.BlockSpec((seq_len, KV_DIM), row),
                  pl.BlockSpec((seq_len, KV_DIM), row)],
        out_specs=pl.BlockSpec((seq_len, Q_DIM), row),
        out_shape=jax.ShapeDtypeStruct((n, Q_DIM), BF16),
        compiler_params=_cparams(("arbitrary",)),
        name="attn_ctx",
    )(q, kr, v)


def _attn_lat_kernel(*refs, past, n_riders):
    _convert_riders(refs[5:5 + n_riders], refs[6 + n_riders:6 + 2 * n_riders])
    q_ref, kc_ref, vc_ref, kl_ref, vl_ref = refs[:5]
    o_ref = refs[5 + n_riders]
    k_all, v_all = refs[6 + 2 * n_riders:]

    @pl.when(pl.program_id(1) == 0)
    def _():
        k_all[0:past, :] = kc_ref[...].astype(BF16)
        k_all[past:, :] = kl_ref[...]
        v_all[0:past, :] = vc_ref[...].astype(BF16)
        v_all[past:, :] = vl_ref[...].astype(BF16)

    _attend_heads(q_ref, k_all[...], v_all[...], o_ref, stack_heads=False)


def _attn_lat(q, kr, v, cache_k4, cache_v4, l, seq_len, riders=()):
    n = q.shape[0]
    past = cache_k4.shape[2]
    nq = seq_len // TQ
    step_of = lambda b, j: b * nq + j
    r_in, r_out, r_shape = _rider_specs(riders, (n // seq_len) * nq, step_of)
    return pl.pallas_call(
        functools.partial(_attn_lat_kernel, past=past, n_riders=len(riders)),
        grid=(n // seq_len, nq),
        in_specs=[pl.BlockSpec((TQ, Q_DIM), lambda b, j: (b * nq + j, 0)),
                  pl.BlockSpec((None, None, past, KV_DIM), lambda b, j: (b, l, 0, 0)),
                  pl.BlockSpec((None, None, past, KV_DIM), lambda b, j: (b, l, 0, 0)),
                  pl.BlockSpec((seq_len, KV_DIM), lambda b, j: (b, 0)),
                  pl.BlockSpec((seq_len, KV_DIM), lambda b, j: (b, 0))] + r_in,
        out_specs=[pl.BlockSpec((TQ, Q_DIM), lambda b, j: (b * nq + j, 0))] + r_out,
        out_shape=[jax.ShapeDtypeStruct((n, Q_DIM), BF16)] + r_shape,
        scratch_shapes=[pltpu.VMEM((past + seq_len, KV_DIM), BF16),
                        pltpu.VMEM((past + seq_len, KV_DIM), BF16)],
        compiler_params=_cparams(("arbitrary", "arbitrary")),
        name="attn_lat",
    )(q, cache_k4, cache_v4, kr, v, *[r.w2d for r in riders])


R_E1, R_E2, R_W1, R_W2, R_RANK1, R_RANK2 = range(6)


def _route(h2, wrt_ref, brt_ref, count_ref):
    shape = (h2.shape[0], LANES)
    lane = lax.broadcasted_iota(jnp.int32, shape, 1)
    neg = jnp.float32(-jnp.inf)
    h_hi = h2.astype(BF16)
    h_lo = (h2 - h_hi.astype(F32)).astype(BF16)
    z_hi = _dot(h_hi, wrt_ref[...])
    z_lo = _dot(h_lo, wrt_ref[...])
    logits = ((z_hi[:, :LANES] + z_hi[:, LANES:]) + (z_lo[:, :LANES] + z_lo[:, LANES:])
              + brt_ref[...])
    logits = jnp.where(lane < N_EXPERTS, logits, neg)
    v1 = jnp.max(logits, axis=-1, keepdims=True)
    i1 = jnp.min(jnp.where(logits == v1, lane, LANES), axis=-1, keepdims=True)
    rest = jnp.where(lane == i1, neg, logits)
    v2 = jnp.max(rest, axis=-1, keepdims=True)
    i2 = jnp.min(jnp.where(rest == v2, lane, LANES), axis=-1, keepdims=True)
    e2 = jnp.exp(v2 - v1)
    w1 = 1.0 / (1.0 + e2)
    w2 = e2 / (1.0 + e2)
    chosen = jnp.where((lane == i1) | (lane == i2), 1.0, 0.0)
    r_id = lax.broadcasted_iota(jnp.int32, (shape[0], shape[0]), 0)
    c_id = lax.broadcasted_iota(jnp.int32, (shape[0], shape[0]), 1)
    before = jnp.where(c_id < r_id, 1.0, 0.0).astype(BF16)
    rank = _dot(before, chosen.astype(BF16)) + count_ref[...]
    rank1 = jnp.sum(jnp.where(lane == i1, rank, 0.0), axis=-1, keepdims=True)
    rank2 = jnp.sum(jnp.where(lane == i2, rank, 0.0), axis=-1, keepdims=True)
    count_ref[...] += jnp.sum(chosen, axis=0, keepdims=True)
    rec = jnp.zeros(shape, F32)
    for k, val in ((R_E1, i1.astype(F32)), (R_E2, i2.astype(F32)), (R_W1, w1), (R_W2, w2),
                   (R_RANK1, rank1), (R_RANK2, rank2)):
        rec = jnp.where(lane == k, val, rec)
    return rec


def _mix_out_kernel(*refs, route):
    (x_ref, r_ref, a_ref, gr_ref, ga_ref, mod_ref, n2_ref, wr_ref, wa_ref, wo_ref) = refs[:10]
    if route:
        wrt_ref, brt_ref, x1_ref, h2_ref, rec_ref, cnt_ref, count_ref, hbuf, hsem = refs[10:]
    else:
        x1_ref, h2_ref = refs[10:]
    merged = (_sigmoid(gr_ref[...].astype(F32)) * _dot(r_ref[...].astype(BF16), wr_ref[...])
              + _sigmoid(ga_ref[...].astype(F32)) * _dot(a_ref[...], wa_ref[...]))
    mix = _dot(merged.astype(BF16), wo_ref[...])
    x1 = x_ref[...] + mod_ref[2:3, :] * mix
    x1_ref[...] = x1
    h2 = _rms(x1, n2_ref[...]) * (1.0 + mod_ref[4:5, :]) + mod_ref[3:4, :]
    if not route:
        h2_ref[...] = h2.astype(BF16)
        return

    @pl.when(pl.program_id(0) == 0)
    def _():
        count_ref[...] = jnp.zeros_like(count_ref)

    rec_ref[...] = _route(h2, wrt_ref, brt_ref, count_ref)
    cnt_ref[...] = count_ref[...]

    step = pl.program_id(0)
    slot = step % 2

    def h_copies(k, s):
        rows = pl.ds(k * TM, TM)
        return [pltpu.make_async_copy(hbuf.at[s, :, c * LANES:(c + 1) * LANES], h2_ref.at[rows, c],
                                      hsem.at[s]) for c in range(D_MODEL // LANES)]

    @pl.when(step >= 2)
    def _():
        for cp in h_copies(step - 2, slot):
            cp.wait()

    hbuf[slot] = h2
    for cp in h_copies(step, slot):
        cp.start()

    @pl.when(step == pl.num_programs(0) - 1)
    def _():
        for cp in h_copies(step, slot):
            cp.wait()

        @pl.when(step >= 1)
        def _():
            for cp in h_copies(step - 1, 1 - slot):
                cp.wait()


def _mix_out(x, out_r, out_a, gr, ga, mods, l, norm2, wr_bf, wa_bf, wo_bf, wl, seq_len, lat,
             router=None):
    n = x.shape[0]
    tiles_per_seq = seq_len // TM
    if lat:
        mod_idx = lambda i: (l, 1 + i // tiles_per_seq, 0, 0)
    else:
        mod_idx = lambda i: (l, 0, 0, 0)
    row = lambda i: (i, 0)
    wspec = pl.BlockSpec((None, D_MODEL, D_MODEL), lambda i: (wl, 0, 0), pipeline_mode=RESIDENT)
    in_specs = [pl.BlockSpec((TM, D_MODEL), row)] * 5 + [
        pl.BlockSpec((None, None, 6, D_MODEL), mod_idx),
        pl.BlockSpec((None, 1, D_MODEL), lambda i: (l, 0, 0)),
        wspec, wspec, wspec]
    args = [x, out_r, out_a, gr, ga, mods, norm2, wr_bf, wa_bf, wo_bf]
    out_specs = [pl.BlockSpec((TM, D_MODEL), row)]
    out_shape = [jax.ShapeDtypeStruct((n, D_MODEL), F32)]
    scratch = []
    if router is None:
        out_specs.append(pl.BlockSpec((TM, D_MODEL), row))
        out_shape.append(jax.ShapeDtypeStruct((n, D_MODEL), BF16))
    else:
        j, wr_t, br_pad = router
        in_specs += [pl.BlockSpec((None, D_MODEL, 2 * LANES), lambda i: (j, 0, 0)),
                     pl.BlockSpec((None, 1, LANES), lambda i: (j, 0, 0))]
        args += [wr_t, br_pad]
        out_specs += [pl.BlockSpec(memory_space=pl.ANY),
                      pl.BlockSpec((TM, LANES), row),
                      pl.BlockSpec((1, LANES), lambda i: (0, 0))]
        out_shape += [jax.ShapeDtypeStruct((n, D_MODEL // LANES, LANES), F32),
                      jax.ShapeDtypeStruct((n, LANES), F32),
                      jax.ShapeDtypeStruct((1, LANES), F32)]
        scratch = [pltpu.VMEM((1, LANES), F32), pltpu.VMEM((2, TM, D_MODEL), F32),
                   pltpu.SemaphoreType.DMA((2,))]
    return pl.pallas_call(
        functools.partial(_mix_out_kernel, route=router is not None),
        grid=(n // TM,),
        in_specs=in_specs,
        out_specs=out_specs,
        out_shape=out_shape,
        scratch_shapes=scratch,
        compiler_params=_cparams(("arbitrary",)),
        name=("mix_route" if router is not None else "mix_out") + ("_lat" if lat else "_ctx"),
    )(*args)


def _ffn_kernel(x_ref, h_ref, mod_ref, wg_ref, wu_ref, wd_ref, o_ref, *, f_chunk):
    h = h_ref[...]
    d_ff = wg_ref.shape[1]
    acc = jnp.zeros((h.shape[0], D_MODEL), F32)
    for c in range(d_ff // f_chunk):
        sl = slice(c * f_chunk, (c + 1) * f_chunk)
        act = _silu(_dot(h, wg_ref[:, sl])) * _dot(h, wu_ref[:, sl])
        acc = acc + _dot(act.astype(BF16), wd_ref[sl, :])
    o_ref[...] = x_ref[...] + mod_ref[5:6, :] * acc


def _ffn(x1, h2, mods, l, j, wg_bf, wu_bf, wd_bf, seq_len, lat):
    n = x1.shape[0]
    d_ff = wg_bf.shape[2]
    tiles_per_seq = seq_len // TM
    if lat:
        mod_idx = lambda i: (l, 1 + i // tiles_per_seq, 0, 0)
    else:
        mod_idx = lambda i: (l, 0, 0, 0)
    row = lambda i: (i, 0)
    return pl.pallas_call(
        functools.partial(_ffn_kernel, f_chunk=d_ff // 2),
        grid=(n // TM,),
        in_specs=[pl.BlockSpec((TM, D_MODEL), row), pl.BlockSpec((TM, D_MODEL), row),
                  pl.BlockSpec((None, None, 6, D_MODEL), mod_idx),
                  pl.BlockSpec((None, D_MODEL, d_ff), lambda i: (j, 0, 0), pipeline_mode=RESIDENT),
                  pl.BlockSpec((None, D_MODEL, d_ff), lambda i: (j, 0, 0), pipeline_mode=RESIDENT),
                  pl.BlockSpec((None, d_ff, D_MODEL), lambda i: (j, 0, 0), pipeline_mode=RESIDENT)],
        out_specs=pl.BlockSpec((TM, D_MODEL), row),
        out_shape=jax.ShapeDtypeStruct((n, D_MODEL), F32),
        compiler_params=_cparams(("arbitrary",)),
        name="ffn_lat" if lat else "ffn_ctx",
    )(x1, h2, mods, wg_bf, wu_bf, wd_bf)


def _group_layout(rec, counts, n_tiles):
    as_int = lambda k: rec[:, k].astype(jnp.int32)
    cnt = counts[0, :N_EXPERTS].astype(jnp.int32)
    tiles = (cnt + TM_GROUP - 1) // TM_GROUP
    cum = jnp.cumsum(tiles)
    start = (cum - tiles) * TM_GROUP
    experts = jnp.arange(N_EXPERTS, dtype=jnp.int32)

    def pos(e, rank):
        return jnp.sum(jnp.where(e[:, None] == experts[None, :], start[None, :], 0), axis=1) + rank

    pos2 = jnp.stack([pos(as_int(R_E1), as_int(R_RANK1)), pos(as_int(R_E2), as_int(R_RANK2))])
    n_active = cum[N_EXPERTS - 1]
    tile_expert = jnp.sum(jnp.arange(n_tiles, dtype=jnp.int32)[:, None] >= cum[None, :], axis=1)
    tile_expert = jnp.minimum(tile_expert, tile_expert[n_active - 1]).astype(jnp.int32)
    tile_ids = jnp.arange(n_tiles, dtype=jnp.int32)
    group_last = jnp.any((tile_ids[:, None] == cum[None, :] - 1) & (tiles[None, :] > 0), axis=1)
    zero_tile = (group_last | (tile_ids >= n_active)).astype(jnp.int32)
    return (pos2.astype(jnp.int32), tile_expert, n_active.reshape(1).astype(jnp.int32), zero_tile)


def _row_copy(src, dst, sem):
    return pltpu.make_async_copy(src, dst, sem)


def _dispatch_kernel(pos_ref, zt_ref, h_ref, xs_ref, zbuf, sem, zsem):
    n_chunks = h_ref.shape[0] // DMA_CHUNK
    n_tiles = xs_ref.shape[0] // TM_GROUP

    zbuf[...] = jnp.zeros_like(zbuf)

    def zero_fill(j):
        return pltpu.make_async_copy(zbuf, xs_ref.at[pl.ds(j * TM_GROUP, TM_GROUP)], zsem)

    def zero_start(j, carry):
        @pl.when(zt_ref[j] != 0)
        def _():
            zero_fill(j).start()
        return carry

    def zero_wait(j, carry):
        @pl.when(zt_ref[j] != 0)
        def _():
            zero_fill(j).wait()
        return carry

    lax.fori_loop(0, n_tiles, zero_start, 0)
    lax.fori_loop(0, n_tiles, zero_wait, 0)

    def drain_chunk():
        rows = pl.ds(0, 2 * DMA_CHUNK)
        _row_copy(xs_ref.at[rows], xs_ref.at[rows], sem).wait()

    def chunk(k, carry):
        for r in range(DMA_CHUNK):
            t = k * DMA_CHUNK + r
            _row_copy(h_ref.at[t], xs_ref.at[pos_ref[0, t]], sem).start(priority=0)
            _row_copy(h_ref.at[t], xs_ref.at[pos_ref[1, t]], sem).start(priority=1)

        @pl.when(k > 0)
        def _():
            drain_chunk()

        return carry

    lax.fori_loop(0, n_chunks, chunk, 0)
    drain_chunk()


def _dispatch(pos2, zero_tile, h2_rows, n_rows):
    tile = h2_rows.shape[1:]
    return pl.pallas_call(
        _dispatch_kernel,
        grid_spec=pltpu.PrefetchScalarGridSpec(
            num_scalar_prefetch=2, grid=(1,),
            in_specs=[pl.BlockSpec(h2_rows.shape, lambda i, p, z: (0, 0, 0))],
            out_specs=pl.BlockSpec(memory_space=pl.ANY),
            scratch_shapes=[pltpu.VMEM((TM_GROUP,) + tile, F32),
                            pltpu.SemaphoreType.DMA(()), pltpu.SemaphoreType.DMA(())]),
        out_shape=jax.ShapeDtypeStruct((n_rows,) + tile, F32),
        compiler_params=_cparams(("arbitrary",)),
        name="moe_dispatch",
    )(pos2, zero_tile, h2_rows)


def _expert_kernel(te_ref, na_ref, xs_ref, wg_ref, wu_ref, wd_ref, ys_ref,
                   xbuf, ybuf, sem_in, sem_out):
    del te_ref
    i = pl.program_id(0)
    last = pl.num_programs(0) - 1
    n_active = na_ref[0]
    slot = i % 2
    n_col = D_MODEL // LANES

    def in_copies(tile, s):
        rows = pl.ds(tile * TM_GROUP, TM_GROUP)
        return [pltpu.make_async_copy(xs_ref.at[rows, c], xbuf.at[s, :, c * LANES:(c + 1) * LANES],
                                      sem_in.at[s]) for c in range(n_col)]

    def out_copies(tile, s):
        rows = pl.ds(tile * TM_GROUP, TM_GROUP)
        return [pltpu.make_async_copy(ybuf.at[s, :, c * LANES:(c + 1) * LANES], ys_ref.at[rows, c],
                                      sem_out.at[s]) for c in range(n_col)]

    @pl.when(i == 0)
    def _():
        for cp in in_copies(0, 0):
            cp.start()

    @pl.when(i + 1 < n_active)
    def _():
        for cp in in_copies(i + 1, 1 - slot):
            cp.start()

    @pl.when(i >= 2)
    def _():
        for cp in out_copies(i - 2, slot):
            cp.wait()

    @pl.when(i < n_active)
    def _():
        for cp in in_copies(i, slot):
            cp.wait()
        xb = xbuf[slot].astype(BF16)
        act = _silu(_dot(xb, wg_ref[...])) * _dot(xb, wu_ref[...])
        ybuf[slot] = _dot(act.astype(BF16), wd_ref[...])

    @pl.when(i >= n_active)
    def _():
        ybuf[slot] = jnp.zeros((TM_GROUP, D_MODEL), F32)

    for cp in out_copies(i, slot):
        cp.start()

    @pl.when(i == last)
    def _():
        for cp in out_copies(i, slot):
            cp.wait()

        @pl.when(i >= 1)
        def _():
            for cp in out_copies(i - 1, 1 - slot):
                cp.wait()


def _experts(tile_expert, n_active, xs, j, wg_bf, wu_bf, wd_bf):
    n_rows = xs.shape[0]
    d_e = wg_bf.shape[3]
    wspec = lambda shape: pl.BlockSpec((None, None) + shape, lambda i, te, na: (j, te[i], 0, 0))
    return pl.pallas_call(
        _expert_kernel,
        grid_spec=pltpu.PrefetchScalarGridSpec(
            num_scalar_prefetch=2, grid=(n_rows // TM_GROUP,),
            in_specs=[pl.BlockSpec(memory_space=pl.ANY),
                      wspec((D_MODEL, d_e)), wspec((D_MODEL, d_e)), wspec((d_e, D_MODEL))],
            out_specs=pl.BlockSpec(memory_space=pl.ANY),
            scratch_shapes=[pltpu.VMEM((2, TM_GROUP, D_MODEL), F32),
                            pltpu.VMEM((2, TM_GROUP, D_MODEL), F32),
                            pltpu.SemaphoreType.DMA((2,)), pltpu.SemaphoreType.DMA((2,))]),
        out_shape=jax.ShapeDtypeStruct(xs.shape, F32),
        compiler_params=_cparams(("arbitrary",)),
        name="moe_experts",
    )(tile_expert, n_active, xs, wg_bf, wu_bf, wd_bf)


def _combine_kernel(*refs, final):
    pos_ref, y_ref, x_ref, rec_ref, mod_ref = refs[:5]
    g_ref = refs[5] if final else None
    o_ref, a1, a2, b1, b2, sem_a, sem_b = refs[5 + int(final):]
    i = pl.program_id(0)

    def issue(half_idx, d1, d2, sem):
        base = half_idx * DMA_CHUNK

        for r in range(DMA_CHUNK):
            _row_copy(y_ref.at[pos_ref[0, base + r]], d1.at[r], sem).start(priority=0)
            _row_copy(y_ref.at[pos_ref[1, base + r]], d2.at[r], sem).start(priority=1)

    def drain(d1, d2, sem):
        rows = pl.ds(0, DMA_CHUNK)
        _row_copy(y_ref.at[rows], d1, sem).wait()
        _row_copy(y_ref.at[rows], d2, sem).wait()

    def finish(half, d1, d2):
        rows = slice(half * DMA_CHUNK, (half + 1) * DMA_CHUNK)
        w1 = rec_ref[rows, R_W1:R_W1 + 1]
        w2 = rec_ref[rows, R_W2:R_W2 + 1]
        for c in range(D_MODEL // LANES):
            sl = slice(c * LANES, (c + 1) * LANES)
            y = w1 * d1[:, c, :] + w2 * d2[:, c, :]
            o_ref[rows, sl] = x_ref[rows, sl] + mod_ref[5:6, sl] * y
        if final:
            o_ref[rows, :] = _rms(o_ref[rows, :], g_ref[...])

    @pl.when(i == 0)
    def _():
        issue(0, a1, a2, sem_a)

    issue(2 * i + 1, b1, b2, sem_b)
    drain(a1, a2, sem_a)
    finish(0, a1, a2)

    @pl.when(i + 1 < pl.num_programs(0))
    def _():
        issue(2 * i + 2, a1, a2, sem_a)

    drain(b1, b2, sem_b)
    finish(1, b1, b2)


def _combine(pos2, ys, x1, rec, mods, l, seq_len, lat, final_g):
    n = x1.shape[0]
    final = final_g is not None
    tile = ys.shape[1:]
    tm = 2 * DMA_CHUNK
    tiles_per_seq = max(seq_len // tm, 1)
    if lat:
        mod_idx = lambda i, p: (l, 1 + i // tiles_per_seq, 0, 0)
    else:
        mod_idx = lambda i, p: (l, 0, 0, 0)
    row = lambda i, p: (i, 0)
    slot = pltpu.VMEM((DMA_CHUNK,) + tile, F32)
    in_specs = [pl.BlockSpec(memory_space=pl.ANY),
                pl.BlockSpec((tm, D_MODEL), row),
                pl.BlockSpec((tm, LANES), row),
                pl.BlockSpec((None, None, 6, D_MODEL), mod_idx)]
    args = [pos2, ys, x1, rec, mods]
    if final:
        in_specs.append(pl.BlockSpec((1, D_MODEL), lambda i, p: (0, 0)))
        args.append(final_g)
    return pl.pallas_call(
        functools.partial(_combine_kernel, final=final),
        grid_spec=pltpu.PrefetchScalarGridSpec(
            num_scalar_prefetch=1, grid=(n // tm,),
            in_specs=in_specs,
            out_specs=pl.BlockSpec((tm, D_MODEL), row),
            scratch_shapes=[slot, slot, slot, slot,
                            pltpu.SemaphoreType.DMA(()), pltpu.SemaphoreType.DMA(())]),
        out_shape=jax.ShapeDtypeStruct((n, D_MODEL), F32),
        compiler_params=_cparams(("arbitrary",)),
        name="moe_combine_final" if final else "moe_combine",
    )(*args)


def _moe(x1, h2_rows, rec, counts, mods, l, j, wg_bf, wu_bf, wd_bf, seq_len, lat, final_g):
    n = x1.shape[0]
    n_tiles = 2 * n // TM_GROUP + N_EXPERTS
    pos2, tile_expert, n_active, zero_tile = _group_layout(rec, counts, n_tiles)
    xs = _dispatch(pos2, zero_tile, h2_rows, n_tiles * TM_GROUP)
    ys = _experts(tile_expert, n_active, xs, j, wg_bf, wu_bf, wd_bf)
    return _combine(pos2, ys, x1, rec, mods, l, seq_len, lat, final_g)


def _final_norm_kernel(x_ref, g_ref, o_ref):
    o_ref[...] = _rms(x_ref[...], g_ref[...])


def _final_norm(x, g):
    n = x.shape[0]
    row = lambda i: (i, 0)
    return pl.pallas_call(
        _final_norm_kernel,
        grid=(n // TM_FF,),
        in_specs=[pl.BlockSpec((TM_FF, D_MODEL), row), pl.BlockSpec((1, D_MODEL), lambda i: (0, 0))],
        out_specs=pl.BlockSpec((TM_FF, D_MODEL), row),
        out_shape=jax.ShapeDtypeStruct((n, D_MODEL), F32),
        compiler_params=_cparams(("arbitrary",)),
        name="final_norm",
    )(x, g)


def _rope_tables(t_len):
    t = jnp.arange(t_len)
    pos = jnp.stack([t // GRID_W, t % GRID_W], axis=-1).astype(F32)
    inv = ROPE_THETA ** (-jnp.arange(0, AXIS_DIM, 2, dtype=F32) / AXIS_DIM)
    ang = pos[..., None] * inv
    cos, sin = jnp.cos(ang), jnp.sin(ang)
    c = jnp.concatenate([cos[:, 0], cos[:, 0], cos[:, 1], cos[:, 1]], axis=-1)
    s = jnp.concatenate([-sin[:, 0], sin[:, 0], -sin[:, 1], sin[:, 1]], axis=-1)
    return c, s


def _gate_weights(wa, wi):
    depth = wa.shape[0]
    groups = D_RNN // LANES

    def pair(w):
        w = w.reshape(depth, 2, groups, 2, RNN_BLOCK, RNN_BLOCK)
        z = jnp.zeros_like(w[:, :, :, 0])
        top = jnp.concatenate([w[:, :, :, 0], z], axis=-1)
        bot = jnp.concatenate([z, w[:, :, :, 1]], axis=-1)
        return jnp.concatenate([top, bot], axis=-2)

    w = jnp.concatenate([pair(wa), pair(wi)], axis=-1)
    return (0.5 * jnp.concatenate([w[:, 0], w[:, 1]], axis=-1)).astype(BF16)


def kernel(x_prompt, x_sample, cache_k, cache_v, state_rnn, c, c_ctx, w_mod, b_mod, norm1, norm2, w_in, conv_w, conv_b, lru_wa, lru_ba, lru_wi, lru_bi, lru_lambda, q_norm, k_norm, w_rnn_out, w_attn_out, w_out, ff_gate, ff_up, ff_down, router_w, router_b, exp_gate, exp_up, exp_down, final_norm):
    batch, seq, _ = x_prompt.shape
    dec_batch, dec_seq, _ = x_sample.shape
    depth = w_mod.shape[0]
    past = cache_k.shape[2]
    assert dec_batch + 1 <= MOD_ROWS

    cond8 = jnp.zeros((MOD_ROWS, D_MODEL), F32).at[0].set(c_ctx).at[1:1 + dec_batch].set(c)
    mods = _modulation(cond8, w_mod, b_mod).reshape(depth, MOD_ROWS, 6, D_MODEL)

    gate_w = _gate_weights(lru_wa, lru_wi)
    n_moe = router_w.shape[0]
    wr_pad = jnp.zeros((n_moe, D_MODEL, LANES), F32).at[:, :, :N_EXPERTS].set(router_w)
    wr_head = wr_pad.astype(BF16)
    wr_t = jnp.concatenate([wr_head, (wr_pad - wr_head.astype(F32)).astype(BF16)], axis=-1)
    br_pad = jnp.zeros((n_moe, 1, LANES), F32).at[:, 0, :N_EXPERTS].set(router_b)

    norm1_3 = norm1.reshape(depth, 1, D_MODEL)
    norm2_3 = norm2.reshape(depth, 1, D_MODEL)
    qn3 = q_norm.reshape(depth, 1, HEAD_DIM)
    kn3 = k_norm.reshape(depth, 1, HEAD_DIM)
    conv_b3 = conv_b.reshape(depth, 1, D_RNN)
    ba4 = (0.5 * lru_ba).reshape(depth, 2, 1, D_RNN)
    bi4 = (0.5 * lru_bi).reshape(depth, 2, 1, D_RNN)
    lam4 = lru_lambda.reshape(depth, 2, 1, D_RNN)
    state5 = state_rnn.reshape(dec_batch, depth, 2, 1, D_RNN)
    cache_k4 = cache_k.reshape(dec_batch, depth, past, KV_DIM)
    cache_v4 = cache_v.reshape(dec_batch, depth, past, KV_DIM)
    rope_tabs = _rope_tables(dec_seq)
    final_g = final_norm.reshape(1, D_MODEL)

    n_steps = dec_batch * dec_seq // TQ
    assert n_steps == dec_batch * dec_seq // TM_ROPE

    def flat(w):
        return w.reshape(-1, w.shape[-1])

    def channel_names(l):
        return ("exp_gate", "exp_up", "exp_down") if l % 2 == 1 else ("ff_gate", "ff_up", "ff_down")

    stacked = dict(w_in=w_in, w_rnn_out=w_rnn_out, w_attn_out=w_attn_out, w_out=w_out,
                   ff_gate=ff_gate, ff_up=ff_up, ff_down=ff_down,
                   exp_gate=exp_gate, exp_up=exp_up, exp_down=exp_down)

    def layer_shape(name):
        return (1,) + stacked[name].shape[1:]

    def layer_index(name, l):
        return l if name.startswith("w_") else l // 2

    def rider(name, l, steps):
        w = stacked[name]
        rows = flat(w).shape[0] // w.shape[0] // steps
        return _Rider(flat(w), rows, layer_index(name, l) * steps)

    def names_of(l):
        return ("w_in", "w_rnn_out", "w_attn_out", "w_out") + channel_names(l)

    weights = {l: {} for l in range(depth)}

    def keep(l, names, arrays):
        for name, arr in zip(names, arrays):
            weights[l][name] = arr.reshape(layer_shape(name))

    keep(0, ("w_in",), _convert([rider("w_in", 0, n_steps)], n_steps))
    ctx_steps = batch * seq // TM

    def run_layer(x, l, lat):
        seq_len = dec_seq if lat else seq
        wts = weights[l]
        convert_next = lat and l + 1 < depth
        if convert_next:
            proj_names = names_of(l + 1)[:4]
            attn_names = names_of(l + 1)[4:]
            proj_riders = [rider(name, l + 1, n_steps) for name in proj_names]
            attn_riders = [rider(name, l + 1, n_steps) for name in attn_names]
            target = l + 1
        elif l == 0 and not lat:
            proj_names, attn_names, attn_riders, target = names_of(0)[1:], (), [], 0
            proj_riders = [rider(name, 0, ctx_steps) for name in proj_names]
        else:
            proj_names, attn_names, proj_riders, attn_riders, target = (), (), [], [], None
        outs = _in_proj(x, mods, l, norm1_3, wts["w_in"], 0, qn3, kn3,
                        rope_tabs if lat else None, seq_len, riders=proj_riders,
                        caches=None if lat else new_kv)
        xr, yr, q, _, kr, v, gr, ga = outs[:8]
        if target is not None:
            keep(target, proj_names, outs[8:8 + len(proj_names)])
        if not lat:
            new_kv[:] = outs[8 + len(proj_names):]
        out_r, end_f, end_b = _rnn(xr, yr, l, conv_w, conv_b3, gate_w, ba4, bi4, lam4,
                                   state5 if lat else None, seq_len)
        if lat:
            outs = _attn_lat(q, kr, v, cache_k4, cache_v4, l, seq_len, riders=attn_riders)
            out_a = outs[0]
            if target is not None:
                keep(target, attn_names, outs[1:])
        else:
            out_a = _attn_ctx(q, kr, v, seq_len)
        mix_args = (x, out_r, out_a, gr, ga, mods, l, norm2_3, wts["w_rnn_out"],
                    wts["w_attn_out"], wts["w_out"], 0, seq_len, lat)
        if l % 2 == 1:
            x1, h2_rows, rec, counts = _mix_out(*mix_args, router=(l // 2, wr_t, br_pad))
            x = _moe(x1, h2_rows, rec, counts, mods, l, 0, wts["exp_gate"], wts["exp_up"],
                     wts["exp_down"], seq_len, lat, final_g if l == depth - 1 else None)
        else:
            x1, h2 = _mix_out(*mix_args)
            x = _ffn(x1, h2, mods, l, 0, wts["ff_gate"], wts["ff_up"], wts["ff_down"],
                     seq_len, lat)
            if l == depth - 1:
                x = _final_norm(x, final_g)
        return x, jnp.stack([end_f, end_b], axis=1)

    new_kv = [jnp.zeros((batch, depth, seq, N_KV_HEADS, HEAD_DIM), F32) for _ in range(2)]
    y_ctx = x_prompt.reshape(batch * seq, D_MODEL)
    y_lat = x_sample.reshape(dec_batch * dec_seq, D_MODEL)
    ss = []
    for l in range(depth):
        y_ctx, st = run_layer(y_ctx, l, False)
        ss.append(st)
        y_lat, _ = run_layer(y_lat, l, True)

    new_state = jnp.stack(ss, axis=1)
    return (y_ctx.reshape(batch, seq, D_MODEL), y_lat.reshape(dec_batch, dec_seq, D_MODEL),
            new_kv[0], new_kv[1], new_state)
```

```python
import functools

import jax
import jax.numpy as jnp
from jax import lax
from jax.experimental import pallas as pl
from jax.experimental.pallas import tpu as pltpu

F32 = jnp.float32
BF16 = jnp.bfloat16

D_MODEL = 1024
D_RNN = 1024
N_RNN_BLOCKS = 16
RNN_BLOCK = D_RNN // N_RNN_BLOCKS
CONV_W = 4
CONV_LEFT = 2
LRU_C = 8.0
N_HEADS = 8
N_KV_HEADS = 2
HEAD_DIM = 128
AXIS_DIM = HEAD_DIM // 2
ROPE_THETA = 10000.0
GRID_W = 64
Q_DIM = N_HEADS * HEAD_DIM
KV_DIM = N_KV_HEADS * HEAD_DIM
N_EXPERTS = 8
EPS = 1e-6
LOG2_E = 1.4426950408889634
RSQRT_FLOOR = 1e-30
D_IN = 2 * D_RNN + Q_DIM + 2 * KV_DIM + 2 * D_MODEL
O_XR, O_YR, O_Q = 0, D_RNN, 2 * D_RNN
O_K = O_Q + Q_DIM
O_V = O_K + KV_DIM
O_GR = O_V + KV_DIM
O_GA = O_GR + D_MODEL

LANES = 128
SUBLANES = 8
MOD_ROWS = 8
VMEM_LIMIT = 56 * 1024 * 1024

TM = 512
TM_ROPE = 256
RESIDENT = pl.Buffered(1)
RNN_T = 256
RNN_SEG = SUBLANES
RNN_C = 512
RNN_TCH = 32
RNN_SCAN_UNROLL = 8
TQ = 256
ATTN_CTX_SEQS = 2
TM_FF = 512
TM_GROUP = 256
DMA_CHUNK = 256


def _cparams(sem):
    return pltpu.CompilerParams(dimension_semantics=sem, vmem_limit_bytes=VMEM_LIMIT)


def _sigmoid(x):
    return 1.0 / (1.0 + jnp.exp(-x))


def _silu(x):
    return x * _sigmoid(x)


def _gelu_tanh(x):
    return 0.5 * x * (1.0 + jnp.tanh(0.7978845608028654 * (x + 0.044715 * (x * x * x))))


def _rms(x, g):
    ms = jnp.mean(x * x, axis=-1, keepdims=True)
    return x * lax.rsqrt(ms + EPS) * g


def _dot(a, b):
    return jnp.dot(a, b, preferred_element_type=F32)


MOD_TN = 1536


def _mod_kernel(*refs, n_riders):
    cond_ref, w_ref, b_ref = refs[:3]
    o_ref = refs[3 + n_riders]
    _convert_riders(refs[3:3 + n_riders], refs[4 + n_riders:])
    s = _silu(cond_ref[...]).astype(BF16)
    o_ref[...] = _dot(s, w_ref[...].astype(BF16)) + b_ref[...]


def _modulation_steps(depth):
    return depth * (6 * D_MODEL // MOD_TN)


def _modulation(cond8, w_mod, b_mod, riders=()):
    depth = w_mod.shape[0]
    per_layer = 6 * D_MODEL // MOD_TN
    r_in, r_out, r_shape = _rider_specs(riders, depth * per_layer, lambda l, j: l * per_layer + j)
    return pl.pallas_call(
        functools.partial(_mod_kernel, n_riders=len(riders)),
        grid=(depth, per_layer),
        in_specs=[
            pl.BlockSpec((MOD_ROWS, D_MODEL), lambda l, j: (0, 0)),
            pl.BlockSpec((None, D_MODEL, MOD_TN), lambda l, j: (l, 0, j)),
            pl.BlockSpec((None, 1, MOD_TN), lambda l, j: (l, 0, j)),
        ] + r_in,
        out_specs=[pl.BlockSpec((None, MOD_ROWS, MOD_TN), lambda l, j: (l, 0, j))] + r_out,
        out_shape=[jax.ShapeDtypeStruct((depth, MOD_ROWS, 6 * D_MODEL), F32)] + r_shape,
        compiler_params=_cparams(("arbitrary", "arbitrary")),
        name="adaln_mod",
    )(cond8, w_mod, b_mod.reshape(depth, 1, 6 * D_MODEL), *[r.w2d for r in riders])


def _swap_halves(x):
    lane = lax.broadcasted_iota(jnp.int32, x.shape, 1)
    return jnp.where((lane % AXIS_DIM) < AXIS_DIM // 2,
                     pltpu.roll(x, LANES - AXIS_DIM // 2, axis=1),
                     pltpu.roll(x, AXIS_DIM // 2, axis=1))


class _Rider:
    def __init__(self, w2d, rows, first):
        self.w2d, self.rows, self.first = w2d, rows, first


def _rider_specs(riders, n_steps, step_of):
    in_specs, out_specs, out_shape = [], [], []
    for r in riders:
        cols = r.w2d.shape[1]
        assert r.rows % 16 == 0 and (r.first + n_steps) * r.rows <= r.w2d.shape[0]
        in_specs.append(pl.BlockSpec((r.rows, cols), lambda *g, r=r: (r.first + step_of(*g), 0)))
        out_specs.append(pl.BlockSpec((r.rows, cols), lambda *g: (step_of(*g), 0)))
        out_shape.append(jax.ShapeDtypeStruct((r.rows * n_steps, cols), BF16))
    return in_specs, out_specs, out_shape


def _convert_riders(in_refs, out_refs):
    for w_ref, o_ref in zip(in_refs, out_refs):
        o_ref[...] = w_ref[...].astype(BF16)


def _in_proj_kernel(*refs, rope, n_riders, cache_layer, seq_len):
    n_in = 8 if rope else 6
    n_cache = 0 if cache_layer is None else 2
    first_out = n_in + n_riders + n_cache
    _convert_riders(refs[n_in:n_in + n_riders], refs[first_out + 8:first_out + 8 + n_riders])
    cache_refs = refs[first_out + 8 + n_riders:]
    refs = refs[:n_in] + refs[first_out:first_out + 8]
    if rope:
        (x_ref, mod_ref, n1_ref, w_ref, qn_ref, kn_ref, c_ref, s_ref,
         xr_ref, yr_ref, q_ref, kno_ref, kro_ref, v_ref, gr_ref, ga_ref) = refs
    else:
        (x_ref, mod_ref, n1_ref, w_ref, qn_ref, kn_ref,
         xr_ref, yr_ref, q_ref, kno_ref, kro_ref, v_ref, gr_ref, ga_ref) = refs
    x = x_ref[...]
    h = _rms(x, n1_ref[...]) * (1.0 + mod_ref[1:2, :]) + mod_ref[0:1, :]
    hb = h.astype(BF16)

    xr_ref[...] = _dot(hb, w_ref[:, O_XR:O_XR + D_RNN])
    yr_ref[...] = _dot(hb, w_ref[:, O_YR:O_YR + D_RNN])
    gr_ref[...] = _dot(hb, w_ref[:, O_GR:O_GR + D_MODEL]).astype(BF16)
    ga_ref[...] = _dot(hb, w_ref[:, O_GA:O_GA + D_MODEL]).astype(BF16)
    v_ref[...] = _dot(hb, w_ref[:, O_V:O_V + KV_DIM])

    if rope:
        cs, sn = c_ref[...], s_ref[...]
    zq = _dot(hb, w_ref[:, O_Q:O_Q + Q_DIM])
    for hd in range(N_HEADS):
        t = _rms(zq[:, hd * HEAD_DIM:(hd + 1) * HEAD_DIM], qn_ref[...])
        if rope:
            t = t * cs + _swap_halves(t) * sn
        q_ref[:, hd * HEAD_DIM:(hd + 1) * HEAD_DIM] = t.astype(BF16)
    zk = _dot(hb, w_ref[:, O_K:O_K + KV_DIM])
    for hd in range(N_KV_HEADS):
        t = _rms(zk[:, hd * HEAD_DIM:(hd + 1) * HEAD_DIM], kn_ref[...])
        kno_ref[:, hd * HEAD_DIM:(hd + 1) * HEAD_DIM] = t
        if rope:
            t = t * cs + _swap_halves(t) * sn
        kro_ref[:, hd * HEAD_DIM:(hd + 1) * HEAD_DIM] = t.astype(BF16)

    if cache_layer is None:
        return
    newk_ref, newv_ref, kbuf, vbuf, csem = cache_refs
    step = pl.program_id(0)
    slot = step % 2
    seqs_per_tile = x_ref.shape[0] // seq_len

    def cache_copies(k, s):
        cps = []
        for sq in range(seqs_per_tile):
            rows = slice(sq * seq_len, (sq + 1) * seq_len)
            for hd in range(N_KV_HEADS):
                cols = slice(hd * HEAD_DIM, (hd + 1) * HEAD_DIM)
                dst = (k * seqs_per_tile + sq, cache_layer, slice(None), hd, slice(None))
                cps.append(pltpu.make_async_copy(kbuf.at[s, rows, cols], newk_ref.at[dst], csem.at[s]))
                cps.append(pltpu.make_async_copy(vbuf.at[s, rows, cols], newv_ref.at[dst], csem.at[s]))
        return cps

    @pl.when(step >= 2)
    def _():
        for cp in cache_copies(step - 2, slot):
            cp.wait()

    kbuf[slot] = kno_ref[...]
    vbuf[slot] = v_ref[...]
    for cp in cache_copies(step, slot):
        cp.start()

    @pl.when(step == pl.num_programs(0) - 1)
    def _():
        for cp in cache_copies(step, slot):
            cp.wait()

        @pl.when(step >= 1)
        def _():
            for cp in cache_copies(step - 1, 1 - slot):
                cp.wait()


def _in_proj(x, mods, l, norm1, w_in_bf, wl, q_norm, k_norm, rope_tabs, seq_len, riders=(),
             caches=None):
    n = x.shape[0]
    rope = rope_tabs is not None
    tm = TM_ROPE if rope else TM
    tiles_per_seq = seq_len // tm
    if rope:
        mod_idx = lambda i: (l, 1 + i // tiles_per_seq, 0, 0)
    else:
        mod_idx = lambda i: (l, 0, 0, 0)
    row = lambda i: (i, 0)
    in_specs = [
        pl.BlockSpec((tm, D_MODEL), row),
        pl.BlockSpec((None, None, 6, D_MODEL), mod_idx),
        pl.BlockSpec((None, 1, D_MODEL), lambda i: (l, 0, 0)),
        pl.BlockSpec((None, D_MODEL, D_IN), lambda i: (wl, 0, 0), pipeline_mode=RESIDENT),
        pl.BlockSpec((None, 1, HEAD_DIM), lambda i: (l, 0, 0)),
        pl.BlockSpec((None, 1, HEAD_DIM), lambda i: (l, 0, 0)),
    ]
    args = [x, mods, norm1, w_in_bf, q_norm, k_norm]
    if rope:
        in_specs += [pl.BlockSpec((tm, HEAD_DIM), lambda i: (i % tiles_per_seq, 0))] * 2
        args += list(rope_tabs)
    widths = [(D_RNN, F32), (D_RNN, F32), (Q_DIM, BF16), (KV_DIM, F32), (KV_DIM, BF16),
              (KV_DIM, F32), (D_MODEL, BF16), (D_MODEL, BF16)]
    r_in, r_out, r_shape = _rider_specs(riders, n // tm, lambda i: i)
    in_specs = in_specs + r_in
    out_specs = [pl.BlockSpec((tm, w), row) for w, _ in widths] + r_out
    out_shape = [jax.ShapeDtypeStruct((n, w), dt) for w, dt in widths] + r_shape
    args = args + [r.w2d for r in riders]
    scratch, aliases = [], {}
    if caches is not None:
        assert tm % seq_len == 0
        for c in caches:
            aliases[len(args)] = len(out_shape)
            args.append(c)
            in_specs.append(pl.BlockSpec(memory_space=pl.ANY))
            out_specs.append(pl.BlockSpec(memory_space=pl.ANY))
            out_shape.append(jax.ShapeDtypeStruct(c.shape, c.dtype))
        scratch = [pltpu.VMEM((2, tm, KV_DIM), F32), pltpu.VMEM((2, tm, KV_DIM), F32),
                   pltpu.SemaphoreType.DMA((2,))]
    return pl.pallas_call(
        functools.partial(_in_proj_kernel, rope=rope, n_riders=len(riders),
                          cache_layer=None if caches is None else l, seq_len=seq_len),
        grid=(n // tm,),
        in_specs=in_specs,
        out_specs=out_specs,
        out_shape=out_shape,
        scratch_shapes=scratch,
        input_output_aliases=aliases,
        compiler_params=_cparams(("arbitrary",)),
        name="in_proj_lat" if rope else "in_proj_ctx",
    )(*args)


def _rnn_kernel(*refs, chained):
    refs = list(refs)
    xr_ref, yr_ref, cw_ref, cb_ref, gw_ref, ba_ref, bi_ref, lam_ref = refs[:8]
    refs = refs[8:]
    h0_ref = refs.pop(0) if chained else None
    out_ref, sf_ref, sb_ref, xext, ybuf, obuf, a_f, b_f, a_b, b_b, sem_in, sem_out = refs

    g = pl.program_id(0)
    cb = pl.program_id(1)
    n_cb = pl.num_programs(1)
    step = g * n_cb + cb
    n_steps = pl.num_programs(0) * n_cb
    slot = step % 2

    def hbm_window(kk, s):
        rows = pl.ds(pl.multiple_of(((kk // n_cb) * RNN_SEG + s) * RNN_T, RNN_T), RNN_T)
        cols = pl.ds(pl.multiple_of((kk % n_cb) * RNN_C, RNN_C), RNN_C)
        return rows, cols

    def in_copies(kk, sl_):
        cps = []
        for s in range(RNN_SEG):
            win = hbm_window(kk, s)
            cps.append(pltpu.make_async_copy(
                xr_ref.at[win], xext.at[sl_, pl.ds(CONV_LEFT, RNN_T), s], sem_in.at[sl_]))
            cps.append(pltpu.make_async_copy(yr_ref.at[win], ybuf.at[sl_, :, s], sem_in.at[sl_]))
        return cps

    def out_copies(kk, sl_):
        return [pltpu.make_async_copy(obuf.at[sl_, :, s], out_ref.at[hbm_window(kk, s)],
                                      sem_out.at[sl_]) for s in range(RNN_SEG)]

    @pl.when(step == 0)
    def _():
        for cp in in_copies(0, 0):
            cp.start()

    @pl.when(step + 1 < n_steps)
    def _():
        for cp in in_copies(step + 1, 1 - slot):
            cp.start()

    for cp in in_copies(step, slot):
        cp.wait()

    seg = lax.broadcasted_iota(jnp.int32, (RNN_SEG, RNN_C), 0)
    zero_row = jnp.zeros((RNN_SEG, RNN_C), F32)
    if chained:
        xext[slot, 0] = jnp.where(seg == 0, zero_row, pltpu.roll(xext[slot, RNN_T], 1, axis=0))
        xext[slot, 1] = jnp.where(seg == 0, zero_row, pltpu.roll(xext[slot, RNN_T + 1], 1, axis=0))
        xext[slot, RNN_T + CONV_LEFT] = jnp.where(
            seg == RNN_SEG - 1, zero_row, pltpu.roll(xext[slot, CONV_LEFT], RNN_SEG - 1, axis=0))
    else:
        xext[slot, 0] = zero_row
        xext[slot, 1] = zero_row
        xext[slot, RNN_T + CONV_LEFT] = zero_row

    scans = ((a_f, b_f), (a_b, b_b))
    rows2d = RNN_TCH * RNN_SEG

    def gate_chunk(tc, carry):
        t0 = pl.multiple_of(tc * RNN_TCH, RNN_TCH)
        for j in range(RNN_C // LANES):
            sl = slice(j * LANES, (j + 1) * LANES)
            xc = jnp.broadcast_to(cb_ref[:, sl].reshape(1, 1, LANES), (RNN_TCH, RNN_SEG, LANES))
            for k in range(CONV_W):
                xc = xc + (xext[slot, pl.ds(t0 + k, RNN_TCH), :, sl]
                           * cw_ref[k:k + 1, sl].reshape(1, 1, LANES))
            xc = xc.reshape(rows2d, LANES)
            pre = _dot(xc.astype(BF16), gw_ref[j])
            for d, (a_s, b_s) in enumerate(scans):
                lam = lam_ref[d, :, sl]
                log2_decay = (-LRU_C * LOG2_E) * (jnp.maximum(-lam, 0.0)
                                                  + jnp.log(1.0 + jnp.exp(-jnp.abs(lam))))
                r = 0.5 * jnp.tanh(pre[:, 2 * d * LANES:(2 * d + 1) * LANES]
                                   + ba_ref[d, :, sl]) + 0.5
                gi = 0.5 * jnp.tanh(pre[:, (2 * d + 1) * LANES:(2 * d + 2) * LANES]
                                    + bi_ref[d, :, sl]) + 0.5
                a = jnp.exp2(r * log2_decay)
                y = 1.0 - a * a
                b = (y * lax.rsqrt(jnp.maximum(y, RSQRT_FLOOR))) * (gi * xc)
                a_s[pl.ds(t0, RNN_TCH), :, sl] = a.reshape(RNN_TCH, RNN_SEG, LANES)
                b_s[pl.ds(t0, RNN_TCH), :, sl] = b.reshape(RNN_TCH, RNN_SEG, LANES)
        return carry

    lax.fori_loop(0, RNN_T // RNN_TCH, gate_chunk, 0)

    def scan_step(t, carry):
        hf, hb, pf, pb = carry
        tb = RNN_T - 1 - t
        af_t = a_f[t]
        hf = af_t * hf + b_f[t]
        b_f[t] = hf
        ab_t = a_b[tb]
        hb = ab_t * hb + b_b[tb]
        b_b[tb] = hb
        if chained:
            pf = af_t * pf
            a_f[t] = pf
            pb = ab_t * pb
            a_b[tb] = pb
        return hf, hb, pf, pb

    ones = jnp.ones((RNN_SEG, RNN_C), F32)
    hf, hb, pf, pb = lax.fori_loop(0, RNN_T, scan_step, (zero_row, zero_row, ones, ones),
                                   unroll=RNN_SCAN_UNROLL)

    if chained:
        st_f = jnp.broadcast_to(h0_ref[0], (RNN_SEG, RNN_C))
        st_b = jnp.broadcast_to(h0_ref[1], (RNN_SEG, RNN_C))
        in_f, in_b = st_f, st_b
        for _ in range(RNN_SEG - 1):
            in_f = jnp.where(seg == 0, st_f, pltpu.roll(pf * in_f + hf, 1, axis=0))
            in_b = jnp.where(seg == RNN_SEG - 1, st_b,
                             pltpu.roll(pb * in_b + hb, RNN_SEG - 1, axis=0))
        sf_ref[...] = pf * in_f + hf
        sb_ref[...] = pb * in_b + hb
    else:
        sf_ref[...] = hf
        sb_ref[...] = hb

    @pl.when(step >= 2)
    def _():
        for cp in out_copies(step - 2, slot):
            cp.wait()

    def finish_chunk(tc, carry):
        rows = pl.ds(pl.multiple_of(tc * RNN_TCH, RNN_TCH), RNN_TCH)
        h_f, h_b = b_f[rows], b_b[rows]
        if chained:
            h_f = a_f[rows] * in_f.reshape(1, RNN_SEG, RNN_C) + h_f
            h_b = a_b[rows] * in_b.reshape(1, RNN_SEG, RNN_C) + h_b
        obuf[slot, rows] = (h_f + h_b) * _gelu_tanh(ybuf[slot, rows])
        return carry

    lax.fori_loop(0, RNN_T // RNN_TCH, finish_chunk, 0)

    for cp in out_copies(step, slot):
        cp.start()

    @pl.when(step == n_steps - 1)
    def _():
        for cp in out_copies(step, slot):
            cp.wait()

        @pl.when(step >= 1)
        def _():
            for cp in out_copies(step - 1, 1 - slot):
                cp.wait()


def _rnn(xr, yr, l, conv_w, conv_b, gate_w, lru_ba, lru_bi, lru_lambda, state5, seq_len):
    n = xr.shape[0]
    chained = state5 is not None
    group_len = RNN_SEG * RNN_T
    assert n % group_len == 0 and seq_len in ((group_len,) if chained else (RNN_T,))
    n_groups = n // group_len
    lanes_per_step = RNN_C // LANES
    vec = pl.BlockSpec((None, 2, 1, RNN_C), lambda g, c: (l, 0, 0, c))
    in_specs = [
        pl.BlockSpec(memory_space=pl.ANY),
        pl.BlockSpec(memory_space=pl.ANY),
        pl.BlockSpec((None, CONV_W, RNN_C), lambda g, c: (l, 0, c)),
        pl.BlockSpec((None, 1, RNN_C), lambda g, c: (l, 0, c)),
        pl.BlockSpec((None, lanes_per_step, LANES, 4 * LANES), lambda g, c: (l, c, 0, 0)),
        vec, vec, vec,
    ]
    args = [xr, yr, conv_w, conv_b, gate_w, lru_ba, lru_bi, lru_lambda]
    if chained:
        in_specs.append(pl.BlockSpec((None, None, 2, 1, RNN_C), lambda g, c: (g, l, 0, 0, c)))
        args.append(state5)
    state_spec = pl.BlockSpec((RNN_SEG, RNN_C), lambda g, c: (g, c))
    state_shape = jax.ShapeDtypeStruct((n_groups * RNN_SEG, D_RNN), F32)
    work = lambda t: pltpu.VMEM((t, RNN_SEG, RNN_C), F32)
    return pl.pallas_call(
        functools.partial(_rnn_kernel, chained=chained),
        grid=(n_groups, D_RNN // RNN_C),
        in_specs=in_specs,
        out_specs=[pl.BlockSpec(memory_space=pl.ANY), state_spec, state_spec],
        out_shape=[jax.ShapeDtypeStruct((n, D_RNN), F32), state_shape, state_shape],
        scratch_shapes=[
            pltpu.VMEM((2, RNN_T + CONV_W - 1, RNN_SEG, RNN_C), F32),
            pltpu.VMEM((2, RNN_T, RNN_SEG, RNN_C), F32),
            pltpu.VMEM((2, RNN_T, RNN_SEG, RNN_C), F32),
            work(RNN_T), work(RNN_T), work(RNN_T), work(RNN_T),
            pltpu.SemaphoreType.DMA((2,)), pltpu.SemaphoreType.DMA((2,)),
        ],
        compiler_params=_cparams(("arbitrary", "arbitrary")),
        name="rnn_lat" if chained else "rnn_ctx",
    )(*args)


def _attend_heads(q_ref, k_all, v_all, o_ref, stack_heads):
    exp2_scale = (1.0 / (HEAD_DIM ** 0.5)) * LOG2_E
    group = N_HEADS // N_KV_HEADS
    tq = q_ref.shape[0]
    for kh in range(N_KV_HEADS):
        k = k_all[:, kh * HEAD_DIM:(kh + 1) * HEAD_DIM]
        v = v_all[:, kh * HEAD_DIM:(kh + 1) * HEAD_DIM]
        heads = [slice((kh * group + g) * HEAD_DIM, (kh * group + g + 1) * HEAD_DIM)
                 for g in range(group)]
        stacks = [heads] if stack_heads else [[sl] for sl in heads]
        for stack in stacks:
            q = jnp.concatenate([q_ref[:, sl] for sl in stack], axis=0)
            s = lax.dot_general(q, k, (((1,), (1,)), ((), ())), preferred_element_type=F32)
            m = jnp.max(s, axis=-1, keepdims=True)
            p = jnp.exp2((s - m) * exp2_scale)
            denom = jnp.sum(p, axis=-1, keepdims=True)
            o = (_dot(p.astype(BF16), v) / denom).astype(BF16)
            for g, sl in enumerate(stack):
                o_ref[:, sl] = o[g * tq:(g + 1) * tq]


def _attn_ctx_kernel(q_ref, k_ref, v_ref, o_ref, *, seq_len):
    for s in range(q_ref.shape[0] // seq_len):
        rows = pl.ds(s * seq_len, seq_len)
        _attend_heads(q_ref.at[rows], k_ref[rows, :], v_ref[rows, :].astype(BF16), o_ref.at[rows],
                      stack_heads=True)


def _attn_ctx(q, kr, v, seq_len):
    n = q.shape[0]
    row = lambda b: (b, 0)
    tile = ATTN_CTX_SEQS * seq_len
    return pl.pallas_call(
        functools.partial(_attn_ctx_kernel, seq_len=seq_len),
        grid=(n // tile,),
        in_specs=[pl.BlockSpec((tile, Q_DIM), row),
                  pl.BlockSpec((tile, KV_DIM), row),
                  pl.BlockSpec((tile, KV_DIM), row)],
        out_specs=pl.BlockSpec((tile, Q_DIM), row),
        out_shape=jax.ShapeDtypeStruct((n, Q_DIM), BF16),
        compiler_params=_cparams(("arbitrary",)),
        name="attn_ctx",
    )(q, kr, v)


def _attn_lat_kernel(*refs, past, n_riders):
    _convert_riders(refs[5:5 + n_riders], refs[6 + n_riders:6 + 2 * n_riders])
    q_ref, kc_ref, vc_ref, kl_ref, vl_ref = refs[:5]
    o_ref = refs[5 + n_riders]
    k_all, v_all = refs[6 + 2 * n_riders:]

    @pl.when(pl.program_id(1) == 0)
    def _():
        k_all[0:past, :] = kc_ref[...].astype(BF16)
        k_all[past:, :] = kl_ref[...]
        v_all[0:past, :] = vc_ref[...].astype(BF16)
        v_all[past:, :] = vl_ref[...].astype(BF16)

    _attend_heads(q_ref, k_all[...], v_all[...], o_ref, stack_heads=False)


def _attn_lat(q, kr, v, cache_k4, cache_v4, l, seq_len, riders=()):
    n = q.shape[0]
    past = cache_k4.shape[2]
    nq = seq_len // TQ
    step_of = lambda b, j: b * nq + j
    r_in, r_out, r_shape = _rider_specs(riders, (n // seq_len) * nq, step_of)
    return pl.pallas_call(
        functools.partial(_attn_lat_kernel, past=past, n_riders=len(riders)),
        grid=(n // seq_len, nq),
        in_specs=[pl.BlockSpec((TQ, Q_DIM), lambda b, j: (b * nq + j, 0)),
                  pl.BlockSpec((None, None, past, KV_DIM), lambda b, j: (b, l, 0, 0)),
                  pl.BlockSpec((None, None, past, KV_DIM), lambda b, j: (b, l, 0, 0)),
                  pl.BlockSpec((seq_len, KV_DIM), lambda b, j: (b, 0)),
                  pl.BlockSpec((seq_len, KV_DIM), lambda b, j: (b, 0))] + r_in,
        out_specs=[pl.BlockSpec((TQ, Q_DIM), lambda b, j: (b * nq + j, 0))] + r_out,
        out_shape=[jax.ShapeDtypeStruct((n, Q_DIM), BF16)] + r_shape,
        scratch_shapes=[pltpu.VMEM((past + seq_len, KV_DIM), BF16),
                        pltpu.VMEM((past + seq_len, KV_DIM), BF16)],
        compiler_params=_cparams(("arbitrary", "arbitrary")),
        name="attn_lat",
    )(q, cache_k4, cache_v4, kr, v, *[r.w2d for r in riders])


R_E1, R_E2, R_W1, R_W2, R_RANK1, R_RANK2 = range(6)


def _route(h2, wrt_ref, brt_ref, count_ref):
    shape = (h2.shape[0], LANES)
    lane = lax.broadcasted_iota(jnp.int32, shape, 1)
    neg = jnp.float32(-jnp.inf)
    h_hi = h2.astype(BF16)
    h_lo = (h2 - h_hi.astype(F32)).astype(BF16)
    z_hi = _dot(h_hi, wrt_ref[...])
    z_lo = _dot(h_lo, wrt_ref[...])
    logits = ((z_hi[:, :LANES] + z_hi[:, LANES:]) + (z_lo[:, :LANES] + z_lo[:, LANES:])
              + brt_ref[...])
    logits = jnp.where(lane < N_EXPERTS, logits, neg)
    v1 = jnp.max(logits, axis=-1, keepdims=True)
    i1 = jnp.min(jnp.where(logits == v1, lane, LANES), axis=-1, keepdims=True)
    rest = jnp.where(lane == i1, neg, logits)
    v2 = jnp.max(rest, axis=-1, keepdims=True)
    i2 = jnp.min(jnp.where(rest == v2, lane, LANES), axis=-1, keepdims=True)
    e2 = jnp.exp(v2 - v1)
    w1 = 1.0 / (1.0 + e2)
    w2 = e2 / (1.0 + e2)
    chosen = jnp.where((lane == i1) | (lane == i2), 1.0, 0.0)
    r_id = lax.broadcasted_iota(jnp.int32, (shape[0], shape[0]), 0)
    c_id = lax.broadcasted_iota(jnp.int32, (shape[0], shape[0]), 1)
    before = jnp.where(c_id < r_id, 1.0, 0.0).astype(BF16)
    rank = _dot(before, chosen.astype(BF16)) + count_ref[...]
    rank1 = jnp.sum(jnp.where(lane == i1, rank, 0.0), axis=-1, keepdims=True)
    rank2 = jnp.sum(jnp.where(lane == i2, rank, 0.0), axis=-1, keepdims=True)
    count_ref[...] += jnp.sum(chosen, axis=0, keepdims=True)
    rec = jnp.zeros(shape, F32)
    for k, val in ((R_E1, i1.astype(F32)), (R_E2, i2.astype(F32)), (R_W1, w1), (R_W2, w2),
                   (R_RANK1, rank1), (R_RANK2, rank2)):
        rec = jnp.where(lane == k, val, rec)
    return rec


def _mix_out_kernel(*refs, route):
    (x_ref, r_ref, a_ref, gr_ref, ga_ref, mod_ref, n2_ref, wr_ref, wa_ref, wo_ref) = refs[:10]
    if route:
        wrt_ref, brt_ref, x1_ref, h2_ref, rec_ref, cnt_ref, count_ref, hbuf, hsem = refs[10:]
    else:
        x1_ref, h2_ref = refs[10:]
    merged = (_sigmoid(gr_ref[...].astype(F32)) * _dot(r_ref[...].astype(BF16), wr_ref[...])
              + _sigmoid(ga_ref[...].astype(F32)) * _dot(a_ref[...], wa_ref[...]))
    mix = _dot(merged.astype(BF16), wo_ref[...])
    x1 = x_ref[...] + mod_ref[2:3, :] * mix
    x1_ref[...] = x1
    h2 = _rms(x1, n2_ref[...]) * (1.0 + mod_ref[4:5, :]) + mod_ref[3:4, :]
    if not route:
        h2_ref[...] = h2.astype(BF16)
        return

    @pl.when(pl.program_id(0) == 0)
    def _():
        count_ref[...] = jnp.zeros_like(count_ref)

    rec_ref[...] = _route(h2, wrt_ref, brt_ref, count_ref)
    cnt_ref[...] = count_ref[...]

    step = pl.program_id(0)
    slot = step % 2

    def h_copies(k, s):
        rows = pl.ds(k * TM, TM)
        return [pltpu.make_async_copy(hbuf.at[s, :, c * LANES:(c + 1) * LANES], h2_ref.at[rows, c],
                                      hsem.at[s]) for c in range(D_MODEL // LANES)]

    @pl.when(step >= 2)
    def _():
        for cp in h_copies(step - 2, slot):
            cp.wait()

    hbuf[slot] = h2
    for cp in h_copies(step, slot):
        cp.start()

    @pl.when(step == pl.num_programs(0) - 1)
    def _():
        for cp in h_copies(step, slot):
            cp.wait()

        @pl.when(step >= 1)
        def _():
            for cp in h_copies(step - 1, 1 - slot):
                cp.wait()


def _mix_out(x, out_r, out_a, gr, ga, mods, l, norm2, wr_bf, wa_bf, wo_bf, wl, seq_len, lat,
             router=None):
    n = x.shape[0]
    tiles_per_seq = seq_len // TM
    if lat:
        mod_idx = lambda i: (l, 1 + i // tiles_per_seq, 0, 0)
    else:
        mod_idx = lambda i: (l, 0, 0, 0)
    row = lambda i: (i, 0)
    wspec = pl.BlockSpec((None, D_MODEL, D_MODEL), lambda i: (wl, 0, 0), pipeline_mode=RESIDENT)
    in_specs = [pl.BlockSpec((TM, D_MODEL), row)] * 5 + [
        pl.BlockSpec((None, None, 6, D_MODEL), mod_idx),
        pl.BlockSpec((None, 1, D_MODEL), lambda i: (l, 0, 0)),
        wspec, wspec, wspec]
    args = [x, out_r, out_a, gr, ga, mods, norm2, wr_bf, wa_bf, wo_bf]
    out_specs = [pl.BlockSpec((TM, D_MODEL), row)]
    out_shape = [jax.ShapeDtypeStruct((n, D_MODEL), F32)]
    scratch = []
    if router is None:
        out_specs.append(pl.BlockSpec((TM, D_MODEL), row))
        out_shape.append(jax.ShapeDtypeStruct((n, D_MODEL), BF16))
    else:
        j, wr_t, br_pad = router
        in_specs += [pl.BlockSpec((None, D_MODEL, 2 * LANES), lambda i: (j, 0, 0)),
                     pl.BlockSpec((None, 1, LANES), lambda i: (j, 0, 0))]
        args += [wr_t, br_pad]
        out_specs += [pl.BlockSpec(memory_space=pl.ANY),
                      pl.BlockSpec((TM, LANES), row),
                      pl.BlockSpec((1, LANES), lambda i: (0, 0))]
        out_shape += [jax.ShapeDtypeStruct((n, D_MODEL // LANES, LANES), F32),
                      jax.ShapeDtypeStruct((n, LANES), F32),
                      jax.ShapeDtypeStruct((1, LANES), F32)]
        scratch = [pltpu.VMEM((1, LANES), F32), pltpu.VMEM((2, TM, D_MODEL), F32),
                   pltpu.SemaphoreType.DMA((2,))]
    return pl.pallas_call(
        functools.partial(_mix_out_kernel, route=router is not None),
        grid=(n // TM,),
        in_specs=in_specs,
        out_specs=out_specs,
        out_shape=out_shape,
        scratch_shapes=scratch,
        compiler_params=_cparams(("arbitrary",)),
        name=("mix_route" if router is not None else "mix_out") + ("_lat" if lat else "_ctx"),
    )(*args)


def _ffn_kernel(x_ref, h_ref, mod_ref, wg_ref, wu_ref, wd_ref, o_ref, *, f_chunk):
    h = h_ref[...]
    d_ff = wg_ref.shape[1]
    acc = jnp.zeros((h.shape[0], D_MODEL), F32)
    for c in range(d_ff // f_chunk):
        sl = slice(c * f_chunk, (c + 1) * f_chunk)
        act = _silu(_dot(h, wg_ref[:, sl])) * _dot(h, wu_ref[:, sl])
        acc = acc + _dot(act.astype(BF16), wd_ref[sl, :])
    o_ref[...] = x_ref[...] + mod_ref[5:6, :] * acc


def _ffn(x1, h2, mods, l, j, wg_bf, wu_bf, wd_bf, seq_len, lat):
    n = x1.shape[0]
    d_ff = wg_bf.shape[2]
    tiles_per_seq = seq_len // TM
    if lat:
        mod_idx = lambda i: (l, 1 + i // tiles_per_seq, 0, 0)
    else:
        mod_idx = lambda i: (l, 0, 0, 0)
    row = lambda i: (i, 0)
    return pl.pallas_call(
        functools.partial(_ffn_kernel, f_chunk=d_ff // 2),
        grid=(n // TM,),
        in_specs=[pl.BlockSpec((TM, D_MODEL), row), pl.BlockSpec((TM, D_MODEL), row),
                  pl.BlockSpec((None, None, 6, D_MODEL), mod_idx),
                  pl.BlockSpec((None, D_MODEL, d_ff), lambda i: (j, 0, 0), pipeline_mode=RESIDENT),
                  pl.BlockSpec((None, D_MODEL, d_ff), lambda i: (j, 0, 0), pipeline_mode=RESIDENT),
                  pl.BlockSpec((None, d_ff, D_MODEL), lambda i: (j, 0, 0), pipeline_mode=RESIDENT)],
        out_specs=pl.BlockSpec((TM, D_MODEL), row),
        out_shape=jax.ShapeDtypeStruct((n, D_MODEL), F32),
        compiler_params=_cparams(("arbitrary",)),
        name="ffn_lat" if lat else "ffn_ctx",
    )(x1, h2, mods, wg_bf, wu_bf, wd_bf)


def _group_layout(rec, counts, n_tiles):
    as_int = lambda k: rec[:, k].astype(jnp.int32)
    cnt = counts[0, :N_EXPERTS].astype(jnp.int32)
    tiles = (cnt + TM_GROUP - 1) // TM_GROUP
    cum = jnp.cumsum(tiles)
    start = (cum - tiles) * TM_GROUP
    experts = jnp.arange(N_EXPERTS, dtype=jnp.int32)

    def pos(e, rank):
        return jnp.sum(jnp.where(e[:, None] == experts[None, :], start[None, :], 0), axis=1) + rank

    pos2 = jnp.stack([pos(as_int(R_E1), as_int(R_RANK1)), pos(as_int(R_E2), as_int(R_RANK2))])
    n_active = cum[N_EXPERTS - 1]
    tile_expert = jnp.sum(jnp.arange(n_tiles, dtype=jnp.int32)[:, None] >= cum[None, :], axis=1)
    tile_expert = jnp.minimum(tile_expert, tile_expert[n_active - 1]).astype(jnp.int32)
    tile_ids = jnp.arange(n_tiles, dtype=jnp.int32)
    group_last = jnp.any((tile_ids[:, None] == cum[None, :] - 1) & (tiles[None, :] > 0), axis=1)
    zero_tile = (group_last | (tile_ids >= n_active)).astype(jnp.int32)
    return (pos2.astype(jnp.int32), tile_expert, n_active.reshape(1).astype(jnp.int32), zero_tile)


def _row_copy(src, dst, sem):
    return pltpu.make_async_copy(src, dst, sem)


def _dispatch_kernel(pos_ref, zt_ref, h_ref, xs_ref, zbuf, sem, zsem):
    n_chunks = h_ref.shape[0] // DMA_CHUNK
    n_tiles = xs_ref.shape[0] // TM_GROUP

    zbuf[...] = jnp.zeros_like(zbuf)

    def zero_fill(j):
        return pltpu.make_async_copy(zbuf, xs_ref.at[pl.ds(j * TM_GROUP, TM_GROUP)], zsem)

    def zero_start(j, carry):
        @pl.when(zt_ref[j] != 0)
        def _():
            zero_fill(j).start()
        return carry

    def zero_wait(j, carry):
        @pl.when(zt_ref[j] != 0)
        def _():
            zero_fill(j).wait()
        return carry

    lax.fori_loop(0, n_tiles, zero_start, 0)
    lax.fori_loop(0, n_tiles, zero_wait, 0)

    def drain_chunk():
        rows = pl.ds(0, 2 * DMA_CHUNK)
        _row_copy(xs_ref.at[rows], xs_ref.at[rows], sem).wait()

    def chunk(k, carry):
        for r in range(DMA_CHUNK):
            t = k * DMA_CHUNK + r
            _row_copy(h_ref.at[t], xs_ref.at[pos_ref[0, t]], sem).start(priority=0)
            _row_copy(h_ref.at[t], xs_ref.at[pos_ref[1, t]], sem).start(priority=1)

        @pl.when(k > 0)
        def _():
            drain_chunk()

        return carry

    lax.fori_loop(0, n_chunks, chunk, 0)
    drain_chunk()


def _dispatch(pos2, zero_tile, h2_rows, n_rows):
    tile = h2_rows.shape[1:]
    return pl.pallas_call(
        _dispatch_kernel,
        grid_spec=pltpu.PrefetchScalarGridSpec(
            num_scalar_prefetch=2, grid=(1,),
            in_specs=[pl.BlockSpec(h2_rows.shape, lambda i, p, z: (0, 0, 0))],
            out_specs=pl.BlockSpec(memory_space=pl.ANY),
            scratch_shapes=[pltpu.VMEM((TM_GROUP,) + tile, F32),
                            pltpu.SemaphoreType.DMA(()), pltpu.SemaphoreType.DMA(())]),
        out_shape=jax.ShapeDtypeStruct((n_rows,) + tile, F32),
        compiler_params=_cparams(("arbitrary",)),
        name="moe_dispatch",
    )(pos2, zero_tile, h2_rows)


def _expert_kernel(te_ref, na_ref, xs_ref, wg_ref, wu_ref, wd_ref, ys_ref,
                   xbuf, ybuf, sem_in, sem_out):
    del te_ref
    i = pl.program_id(0)
    last = pl.num_programs(0) - 1
    n_active = na_ref[0]
    slot = i % 2
    n_col = D_MODEL // LANES

    def in_copies(tile, s):
        rows = pl.ds(tile * TM_GROUP, TM_GROUP)
        return [pltpu.make_async_copy(xs_ref.at[rows, c], xbuf.at[s, :, c * LANES:(c + 1) * LANES],
                                      sem_in.at[s]) for c in range(n_col)]

    def out_copies(tile, s):
        rows = pl.ds(tile * TM_GROUP, TM_GROUP)
        return [pltpu.make_async_copy(ybuf.at[s, :, c * LANES:(c + 1) * LANES], ys_ref.at[rows, c],
                                      sem_out.at[s]) for c in range(n_col)]

    @pl.when(i == 0)
    def _():
        for cp in in_copies(0, 0):
            cp.start()

    @pl.when(i + 1 < n_active)
    def _():
        for cp in in_copies(i + 1, 1 - slot):
            cp.start()

    @pl.when(i >= 2)
    def _():
        for cp in out_copies(i - 2, slot):
            cp.wait()

    @pl.when(i < n_active)
    def _():
        for cp in in_copies(i, slot):
            cp.wait()
        xb = xbuf[slot].astype(BF16)
        act = _silu(_dot(xb, wg_ref[...])) * _dot(xb, wu_ref[...])
        ybuf[slot] = _dot(act.astype(BF16), wd_ref[...])

    @pl.when(i >= n_active)
    def _():
        ybuf[slot] = jnp.zeros((TM_GROUP, D_MODEL), F32)

    for cp in out_copies(i, slot):
        cp.start()

    @pl.when(i == last)
    def _():
        for cp in out_copies(i, slot):
            cp.wait()

        @pl.when(i >= 1)
        def _():
            for cp in out_copies(i - 1, 1 - slot):
                cp.wait()


def _experts(tile_expert, n_active, xs, j, wg_bf, wu_bf, wd_bf):
    n_rows = xs.shape[0]
    d_e = wg_bf.shape[3]
    wspec = lambda shape: pl.BlockSpec((None, None) + shape, lambda i, te, na: (j, te[i], 0, 0))
    return pl.pallas_call(
        _expert_kernel,
        grid_spec=pltpu.PrefetchScalarGridSpec(
            num_scalar_prefetch=2, grid=(n_rows // TM_GROUP,),
            in_specs=[pl.BlockSpec(memory_space=pl.ANY),
                      wspec((D_MODEL, d_e)), wspec((D_MODEL, d_e)), wspec((d_e, D_MODEL))],
            out_specs=pl.BlockSpec(memory_space=pl.ANY),
            scratch_shapes=[pltpu.VMEM((2, TM_GROUP, D_MODEL), F32),
                            pltpu.VMEM((2, TM_GROUP, D_MODEL), F32),
                            pltpu.SemaphoreType.DMA((2,)), pltpu.SemaphoreType.DMA((2,))]),
        out_shape=jax.ShapeDtypeStruct(xs.shape, F32),
        compiler_params=_cparams(("arbitrary",)),
        name="moe_experts",
    )(tile_expert, n_active, xs, wg_bf, wu_bf, wd_bf)


def _combine_kernel(*refs, final):
    pos_ref, y_ref, x_ref, rec_ref, mod_ref = refs[:5]
    g_ref = refs[5] if final else None
    o_ref, a1, a2, b1, b2, sem_a, sem_b = refs[5 + int(final):]
    i = pl.program_id(0)

    def issue(half_idx, d1, d2, sem):
        base = half_idx * DMA_CHUNK

        for r in range(DMA_CHUNK):
            _row_copy(y_ref.at[pos_ref[0, base + r]], d1.at[r], sem).start(priority=1)
            _row_copy(y_ref.at[pos_ref[1, base + r]], d2.at[r], sem).start(priority=1)

    def drain(d1, d2, sem):
        rows = pl.ds(0, DMA_CHUNK)
        _row_copy(y_ref.at[rows], d1, sem).wait()
        _row_copy(y_ref.at[rows], d2, sem).wait()

    def finish(half, d1, d2):
        rows = slice(half * DMA_CHUNK, (half + 1) * DMA_CHUNK)
        w1 = rec_ref[rows, R_W1:R_W1 + 1]
        w2 = rec_ref[rows, R_W2:R_W2 + 1]
        for c in range(D_MODEL // LANES):
            sl = slice(c * LANES, (c + 1) * LANES)
            y = w1 * d1[:, c, :] + w2 * d2[:, c, :]
            o_ref[rows, sl] = x_ref[rows, sl] + mod_ref[5:6, sl] * y
        if final:
            o_ref[rows, :] = _rms(o_ref[rows, :], g_ref[...])

    @pl.when(i == 0)
    def _():
        issue(0, a1, a2, sem_a)

    issue(2 * i + 1, b1, b2, sem_b)
    drain(a1, a2, sem_a)
    finish(0, a1, a2)

    @pl.when(i + 1 < pl.num_programs(0))
    def _():
        issue(2 * i + 2, a1, a2, sem_a)

    drain(b1, b2, sem_b)
    finish(1, b1, b2)


def _combine(pos2, ys, x1, rec, mods, l, seq_len, lat, final_g):
    n = x1.shape[0]
    final = final_g is not None
    tile = ys.shape[1:]
    tm = 2 * DMA_CHUNK
    tiles_per_seq = max(seq_len // tm, 1)
    if lat:
        mod_idx = lambda i, p: (l, 1 + i // tiles_per_seq, 0, 0)
    else:
        mod_idx = lambda i, p: (l, 0, 0, 0)
    row = lambda i, p: (i, 0)
    slot = pltpu.VMEM((DMA_CHUNK,) + tile, F32)
    in_specs = [pl.BlockSpec(memory_space=pl.ANY),
                pl.BlockSpec((tm, D_MODEL), row),
                pl.BlockSpec((tm, LANES), row),
                pl.BlockSpec((None, None, 6, D_MODEL), mod_idx)]
    args = [pos2, ys, x1, rec, mods]
    if final:
        in_specs.append(pl.BlockSpec((1, D_MODEL), lambda i, p: (0, 0)))
        args.append(final_g)
    return pl.pallas_call(
        functools.partial(_combine_kernel, final=final),
        grid_spec=pltpu.PrefetchScalarGridSpec(
            num_scalar_prefetch=1, grid=(n // tm,),
            in_specs=in_specs,
            out_specs=pl.BlockSpec((tm, D_MODEL), row),
            scratch_shapes=[slot, slot, slot, slot,
                            pltpu.SemaphoreType.DMA(()), pltpu.SemaphoreType.DMA(())]),
        out_shape=jax.ShapeDtypeStruct((n, D_MODEL), F32),
        compiler_params=_cparams(("arbitrary",)),
        name="moe_combine_final" if final else "moe_combine",
    )(*args)


def _moe(x1, h2_rows, rec, counts, mods, l, j, wg_bf, wu_bf, wd_bf, seq_len, lat, final_g):
    n = x1.shape[0]
    n_tiles = 2 * n // TM_GROUP + N_EXPERTS
    pos2, tile_expert, n_active, zero_tile = _group_layout(rec, counts, n_tiles)
    xs = _dispatch(pos2, zero_tile, h2_rows, n_tiles * TM_GROUP)
    ys = _experts(tile_expert, n_active, xs, j, wg_bf, wu_bf, wd_bf)
    return _combine(pos2, ys, x1, rec, mods, l, seq_len, lat, final_g)


def _final_norm_kernel(x_ref, g_ref, o_ref):
    o_ref[...] = _rms(x_ref[...], g_ref[...])


def _final_norm(x, g):
    n = x.shape[0]
    row = lambda i: (i, 0)
    return pl.pallas_call(
        _final_norm_kernel,
        grid=(n // TM_FF,),
        in_specs=[pl.BlockSpec((TM_FF, D_MODEL), row), pl.BlockSpec((1, D_MODEL), lambda i: (0, 0))],
        out_specs=pl.BlockSpec((TM_FF, D_MODEL), row),
        out_shape=jax.ShapeDtypeStruct((n, D_MODEL), F32),
        compiler_params=_cparams(("arbitrary",)),
        name="final_norm",
    )(x, g)


def _rope_tables(t_len):
    t = jnp.arange(t_len)
    pos = jnp.stack([t // GRID_W, t % GRID_W], axis=-1).astype(F32)
    inv = ROPE_THETA ** (-jnp.arange(0, AXIS_DIM, 2, dtype=F32) / AXIS_DIM)
    ang = pos[..., None] * inv
    cos, sin = jnp.cos(ang), jnp.sin(ang)
    c = jnp.concatenate([cos[:, 0], cos[:, 0], cos[:, 1], cos[:, 1]], axis=-1)
    s = jnp.concatenate([-sin[:, 0], sin[:, 0], -sin[:, 1], sin[:, 1]], axis=-1)
    return c, s


def _gate_weights(wa, wi):
    depth = wa.shape[0]
    groups = D_RNN // LANES

    def pair(w):
        w = w.reshape(depth, 2, groups, 2, RNN_BLOCK, RNN_BLOCK)
        z = jnp.zeros_like(w[:, :, :, 0])
        top = jnp.concatenate([w[:, :, :, 0], z], axis=-1)
        bot = jnp.concatenate([z, w[:, :, :, 1]], axis=-1)
        return jnp.concatenate([top, bot], axis=-2)

    w = jnp.concatenate([pair(wa), pair(wi)], axis=-1)
    return (0.5 * jnp.concatenate([w[:, 0], w[:, 1]], axis=-1)).astype(BF16)


def kernel(x_prompt, x_sample, cache_k, cache_v, state_rnn, c, c_ctx, w_mod, b_mod, norm1, norm2, w_in, conv_w, conv_b, lru_wa, lru_ba, lru_wi, lru_bi, lru_lambda, q_norm, k_norm, w_rnn_out, w_attn_out, w_out, ff_gate, ff_up, ff_down, router_w, router_b, exp_gate, exp_up, exp_down, final_norm):
    batch, seq, _ = x_prompt.shape
    dec_batch, dec_seq, _ = x_sample.shape
    depth = w_mod.shape[0]
    past = cache_k.shape[2]
    assert dec_batch + 1 <= MOD_ROWS

    cond8 = jnp.zeros((MOD_ROWS, D_MODEL), F32).at[0].set(c_ctx).at[1:1 + dec_batch].set(c)

    gate_w = _gate_weights(lru_wa, lru_wi)
    n_moe = router_w.shape[0]
    wr_pad = jnp.zeros((n_moe, D_MODEL, LANES), F32).at[:, :, :N_EXPERTS].set(router_w)
    wr_head = wr_pad.astype(BF16)
    wr_t = jnp.concatenate([wr_head, (wr_pad - wr_head.astype(F32)).astype(BF16)], axis=-1)
    br_pad = jnp.zeros((n_moe, 1, LANES), F32).at[:, 0, :N_EXPERTS].set(router_b)

    norm1_3 = norm1.reshape(depth, 1, D_MODEL)
    norm2_3 = norm2.reshape(depth, 1, D_MODEL)
    qn3 = q_norm.reshape(depth, 1, HEAD_DIM)
    kn3 = k_norm.reshape(depth, 1, HEAD_DIM)
    conv_b3 = conv_b.reshape(depth, 1, D_RNN)
    ba4 = (0.5 * lru_ba).reshape(depth, 2, 1, D_RNN)
    bi4 = (0.5 * lru_bi).reshape(depth, 2, 1, D_RNN)
    lam4 = lru_lambda.reshape(depth, 2, 1, D_RNN)
    state5 = state_rnn.reshape(dec_batch, depth, 2, 1, D_RNN)
    cache_k4 = cache_k.reshape(dec_batch, depth, past, KV_DIM)
    cache_v4 = cache_v.reshape(dec_batch, depth, past, KV_DIM)
    rope_tabs = _rope_tables(dec_seq)
    final_g = final_norm.reshape(1, D_MODEL)

    n_steps = dec_batch * dec_seq // TQ
    assert n_steps == dec_batch * dec_seq // TM_ROPE

    def flat(w):
        return w.reshape(-1, w.shape[-1])

    def channel_names(l):
        return ("exp_gate", "exp_up", "exp_down") if l % 2 == 1 else ("ff_gate", "ff_up", "ff_down")

    stacked = dict(w_in=w_in, w_rnn_out=w_rnn_out, w_attn_out=w_attn_out, w_out=w_out,
                   ff_gate=ff_gate, ff_up=ff_up, ff_down=ff_down,
                   exp_gate=exp_gate, exp_up=exp_up, exp_down=exp_down)

    def layer_shape(name):
        return (1,) + stacked[name].shape[1:]

    def layer_index(name, l):
        return l if name.startswith("w_") else l // 2

    def rider(name, l, steps):
        w = stacked[name]
        rows = flat(w).shape[0] // w.shape[0] // steps
        return _Rider(flat(w), rows, layer_index(name, l) * steps)

    def names_of(l):
        return ("w_in", "w_rnn_out", "w_attn_out", "w_out") + channel_names(l)

    weights = {l: {} for l in range(depth)}

    def keep(l, names, arrays):
        for name, arr in zip(names, arrays):
            weights[l][name] = arr.reshape(layer_shape(name))

    mod_steps = _modulation_steps(depth)
    mods, w_in0 = _modulation(cond8, w_mod, b_mod, riders=[rider("w_in", 0, mod_steps)])
    mods = mods.reshape(depth, MOD_ROWS, 6, D_MODEL)
    keep(0, ("w_in",), [w_in0])
    ctx_steps = batch * seq // TM

    def run_layer(x, l, lat):
        seq_len = dec_seq if lat else seq
        wts = weights[l]
        convert_next = lat and l + 1 < depth
        if convert_next:
            proj_names = names_of(l + 1)[:4]
            attn_names = names_of(l + 1)[4:]
            proj_riders = [rider(name, l + 1, n_steps) for name in proj_names]
            attn_riders = [rider(name, l + 1, n_steps) for name in attn_names]
            target = l + 1
        elif l == 0 and not lat:
            proj_names, attn_names, attn_riders, target = names_of(0)[1:], (), [], 0
            proj_riders = [rider(name, 0, ctx_steps) for name in proj_names]
        else:
            proj_names, attn_names, proj_riders, attn_riders, target = (), (), [], [], None
        outs = _in_proj(x, mods, l, norm1_3, wts["w_in"], 0, qn3, kn3,
                        rope_tabs if lat else None, seq_len, riders=proj_riders,
                        caches=None if lat else new_kv)
        xr, yr, q, _, kr, v, gr, ga = outs[:8]
        if target is not None:
            keep(target, proj_names, outs[8:8 + len(proj_names)])
        if not lat:
            new_kv[:] = outs[8 + len(proj_names):]
        out_r, end_f, end_b = _rnn(xr, yr, l, conv_w, conv_b3, gate_w, ba4, bi4, lam4,
                                   state5 if lat else None, seq_len)
        if lat:
            outs = _attn_lat(q, kr, v, cache_k4, cache_v4, l, seq_len, riders=attn_riders)
            out_a = outs[0]
            if target is not None:
                keep(target, attn_names, outs[1:])
        else:
            out_a = _attn_ctx(q, kr, v, seq_len)
        mix_args = (x, out_r, out_a, gr, ga, mods, l, norm2_3, wts["w_rnn_out"],
                    wts["w_attn_out"], wts["w_out"], 0, seq_len, lat)
        if l % 2 == 1:
            x1, h2_rows, rec, counts = _mix_out(*mix_args, router=(l // 2, wr_t, br_pad))
            x = _moe(x1, h2_rows, rec, counts, mods, l, 0, wts["exp_gate"], wts["exp_up"],
                     wts["exp_down"], seq_len, lat, final_g if l == depth - 1 else None)
        else:
            x1, h2 = _mix_out(*mix_args)
            x = _ffn(x1, h2, mods, l, 0, wts["ff_gate"], wts["ff_up"], wts["ff_down"],
                     seq_len, lat)
            if l == depth - 1:
                x = _final_norm(x, final_g)
        return x, jnp.stack([end_f, end_b], axis=1)

    new_kv = [jnp.zeros((batch, depth, seq, N_KV_HEADS, HEAD_DIM), F32) for _ in range(2)]
    y_ctx = x_prompt.reshape(batch * seq, D_MODEL)
    y_lat = x_sample.reshape(dec_batch * dec_seq, D_MODEL)
    ss = []
    for l in range(depth):
        y_ctx, st = run_layer(y_ctx, l, False)
        ss.append(st)
        y_lat, _ = run_layer(y_lat, l, True)

    new_state = jnp.stack(ss, axis=1)
    return (y_ctx.reshape(batch, seq, D_MODEL), y_lat.reshape(dec_batch, dec_seq, D_MODEL),
            new_kv[0], new_kv[1], new_state)
```

```python
import functools

import jax
import jax.numpy as jnp
from jax import lax
from jax.experimental import pallas as pl
from jax.experimental.pallas import tpu as pltpu

F32 = jnp.float32
BF16 = jnp.bfloat16

D_MODEL = 1024
D_RNN = 1024
N_RNN_BLOCKS = 16
RNN_BLOCK = D_RNN // N_RNN_BLOCKS
CONV_W = 4
CONV_LEFT = 2
LRU_C = 8.0
N_HEADS = 8
N_KV_HEADS = 2
HEAD_DIM = 128
AXIS_DIM = HEAD_DIM // 2
ROPE_THETA = 10000.0
GRID_W = 64
Q_DIM = N_HEADS * HEAD_DIM
KV_DIM = N_KV_HEADS * HEAD_DIM
N_EXPERTS = 8
EPS = 1e-6
LOG2_E = 1.4426950408889634
RSQRT_FLOOR = 1e-30
D_IN = 2 * D_RNN + Q_DIM + 2 * KV_DIM + 2 * D_MODEL
O_XR, O_YR, O_Q = 0, D_RNN, 2 * D_RNN
O_K = O_Q + Q_DIM
O_V = O_K + KV_DIM
O_GR = O_V + KV_DIM
O_GA = O_GR + D_MODEL

LANES = 128
SUBLANES = 8
MOD_ROWS = 8
VMEM_LIMIT = 56 * 1024 * 1024

TM = 512
TM_ROPE = 256
RESIDENT = pl.Buffered(1)
RNN_T = 256
RNN_SEG = SUBLANES
RNN_C = 512
RNN_TCH = 32
RNN_SCAN_UNROLL = 8
TQ = 256
ATTN_CTX_SEQS = 2
TM_FF = 512
TM_GROUP = 256
DMA_CHUNK = 256


def _cparams(sem):
    return pltpu.CompilerParams(dimension_semantics=sem, vmem_limit_bytes=VMEM_LIMIT)


def _sigmoid(x):
    return 1.0 / (1.0 + jnp.exp(-x))


def _silu(x):
    return x * _sigmoid(x)


def _gelu_tanh(x):
    return 0.5 * x * (1.0 + jnp.tanh(0.7978845608028654 * (x + 0.044715 * (x * x * x))))


def _rms(x, g):
    ms = jnp.mean(x * x, axis=-1, keepdims=True)
    return x * lax.rsqrt(ms + EPS) * g


def _dot(a, b):
    return jnp.dot(a, b, preferred_element_type=F32)


MOD_TN = 1536


def _mod_kernel(*refs, groups):
    n_rin = sum(groups)
    cond_ref, w_ref, b_ref = refs[:3]
    o_ref = refs[3 + n_rin]
    _convert_riders(refs[3:3 + n_rin], refs[4 + n_rin:], groups)
    s = _silu(cond_ref[...]).astype(BF16)
    o_ref[...] = _dot(s, w_ref[...].astype(BF16)) + b_ref[...]


def _modulation_steps(depth):
    return depth * (6 * D_MODEL // MOD_TN)


def _modulation(cond8, w_mod, b_mod, riders=()):
    depth = w_mod.shape[0]
    per_layer = 6 * D_MODEL // MOD_TN
    r_in, r_out, r_shape = _rider_specs(riders, depth * per_layer, lambda l, j: l * per_layer + j)
    return pl.pallas_call(
        functools.partial(_mod_kernel, groups=_rider_groups(riders)),
        grid=(depth, per_layer),
        in_specs=[
            pl.BlockSpec((MOD_ROWS, D_MODEL), lambda l, j: (0, 0)),
            pl.BlockSpec((None, D_MODEL, MOD_TN), lambda l, j: (l, 0, j)),
            pl.BlockSpec((None, 1, MOD_TN), lambda l, j: (l, 0, j)),
        ] + r_in,
        out_specs=[pl.BlockSpec((None, MOD_ROWS, MOD_TN), lambda l, j: (l, 0, j))] + r_out,
        out_shape=[jax.ShapeDtypeStruct((depth, MOD_ROWS, 6 * D_MODEL), F32)] + r_shape,
        compiler_params=_cparams(("arbitrary", "arbitrary")),
        name="adaln_mod",
    )(cond8, w_mod, b_mod.reshape(depth, 1, 6 * D_MODEL), *_rider_args(riders))


def _swap_halves(x):
    lane = lax.broadcasted_iota(jnp.int32, x.shape, 1)
    return jnp.where((lane % AXIS_DIM) < AXIS_DIM // 2,
                     pltpu.roll(x, LANES - AXIS_DIM // 2, axis=1),
                     pltpu.roll(x, AXIS_DIM // 2, axis=1))


class _Rider:
    def __init__(self, w2d, rows, first):
        self.parts = list(w2d) if isinstance(w2d, (list, tuple)) else [w2d]
        self.rows, self.first = rows, first


def _rider_specs(riders, n_steps, step_of):
    in_specs, out_specs, out_shape = [], [], []
    for r in riders:
        for w in r.parts:
            assert r.rows % 16 == 0 and (r.first + n_steps) * r.rows <= w.shape[0]
            in_specs.append(pl.BlockSpec((r.rows, w.shape[1]),
                                         lambda *g, r=r: (r.first + step_of(*g), 0)))
        cols = sum(w.shape[1] for w in r.parts)
        out_specs.append(pl.BlockSpec((r.rows, cols), lambda *g: (step_of(*g), 0)))
        out_shape.append(jax.ShapeDtypeStruct((r.rows * n_steps, cols), BF16))
    return in_specs, out_specs, out_shape


def _rider_groups(riders):
    return tuple(len(r.parts) for r in riders)


def _rider_args(riders):
    return [w for r in riders for w in r.parts]


def _convert_riders(in_refs, out_refs, groups):
    k = 0
    for o_ref, n_parts in zip(out_refs, groups):
        col = 0
        for w_ref in in_refs[k:k + n_parts]:
            o_ref[:, col:col + w_ref.shape[1]] = w_ref[...].astype(BF16)
            col += w_ref.shape[1]
        k += n_parts


def _in_proj_kernel(*refs, rope, groups, cache_layer, seq_len):
    n_in = 8 if rope else 6
    n_rin, n_rout = sum(groups), len(groups)
    n_cache = 0 if cache_layer is None else 2
    first_out = n_in + n_rin + n_cache
    _convert_riders(refs[n_in:n_in + n_rin], refs[first_out + 8:first_out + 8 + n_rout], groups)
    cache_refs = refs[first_out + 8 + n_rout:]
    refs = refs[:n_in] + refs[first_out:first_out + 8]
    if rope:
        (x_ref, mod_ref, n1_ref, w_ref, qn_ref, kn_ref, c_ref, s_ref,
         xr_ref, yr_ref, q_ref, kno_ref, kro_ref, v_ref, gr_ref, ga_ref) = refs
    else:
        (x_ref, mod_ref, n1_ref, w_ref, qn_ref, kn_ref,
         xr_ref, yr_ref, q_ref, kno_ref, kro_ref, v_ref, gr_ref, ga_ref) = refs
    x = x_ref[...]
    h = _rms(x, n1_ref[...]) * (1.0 + mod_ref[1:2, :]) + mod_ref[0:1, :]
    hb = h.astype(BF16)

    xr_ref[...] = _dot(hb, w_ref[:, O_XR:O_XR + D_RNN])
    yr_ref[...] = _dot(hb, w_ref[:, O_YR:O_YR + D_RNN])
    gr_ref[...] = _dot(hb, w_ref[:, O_GR:O_GR + D_MODEL]).astype(BF16)
    ga_ref[...] = _dot(hb, w_ref[:, O_GA:O_GA + D_MODEL]).astype(BF16)
    v_ref[...] = _dot(hb, w_ref[:, O_V:O_V + KV_DIM])

    if rope:
        cs, sn = c_ref[...], s_ref[...]
    zq = _dot(hb, w_ref[:, O_Q:O_Q + Q_DIM])
    for hd in range(N_HEADS):
        t = _rms(zq[:, hd * HEAD_DIM:(hd + 1) * HEAD_DIM], qn_ref[...])
        if rope:
            t = t * cs + _swap_halves(t) * sn
        q_ref[:, hd * HEAD_DIM:(hd + 1) * HEAD_DIM] = t.astype(BF16)
    zk = _dot(hb, w_ref[:, O_K:O_K + KV_DIM])
    for hd in range(N_KV_HEADS):
        t = _rms(zk[:, hd * HEAD_DIM:(hd + 1) * HEAD_DIM], kn_ref[...])
        kno_ref[:, hd * HEAD_DIM:(hd + 1) * HEAD_DIM] = t
        if rope:
            t = t * cs + _swap_halves(t) * sn
        kro_ref[:, hd * HEAD_DIM:(hd + 1) * HEAD_DIM] = t.astype(BF16)

    if cache_layer is None:
        return
    newk_ref, newv_ref, kbuf, vbuf, csem = cache_refs
    step = pl.program_id(0)
    slot = step % 2
    seqs_per_tile = x_ref.shape[0] // seq_len

    def cache_copies(k, s):
        cps = []
        for sq in range(seqs_per_tile):
            rows = slice(sq * seq_len, (sq + 1) * seq_len)
            for hd in range(N_KV_HEADS):
                cols = slice(hd * HEAD_DIM, (hd + 1) * HEAD_DIM)
                dst = (k * seqs_per_tile + sq, cache_layer, slice(None), hd, slice(None))
                cps.append(pltpu.make_async_copy(kbuf.at[s, rows, cols], newk_ref.at[dst], csem.at[s]))
                cps.append(pltpu.make_async_copy(vbuf.at[s, rows, cols], newv_ref.at[dst], csem.at[s]))
        return cps

    @pl.when(step >= 2)
    def _():
        for cp in cache_copies(step - 2, slot):
            cp.wait()

    kbuf[slot] = kno_ref[...]
    vbuf[slot] = v_ref[...]
    for cp in cache_copies(step, slot):
        cp.start()

    @pl.when(step == pl.num_programs(0) - 1)
    def _():
        for cp in cache_copies(step, slot):
            cp.wait()

        @pl.when(step >= 1)
        def _():
            for cp in cache_copies(step - 1, 1 - slot):
                cp.wait()


def _in_proj(x, mods, l, norm1, w_in_bf, wl, q_norm, k_norm, rope_tabs, seq_len, riders=(),
             caches=None):
    n = x.shape[0]
    rope = rope_tabs is not None
    tm = TM_ROPE if rope else TM
    tiles_per_seq = seq_len // tm
    if rope:
        mod_idx = lambda i: (l, 1 + i // tiles_per_seq, 0, 0)
    else:
        mod_idx = lambda i: (l, 0, 0, 0)
    row = lambda i: (i, 0)
    in_specs = [
        pl.BlockSpec((tm, D_MODEL), row),
        pl.BlockSpec((None, None, 6, D_MODEL), mod_idx),
        pl.BlockSpec((None, 1, D_MODEL), lambda i: (l, 0, 0)),
        pl.BlockSpec((None, D_MODEL, D_IN), lambda i: (wl, 0, 0), pipeline_mode=RESIDENT),
        pl.BlockSpec((None, 1, HEAD_DIM), lambda i: (l, 0, 0)),
        pl.BlockSpec((None, 1, HEAD_DIM), lambda i: (l, 0, 0)),
    ]
    args = [x, mods, norm1, w_in_bf, q_norm, k_norm]
    if rope:
        in_specs += [pl.BlockSpec((tm, HEAD_DIM), lambda i: (i % tiles_per_seq, 0))] * 2
        args += list(rope_tabs)
    widths = [(D_RNN, F32), (D_RNN, F32), (Q_DIM, BF16), (KV_DIM, F32), (KV_DIM, BF16),
              (KV_DIM, F32), (D_MODEL, BF16), (D_MODEL, BF16)]
    r_in, r_out, r_shape = _rider_specs(riders, n // tm, lambda i: i)
    in_specs = in_specs + r_in
    out_specs = [pl.BlockSpec((tm, w), row) for w, _ in widths] + r_out
    out_shape = [jax.ShapeDtypeStruct((n, w), dt) for w, dt in widths] + r_shape
    args = args + _rider_args(riders)
    scratch, aliases = [], {}
    if caches is not None:
        assert tm % seq_len == 0
        for c in caches:
            aliases[len(args)] = len(out_shape)
            args.append(c)
            in_specs.append(pl.BlockSpec(memory_space=pl.ANY))
            out_specs.append(pl.BlockSpec(memory_space=pl.ANY))
            out_shape.append(jax.ShapeDtypeStruct(c.shape, c.dtype))
        scratch = [pltpu.VMEM((2, tm, KV_DIM), F32), pltpu.VMEM((2, tm, KV_DIM), F32),
                   pltpu.SemaphoreType.DMA((2,))]
    return pl.pallas_call(
        functools.partial(_in_proj_kernel, rope=rope, groups=_rider_groups(riders),
                          cache_layer=None if caches is None else l, seq_len=seq_len),
        grid=(n // tm,),
        in_specs=in_specs,
        out_specs=out_specs,
        out_shape=out_shape,
        scratch_shapes=scratch,
        input_output_aliases=aliases,
        compiler_params=_cparams(("arbitrary",)),
        name="in_proj_lat" if rope else "in_proj_ctx",
    )(*args)


def _rnn_kernel(*refs, chained):
    refs = list(refs)
    xr_ref, yr_ref, cw_ref, cb_ref, gw_ref, ba_ref, bi_ref, lam_ref = refs[:8]
    refs = refs[8:]
    h0_ref = refs.pop(0) if chained else None
    out_ref, sf_ref, sb_ref, xext, ybuf, obuf, a_f, b_f, a_b, b_b, sem_in, sem_out = refs

    g = pl.program_id(0)
    cb = pl.program_id(1)
    n_cb = pl.num_programs(1)
    step = g * n_cb + cb
    n_steps = pl.num_programs(0) * n_cb
    slot = step % 2

    def hbm_window(kk, s):
        rows = pl.ds(pl.multiple_of(((kk // n_cb) * RNN_SEG + s) * RNN_T, RNN_T), RNN_T)
        cols = pl.ds(pl.multiple_of((kk % n_cb) * RNN_C, RNN_C), RNN_C)
        return rows, cols

    def in_copies(kk, sl_):
        cps = []
        for s in range(RNN_SEG):
            win = hbm_window(kk, s)
            cps.append(pltpu.make_async_copy(
                xr_ref.at[win], xext.at[sl_, pl.ds(CONV_LEFT, RNN_T), s], sem_in.at[sl_]))
            cps.append(pltpu.make_async_copy(yr_ref.at[win], ybuf.at[sl_, :, s], sem_in.at[sl_]))
        return cps

    def out_copies(kk, sl_):
        return [pltpu.make_async_copy(obuf.at[sl_, :, s], out_ref.at[hbm_window(kk, s)],
                                      sem_out.at[sl_]) for s in range(RNN_SEG)]

    @pl.when(step == 0)
    def _():
        for cp in in_copies(0, 0):
            cp.start()

    @pl.when(step + 1 < n_steps)
    def _():
        for cp in in_copies(step + 1, 1 - slot):
            cp.start()

    for cp in in_copies(step, slot):
        cp.wait()

    seg = lax.broadcasted_iota(jnp.int32, (RNN_SEG, RNN_C), 0)
    zero_row = jnp.zeros((RNN_SEG, RNN_C), F32)
    if chained:
        xext[slot, 0] = jnp.where(seg == 0, zero_row, pltpu.roll(xext[slot, RNN_T], 1, axis=0))
        xext[slot, 1] = jnp.where(seg == 0, zero_row, pltpu.roll(xext[slot, RNN_T + 1], 1, axis=0))
        xext[slot, RNN_T + CONV_LEFT] = jnp.where(
            seg == RNN_SEG - 1, zero_row, pltpu.roll(xext[slot, CONV_LEFT], RNN_SEG - 1, axis=0))
    else:
        xext[slot, 0] = zero_row
        xext[slot, 1] = zero_row
        xext[slot, RNN_T + CONV_LEFT] = zero_row

    scans = ((a_f, b_f), (a_b, b_b))
    rows2d = RNN_TCH * RNN_SEG

    def gate_chunk(tc, carry):
        t0 = pl.multiple_of(tc * RNN_TCH, RNN_TCH)
        for j in range(RNN_C // LANES):
            sl = slice(j * LANES, (j + 1) * LANES)
            xc = jnp.broadcast_to(cb_ref[:, sl].reshape(1, 1, LANES), (RNN_TCH, RNN_SEG, LANES))
            for k in range(CONV_W):
                xc = xc + (xext[slot, pl.ds(t0 + k, RNN_TCH), :, sl]
                           * cw_ref[k:k + 1, sl].reshape(1, 1, LANES))
            xc = xc.reshape(rows2d, LANES)
            pre = _dot(xc.astype(BF16), gw_ref[j])
            for d, (a_s, b_s) in enumerate(scans):
                lam = lam_ref[d, :, sl]
                log2_decay = (-LRU_C * LOG2_E) * (jnp.maximum(-lam, 0.0)
                                                  + jnp.log(1.0 + jnp.exp(-jnp.abs(lam))))
                r = 0.5 * jnp.tanh(pre[:, 2 * d * LANES:(2 * d + 1) * LANES]
                                   + ba_ref[d, :, sl]) + 0.5
                gi = 0.5 * jnp.tanh(pre[:, (2 * d + 1) * LANES:(2 * d + 2) * LANES]
                                    + bi_ref[d, :, sl]) + 0.5
                a = jnp.exp2(r * log2_decay)
                y = 1.0 - a * a
                b = (y * lax.rsqrt(jnp.maximum(y, RSQRT_FLOOR))) * (gi * xc)
                a_s[pl.ds(t0, RNN_TCH), :, sl] = a.reshape(RNN_TCH, RNN_SEG, LANES)
                b_s[pl.ds(t0, RNN_TCH), :, sl] = b.reshape(RNN_TCH, RNN_SEG, LANES)
        return carry

    lax.fori_loop(0, RNN_T // RNN_TCH, gate_chunk, 0)

    def scan_step(t, carry):
        hf, hb, pf, pb = carry
        tb = RNN_T - 1 - t
        af_t = a_f[t]
        hf = af_t * hf + b_f[t]
        b_f[t] = hf
        ab_t = a_b[tb]
        hb = ab_t * hb + b_b[tb]
        b_b[tb] = hb
        if chained:
            pf = af_t * pf
            a_f[t] = pf
            pb = ab_t * pb
            a_b[tb] = pb
        return hf, hb, pf, pb

    ones = jnp.ones((RNN_SEG, RNN_C), F32)
    hf, hb, pf, pb = lax.fori_loop(0, RNN_T, scan_step, (zero_row, zero_row, ones, ones),
                                   unroll=RNN_SCAN_UNROLL)

    if chained:
        st_f = jnp.broadcast_to(h0_ref[0], (RNN_SEG, RNN_C))
        st_b = jnp.broadcast_to(h0_ref[1], (RNN_SEG, RNN_C))
        in_f, in_b = st_f, st_b
        for _ in range(RNN_SEG - 1):
            in_f = jnp.where(seg == 0, st_f, pltpu.roll(pf * in_f + hf, 1, axis=0))
            in_b = jnp.where(seg == RNN_SEG - 1, st_b,
                             pltpu.roll(pb * in_b + hb, RNN_SEG - 1, axis=0))
        sf_ref[...] = pf * in_f + hf
        sb_ref[...] = pb * in_b + hb
    else:
        sf_ref[...] = hf
        sb_ref[...] = hb

    @pl.when(step >= 2)
    def _():
        for cp in out_copies(step - 2, slot):
            cp.wait()

    def finish_chunk(tc, carry):
        rows = pl.ds(pl.multiple_of(tc * RNN_TCH, RNN_TCH), RNN_TCH)
        h_f, h_b = b_f[rows], b_b[rows]
        if chained:
            h_f = a_f[rows] * in_f.reshape(1, RNN_SEG, RNN_C) + h_f
            h_b = a_b[rows] * in_b.reshape(1, RNN_SEG, RNN_C) + h_b
        obuf[slot, rows] = (h_f + h_b) * _gelu_tanh(ybuf[slot, rows])
        return carry

    lax.fori_loop(0, RNN_T // RNN_TCH, finish_chunk, 0)

    for cp in out_copies(step, slot):
        cp.start()

    @pl.when(step == n_steps - 1)
    def _():
        for cp in out_copies(step, slot):
            cp.wait()

        @pl.when(step >= 1)
        def _():
            for cp in out_copies(step - 1, 1 - slot):
                cp.wait()


def _rnn(xr, yr, l, conv_w, conv_b, gate_w, lru_ba, lru_bi, lru_lambda, state5, seq_len):
    n = xr.shape[0]
    chained = state5 is not None
    group_len = RNN_SEG * RNN_T
    assert n % group_len == 0 and seq_len in ((group_len,) if chained else (RNN_T,))
    n_groups = n // group_len
    lanes_per_step = RNN_C // LANES
    vec = pl.BlockSpec((None, 2, 1, RNN_C), lambda g, c: (l, 0, 0, c))
    in_specs = [
        pl.BlockSpec(memory_space=pl.ANY),
        pl.BlockSpec(memory_space=pl.ANY),
        pl.BlockSpec((None, CONV_W, RNN_C), lambda g, c: (l, 0, c)),
        pl.BlockSpec((None, 1, RNN_C), lambda g, c: (l, 0, c)),
        pl.BlockSpec((None, lanes_per_step, LANES, 4 * LANES), lambda g, c: (l, c, 0, 0)),
        vec, vec, vec,
    ]
    args = [xr, yr, conv_w, conv_b, gate_w, lru_ba, lru_bi, lru_lambda]
    if chained:
        in_specs.append(pl.BlockSpec((None, None, 2, 1, RNN_C), lambda g, c: (g, l, 0, 0, c)))
        args.append(state5)
    state_spec = pl.BlockSpec((RNN_SEG, RNN_C), lambda g, c: (g, c))
    state_shape = jax.ShapeDtypeStruct((n_groups * RNN_SEG, D_RNN), F32)
    work = lambda t: pltpu.VMEM((t, RNN_SEG, RNN_C), F32)
    return pl.pallas_call(
        functools.partial(_rnn_kernel, chained=chained),
        grid=(n_groups, D_RNN // RNN_C),
        in_specs=in_specs,
        out_specs=[pl.BlockSpec(memory_space=pl.ANY), state_spec, state_spec],
        out_shape=[jax.ShapeDtypeStruct((n, D_RNN), F32), state_shape, state_shape],
        scratch_shapes=[
            pltpu.VMEM((2, RNN_T + CONV_W - 1, RNN_SEG, RNN_C), F32),
            pltpu.VMEM((2, RNN_T, RNN_SEG, RNN_C), F32),
            pltpu.VMEM((2, RNN_T, RNN_SEG, RNN_C), F32),
            work(RNN_T), work(RNN_T), work(RNN_T), work(RNN_T),
            pltpu.SemaphoreType.DMA((2,)), pltpu.SemaphoreType.DMA((2,)),
        ],
        compiler_params=_cparams(("arbitrary", "arbitrary")),
        name="rnn_lat" if chained else "rnn_ctx",
    )(*args)


def _attend_heads(q_ref, k_all, v_all, o_ref, stack_heads):
    exp2_scale = (1.0 / (HEAD_DIM ** 0.5)) * LOG2_E
    group = N_HEADS // N_KV_HEADS
    tq = q_ref.shape[0]
    for kh in range(N_KV_HEADS):
        k = k_all[:, kh * HEAD_DIM:(kh + 1) * HEAD_DIM]
        v = v_all[:, kh * HEAD_DIM:(kh + 1) * HEAD_DIM]
        heads = [slice((kh * group + g) * HEAD_DIM, (kh * group + g + 1) * HEAD_DIM)
                 for g in range(group)]
        stacks = [heads] if stack_heads else [[sl] for sl in heads]
        for stack in stacks:
            q = jnp.concatenate([q_ref[:, sl] for sl in stack], axis=0)
            s = lax.dot_general(q, k, (((1,), (1,)), ((), ())), preferred_element_type=F32)
            m = jnp.max(s, axis=-1, keepdims=True)
            p = jnp.exp2((s - m) * exp2_scale)
            denom = jnp.sum(p, axis=-1, keepdims=True)
            o = (_dot(p.astype(BF16), v) / denom).astype(BF16)
            for g, sl in enumerate(stack):
                o_ref[:, sl] = o[g * tq:(g + 1) * tq]


def _attn_ctx_kernel(q_ref, k_ref, v_ref, o_ref, *, seq_len):
    for s in range(q_ref.shape[0] // seq_len):
        rows = pl.ds(s * seq_len, seq_len)
        _attend_heads(q_ref.at[rows], k_ref[rows, :], v_ref[rows, :].astype(BF16), o_ref.at[rows],
                      stack_heads=True)


def _attn_ctx(q, kr, v, seq_len):
    n = q.shape[0]
    row = lambda b: (b, 0)
    tile = ATTN_CTX_SEQS * seq_len
    return pl.pallas_call(
        functools.partial(_attn_ctx_kernel, seq_len=seq_len),
        grid=(n // tile,),
        in_specs=[pl.BlockSpec((tile, Q_DIM), row),
                  pl.BlockSpec((tile, KV_DIM), row),
                  pl.BlockSpec((tile, KV_DIM), row)],
        out_specs=pl.BlockSpec((tile, Q_DIM), row),
        out_shape=jax.ShapeDtypeStruct((n, Q_DIM), BF16),
        compiler_params=_cparams(("arbitrary",)),
        name="attn_ctx",
    )(q, kr, v)


def _attn_lat_kernel(*refs, past, groups):
    n_rin, n_rout = sum(groups), len(groups)
    _convert_riders(refs[5:5 + n_rin], refs[6 + n_rin:6 + n_rin + n_rout], groups)
    q_ref, kc_ref, vc_ref, kl_ref, vl_ref = refs[:5]
    o_ref = refs[5 + n_rin]
    k_all, v_all = refs[6 + n_rin + n_rout:]

    @pl.when(pl.program_id(1) == 0)
    def _():
        k_all[0:past, :] = kc_ref[...].astype(BF16)
        k_all[past:, :] = kl_ref[...]
        v_all[0:past, :] = vc_ref[...].astype(BF16)
        v_all[past:, :] = vl_ref[...].astype(BF16)

    _attend_heads(q_ref, k_all[...], v_all[...], o_ref, stack_heads=False)


def _attn_lat(q, kr, v, cache_k4, cache_v4, l, seq_len, riders=()):
    n = q.shape[0]
    past = cache_k4.shape[2]
    nq = seq_len // TQ
    step_of = lambda b, j: b * nq + j
    r_in, r_out, r_shape = _rider_specs(riders, (n // seq_len) * nq, step_of)
    return pl.pallas_call(
        functools.partial(_attn_lat_kernel, past=past, groups=_rider_groups(riders)),
        grid=(n // seq_len, nq),
        in_specs=[pl.BlockSpec((TQ, Q_DIM), lambda b, j: (b * nq + j, 0)),
                  pl.BlockSpec((None, None, past, KV_DIM), lambda b, j: (b, l, 0, 0)),
                  pl.BlockSpec((None, None, past, KV_DIM), lambda b, j: (b, l, 0, 0)),
                  pl.BlockSpec((seq_len, KV_DIM), lambda b, j: (b, 0)),
                  pl.BlockSpec((seq_len, KV_DIM), lambda b, j: (b, 0))] + r_in,
        out_specs=[pl.BlockSpec((TQ, Q_DIM), lambda b, j: (b * nq + j, 0))] + r_out,
        out_shape=[jax.ShapeDtypeStruct((n, Q_DIM), BF16)] + r_shape,
        scratch_shapes=[pltpu.VMEM((past + seq_len, KV_DIM), BF16),
                        pltpu.VMEM((past + seq_len, KV_DIM), BF16)],
        compiler_params=_cparams(("arbitrary", "arbitrary")),
        name="attn_lat",
    )(q, cache_k4, cache_v4, kr, v, *_rider_args(riders))


R_E1, R_E2, R_W1, R_W2, R_RANK1, R_RANK2 = range(6)


def _route(h2, wrt_ref, brt_ref, count_ref):
    shape = (h2.shape[0], LANES)
    lane = lax.broadcasted_iota(jnp.int32, shape, 1)
    neg = jnp.float32(-jnp.inf)
    h_hi = h2.astype(BF16)
    h_lo = (h2 - h_hi.astype(F32)).astype(BF16)
    z_hi = _dot(h_hi, wrt_ref[...])
    z_lo = _dot(h_lo, wrt_ref[...])
    logits = ((z_hi[:, :LANES] + z_hi[:, LANES:]) + (z_lo[:, :LANES] + z_lo[:, LANES:])
              + brt_ref[...])
    logits = jnp.where(lane < N_EXPERTS, logits, neg)
    v1 = jnp.max(logits, axis=-1, keepdims=True)
    i1 = jnp.min(jnp.where(logits == v1, lane, LANES), axis=-1, keepdims=True)
    rest = jnp.where(lane == i1, neg, logits)
    v2 = jnp.max(rest, axis=-1, keepdims=True)
    i2 = jnp.min(jnp.where(rest == v2, lane, LANES), axis=-1, keepdims=True)
    e2 = jnp.exp(v2 - v1)
    w1 = 1.0 / (1.0 + e2)
    w2 = e2 / (1.0 + e2)
    chosen = jnp.where((lane == i1) | (lane == i2), 1.0, 0.0)
    r_id = lax.broadcasted_iota(jnp.int32, (shape[0], shape[0]), 0)
    c_id = lax.broadcasted_iota(jnp.int32, (shape[0], shape[0]), 1)
    before = jnp.where(c_id < r_id, 1.0, 0.0).astype(BF16)
    rank = _dot(before, chosen.astype(BF16)) + count_ref[...]
    rank1 = jnp.sum(jnp.where(lane == i1, rank, 0.0), axis=-1, keepdims=True)
    rank2 = jnp.sum(jnp.where(lane == i2, rank, 0.0), axis=-1, keepdims=True)
    count_ref[...] += jnp.sum(chosen, axis=0, keepdims=True)
    rec = jnp.zeros(shape, F32)
    for k, val in ((R_E1, i1.astype(F32)), (R_E2, i2.astype(F32)), (R_W1, w1), (R_W2, w2),
                   (R_RANK1, rank1), (R_RANK2, rank2)):
        rec = jnp.where(lane == k, val, rec)
    return rec


def _mix_out_kernel(*refs, route):
    (x_ref, r_ref, a_ref, gr_ref, ga_ref, mod_ref, n2_ref, wr_ref, wa_ref, wo_ref) = refs[:10]
    if route:
        wrt_ref, brt_ref, x1_ref, h2_ref, rec_ref, cnt_ref, count_ref, hbuf, hsem = refs[10:]
    else:
        x1_ref, h2_ref = refs[10:]
    merged = (_sigmoid(gr_ref[...].astype(F32)) * _dot(r_ref[...].astype(BF16), wr_ref[...])
              + _sigmoid(ga_ref[...].astype(F32)) * _dot(a_ref[...], wa_ref[...]))
    mix = _dot(merged.astype(BF16), wo_ref[...])
    x1 = x_ref[...] + mod_ref[2:3, :] * mix
    x1_ref[...] = x1
    h2 = _rms(x1, n2_ref[...]) * (1.0 + mod_ref[4:5, :]) + mod_ref[3:4, :]
    if not route:
        h2_ref[...] = h2.astype(BF16)
        return

    @pl.when(pl.program_id(0) == 0)
    def _():
        count_ref[...] = jnp.zeros_like(count_ref)

    rec_ref[...] = _route(h2, wrt_ref, brt_ref, count_ref)
    cnt_ref[...] = count_ref[...]

    step = pl.program_id(0)
    slot = step % 2

    def h_copies(k, s):
        rows = pl.ds(k * TM, TM)
        return [pltpu.make_async_copy(hbuf.at[s, :, c * LANES:(c + 1) * LANES], h2_ref.at[rows, c],
                                      hsem.at[s]) for c in range(D_MODEL // LANES)]

    @pl.when(step >= 2)
    def _():
        for cp in h_copies(step - 2, slot):
            cp.wait()

    hbuf[slot] = h2
    for cp in h_copies(step, slot):
        cp.start()

    @pl.when(step == pl.num_programs(0) - 1)
    def _():
        for cp in h_copies(step, slot):
            cp.wait()

        @pl.when(step >= 1)
        def _():
            for cp in h_copies(step - 1, 1 - slot):
                cp.wait()


def _mix_out(x, out_r, out_a, gr, ga, mods, l, norm2, wr_bf, wa_bf, wo_bf, wl, seq_len, lat,
             router=None):
    n = x.shape[0]
    tiles_per_seq = seq_len // TM
    if lat:
        mod_idx = lambda i: (l, 1 + i // tiles_per_seq, 0, 0)
    else:
        mod_idx = lambda i: (l, 0, 0, 0)
    row = lambda i: (i, 0)
    wspec = pl.BlockSpec((None, D_MODEL, D_MODEL), lambda i: (wl, 0, 0), pipeline_mode=RESIDENT)
    in_specs = [pl.BlockSpec((TM, D_MODEL), row)] * 5 + [
        pl.BlockSpec((None, None, 6, D_MODEL), mod_idx),
        pl.BlockSpec((None, 1, D_MODEL), lambda i: (l, 0, 0)),
        wspec, wspec, wspec]
    args = [x, out_r, out_a, gr, ga, mods, norm2, wr_bf, wa_bf, wo_bf]
    out_specs = [pl.BlockSpec((TM, D_MODEL), row)]
    out_shape = [jax.ShapeDtypeStruct((n, D_MODEL), F32)]
    scratch = []
    if router is None:
        out_specs.append(pl.BlockSpec((TM, D_MODEL), row))
        out_shape.append(jax.ShapeDtypeStruct((n, D_MODEL), BF16))
    else:
        j, wr_t, br_pad = router
        in_specs += [pl.BlockSpec((None, D_MODEL, 2 * LANES), lambda i: (j, 0, 0)),
                     pl.BlockSpec((None, 1, LANES), lambda i: (j, 0, 0))]
        args += [wr_t, br_pad]
        out_specs += [pl.BlockSpec(memory_space=pl.ANY),
                      pl.BlockSpec((TM, LANES), row),
                      pl.BlockSpec((1, LANES), lambda i: (0, 0))]
        out_shape += [jax.ShapeDtypeStruct((n, D_MODEL // LANES, LANES), F32),
                      jax.ShapeDtypeStruct((n, LANES), F32),
                      jax.ShapeDtypeStruct((1, LANES), F32)]
        scratch = [pltpu.VMEM((1, LANES), F32), pltpu.VMEM((2, TM, D_MODEL), F32),
                   pltpu.SemaphoreType.DMA((2,))]
    return pl.pallas_call(
        functools.partial(_mix_out_kernel, route=router is not None),
        grid=(n // TM,),
        in_specs=in_specs,
        out_specs=out_specs,
        out_shape=out_shape,
        scratch_shapes=scratch,
        compiler_params=_cparams(("arbitrary",)),
        name=("mix_route" if router is not None else "mix_out") + ("_lat" if lat else "_ctx"),
    )(*args)


def _ffn_kernel(x_ref, h_ref, mod_ref, wg_ref, wu_ref, wd_ref, o_ref, *, f_chunk):
    h = h_ref[...]
    d_ff = wg_ref.shape[1]
    acc = jnp.zeros((h.shape[0], D_MODEL), F32)
    for c in range(d_ff // f_chunk):
        sl = slice(c * f_chunk, (c + 1) * f_chunk)
        act = _silu(_dot(h, wg_ref[:, sl])) * _dot(h, wu_ref[:, sl])
        acc = acc + _dot(act.astype(BF16), wd_ref[sl, :])
    o_ref[...] = x_ref[...] + mod_ref[5:6, :] * acc


def _ffn(x1, h2, mods, l, j, wg_bf, wu_bf, wd_bf, seq_len, lat):
    n = x1.shape[0]
    d_ff = wg_bf.shape[2]
    tiles_per_seq = seq_len // TM
    if lat:
        mod_idx = lambda i: (l, 1 + i // tiles_per_seq, 0, 0)
    else:
        mod_idx = lambda i: (l, 0, 0, 0)
    row = lambda i: (i, 0)
    return pl.pallas_call(
        functools.partial(_ffn_kernel, f_chunk=d_ff // 2),
        grid=(n // TM,),
        in_specs=[pl.BlockSpec((TM, D_MODEL), row), pl.BlockSpec((TM, D_MODEL), row),
                  pl.BlockSpec((None, None, 6, D_MODEL), mod_idx),
                  pl.BlockSpec((None, D_MODEL, d_ff), lambda i: (j, 0, 0), pipeline_mode=RESIDENT),
                  pl.BlockSpec((None, D_MODEL, d_ff), lambda i: (j, 0, 0), pipeline_mode=RESIDENT),
                  pl.BlockSpec((None, d_ff, D_MODEL), lambda i: (j, 0, 0), pipeline_mode=RESIDENT)],
        out_specs=pl.BlockSpec((TM, D_MODEL), row),
        out_shape=jax.ShapeDtypeStruct((n, D_MODEL), F32),
        compiler_params=_cparams(("arbitrary",)),
        name="ffn_lat" if lat else "ffn_ctx",
    )(x1, h2, mods, wg_bf, wu_bf, wd_bf)


def _group_layout(rec, counts, n_tiles):
    as_int = lambda k: rec[:, k].astype(jnp.int32)
    cnt = counts[0, :N_EXPERTS].astype(jnp.int32)
    tiles = (cnt + TM_GROUP - 1) // TM_GROUP
    cum = jnp.cumsum(tiles)
    start = (cum - tiles) * TM_GROUP
    experts = jnp.arange(N_EXPERTS, dtype=jnp.int32)

    def pos(e, rank):
        return jnp.sum(jnp.where(e[:, None] == experts[None, :], start[None, :], 0), axis=1) + rank

    pos2 = jnp.stack([pos(as_int(R_E1), as_int(R_RANK1)), pos(as_int(R_E2), as_int(R_RANK2))])
    n_active = cum[N_EXPERTS - 1]
    tile_expert = jnp.sum(jnp.arange(n_tiles, dtype=jnp.int32)[:, None] >= cum[None, :], axis=1)
    tile_expert = jnp.minimum(tile_expert, tile_expert[n_active - 1]).astype(jnp.int32)
    tile_ids = jnp.arange(n_tiles, dtype=jnp.int32)
    group_last = jnp.any((tile_ids[:, None] == cum[None, :] - 1) & (tiles[None, :] > 0), axis=1)
    zero_tile = (group_last | (tile_ids >= n_active)).astype(jnp.int32)
    return (pos2.astype(jnp.int32), tile_expert, n_active.reshape(1).astype(jnp.int32), zero_tile)


def _row_copy(src, dst, sem):
    return pltpu.make_async_copy(src, dst, sem)


def _dispatch_kernel(pos_ref, zt_ref, h_ref, xs_ref, zbuf, sem, zsem):
    n_chunks = h_ref.shape[0] // DMA_CHUNK
    n_tiles = xs_ref.shape[0] // TM_GROUP

    zbuf[...] = jnp.zeros_like(zbuf)

    def zero_fill(j):
        return pltpu.make_async_copy(zbuf, xs_ref.at[pl.ds(j * TM_GROUP, TM_GROUP)], zsem)

    def zero_start(j, carry):
        @pl.when(zt_ref[j] != 0)
        def _():
            zero_fill(j).start()
        return carry

    def zero_wait(j, carry):
        @pl.when(zt_ref[j] != 0)
        def _():
            zero_fill(j).wait()
        return carry

    lax.fori_loop(0, n_tiles, zero_start, 0)
    lax.fori_loop(0, n_tiles, zero_wait, 0)

    def drain_chunk():
        rows = pl.ds(0, 2 * DMA_CHUNK)
        _row_copy(xs_ref.at[rows], xs_ref.at[rows], sem).wait()

    def chunk(k, carry):
        for r in range(DMA_CHUNK):
            t = k * DMA_CHUNK + r
            _row_copy(h_ref.at[t], xs_ref.at[pos_ref[0, t]], sem).start(priority=0)
            _row_copy(h_ref.at[t], xs_ref.at[pos_ref[1, t]], sem).start(priority=1)

        @pl.when(k > 0)
        def _():
            drain_chunk()

        return carry

    lax.fori_loop(0, n_chunks, chunk, 0)
    drain_chunk()


def _dispatch(pos2, zero_tile, h2_rows, n_rows):
    tile = h2_rows.shape[1:]
    return pl.pallas_call(
        _dispatch_kernel,
        grid_spec=pltpu.PrefetchScalarGridSpec(
            num_scalar_prefetch=2, grid=(1,),
            in_specs=[pl.BlockSpec(h2_rows.shape, lambda i, p, z: (0, 0, 0))],
            out_specs=pl.BlockSpec(memory_space=pl.ANY),
            scratch_shapes=[pltpu.VMEM((TM_GROUP,) + tile, F32),
                            pltpu.SemaphoreType.DMA(()), pltpu.SemaphoreType.DMA(())]),
        out_shape=jax.ShapeDtypeStruct((n_rows,) + tile, F32),
        compiler_params=_cparams(("arbitrary",)),
        name="moe_dispatch",
    )(pos2, zero_tile, h2_rows)


def _expert_kernel(te_ref, na_ref, xs_ref, wgu_ref, wd_ref, ys_ref,
                   xbuf, ybuf, sem_in, sem_out):
    del te_ref
    i = pl.program_id(0)
    last = pl.num_programs(0) - 1
    n_active = na_ref[0]
    slot = i % 2
    n_col = D_MODEL // LANES

    def in_copies(tile, s):
        rows = pl.ds(tile * TM_GROUP, TM_GROUP)
        return [pltpu.make_async_copy(xs_ref.at[rows, c], xbuf.at[s, :, c * LANES:(c + 1) * LANES],
                                      sem_in.at[s]) for c in range(n_col)]

    def out_copies(tile, s):
        rows = pl.ds(tile * TM_GROUP, TM_GROUP)
        return [pltpu.make_async_copy(ybuf.at[s, :, c * LANES:(c + 1) * LANES], ys_ref.at[rows, c],
                                      sem_out.at[s]) for c in range(n_col)]

    @pl.when(i == 0)
    def _():
        for cp in in_copies(0, 0):
            cp.start()

    @pl.when(i + 1 < n_active)
    def _():
        for cp in in_copies(i + 1, 1 - slot):
            cp.start()

    @pl.when(i >= 2)
    def _():
        for cp in out_copies(i - 2, slot):
            cp.wait()

    @pl.when(i < n_active)
    def _():
        for cp in in_copies(i, slot):
            cp.wait()
        xb = xbuf[slot].astype(BF16)
        z = _dot(xb, wgu_ref[...])
        d_e = wd_ref.shape[0]
        act = _silu(z[:, :d_e]) * z[:, d_e:]
        ybuf[slot] = _dot(act.astype(BF16), wd_ref[...])

    @pl.when(i >= n_active)
    def _():
        ybuf[slot] = jnp.zeros((TM_GROUP, D_MODEL), F32)

    for cp in out_copies(i, slot):
        cp.start()

    @pl.when(i == last)
    def _():
        for cp in out_copies(i, slot):
            cp.wait()

        @pl.when(i >= 1)
        def _():
            for cp in out_copies(i - 1, 1 - slot):
                cp.wait()


def _experts(tile_expert, n_active, xs, j, wgu_bf, wd_bf):
    n_rows = xs.shape[0]
    d_e = wd_bf.shape[2]
    wspec = lambda shape: pl.BlockSpec((None, None) + shape, lambda i, te, na: (j, te[i], 0, 0))
    return pl.pallas_call(
        _expert_kernel,
        grid_spec=pltpu.PrefetchScalarGridSpec(
            num_scalar_prefetch=2, grid=(n_rows // TM_GROUP,),
            in_specs=[pl.BlockSpec(memory_space=pl.ANY),
                      wspec((D_MODEL, 2 * d_e)), wspec((d_e, D_MODEL))],
            out_specs=pl.BlockSpec(memory_space=pl.ANY),
            scratch_shapes=[pltpu.VMEM((2, TM_GROUP, D_MODEL), F32),
                            pltpu.VMEM((2, TM_GROUP, D_MODEL), F32),
                            pltpu.SemaphoreType.DMA((2,)), pltpu.SemaphoreType.DMA((2,))]),
        out_shape=jax.ShapeDtypeStruct(xs.shape, F32),
        compiler_params=_cparams(("arbitrary",)),
        name="moe_experts",
    )(tile_expert, n_active, xs, wgu_bf, wd_bf)


def _combine_kernel(*refs, final):
    pos_ref, y_ref, x_ref, rec_ref, mod_ref = refs[:5]
    g_ref = refs[5] if final else None
    o_ref, a1, a2, b1, b2, sem_a, sem_b = refs[5 + int(final):]
    i = pl.program_id(0)

    def issue(half_idx, d1, d2, sem):
        base = half_idx * DMA_CHUNK

        for r in range(DMA_CHUNK):
            _row_copy(y_ref.at[pos_ref[0, base + r]], d1.at[r], sem).start(priority=1)
            _row_copy(y_ref.at[pos_ref[1, base + r]], d2.at[r], sem).start(priority=1)

    def drain(d1, d2, sem):
        rows = pl.ds(0, DMA_CHUNK)
        _row_copy(y_ref.at[rows], d1, sem).wait()
        _row_copy(y_ref.at[rows], d2, sem).wait()

    def finish(half, d1, d2):
        rows = slice(half * DMA_CHUNK, (half + 1) * DMA_CHUNK)
        w1 = rec_ref[rows, R_W1:R_W1 + 1]
        w2 = rec_ref[rows, R_W2:R_W2 + 1]
        for c in range(D_MODEL // LANES):
            sl = slice(c * LANES, (c + 1) * LANES)
            y = w1 * d1[:, c, :] + w2 * d2[:, c, :]
            o_ref[rows, sl] = x_ref[rows, sl] + mod_ref[5:6, sl] * y
        if final:
            o_ref[rows, :] = _rms(o_ref[rows, :], g_ref[...])

    @pl.when(i == 0)
    def _():
        issue(0, a1, a2, sem_a)

    issue(2 * i + 1, b1, b2, sem_b)
    drain(a1, a2, sem_a)
    finish(0, a1, a2)

    @pl.when(i + 1 < pl.num_programs(0))
    def _():
        issue(2 * i + 2, a1, a2, sem_a)

    drain(b1, b2, sem_b)
    finish(1, b1, b2)


def _combine(pos2, ys, x1, rec, mods, l, seq_len, lat, final_g):
    n = x1.shape[0]
    final = final_g is not None
    tile = ys.shape[1:]
    tm = 2 * DMA_CHUNK
    tiles_per_seq = max(seq_len // tm, 1)
    if lat:
        mod_idx = lambda i, p: (l, 1 + i // tiles_per_seq, 0, 0)
    else:
        mod_idx = lambda i, p: (l, 0, 0, 0)
    row = lambda i, p: (i, 0)
    slot = pltpu.VMEM((DMA_CHUNK,) + tile, F32)
    in_specs = [pl.BlockSpec(memory_space=pl.ANY),
                pl.BlockSpec((tm, D_MODEL), row),
                pl.BlockSpec((tm, LANES), row),
                pl.BlockSpec((None, None, 6, D_MODEL), mod_idx)]
    args = [pos2, ys, x1, rec, mods]
    if final:
        in_specs.append(pl.BlockSpec((1, D_MODEL), lambda i, p: (0, 0)))
        args.append(final_g)
    return pl.pallas_call(
        functools.partial(_combine_kernel, final=final),
        grid_spec=pltpu.PrefetchScalarGridSpec(
            num_scalar_prefetch=1, grid=(n // tm,),
            in_specs=in_specs,
            out_specs=pl.BlockSpec((tm, D_MODEL), row),
            scratch_shapes=[slot, slot, slot, slot,
                            pltpu.SemaphoreType.DMA(()), pltpu.SemaphoreType.DMA(())]),
        out_shape=jax.ShapeDtypeStruct((n, D_MODEL), F32),
        compiler_params=_cparams(("arbitrary",)),
        name="moe_combine_final" if final else "moe_combine",
    )(*args)


def _moe(x1, h2_rows, rec, counts, mods, l, j, wgu_bf, wd_bf, seq_len, lat, final_g):
    n = x1.shape[0]
    n_tiles = 2 * n // TM_GROUP + N_EXPERTS
    pos2, tile_expert, n_active, zero_tile = _group_layout(rec, counts, n_tiles)
    xs = _dispatch(pos2, zero_tile, h2_rows, n_tiles * TM_GROUP)
    ys = _experts(tile_expert, n_active, xs, j, wgu_bf, wd_bf)
    return _combine(pos2, ys, x1, rec, mods, l, seq_len, lat, final_g)


def _final_norm_kernel(x_ref, g_ref, o_ref):
    o_ref[...] = _rms(x_ref[...], g_ref[...])


def _final_norm(x, g):
    n = x.shape[0]
    row = lambda i: (i, 0)
    return pl.pallas_call(
        _final_norm_kernel,
        grid=(n // TM_FF,),
        in_specs=[pl.BlockSpec((TM_FF, D_MODEL), row), pl.BlockSpec((1, D_MODEL), lambda i: (0, 0))],
        out_specs=pl.BlockSpec((TM_FF, D_MODEL), row),
        out_shape=jax.ShapeDtypeStruct((n, D_MODEL), F32),
        compiler_params=_cparams(("arbitrary",)),
        name="final_norm",
    )(x, g)


def _rope_tables(t_len):
    t = jnp.arange(t_len)
    pos = jnp.stack([t // GRID_W, t % GRID_W], axis=-1).astype(F32)
    inv = ROPE_THETA ** (-jnp.arange(0, AXIS_DIM, 2, dtype=F32) / AXIS_DIM)
    ang = pos[..., None] * inv
    cos, sin = jnp.cos(ang), jnp.sin(ang)
    c = jnp.concatenate([cos[:, 0], cos[:, 0], cos[:, 1], cos[:, 1]], axis=-1)
    s = jnp.concatenate([-sin[:, 0], sin[:, 0], -sin[:, 1], sin[:, 1]], axis=-1)
    return c, s


def _gate_weights(wa, wi):
    depth = wa.shape[0]
    groups = D_RNN // LANES

    def pair(w):
        w = w.reshape(depth, 2, groups, 2, RNN_BLOCK, RNN_BLOCK)
        z = jnp.zeros_like(w[:, :, :, 0])
        top = jnp.concatenate([w[:, :, :, 0], z], axis=-1)
        bot = jnp.concatenate([z, w[:, :, :, 1]], axis=-1)
        return jnp.concatenate([top, bot], axis=-2)

    w = jnp.concatenate([pair(wa), pair(wi)], axis=-1)
    return (0.5 * jnp.concatenate([w[:, 0], w[:, 1]], axis=-1)).astype(BF16)


def kernel(x_prompt, x_sample, cache_k, cache_v, state_rnn, c, c_ctx, w_mod, b_mod, norm1, norm2, w_in, conv_w, conv_b, lru_wa, lru_ba, lru_wi, lru_bi, lru_lambda, q_norm, k_norm, w_rnn_out, w_attn_out, w_out, ff_gate, ff_up, ff_down, router_w, router_b, exp_gate, exp_up, exp_down, final_norm):
    batch, seq, _ = x_prompt.shape
    dec_batch, dec_seq, _ = x_sample.shape
    depth = w_mod.shape[0]
    past = cache_k.shape[2]
    assert dec_batch + 1 <= MOD_ROWS

    cond8 = jnp.zeros((MOD_ROWS, D_MODEL), F32).at[0].set(c_ctx).at[1:1 + dec_batch].set(c)

    gate_w = _gate_weights(lru_wa, lru_wi)
    n_moe = router_w.shape[0]
    wr_pad = jnp.zeros((n_moe, D_MODEL, LANES), F32).at[:, :, :N_EXPERTS].set(router_w)
    wr_head = wr_pad.astype(BF16)
    wr_t = jnp.concatenate([wr_head, (wr_pad - wr_head.astype(F32)).astype(BF16)], axis=-1)
    br_pad = jnp.zeros((n_moe, 1, LANES), F32).at[:, 0, :N_EXPERTS].set(router_b)

    norm1_3 = norm1.reshape(depth, 1, D_MODEL)
    norm2_3 = norm2.reshape(depth, 1, D_MODEL)
    qn3 = q_norm.reshape(depth, 1, HEAD_DIM)
    kn3 = k_norm.reshape(depth, 1, HEAD_DIM)
    conv_b3 = conv_b.reshape(depth, 1, D_RNN)
    ba4 = (0.5 * lru_ba).reshape(depth, 2, 1, D_RNN)
    bi4 = (0.5 * lru_bi).reshape(depth, 2, 1, D_RNN)
    lam4 = lru_lambda.reshape(depth, 2, 1, D_RNN)
    state5 = state_rnn.reshape(dec_batch, depth, 2, 1, D_RNN)
    cache_k4 = cache_k.reshape(dec_batch, depth, past, KV_DIM)
    cache_v4 = cache_v.reshape(dec_batch, depth, past, KV_DIM)
    rope_tabs = _rope_tables(dec_seq)
    final_g = final_norm.reshape(1, D_MODEL)

    n_steps = dec_batch * dec_seq // TQ
    assert n_steps == dec_batch * dec_seq // TM_ROPE

    def flat(w):
        return w.reshape(-1, w.shape[-1])

    def channel_names(l):
        return ("exp_gate_up", "exp_down") if l % 2 == 1 else ("ff_gate", "ff_up", "ff_down")

    stacked = dict(w_in=(w_in,), w_rnn_out=(w_rnn_out,), w_attn_out=(w_attn_out,), w_out=(w_out,),
                   ff_gate=(ff_gate,), ff_up=(ff_up,), ff_down=(ff_down,),
                   exp_gate_up=(exp_gate, exp_up), exp_down=(exp_down,))

    def layer_shape(name):
        parts = stacked[name]
        return (1,) + parts[0].shape[1:-1] + (sum(w.shape[-1] for w in parts),)

    def layer_index(name, l):
        return l if name.startswith("w_") else l // 2

    def rider(name, l, steps):
        parts = stacked[name]
        rows = flat(parts[0]).shape[0] // parts[0].shape[0] // steps
        return _Rider([flat(w) for w in parts], rows, layer_index(name, l) * steps)

    def names_of(l):
        return ("w_in", "w_rnn_out", "w_attn_out", "w_out") + channel_names(l)

    weights = {l: {} for l in range(depth)}

    def keep(l, names, arrays):
        for name, arr in zip(names, arrays):
            weights[l][name] = arr.reshape(layer_shape(name))

    mod_steps = _modulation_steps(depth)
    mods, w_in0 = _modulation(cond8, w_mod, b_mod, riders=[rider("w_in", 0, mod_steps)])
    mods = mods.reshape(depth, MOD_ROWS, 6, D_MODEL)
    keep(0, ("w_in",), [w_in0])
    ctx_steps = batch * seq // TM

    def run_layer(x, l, lat):
        seq_len = dec_seq if lat else seq
        wts = weights[l]
        convert_next = lat and l + 1 < depth
        if convert_next:
            proj_names = names_of(l + 1)[:4]
            attn_names = names_of(l + 1)[4:]
            proj_riders = [rider(name, l + 1, n_steps) for name in proj_names]
            attn_riders = [rider(name, l + 1, n_steps) for name in attn_names]
            target = l + 1
        elif l == 0 and not lat:
            proj_names, attn_names, attn_riders, target = names_of(0)[1:], (), [], 0
            proj_riders = [rider(name, 0, ctx_steps) for name in proj_names]
        else:
            proj_names, attn_names, proj_riders, attn_riders, target = (), (), [], [], None
        outs = _in_proj(x, mods, l, norm1_3, wts["w_in"], 0, qn3, kn3,
                        rope_tabs if lat else None, seq_len, riders=proj_riders,
                        caches=None if lat else new_kv)
        xr, yr, q, _, kr, v, gr, ga = outs[:8]
        if target is not None:
            keep(target, proj_names, outs[8:8 + len(proj_names)])
        if not lat:
            new_kv[:] = outs[8 + len(proj_names):]
        out_r, end_f, end_b = _rnn(xr, yr, l, conv_w, conv_b3, gate_w, ba4, bi4, lam4,
                                   state5 if lat else None, seq_len)
        if lat:
            outs = _attn_lat(q, kr, v, cache_k4, cache_v4, l, seq_len, riders=attn_riders)
            out_a = outs[0]
            if target is not None:
                keep(target, attn_names, outs[1:])
        else:
            out_a = _attn_ctx(q, kr, v, seq_len)
        mix_args = (x, out_r, out_a, gr, ga, mods, l, norm2_3, wts["w_rnn_out"],
                    wts["w_attn_out"], wts["w_out"], 0, seq_len, lat)
        if l % 2 == 1:
            x1, h2_rows, rec, counts = _mix_out(*mix_args, router=(l // 2, wr_t, br_pad))
            x = _moe(x1, h2_rows, rec, counts, mods, l, 0, wts["exp_gate_up"], wts["exp_down"],
                     seq_len, lat, final_g if l == depth - 1 else None)
        else:
            x1, h2 = _mix_out(*mix_args)
            x = _ffn(x1, h2, mods, l, 0, wts["ff_gate"], wts["ff_up"], wts["ff_down"],
                     seq_len, lat)
            if l == depth - 1:
                x = _final_norm(x, final_g)
        return x, jnp.stack([end_f, end_b], axis=1)

    new_kv = [jnp.zeros((batch, depth, seq, N_KV_HEADS, HEAD_DIM), F32) for _ in range(2)]
    y_ctx = x_prompt.reshape(batch * seq, D_MODEL)
    y_lat = x_sample.reshape(dec_batch * dec_seq, D_MODEL)
    ss = []
    for l in range(depth):
        y_ctx, st = run_layer(y_ctx, l, False)
        ss.append(st)
        y_lat, _ = run_layer(y_lat, l, True)

    new_state = jnp.stack(ss, axis=1)
    return (y_ctx.reshape(batch, seq, D_MODEL), y_lat.reshape(dec_batch, dec_seq, D_MODEL),
            new_kv[0], new_kv[1], new_state)
```

```python
import functools

import jax
import jax.numpy as jnp
from jax import lax
from jax.experimental import pallas as pl
from jax.experimental.pallas import tpu as pltpu

F32 = jnp.float32
BF16 = jnp.bfloat16

D_MODEL = 1024
D_RNN = 1024
N_RNN_BLOCKS = 16
RNN_BLOCK = D_RNN // N_RNN_BLOCKS
CONV_W = 4
CONV_LEFT = 2
LRU_C = 8.0
N_HEADS = 8
N_KV_HEADS = 2
HEAD_DIM = 128
AXIS_DIM = HEAD_DIM // 2
ROPE_THETA = 10000.0
GRID_W = 64
Q_DIM = N_HEADS * HEAD_DIM
KV_DIM = N_KV_HEADS * HEAD_DIM
N_EXPERTS = 8
EPS = 1e-6
LOG2_E = 1.4426950408889634
RSQRT_FLOOR = 1e-30
D_IN = 2 * D_RNN + Q_DIM + 2 * KV_DIM + 2 * D_MODEL
O_XR, O_YR, O_Q = 0, D_RNN, 2 * D_RNN
O_K = O_Q + Q_DIM
O_V = O_K + KV_DIM
O_GR = O_V + KV_DIM
O_GA = O_GR + D_MODEL

LANES = 128
SUBLANES = 8
MOD_ROWS = 8
VMEM_LIMIT = 56 * 1024 * 1024

TM = 512
TM_ROPE = 256
RESIDENT = pl.Buffered(1)
RNN_T = 256
RNN_SEG = SUBLANES
RNN_C = 512
RNN_TCH = 32
RNN_SCAN_UNROLL = 8
TQ = 256
ATTN_CTX_SEQS = 2
TM_FF = 512
TM_GROUP = 256
DMA_CHUNK = 256


def _cparams(sem):
    return pltpu.CompilerParams(dimension_semantics=sem, vmem_limit_bytes=VMEM_LIMIT)


def _sigmoid(x):
    return 1.0 / (1.0 + jnp.exp(-x))


def _silu(x):
    return x * _sigmoid(x)


def _gelu_tanh(x):
    return 0.5 * x * (1.0 + jnp.tanh(0.7978845608028654 * (x + 0.044715 * (x * x * x))))


def _rms(x, g):
    ms = jnp.mean(x * x, axis=-1, keepdims=True)
    return x * lax.rsqrt(ms + EPS) * g


def _dot(a, b):
    return jnp.dot(a, b, preferred_element_type=F32)


MOD_TN = 1536


def _mod_kernel(*refs, groups):
    n_rin = sum(groups)
    cond_ref, w_ref, b_ref = refs[:3]
    o_ref = refs[3 + n_rin]
    _convert_riders(refs[3:3 + n_rin], refs[4 + n_rin:], groups)
    s = _silu(cond_ref[...]).astype(BF16)
    o_ref[...] = _dot(s, w_ref[...].astype(BF16)) + b_ref[...]


def _modulation_steps(depth):
    return depth * (6 * D_MODEL // MOD_TN)


def _modulation(cond8, w_mod, b_mod, riders=()):
    depth = w_mod.shape[0]
    per_layer = 6 * D_MODEL // MOD_TN
    r_in, r_out, r_shape = _rider_specs(riders, depth * per_layer, lambda l, j: l * per_layer + j)
    return pl.pallas_call(
        functools.partial(_mod_kernel, groups=_rider_groups(riders)),
        grid=(depth, per_layer),
        in_specs=[
            pl.BlockSpec((MOD_ROWS, D_MODEL), lambda l, j: (0, 0)),
            pl.BlockSpec((None, D_MODEL, MOD_TN), lambda l, j: (l, 0, j)),
            pl.BlockSpec((None, 1, MOD_TN), lambda l, j: (l, 0, j)),
        ] + r_in,
        out_specs=[pl.BlockSpec((None, MOD_ROWS, MOD_TN), lambda l, j: (l, 0, j))] + r_out,
        out_shape=[jax.ShapeDtypeStruct((depth, MOD_ROWS, 6 * D_MODEL), F32)] + r_shape,
        compiler_params=_cparams(("arbitrary", "arbitrary")),
        name="adaln_mod",
    )(cond8, w_mod, b_mod.reshape(depth, 1, 6 * D_MODEL), *_rider_args(riders))


def _swap_halves(x):
    lane = lax.broadcasted_iota(jnp.int32, x.shape, 1)
    return jnp.where((lane % AXIS_DIM) < AXIS_DIM // 2,
                     pltpu.roll(x, LANES - AXIS_DIM // 2, axis=1),
                     pltpu.roll(x, AXIS_DIM // 2, axis=1))


class _Rider:
    def __init__(self, w2d, rows, first):
        self.parts = list(w2d) if isinstance(w2d, (list, tuple)) else [w2d]
        self.rows, self.first = rows, first


def _rider_specs(riders, n_steps, step_of):
    in_specs, out_specs, out_shape = [], [], []
    for r in riders:
        for w in r.parts:
            assert r.rows % 16 == 0 and (r.first + n_steps) * r.rows <= w.shape[0]
            in_specs.append(pl.BlockSpec((r.rows, w.shape[1]),
                                         lambda *g, r=r: (r.first + step_of(*g), 0)))
        cols = sum(w.shape[1] for w in r.parts)
        out_specs.append(pl.BlockSpec((r.rows, cols), lambda *g: (step_of(*g), 0)))
        out_shape.append(jax.ShapeDtypeStruct((r.rows * n_steps, cols), BF16))
    return in_specs, out_specs, out_shape


def _rider_groups(riders):
    return tuple(len(r.parts) for r in riders)


def _rider_args(riders):
    return [w for r in riders for w in r.parts]


def _convert_riders(in_refs, out_refs, groups):
    k = 0
    for o_ref, n_parts in zip(out_refs, groups):
        col = 0
        for w_ref in in_refs[k:k + n_parts]:
            o_ref[:, col:col + w_ref.shape[1]] = w_ref[...].astype(BF16)
            col += w_ref.shape[1]
        k += n_parts


def _in_proj_kernel(*refs, rope, groups, cache_layer, seq_len):
    n_in = 8 if rope else 6
    n_rin, n_rout = sum(groups), len(groups)
    n_cache = 0 if cache_layer is None else 2
    first_out = n_in + n_rin + n_cache
    _convert_riders(refs[n_in:n_in + n_rin], refs[first_out + 8:first_out + 8 + n_rout], groups)
    cache_refs = refs[first_out + 8 + n_rout:]
    refs = refs[:n_in] + refs[first_out:first_out + 8]
    if rope:
        (x_ref, mod_ref, n1_ref, w_ref, qn_ref, kn_ref, c_ref, s_ref,
         xr_ref, yr_ref, q_ref, kno_ref, kro_ref, v_ref, gr_ref, ga_ref) = refs
    else:
        (x_ref, mod_ref, n1_ref, w_ref, qn_ref, kn_ref,
         xr_ref, yr_ref, q_ref, kno_ref, kro_ref, v_ref, gr_ref, ga_ref) = refs
    x = x_ref[...]
    h = _rms(x, n1_ref[...]) * (1.0 + mod_ref[1:2, :]) + mod_ref[0:1, :]
    hb = h.astype(BF16)

    xr_ref[...] = _dot(hb, w_ref[:, O_XR:O_XR + D_RNN])
    yr_ref[...] = _dot(hb, w_ref[:, O_YR:O_YR + D_RNN])
    gr_ref[...] = _dot(hb, w_ref[:, O_GR:O_GR + D_MODEL]).astype(BF16)
    ga_ref[...] = _dot(hb, w_ref[:, O_GA:O_GA + D_MODEL]).astype(BF16)
    v_ref[...] = _dot(hb, w_ref[:, O_V:O_V + KV_DIM])

    if rope:
        cs, sn = c_ref[...], s_ref[...]
    zq = _dot(hb, w_ref[:, O_Q:O_Q + Q_DIM])
    for hd in range(N_HEADS):
        t = _rms(zq[:, hd * HEAD_DIM:(hd + 1) * HEAD_DIM], qn_ref[...])
        if rope:
            t = t * cs + _swap_halves(t) * sn
        q_ref[:, hd * HEAD_DIM:(hd + 1) * HEAD_DIM] = t.astype(BF16)
    zk = _dot(hb, w_ref[:, O_K:O_K + KV_DIM])
    for hd in range(N_KV_HEADS):
        t = _rms(zk[:, hd * HEAD_DIM:(hd + 1) * HEAD_DIM], kn_ref[...])
        kno_ref[:, hd * HEAD_DIM:(hd + 1) * HEAD_DIM] = t
        if rope:
            t = t * cs + _swap_halves(t) * sn
        kro_ref[:, hd * HEAD_DIM:(hd + 1) * HEAD_DIM] = t.astype(BF16)

    if cache_layer is None:
        return
    newk_ref, newv_ref, kbuf, vbuf, csem = cache_refs
    step = pl.program_id(0)
    slot = step % 2
    seqs_per_tile = x_ref.shape[0] // seq_len

    def cache_copies(k, s):
        cps = []
        for sq in range(seqs_per_tile):
            rows = slice(sq * seq_len, (sq + 1) * seq_len)
            for hd in range(N_KV_HEADS):
                cols = slice(hd * HEAD_DIM, (hd + 1) * HEAD_DIM)
                dst = (k * seqs_per_tile + sq, cache_layer, slice(None), hd, slice(None))
                cps.append(pltpu.make_async_copy(kbuf.at[s, rows, cols], newk_ref.at[dst], csem.at[s]))
                cps.append(pltpu.make_async_copy(vbuf.at[s, rows, cols], newv_ref.at[dst], csem.at[s]))
        return cps

    @pl.when(step >= 2)
    def _():
        for cp in cache_copies(step - 2, slot):
            cp.wait()

    kbuf[slot] = kno_ref[...]
    vbuf[slot] = v_ref[...]
    for cp in cache_copies(step, slot):
        cp.start()

    @pl.when(step == pl.num_programs(0) - 1)
    def _():
        for cp in cache_copies(step, slot):
            cp.wait()

        @pl.when(step >= 1)
        def _():
            for cp in cache_copies(step - 1, 1 - slot):
                cp.wait()


def _in_proj(x, mods, l, norm1, w_in_bf, wl, q_norm, k_norm, rope_tabs, seq_len, riders=(),
             caches=None):
    n = x.shape[0]
    rope = rope_tabs is not None
    tm = TM_ROPE if rope else TM
    tiles_per_seq = seq_len // tm
    if rope:
        mod_idx = lambda i: (l, 1 + i // tiles_per_seq, 0, 0)
    else:
        mod_idx = lambda i: (l, 0, 0, 0)
    row = lambda i: (i, 0)
    in_specs = [
        pl.BlockSpec((tm, D_MODEL), row),
        pl.BlockSpec((None, None, 6, D_MODEL), mod_idx),
        pl.BlockSpec((None, 1, D_MODEL), lambda i: (l, 0, 0)),
        pl.BlockSpec((None, D_MODEL, D_IN), lambda i: (wl, 0, 0), pipeline_mode=RESIDENT),
        pl.BlockSpec((None, 1, HEAD_DIM), lambda i: (l, 0, 0)),
        pl.BlockSpec((None, 1, HEAD_DIM), lambda i: (l, 0, 0)),
    ]
    args = [x, mods, norm1, w_in_bf, q_norm, k_norm]
    if rope:
        in_specs += [pl.BlockSpec((tm, HEAD_DIM), lambda i: (i % tiles_per_seq, 0))] * 2
        args += list(rope_tabs)
    widths = [(D_RNN, F32), (D_RNN, F32), (Q_DIM, BF16), (KV_DIM, F32), (KV_DIM, BF16),
              (KV_DIM, F32), (D_MODEL, BF16), (D_MODEL, BF16)]
    r_in, r_out, r_shape = _rider_specs(riders, n // tm, lambda i: i)
    in_specs = in_specs + r_in
    out_specs = [pl.BlockSpec((tm, w), row) for w, _ in widths] + r_out
    out_shape = [jax.ShapeDtypeStruct((n, w), dt) for w, dt in widths] + r_shape
    args = args + _rider_args(riders)
    scratch, aliases = [], {}
    if caches is not None:
        assert tm % seq_len == 0
        for c in caches:
            aliases[len(args)] = len(out_shape)
            args.append(c)
            in_specs.append(pl.BlockSpec(memory_space=pl.ANY))
            out_specs.append(pl.BlockSpec(memory_space=pl.ANY))
            out_shape.append(jax.ShapeDtypeStruct(c.shape, c.dtype))
        scratch = [pltpu.VMEM((2, tm, KV_DIM), F32), pltpu.VMEM((2, tm, KV_DIM), F32),
                   pltpu.SemaphoreType.DMA((2,))]
    return pl.pallas_call(
        functools.partial(_in_proj_kernel, rope=rope, groups=_rider_groups(riders),
                          cache_layer=None if caches is None else l, seq_len=seq_len),
        grid=(n // tm,),
        in_specs=in_specs,
        out_specs=out_specs,
        out_shape=out_shape,
        scratch_shapes=scratch,
        input_output_aliases=aliases,
        compiler_params=_cparams(("arbitrary",)),
        name="in_proj_lat" if rope else "in_proj_ctx",
    )(*args)


def _rnn_kernel(*refs, chained):
    refs = list(refs)
    xr_ref, yr_ref, cw_ref, cb_ref, gw_ref, ba_ref, bi_ref, lam_ref = refs[:8]
    refs = refs[8:]
    h0_ref = refs.pop(0) if chained else None
    out_ref, sf_ref, sb_ref, xext, ybuf, obuf, a_f, b_f, a_b, b_b, sem_in, sem_out = refs

    g = pl.program_id(0)
    cb = pl.program_id(1)
    n_cb = pl.num_programs(1)
    step = g * n_cb + cb
    n_steps = pl.num_programs(0) * n_cb
    slot = step % 2

    def hbm_window(kk, s):
        rows = pl.ds(pl.multiple_of(((kk // n_cb) * RNN_SEG + s) * RNN_T, RNN_T), RNN_T)
        cols = pl.ds(pl.multiple_of((kk % n_cb) * RNN_C, RNN_C), RNN_C)
        return rows, cols

    def in_copies(kk, sl_):
        cps = []
        for s in range(RNN_SEG):
            win = hbm_window(kk, s)
            cps.append(pltpu.make_async_copy(
                xr_ref.at[win], xext.at[sl_, pl.ds(CONV_LEFT, RNN_T), s], sem_in.at[sl_]))
            cps.append(pltpu.make_async_copy(yr_ref.at[win], ybuf.at[sl_, :, s], sem_in.at[sl_]))
        return cps

    def out_copies(kk, sl_):
        return [pltpu.make_async_copy(obuf.at[sl_, :, s], out_ref.at[hbm_window(kk, s)],
                                      sem_out.at[sl_]) for s in range(RNN_SEG)]

    @pl.when(step == 0)
    def _():
        for cp in in_copies(0, 0):
            cp.start()

    @pl.when(step + 1 < n_steps)
    def _():
        for cp in in_copies(step + 1, 1 - slot):
            cp.start()

    for cp in in_copies(step, slot):
        cp.wait()

    seg = lax.broadcasted_iota(jnp.int32, (RNN_SEG, RNN_C), 0)
    zero_row = jnp.zeros((RNN_SEG, RNN_C), F32)
    if chained:
        xext[slot, 0] = jnp.where(seg == 0, zero_row, pltpu.roll(xext[slot, RNN_T], 1, axis=0))
        xext[slot, 1] = jnp.where(seg == 0, zero_row, pltpu.roll(xext[slot, RNN_T + 1], 1, axis=0))
        xext[slot, RNN_T + CONV_LEFT] = jnp.where(
            seg == RNN_SEG - 1, zero_row, pltpu.roll(xext[slot, CONV_LEFT], RNN_SEG - 1, axis=0))
    else:
        xext[slot, 0] = zero_row
        xext[slot, 1] = zero_row
        xext[slot, RNN_T + CONV_LEFT] = zero_row

    scans = ((a_f, b_f), (a_b, b_b))
    rows2d = RNN_TCH * RNN_SEG

    def gate_chunk(tc, carry):
        t0 = pl.multiple_of(tc * RNN_TCH, RNN_TCH)
        for j in range(RNN_C // LANES):
            sl = slice(j * LANES, (j + 1) * LANES)
            xc = jnp.broadcast_to(cb_ref[:, sl].reshape(1, 1, LANES), (RNN_TCH, RNN_SEG, LANES))
            for k in range(CONV_W):
                xc = xc + (xext[slot, pl.ds(t0 + k, RNN_TCH), :, sl]
                           * cw_ref[k:k + 1, sl].reshape(1, 1, LANES))
            xc = xc.reshape(rows2d, LANES)
            pre = _dot(xc.astype(BF16), gw_ref[j])
            half_xc = 0.5 * xc
            for d, (a_s, b_s) in enumerate(scans):
                lam = lam_ref[d, :, sl]
                log2_decay = (-LRU_C * LOG2_E) * (jnp.maximum(-lam, 0.0)
                                                  + jnp.log(1.0 + jnp.exp(-jnp.abs(lam))))
                t_r = jnp.tanh(pre[:, 2 * d * LANES:(2 * d + 1) * LANES] + ba_ref[d, :, sl])
                t_i = jnp.tanh(pre[:, (2 * d + 1) * LANES:(2 * d + 2) * LANES]
                               + bi_ref[d, :, sl])
                half_decay = 0.5 * log2_decay
                a = jnp.exp2(t_r * half_decay + half_decay)
                y = 1.0 - a * a
                b = (y * lax.rsqrt(jnp.maximum(y, RSQRT_FLOOR))) * ((t_i + 1.0) * half_xc)
                a_s[pl.ds(t0, RNN_TCH), :, sl] = a.reshape(RNN_TCH, RNN_SEG, LANES)
                b_s[pl.ds(t0, RNN_TCH), :, sl] = b.reshape(RNN_TCH, RNN_SEG, LANES)
        return carry

    lax.fori_loop(0, RNN_T // RNN_TCH, gate_chunk, 0)

    def scan_step(t, carry):
        hf, hb, pf, pb = carry
        tb = RNN_T - 1 - t
        af_t = a_f[t]
        hf = af_t * hf + b_f[t]
        b_f[t] = hf
        ab_t = a_b[tb]
        hb = ab_t * hb + b_b[tb]
        b_b[tb] = hb
        if chained:
            pf = af_t * pf
            a_f[t] = pf
            pb = ab_t * pb
            a_b[tb] = pb
        return hf, hb, pf, pb

    ones = jnp.ones((RNN_SEG, RNN_C), F32)
    hf, hb, pf, pb = lax.fori_loop(0, RNN_T, scan_step, (zero_row, zero_row, ones, ones),
                                   unroll=RNN_SCAN_UNROLL)

    if chained:
        st_f = jnp.broadcast_to(h0_ref[0], (RNN_SEG, RNN_C))
        st_b = jnp.broadcast_to(h0_ref[1], (RNN_SEG, RNN_C))
        in_f, in_b = st_f, st_b
        for _ in range(RNN_SEG - 1):
            in_f = jnp.where(seg == 0, st_f, pltpu.roll(pf * in_f + hf, 1, axis=0))
            in_b = jnp.where(seg == RNN_SEG - 1, st_b,
                             pltpu.roll(pb * in_b + hb, RNN_SEG - 1, axis=0))
        sf_ref[...] = pf * in_f + hf
        sb_ref[...] = pb * in_b + hb
    else:
        sf_ref[...] = hf
        sb_ref[...] = hb

    @pl.when(step >= 2)
    def _():
        for cp in out_copies(step - 2, slot):
            cp.wait()

    def finish_chunk(tc, carry):
        rows = pl.ds(pl.multiple_of(tc * RNN_TCH, RNN_TCH), RNN_TCH)
        h_f, h_b = b_f[rows], b_b[rows]
        if chained:
            h_f = a_f[rows] * in_f.reshape(1, RNN_SEG, RNN_C) + h_f
            h_b = a_b[rows] * in_b.reshape(1, RNN_SEG, RNN_C) + h_b
        obuf[slot, rows] = (h_f + h_b) * _gelu_tanh(ybuf[slot, rows])
        return carry

    lax.fori_loop(0, RNN_T // RNN_TCH, finish_chunk, 0)

    for cp in out_copies(step, slot):
        cp.start()

    @pl.when(step == n_steps - 1)
    def _():
        for cp in out_copies(step, slot):
            cp.wait()

        @pl.when(step >= 1)
        def _():
            for cp in out_copies(step - 1, 1 - slot):
                cp.wait()


def _rnn(xr, yr, l, conv_w, conv_b, gate_w, lru_ba, lru_bi, lru_lambda, state5, seq_len):
    n = xr.shape[0]
    chained = state5 is not None
    group_len = RNN_SEG * RNN_T
    assert n % group_len == 0 and seq_len in ((group_len,) if chained else (RNN_T,))
    n_groups = n // group_len
    lanes_per_step = RNN_C // LANES
    vec = pl.BlockSpec((None, 2, 1, RNN_C), lambda g, c: (l, 0, 0, c))
    in_specs = [
        pl.BlockSpec(memory_space=pl.ANY),
        pl.BlockSpec(memory_space=pl.ANY),
        pl.BlockSpec((None, CONV_W, RNN_C), lambda g, c: (l, 0, c)),
        pl.BlockSpec((None, 1, RNN_C), lambda g, c: (l, 0, c)),
        pl.BlockSpec((None, lanes_per_step, LANES, 4 * LANES), lambda g, c: (l, c, 0, 0)),
        vec, vec, vec,
    ]
    args = [xr, yr, conv_w, conv_b, gate_w, lru_ba, lru_bi, lru_lambda]
    if chained:
        in_specs.append(pl.BlockSpec((None, None, 2, 1, RNN_C), lambda g, c: (g, l, 0, 0, c)))
        args.append(state5)
    state_spec = pl.BlockSpec((RNN_SEG, RNN_C), lambda g, c: (g, c))
    state_shape = jax.ShapeDtypeStruct((n_groups * RNN_SEG, D_RNN), F32)
    work = lambda t: pltpu.VMEM((t, RNN_SEG, RNN_C), F32)
    return pl.pallas_call(
        functools.partial(_rnn_kernel, chained=chained),
        grid=(n_groups, D_RNN // RNN_C),
        in_specs=in_specs,
        out_specs=[pl.BlockSpec(memory_space=pl.ANY), state_spec, state_spec],
        out_shape=[jax.ShapeDtypeStruct((n, D_RNN), F32), state_shape, state_shape],
        scratch_shapes=[
            pltpu.VMEM((2, RNN_T + CONV_W - 1, RNN_SEG, RNN_C), F32),
            pltpu.VMEM((2, RNN_T, RNN_SEG, RNN_C), F32),
            pltpu.VMEM((2, RNN_T, RNN_SEG, RNN_C), F32),
            work(RNN_T), work(RNN_T), work(RNN_T), work(RNN_T),
            pltpu.SemaphoreType.DMA((2,)), pltpu.SemaphoreType.DMA((2,)),
        ],
        compiler_params=_cparams(("arbitrary", "arbitrary")),
        name="rnn_lat" if chained else "rnn_ctx",
    )(*args)


def _attend_heads(q_ref, k_all, v_all, o_ref, stack_heads):
    exp2_scale = (1.0 / (HEAD_DIM ** 0.5)) * LOG2_E
    group = N_HEADS // N_KV_HEADS
    tq = q_ref.shape[0]
    for kh in range(N_KV_HEADS):
        k = k_all[:, kh * HEAD_DIM:(kh + 1) * HEAD_DIM]
        v = v_all[:, kh * HEAD_DIM:(kh + 1) * HEAD_DIM]
        heads = [slice((kh * group + g) * HEAD_DIM, (kh * group + g + 1) * HEAD_DIM)
                 for g in range(group)]
        stacks = [heads] if stack_heads else [[sl] for sl in heads]
        for stack in stacks:
            q = jnp.concatenate([q_ref[:, sl] for sl in stack], axis=0)
            s = lax.dot_general(q, k, (((1,), (1,)), ((), ())), preferred_element_type=F32)
            m = jnp.max(s, axis=-1, keepdims=True)
            p = jnp.exp2((s - m) * exp2_scale)
            denom = jnp.sum(p, axis=-1, keepdims=True)
            o = (_dot(p.astype(BF16), v) / denom).astype(BF16)
            for g, sl in enumerate(stack):
                o_ref[:, sl] = o[g * tq:(g + 1) * tq]


def _attn_ctx_kernel(q_ref, k_ref, v_ref, o_ref, *, seq_len):
    for s in range(q_ref.shape[0] // seq_len):
        rows = pl.ds(s * seq_len, seq_len)
        _attend_heads(q_ref.at[rows], k_ref[rows, :], v_ref[rows, :].astype(BF16), o_ref.at[rows],
                      stack_heads=True)


def _attn_ctx(q, kr, v, seq_len):
    n = q.shape[0]
    row = lambda b: (b, 0)
    tile = ATTN_CTX_SEQS * seq_len
    return pl.pallas_call(
        functools.partial(_attn_ctx_kernel, seq_len=seq_len),
        grid=(n // tile,),
        in_specs=[pl.BlockSpec((tile, Q_DIM), row),
                  pl.BlockSpec((tile, KV_DIM), row),
                  pl.BlockSpec((tile, KV_DIM), row)],
        out_specs=pl.BlockSpec((tile, Q_DIM), row),
        out_shape=jax.ShapeDtypeStruct((n, Q_DIM), BF16),
        compiler_params=_cparams(("arbitrary",)),
        name="attn_ctx",
    )(q, kr, v)


def _attn_lat_kernel(*refs, past, groups):
    n_rin, n_rout = sum(groups), len(groups)
    _convert_riders(refs[5:5 + n_rin], refs[6 + n_rin:6 + n_rin + n_rout], groups)
    q_ref, kc_ref, vc_ref, kl_ref, vl_ref = refs[:5]
    o_ref = refs[5 + n_rin]
    k_all, v_all = refs[6 + n_rin + n_rout:]

    @pl.when(pl.program_id(1) == 0)
    def _():
        k_all[0:past, :] = kc_ref[...].astype(BF16)
        k_all[past:, :] = kl_ref[...]
        v_all[0:past, :] = vc_ref[...].astype(BF16)
        v_all[past:, :] = vl_ref[...].astype(BF16)

    _attend_heads(q_ref, k_all[...], v_all[...], o_ref, stack_heads=False)


def _attn_lat(q, kr, v, cache_k4, cache_v4, l, seq_len, riders=()):
    n = q.shape[0]
    past = cache_k4.shape[2]
    nq = seq_len // TQ
    step_of = lambda b, j: b * nq + j
    r_in, r_out, r_shape = _rider_specs(riders, (n // seq_len) * nq, step_of)
    return pl.pallas_call(
        functools.partial(_attn_lat_kernel, past=past, groups=_rider_groups(riders)),
        grid=(n // seq_len, nq),
        in_specs=[pl.BlockSpec((TQ, Q_DIM), lambda b, j: (b * nq + j, 0)),
                  pl.BlockSpec((None, None, past, KV_DIM), lambda b, j: (b, l, 0, 0)),
                  pl.BlockSpec((None, None, past, KV_DIM), lambda b, j: (b, l, 0, 0)),
                  pl.BlockSpec((seq_len, KV_DIM), lambda b, j: (b, 0)),
                  pl.BlockSpec((seq_len, KV_DIM), lambda b, j: (b, 0))] + r_in,
        out_specs=[pl.BlockSpec((TQ, Q_DIM), lambda b, j: (b * nq + j, 0))] + r_out,
        out_shape=[jax.ShapeDtypeStruct((n, Q_DIM), BF16)] + r_shape,
        scratch_shapes=[pltpu.VMEM((past + seq_len, KV_DIM), BF16),
                        pltpu.VMEM((past + seq_len, KV_DIM), BF16)],
        compiler_params=_cparams(("arbitrary", "arbitrary")),
        name="attn_lat",
    )(q, cache_k4, cache_v4, kr, v, *_rider_args(riders))


R_E1, R_E2, R_W1, R_W2, R_RANK1, R_RANK2 = range(6)


def _route(h2, wrt_ref, brt_ref, count_ref):
    shape = (h2.shape[0], LANES)
    lane = lax.broadcasted_iota(jnp.int32, shape, 1)
    neg = jnp.float32(-jnp.inf)
    h_hi = h2.astype(BF16)
    h_lo = (h2 - h_hi.astype(F32)).astype(BF16)
    z_hi = _dot(h_hi, wrt_ref[...])
    z_lo = _dot(h_lo, wrt_ref[...])
    logits = ((z_hi[:, :LANES] + z_hi[:, LANES:]) + (z_lo[:, :LANES] + z_lo[:, LANES:])
              + brt_ref[...])
    logits = jnp.where(lane < N_EXPERTS, logits, neg)
    v1 = jnp.max(logits, axis=-1, keepdims=True)
    i1 = jnp.min(jnp.where(logits == v1, lane, LANES), axis=-1, keepdims=True)
    rest = jnp.where(lane == i1, neg, logits)
    v2 = jnp.max(rest, axis=-1, keepdims=True)
    i2 = jnp.min(jnp.where(rest == v2, lane, LANES), axis=-1, keepdims=True)
    e2 = jnp.exp(v2 - v1)
    w1 = 1.0 / (1.0 + e2)
    w2 = e2 / (1.0 + e2)
    chosen = jnp.where((lane == i1) | (lane == i2), 1.0, 0.0)
    r_id = lax.broadcasted_iota(jnp.int32, (shape[0], shape[0]), 0)
    c_id = lax.broadcasted_iota(jnp.int32, (shape[0], shape[0]), 1)
    before = jnp.where(c_id < r_id, 1.0, 0.0).astype(BF16)
    rank = _dot(before, chosen.astype(BF16)) + count_ref[...]
    rank1 = jnp.sum(jnp.where(lane == i1, rank, 0.0), axis=-1, keepdims=True)
    rank2 = jnp.sum(jnp.where(lane == i2, rank, 0.0), axis=-1, keepdims=True)
    count_ref[...] += jnp.sum(chosen, axis=0, keepdims=True)
    rec = jnp.zeros(shape, F32)
    for k, val in ((R_E1, i1.astype(F32)), (R_E2, i2.astype(F32)), (R_W1, w1), (R_W2, w2),
                   (R_RANK1, rank1), (R_RANK2, rank2)):
        rec = jnp.where(lane == k, val, rec)
    return rec


def _mix_out_kernel(*refs, route):
    (x_ref, r_ref, a_ref, gr_ref, ga_ref, mod_ref, n2_ref, wr_ref, wa_ref, wo_ref) = refs[:10]
    if route:
        wrt_ref, brt_ref, x1_ref, h2_ref, rec_ref, cnt_ref, count_ref, hbuf, hsem = refs[10:]
    else:
        x1_ref, h2_ref = refs[10:]
    merged = (_sigmoid(gr_ref[...].astype(F32)) * _dot(r_ref[...].astype(BF16), wr_ref[...])
              + _sigmoid(ga_ref[...].astype(F32)) * _dot(a_ref[...], wa_ref[...]))
    mix = _dot(merged.astype(BF16), wo_ref[...])
    x1 = x_ref[...] + mod_ref[2:3, :] * mix
    x1_ref[...] = x1
    h2 = _rms(x1, n2_ref[...]) * (1.0 + mod_ref[4:5, :]) + mod_ref[3:4, :]
    if not route:
        h2_ref[...] = h2.astype(BF16)
        return

    @pl.when(pl.program_id(0) == 0)
    def _():
        count_ref[...] = jnp.zeros_like(count_ref)

    rec_ref[...] = _route(h2, wrt_ref, brt_ref, count_ref)
    cnt_ref[...] = count_ref[...]

    step = pl.program_id(0)
    slot = step % 2

    def h_copies(k, s):
        rows = pl.ds(k * TM, TM)
        return [pltpu.make_async_copy(hbuf.at[s, :, c * LANES:(c + 1) * LANES], h2_ref.at[rows, c],
                                      hsem.at[s]) for c in range(D_MODEL // LANES)]

    @pl.when(step >= 2)
    def _():
        for cp in h_copies(step - 2, slot):
            cp.wait()

    hbuf[slot] = h2
    for cp in h_copies(step, slot):
        cp.start()

    @pl.when(step == pl.num_programs(0) - 1)
    def _():
        for cp in h_copies(step, slot):
            cp.wait()

        @pl.when(step >= 1)
        def _():
            for cp in h_copies(step - 1, 1 - slot):
                cp.wait()


def _mix_out(x, out_r, out_a, gr, ga, mods, l, norm2, wr_bf, wa_bf, wo_bf, wl, seq_len, lat,
             router=None):
    n = x.shape[0]
    tiles_per_seq = seq_len // TM
    if lat:
        mod_idx = lambda i: (l, 1 + i // tiles_per_seq, 0, 0)
    else:
        mod_idx = lambda i: (l, 0, 0, 0)
    row = lambda i: (i, 0)
    wspec = pl.BlockSpec((None, D_MODEL, D_MODEL), lambda i: (wl, 0, 0), pipeline_mode=RESIDENT)
    in_specs = [pl.BlockSpec((TM, D_MODEL), row)] * 5 + [
        pl.BlockSpec((None, None, 6, D_MODEL), mod_idx),
        pl.BlockSpec((None, 1, D_MODEL), lambda i: (l, 0, 0)),
        wspec, wspec, wspec]
    args = [x, out_r, out_a, gr, ga, mods, norm2, wr_bf, wa_bf, wo_bf]
    out_specs = [pl.BlockSpec((TM, D_MODEL), row)]
    out_shape = [jax.ShapeDtypeStruct((n, D_MODEL), F32)]
    scratch = []
    if router is None:
        out_specs.append(pl.BlockSpec((TM, D_MODEL), row))
        out_shape.append(jax.ShapeDtypeStruct((n, D_MODEL), BF16))
    else:
        j, wr_t, br_pad = router
        in_specs += [pl.BlockSpec((None, D_MODEL, 2 * LANES), lambda i: (j, 0, 0)),
                     pl.BlockSpec((None, 1, LANES), lambda i: (j, 0, 0))]
        args += [wr_t, br_pad]
        out_specs += [pl.BlockSpec(memory_space=pl.ANY),
                      pl.BlockSpec((TM, LANES), row),
                      pl.BlockSpec((1, LANES), lambda i: (0, 0))]
        out_shape += [jax.ShapeDtypeStruct((n, D_MODEL // LANES, LANES), F32),
                      jax.ShapeDtypeStruct((n, LANES), F32),
                      jax.ShapeDtypeStruct((1, LANES), F32)]
        scratch = [pltpu.VMEM((1, LANES), F32), pltpu.VMEM((2, TM, D_MODEL), F32),
                   pltpu.SemaphoreType.DMA((2,))]
    return pl.pallas_call(
        functools.partial(_mix_out_kernel, route=router is not None),
        grid=(n // TM,),
        in_specs=in_specs,
        out_specs=out_specs,
        out_shape=out_shape,
        scratch_shapes=scratch,
        compiler_params=_cparams(("arbitrary",)),
        name=("mix_route" if router is not None else "mix_out") + ("_lat" if lat else "_ctx"),
    )(*args)


def _ffn_kernel(x_ref, h_ref, mod_ref, wg_ref, wu_ref, wd_ref, o_ref, *, f_chunk):
    h = h_ref[...]
    d_ff = wg_ref.shape[1]
    acc = jnp.zeros((h.shape[0], D_MODEL), F32)
    for c in range(d_ff // f_chunk):
        sl = slice(c * f_chunk, (c + 1) * f_chunk)
        act = _silu(_dot(h, wg_ref[:, sl])) * _dot(h, wu_ref[:, sl])
        acc = acc + _dot(act.astype(BF16), wd_ref[sl, :])
    o_ref[...] = x_ref[...] + mod_ref[5:6, :] * acc


def _ffn(x1, h2, mods, l, j, wg_bf, wu_bf, wd_bf, seq_len, lat):
    n = x1.shape[0]
    d_ff = wg_bf.shape[2]
    tiles_per_seq = seq_len // TM
    if lat:
        mod_idx = lambda i: (l, 1 + i // tiles_per_seq, 0, 0)
    else:
        mod_idx = lambda i: (l, 0, 0, 0)
    row = lambda i: (i, 0)
    return pl.pallas_call(
        functools.partial(_ffn_kernel, f_chunk=d_ff // 2),
        grid=(n // TM,),
        in_specs=[pl.BlockSpec((TM, D_MODEL), row), pl.BlockSpec((TM, D_MODEL), row),
                  pl.BlockSpec((None, None, 6, D_MODEL), mod_idx),
                  pl.BlockSpec((None, D_MODEL, d_ff), lambda i: (j, 0, 0), pipeline_mode=RESIDENT),
                  pl.BlockSpec((None, D_MODEL, d_ff), lambda i: (j, 0, 0), pipeline_mode=RESIDENT),
                  pl.BlockSpec((None, d_ff, D_MODEL), lambda i: (j, 0, 0), pipeline_mode=RESIDENT)],
        out_specs=pl.BlockSpec((TM, D_MODEL), row),
        out_shape=jax.ShapeDtypeStruct((n, D_MODEL), F32),
        compiler_params=_cparams(("arbitrary",)),
        name="ffn_lat" if lat else "ffn_ctx",
    )(x1, h2, mods, wg_bf, wu_bf, wd_bf)


def _group_layout(rec, counts, n_tiles):
    cnt = counts[0, :N_EXPERTS].astype(jnp.int32)
    tiles = (cnt + TM_GROUP - 1) // TM_GROUP
    cum = jnp.cumsum(tiles)
    start = (cum - tiles) * TM_GROUP
    experts = jnp.arange(N_EXPERTS, dtype=jnp.int32)

    assert (R_E2, R_RANK2) == (R_E1 + 1, R_RANK1 + 1)
    e12 = rec[:, R_E1:R_E2 + 1].astype(jnp.int32)
    rank12 = rec[:, R_RANK1:R_RANK2 + 1].astype(jnp.int32)
    group_start = jnp.sum(jnp.where(e12[:, :, None] == experts, start, 0), axis=2)
    pos2 = (group_start + rank12).T
    n_active = cum[N_EXPERTS - 1]
    tile_expert = jnp.sum(jnp.arange(n_tiles, dtype=jnp.int32)[:, None] >= cum[None, :], axis=1)
    tile_expert = jnp.minimum(tile_expert, tile_expert[n_active - 1]).astype(jnp.int32)
    tile_ids = jnp.arange(n_tiles, dtype=jnp.int32)
    group_last = jnp.any((tile_ids[:, None] == cum[None, :] - 1) & (tiles[None, :] > 0), axis=1)
    zero_tile = (group_last | (tile_ids >= n_active)).astype(jnp.int32)
    return (pos2.astype(jnp.int32), tile_expert, n_active.reshape(1).astype(jnp.int32), zero_tile)


def _row_copy(src, dst, sem):
    return pltpu.make_async_copy(src, dst, sem)


def _dispatch_kernel(pos_ref, zt_ref, h_ref, xs_ref, zbuf, sem, zsem):
    n_chunks = h_ref.shape[0] // DMA_CHUNK
    n_tiles = xs_ref.shape[0] // TM_GROUP

    zbuf[...] = jnp.zeros_like(zbuf)

    def zero_fill(j):
        return pltpu.make_async_copy(zbuf, xs_ref.at[pl.ds(j * TM_GROUP, TM_GROUP)], zsem)

    def zero_start(j, carry):
        @pl.when(zt_ref[j] != 0)
        def _():
            zero_fill(j).start()
        return carry

    def zero_wait(j, carry):
        @pl.when(zt_ref[j] != 0)
        def _():
            zero_fill(j).wait()
        return carry

    lax.fori_loop(0, n_tiles, zero_start, 0)
    lax.fori_loop(0, n_tiles, zero_wait, 0)

    def drain_chunk():
        rows = pl.ds(0, 2 * DMA_CHUNK)
        _row_copy(xs_ref.at[rows], xs_ref.at[rows], sem).wait()

    def chunk(k, carry):
        for r in range(DMA_CHUNK):
            t = k * DMA_CHUNK + r
            _row_copy(h_ref.at[t], xs_ref.at[pos_ref[0, t]], sem).start(priority=0)
            _row_copy(h_ref.at[t], xs_ref.at[pos_ref[1, t]], sem).start(priority=1)

        @pl.when(k > 0)
        def _():
            drain_chunk()

        return carry

    lax.fori_loop(0, n_chunks, chunk, 0)
    drain_chunk()


def _dispatch(pos2, zero_tile, h2_rows, n_rows):
    tile = h2_rows.shape[1:]
    return pl.pallas_call(
        _dispatch_kernel,
        grid_spec=pltpu.PrefetchScalarGridSpec(
            num_scalar_prefetch=2, grid=(1,),
            in_specs=[pl.BlockSpec(h2_rows.shape, lambda i, p, z: (0, 0, 0))],
            out_specs=pl.BlockSpec(memory_space=pl.ANY),
            scratch_shapes=[pltpu.VMEM((TM_GROUP,) + tile, F32),
                            pltpu.SemaphoreType.DMA(()), pltpu.SemaphoreType.DMA(())]),
        out_shape=jax.ShapeDtypeStruct((n_rows,) + tile, F32),
        compiler_params=_cparams(("arbitrary",)),
        name="moe_dispatch",
    )(pos2, zero_tile, h2_rows)


def _expert_kernel(te_ref, na_ref, xs_ref, wgu_ref, wd_ref, ys_ref,
                   xbuf, ybuf, sem_in, sem_out):
    del te_ref
    i = pl.program_id(0)
    last = pl.num_programs(0) - 1
    n_active = na_ref[0]
    slot = i % 2
    n_col = D_MODEL // LANES

    def in_copies(tile, s):
        rows = pl.ds(tile * TM_GROUP, TM_GROUP)
        return [pltpu.make_async_copy(xs_ref.at[rows, c], xbuf.at[s, :, c * LANES:(c + 1) * LANES],
                                      sem_in.at[s]) for c in range(n_col)]

    def out_copies(tile, s):
        rows = pl.ds(tile * TM_GROUP, TM_GROUP)
        return [pltpu.make_async_copy(ybuf.at[s, :, c * LANES:(c + 1) * LANES], ys_ref.at[rows, c],
                                      sem_out.at[s]) for c in range(n_col)]

    @pl.when(i == 0)
    def _():
        for cp in in_copies(0, 0):
            cp.start()

    @pl.when(i + 1 < n_active)
    def _():
        for cp in in_copies(i + 1, 1 - slot):
            cp.start()

    @pl.when(i >= 2)
    def _():
        for cp in out_copies(i - 2, slot):
            cp.wait()

    @pl.when(i < n_active)
    def _():
        for cp in in_copies(i, slot):
            cp.wait()
        xb = xbuf[slot].astype(BF16)
        z = _dot(xb, wgu_ref[...])
        d_e = wd_ref.shape[0]
        act = _silu(z[:, :d_e]) * z[:, d_e:]
        ybuf[slot] = _dot(act.astype(BF16), wd_ref[...])

    @pl.when(i >= n_active)
    def _():
        ybuf[slot] = jnp.zeros((TM_GROUP, D_MODEL), F32)

    for cp in out_copies(i, slot):
        cp.start()

    @pl.when(i == last)
    def _():
        for cp in out_copies(i, slot):
            cp.wait()

        @pl.when(i >= 1)
        def _():
            for cp in out_copies(i - 1, 1 - slot):
                cp.wait()


def _experts(tile_expert, n_active, xs, j, wgu_bf, wd_bf):
    n_rows = xs.shape[0]
    d_e = wd_bf.shape[2]
    wspec = lambda shape: pl.BlockSpec((None, None) + shape, lambda i, te, na: (j, te[i], 0, 0))
    return pl.pallas_call(
        _expert_kernel,
        grid_spec=pltpu.PrefetchScalarGridSpec(
            num_scalar_prefetch=2, grid=(n_rows // TM_GROUP,),
            in_specs=[pl.BlockSpec(memory_space=pl.ANY),
                      wspec((D_MODEL, 2 * d_e)), wspec((d_e, D_MODEL))],
            out_specs=pl.BlockSpec(memory_space=pl.ANY),
            scratch_shapes=[pltpu.VMEM((2, TM_GROUP, D_MODEL), F32),
                            pltpu.VMEM((2, TM_GROUP, D_MODEL), F32),
                            pltpu.SemaphoreType.DMA((2,)), pltpu.SemaphoreType.DMA((2,))]),
        out_shape=jax.ShapeDtypeStruct(xs.shape, F32),
        compiler_params=_cparams(("arbitrary",)),
        name="moe_experts",
    )(tile_expert, n_active, xs, wgu_bf, wd_bf)


def _combine_kernel(*refs, final):
    pos_ref, y_ref, x_ref, rec_ref, mod_ref = refs[:5]
    g_ref = refs[5] if final else None
    o_ref, a1, a2, b1, b2, sem_a, sem_b = refs[5 + int(final):]
    i = pl.program_id(0)

    def issue(half_idx, d1, d2, sem):
        base = half_idx * DMA_CHUNK

        for r in range(DMA_CHUNK):
            _row_copy(y_ref.at[pos_ref[0, base + r]], d1.at[r], sem).start(priority=1)
            _row_copy(y_ref.at[pos_ref[1, base + r]], d2.at[r], sem).start(priority=1)

    def drain(d1, d2, sem):
        rows = pl.ds(0, DMA_CHUNK)
        _row_copy(y_ref.at[rows], d1, sem).wait()
        _row_copy(y_ref.at[rows], d2, sem).wait()

    def finish(half, d1, d2):
        rows = slice(half * DMA_CHUNK, (half + 1) * DMA_CHUNK)
        w1 = rec_ref[rows, R_W1:R_W1 + 1]
        w2 = rec_ref[rows, R_W2:R_W2 + 1]
        for c in range(D_MODEL // LANES):
            sl = slice(c * LANES, (c + 1) * LANES)
            y = w1 * d1[:, c, :] + w2 * d2[:, c, :]
            o_ref[rows, sl] = x_ref[rows, sl] + mod_ref[5:6, sl] * y
        if final:
            o_ref[rows, :] = _rms(o_ref[rows, :], g_ref[...])

    @pl.when(i == 0)
    def _():
        issue(0, a1, a2, sem_a)

    issue(2 * i + 1, b1, b2, sem_b)
    drain(a1, a2, sem_a)
    finish(0, a1, a2)

    @pl.when(i + 1 < pl.num_programs(0))
    def _():
        issue(2 * i + 2, a1, a2, sem_a)

    drain(b1, b2, sem_b)
    finish(1, b1, b2)


def _combine(pos2, ys, x1, rec, mods, l, seq_len, lat, final_g):
    n = x1.shape[0]
    final = final_g is not None
    tile = ys.shape[1:]
    tm = 2 * DMA_CHUNK
    tiles_per_seq = max(seq_len // tm, 1)
    if lat:
        mod_idx = lambda i, p: (l, 1 + i // tiles_per_seq, 0, 0)
    else:
        mod_idx = lambda i, p: (l, 0, 0, 0)
    row = lambda i, p: (i, 0)
    slot = pltpu.VMEM((DMA_CHUNK,) + tile, F32)
    in_specs = [pl.BlockSpec(memory_space=pl.ANY),
                pl.BlockSpec((tm, D_MODEL), row),
                pl.BlockSpec((tm, LANES), row),
                pl.BlockSpec((None, None, 6, D_MODEL), mod_idx)]
    args = [pos2, ys, x1, rec, mods]
    if final:
        in_specs.append(pl.BlockSpec((1, D_MODEL), lambda i, p: (0, 0)))
        args.append(final_g)
    return pl.pallas_call(
        functools.partial(_combine_kernel, final=final),
        grid_spec=pltpu.PrefetchScalarGridSpec(
            num_scalar_prefetch=1, grid=(n // tm,),
            in_specs=in_specs,
            out_specs=pl.BlockSpec((tm, D_MODEL), row),
            scratch_shapes=[slot, slot, slot, slot,
                            pltpu.SemaphoreType.DMA(()), pltpu.SemaphoreType.DMA(())]),
        out_shape=jax.ShapeDtypeStruct((n, D_MODEL), F32),
        compiler_params=_cparams(("arbitrary",)),
        name="moe_combine_final" if final else "moe_combine",
    )(*args)


def _moe(x1, h2_rows, rec, counts, mods, l, j, wgu_bf, wd_bf, seq_len, lat, final_g):
    n = x1.shape[0]
    n_tiles = 2 * n // TM_GROUP + N_EXPERTS
    pos2, tile_expert, n_active, zero_tile = _group_layout(rec, counts, n_tiles)
    xs = _dispatch(pos2, zero_tile, h2_rows, n_tiles * TM_GROUP)
    ys = _experts(tile_expert, n_active, xs, j, wgu_bf, wd_bf)
    return _combine(pos2, ys, x1, rec, mods, l, seq_len, lat, final_g)


def _final_norm_kernel(x_ref, g_ref, o_ref):
    o_ref[...] = _rms(x_ref[...], g_ref[...])


def _final_norm(x, g):
    n = x.shape[0]
    row = lambda i: (i, 0)
    return pl.pallas_call(
        _final_norm_kernel,
        grid=(n // TM_FF,),
        in_specs=[pl.BlockSpec((TM_FF, D_MODEL), row), pl.BlockSpec((1, D_MODEL), lambda i: (0, 0))],
        out_specs=pl.BlockSpec((TM_FF, D_MODEL), row),
        out_shape=jax.ShapeDtypeStruct((n, D_MODEL), F32),
        compiler_params=_cparams(("arbitrary",)),
        name="final_norm",
    )(x, g)


def _rope_tables(t_len):
    t = jnp.arange(t_len)
    pos = jnp.stack([t // GRID_W, t % GRID_W], axis=-1).astype(F32)
    inv = ROPE_THETA ** (-jnp.arange(0, AXIS_DIM, 2, dtype=F32) / AXIS_DIM)
    ang = pos[..., None] * inv
    cos, sin = jnp.cos(ang), jnp.sin(ang)
    c = jnp.concatenate([cos[:, 0], cos[:, 0], cos[:, 1], cos[:, 1]], axis=-1)
    s = jnp.concatenate([-sin[:, 0], sin[:, 0], -sin[:, 1], sin[:, 1]], axis=-1)
    return c, s


def _gate_weights(wa, wi):
    depth = wa.shape[0]
    groups = D_RNN // LANES

    def pair(w):
        w = w.reshape(depth, 2, groups, 2, RNN_BLOCK, RNN_BLOCK)
        z = jnp.zeros_like(w[:, :, :, 0])
        top = jnp.concatenate([w[:, :, :, 0], z], axis=-1)
        bot = jnp.concatenate([z, w[:, :, :, 1]], axis=-1)
        return jnp.concatenate([top, bot], axis=-2)

    w = jnp.concatenate([pair(wa), pair(wi)], axis=-1)
    return (0.5 * jnp.concatenate([w[:, 0], w[:, 1]], axis=-1)).astype(BF16)


def kernel(x_prompt, x_sample, cache_k, cache_v, state_rnn, c, c_ctx, w_mod, b_mod, norm1, norm2, w_in, conv_w, conv_b, lru_wa, lru_ba, lru_wi, lru_bi, lru_lambda, q_norm, k_norm, w_rnn_out, w_attn_out, w_out, ff_gate, ff_up, ff_down, router_w, router_b, exp_gate, exp_up, exp_down, final_norm):
    batch, seq, _ = x_prompt.shape
    dec_batch, dec_seq, _ = x_sample.shape
    depth = w_mod.shape[0]
    past = cache_k.shape[2]
    assert dec_batch + 1 <= MOD_ROWS

    cond8 = jnp.zeros((MOD_ROWS, D_MODEL), F32).at[0].set(c_ctx).at[1:1 + dec_batch].set(c)

    gate_w = _gate_weights(lru_wa, lru_wi)
    n_moe = router_w.shape[0]
    wr_pad = jnp.zeros((n_moe, D_MODEL, LANES), F32).at[:, :, :N_EXPERTS].set(router_w)
    wr_head = wr_pad.astype(BF16)
    wr_t = jnp.concatenate([wr_head, (wr_pad - wr_head.astype(F32)).astype(BF16)], axis=-1)
    br_pad = jnp.zeros((n_moe, 1, LANES), F32).at[:, 0, :N_EXPERTS].set(router_b)

    norm1_3 = norm1.reshape(depth, 1, D_MODEL)
    norm2_3 = norm2.reshape(depth, 1, D_MODEL)
    qn3 = q_norm.reshape(depth, 1, HEAD_DIM)
    kn3 = k_norm.reshape(depth, 1, HEAD_DIM)
    conv_b3 = conv_b.reshape(depth, 1, D_RNN)
    ba4 = (0.5 * lru_ba).reshape(depth, 2, 1, D_RNN)
    bi4 = (0.5 * lru_bi).reshape(depth, 2, 1, D_RNN)
    lam4 = lru_lambda.reshape(depth, 2, 1, D_RNN)
    state5 = state_rnn.reshape(dec_batch, depth, 2, 1, D_RNN)
    cache_k4 = cache_k.reshape(dec_batch, depth, past, KV_DIM)
    cache_v4 = cache_v.reshape(dec_batch, depth, past, KV_DIM)
    rope_tabs = _rope_tables(dec_seq)
    final_g = final_norm.reshape(1, D_MODEL)

    n_steps = dec_batch * dec_seq // TQ
    assert n_steps == dec_batch * dec_seq // TM_ROPE

    def flat(w):
        return w.reshape(-1, w.shape[-1])

    def channel_names(l):
        return ("exp_gate_up", "exp_down") if l % 2 == 1 else ("ff_gate", "ff_up", "ff_down")

    stacked = dict(w_in=(w_in,), w_rnn_out=(w_rnn_out,), w_attn_out=(w_attn_out,), w_out=(w_out,),
                   ff_gate=(ff_gate,), ff_up=(ff_up,), ff_down=(ff_down,),
                   exp_gate_up=(exp_gate, exp_up), exp_down=(exp_down,))

    def layer_shape(name):
        parts = stacked[name]
        return (1,) + parts[0].shape[1:-1] + (sum(w.shape[-1] for w in parts),)

    def layer_index(name, l):
        return l if name.startswith("w_") else l // 2

    def rider(name, l, steps):
        parts = stacked[name]
        rows = flat(parts[0]).shape[0] // parts[0].shape[0] // steps
        return _Rider([flat(w) for w in parts], rows, layer_index(name, l) * steps)

    def names_of(l):
        return ("w_in", "w_rnn_out", "w_attn_out", "w_out") + channel_names(l)

    weights = {l: {} for l in range(depth)}

    def keep(l, names, arrays):
        for name, arr in zip(names, arrays):
            weights[l][name] = arr.reshape(layer_shape(name))

    mod_steps = _modulation_steps(depth)
    mods, w_in0 = _modulation(cond8, w_mod, b_mod, riders=[rider("w_in", 0, mod_steps)])
    mods = mods.reshape(depth, MOD_ROWS, 6, D_MODEL)
    keep(0, ("w_in",), [w_in0])
    ctx_steps = batch * seq // TM

    def run_layer(x, l, lat):
        seq_len = dec_seq if lat else seq
        wts = weights[l]
        convert_next = lat and l + 1 < depth
        if convert_next:
            proj_names = names_of(l + 1)[:4]
            attn_names = names_of(l + 1)[4:]
            proj_riders = [rider(name, l + 1, n_steps) for name in proj_names]
            attn_riders = [rider(name, l + 1, n_steps) for name in attn_names]
            target = l + 1
        elif l == 0 and not lat:
            proj_names, attn_names, attn_riders, target = names_of(0)[1:], (), [], 0
            proj_riders = [rider(name, 0, ctx_steps) for name in proj_names]
        else:
            proj_names, attn_names, proj_riders, attn_riders, target = (), (), [], [], None
        outs = _in_proj(x, mods, l, norm1_3, wts["w_in"], 0, qn3, kn3,
                        rope_tabs if lat else None, seq_len, riders=proj_riders,
                        caches=None if lat else new_kv)
        xr, yr, q, _, kr, v, gr, ga = outs[:8]
        if target is not None:
            keep(target, proj_names, outs[8:8 + len(proj_names)])
        if not lat:
            new_kv[:] = outs[8 + len(proj_names):]
        out_r, end_f, end_b = _rnn(xr, yr, l, conv_w, conv_b3, gate_w, ba4, bi4, lam4,
                                   state5 if lat else None, seq_len)
        if lat:
            outs = _attn_lat(q, kr, v, cache_k4, cache_v4, l, seq_len, riders=attn_riders)
            out_a = outs[0]
            if target is not None:
                keep(target, attn_names, outs[1:])
        else:
            out_a = _attn_ctx(q, kr, v, seq_len)
        mix_args = (x, out_r, out_a, gr, ga, mods, l, norm2_3, wts["w_rnn_out"],
                    wts["w_attn_out"], wts["w_out"], 0, seq_len, lat)
        if l % 2 == 1:
            x1, h2_rows, rec, counts = _mix_out(*mix_args, router=(l // 2, wr_t, br_pad))
            x = _moe(x1, h2_rows, rec, counts, mods, l, 0, wts["exp_gate_up"], wts["exp_down"],
                     seq_len, lat, final_g if l == depth - 1 else None)
        else:
            x1, h2 = _mix_out(*mix_args)
            x = _ffn(x1, h2, mods, l, 0, wts["ff_gate"], wts["ff_up"], wts["ff_down"],
                     seq_len, lat)
            if l == depth - 1:
                x = _final_norm(x, final_g)
        return x, jnp.stack([end_f, end_b], axis=1)

    new_kv = [jnp.zeros((batch, depth, seq, N_KV_HEADS, HEAD_DIM), F32) for _ in range(2)]
    y_ctx = x_prompt.reshape(batch * seq, D_MODEL)
    y_lat = x_sample.reshape(dec_batch * dec_seq, D_MODEL)
    ss = []
    for l in range(depth):
        y_ctx, st = run_layer(y_ctx, l, False)
        ss.append(st)
        y_lat, _ = run_layer(y_lat, l, True)

    new_state = jnp.stack(ss, axis=1)
    return (y_ctx.reshape(batch, seq, D_MODEL), y_lat.reshape(dec_batch, dec_seq, D_MODEL),
            new_kv[0], new_kv[1], new_state)
```

```python
import functools

import jax
import jax.numpy as jnp
from jax import lax
from jax.experimental import pallas as pl
from jax.experimental.pallas import tpu as pltpu

F32 = jnp.float32
BF16 = jnp.bfloat16

D_MODEL = 1024
D_RNN = 1024
N_RNN_BLOCKS = 16
RNN_BLOCK = D_RNN // N_RNN_BLOCKS
CONV_W = 4
CONV_LEFT = 2
LRU_C = 8.0
N_HEADS = 8
N_KV_HEADS = 2
HEAD_DIM = 128
AXIS_DIM = HEAD_DIM // 2
ROPE_THETA = 10000.0
GRID_W = 64
Q_DIM = N_HEADS * HEAD_DIM
KV_DIM = N_KV_HEADS * HEAD_DIM
N_EXPERTS = 8
EPS = 1e-6
LOG2_E = 1.4426950408889634
RSQRT_FLOOR = 1e-30
D_IN = 2 * D_RNN + Q_DIM + 2 * KV_DIM + 2 * D_MODEL
O_XR, O_YR, O_Q = 0, D_RNN, 2 * D_RNN
O_K = O_Q + Q_DIM
O_V = O_K + KV_DIM
O_GR = O_V + KV_DIM
O_GA = O_GR + D_MODEL

LANES = 128
SUBLANES = 8
MOD_ROWS = 8
VMEM_LIMIT = 56 * 1024 * 1024

TM = 512
TM_ROPE = 256
RESIDENT = pl.Buffered(1)
RNN_T = 256
RNN_SEG = SUBLANES
RNN_C = 512
RNN_TCH = 32
TQ = 256
ATTN_CTX_SEQS = 2
TM_FF = 512
TM_GROUP = 256
DMA_CHUNK = 256


def _cparams(sem):
    return pltpu.CompilerParams(dimension_semantics=sem, vmem_limit_bytes=VMEM_LIMIT)


def _sigmoid(x):
    return 1.0 / (1.0 + jnp.exp(-x))


def _silu(x):
    return x * _sigmoid(x)


def _gelu_tanh(x):
    return 0.5 * x * (1.0 + jnp.tanh(0.7978845608028654 * (x + 0.044715 * (x * x * x))))


def _rms(x, g):
    ms = jnp.mean(x * x, axis=-1, keepdims=True)
    return x * lax.rsqrt(ms + EPS) * g


def _dot(a, b):
    return jnp.dot(a, b, preferred_element_type=F32)


MOD_TN = 1536


def _mod_kernel(*refs, groups):
    n_rin = sum(groups)
    cond_ref, w_ref, b_ref = refs[:3]
    o_ref = refs[3 + n_rin]
    _convert_riders(refs[3:3 + n_rin], refs[4 + n_rin:], groups)
    s = _silu(cond_ref[...]).astype(BF16)
    o_ref[...] = _dot(s, w_ref[...].astype(BF16)) + b_ref[...]


def _modulation_steps(depth):
    return depth * (6 * D_MODEL // MOD_TN)


def _modulation(cond8, w_mod, b_mod, riders=()):
    depth = w_mod.shape[0]
    per_layer = 6 * D_MODEL // MOD_TN
    r_in, r_out, r_shape = _rider_specs(riders, depth * per_layer, lambda l, j: l * per_layer + j)
    return pl.pallas_call(
        functools.partial(_mod_kernel, groups=_rider_groups(riders)),
        grid=(depth, per_layer),
        in_specs=[
            pl.BlockSpec((MOD_ROWS, D_MODEL), lambda l, j: (0, 0)),
            pl.BlockSpec((None, D_MODEL, MOD_TN), lambda l, j: (l, 0, j)),
            pl.BlockSpec((None, 1, MOD_TN), lambda l, j: (l, 0, j)),
        ] + r_in,
        out_specs=[pl.BlockSpec((None, MOD_ROWS, MOD_TN), lambda l, j: (l, 0, j))] + r_out,
        out_shape=[jax.ShapeDtypeStruct((depth, MOD_ROWS, 6 * D_MODEL), F32)] + r_shape,
        compiler_params=_cparams(("arbitrary", "arbitrary")),
        name="adaln_mod",
    )(cond8, w_mod, b_mod.reshape(depth, 1, 6 * D_MODEL), *_rider_args(riders))


def _swap_halves(x):
    lane = lax.broadcasted_iota(jnp.int32, x.shape, 1)
    return jnp.where((lane % AXIS_DIM) < AXIS_DIM // 2,
                     pltpu.roll(x, LANES - AXIS_DIM // 2, axis=1),
                     pltpu.roll(x, AXIS_DIM // 2, axis=1))


class _Rider:
    def __init__(self, w2d, rows, first):
        self.parts = list(w2d) if isinstance(w2d, (list, tuple)) else [w2d]
        self.rows, self.first = rows, first


def _rider_specs(riders, n_steps, step_of):
    in_specs, out_specs, out_shape = [], [], []
    for r in riders:
        for w in r.parts:
            assert r.rows % 16 == 0 and (r.first + n_steps) * r.rows <= w.shape[0]
            in_specs.append(pl.BlockSpec((r.rows, w.shape[1]),
                                         lambda *g, r=r: (r.first + step_of(*g), 0)))
        cols = sum(w.shape[1] for w in r.parts)
        out_specs.append(pl.BlockSpec((r.rows, cols), lambda *g: (step_of(*g), 0)))
        out_shape.append(jax.ShapeDtypeStruct((r.rows * n_steps, cols), BF16))
    return in_specs, out_specs, out_shape


def _rider_groups(riders):
    return tuple(len(r.parts) for r in riders)


def _rider_args(riders):
    return [w for r in riders for w in r.parts]


def _convert_riders(in_refs, out_refs, groups):
    k = 0
    for o_ref, n_parts in zip(out_refs, groups):
        col = 0
        for w_ref in in_refs[k:k + n_parts]:
            o_ref[:, col:col + w_ref.shape[1]] = w_ref[...].astype(BF16)
            col += w_ref.shape[1]
        k += n_parts


def _in_proj_kernel(*refs, rope, groups, cache_layer, seq_len):
    n_in = 8 if rope else 6
    n_rin, n_rout = sum(groups), len(groups)
    n_cache = 0 if cache_layer is None else 2
    first_out = n_in + n_rin + n_cache
    _convert_riders(refs[n_in:n_in + n_rin], refs[first_out + 8:first_out + 8 + n_rout], groups)
    cache_refs = refs[first_out + 8 + n_rout:]
    refs = refs[:n_in] + refs[first_out:first_out + 8]
    if rope:
        (x_ref, mod_ref, n1_ref, w_ref, qn_ref, kn_ref, c_ref, s_ref,
         xr_ref, yr_ref, q_ref, kno_ref, kro_ref, v_ref, gr_ref, ga_ref) = refs
    else:
        (x_ref, mod_ref, n1_ref, w_ref, qn_ref, kn_ref,
         xr_ref, yr_ref, q_ref, kno_ref, kro_ref, v_ref, gr_ref, ga_ref) = refs
    x = x_ref[...]
    h = _rms(x, n1_ref[...]) * (1.0 + mod_ref[1:2, :]) + mod_ref[0:1, :]
    hb = h.astype(BF16)

    xr_ref[...] = _dot(hb, w_ref[:, O_XR:O_XR + D_RNN])
    yr_ref[...] = _dot(hb, w_ref[:, O_YR:O_YR + D_RNN])
    gr_ref[...] = _dot(hb, w_ref[:, O_GR:O_GR + D_MODEL]).astype(BF16)
    ga_ref[...] = _dot(hb, w_ref[:, O_GA:O_GA + D_MODEL]).astype(BF16)
    v_ref[...] = _dot(hb, w_ref[:, O_V:O_V + KV_DIM])

    if rope:
        cs, sn = c_ref[...], s_ref[...]
    zq = _dot(hb, w_ref[:, O_Q:O_Q + Q_DIM])
    for hd in range(N_HEADS):
        t = _rms(zq[:, hd * HEAD_DIM:(hd + 1) * HEAD_DIM], qn_ref[...])
        if rope:
            t = t * cs + _swap_halves(t) * sn
        q_ref[:, hd * HEAD_DIM:(hd + 1) * HEAD_DIM] = t.astype(BF16)
    zk = _dot(hb, w_ref[:, O_K:O_K + KV_DIM])
    for hd in range(N_KV_HEADS):
        t = _rms(zk[:, hd * HEAD_DIM:(hd + 1) * HEAD_DIM], kn_ref[...])
        kno_ref[:, hd * HEAD_DIM:(hd + 1) * HEAD_DIM] = t
        if rope:
            t = t * cs + _swap_halves(t) * sn
        kro_ref[:, hd * HEAD_DIM:(hd + 1) * HEAD_DIM] = t.astype(BF16)

    if cache_layer is None:
        return
    newk_ref, newv_ref, kbuf, vbuf, csem = cache_refs
    step = pl.program_id(0)
    slot = step % 2
    seqs_per_tile = x_ref.shape[0] // seq_len

    def cache_copies(k, s):
        cps = []
        for sq in range(seqs_per_tile):
            rows = slice(sq * seq_len, (sq + 1) * seq_len)
            for hd in range(N_KV_HEADS):
                cols = slice(hd * HEAD_DIM, (hd + 1) * HEAD_DIM)
                dst = (k * seqs_per_tile + sq, cache_layer, slice(None), hd, slice(None))
                cps.append(pltpu.make_async_copy(kbuf.at[s, rows, cols], newk_ref.at[dst], csem.at[s]))
                cps.append(pltpu.make_async_copy(vbuf.at[s, rows, cols], newv_ref.at[dst], csem.at[s]))
        return cps

    @pl.when(step >= 2)
    def _():
        for cp in cache_copies(step - 2, slot):
            cp.wait()

    kbuf[slot] = kno_ref[...]
    vbuf[slot] = v_ref[...]
    for cp in cache_copies(step, slot):
        cp.start()

    @pl.when(step == pl.num_programs(0) - 1)
    def _():
        for cp in cache_copies(step, slot):
            cp.wait()

        @pl.when(step >= 1)
        def _():
            for cp in cache_copies(step - 1, 1 - slot):
                cp.wait()


def _in_proj(x, mods, l, norm1, w_in_bf, wl, q_norm, k_norm, rope_tabs, seq_len, riders=(),
             caches=None):
    n = x.shape[0]
    rope = rope_tabs is not None
    tm = TM_ROPE if rope else TM
    tiles_per_seq = seq_len // tm
    if rope:
        mod_idx = lambda i: (l, 1 + i // tiles_per_seq, 0, 0)
    else:
        mod_idx = lambda i: (l, 0, 0, 0)
    row = lambda i: (i, 0)
    in_specs = [
        pl.BlockSpec((tm, D_MODEL), row),
        pl.BlockSpec((None, None, 6, D_MODEL), mod_idx),
        pl.BlockSpec((None, 1, D_MODEL), lambda i: (l, 0, 0)),
        pl.BlockSpec((None, D_MODEL, D_IN), lambda i: (wl, 0, 0), pipeline_mode=RESIDENT),
        pl.BlockSpec((None, 1, HEAD_DIM), lambda i: (l, 0, 0)),
        pl.BlockSpec((None, 1, HEAD_DIM), lambda i: (l, 0, 0)),
    ]
    args = [x, mods, norm1, w_in_bf, q_norm, k_norm]
    if rope:
        in_specs += [pl.BlockSpec((tm, HEAD_DIM), lambda i: (i % tiles_per_seq, 0))] * 2
        args += list(rope_tabs)
    widths = [(D_RNN, F32), (D_RNN, F32), (Q_DIM, BF16), (KV_DIM, F32), (KV_DIM, BF16),
              (KV_DIM, F32), (D_MODEL, BF16), (D_MODEL, BF16)]
    r_in, r_out, r_shape = _rider_specs(riders, n // tm, lambda i: i)
    in_specs = in_specs + r_in
    out_specs = [pl.BlockSpec((tm, w), row) for w, _ in widths] + r_out
    out_shape = [jax.ShapeDtypeStruct((n, w), dt) for w, dt in widths] + r_shape
    args = args + _rider_args(riders)
    scratch, aliases = [], {}
    if caches is not None:
        assert tm % seq_len == 0
        for c in caches:
            aliases[len(args)] = len(out_shape)
            args.append(c)
            in_specs.append(pl.BlockSpec(memory_space=pl.ANY))
            out_specs.append(pl.BlockSpec(memory_space=pl.ANY))
            out_shape.append(jax.ShapeDtypeStruct(c.shape, c.dtype))
        scratch = [pltpu.VMEM((2, tm, KV_DIM), F32), pltpu.VMEM((2, tm, KV_DIM), F32),
                   pltpu.SemaphoreType.DMA((2,))]
    return pl.pallas_call(
        functools.partial(_in_proj_kernel, rope=rope, groups=_rider_groups(riders),
                          cache_layer=None if caches is None else l, seq_len=seq_len),
        grid=(n // tm,),
        in_specs=in_specs,
        out_specs=out_specs,
        out_shape=out_shape,
        scratch_shapes=scratch,
        input_output_aliases=aliases,
        compiler_params=_cparams(("arbitrary",)),
        name="in_proj_lat" if rope else "in_proj_ctx",
    )(*args)


def _rnn_kernel(*refs, chained):
    refs = list(refs)
    xr_ref, yr_ref, cw_ref, cb_ref, gw_ref, ba_ref, bi_ref, lam_ref = refs[:8]
    refs = refs[8:]
    h0_ref = refs.pop(0) if chained else None
    out_ref, sf_ref, sb_ref, xext, ybuf, obuf, a_f, b_f, a_b, b_b, sem_in, sem_out = refs

    g = pl.program_id(0)
    cb = pl.program_id(1)
    n_cb = pl.num_programs(1)
    step = g * n_cb + cb
    n_steps = pl.num_programs(0) * n_cb
    slot = step % 2

    def hbm_window(kk, s):
        rows = pl.ds(pl.multiple_of(((kk // n_cb) * RNN_SEG + s) * RNN_T, RNN_T), RNN_T)
        cols = pl.ds(pl.multiple_of((kk % n_cb) * RNN_C, RNN_C), RNN_C)
        return rows, cols

    def in_copies(kk, sl_):
        cps = []
        for s in range(RNN_SEG):
            win = hbm_window(kk, s)
            cps.append(pltpu.make_async_copy(
                xr_ref.at[win], xext.at[sl_, pl.ds(CONV_LEFT, RNN_T), s], sem_in.at[sl_]))
            cps.append(pltpu.make_async_copy(yr_ref.at[win], ybuf.at[sl_, :, s], sem_in.at[sl_]))
        return cps

    def out_copies(kk, sl_):
        return [pltpu.make_async_copy(obuf.at[sl_, :, s], out_ref.at[hbm_window(kk, s)],
                                      sem_out.at[sl_]) for s in range(RNN_SEG)]

    @pl.when(step == 0)
    def _():
        for cp in in_copies(0, 0):
            cp.start()

    @pl.when(step + 1 < n_steps)
    def _():
        for cp in in_copies(step + 1, 1 - slot):
            cp.start()

    for cp in in_copies(step, slot):
        cp.wait()

    seg = lax.broadcasted_iota(jnp.int32, (RNN_SEG, RNN_C), 0)
    zero_row = jnp.zeros((RNN_SEG, RNN_C), F32)
    if chained:
        xext[slot, 0] = jnp.where(seg == 0, zero_row, pltpu.roll(xext[slot, RNN_T], 1, axis=0))
        xext[slot, 1] = jnp.where(seg == 0, zero_row, pltpu.roll(xext[slot, RNN_T + 1], 1, axis=0))
        xext[slot, RNN_T + CONV_LEFT] = jnp.where(
            seg == RNN_SEG - 1, zero_row, pltpu.roll(xext[slot, CONV_LEFT], RNN_SEG - 1, axis=0))
    else:
        xext[slot, 0] = zero_row
        xext[slot, 1] = zero_row
        xext[slot, RNN_T + CONV_LEFT] = zero_row

    scans = ((a_f, b_f), (a_b, b_b))
    rows2d = RNN_TCH * RNN_SEG

    def gate_chunk(tc, carry):
        t0 = pl.multiple_of(tc * RNN_TCH, RNN_TCH)
        for j in range(RNN_C // LANES):
            sl = slice(j * LANES, (j + 1) * LANES)
            xc = jnp.broadcast_to(cb_ref[:, sl].reshape(1, 1, LANES), (RNN_TCH, RNN_SEG, LANES))
            for k in range(CONV_W):
                xc = xc + (xext[slot, pl.ds(t0 + k, RNN_TCH), :, sl]
                           * cw_ref[k:k + 1, sl].reshape(1, 1, LANES))
            xc = xc.reshape(rows2d, LANES)
            pre = _dot(xc.astype(BF16), gw_ref[j])
            half_xc = 0.5 * xc
            for d, (a_s, b_s) in enumerate(scans):
                lam = lam_ref[d, :, sl]
                log2_decay = (-LRU_C * LOG2_E) * (jnp.maximum(-lam, 0.0)
                                                  + jnp.log(1.0 + jnp.exp(-jnp.abs(lam))))
                t_r = jnp.tanh(pre[:, 2 * d * LANES:(2 * d + 1) * LANES] + ba_ref[d, :, sl])
                t_i = jnp.tanh(pre[:, (2 * d + 1) * LANES:(2 * d + 2) * LANES]
                               + bi_ref[d, :, sl])
                half_decay = 0.5 * log2_decay
                a = jnp.exp2(t_r * half_decay + half_decay)
                y = 1.0 - a * a
                b = (y * lax.rsqrt(jnp.maximum(y, RSQRT_FLOOR))) * ((t_i + 1.0) * half_xc)
                a_s[pl.ds(t0, RNN_TCH), :, sl] = a.reshape(RNN_TCH, RNN_SEG, LANES)
                b_s[pl.ds(t0, RNN_TCH), :, sl] = b.reshape(RNN_TCH, RNN_SEG, LANES)
        return carry

    n_chunks = RNN_T // RNN_TCH
    ones = jnp.ones((RNN_SEG, RNN_C), F32)

    def fwd_scan_chunk(tc, hf, pf):
        for u in range(RNN_TCH):
            t = tc * RNN_TCH + u
            af_t = a_f[t]
            hf = af_t * hf + b_f[t]
            b_f[t] = hf
            if chained:
                pf = af_t * pf
                a_f[t] = pf
        return hf, pf

    def gate_and_scan(tc, carry):
        carry = fwd_scan_chunk(tc - 1, *carry)
        gate_chunk(tc, 0)
        return carry

    gate_chunk(0, 0)
    hf, pf = lax.fori_loop(1, n_chunks, gate_and_scan, (zero_row, ones))
    hf, pf = fwd_scan_chunk(n_chunks - 1, hf, pf)

    @pl.when(step >= 2)
    def _():
        for cp in out_copies(step - 2, slot):
            cp.wait()

    if not chained:
        def bwd_finish_chunk(i, hb):
            tc = n_chunks - 1 - i
            for u in reversed(range(RNN_TCH)):
                t = tc * RNN_TCH + u
                hb = a_b[t] * hb + b_b[t]
                obuf[slot, t] = (b_f[t] + hb) * _gelu_tanh(ybuf[slot, t])
            return hb

        hb = lax.fori_loop(0, n_chunks, bwd_finish_chunk, zero_row)
        sf_ref[...] = hf
        sb_ref[...] = hb
    else:
        st_f = jnp.broadcast_to(h0_ref[0], (RNN_SEG, RNN_C))
        st_b = jnp.broadcast_to(h0_ref[1], (RNN_SEG, RNN_C))
        in_f = st_f
        for _ in range(RNN_SEG - 1):
            in_f = jnp.where(seg == 0, st_f, pltpu.roll(pf * in_f + hf, 1, axis=0))
        sf_ref[...] = pf * in_f + hf

        def bwd_scan_chunk(i, carry):
            hb, pb = carry
            tc = n_chunks - 1 - i
            for u in reversed(range(RNN_TCH)):
                t = tc * RNN_TCH + u
                ab_t = a_b[t]
                hb = ab_t * hb + b_b[t]
                b_b[t] = hb
                pb = ab_t * pb
                a_b[t] = pb
                b_f[t] = a_f[t] * in_f + b_f[t]
            return hb, pb

        hb, pb = lax.fori_loop(0, n_chunks, bwd_scan_chunk, (zero_row, ones))
        in_b = st_b
        for _ in range(RNN_SEG - 1):
            in_b = jnp.where(seg == RNN_SEG - 1, st_b,
                             pltpu.roll(pb * in_b + hb, RNN_SEG - 1, axis=0))
        sb_ref[...] = pb * in_b + hb

        def finish_chunk(tc, carry):
            rows = pl.ds(pl.multiple_of(tc * RNN_TCH, RNN_TCH), RNN_TCH)
            h_b = a_b[rows] * in_b.reshape(1, RNN_SEG, RNN_C) + b_b[rows]
            obuf[slot, rows] = (b_f[rows] + h_b) * _gelu_tanh(ybuf[slot, rows])
            return carry

        lax.fori_loop(0, n_chunks, finish_chunk, 0)

    for cp in out_copies(step, slot):
        cp.start()

    @pl.when(step == n_steps - 1)
    def _():
        for cp in out_copies(step, slot):
            cp.wait()

        @pl.when(step >= 1)
        def _():
            for cp in out_copies(step - 1, 1 - slot):
                cp.wait()


def _rnn(xr, yr, l, conv_w, conv_b, gate_w, lru_ba, lru_bi, lru_lambda, state5, seq_len):
    n = xr.shape[0]
    chained = state5 is not None
    group_len = RNN_SEG * RNN_T
    assert n % group_len == 0 and seq_len in ((group_len,) if chained else (RNN_T,))
    n_groups = n // group_len
    lanes_per_step = RNN_C // LANES
    vec = pl.BlockSpec((None, 2, 1, RNN_C), lambda g, c: (l, 0, 0, c))
    in_specs = [
        pl.BlockSpec(memory_space=pl.ANY),
        pl.BlockSpec(memory_space=pl.ANY),
        pl.BlockSpec((None, CONV_W, RNN_C), lambda g, c: (l, 0, c)),
        pl.BlockSpec((None, 1, RNN_C), lambda g, c: (l, 0, c)),
        pl.BlockSpec((None, lanes_per_step, LANES, 4 * LANES), lambda g, c: (l, c, 0, 0)),
        vec, vec, vec,
    ]
    args = [xr, yr, conv_w, conv_b, gate_w, lru_ba, lru_bi, lru_lambda]
    if chained:
        in_specs.append(pl.BlockSpec((None, None, 2, 1, RNN_C), lambda g, c: (g, l, 0, 0, c)))
        args.append(state5)
    state_spec = pl.BlockSpec((RNN_SEG, RNN_C), lambda g, c: (g, c))
    state_shape = jax.ShapeDtypeStruct((n_groups * RNN_SEG, D_RNN), F32)
    work = lambda t: pltpu.VMEM((t, RNN_SEG, RNN_C), F32)
    return pl.pallas_call(
        functools.partial(_rnn_kernel, chained=chained),
        grid=(n_groups, D_RNN // RNN_C),
        in_specs=in_specs,
        out_specs=[pl.BlockSpec(memory_space=pl.ANY), state_spec, state_spec],
        out_shape=[jax.ShapeDtypeStruct((n, D_RNN), F32), state_shape, state_shape],
        scratch_shapes=[
            pltpu.VMEM((2, RNN_T + CONV_W - 1, RNN_SEG, RNN_C), F32),
            pltpu.VMEM((2, RNN_T, RNN_SEG, RNN_C), F32),
            pltpu.VMEM((2, RNN_T, RNN_SEG, RNN_C), F32),
            work(RNN_T), work(RNN_T), work(RNN_T), work(RNN_T),
            pltpu.SemaphoreType.DMA((2,)), pltpu.SemaphoreType.DMA((2,)),
        ],
        compiler_params=_cparams(("arbitrary", "arbitrary")),
        name="rnn_lat" if chained else "rnn_ctx",
    )(*args)


def _attend_heads(q_ref, k_all, v_all, o_ref, stack_heads):
    exp2_scale = (1.0 / (HEAD_DIM ** 0.5)) * LOG2_E
    group = N_HEADS // N_KV_HEADS
    tq = q_ref.shape[0]
    for kh in range(N_KV_HEADS):
        k = k_all[:, kh * HEAD_DIM:(kh + 1) * HEAD_DIM]
        v = v_all[:, kh * HEAD_DIM:(kh + 1) * HEAD_DIM]
        heads = [slice((kh * group + g) * HEAD_DIM, (kh * group + g + 1) * HEAD_DIM)
                 for g in range(group)]
        stacks = [heads] if stack_heads else [[sl] for sl in heads]
        for stack in stacks:
            q = jnp.concatenate([q_ref[:, sl] for sl in stack], axis=0)
            s = lax.dot_general(q, k, (((1,), (1,)), ((), ())), preferred_element_type=F32)
            m = jnp.max(s, axis=-1, keepdims=True)
            p = jnp.exp2((s - m) * exp2_scale)
            denom = jnp.sum(p, axis=-1, keepdims=True)
            o = (_dot(p.astype(BF16), v) / denom).astype(BF16)
            for g, sl in enumerate(stack):
                o_ref[:, sl] = o[g * tq:(g + 1) * tq]


def _attn_ctx_kernel(q_ref, k_ref, v_ref, o_ref, *, seq_len):
    for s in range(q_ref.shape[0] // seq_len):
        rows = pl.ds(s * seq_len, seq_len)
        _attend_heads(q_ref.at[rows], k_ref[rows, :], v_ref[rows, :].astype(BF16), o_ref.at[rows],
                      stack_heads=True)


def _attn_ctx(q, kr, v, seq_len):
    n = q.shape[0]
    row = lambda b: (b, 0)
    tile = ATTN_CTX_SEQS * seq_len
    return pl.pallas_call(
        functools.partial(_attn_ctx_kernel, seq_len=seq_len),
        grid=(n // tile,),
        in_specs=[pl.BlockSpec((tile, Q_DIM), row),
                  pl.BlockSpec((tile, KV_DIM), row),
                  pl.BlockSpec((tile, KV_DIM), row)],
        out_specs=pl.BlockSpec((tile, Q_DIM), row),
        out_shape=jax.ShapeDtypeStruct((n, Q_DIM), BF16),
        compiler_params=_cparams(("arbitrary",)),
        name="attn_ctx",
    )(q, kr, v)


def _attn_lat_kernel(*refs, past, groups):
    n_rin, n_rout = sum(groups), len(groups)
    _convert_riders(refs[5:5 + n_rin], refs[6 + n_rin:6 + n_rin + n_rout], groups)
    q_ref, kc_ref, vc_ref, kl_ref, vl_ref = refs[:5]
    o_ref = refs[5 + n_rin]
    k_all, v_all = refs[6 + n_rin + n_rout:]

    @pl.when(pl.program_id(1) == 0)
    def _():
        k_all[0:past, :] = kc_ref[...].astype(BF16)
        k_all[past:, :] = kl_ref[...]
        v_all[0:past, :] = vc_ref[...].astype(BF16)
        v_all[past:, :] = vl_ref[...].astype(BF16)

    _attend_heads(q_ref, k_all[...], v_all[...], o_ref, stack_heads=False)


def _attn_lat(q, kr, v, cache_k4, cache_v4, l, seq_len, riders=()):
    n = q.shape[0]
    past = cache_k4.shape[2]
    nq = seq_len // TQ
    step_of = lambda b, j: b * nq + j
    r_in, r_out, r_shape = _rider_specs(riders, (n // seq_len) * nq, step_of)
    return pl.pallas_call(
        functools.partial(_attn_lat_kernel, past=past, groups=_rider_groups(riders)),
        grid=(n // seq_len, nq),
        in_specs=[pl.BlockSpec((TQ, Q_DIM), lambda b, j: (b * nq + j, 0)),
                  pl.BlockSpec((None, None, past, KV_DIM), lambda b, j: (b, l, 0, 0)),
                  pl.BlockSpec((None, None, past, KV_DIM), lambda b, j: (b, l, 0, 0)),
                  pl.BlockSpec((seq_len, KV_DIM), lambda b, j: (b, 0)),
                  pl.BlockSpec((seq_len, KV_DIM), lambda b, j: (b, 0))] + r_in,
        out_specs=[pl.BlockSpec((TQ, Q_DIM), lambda b, j: (b * nq + j, 0))] + r_out,
        out_shape=[jax.ShapeDtypeStruct((n, Q_DIM), BF16)] + r_shape,
        scratch_shapes=[pltpu.VMEM((past + seq_len, KV_DIM), BF16),
                        pltpu.VMEM((past + seq_len, KV_DIM), BF16)],
        compiler_params=_cparams(("arbitrary", "arbitrary")),
        name="attn_lat",
    )(q, cache_k4, cache_v4, kr, v, *_rider_args(riders))


R_E1, R_E2, R_W1, R_W2, R_RANK1, R_RANK2 = range(6)


def _route(h2, wrt_ref, brt_ref, count_ref):
    shape = (h2.shape[0], LANES)
    lane = lax.broadcasted_iota(jnp.int32, shape, 1)
    neg = jnp.float32(-jnp.inf)
    h_hi = h2.astype(BF16)
    h_lo = (h2 - h_hi.astype(F32)).astype(BF16)
    z_hi = _dot(h_hi, wrt_ref[...])
    z_lo = _dot(h_lo, wrt_ref[...])
    logits = ((z_hi[:, :LANES] + z_hi[:, LANES:]) + (z_lo[:, :LANES] + z_lo[:, LANES:])
              + brt_ref[...])
    logits = jnp.where(lane < N_EXPERTS, logits, neg)
    v1 = jnp.max(logits, axis=-1, keepdims=True)
    i1 = jnp.min(jnp.where(logits == v1, lane, LANES), axis=-1, keepdims=True)
    rest = jnp.where(lane == i1, neg, logits)
    v2 = jnp.max(rest, axis=-1, keepdims=True)
    i2 = jnp.min(jnp.where(rest == v2, lane, LANES), axis=-1, keepdims=True)
    e2 = jnp.exp(v2 - v1)
    w1 = 1.0 / (1.0 + e2)
    w2 = e2 / (1.0 + e2)
    chosen = jnp.where((lane == i1) | (lane == i2), 1.0, 0.0)
    r_id = lax.broadcasted_iota(jnp.int32, (shape[0], shape[0]), 0)
    c_id = lax.broadcasted_iota(jnp.int32, (shape[0], shape[0]), 1)
    before = jnp.where(c_id < r_id, 1.0, 0.0).astype(BF16)
    rank = _dot(before, chosen.astype(BF16)) + count_ref[...]
    rank1 = jnp.sum(jnp.where(lane == i1, rank, 0.0), axis=-1, keepdims=True)
    rank2 = jnp.sum(jnp.where(lane == i2, rank, 0.0), axis=-1, keepdims=True)
    count_ref[...] += jnp.sum(chosen, axis=0, keepdims=True)
    rec = jnp.zeros(shape, F32)
    for k, val in ((R_E1, i1.astype(F32)), (R_E2, i2.astype(F32)), (R_W1, w1), (R_W2, w2),
                   (R_RANK1, rank1), (R_RANK2, rank2)):
        rec = jnp.where(lane == k, val, rec)
    return rec


def _mix_out_kernel(*refs, route):
    (x_ref, r_ref, a_ref, gr_ref, ga_ref, mod_ref, n2_ref, wr_ref, wa_ref, wo_ref) = refs[:10]
    if route:
        wrt_ref, brt_ref, x1_ref, h2_ref, rec_ref, cnt_ref, count_ref, hbuf, hsem = refs[10:]
    else:
        x1_ref, h2_ref = refs[10:]
    merged = (_sigmoid(gr_ref[...].astype(F32)) * _dot(r_ref[...].astype(BF16), wr_ref[...])
              + _sigmoid(ga_ref[...].astype(F32)) * _dot(a_ref[...], wa_ref[...]))
    mix = _dot(merged.astype(BF16), wo_ref[...])
    x1 = x_ref[...] + mod_ref[2:3, :] * mix
    x1_ref[...] = x1
    h2 = _rms(x1, n2_ref[...]) * (1.0 + mod_ref[4:5, :]) + mod_ref[3:4, :]
    if not route:
        h2_ref[...] = h2.astype(BF16)
        return

    @pl.when(pl.program_id(0) == 0)
    def _():
        count_ref[...] = jnp.zeros_like(count_ref)

    rec_ref[...] = _route(h2, wrt_ref, brt_ref, count_ref)
    cnt_ref[...] = count_ref[...]

    step = pl.program_id(0)
    slot = step % 2

    def h_copies(k, s):
        rows = pl.ds(k * TM, TM)
        return [pltpu.make_async_copy(hbuf.at[s, :, c * LANES:(c + 1) * LANES], h2_ref.at[rows, c],
                                      hsem.at[s]) for c in range(D_MODEL // LANES)]

    @pl.when(step >= 2)
    def _():
        for cp in h_copies(step - 2, slot):
            cp.wait()

    hbuf[slot] = h2
    for cp in h_copies(step, slot):
        cp.start()

    @pl.when(step == pl.num_programs(0) - 1)
    def _():
        for cp in h_copies(step, slot):
            cp.wait()

        @pl.when(step >= 1)
        def _():
            for cp in h_copies(step - 1, 1 - slot):
                cp.wait()


def _mix_out(x, out_r, out_a, gr, ga, mods, l, norm2, wr_bf, wa_bf, wo_bf, wl, seq_len, lat,
             router=None):
    n = x.shape[0]
    tiles_per_seq = seq_len // TM
    if lat:
        mod_idx = lambda i: (l, 1 + i // tiles_per_seq, 0, 0)
    else:
        mod_idx = lambda i: (l, 0, 0, 0)
    row = lambda i: (i, 0)
    wspec = pl.BlockSpec((None, D_MODEL, D_MODEL), lambda i: (wl, 0, 0), pipeline_mode=RESIDENT)
    in_specs = [pl.BlockSpec((TM, D_MODEL), row)] * 5 + [
        pl.BlockSpec((None, None, 6, D_MODEL), mod_idx),
        pl.BlockSpec((None, 1, D_MODEL), lambda i: (l, 0, 0)),
        wspec, wspec, wspec]
    args = [x, out_r, out_a, gr, ga, mods, norm2, wr_bf, wa_bf, wo_bf]
    out_specs = [pl.BlockSpec((TM, D_MODEL), row)]
    out_shape = [jax.ShapeDtypeStruct((n, D_MODEL), F32)]
    scratch = []
    if router is None:
        out_specs.append(pl.BlockSpec((TM, D_MODEL), row))
        out_shape.append(jax.ShapeDtypeStruct((n, D_MODEL), BF16))
    else:
        j, wr_t, br_pad = router
        in_specs += [pl.BlockSpec((None, D_MODEL, 2 * LANES), lambda i: (j, 0, 0)),
                     pl.BlockSpec((None, 1, LANES), lambda i: (j, 0, 0))]
        args += [wr_t, br_pad]
        out_specs += [pl.BlockSpec(memory_space=pl.ANY),
                      pl.BlockSpec((TM, LANES), row),
                      pl.BlockSpec((1, LANES), lambda i: (0, 0))]
        out_shape += [jax.ShapeDtypeStruct((n, D_MODEL // LANES, LANES), F32),
                      jax.ShapeDtypeStruct((n, LANES), F32),
                      jax.ShapeDtypeStruct((1, LANES), F32)]
        scratch = [pltpu.VMEM((1, LANES), F32), pltpu.VMEM((2, TM, D_MODEL), F32),
                   pltpu.SemaphoreType.DMA((2,))]
    return pl.pallas_call(
        functools.partial(_mix_out_kernel, route=router is not None),
        grid=(n // TM,),
        in_specs=in_specs,
        out_specs=out_specs,
        out_shape=out_shape,
        scratch_shapes=scratch,
        compiler_params=_cparams(("arbitrary",)),
        name=("mix_route" if router is not None else "mix_out") + ("_lat" if lat else "_ctx"),
    )(*args)


def _ffn_kernel(x_ref, h_ref, mod_ref, wg_ref, wu_ref, wd_ref, o_ref, *, f_chunk):
    h = h_ref[...]
    d_ff = wg_ref.shape[1]
    acc = jnp.zeros((h.shape[0], D_MODEL), F32)
    for c in range(d_ff // f_chunk):
        sl = slice(c * f_chunk, (c + 1) * f_chunk)
        act = _silu(_dot(h, wg_ref[:, sl])) * _dot(h, wu_ref[:, sl])
        acc = acc + _dot(act.astype(BF16), wd_ref[sl, :])
    o_ref[...] = x_ref[...] + mod_ref[5:6, :] * acc


def _ffn(x1, h2, mods, l, j, wg_bf, wu_bf, wd_bf, seq_len, lat):
    n = x1.shape[0]
    d_ff = wg_bf.shape[2]
    tiles_per_seq = seq_len // TM
    if lat:
        mod_idx = lambda i: (l, 1 + i // tiles_per_seq, 0, 0)
    else:
        mod_idx = lambda i: (l, 0, 0, 0)
    row = lambda i: (i, 0)
    return pl.pallas_call(
        functools.partial(_ffn_kernel, f_chunk=d_ff // 2),
        grid=(n // TM,),
        in_specs=[pl.BlockSpec((TM, D_MODEL), row), pl.BlockSpec((TM, D_MODEL), row),
                  pl.BlockSpec((None, None, 6, D_MODEL), mod_idx),
                  pl.BlockSpec((None, D_MODEL, d_ff), lambda i: (j, 0, 0), pipeline_mode=RESIDENT),
                  pl.BlockSpec((None, D_MODEL, d_ff), lambda i: (j, 0, 0), pipeline_mode=RESIDENT),
                  pl.BlockSpec((None, d_ff, D_MODEL), lambda i: (j, 0, 0), pipeline_mode=RESIDENT)],
        out_specs=pl.BlockSpec((TM, D_MODEL), row),
        out_shape=jax.ShapeDtypeStruct((n, D_MODEL), F32),
        compiler_params=_cparams(("arbitrary",)),
        name="ffn_lat" if lat else "ffn_ctx",
    )(x1, h2, mods, wg_bf, wu_bf, wd_bf)


def _group_layout(rec, counts, n_tiles):
    cnt = counts[0, :N_EXPERTS].astype(jnp.int32)
    tiles = (cnt + TM_GROUP - 1) // TM_GROUP
    cum = jnp.cumsum(tiles)
    start = (cum - tiles) * TM_GROUP
    experts = jnp.arange(N_EXPERTS, dtype=jnp.int32)

    assert (R_E2, R_RANK2) == (R_E1 + 1, R_RANK1 + 1)
    e12 = rec[:, R_E1:R_E2 + 1].astype(jnp.int32)
    rank12 = rec[:, R_RANK1:R_RANK2 + 1].astype(jnp.int32)
    group_start = jnp.sum(jnp.where(e12[:, :, None] == experts, start, 0), axis=2)
    pos2 = (group_start + rank12).T
    n_active = cum[N_EXPERTS - 1]
    tile_expert = jnp.sum(jnp.arange(n_tiles, dtype=jnp.int32)[:, None] >= cum[None, :], axis=1)
    tile_expert = jnp.minimum(tile_expert, tile_expert[n_active - 1]).astype(jnp.int32)
    tile_ids = jnp.arange(n_tiles, dtype=jnp.int32)
    group_last = jnp.any((tile_ids[:, None] == cum[None, :] - 1) & (tiles[None, :] > 0), axis=1)
    zero_tile = (group_last | (tile_ids >= n_active)).astype(jnp.int32)
    return (pos2.astype(jnp.int32), tile_expert, n_active.reshape(1).astype(jnp.int32), zero_tile)


def _row_copy(src, dst, sem):
    return pltpu.make_async_copy(src, dst, sem)


def _dispatch_kernel(pos_ref, zt_ref, h_ref, xs_ref, zbuf, sem, zsem):
    n_chunks = h_ref.shape[0] // DMA_CHUNK
    n_tiles = xs_ref.shape[0] // TM_GROUP

    zbuf[...] = jnp.zeros_like(zbuf)

    def zero_fill(j):
        return pltpu.make_async_copy(zbuf, xs_ref.at[pl.ds(j * TM_GROUP, TM_GROUP)], zsem)

    def zero_start(j, carry):
        @pl.when(zt_ref[j] != 0)
        def _():
            zero_fill(j).start()
        return carry

    def zero_wait(j, carry):
        @pl.when(zt_ref[j] != 0)
        def _():
            zero_fill(j).wait()
        return carry

    lax.fori_loop(0, n_tiles, zero_start, 0)
    lax.fori_loop(0, n_tiles, zero_wait, 0)

    def drain_chunk():
        rows = pl.ds(0, 2 * DMA_CHUNK)
        _row_copy(xs_ref.at[rows], xs_ref.at[rows], sem).wait()

    def chunk(k, carry):
        for r in range(DMA_CHUNK):
            t = k * DMA_CHUNK + r
            _row_copy(h_ref.at[t], xs_ref.at[pos_ref[0, t]], sem).start(priority=0)
            _row_copy(h_ref.at[t], xs_ref.at[pos_ref[1, t]], sem).start(priority=1)

        @pl.when(k > 0)
        def _():
            drain_chunk()

        return carry

    lax.fori_loop(0, n_chunks, chunk, 0)
    drain_chunk()


def _dispatch(pos2, zero_tile, h2_rows, n_rows):
    tile = h2_rows.shape[1:]
    return pl.pallas_call(
        _dispatch_kernel,
        grid_spec=pltpu.PrefetchScalarGridSpec(
            num_scalar_prefetch=2, grid=(1,),
            in_specs=[pl.BlockSpec(h2_rows.shape, lambda i, p, z: (0, 0, 0))],
            out_specs=pl.BlockSpec(memory_space=pl.ANY),
            scratch_shapes=[pltpu.VMEM((TM_GROUP,) + tile, F32),
                            pltpu.SemaphoreType.DMA(()), pltpu.SemaphoreType.DMA(())]),
        out_shape=jax.ShapeDtypeStruct((n_rows,) + tile, F32),
        compiler_params=_cparams(("arbitrary",)),
        name="moe_dispatch",
    )(pos2, zero_tile, h2_rows)


def _expert_kernel(te_ref, na_ref, xs_ref, wgu_ref, wd_ref, ys_ref,
                   xbuf, ybuf, sem_in, sem_out):
    del te_ref
    i = pl.program_id(0)
    last = pl.num_programs(0) - 1
    n_active = na_ref[0]
    slot = i % 2
    n_col = D_MODEL // LANES

    def in_copies(tile, s):
        rows = pl.ds(tile * TM_GROUP, TM_GROUP)
        return [pltpu.make_async_copy(xs_ref.at[rows, c], xbuf.at[s, :, c * LANES:(c + 1) * LANES],
                                      sem_in.at[s]) for c in range(n_col)]

    def out_copies(tile, s):
        rows = pl.ds(tile * TM_GROUP, TM_GROUP)
        return [pltpu.make_async_copy(ybuf.at[s, :, c * LANES:(c + 1) * LANES], ys_ref.at[rows, c],
                                      sem_out.at[s]) for c in range(n_col)]

    @pl.when(i == 0)
    def _():
        for cp in in_copies(0, 0):
            cp.start()

    @pl.when(i + 1 < n_active)
    def _():
        for cp in in_copies(i + 1, 1 - slot):
            cp.start()

    @pl.when(i >= 2)
    def _():
        for cp in out_copies(i - 2, slot):
            cp.wait()

    @pl.when(i < n_active)
    def _():
        for cp in in_copies(i, slot):
            cp.wait()
        xb = xbuf[slot].astype(BF16)
        z = _dot(xb, wgu_ref[...])
        d_e = wd_ref.shape[0]
        act = _silu(z[:, :d_e]) * z[:, d_e:]
        ybuf[slot] = _dot(act.astype(BF16), wd_ref[...])

    @pl.when(i >= n_active)
    def _():
        ybuf[slot] = jnp.zeros((TM_GROUP, D_MODEL), F32)

    for cp in out_copies(i, slot):
        cp.start()

    @pl.when(i == last)
    def _():
        for cp in out_copies(i, slot):
            cp.wait()

        @pl.when(i >= 1)
        def _():
            for cp in out_copies(i - 1, 1 - slot):
                cp.wait()


def _experts(tile_expert, n_active, xs, j, wgu_bf, wd_bf):
    n_rows = xs.shape[0]
    d_e = wd_bf.shape[2]
    wspec = lambda shape: pl.BlockSpec((None, None) + shape, lambda i, te, na: (j, te[i], 0, 0))
    return pl.pallas_call(
        _expert_kernel,
        grid_spec=pltpu.PrefetchScalarGridSpec(
            num_scalar_prefetch=2, grid=(n_rows // TM_GROUP,),
            in_specs=[pl.BlockSpec(memory_space=pl.ANY),
                      wspec((D_MODEL, 2 * d_e)), wspec((d_e, D_MODEL))],
            out_specs=pl.BlockSpec(memory_space=pl.ANY),
            scratch_shapes=[pltpu.VMEM((2, TM_GROUP, D_MODEL), F32),
                            pltpu.VMEM((2, TM_GROUP, D_MODEL), F32),
                            pltpu.SemaphoreType.DMA((2,)), pltpu.SemaphoreType.DMA((2,))]),
        out_shape=jax.ShapeDtypeStruct(xs.shape, F32),
        compiler_params=_cparams(("arbitrary",)),
        name="moe_experts",
    )(tile_expert, n_active, xs, wgu_bf, wd_bf)


def _combine_kernel(*refs, final):
    pos_ref, y_ref, x_ref, rec_ref, mod_ref = refs[:5]
    g_ref = refs[5] if final else None
    o_ref, a1, a2, b1, b2, sem_a, sem_b = refs[5 + int(final):]
    i = pl.program_id(0)

    def issue(half_idx, d1, d2, sem):
        base = half_idx * DMA_CHUNK

        for r in range(DMA_CHUNK):
            _row_copy(y_ref.at[pos_ref[0, base + r]], d1.at[r], sem).start(priority=1)
            _row_copy(y_ref.at[pos_ref[1, base + r]], d2.at[r], sem).start(priority=1)

    def drain(d1, d2, sem):
        rows = pl.ds(0, DMA_CHUNK)
        _row_copy(y_ref.at[rows], d1, sem).wait()
        _row_copy(y_ref.at[rows], d2, sem).wait()

    def finish(half, d1, d2):
        rows = slice(half * DMA_CHUNK, (half + 1) * DMA_CHUNK)
        w1 = rec_ref[rows, R_W1:R_W1 + 1]
        w2 = rec_ref[rows, R_W2:R_W2 + 1]
        for c in range(D_MODEL // LANES):
            sl = slice(c * LANES, (c + 1) * LANES)
            y = w1 * d1[:, c, :] + w2 * d2[:, c, :]
            o_ref[rows, sl] = x_ref[rows, sl] + mod_ref[5:6, sl] * y
        if final:
            o_ref[rows, :] = _rms(o_ref[rows, :], g_ref[...])

    @pl.when(i == 0)
    def _():
        issue(0, a1, a2, sem_a)

    issue(2 * i + 1, b1, b2, sem_b)
    drain(a1, a2, sem_a)
    finish(0, a1, a2)

    @pl.when(i + 1 < pl.num_programs(0))
    def _():
        issue(2 * i + 2, a1, a2, sem_a)

    drain(b1, b2, sem_b)
    finish(1, b1, b2)


def _combine(pos2, ys, x1, rec, mods, l, seq_len, lat, final_g):
    n = x1.shape[0]
    final = final_g is not None
    tile = ys.shape[1:]
    tm = 2 * DMA_CHUNK
    tiles_per_seq = max(seq_len // tm, 1)
    if lat:
        mod_idx = lambda i, p: (l, 1 + i // tiles_per_seq, 0, 0)
    else:
        mod_idx = lambda i, p: (l, 0, 0, 0)
    row = lambda i, p: (i, 0)
    slot = pltpu.VMEM((DMA_CHUNK,) + tile, F32)
    in_specs = [pl.BlockSpec(memory_space=pl.ANY),
                pl.BlockSpec((tm, D_MODEL), row),
                pl.BlockSpec((tm, LANES), row),
                pl.BlockSpec((None, None, 6, D_MODEL), mod_idx)]
    args = [pos2, ys, x1, rec, mods]
    if final:
        in_specs.append(pl.BlockSpec((1, D_MODEL), lambda i, p: (0, 0)))
        args.append(final_g)
    return pl.pallas_call(
        functools.partial(_combine_kernel, final=final),
        grid_spec=pltpu.PrefetchScalarGridSpec(
            num_scalar_prefetch=1, grid=(n // tm,),
            in_specs=in_specs,
            out_specs=pl.BlockSpec((tm, D_MODEL), row),
            scratch_shapes=[slot, slot, slot, slot,
                            pltpu.SemaphoreType.DMA(()), pltpu.SemaphoreType.DMA(())]),
        out_shape=jax.ShapeDtypeStruct((n, D_MODEL), F32),
        compiler_params=_cparams(("arbitrary",)),
        name="moe_combine_final" if final else "moe_combine",
    )(*args)


def _moe(x1, h2_rows, rec, counts, mods, l, j, wgu_bf, wd_bf, seq_len, lat, final_g):
    n = x1.shape[0]
    n_tiles = 2 * n // TM_GROUP + N_EXPERTS
    pos2, tile_expert, n_active, zero_tile = _group_layout(rec, counts, n_tiles)
    xs = _dispatch(pos2, zero_tile, h2_rows, n_tiles * TM_GROUP)
    ys = _experts(tile_expert, n_active, xs, j, wgu_bf, wd_bf)
    return _combine(pos2, ys, x1, rec, mods, l, seq_len, lat, final_g)


def _final_norm_kernel(x_ref, g_ref, o_ref):
    o_ref[...] = _rms(x_ref[...], g_ref[...])


def _final_norm(x, g):
    n = x.shape[0]
    row = lambda i: (i, 0)
    return pl.pallas_call(
        _final_norm_kernel,
        grid=(n // TM_FF,),
        in_specs=[pl.BlockSpec((TM_FF, D_MODEL), row), pl.BlockSpec((1, D_MODEL), lambda i: (0, 0))],
        out_specs=pl.BlockSpec((TM_FF, D_MODEL), row),
        out_shape=jax.ShapeDtypeStruct((n, D_MODEL), F32),
        compiler_params=_cparams(("arbitrary",)),
        name="final_norm",
    )(x, g)


def _rope_tables(t_len):
    t = jnp.arange(t_len)
    pos = jnp.stack([t // GRID_W, t % GRID_W], axis=-1).astype(F32)
    inv = ROPE_THETA ** (-jnp.arange(0, AXIS_DIM, 2, dtype=F32) / AXIS_DIM)
    ang = pos[..., None] * inv
    cos, sin = jnp.cos(ang), jnp.sin(ang)
    c = jnp.concatenate([cos[:, 0], cos[:, 0], cos[:, 1], cos[:, 1]], axis=-1)
    s = jnp.concatenate([-sin[:, 0], sin[:, 0], -sin[:, 1], sin[:, 1]], axis=-1)
    return c, s


def _gate_weights(wa, wi):
    depth = wa.shape[0]
    groups = D_RNN // LANES

    def pair(w):
        w = w.reshape(depth, 2, groups, 2, RNN_BLOCK, RNN_BLOCK)
        z = jnp.zeros_like(w[:, :, :, 0])
        top = jnp.concatenate([w[:, :, :, 0], z], axis=-1)
        bot = jnp.concatenate([z, w[:, :, :, 1]], axis=-1)
        return jnp.concatenate([top, bot], axis=-2)

    w = jnp.concatenate([pair(wa), pair(wi)], axis=-1)
    return (0.5 * jnp.concatenate([w[:, 0], w[:, 1]], axis=-1)).astype(BF16)


def kernel(x_prompt, x_sample, cache_k, cache_v, state_rnn, c, c_ctx, w_mod, b_mod, norm1, norm2, w_in, conv_w, conv_b, lru_wa, lru_ba, lru_wi, lru_bi, lru_lambda, q_norm, k_norm, w_rnn_out, w_attn_out, w_out, ff_gate, ff_up, ff_down, router_w, router_b, exp_gate, exp_up, exp_down, final_norm):
    batch, seq, _ = x_prompt.shape
    dec_batch, dec_seq, _ = x_sample.shape
    depth = w_mod.shape[0]
    past = cache_k.shape[2]
    assert dec_batch + 1 <= MOD_ROWS

    cond8 = jnp.zeros((MOD_ROWS, D_MODEL), F32).at[0].set(c_ctx).at[1:1 + dec_batch].set(c)

    gate_w = _gate_weights(lru_wa, lru_wi)
    n_moe = router_w.shape[0]
    wr_pad = jnp.zeros((n_moe, D_MODEL, LANES), F32).at[:, :, :N_EXPERTS].set(router_w)
    wr_head = wr_pad.astype(BF16)
    wr_t = jnp.concatenate([wr_head, (wr_pad - wr_head.astype(F32)).astype(BF16)], axis=-1)
    br_pad = jnp.zeros((n_moe, 1, LANES), F32).at[:, 0, :N_EXPERTS].set(router_b)

    norm1_3 = norm1.reshape(depth, 1, D_MODEL)
    norm2_3 = norm2.reshape(depth, 1, D_MODEL)
    qn3 = q_norm.reshape(depth, 1, HEAD_DIM)
    kn3 = k_norm.reshape(depth, 1, HEAD_DIM)
    conv_b3 = conv_b.reshape(depth, 1, D_RNN)
    ba4 = (0.5 * lru_ba).reshape(depth, 2, 1, D_RNN)
    bi4 = (0.5 * lru_bi).reshape(depth, 2, 1, D_RNN)
    lam4 = lru_lambda.reshape(depth, 2, 1, D_RNN)
    state5 = state_rnn.reshape(dec_batch, depth, 2, 1, D_RNN)
    cache_k4 = cache_k.reshape(dec_batch, depth, past, KV_DIM)
    cache_v4 = cache_v.reshape(dec_batch, depth, past, KV_DIM)
    rope_tabs = _rope_tables(dec_seq)
    final_g = final_norm.reshape(1, D_MODEL)

    n_steps = dec_batch * dec_seq // TQ
    assert n_steps == dec_batch * dec_seq // TM_ROPE

    def flat(w):
        return w.reshape(-1, w.shape[-1])

    def channel_names(l):
        return ("exp_gate_up", "exp_down") if l % 2 == 1 else ("ff_gate", "ff_up", "ff_down")

    stacked = dict(w_in=(w_in,), w_rnn_out=(w_rnn_out,), w_attn_out=(w_attn_out,), w_out=(w_out,),
                   ff_gate=(ff_gate,), ff_up=(ff_up,), ff_down=(ff_down,),
                   exp_gate_up=(exp_gate, exp_up), exp_down=(exp_down,))

    def layer_shape(name):
        parts = stacked[name]
        return (1,) + parts[0].shape[1:-1] + (sum(w.shape[-1] for w in parts),)

    def layer_index(name, l):
        return l if name.startswith("w_") else l // 2

    def rider(name, l, steps):
        parts = stacked[name]
        rows = flat(parts[0]).shape[0] // parts[0].shape[0] // steps
        return _Rider([flat(w) for w in parts], rows, layer_index(name, l) * steps)

    def names_of(l):
        return ("w_in", "w_rnn_out", "w_attn_out", "w_out") + channel_names(l)

    weights = {l: {} for l in range(depth)}

    def keep(l, names, arrays):
        for name, arr in zip(names, arrays):
            weights[l][name] = arr.reshape(layer_shape(name))

    mod_steps = _modulation_steps(depth)
    mods, w_in0 = _modulation(cond8, w_mod, b_mod, riders=[rider("w_in", 0, mod_steps)])
    mods = mods.reshape(depth, MOD_ROWS, 6, D_MODEL)
    keep(0, ("w_in",), [w_in0])
    ctx_steps = batch * seq // TM

    def run_layer(x, l, lat):
        seq_len = dec_seq if lat else seq
        wts = weights[l]
        convert_next = lat and l + 1 < depth
        if convert_next:
            proj_names = names_of(l + 1)[:4]
            attn_names = names_of(l + 1)[4:]
            proj_riders = [rider(name, l + 1, n_steps) for name in proj_names]
            attn_riders = [rider(name, l + 1, n_steps) for name in attn_names]
            target = l + 1
        elif l == 0 and not lat:
            proj_names, attn_names, attn_riders, target = names_of(0)[1:], (), [], 0
            proj_riders = [rider(name, 0, ctx_steps) for name in proj_names]
        else:
            proj_names, attn_names, proj_riders, attn_riders, target = (), (), [], [], None
        outs = _in_proj(x, mods, l, norm1_3, wts["w_in"], 0, qn3, kn3,
                        rope_tabs if lat else None, seq_len, riders=proj_riders,
                        caches=None if lat else new_kv)
        xr, yr, q, _, kr, v, gr, ga = outs[:8]
        if target is not None:
            keep(target, proj_names, outs[8:8 + len(proj_names)])
        if not lat:
            new_kv[:] = outs[8 + len(proj_names):]
        out_r, end_f, end_b = _rnn(xr, yr, l, conv_w, conv_b3, gate_w, ba4, bi4, lam4,
                                   state5 if lat else None, seq_len)
        if lat:
            outs = _attn_lat(q, kr, v, cache_k4, cache_v4, l, seq_len, riders=attn_riders)
            out_a = outs[0]
            if target is not None:
                keep(target, attn_names, outs[1:])
        else:
            out_a = _attn_ctx(q, kr, v, seq_len)
        mix_args = (x, out_r, out_a, gr, ga, mods, l, norm2_3, wts["w_rnn_out"],
                    wts["w_attn_out"], wts["w_out"], 0, seq_len, lat)
        if l % 2 == 1:
            x1, h2_rows, rec, counts = _mix_out(*mix_args, router=(l // 2, wr_t, br_pad))
            x = _moe(x1, h2_rows, rec, counts, mods, l, 0, wts["exp_gate_up"], wts["exp_down"],
                     seq_len, lat, final_g if l == depth - 1 else None)
        else:
            x1, h2 = _mix_out(*mix_args)
            x = _ffn(x1, h2, mods, l, 0, wts["ff_gate"], wts["ff_up"], wts["ff_down"],
                     seq_len, lat)
            if l == depth - 1:
                x = _final_norm(x, final_g)
        return x, jnp.stack([end_f, end_b], axis=1)

    new_kv = [jnp.zeros((batch, depth, seq, N_KV_HEADS, HEAD_DIM), F32) for _ in range(2)]
    y_ctx = x_prompt.reshape(batch * seq, D_MODEL)
    y_lat = x_sample.reshape(dec_batch * dec_seq, D_MODEL)
    ss = []
    for l in range(depth):
        y_ctx, st = run_layer(y_ctx, l, False)
        ss.append(st)
        y_lat, _ = run_layer(y_lat, l, True)

    new_state = jnp.stack(ss, axis=1)
    return (y_ctx.reshape(batch, seq, D_MODEL), y_lat.reshape(dec_batch, dec_seq, D_MODEL),
            new_kv[0], new_kv[1], new_state)
```

```python
import functools

import jax
import jax.numpy as jnp
from jax import lax
from jax.experimental import pallas as pl
from jax.experimental.pallas import tpu as pltpu

F32 = jnp.float32
BF16 = jnp.bfloat16

D_MODEL = 1024
D_RNN = 1024
N_RNN_BLOCKS = 16
RNN_BLOCK = D_RNN // N_RNN_BLOCKS
CONV_W = 4
CONV_LEFT = 2
LRU_C = 8.0
N_HEADS = 8
N_KV_HEADS = 2
HEAD_DIM = 128
AXIS_DIM = HEAD_DIM // 2
ROPE_THETA = 10000.0
GRID_W = 64
Q_DIM = N_HEADS * HEAD_DIM
KV_DIM = N_KV_HEADS * HEAD_DIM
N_EXPERTS = 8
EPS = 1e-6
LOG2_E = 1.4426950408889634
RSQRT_FLOOR = 1e-30
D_IN = 2 * D_RNN + Q_DIM + 2 * KV_DIM + 2 * D_MODEL
O_XR, O_YR, O_Q = 0, D_RNN, 2 * D_RNN
O_K = O_Q + Q_DIM
O_V = O_K + KV_DIM
O_GR = O_V + KV_DIM
O_GA = O_GR + D_MODEL

LANES = 128
SUBLANES = 8
MOD_ROWS = 8
VMEM_LIMIT = 56 * 1024 * 1024

TM = 512
TM_ROPE = 256
RESIDENT = pl.Buffered(1)
RNN_T = 256
RNN_SEG = SUBLANES
RNN_C = 512
RNN_TCH = 32
TQ = 256
ATTN_CTX_SEQS = 2
TM_FF = 512
TM_GROUP = 256
DMA_CHUNK = 256


def _cparams(sem):
    return pltpu.CompilerParams(dimension_semantics=sem, vmem_limit_bytes=VMEM_LIMIT)


def _sigmoid(x):
    return 1.0 / (1.0 + jnp.exp(-x))


def _silu(x):
    return x * _sigmoid(x)


def _gelu_tanh(x):
    return 0.5 * x * (1.0 + jnp.tanh(0.7978845608028654 * (x + 0.044715 * (x * x * x))))


def _rms(x, g):
    ms = jnp.mean(x * x, axis=-1, keepdims=True)
    return x * lax.rsqrt(ms + EPS) * g


def _dot(a, b):
    return jnp.dot(a, b, preferred_element_type=F32)


MOD_TN = 1536


def _mod_kernel(*refs, groups):
    n_rin = sum(groups)
    cond_ref, w_ref, b_ref = refs[:3]
    o_ref = refs[3 + n_rin]
    _convert_riders(refs[3:3 + n_rin], refs[4 + n_rin:], groups)
    s = _silu(cond_ref[...]).astype(BF16)
    o_ref[...] = _dot(s, w_ref[...].astype(BF16)) + b_ref[...]


def _modulation_steps(depth):
    return depth * (6 * D_MODEL // MOD_TN)


def _modulation(cond8, w_mod, b_mod, riders=()):
    depth = w_mod.shape[0]
    per_layer = 6 * D_MODEL // MOD_TN
    r_in, r_out, r_shape = _rider_specs(riders, depth * per_layer, lambda l, j: l * per_layer + j)
    return pl.pallas_call(
        functools.partial(_mod_kernel, groups=_rider_groups(riders)),
        grid=(depth, per_layer),
        in_specs=[
            pl.BlockSpec((MOD_ROWS, D_MODEL), lambda l, j: (0, 0)),
            pl.BlockSpec((None, D_MODEL, MOD_TN), lambda l, j: (l, 0, j)),
            pl.BlockSpec((None, 1, MOD_TN), lambda l, j: (l, 0, j)),
        ] + r_in,
        out_specs=[pl.BlockSpec((None, MOD_ROWS, MOD_TN), lambda l, j: (l, 0, j))] + r_out,
        out_shape=[jax.ShapeDtypeStruct((depth, MOD_ROWS, 6 * D_MODEL), F32)] + r_shape,
        compiler_params=_cparams(("arbitrary", "arbitrary")),
        name="adaln_mod",
    )(cond8, w_mod, b_mod.reshape(depth, 1, 6 * D_MODEL), *_rider_args(riders))


def _swap_halves(x):
    lane = lax.broadcasted_iota(jnp.int32, x.shape, 1)
    return jnp.where((lane % AXIS_DIM) < AXIS_DIM // 2,
                     pltpu.roll(x, LANES - AXIS_DIM // 2, axis=1),
                     pltpu.roll(x, AXIS_DIM // 2, axis=1))


class _Rider:
    def __init__(self, w2d, rows, first):
        self.parts = list(w2d) if isinstance(w2d, (list, tuple)) else [w2d]
        self.rows, self.first = rows, first


def _rider_specs(riders, n_steps, step_of):
    in_specs, out_specs, out_shape = [], [], []
    for r in riders:
        for w in r.parts:
            assert r.rows % 16 == 0 and (r.first + n_steps) * r.rows <= w.shape[0]
            in_specs.append(pl.BlockSpec((r.rows, w.shape[1]),
                                         lambda *g, r=r: (r.first + step_of(*g), 0)))
        cols = sum(w.shape[1] for w in r.parts)
        out_specs.append(pl.BlockSpec((r.rows, cols), lambda *g: (step_of(*g), 0)))
        out_shape.append(jax.ShapeDtypeStruct((r.rows * n_steps, cols), BF16))
    return in_specs, out_specs, out_shape


def _rider_groups(riders):
    return tuple(len(r.parts) for r in riders)


def _rider_args(riders):
    return [w for r in riders for w in r.parts]


def _convert_riders(in_refs, out_refs, groups):
    k = 0
    for o_ref, n_parts in zip(out_refs, groups):
        col = 0
        for w_ref in in_refs[k:k + n_parts]:
            o_ref[:, col:col + w_ref.shape[1]] = w_ref[...].astype(BF16)
            col += w_ref.shape[1]
        k += n_parts


def _in_proj_kernel(*refs, rope, groups, cache_layer, seq_len):
    n_in = 8 if rope else 6
    n_rin, n_rout = sum(groups), len(groups)
    n_cache = 0 if cache_layer is None else 2
    first_out = n_in + n_rin + n_cache
    _convert_riders(refs[n_in:n_in + n_rin], refs[first_out + 8:first_out + 8 + n_rout], groups)
    cache_refs = refs[first_out + 8 + n_rout:]
    refs = refs[:n_in] + refs[first_out:first_out + 8]
    if rope:
        (x_ref, mod_ref, n1_ref, w_ref, qn_ref, kn_ref, c_ref, s_ref,
         xr_ref, yr_ref, q_ref, kno_ref, kro_ref, v_ref, gr_ref, ga_ref) = refs
    else:
        (x_ref, mod_ref, n1_ref, w_ref, qn_ref, kn_ref,
         xr_ref, yr_ref, q_ref, kno_ref, kro_ref, v_ref, gr_ref, ga_ref) = refs
    x = x_ref[...]
    h = _rms(x, n1_ref[...]) * (1.0 + mod_ref[1:2, :]) + mod_ref[0:1, :]
    hb = h.astype(BF16)

    xr_ref[...] = _dot(hb, w_ref[:, O_XR:O_XR + D_RNN])
    yr_ref[...] = _dot(hb, w_ref[:, O_YR:O_YR + D_RNN])
    gr_ref[...] = _dot(hb, w_ref[:, O_GR:O_GR + D_MODEL]).astype(BF16)
    ga_ref[...] = _dot(hb, w_ref[:, O_GA:O_GA + D_MODEL]).astype(BF16)
    v_ref[...] = _dot(hb, w_ref[:, O_V:O_V + KV_DIM])

    if rope:
        cs, sn = c_ref[...], s_ref[...]
    zq = _dot(hb, w_ref[:, O_Q:O_Q + Q_DIM])
    for hd in range(N_HEADS):
        t = _rms(zq[:, hd * HEAD_DIM:(hd + 1) * HEAD_DIM], qn_ref[...])
        if rope:
            t = t * cs + _swap_halves(t) * sn
        q_ref[:, hd * HEAD_DIM:(hd + 1) * HEAD_DIM] = t.astype(BF16)
    zk = _dot(hb, w_ref[:, O_K:O_K + KV_DIM])
    for hd in range(N_KV_HEADS):
        t = _rms(zk[:, hd * HEAD_DIM:(hd + 1) * HEAD_DIM], kn_ref[...])
        kno_ref[:, hd * HEAD_DIM:(hd + 1) * HEAD_DIM] = t
        if rope:
            t = t * cs + _swap_halves(t) * sn
        kro_ref[:, hd * HEAD_DIM:(hd + 1) * HEAD_DIM] = t.astype(BF16)

    if cache_layer is None:
        return
    newk_ref, newv_ref, kbuf, vbuf, csem = cache_refs
    step = pl.program_id(0)
    slot = step % 2
    seqs_per_tile = x_ref.shape[0] // seq_len

    def cache_copies(k, s):
        cps = []
        for sq in range(seqs_per_tile):
            rows = slice(sq * seq_len, (sq + 1) * seq_len)
            for hd in range(N_KV_HEADS):
                cols = slice(hd * HEAD_DIM, (hd + 1) * HEAD_DIM)
                dst = (k * seqs_per_tile + sq, cache_layer, slice(None), hd, slice(None))
                cps.append(pltpu.make_async_copy(kbuf.at[s, rows, cols], newk_ref.at[dst], csem.at[s]))
                cps.append(pltpu.make_async_copy(vbuf.at[s, rows, cols], newv_ref.at[dst], csem.at[s]))
        return cps

    @pl.when(step >= 2)
    def _():
        for cp in cache_copies(step - 2, slot):
            cp.wait()

    kbuf[slot] = kno_ref[...]
    vbuf[slot] = v_ref[...]
    for cp in cache_copies(step, slot):
        cp.start()

    @pl.when(step == pl.num_programs(0) - 1)
    def _():
        for cp in cache_copies(step, slot):
            cp.wait()

        @pl.when(step >= 1)
        def _():
            for cp in cache_copies(step - 1, 1 - slot):
                cp.wait()


def _in_proj(x, mods, l, norm1, w_in_bf, wl, q_norm, k_norm, rope_tabs, seq_len, riders=(),
             caches=None):
    n = x.shape[0]
    rope = rope_tabs is not None
    tm = TM_ROPE if rope else TM
    tiles_per_seq = seq_len // tm
    if rope:
        mod_idx = lambda i: (l, 1 + i // tiles_per_seq, 0, 0)
    else:
        mod_idx = lambda i: (l, 0, 0, 0)
    row = lambda i: (i, 0)
    in_specs = [
        pl.BlockSpec((tm, D_MODEL), row),
        pl.BlockSpec((None, None, 6, D_MODEL), mod_idx),
        pl.BlockSpec((None, 1, D_MODEL), lambda i: (l, 0, 0)),
        pl.BlockSpec((None, D_MODEL, D_IN), lambda i: (wl, 0, 0), pipeline_mode=RESIDENT),
        pl.BlockSpec((None, 1, HEAD_DIM), lambda i: (l, 0, 0)),
        pl.BlockSpec((None, 1, HEAD_DIM), lambda i: (l, 0, 0)),
    ]
    args = [x, mods, norm1, w_in_bf, q_norm, k_norm]
    if rope:
        in_specs += [pl.BlockSpec((tm, HEAD_DIM), lambda i: (i % tiles_per_seq, 0))] * 2
        args += list(rope_tabs)
    widths = [(D_RNN, F32), (D_RNN, F32), (Q_DIM, BF16), (KV_DIM, F32), (KV_DIM, BF16),
              (KV_DIM, F32), (D_MODEL, BF16), (D_MODEL, BF16)]
    r_in, r_out, r_shape = _rider_specs(riders, n // tm, lambda i: i)
    in_specs = in_specs + r_in
    out_specs = [pl.BlockSpec((tm, w), row) for w, _ in widths] + r_out
    out_shape = [jax.ShapeDtypeStruct((n, w), dt) for w, dt in widths] + r_shape
    args = args + _rider_args(riders)
    scratch, aliases = [], {}
    if caches is not None:
        assert tm % seq_len == 0
        for c in caches:
            aliases[len(args)] = len(out_shape)
            args.append(c)
            in_specs.append(pl.BlockSpec(memory_space=pl.ANY))
            out_specs.append(pl.BlockSpec(memory_space=pl.ANY))
            out_shape.append(jax.ShapeDtypeStruct(c.shape, c.dtype))
        scratch = [pltpu.VMEM((2, tm, KV_DIM), F32), pltpu.VMEM((2, tm, KV_DIM), F32),
                   pltpu.SemaphoreType.DMA((2,))]
    return pl.pallas_call(
        functools.partial(_in_proj_kernel, rope=rope, groups=_rider_groups(riders),
                          cache_layer=None if caches is None else l, seq_len=seq_len),
        grid=(n // tm,),
        in_specs=in_specs,
        out_specs=out_specs,
        out_shape=out_shape,
        scratch_shapes=scratch,
        input_output_aliases=aliases,
        compiler_params=_cparams(("arbitrary",)),
        name="in_proj_lat" if rope else "in_proj_ctx",
    )(*args)


def _rnn_kernel(*refs, chained):
    refs = list(refs)
    xr_ref, yr_ref, cw_ref, cb_ref, gw_ref, ba_ref, bi_ref, lam_ref = refs[:8]
    refs = refs[8:]
    h0_ref = refs.pop(0) if chained else None
    out_ref, sf_ref, sb_ref, xext, ybuf, obuf, a_f, b_f, a_b, b_b, sem_in, sem_out = refs

    g = pl.program_id(0)
    cb = pl.program_id(1)
    n_cb = pl.num_programs(1)
    step = g * n_cb + cb
    n_steps = pl.num_programs(0) * n_cb
    slot = step % 2

    def hbm_window(kk, s):
        rows = pl.ds(pl.multiple_of(((kk // n_cb) * RNN_SEG + s) * RNN_T, RNN_T), RNN_T)
        cols = pl.ds(pl.multiple_of((kk % n_cb) * RNN_C, RNN_C), RNN_C)
        return rows, cols

    def in_copies(kk, sl_):
        cps = []
        for s in range(RNN_SEG):
            win = hbm_window(kk, s)
            cps.append(pltpu.make_async_copy(
                xr_ref.at[win], xext.at[sl_, pl.ds(CONV_LEFT, RNN_T), s], sem_in.at[sl_]))
            cps.append(pltpu.make_async_copy(yr_ref.at[win], ybuf.at[sl_, :, s], sem_in.at[sl_]))
        return cps

    def out_copies(kk, sl_):
        return [pltpu.make_async_copy(obuf.at[sl_, :, s], out_ref.at[hbm_window(kk, s)],
                                      sem_out.at[sl_]) for s in range(RNN_SEG)]

    @pl.when(step == 0)
    def _():
        for cp in in_copies(0, 0):
            cp.start()

    @pl.when(step + 1 < n_steps)
    def _():
        for cp in in_copies(step + 1, 1 - slot):
            cp.start()

    for cp in in_copies(step, slot):
        cp.wait()

    seg = lax.broadcasted_iota(jnp.int32, (RNN_SEG, RNN_C), 0)
    zero_row = jnp.zeros((RNN_SEG, RNN_C), F32)
    if chained:
        xext[slot, 0] = jnp.where(seg == 0, zero_row, pltpu.roll(xext[slot, RNN_T], 1, axis=0))
        xext[slot, 1] = jnp.where(seg == 0, zero_row, pltpu.roll(xext[slot, RNN_T + 1], 1, axis=0))
        xext[slot, RNN_T + CONV_LEFT] = jnp.where(
            seg == RNN_SEG - 1, zero_row, pltpu.roll(xext[slot, CONV_LEFT], RNN_SEG - 1, axis=0))
    else:
        xext[slot, 0] = zero_row
        xext[slot, 1] = zero_row
        xext[slot, RNN_T + CONV_LEFT] = zero_row

    scans = ((a_f, b_f), (a_b, b_b))
    rows2d = RNN_TCH * RNN_SEG

    def gate_chunk(tc, carry):
        t0 = pl.multiple_of(tc * RNN_TCH, RNN_TCH)
        for j in range(RNN_C // LANES):
            sl = slice(j * LANES, (j + 1) * LANES)
            xc = jnp.broadcast_to(cb_ref[:, sl].reshape(1, 1, LANES), (RNN_TCH, RNN_SEG, LANES))
            for k in range(CONV_W):
                xc = xc + (xext[slot, pl.ds(t0 + k, RNN_TCH), :, sl]
                           * cw_ref[k:k + 1, sl].reshape(1, 1, LANES))
            xc = xc.reshape(rows2d, LANES)
            pre = _dot(xc.astype(BF16), gw_ref[j])
            half_xc = 0.5 * xc
            for d, (a_s, b_s) in enumerate(scans):
                lam = lam_ref[d, :, sl]
                log2_decay = (-LRU_C * LOG2_E) * (jnp.maximum(-lam, 0.0)
                                                  + jnp.log(1.0 + jnp.exp(-jnp.abs(lam))))
                t_r = jnp.tanh(pre[:, 2 * d * LANES:(2 * d + 1) * LANES] + ba_ref[d, :, sl])
                t_i = jnp.tanh(pre[:, (2 * d + 1) * LANES:(2 * d + 2) * LANES]
                               + bi_ref[d, :, sl])
                half_decay = 0.5 * log2_decay
                a = jnp.exp2(t_r * half_decay + half_decay)
                y = 1.0 - a * a
                b = (y * lax.rsqrt(jnp.maximum(y, RSQRT_FLOOR))) * ((t_i + 1.0) * half_xc)
                a_s[pl.ds(t0, RNN_TCH), :, sl] = a.reshape(RNN_TCH, RNN_SEG, LANES)
                b_s[pl.ds(t0, RNN_TCH), :, sl] = b.reshape(RNN_TCH, RNN_SEG, LANES)
        return carry

    n_chunks = RNN_T // RNN_TCH
    ones = jnp.ones((RNN_SEG, RNN_C), F32)

    def fwd_scan_chunk(tc, hf, pf):
        for u in range(RNN_TCH):
            t = tc * RNN_TCH + u
            af_t = a_f[t]
            hf = af_t * hf + b_f[t]
            b_f[t] = hf
            if chained:
                pf = af_t * pf
                a_f[t] = pf
        return hf, pf

    def gate_and_scan(tc, carry):
        carry = fwd_scan_chunk(tc - 1, *carry)
        gate_chunk(tc, 0)
        return carry

    gate_chunk(0, 0)
    hf, pf = lax.fori_loop(1, n_chunks, gate_and_scan, (zero_row, ones))
    hf, pf = fwd_scan_chunk(n_chunks - 1, hf, pf)

    @pl.when(step >= 2)
    def _():
        for cp in out_copies(step - 2, slot):
            cp.wait()

    if not chained:
        def bwd_finish_chunk(i, hb):
            tc = n_chunks - 1 - i
            for u in reversed(range(RNN_TCH)):
                t = tc * RNN_TCH + u
                hb = a_b[t] * hb + b_b[t]
                obuf[slot, t] = (b_f[t] + hb) * _gelu_tanh(ybuf[slot, t])
            return hb

        hb = lax.fori_loop(0, n_chunks, bwd_finish_chunk, zero_row)
        sf_ref[...] = hf
        sb_ref[...] = hb
    else:
        st_f = jnp.broadcast_to(h0_ref[0], (RNN_SEG, RNN_C))
        st_b = jnp.broadcast_to(h0_ref[1], (RNN_SEG, RNN_C))
        in_f = st_f
        for _ in range(RNN_SEG - 1):
            in_f = jnp.where(seg == 0, st_f, pltpu.roll(pf * in_f + hf, 1, axis=0))
        sf_ref[...] = pf * in_f + hf

        def bwd_scan_chunk(i, carry):
            hb, pb = carry
            tc = n_chunks - 1 - i
            for u in reversed(range(RNN_TCH)):
                t = tc * RNN_TCH + u
                ab_t = a_b[t]
                hb = ab_t * hb + b_b[t]
                b_b[t] = hb
                pb = ab_t * pb
                a_b[t] = pb
                b_f[t] = a_f[t] * in_f + b_f[t]
            return hb, pb

        hb, pb = lax.fori_loop(0, n_chunks, bwd_scan_chunk, (zero_row, ones))
        in_b = st_b
        for _ in range(RNN_SEG - 1):
            in_b = jnp.where(seg == RNN_SEG - 1, st_b,
                             pltpu.roll(pb * in_b + hb, RNN_SEG - 1, axis=0))
        sb_ref[...] = pb * in_b + hb

        def finish_chunk(tc, carry):
            rows = pl.ds(pl.multiple_of(tc * RNN_TCH, RNN_TCH), RNN_TCH)
            h_b = a_b[rows] * in_b.reshape(1, RNN_SEG, RNN_C) + b_b[rows]
            obuf[slot, rows] = (b_f[rows] + h_b) * _gelu_tanh(ybuf[slot, rows])
            return carry

        lax.fori_loop(0, n_chunks, finish_chunk, 0)

    for cp in out_copies(step, slot):
        cp.start()

    @pl.when(step == n_steps - 1)
    def _():
        for cp in out_copies(step, slot):
            cp.wait()

        @pl.when(step >= 1)
        def _():
            for cp in out_copies(step - 1, 1 - slot):
                cp.wait()


def _rnn(xr, yr, l, conv_w, conv_b, gate_w, lru_ba, lru_bi, lru_lambda, state5, seq_len):
    n = xr.shape[0]
    chained = state5 is not None
    group_len = RNN_SEG * RNN_T
    assert n % group_len == 0 and seq_len in ((group_len,) if chained else (RNN_T,))
    n_groups = n // group_len
    lanes_per_step = RNN_C // LANES
    vec = pl.BlockSpec((None, 2, 1, RNN_C), lambda g, c: (l, 0, 0, c))
    in_specs = [
        pl.BlockSpec(memory_space=pl.ANY),
        pl.BlockSpec(memory_space=pl.ANY),
        pl.BlockSpec((None, CONV_W, RNN_C), lambda g, c: (l, 0, c)),
        pl.BlockSpec((None, 1, RNN_C), lambda g, c: (l, 0, c)),
        pl.BlockSpec((None, lanes_per_step, LANES, 4 * LANES), lambda g, c: (l, c, 0, 0)),
        vec, vec, vec,
    ]
    args = [xr, yr, conv_w, conv_b, gate_w, lru_ba, lru_bi, lru_lambda]
    if chained:
        in_specs.append(pl.BlockSpec((None, None, 2, 1, RNN_C), lambda g, c: (g, l, 0, 0, c)))
        args.append(state5)
    state_spec = pl.BlockSpec((RNN_SEG, RNN_C), lambda g, c: (g, c))
    state_shape = jax.ShapeDtypeStruct((n_groups * RNN_SEG, D_RNN), F32)
    work = lambda t: pltpu.VMEM((t, RNN_SEG, RNN_C), F32)
    return pl.pallas_call(
        functools.partial(_rnn_kernel, chained=chained),
        grid=(n_groups, D_RNN // RNN_C),
        in_specs=in_specs,
        out_specs=[pl.BlockSpec(memory_space=pl.ANY), state_spec, state_spec],
        out_shape=[jax.ShapeDtypeStruct((n, D_RNN), F32), state_shape, state_shape],
        scratch_shapes=[
            pltpu.VMEM((2, RNN_T + CONV_W - 1, RNN_SEG, RNN_C), F32),
            pltpu.VMEM((2, RNN_T, RNN_SEG, RNN_C), F32),
            pltpu.VMEM((2, RNN_T, RNN_SEG, RNN_C), F32),
            work(RNN_T), work(RNN_T), work(RNN_T), work(RNN_T),
            pltpu.SemaphoreType.DMA((2,)), pltpu.SemaphoreType.DMA((2,)),
        ],
        compiler_params=_cparams(("arbitrary", "arbitrary")),
        name="rnn_lat" if chained else "rnn_ctx",
    )(*args)


def _attend_heads(q_ref, k_all, v_all, o_ref, stack_heads):
    exp2_scale = (1.0 / (HEAD_DIM ** 0.5)) * LOG2_E
    group = N_HEADS // N_KV_HEADS
    tq = q_ref.shape[0]
    for kh in range(N_KV_HEADS):
        k = k_all[:, kh * HEAD_DIM:(kh + 1) * HEAD_DIM]
        v = v_all[:, kh * HEAD_DIM:(kh + 1) * HEAD_DIM]
        heads = [slice((kh * group + g) * HEAD_DIM, (kh * group + g + 1) * HEAD_DIM)
                 for g in range(group)]
        stacks = [heads] if stack_heads else [[sl] for sl in heads]
        for stack in stacks:
            q = jnp.concatenate([q_ref[:, sl] for sl in stack], axis=0)
            s = lax.dot_general(q, k, (((1,), (1,)), ((), ())), preferred_element_type=F32)
            m = jnp.max(s, axis=-1, keepdims=True)
            p = jnp.exp2((s - m) * exp2_scale)
            denom = jnp.sum(p, axis=-1, keepdims=True)
            o = (_dot(p.astype(BF16), v) / denom).astype(BF16)
            for g, sl in enumerate(stack):
                o_ref[:, sl] = o[g * tq:(g + 1) * tq]


def _attn_ctx_kernel(q_ref, k_ref, v_ref, o_ref, *, seq_len):
    for s in range(q_ref.shape[0] // seq_len):
        rows = pl.ds(s * seq_len, seq_len)
        _attend_heads(q_ref.at[rows], k_ref[rows, :], v_ref[rows, :].astype(BF16), o_ref.at[rows],
                      stack_heads=True)


def _attn_ctx(q, kr, v, seq_len):
    n = q.shape[0]
    row = lambda b: (b, 0)
    tile = ATTN_CTX_SEQS * seq_len
    return pl.pallas_call(
        functools.partial(_attn_ctx_kernel, seq_len=seq_len),
        grid=(n // tile,),
        in_specs=[pl.BlockSpec((tile, Q_DIM), row),
                  pl.BlockSpec((tile, KV_DIM), row),
                  pl.BlockSpec((tile, KV_DIM), row)],
        out_specs=pl.BlockSpec((tile, Q_DIM), row),
        out_shape=jax.ShapeDtypeStruct((n, Q_DIM), BF16),
        compiler_params=_cparams(("arbitrary",)),
        name="attn_ctx",
    )(q, kr, v)


def _attn_lat_kernel(*refs, past, groups):
    n_rin, n_rout = sum(groups), len(groups)
    _convert_riders(refs[5:5 + n_rin], refs[6 + n_rin:6 + n_rin + n_rout], groups)
    q_ref, kc_ref, vc_ref, kl_ref, vl_ref = refs[:5]
    o_ref = refs[5 + n_rin]
    k_all, v_all = refs[6 + n_rin + n_rout:]

    @pl.when(pl.program_id(1) == 0)
    def _():
        k_all[0:past, :] = kc_ref[...].astype(BF16)
        k_all[past:, :] = kl_ref[...]
        v_all[0:past, :] = vc_ref[...].astype(BF16)
        v_all[past:, :] = vl_ref[...].astype(BF16)

    _attend_heads(q_ref, k_all[...], v_all[...], o_ref, stack_heads=False)


def _attn_lat(q, kr, v, cache_k4, cache_v4, l, seq_len, riders=()):
    n = q.shape[0]
    past = cache_k4.shape[2]
    nq = seq_len // TQ
    step_of = lambda b, j: b * nq + j
    r_in, r_out, r_shape = _rider_specs(riders, (n // seq_len) * nq, step_of)
    return pl.pallas_call(
        functools.partial(_attn_lat_kernel, past=past, groups=_rider_groups(riders)),
        grid=(n // seq_len, nq),
        in_specs=[pl.BlockSpec((TQ, Q_DIM), lambda b, j: (b * nq + j, 0)),
                  pl.BlockSpec((None, None, past, KV_DIM), lambda b, j: (b, l, 0, 0)),
                  pl.BlockSpec((None, None, past, KV_DIM), lambda b, j: (b, l, 0, 0)),
                  pl.BlockSpec((seq_len, KV_DIM), lambda b, j: (b, 0)),
                  pl.BlockSpec((seq_len, KV_DIM), lambda b, j: (b, 0))] + r_in,
        out_specs=[pl.BlockSpec((TQ, Q_DIM), lambda b, j: (b * nq + j, 0))] + r_out,
        out_shape=[jax.ShapeDtypeStruct((n, Q_DIM), BF16)] + r_shape,
        scratch_shapes=[pltpu.VMEM((past + seq_len, KV_DIM), BF16),
                        pltpu.VMEM((past + seq_len, KV_DIM), BF16)],
        compiler_params=_cparams(("arbitrary", "arbitrary")),
        name="attn_lat",
    )(q, cache_k4, cache_v4, kr, v, *_rider_args(riders))


R_E1, R_E2, R_W1, R_W2, R_RANK1, R_RANK2 = range(6)


def _route(h2, wrt_ref, brt_ref, count_ref):
    shape = (h2.shape[0], LANES)
    lane = lax.broadcasted_iota(jnp.int32, shape, 1)
    neg = jnp.float32(-jnp.inf)
    h_hi = h2.astype(BF16)
    h_lo = (h2 - h_hi.astype(F32)).astype(BF16)
    z_hi = _dot(h_hi, wrt_ref[...])
    z_lo = _dot(h_lo, wrt_ref[...])
    logits = ((z_hi[:, :LANES] + z_hi[:, LANES:]) + (z_lo[:, :LANES] + z_lo[:, LANES:])
              + brt_ref[...])
    logits = jnp.where(lane < N_EXPERTS, logits, neg)
    v1 = jnp.max(logits, axis=-1, keepdims=True)
    i1 = jnp.min(jnp.where(logits == v1, lane, LANES), axis=-1, keepdims=True)
    rest = jnp.where(lane == i1, neg, logits)
    v2 = jnp.max(rest, axis=-1, keepdims=True)
    i2 = jnp.min(jnp.where(rest == v2, lane, LANES), axis=-1, keepdims=True)
    e2 = jnp.exp(v2 - v1)
    w1 = 1.0 / (1.0 + e2)
    w2 = e2 / (1.0 + e2)
    chosen = jnp.where((lane == i1) | (lane == i2), 1.0, 0.0)
    r_id = lax.broadcasted_iota(jnp.int32, (shape[0], shape[0]), 0)
    c_id = lax.broadcasted_iota(jnp.int32, (shape[0], shape[0]), 1)
    before = jnp.where(c_id < r_id, 1.0, 0.0).astype(BF16)
    rank = _dot(before, chosen.astype(BF16)) + count_ref[...]
    rank1 = jnp.sum(jnp.where(lane == i1, rank, 0.0), axis=-1, keepdims=True)
    rank2 = jnp.sum(jnp.where(lane == i2, rank, 0.0), axis=-1, keepdims=True)
    count_ref[...] += jnp.sum(chosen, axis=0, keepdims=True)
    rec = jnp.zeros(shape, F32)
    for k, val in ((R_E1, i1.astype(F32)), (R_E2, i2.astype(F32)), (R_W1, w1), (R_W2, w2),
                   (R_RANK1, rank1), (R_RANK2, rank2)):
        rec = jnp.where(lane == k, val, rec)
    return rec


def _mix_out_kernel(*refs, route):
    (x_ref, r_ref, a_ref, gr_ref, ga_ref, mod_ref, n2_ref, wr_ref, wa_ref, wo_ref) = refs[:10]
    if route:
        wrt_ref, brt_ref, x1_ref, h2_ref, rec_ref, cnt_ref, count_ref, hbuf, hsem = refs[10:]
    else:
        x1_ref, h2_ref = refs[10:]
    merged = (_sigmoid(gr_ref[...].astype(F32)) * _dot(r_ref[...].astype(BF16), wr_ref[...])
              + _sigmoid(ga_ref[...].astype(F32)) * _dot(a_ref[...], wa_ref[...]))
    mix = _dot(merged.astype(BF16), wo_ref[...])
    x1 = x_ref[...] + mod_ref[2:3, :] * mix
    x1_ref[...] = x1
    h2 = _rms(x1, n2_ref[...]) * (1.0 + mod_ref[4:5, :]) + mod_ref[3:4, :]
    if not route:
        h2_ref[...] = h2.astype(BF16)
        return

    @pl.when(pl.program_id(0) == 0)
    def _():
        count_ref[...] = jnp.zeros_like(count_ref)

    rec_ref[...] = _route(h2, wrt_ref, brt_ref, count_ref)
    cnt_ref[...] = count_ref[...]

    step = pl.program_id(0)
    slot = step % 2

    def h_copies(k, s):
        rows = pl.ds(k * TM, TM)
        return [pltpu.make_async_copy(hbuf.at[s, :, c * LANES:(c + 1) * LANES], h2_ref.at[rows, c],
                                      hsem.at[s]) for c in range(D_MODEL // LANES)]

    @pl.when(step >= 2)
    def _():
        for cp in h_copies(step - 2, slot):
            cp.wait()

    hbuf[slot] = h2
    for cp in h_copies(step, slot):
        cp.start()

    @pl.when(step == pl.num_programs(0) - 1)
    def _():
        for cp in h_copies(step, slot):
            cp.wait()

        @pl.when(step >= 1)
        def _():
            for cp in h_copies(step - 1, 1 - slot):
                cp.wait()


def _mix_out(x, out_r, out_a, gr, ga, mods, l, norm2, wr_bf, wa_bf, wo_bf, wl, seq_len, lat,
             router=None):
    n = x.shape[0]
    tiles_per_seq = seq_len // TM
    if lat:
        mod_idx = lambda i: (l, 1 + i // tiles_per_seq, 0, 0)
    else:
        mod_idx = lambda i: (l, 0, 0, 0)
    row = lambda i: (i, 0)
    wspec = pl.BlockSpec((None, D_MODEL, D_MODEL), lambda i: (wl, 0, 0), pipeline_mode=RESIDENT)
    in_specs = [pl.BlockSpec((TM, D_MODEL), row)] * 5 + [
        pl.BlockSpec((None, None, 6, D_MODEL), mod_idx),
        pl.BlockSpec((None, 1, D_MODEL), lambda i: (l, 0, 0)),
        wspec, wspec, wspec]
    args = [x, out_r, out_a, gr, ga, mods, norm2, wr_bf, wa_bf, wo_bf]
    out_specs = [pl.BlockSpec((TM, D_MODEL), row)]
    out_shape = [jax.ShapeDtypeStruct((n, D_MODEL), F32)]
    scratch = []
    if router is None:
        out_specs.append(pl.BlockSpec((TM, D_MODEL), row))
        out_shape.append(jax.ShapeDtypeStruct((n, D_MODEL), BF16))
    else:
        j, wr_t, br_pad = router
        in_specs += [pl.BlockSpec((None, D_MODEL, 2 * LANES), lambda i: (j, 0, 0)),
                     pl.BlockSpec((None, 1, LANES), lambda i: (j, 0, 0))]
        args += [wr_t, br_pad]
        out_specs += [pl.BlockSpec(memory_space=pl.ANY),
                      pl.BlockSpec((TM, LANES), row),
                      pl.BlockSpec((1, LANES), lambda i: (0, 0))]
        out_shape += [jax.ShapeDtypeStruct((n, D_MODEL // LANES, LANES), F32),
                      jax.ShapeDtypeStruct((n, LANES), F32),
                      jax.ShapeDtypeStruct((1, LANES), F32)]
        scratch = [pltpu.VMEM((1, LANES), F32), pltpu.VMEM((2, TM, D_MODEL), F32),
                   pltpu.SemaphoreType.DMA((2,))]
    return pl.pallas_call(
        functools.partial(_mix_out_kernel, route=router is not None),
        grid=(n // TM,),
        in_specs=in_specs,
        out_specs=out_specs,
        out_shape=out_shape,
        scratch_shapes=scratch,
        compiler_params=_cparams(("arbitrary",)),
        name=("mix_route" if router is not None else "mix_out") + ("_lat" if lat else "_ctx"),
    )(*args)


def _ffn_kernel(x_ref, h_ref, mod_ref, wg_ref, wu_ref, wd_ref, o_ref, *, f_chunk):
    h = h_ref[...]
    d_ff = wg_ref.shape[1]
    acc = jnp.zeros((h.shape[0], D_MODEL), F32)
    for c in range(d_ff // f_chunk):
        sl = slice(c * f_chunk, (c + 1) * f_chunk)
        act = _silu(_dot(h, wg_ref[:, sl])) * _dot(h, wu_ref[:, sl])
        acc = acc + _dot(act.astype(BF16), wd_ref[sl, :])
    o_ref[...] = x_ref[...] + mod_ref[5:6, :] * acc


def _ffn(x1, h2, mods, l, j, wg_bf, wu_bf, wd_bf, seq_len, lat):
    n = x1.shape[0]
    d_ff = wg_bf.shape[2]
    tiles_per_seq = seq_len // TM
    if lat:
        mod_idx = lambda i: (l, 1 + i // tiles_per_seq, 0, 0)
    else:
        mod_idx = lambda i: (l, 0, 0, 0)
    row = lambda i: (i, 0)
    return pl.pallas_call(
        functools.partial(_ffn_kernel, f_chunk=d_ff // 2),
        grid=(n // TM,),
        in_specs=[pl.BlockSpec((TM, D_MODEL), row), pl.BlockSpec((TM, D_MODEL), row),
                  pl.BlockSpec((None, None, 6, D_MODEL), mod_idx),
                  pl.BlockSpec((None, D_MODEL, d_ff), lambda i: (j, 0, 0), pipeline_mode=RESIDENT),
                  pl.BlockSpec((None, D_MODEL, d_ff), lambda i: (j, 0, 0), pipeline_mode=RESIDENT),
                  pl.BlockSpec((None, d_ff, D_MODEL), lambda i: (j, 0, 0), pipeline_mode=RESIDENT)],
        out_specs=pl.BlockSpec((TM, D_MODEL), row),
        out_shape=jax.ShapeDtypeStruct((n, D_MODEL), F32),
        compiler_params=_cparams(("arbitrary",)),
        name="ffn_lat" if lat else "ffn_ctx",
    )(x1, h2, mods, wg_bf, wu_bf, wd_bf)


def _group_layout(rec, counts, n_tiles):
    cnt = counts[0, :N_EXPERTS].astype(jnp.int32)
    tiles = (cnt + TM_GROUP - 1) // TM_GROUP
    cum = jnp.cumsum(tiles)
    start = (cum - tiles) * TM_GROUP
    experts = jnp.arange(N_EXPERTS, dtype=jnp.int32)

    assert (R_E2, R_RANK2) == (R_E1 + 1, R_RANK1 + 1)
    e12 = rec[:, R_E1:R_E2 + 1].astype(jnp.int32)
    rank12 = rec[:, R_RANK1:R_RANK2 + 1].astype(jnp.int32)
    group_start = jnp.sum(jnp.where(e12[:, :, None] == experts, start, 0), axis=2)
    pos2 = (group_start + rank12).T
    n_active = cum[N_EXPERTS - 1]
    tile_expert = jnp.sum(jnp.arange(n_tiles, dtype=jnp.int32)[:, None] >= cum[None, :], axis=1)
    tile_expert = jnp.minimum(tile_expert, tile_expert[n_active - 1]).astype(jnp.int32)
    tile_ids = jnp.arange(n_tiles, dtype=jnp.int32)
    group_last = jnp.any((tile_ids[:, None] == cum[None, :] - 1) & (tiles[None, :] > 0), axis=1)
    zero_tile = (group_last | (tile_ids >= n_active)).astype(jnp.int32)
    return (pos2.astype(jnp.int32), tile_expert, n_active.reshape(1).astype(jnp.int32), zero_tile)


def _row_copy(src, dst, sem):
    return pltpu.make_async_copy(src, dst, sem)


def _dispatch_kernel(pos_ref, zt_ref, h_ref, xs_ref, zbuf, sem, zsem):
    n_chunks = h_ref.shape[0] // DMA_CHUNK
    n_tiles = xs_ref.shape[0] // TM_GROUP

    zbuf[...] = jnp.zeros_like(zbuf)

    def zero_fill(j):
        return pltpu.make_async_copy(zbuf, xs_ref.at[pl.ds(j * TM_GROUP, TM_GROUP)], zsem)

    def zero_start(j, carry):
        @pl.when(zt_ref[j] != 0)
        def _():
            zero_fill(j).start()
        return carry

    def zero_wait(j, carry):
        @pl.when(zt_ref[j] != 0)
        def _():
            zero_fill(j).wait()
        return carry

    lax.fori_loop(0, n_tiles, zero_start, 0)
    lax.fori_loop(0, n_tiles, zero_wait, 0)

    def drain_chunk():
        rows = pl.ds(0, 2 * DMA_CHUNK)
        _row_copy(xs_ref.at[rows], xs_ref.at[rows], sem).wait()

    def chunk(k, carry):
        for r in range(DMA_CHUNK):
            t = k * DMA_CHUNK + r
            _row_copy(h_ref.at[t], xs_ref.at[pos_ref[0, t]], sem).start(priority=0)
            _row_copy(h_ref.at[t], xs_ref.at[pos_ref[1, t]], sem).start(priority=1)

        @pl.when(k > 0)
        def _():
            drain_chunk()

        return carry

    lax.fori_loop(0, n_chunks, chunk, 0)
    drain_chunk()


def _dispatch(pos2, zero_tile, h2_rows, n_rows):
    tile = h2_rows.shape[1:]
    return pl.pallas_call(
        _dispatch_kernel,
        grid_spec=pltpu.PrefetchScalarGridSpec(
            num_scalar_prefetch=2, grid=(1,),
            in_specs=[pl.BlockSpec(h2_rows.shape, lambda i, p, z: (0, 0, 0))],
            out_specs=pl.BlockSpec(memory_space=pl.ANY),
            scratch_shapes=[pltpu.VMEM((TM_GROUP,) + tile, F32),
                            pltpu.SemaphoreType.DMA(()), pltpu.SemaphoreType.DMA(())]),
        out_shape=jax.ShapeDtypeStruct((n_rows,) + tile, F32),
        compiler_params=_cparams(("arbitrary",)),
        name="moe_dispatch",
    )(pos2, zero_tile, h2_rows)


def _expert_kernel(te_ref, na_ref, xs_ref, wgu_ref, wd_ref, ys_ref,
                   xbuf, ybuf, sem_in, sem_out):
    del te_ref
    i = pl.program_id(0)
    last = pl.num_programs(0) - 1
    n_active = na_ref[0]
    slot = i % 2
    n_col = D_MODEL // LANES

    def in_copies(tile, s):
        rows = pl.ds(tile * TM_GROUP, TM_GROUP)
        return [pltpu.make_async_copy(xs_ref.at[rows, c], xbuf.at[s, :, c * LANES:(c + 1) * LANES],
                                      sem_in.at[s]) for c in range(n_col)]

    def out_copies(tile, s):
        rows = pl.ds(tile * TM_GROUP, TM_GROUP)
        return [pltpu.make_async_copy(ybuf.at[s, :, c * LANES:(c + 1) * LANES], ys_ref.at[rows, c],
                                      sem_out.at[s]) for c in range(n_col)]

    @pl.when(i == 0)
    def _():
        for cp in in_copies(0, 0):
            cp.start()

    @pl.when(i + 1 < n_active)
    def _():
        for cp in in_copies(i + 1, 1 - slot):
            cp.start()

    @pl.when(i >= 2)
    def _():
        for cp in out_copies(i - 2, slot):
            cp.wait()

    @pl.when(i < n_active)
    def _():
        for cp in in_copies(i, slot):
            cp.wait()
        xb = xbuf[slot].astype(BF16)
        z = _dot(xb, wgu_ref[...])
        d_e = wd_ref.shape[0]
        act = _silu(z[:, :d_e]) * z[:, d_e:]
        ybuf[slot] = _dot(act.astype(BF16), wd_ref[...])

    @pl.when(i >= n_active)
    def _():
        ybuf[slot] = jnp.zeros((TM_GROUP, D_MODEL), F32)

    for cp in out_copies(i, slot):
        cp.start()

    @pl.when(i == last)
    def _():
        for cp in out_copies(i, slot):
            cp.wait()

        @pl.when(i >= 1)
        def _():
            for cp in out_copies(i - 1, 1 - slot):
                cp.wait()


def _experts(tile_expert, n_active, xs, j, wgu_bf, wd_bf):
    n_rows = xs.shape[0]
    d_e = wd_bf.shape[2]
    wspec = lambda shape: pl.BlockSpec((None, None) + shape, lambda i, te, na: (j, te[i], 0, 0))
    return pl.pallas_call(
        _expert_kernel,
        grid_spec=pltpu.PrefetchScalarGridSpec(
            num_scalar_prefetch=2, grid=(n_rows // TM_GROUP,),
            in_specs=[pl.BlockSpec(memory_space=pl.ANY),
                      wspec((D_MODEL, 2 * d_e)), wspec((d_e, D_MODEL))],
            out_specs=pl.BlockSpec(memory_space=pl.ANY),
            scratch_shapes=[pltpu.VMEM((2, TM_GROUP, D_MODEL), F32),
                            pltpu.VMEM((2, TM_GROUP, D_MODEL), F32),
                            pltpu.SemaphoreType.DMA((2,)), pltpu.SemaphoreType.DMA((2,))]),
        out_shape=jax.ShapeDtypeStruct(xs.shape, F32),
        compiler_params=_cparams(("arbitrary",)),
        name="moe_experts",
    )(tile_expert, n_active, xs, wgu_bf, wd_bf)


def _combine_kernel(*refs, final):
    pos_ref, y_ref, x_ref, rec_ref, mod_ref = refs[:5]
    g_ref = refs[5] if final else None
    o_ref = refs[5 + int(final)]
    bufs = refs[6 + int(final):-1]
    sems = refs[-1]
    sets = [(bufs[2 * k], bufs[2 * k + 1], sems.at[k]) for k in range(4)]
    i = pl.program_id(0)
    last = pl.num_programs(0) - 1

    def issue(half_idx, buf_set):
        d1, d2, sem = buf_set
        base = half_idx * DMA_CHUNK

        for r in range(DMA_CHUNK):
            _row_copy(y_ref.at[pos_ref[0, base + r]], d1.at[r], sem).start(priority=1)
            _row_copy(y_ref.at[pos_ref[1, base + r]], d2.at[r], sem).start(priority=1)

    def drain(buf_set):
        d1, d2, sem = buf_set
        rows = pl.ds(0, DMA_CHUNK)
        _row_copy(y_ref.at[rows], d1, sem).wait()
        _row_copy(y_ref.at[rows], d2, sem).wait()

    def finish(half, buf_set):
        d1, d2, _ = buf_set
        rows = slice(half * DMA_CHUNK, (half + 1) * DMA_CHUNK)
        w1 = rec_ref[rows, R_W1:R_W1 + 1]
        w2 = rec_ref[rows, R_W2:R_W2 + 1]
        for c in range(D_MODEL // LANES):
            sl = slice(c * LANES, (c + 1) * LANES)
            y = w1 * d1[:, c, :] + w2 * d2[:, c, :]
            o_ref[rows, sl] = x_ref[rows, sl] + mod_ref[5:6, sl] * y
        if final:
            o_ref[rows, :] = _rms(o_ref[rows, :], g_ref[...])

    @pl.when(i == 0)
    def _():
        issue(0, sets[0])
        issue(1, sets[1])

    def half_block(half, cur, nxt):
        drain(cur)

        @pl.when(i < last)
        def _():
            issue(2 * i + 2 + half, nxt)
            finish(half, cur)

        @pl.when(i == last)
        def _():
            finish(half, cur)

    @pl.when(i % 2 == 0)
    def _():
        half_block(0, sets[0], sets[2])
        half_block(1, sets[1], sets[3])

    @pl.when(i % 2 == 1)
    def _():
        half_block(0, sets[2], sets[0])
        half_block(1, sets[3], sets[1])


def _combine(pos2, ys, x1, rec, mods, l, seq_len, lat, final_g):
    n = x1.shape[0]
    final = final_g is not None
    tile = ys.shape[1:]
    tm = 2 * DMA_CHUNK
    tiles_per_seq = max(seq_len // tm, 1)
    if lat:
        mod_idx = lambda i, p: (l, 1 + i // tiles_per_seq, 0, 0)
    else:
        mod_idx = lambda i, p: (l, 0, 0, 0)
    row = lambda i, p: (i, 0)
    slot = pltpu.VMEM((DMA_CHUNK,) + tile, F32)
    in_specs = [pl.BlockSpec(memory_space=pl.ANY),
                pl.BlockSpec((tm, D_MODEL), row),
                pl.BlockSpec((tm, LANES), row),
                pl.BlockSpec((None, None, 6, D_MODEL), mod_idx)]
    args = [pos2, ys, x1, rec, mods]
    if final:
        in_specs.append(pl.BlockSpec((1, D_MODEL), lambda i, p: (0, 0)))
        args.append(final_g)
    return pl.pallas_call(
        functools.partial(_combine_kernel, final=final),
        grid_spec=pltpu.PrefetchScalarGridSpec(
            num_scalar_prefetch=1, grid=(n // tm,),
            in_specs=in_specs,
            out_specs=pl.BlockSpec((tm, D_MODEL), row),
            scratch_shapes=[slot] * 8 + [pltpu.SemaphoreType.DMA((4,))]),
        out_shape=jax.ShapeDtypeStruct((n, D_MODEL), F32),
        compiler_params=_cparams(("arbitrary",)),
        name="moe_combine_final" if final else "moe_combine",
    )(*args)


def _moe(x1, h2_rows, rec, counts, mods, l, j, wgu_bf, wd_bf, seq_len, lat, final_g):
    n = x1.shape[0]
    n_tiles = 2 * n // TM_GROUP + N_EXPERTS
    pos2, tile_expert, n_active, zero_tile = _group_layout(rec, counts, n_tiles)
    xs = _dispatch(pos2, zero_tile, h2_rows, n_tiles * TM_GROUP)
    ys = _experts(tile_expert, n_active, xs, j, wgu_bf, wd_bf)
    return _combine(pos2, ys, x1, rec, mods, l, seq_len, lat, final_g)


def _final_norm_kernel(x_ref, g_ref, o_ref):
    o_ref[...] = _rms(x_ref[...], g_ref[...])


def _final_norm(x, g):
    n = x.shape[0]
    row = lambda i: (i, 0)
    return pl.pallas_call(
        _final_norm_kernel,
        grid=(n // TM_FF,),
        in_specs=[pl.BlockSpec((TM_FF, D_MODEL), row), pl.BlockSpec((1, D_MODEL), lambda i: (0, 0))],
        out_specs=pl.BlockSpec((TM_FF, D_MODEL), row),
        out_shape=jax.ShapeDtypeStruct((n, D_MODEL), F32),
        compiler_params=_cparams(("arbitrary",)),
        name="final_norm",
    )(x, g)


def _rope_tables(t_len):
    t = jnp.arange(t_len)
    pos = jnp.stack([t // GRID_W, t % GRID_W], axis=-1).astype(F32)
    inv = ROPE_THETA ** (-jnp.arange(0, AXIS_DIM, 2, dtype=F32) / AXIS_DIM)
    ang = pos[..., None] * inv
    cos, sin = jnp.cos(ang), jnp.sin(ang)
    c = jnp.concatenate([cos[:, 0], cos[:, 0], cos[:, 1], cos[:, 1]], axis=-1)
    s = jnp.concatenate([-sin[:, 0], sin[:, 0], -sin[:, 1], sin[:, 1]], axis=-1)
    return c, s


def _gate_weights(wa, wi):
    depth = wa.shape[0]
    groups = D_RNN // LANES

    def pair(w):
        w = w.reshape(depth, 2, groups, 2, RNN_BLOCK, RNN_BLOCK)
        z = jnp.zeros_like(w[:, :, :, 0])
        top = jnp.concatenate([w[:, :, :, 0], z], axis=-1)
        bot = jnp.concatenate([z, w[:, :, :, 1]], axis=-1)
        return jnp.concatenate([top, bot], axis=-2)

    w = jnp.concatenate([pair(wa), pair(wi)], axis=-1)
    return (0.5 * jnp.concatenate([w[:, 0], w[:, 1]], axis=-1)).astype(BF16)


def kernel(x_prompt, x_sample, cache_k, cache_v, state_rnn, c, c_ctx, w_mod, b_mod, norm1, norm2, w_in, conv_w, conv_b, lru_wa, lru_ba, lru_wi, lru_bi, lru_lambda, q_norm, k_norm, w_rnn_out, w_attn_out, w_out, ff_gate, ff_up, ff_down, router_w, router_b, exp_gate, exp_up, exp_down, final_norm):
    batch, seq, _ = x_prompt.shape
    dec_batch, dec_seq, _ = x_sample.shape
    depth = w_mod.shape[0]
    past = cache_k.shape[2]
    assert dec_batch + 1 <= MOD_ROWS

    cond8 = jnp.zeros((MOD_ROWS, D_MODEL), F32).at[0].set(c_ctx).at[1:1 + dec_batch].set(c)

    gate_w = _gate_weights(lru_wa, lru_wi)
    n_moe = router_w.shape[0]
    wr_pad = jnp.zeros((n_moe, D_MODEL, LANES), F32).at[:, :, :N_EXPERTS].set(router_w)
    wr_head = wr_pad.astype(BF16)
    wr_t = jnp.concatenate([wr_head, (wr_pad - wr_head.astype(F32)).astype(BF16)], axis=-1)
    br_pad = jnp.zeros((n_moe, 1, LANES), F32).at[:, 0, :N_EXPERTS].set(router_b)

    norm1_3 = norm1.reshape(depth, 1, D_MODEL)
    norm2_3 = norm2.reshape(depth, 1, D_MODEL)
    qn3 = q_norm.reshape(depth, 1, HEAD_DIM)
    kn3 = k_norm.reshape(depth, 1, HEAD_DIM)
    conv_b3 = conv_b.reshape(depth, 1, D_RNN)
    ba4 = (0.5 * lru_ba).reshape(depth, 2, 1, D_RNN)
    bi4 = (0.5 * lru_bi).reshape(depth, 2, 1, D_RNN)
    lam4 = lru_lambda.reshape(depth, 2, 1, D_RNN)
    state5 = state_rnn.reshape(dec_batch, depth, 2, 1, D_RNN)
    cache_k4 = cache_k.reshape(dec_batch, depth, past, KV_DIM)
    cache_v4 = cache_v.reshape(dec_batch, depth, past, KV_DIM)
    rope_tabs = _rope_tables(dec_seq)
    final_g = final_norm.reshape(1, D_MODEL)

    n_steps = dec_batch * dec_seq // TQ
    assert n_steps == dec_batch * dec_seq // TM_ROPE

    def flat(w):
        return w.reshape(-1, w.shape[-1])

    def channel_names(l):
        return ("exp_gate_up", "exp_down") if l % 2 == 1 else ("ff_gate", "ff_up", "ff_down")

    stacked = dict(w_in=(w_in,), w_rnn_out=(w_rnn_out,), w_attn_out=(w_attn_out,), w_out=(w_out,),
                   ff_gate=(ff_gate,), ff_up=(ff_up,), ff_down=(ff_down,),
                   exp_gate_up=(exp_gate, exp_up), exp_down=(exp_down,))

    def layer_shape(name):
        parts = stacked[name]
        return (1,) + parts[0].shape[1:-1] + (sum(w.shape[-1] for w in parts),)

    def layer_index(name, l):
        return l if name.startswith("w_") else l // 2

    def rider(name, l, steps):
        parts = stacked[name]
        rows = flat(parts[0]).shape[0] // parts[0].shape[0] // steps
        return _Rider([flat(w) for w in parts], rows, layer_index(name, l) * steps)

    def names_of(l):
        return ("w_in", "w_rnn_out", "w_attn_out", "w_out") + channel_names(l)

    weights = {l: {} for l in range(depth)}

    def keep(l, names, arrays):
        for name, arr in zip(names, arrays):
            weights[l][name] = arr.reshape(layer_shape(name))

    mod_steps = _modulation_steps(depth)
    mods, w_in0 = _modulation(cond8, w_mod, b_mod, riders=[rider("w_in", 0, mod_steps)])
    mods = mods.reshape(depth, MOD_ROWS, 6, D_MODEL)
    keep(0, ("w_in",), [w_in0])
    ctx_steps = batch * seq // TM

    def run_layer(x, l, lat):
        seq_len = dec_seq if lat else seq
        wts = weights[l]
        convert_next = lat and l + 1 < depth
        if convert_next:
            proj_names = names_of(l + 1)[:4]
            attn_names = names_of(l + 1)[4:]
            proj_riders = [rider(name, l + 1, n_steps) for name in proj_names]
            attn_riders = [rider(name, l + 1, n_steps) for name in attn_names]
            target = l + 1
        elif l == 0 and not lat:
            proj_names, attn_names, attn_riders, target = names_of(0)[1:], (), [], 0
            proj_riders = [rider(name, 0, ctx_steps) for name in proj_names]
        else:
            proj_names, attn_names, proj_riders, attn_riders, target = (), (), [], [], None
        outs = _in_proj(x, mods, l, norm1_3, wts["w_in"], 0, qn3, kn3,
                        rope_tabs if lat else None, seq_len, riders=proj_riders,
                        caches=None if lat else new_kv)
        xr, yr, q, _, kr, v, gr, ga = outs[:8]
        if target is not None:
            keep(target, proj_names, outs[8:8 + len(proj_names)])
        if not lat:
            new_kv[:] = outs[8 + len(proj_names):]
        out_r, end_f, end_b = _rnn(xr, yr, l, conv_w, conv_b3, gate_w, ba4, bi4, lam4,
                                   state5 if lat else None, seq_len)
        if lat:
            outs = _attn_lat(q, kr, v, cache_k4, cache_v4, l, seq_len, riders=attn_riders)
            out_a = outs[0]
            if target is not None:
                keep(target, attn_names, outs[1:])
        else:
            out_a = _attn_ctx(q, kr, v, seq_len)
        mix_args = (x, out_r, out_a, gr, ga, mods, l, norm2_3, wts["w_rnn_out"],
                    wts["w_attn_out"], wts["w_out"], 0, seq_len, lat)
        if l % 2 == 1:
            x1, h2_rows, rec, counts = _mix_out(*mix_args, router=(l // 2, wr_t, br_pad))
            x = _moe(x1, h2_rows, rec, counts, mods, l, 0, wts["exp_gate_up"], wts["exp_down"],
                     seq_len, lat, final_g if l == depth - 1 else None)
        else:
            x1, h2 = _mix_out(*mix_args)
            x = _ffn(x1, h2, mods, l, 0, wts["ff_gate"], wts["ff_up"], wts["ff_down"],
                     seq_len, lat)
            if l == depth - 1:
                x = _final_norm(x, final_g)
        return x, jnp.stack([end_f, end_b], axis=1)

    new_kv = [jnp.zeros((batch, depth, seq, N_KV_HEADS, HEAD_DIM), F32) for _ in range(2)]
    y_ctx = x_prompt.reshape(batch * seq, D_MODEL)
    y_lat = x_sample.reshape(dec_batch * dec_seq, D_MODEL)
    ss = []
    for l in range(depth):
        y_ctx, st = run_layer(y_ctx, l, False)
        ss.append(st)
        y_lat, _ = run_layer(y_lat, l, True)

    new_state = jnp.stack(ss, axis=1)
    return (y_ctx.reshape(batch, seq, D_MODEL), y_lat.reshape(dec_batch, dec_seq, D_MODEL),
            new_kv[0], new_kv[1], new_state)
```

```python
import functools

import jax
import jax.numpy as jnp
from jax import lax
from jax.experimental import pallas as pl
from jax.experimental.pallas import tpu as pltpu

F32 = jnp.float32
BF16 = jnp.bfloat16

D_MODEL = 1024
D_RNN = 1024
N_RNN_BLOCKS = 16
RNN_BLOCK = D_RNN // N_RNN_BLOCKS
CONV_W = 4
CONV_LEFT = 2
LRU_C = 8.0
N_HEADS = 8
N_KV_HEADS = 2
HEAD_DIM = 128
AXIS_DIM = HEAD_DIM // 2
ROPE_THETA = 10000.0
GRID_W = 64
Q_DIM = N_HEADS * HEAD_DIM
KV_DIM = N_KV_HEADS * HEAD_DIM
N_EXPERTS = 8
EPS = 1e-6
LOG2_E = 1.4426950408889634
RSQRT_FLOOR = 1e-30
D_IN = 2 * D_RNN + Q_DIM + 2 * KV_DIM + 2 * D_MODEL
O_XR, O_YR, O_Q = 0, D_RNN, 2 * D_RNN
O_K = O_Q + Q_DIM
O_V = O_K + KV_DIM
O_GR = O_V + KV_DIM
O_GA = O_GR + D_MODEL

LANES = 128
SUBLANES = 8
MOD_ROWS = 8
VMEM_LIMIT = 56 * 1024 * 1024

TM = 512
TM_ROPE = 256
RESIDENT = pl.Buffered(1)
RNN_T = 256
RNN_SEG = SUBLANES
RNN_C = 512
RNN_TCH = 32
TQ = 256
ATTN_CTX_SEQS = 2
TM_FF = 512
TM_GROUP = 256
DMA_CHUNK = 256


def _cparams(sem):
    return pltpu.CompilerParams(dimension_semantics=sem, vmem_limit_bytes=VMEM_LIMIT)


def _sigmoid(x):
    return 1.0 / (1.0 + jnp.exp(-x))


def _silu(x):
    return x * _sigmoid(x)


def _gelu_tanh(x):
    return 0.5 * x * (1.0 + jnp.tanh(0.7978845608028654 * (x + 0.044715 * (x * x * x))))


def _rms(x, g):
    ms = jnp.mean(x * x, axis=-1, keepdims=True)
    return x * lax.rsqrt(ms + EPS) * g


def _dot(a, b):
    return jnp.dot(a, b, preferred_element_type=F32)


MOD_TN = 1536


def _mod_kernel(*refs, groups):
    n_rin = sum(groups)
    cond_ref, w_ref, b_ref = refs[:3]
    o_ref = refs[3 + n_rin]
    _convert_riders(refs[3:3 + n_rin], refs[4 + n_rin:], groups)
    s = _silu(cond_ref[...]).astype(BF16)
    o_ref[...] = _dot(s, w_ref[...].astype(BF16)) + b_ref[...]


def _modulation_steps(depth):
    return depth * (6 * D_MODEL // MOD_TN)


def _modulation(cond8, w_mod, b_mod, riders=()):
    depth = w_mod.shape[0]
    per_layer = 6 * D_MODEL // MOD_TN
    r_in, r_out, r_shape = _rider_specs(riders, depth * per_layer, lambda l, j: l * per_layer + j)
    return pl.pallas_call(
        functools.partial(_mod_kernel, groups=_rider_groups(riders)),
        grid=(depth, per_layer),
        in_specs=[
            pl.BlockSpec((MOD_ROWS, D_MODEL), lambda l, j: (0, 0)),
            pl.BlockSpec((None, D_MODEL, MOD_TN), lambda l, j: (l, 0, j)),
            pl.BlockSpec((None, 1, MOD_TN), lambda l, j: (l, 0, j)),
        ] + r_in,
        out_specs=[pl.BlockSpec((None, MOD_ROWS, MOD_TN), lambda l, j: (l, 0, j))] + r_out,
        out_shape=[jax.ShapeDtypeStruct((depth, MOD_ROWS, 6 * D_MODEL), F32)] + r_shape,
        compiler_params=_cparams(("arbitrary", "arbitrary")),
        name="adaln_mod",
    )(cond8, w_mod, b_mod.reshape(depth, 1, 6 * D_MODEL), *_rider_args(riders))


def _swap_halves(x):
    lane = lax.broadcasted_iota(jnp.int32, x.shape, 1)
    return jnp.where((lane % AXIS_DIM) < AXIS_DIM // 2,
                     pltpu.roll(x, LANES - AXIS_DIM // 2, axis=1),
                     pltpu.roll(x, AXIS_DIM // 2, axis=1))


class _Rider:
    def __init__(self, w2d, rows, first):
        self.parts = list(w2d) if isinstance(w2d, (list, tuple)) else [w2d]
        self.rows, self.first = rows, first


def _rider_specs(riders, n_steps, step_of):
    in_specs, out_specs, out_shape = [], [], []
    for r in riders:
        for w in r.parts:
            assert r.rows % 16 == 0 and (r.first + n_steps) * r.rows <= w.shape[0]
            in_specs.append(pl.BlockSpec((r.rows, w.shape[1]),
                                         lambda *g, r=r: (r.first + step_of(*g), 0)))
        cols = sum(w.shape[1] for w in r.parts)
        out_specs.append(pl.BlockSpec((r.rows, cols), lambda *g: (step_of(*g), 0)))
        out_shape.append(jax.ShapeDtypeStruct((r.rows * n_steps, cols), BF16))
    return in_specs, out_specs, out_shape


def _rider_groups(riders):
    return tuple(len(r.parts) for r in riders)


def _rider_args(riders):
    return [w for r in riders for w in r.parts]


def _convert_riders(in_refs, out_refs, groups):
    k = 0
    for o_ref, n_parts in zip(out_refs, groups):
        col = 0
        for w_ref in in_refs[k:k + n_parts]:
            o_ref[:, col:col + w_ref.shape[1]] = w_ref[...].astype(BF16)
            col += w_ref.shape[1]
        k += n_parts


def _in_proj_kernel(*refs, rope, groups, cache_layer, seq_len):
    n_in = 8 if rope else 6
    n_rin, n_rout = sum(groups), len(groups)
    n_cache = 0 if cache_layer is None else 2
    first_out = n_in + n_rin + n_cache
    _convert_riders(refs[n_in:n_in + n_rin], refs[first_out + 8:first_out + 8 + n_rout], groups)
    cache_refs = refs[first_out + 8 + n_rout:]
    refs = refs[:n_in] + refs[first_out:first_out + 8]
    if rope:
        (x_ref, mod_ref, n1_ref, w_ref, qn_ref, kn_ref, c_ref, s_ref,
         xr_ref, yr_ref, q_ref, kno_ref, kro_ref, v_ref, gr_ref, ga_ref) = refs
    else:
        (x_ref, mod_ref, n1_ref, w_ref, qn_ref, kn_ref,
         xr_ref, yr_ref, q_ref, kno_ref, kro_ref, v_ref, gr_ref, ga_ref) = refs
    x = x_ref[...]
    h = _rms(x, n1_ref[...]) * (1.0 + mod_ref[1:2, :]) + mod_ref[0:1, :]
    hb = h.astype(BF16)

    xr_ref[...] = _dot(hb, w_ref[:, O_XR:O_XR + D_RNN])
    yr_ref[...] = _dot(hb, w_ref[:, O_YR:O_YR + D_RNN])
    gr_ref[...] = _dot(hb, w_ref[:, O_GR:O_GR + D_MODEL]).astype(BF16)
    ga_ref[...] = _dot(hb, w_ref[:, O_GA:O_GA + D_MODEL]).astype(BF16)
    v_ref[...] = _dot(hb, w_ref[:, O_V:O_V + KV_DIM])

    if rope:
        cs, sn = c_ref[...], s_ref[...]
    zq = _dot(hb, w_ref[:, O_Q:O_Q + Q_DIM])
    for hd in range(N_HEADS):
        t = _rms(zq[:, hd * HEAD_DIM:(hd + 1) * HEAD_DIM], qn_ref[...])
        if rope:
            t = t * cs + _swap_halves(t) * sn
        q_ref[:, hd * HEAD_DIM:(hd + 1) * HEAD_DIM] = t.astype(BF16)
    zk = _dot(hb, w_ref[:, O_K:O_K + KV_DIM])
    for hd in range(N_KV_HEADS):
        t = _rms(zk[:, hd * HEAD_DIM:(hd + 1) * HEAD_DIM], kn_ref[...])
        kno_ref[:, hd * HEAD_DIM:(hd + 1) * HEAD_DIM] = t
        if rope:
            t = t * cs + _swap_halves(t) * sn
        kro_ref[:, hd * HEAD_DIM:(hd + 1) * HEAD_DIM] = t.astype(BF16)

    if cache_layer is None:
        return
    newk_ref, newv_ref, kbuf, vbuf, csem = cache_refs
    step = pl.program_id(0)
    slot = step % 2
    seqs_per_tile = x_ref.shape[0] // seq_len

    def cache_copies(k, s):
        cps = []
        for sq in range(seqs_per_tile):
            rows = slice(sq * seq_len, (sq + 1) * seq_len)
            for hd in range(N_KV_HEADS):
                cols = slice(hd * HEAD_DIM, (hd + 1) * HEAD_DIM)
                dst = (k * seqs_per_tile + sq, cache_layer, slice(None), hd, slice(None))
                cps.append(pltpu.make_async_copy(kbuf.at[s, rows, cols], newk_ref.at[dst], csem.at[s]))
                cps.append(pltpu.make_async_copy(vbuf.at[s, rows, cols], newv_ref.at[dst], csem.at[s]))
        return cps

    @pl.when(step >= 2)
    def _():
        for cp in cache_copies(step - 2, slot):
            cp.wait()

    kbuf[slot] = kno_ref[...]
    vbuf[slot] = v_ref[...]
    for cp in cache_copies(step, slot):
        cp.start()

    @pl.when(step == pl.num_programs(0) - 1)
    def _():
        for cp in cache_copies(step, slot):
            cp.wait()

        @pl.when(step >= 1)
        def _():
            for cp in cache_copies(step - 1, 1 - slot):
                cp.wait()


def _in_proj(x, mods, l, norm1, w_in_bf, wl, q_norm, k_norm, rope_tabs, seq_len, riders=(),
             caches=None):
    n = x.shape[0]
    rope = rope_tabs is not None
    tm = TM_ROPE if rope else TM
    tiles_per_seq = seq_len // tm
    if rope:
        mod_idx = lambda i: (l, 1 + i // tiles_per_seq, 0, 0)
    else:
        mod_idx = lambda i: (l, 0, 0, 0)
    row = lambda i: (i, 0)
    in_specs = [
        pl.BlockSpec((tm, D_MODEL), row),
        pl.BlockSpec((None, None, 6, D_MODEL), mod_idx),
        pl.BlockSpec((None, 1, D_MODEL), lambda i: (l, 0, 0)),
        pl.BlockSpec((None, D_MODEL, D_IN), lambda i: (wl, 0, 0), pipeline_mode=RESIDENT),
        pl.BlockSpec((None, 1, HEAD_DIM), lambda i: (l, 0, 0)),
        pl.BlockSpec((None, 1, HEAD_DIM), lambda i: (l, 0, 0)),
    ]
    args = [x, mods, norm1, w_in_bf, q_norm, k_norm]
    if rope:
        in_specs += [pl.BlockSpec((tm, HEAD_DIM), lambda i: (i % tiles_per_seq, 0))] * 2
        args += list(rope_tabs)
    widths = [(D_RNN, F32), (D_RNN, F32), (Q_DIM, BF16), (KV_DIM, F32), (KV_DIM, BF16),
              (KV_DIM, F32), (D_MODEL, BF16), (D_MODEL, BF16)]
    r_in, r_out, r_shape = _rider_specs(riders, n // tm, lambda i: i)
    in_specs = in_specs + r_in
    out_specs = [pl.BlockSpec((tm, w), row) for w, _ in widths] + r_out
    out_shape = [jax.ShapeDtypeStruct((n, w), dt) for w, dt in widths] + r_shape
    args = args + _rider_args(riders)
    scratch, aliases = [], {}
    if caches is not None:
        assert tm % seq_len == 0
        for c in caches:
            aliases[len(args)] = len(out_shape)
            args.append(c)
            in_specs.append(pl.BlockSpec(memory_space=pl.ANY))
            out_specs.append(pl.BlockSpec(memory_space=pl.ANY))
            out_shape.append(jax.ShapeDtypeStruct(c.shape, c.dtype))
        scratch = [pltpu.VMEM((2, tm, KV_DIM), F32), pltpu.VMEM((2, tm, KV_DIM), F32),
                   pltpu.SemaphoreType.DMA((2,))]
    return pl.pallas_call(
        functools.partial(_in_proj_kernel, rope=rope, groups=_rider_groups(riders),
                          cache_layer=None if caches is None else l, seq_len=seq_len),
        grid=(n // tm,),
        in_specs=in_specs,
        out_specs=out_specs,
        out_shape=out_shape,
        scratch_shapes=scratch,
        input_output_aliases=aliases,
        compiler_params=_cparams(("arbitrary",)),
        name="in_proj_lat" if rope else "in_proj_ctx",
    )(*args)


def _rnn_kernel(*refs, chained):
    refs = list(refs)
    xr_ref, yr_ref, cw_ref, cb_ref, gw_ref, ba_ref, bi_ref, lam_ref = refs[:8]
    refs = refs[8:]
    h0_ref = refs.pop(0) if chained else None
    out_ref, sf_ref, sb_ref, xext, ybuf, obuf, a_f, b_f, a_b, b_b, sem_in, sem_out = refs

    g = pl.program_id(0)
    cb = pl.program_id(1)
    n_cb = pl.num_programs(1)
    step = g * n_cb + cb
    n_steps = pl.num_programs(0) * n_cb
    slot = step % 2

    def hbm_window(kk, s):
        rows = pl.ds(pl.multiple_of(((kk // n_cb) * RNN_SEG + s) * RNN_T, RNN_T), RNN_T)
        cols = pl.ds(pl.multiple_of((kk % n_cb) * RNN_C, RNN_C), RNN_C)
        return rows, cols

    def in_copies(kk, sl_):
        cps = []
        for s in range(RNN_SEG):
            win = hbm_window(kk, s)
            cps.append(pltpu.make_async_copy(
                xr_ref.at[win], xext.at[sl_, pl.ds(CONV_LEFT, RNN_T), s], sem_in.at[sl_]))
            cps.append(pltpu.make_async_copy(yr_ref.at[win], ybuf.at[sl_, :, s], sem_in.at[sl_]))
        return cps

    def out_copies(kk, sl_):
        return [pltpu.make_async_copy(obuf.at[sl_, :, s], out_ref.at[hbm_window(kk, s)],
                                      sem_out.at[sl_]) for s in range(RNN_SEG)]

    @pl.when(step == 0)
    def _():
        for cp in in_copies(0, 0):
            cp.start()

    @pl.when(step + 1 < n_steps)
    def _():
        for cp in in_copies(step + 1, 1 - slot):
            cp.start()

    for cp in in_copies(step, slot):
        cp.wait()

    seg = lax.broadcasted_iota(jnp.int32, (RNN_SEG, RNN_C), 0)
    zero_row = jnp.zeros((RNN_SEG, RNN_C), F32)
    if chained:
        xext[slot, 0] = jnp.where(seg == 0, zero_row, pltpu.roll(xext[slot, RNN_T], 1, axis=0))
        xext[slot, 1] = jnp.where(seg == 0, zero_row, pltpu.roll(xext[slot, RNN_T + 1], 1, axis=0))
        xext[slot, RNN_T + CONV_LEFT] = jnp.where(
            seg == RNN_SEG - 1, zero_row, pltpu.roll(xext[slot, CONV_LEFT], RNN_SEG - 1, axis=0))
    else:
        xext[slot, 0] = zero_row
        xext[slot, 1] = zero_row
        xext[slot, RNN_T + CONV_LEFT] = zero_row

    scans = ((a_f, b_f), (a_b, b_b))
    rows2d = RNN_TCH * RNN_SEG

    def gate_chunk(tc, carry):
        t0 = pl.multiple_of(tc * RNN_TCH, RNN_TCH)
        for j in range(RNN_C // LANES):
            sl = slice(j * LANES, (j + 1) * LANES)
            xc = jnp.broadcast_to(cb_ref[:, sl].reshape(1, 1, LANES), (RNN_TCH, RNN_SEG, LANES))
            for k in range(CONV_W):
                xc = xc + (xext[slot, pl.ds(t0 + k, RNN_TCH), :, sl]
                           * cw_ref[k:k + 1, sl].reshape(1, 1, LANES))
            xc = xc.reshape(rows2d, LANES)
            pre = _dot(xc.astype(BF16), gw_ref[j])
            half_xc = 0.5 * xc
            for d, (a_s, b_s) in enumerate(scans):
                lam = lam_ref[d, :, sl]
                log2_decay = (-LRU_C * LOG2_E) * (jnp.maximum(-lam, 0.0)
                                                  + jnp.log(1.0 + jnp.exp(-jnp.abs(lam))))
                t_r = jnp.tanh(pre[:, 2 * d * LANES:(2 * d + 1) * LANES] + ba_ref[d, :, sl])
                t_i = jnp.tanh(pre[:, (2 * d + 1) * LANES:(2 * d + 2) * LANES]
                               + bi_ref[d, :, sl])
                half_decay = 0.5 * log2_decay
                a = jnp.exp2(t_r * half_decay + half_decay)
                y = 1.0 - a * a
                b = (y * lax.rsqrt(jnp.maximum(y, RSQRT_FLOOR))) * ((t_i + 1.0) * half_xc)
                a_s[pl.ds(t0, RNN_TCH), :, sl] = a.reshape(RNN_TCH, RNN_SEG, LANES)
                b_s[pl.ds(t0, RNN_TCH), :, sl] = b.reshape(RNN_TCH, RNN_SEG, LANES)
        return carry

    n_chunks = RNN_T // RNN_TCH
    ones = jnp.ones((RNN_SEG, RNN_C), F32)

    def fwd_scan_chunk(tc, hf, pf):
        for u in range(RNN_TCH):
            t = tc * RNN_TCH + u
            af_t = a_f[t]
            hf = af_t * hf + b_f[t]
            b_f[t] = hf
            if chained:
                pf = af_t * pf
                a_f[t] = pf
        return hf, pf

    def gate_and_scan(tc, carry):
        carry = fwd_scan_chunk(tc - 1, *carry)
        gate_chunk(tc, 0)
        return carry

    gate_chunk(0, 0)
    hf, pf = lax.fori_loop(1, n_chunks, gate_and_scan, (zero_row, ones))
    hf, pf = fwd_scan_chunk(n_chunks - 1, hf, pf)

    @pl.when(step >= 2)
    def _():
        for cp in out_copies(step - 2, slot):
            cp.wait()

    if not chained:
        def bwd_finish_chunk(i, hb):
            tc = n_chunks - 1 - i
            for u in reversed(range(RNN_TCH)):
                t = tc * RNN_TCH + u
                hb = a_b[t] * hb + b_b[t]
                obuf[slot, t] = (b_f[t] + hb) * _gelu_tanh(ybuf[slot, t])
            return hb

        hb = lax.fori_loop(0, n_chunks, bwd_finish_chunk, zero_row)
        sf_ref[...] = hf
        sb_ref[...] = hb
    else:
        st_f = jnp.broadcast_to(h0_ref[0], (RNN_SEG, RNN_C))
        st_b = jnp.broadcast_to(h0_ref[1], (RNN_SEG, RNN_C))
        in_f = st_f
        for _ in range(RNN_SEG - 1):
            in_f = jnp.where(seg == 0, st_f, pltpu.roll(pf * in_f + hf, 1, axis=0))
        sf_ref[...] = pf * in_f + hf

        def bwd_scan_chunk(i, carry):
            hb, pb = carry
            tc = n_chunks - 1 - i
            for u in reversed(range(RNN_TCH)):
                t = tc * RNN_TCH + u
                ab_t = a_b[t]
                hb = ab_t * hb + b_b[t]
                b_b[t] = hb
                pb = ab_t * pb
                a_b[t] = pb
                b_f[t] = a_f[t] * in_f + b_f[t]
            return hb, pb

        hb, pb = lax.fori_loop(0, n_chunks, bwd_scan_chunk, (zero_row, ones))
        in_b = st_b
        for _ in range(RNN_SEG - 1):
            in_b = jnp.where(seg == RNN_SEG - 1, st_b,
                             pltpu.roll(pb * in_b + hb, RNN_SEG - 1, axis=0))
        sb_ref[...] = pb * in_b + hb

        def finish_chunk(tc, carry):
            rows = pl.ds(pl.multiple_of(tc * RNN_TCH, RNN_TCH), RNN_TCH)
            h_b = a_b[rows] * in_b.reshape(1, RNN_SEG, RNN_C) + b_b[rows]
            obuf[slot, rows] = (b_f[rows] + h_b) * _gelu_tanh(ybuf[slot, rows])
            return carry

        lax.fori_loop(0, n_chunks, finish_chunk, 0)

    for cp in out_copies(step, slot):
        cp.start()

    @pl.when(step == n_steps - 1)
    def _():
        for cp in out_copies(step, slot):
            cp.wait()

        @pl.when(step >= 1)
        def _():
            for cp in out_copies(step - 1, 1 - slot):
                cp.wait()


def _rnn(xr, yr, l, conv_w, conv_b, gate_w, lru_ba, lru_bi, lru_lambda, state5, seq_len):
    n = xr.shape[0]
    chained = state5 is not None
    group_len = RNN_SEG * RNN_T
    assert n % group_len == 0 and seq_len in ((group_len,) if chained else (RNN_T,))
    n_groups = n // group_len
    lanes_per_step = RNN_C // LANES
    vec = pl.BlockSpec((None, 2, 1, RNN_C), lambda g, c: (l, 0, 0, c))
    in_specs = [
        pl.BlockSpec(memory_space=pl.ANY),
        pl.BlockSpec(memory_space=pl.ANY),
        pl.BlockSpec((None, CONV_W, RNN_C), lambda g, c: (l, 0, c)),
        pl.BlockSpec((None, 1, RNN_C), lambda g, c: (l, 0, c)),
        pl.BlockSpec((None, lanes_per_step, LANES, 4 * LANES), lambda g, c: (l, c, 0, 0)),
        vec, vec, vec,
    ]
    args = [xr, yr, conv_w, conv_b, gate_w, lru_ba, lru_bi, lru_lambda]
    if chained:
        in_specs.append(pl.BlockSpec((None, None, 2, 1, RNN_C), lambda g, c: (g, l, 0, 0, c)))
        args.append(state5)
    state_spec = pl.BlockSpec((RNN_SEG, RNN_C), lambda g, c: (g, c))
    state_shape = jax.ShapeDtypeStruct((n_groups * RNN_SEG, D_RNN), F32)
    work = lambda t: pltpu.VMEM((t, RNN_SEG, RNN_C), F32)
    return pl.pallas_call(
        functools.partial(_rnn_kernel, chained=chained),
        grid=(n_groups, D_RNN // RNN_C),
        in_specs=in_specs,
        out_specs=[pl.BlockSpec(memory_space=pl.ANY), state_spec, state_spec],
        out_shape=[jax.ShapeDtypeStruct((n, D_RNN), F32), state_shape, state_shape],
        scratch_shapes=[
            pltpu.VMEM((2, RNN_T + CONV_W - 1, RNN_SEG, RNN_C), F32),
            pltpu.VMEM((2, RNN_T, RNN_SEG, RNN_C), F32),
            pltpu.VMEM((2, RNN_T, RNN_SEG, RNN_C), F32),
            work(RNN_T), work(RNN_T), work(RNN_T), work(RNN_T),
            pltpu.SemaphoreType.DMA((2,)), pltpu.SemaphoreType.DMA((2,)),
        ],
        compiler_params=_cparams(("arbitrary", "arbitrary")),
        name="rnn_lat" if chained else "rnn_ctx",
    )(*args)


def _attend_heads(q_ref, k_all, v_all, o_ref, stack_heads):
    exp2_scale = (1.0 / (HEAD_DIM ** 0.5)) * LOG2_E
    group = N_HEADS // N_KV_HEADS
    tq = q_ref.shape[0]
    for kh in range(N_KV_HEADS):
        k = k_all[:, kh * HEAD_DIM:(kh + 1) * HEAD_DIM]
        v = v_all[:, kh * HEAD_DIM:(kh + 1) * HEAD_DIM]
        heads = [slice((kh * group + g) * HEAD_DIM, (kh * group + g + 1) * HEAD_DIM)
                 for g in range(group)]
        stacks = [heads] if stack_heads else [[sl] for sl in heads]
        for stack in stacks:
            q = jnp.concatenate([q_ref[:, sl] for sl in stack], axis=0)
            s = lax.dot_general(q, k, (((1,), (1,)), ((), ())), preferred_element_type=F32)
            m = jnp.max(s, axis=-1, keepdims=True)
            p = jnp.exp2((s - m) * exp2_scale)
            denom = jnp.sum(p, axis=-1, keepdims=True)
            o = (_dot(p.astype(BF16), v) / denom).astype(BF16)
            for g, sl in enumerate(stack):
                o_ref[:, sl] = o[g * tq:(g + 1) * tq]


def _attn_ctx_kernel(q_ref, k_ref, v_ref, o_ref, *, seq_len):
    for s in range(q_ref.shape[0] // seq_len):
        rows = pl.ds(s * seq_len, seq_len)
        _attend_heads(q_ref.at[rows], k_ref[rows, :], v_ref[rows, :].astype(BF16), o_ref.at[rows],
                      stack_heads=True)


def _attn_ctx(q, kr, v, seq_len):
    n = q.shape[0]
    row = lambda b: (b, 0)
    tile = ATTN_CTX_SEQS * seq_len
    return pl.pallas_call(
        functools.partial(_attn_ctx_kernel, seq_len=seq_len),
        grid=(n // tile,),
        in_specs=[pl.BlockSpec((tile, Q_DIM), row),
                  pl.BlockSpec((tile, KV_DIM), row),
                  pl.BlockSpec((tile, KV_DIM), row)],
        out_specs=pl.BlockSpec((tile, Q_DIM), row),
        out_shape=jax.ShapeDtypeStruct((n, Q_DIM), BF16),
        compiler_params=_cparams(("arbitrary",)),
        name="attn_ctx",
    )(q, kr, v)


def _attn_lat_kernel(*refs, past, groups):
    n_rin, n_rout = sum(groups), len(groups)
    _convert_riders(refs[5:5 + n_rin], refs[6 + n_rin:6 + n_rin + n_rout], groups)
    q_ref, kc_ref, vc_ref, kl_ref, vl_ref = refs[:5]
    o_ref = refs[5 + n_rin]
    k_all, v_all = refs[6 + n_rin + n_rout:]

    @pl.when(pl.program_id(1) == 0)
    def _():
        k_all[0:past, :] = kc_ref[...].astype(BF16)
        k_all[past:, :] = kl_ref[...]
        v_all[0:past, :] = vc_ref[...].astype(BF16)
        v_all[past:, :] = vl_ref[...].astype(BF16)

    _attend_heads(q_ref, k_all[...], v_all[...], o_ref, stack_heads=False)


def _attn_lat(q, kr, v, cache_k4, cache_v4, l, seq_len, riders=()):
    n = q.shape[0]
    past = cache_k4.shape[2]
    nq = seq_len // TQ
    step_of = lambda b, j: b * nq + j
    r_in, r_out, r_shape = _rider_specs(riders, (n // seq_len) * nq, step_of)
    return pl.pallas_call(
        functools.partial(_attn_lat_kernel, past=past, groups=_rider_groups(riders)),
        grid=(n // seq_len, nq),
        in_specs=[pl.BlockSpec((TQ, Q_DIM), lambda b, j: (b * nq + j, 0)),
                  pl.BlockSpec((None, None, past, KV_DIM), lambda b, j: (b, l, 0, 0)),
                  pl.BlockSpec((None, None, past, KV_DIM), lambda b, j: (b, l, 0, 0)),
                  pl.BlockSpec((seq_len, KV_DIM), lambda b, j: (b, 0)),
                  pl.BlockSpec((seq_len, KV_DIM), lambda b, j: (b, 0))] + r_in,
        out_specs=[pl.BlockSpec((TQ, Q_DIM), lambda b, j: (b * nq + j, 0))] + r_out,
        out_shape=[jax.ShapeDtypeStruct((n, Q_DIM), BF16)] + r_shape,
        scratch_shapes=[pltpu.VMEM((past + seq_len, KV_DIM), BF16),
                        pltpu.VMEM((past + seq_len, KV_DIM), BF16)],
        compiler_params=_cparams(("arbitrary", "arbitrary")),
        name="attn_lat",
    )(q, cache_k4, cache_v4, kr, v, *_rider_args(riders))


R_E1, R_E2, R_W1, R_W2, R_RANK1, R_RANK2 = range(6)


def _route(h2, wrt_ref, brt_ref, count_ref):
    shape = (h2.shape[0], LANES)
    lane = lax.broadcasted_iota(jnp.int32, shape, 1).astype(F32)
    neg = jnp.float32(-jnp.inf)
    h_hi = h2.astype(BF16)
    h_lo = (h2 - h_hi.astype(F32)).astype(BF16)
    z_hi = _dot(h_hi, wrt_ref[...])
    z_lo = _dot(h_lo, wrt_ref[...])
    logits = ((z_hi[:, :LANES] + z_hi[:, LANES:]) + (z_lo[:, :LANES] + z_lo[:, LANES:])
              + brt_ref[...])
    logits = jnp.where(lane < N_EXPERTS, logits, neg)
    no_lane = jnp.float32(LANES)
    v1 = jnp.max(logits, axis=-1, keepdims=True)
    i1 = jnp.min(jnp.where(logits == v1, lane, no_lane), axis=-1, keepdims=True)
    rest = jnp.where(lane == i1, neg, logits)
    v2 = jnp.max(rest, axis=-1, keepdims=True)
    i2 = jnp.min(jnp.where(rest == v2, lane, no_lane), axis=-1, keepdims=True)
    e2 = jnp.exp(v2 - v1)
    w1 = 1.0 / (1.0 + e2)
    w2 = e2 / (1.0 + e2)
    chosen = jnp.where((lane == i1) | (lane == i2), 1.0, 0.0)
    r_id = lax.broadcasted_iota(jnp.int32, (shape[0], shape[0]), 0)
    c_id = lax.broadcasted_iota(jnp.int32, (shape[0], shape[0]), 1)
    before = jnp.where(c_id < r_id, 1.0, 0.0).astype(BF16)
    rank = _dot(before, chosen.astype(BF16)) + count_ref[...]
    rank1 = jnp.sum(jnp.where(lane == i1, rank, 0.0), axis=-1, keepdims=True)
    rank2 = jnp.sum(jnp.where(lane == i2, rank, 0.0), axis=-1, keepdims=True)
    count_ref[...] += jnp.sum(chosen, axis=0, keepdims=True)
    rec = jnp.zeros(shape, F32)
    for k, val in ((R_E1, i1), (R_E2, i2), (R_W1, w1), (R_W2, w2),
                   (R_RANK1, rank1), (R_RANK2, rank2)):
        rec = jnp.where(lane == float(k), val, rec)
    return rec


def _mix_out_kernel(*refs, route):
    (x_ref, r_ref, a_ref, gr_ref, ga_ref, mod_ref, n2_ref, wr_ref, wa_ref, wo_ref) = refs[:10]
    if route:
        wrt_ref, brt_ref, x1_ref, h2_ref, rec_ref, cnt_ref, count_ref, hbuf, hsem = refs[10:]
    else:
        x1_ref, h2_ref = refs[10:]
    merged = (_sigmoid(gr_ref[...].astype(F32)) * _dot(r_ref[...].astype(BF16), wr_ref[...])
              + _sigmoid(ga_ref[...].astype(F32)) * _dot(a_ref[...], wa_ref[...]))
    mix = _dot(merged.astype(BF16), wo_ref[...])
    x1 = x_ref[...] + mod_ref[2:3, :] * mix
    x1_ref[...] = x1
    h2 = _rms(x1, n2_ref[...]) * (1.0 + mod_ref[4:5, :]) + mod_ref[3:4, :]
    if not route:
        h2_ref[...] = h2.astype(BF16)
        return

    @pl.when(pl.program_id(0) == 0)
    def _():
        count_ref[...] = jnp.zeros_like(count_ref)

    rec_ref[...] = _route(h2, wrt_ref, brt_ref, count_ref)
    cnt_ref[...] = count_ref[...]

    step = pl.program_id(0)
    slot = step % 2

    def h_copies(k, s):
        rows = pl.ds(k * TM, TM)
        return [pltpu.make_async_copy(hbuf.at[s, :, c * LANES:(c + 1) * LANES], h2_ref.at[rows, c],
                                      hsem.at[s]) for c in range(D_MODEL // LANES)]

    @pl.when(step >= 2)
    def _():
        for cp in h_copies(step - 2, slot):
            cp.wait()

    hbuf[slot] = h2
    for cp in h_copies(step, slot):
        cp.start()

    @pl.when(step == pl.num_programs(0) - 1)
    def _():
        for cp in h_copies(step, slot):
            cp.wait()

        @pl.when(step >= 1)
        def _():
            for cp in h_copies(step - 1, 1 - slot):
                cp.wait()


def _mix_out(x, out_r, out_a, gr, ga, mods, l, norm2, wr_bf, wa_bf, wo_bf, wl, seq_len, lat,
             router=None):
    n = x.shape[0]
    tiles_per_seq = seq_len // TM
    if lat:
        mod_idx = lambda i: (l, 1 + i // tiles_per_seq, 0, 0)
    else:
        mod_idx = lambda i: (l, 0, 0, 0)
    row = lambda i: (i, 0)
    wspec = pl.BlockSpec((None, D_MODEL, D_MODEL), lambda i: (wl, 0, 0), pipeline_mode=RESIDENT)
    in_specs = [pl.BlockSpec((TM, D_MODEL), row)] * 5 + [
        pl.BlockSpec((None, None, 6, D_MODEL), mod_idx),
        pl.BlockSpec((None, 1, D_MODEL), lambda i: (l, 0, 0)),
        wspec, wspec, wspec]
    args = [x, out_r, out_a, gr, ga, mods, norm2, wr_bf, wa_bf, wo_bf]
    out_specs = [pl.BlockSpec((TM, D_MODEL), row)]
    out_shape = [jax.ShapeDtypeStruct((n, D_MODEL), F32)]
    scratch = []
    if router is None:
        out_specs.append(pl.BlockSpec((TM, D_MODEL), row))
        out_shape.append(jax.ShapeDtypeStruct((n, D_MODEL), BF16))
    else:
        j, wr_t, br_pad = router
        in_specs += [pl.BlockSpec((None, D_MODEL, 2 * LANES), lambda i: (j, 0, 0)),
                     pl.BlockSpec((None, 1, LANES), lambda i: (j, 0, 0))]
        args += [wr_t, br_pad]
        out_specs += [pl.BlockSpec(memory_space=pl.ANY),
                      pl.BlockSpec((TM, LANES), row),
                      pl.BlockSpec((1, LANES), lambda i: (0, 0))]
        out_shape += [jax.ShapeDtypeStruct((n, D_MODEL // LANES, LANES), F32),
                      jax.ShapeDtypeStruct((n, LANES), F32),
                      jax.ShapeDtypeStruct((1, LANES), F32)]
        scratch = [pltpu.VMEM((1, LANES), F32), pltpu.VMEM((2, TM, D_MODEL), F32),
                   pltpu.SemaphoreType.DMA((2,))]
    return pl.pallas_call(
        functools.partial(_mix_out_kernel, route=router is not None),
        grid=(n // TM,),
        in_specs=in_specs,
        out_specs=out_specs,
        out_shape=out_shape,
        scratch_shapes=scratch,
        compiler_params=_cparams(("arbitrary",)),
        name=("mix_route" if router is not None else "mix_out") + ("_lat" if lat else "_ctx"),
    )(*args)


def _ffn_kernel(x_ref, h_ref, mod_ref, wg_ref, wu_ref, wd_ref, o_ref, *, f_chunk):
    h = h_ref[...]
    d_ff = wg_ref.shape[1]
    acc = jnp.zeros((h.shape[0], D_MODEL), F32)
    for c in range(d_ff // f_chunk):
        sl = slice(c * f_chunk, (c + 1) * f_chunk)
        act = _silu(_dot(h, wg_ref[:, sl])) * _dot(h, wu_ref[:, sl])
        acc = acc + _dot(act.astype(BF16), wd_ref[sl, :])
    o_ref[...] = x_ref[...] + mod_ref[5:6, :] * acc


def _ffn(x1, h2, mods, l, j, wg_bf, wu_bf, wd_bf, seq_len, lat):
    n = x1.shape[0]
    d_ff = wg_bf.shape[2]
    tiles_per_seq = seq_len // TM
    if lat:
        mod_idx = lambda i: (l, 1 + i // tiles_per_seq, 0, 0)
    else:
        mod_idx = lambda i: (l, 0, 0, 0)
    row = lambda i: (i, 0)
    return pl.pallas_call(
        functools.partial(_ffn_kernel, f_chunk=d_ff // 2),
        grid=(n // TM,),
        in_specs=[pl.BlockSpec((TM, D_MODEL), row), pl.BlockSpec((TM, D_MODEL), row),
                  pl.BlockSpec((None, None, 6, D_MODEL), mod_idx),
                  pl.BlockSpec((None, D_MODEL, d_ff), lambda i: (j, 0, 0), pipeline_mode=RESIDENT),
                  pl.BlockSpec((None, D_MODEL, d_ff), lambda i: (j, 0, 0), pipeline_mode=RESIDENT),
                  pl.BlockSpec((None, d_ff, D_MODEL), lambda i: (j, 0, 0), pipeline_mode=RESIDENT)],
        out_specs=pl.BlockSpec((TM, D_MODEL), row),
        out_shape=jax.ShapeDtypeStruct((n, D_MODEL), F32),
        compiler_params=_cparams(("arbitrary",)),
        name="ffn_lat" if lat else "ffn_ctx",
    )(x1, h2, mods, wg_bf, wu_bf, wd_bf)


def _group_layout(rec, counts, n_tiles):
    cnt = counts[0, :N_EXPERTS].astype(jnp.int32)
    tiles = (cnt + TM_GROUP - 1) // TM_GROUP
    cum = jnp.cumsum(tiles)
    start = (cum - tiles) * TM_GROUP
    experts = jnp.arange(N_EXPERTS, dtype=jnp.int32)

    assert (R_E2, R_RANK2) == (R_E1 + 1, R_RANK1 + 1)
    e12 = rec[:, R_E1:R_E2 + 1].astype(jnp.int32)
    rank12 = rec[:, R_RANK1:R_RANK2 + 1].astype(jnp.int32)
    group_start = jnp.sum(jnp.where(e12[:, :, None] == experts, start, 0), axis=2)
    pos2 = (group_start + rank12).T
    n_active = cum[N_EXPERTS - 1]
    tile_expert = jnp.sum(jnp.arange(n_tiles, dtype=jnp.int32)[:, None] >= cum[None, :], axis=1)
    tile_expert = jnp.minimum(tile_expert, tile_expert[n_active - 1]).astype(jnp.int32)
    tile_ids = jnp.arange(n_tiles, dtype=jnp.int32)
    group_last = jnp.any((tile_ids[:, None] == cum[None, :] - 1) & (tiles[None, :] > 0), axis=1)
    zero_tile = (group_last | (tile_ids >= n_active)).astype(jnp.int32)
    return (pos2.astype(jnp.int32), tile_expert, n_active.reshape(1).astype(jnp.int32), zero_tile)


def _row_copy(src, dst, sem):
    return pltpu.make_async_copy(src, dst, sem)


def _dispatch_kernel(pos_ref, zt_ref, h_ref, xs_ref, zbuf, sem, zsem):
    n_chunks = h_ref.shape[0] // DMA_CHUNK
    n_tiles = xs_ref.shape[0] // TM_GROUP

    zbuf[...] = jnp.zeros_like(zbuf)

    def zero_fill(j):
        return pltpu.make_async_copy(zbuf, xs_ref.at[pl.ds(j * TM_GROUP, TM_GROUP)], zsem)

    def zero_start(j, carry):
        @pl.when(zt_ref[j] != 0)
        def _():
            zero_fill(j).start()
        return carry

    def zero_wait(j, carry):
        @pl.when(zt_ref[j] != 0)
        def _():
            zero_fill(j).wait()
        return carry

    lax.fori_loop(0, n_tiles, zero_start, 0)
    lax.fori_loop(0, n_tiles, zero_wait, 0)

    def drain_chunk():
        rows = pl.ds(0, 2 * DMA_CHUNK)
        _row_copy(xs_ref.at[rows], xs_ref.at[rows], sem).wait()

    def chunk(k, carry):
        for r in range(DMA_CHUNK):
            t = k * DMA_CHUNK + r
            _row_copy(h_ref.at[t], xs_ref.at[pos_ref[0, t]], sem).start(priority=0)
            _row_copy(h_ref.at[t], xs_ref.at[pos_ref[1, t]], sem).start(priority=1)

        @pl.when(k > 0)
        def _():
            drain_chunk()

        return carry

    lax.fori_loop(0, n_chunks, chunk, 0)
    drain_chunk()


def _dispatch(pos2, zero_tile, h2_rows, n_rows):
    tile = h2_rows.shape[1:]
    return pl.pallas_call(
        _dispatch_kernel,
        grid_spec=pltpu.PrefetchScalarGridSpec(
            num_scalar_prefetch=2, grid=(1,),
            in_specs=[pl.BlockSpec(h2_rows.shape, lambda i, p, z: (0, 0, 0))],
            out_specs=pl.BlockSpec(memory_space=pl.ANY),
            scratch_shapes=[pltpu.VMEM((TM_GROUP,) + tile, F32),
                            pltpu.SemaphoreType.DMA(()), pltpu.SemaphoreType.DMA(())]),
        out_shape=jax.ShapeDtypeStruct((n_rows,) + tile, F32),
        compiler_params=_cparams(("arbitrary",)),
        name="moe_dispatch",
    )(pos2, zero_tile, h2_rows)


def _expert_kernel(te_ref, na_ref, xs_ref, wgu_ref, wd_ref, ys_ref,
                   xbuf, ybuf, sem_in, sem_out):
    del te_ref
    i = pl.program_id(0)
    last = pl.num_programs(0) - 1
    n_active = na_ref[0]
    slot = i % 2
    n_col = D_MODEL // LANES

    def in_copies(tile, s):
        rows = pl.ds(tile * TM_GROUP, TM_GROUP)
        return [pltpu.make_async_copy(xs_ref.at[rows, c], xbuf.at[s, :, c * LANES:(c + 1) * LANES],
                                      sem_in.at[s]) for c in range(n_col)]

    def out_copies(tile, s):
        rows = pl.ds(tile * TM_GROUP, TM_GROUP)
        return [pltpu.make_async_copy(ybuf.at[s, :, c * LANES:(c + 1) * LANES], ys_ref.at[rows, c],
                                      sem_out.at[s]) for c in range(n_col)]

    @pl.when(i == 0)
    def _():
        for cp in in_copies(0, 0):
            cp.start()

    @pl.when(i + 1 < n_active)
    def _():
        for cp in in_copies(i + 1, 1 - slot):
            cp.start()

    @pl.when(i >= 2)
    def _():
        for cp in out_copies(i - 2, slot):
            cp.wait()

    @pl.when(i < n_active)
    def _():
        for cp in in_copies(i, slot):
            cp.wait()
        xb = xbuf[slot].astype(BF16)
        z = _dot(xb, wgu_ref[...])
        d_e = wd_ref.shape[0]
        act = _silu(z[:, :d_e]) * z[:, d_e:]
        ybuf[slot] = _dot(act.astype(BF16), wd_ref[...])

    @pl.when(i >= n_active)
    def _():
        ybuf[slot] = jnp.zeros((TM_GROUP, D_MODEL), F32)

    for cp in out_copies(i, slot):
        cp.start()

    @pl.when(i == last)
    def _():
        for cp in out_copies(i, slot):
            cp.wait()

        @pl.when(i >= 1)
        def _():
            for cp in out_copies(i - 1, 1 - slot):
                cp.wait()


def _experts(tile_expert, n_active, xs, j, wgu_bf, wd_bf):
    n_rows = xs.shape[0]
    d_e = wd_bf.shape[2]
    wspec = lambda shape: pl.BlockSpec((None, None) + shape, lambda i, te, na: (j, te[i], 0, 0))
    return pl.pallas_call(
        _expert_kernel,
        grid_spec=pltpu.PrefetchScalarGridSpec(
            num_scalar_prefetch=2, grid=(n_rows // TM_GROUP,),
            in_specs=[pl.BlockSpec(memory_space=pl.ANY),
                      wspec((D_MODEL, 2 * d_e)), wspec((d_e, D_MODEL))],
            out_specs=pl.BlockSpec(memory_space=pl.ANY),
            scratch_shapes=[pltpu.VMEM((2, TM_GROUP, D_MODEL), F32),
                            pltpu.VMEM((2, TM_GROUP, D_MODEL), F32),
                            pltpu.SemaphoreType.DMA((2,)), pltpu.SemaphoreType.DMA((2,))]),
        out_shape=jax.ShapeDtypeStruct(xs.shape, F32),
        compiler_params=_cparams(("arbitrary",)),
        name="moe_experts",
    )(tile_expert, n_active, xs, wgu_bf, wd_bf)


def _combine_kernel(*refs, final):
    pos_ref, y_ref, x_ref, rec_ref, mod_ref = refs[:5]
    g_ref = refs[5] if final else None
    o_ref = refs[5 + int(final)]
    bufs = refs[6 + int(final):-1]
    sems = refs[-1]
    sets = [(bufs[2 * k], bufs[2 * k + 1], sems.at[k]) for k in range(4)]
    i = pl.program_id(0)
    last = pl.num_programs(0) - 1

    def issue(half_idx, buf_set):
        d1, d2, sem = buf_set
        base = half_idx * DMA_CHUNK

        for r in range(DMA_CHUNK):
            _row_copy(y_ref.at[pos_ref[0, base + r]], d1.at[r], sem).start(priority=1)
            _row_copy(y_ref.at[pos_ref[1, base + r]], d2.at[r], sem).start(priority=1)

    def drain(buf_set):
        d1, d2, sem = buf_set
        rows = pl.ds(0, DMA_CHUNK)
        _row_copy(y_ref.at[rows], d1, sem).wait()
        _row_copy(y_ref.at[rows], d2, sem).wait()

    def finish(half, buf_set):
        d1, d2, _ = buf_set
        rows = slice(half * DMA_CHUNK, (half + 1) * DMA_CHUNK)
        w1 = rec_ref[rows, R_W1:R_W1 + 1]
        w2 = rec_ref[rows, R_W2:R_W2 + 1]
        for c in range(D_MODEL // LANES):
            sl = slice(c * LANES, (c + 1) * LANES)
            y = w1 * d1[:, c, :] + w2 * d2[:, c, :]
            o_ref[rows, sl] = x_ref[rows, sl] + mod_ref[5:6, sl] * y
        if final:
            o_ref[rows, :] = _rms(o_ref[rows, :], g_ref[...])

    @pl.when(i == 0)
    def _():
        issue(0, sets[0])
        issue(1, sets[1])

    def half_block(half, cur, nxt):
        drain(cur)

        @pl.when(i < last)
        def _():
            issue(2 * i + 2 + half, nxt)
            finish(half, cur)

        @pl.when(i == last)
        def _():
            finish(half, cur)

    @pl.when(i % 2 == 0)
    def _():
        half_block(0, sets[0], sets[2])
        half_block(1, sets[1], sets[3])

    @pl.when(i % 2 == 1)
    def _():
        half_block(0, sets[2], sets[0])
        half_block(1, sets[3], sets[1])


def _combine(pos2, ys, x1, rec, mods, l, seq_len, lat, final_g):
    n = x1.shape[0]
    final = final_g is not None
    tile = ys.shape[1:]
    tm = 2 * DMA_CHUNK
    tiles_per_seq = max(seq_len // tm, 1)
    if lat:
        mod_idx = lambda i, p: (l, 1 + i // tiles_per_seq, 0, 0)
    else:
        mod_idx = lambda i, p: (l, 0, 0, 0)
    row = lambda i, p: (i, 0)
    slot = pltpu.VMEM((DMA_CHUNK,) + tile, F32)
    in_specs = [pl.BlockSpec(memory_space=pl.ANY),
                pl.BlockSpec((tm, D_MODEL), row),
                pl.BlockSpec((tm, LANES), row),
                pl.BlockSpec((None, None, 6, D_MODEL), mod_idx)]
    args = [pos2, ys, x1, rec, mods]
    if final:
        in_specs.append(pl.BlockSpec((1, D_MODEL), lambda i, p: (0, 0)))
        args.append(final_g)
    return pl.pallas_call(
        functools.partial(_combine_kernel, final=final),
        grid_spec=pltpu.PrefetchScalarGridSpec(
            num_scalar_prefetch=1, grid=(n // tm,),
            in_specs=in_specs,
            out_specs=pl.BlockSpec((tm, D_MODEL), row),
            scratch_shapes=[slot] * 8 + [pltpu.SemaphoreType.DMA((4,))]),
        out_shape=jax.ShapeDtypeStruct((n, D_MODEL), F32),
        compiler_params=_cparams(("arbitrary",)),
        name="moe_combine_final" if final else "moe_combine",
    )(*args)


def _moe(x1, h2_rows, rec, counts, mods, l, j, wgu_bf, wd_bf, seq_len, lat, final_g):
    n = x1.shape[0]
    n_tiles = 2 * n // TM_GROUP + N_EXPERTS
    pos2, tile_expert, n_active, zero_tile = _group_layout(rec, counts, n_tiles)
    xs = _dispatch(pos2, zero_tile, h2_rows, n_tiles * TM_GROUP)
    ys = _experts(tile_expert, n_active, xs, j, wgu_bf, wd_bf)
    return _combine(pos2, ys, x1, rec, mods, l, seq_len, lat, final_g)


def _final_norm_kernel(x_ref, g_ref, o_ref):
    o_ref[...] = _rms(x_ref[...], g_ref[...])


def _final_norm(x, g):
    n = x.shape[0]
    row = lambda i: (i, 0)
    return pl.pallas_call(
        _final_norm_kernel,
        grid=(n // TM_FF,),
        in_specs=[pl.BlockSpec((TM_FF, D_MODEL), row), pl.BlockSpec((1, D_MODEL), lambda i: (0, 0))],
        out_specs=pl.BlockSpec((TM_FF, D_MODEL), row),
        out_shape=jax.ShapeDtypeStruct((n, D_MODEL), F32),
        compiler_params=_cparams(("arbitrary",)),
        name="final_norm",
    )(x, g)


def _rope_tables(t_len):
    t = jnp.arange(t_len)
    pos = jnp.stack([t // GRID_W, t % GRID_W], axis=-1).astype(F32)
    inv = ROPE_THETA ** (-jnp.arange(0, AXIS_DIM, 2, dtype=F32) / AXIS_DIM)
    ang = pos[..., None] * inv
    cos, sin = jnp.cos(ang), jnp.sin(ang)
    c = jnp.concatenate([cos[:, 0], cos[:, 0], cos[:, 1], cos[:, 1]], axis=-1)
    s = jnp.concatenate([-sin[:, 0], sin[:, 0], -sin[:, 1], sin[:, 1]], axis=-1)
    return c, s


def _gate_weights(wa, wi):
    depth = wa.shape[0]
    groups = D_RNN // LANES

    def pair(w):
        w = w.reshape(depth, 2, groups, 2, RNN_BLOCK, RNN_BLOCK)
        z = jnp.zeros_like(w[:, :, :, 0])
        top = jnp.concatenate([w[:, :, :, 0], z], axis=-1)
        bot = jnp.concatenate([z, w[:, :, :, 1]], axis=-1)
        return jnp.concatenate([top, bot], axis=-2)

    w = jnp.concatenate([pair(wa), pair(wi)], axis=-1)
    return (0.5 * jnp.concatenate([w[:, 0], w[:, 1]], axis=-1)).astype(BF16)


def kernel(x_prompt, x_sample, cache_k, cache_v, state_rnn, c, c_ctx, w_mod, b_mod, norm1, norm2, w_in, conv_w, conv_b, lru_wa, lru_ba, lru_wi, lru_bi, lru_lambda, q_norm, k_norm, w_rnn_out, w_attn_out, w_out, ff_gate, ff_up, ff_down, router_w, router_b, exp_gate, exp_up, exp_down, final_norm):
    batch, seq, _ = x_prompt.shape
    dec_batch, dec_seq, _ = x_sample.shape
    depth = w_mod.shape[0]
    past = cache_k.shape[2]
    assert dec_batch + 1 <= MOD_ROWS

    cond8 = jnp.zeros((MOD_ROWS, D_MODEL), F32).at[0].set(c_ctx).at[1:1 + dec_batch].set(c)

    gate_w = _gate_weights(lru_wa, lru_wi)
    n_moe = router_w.shape[0]
    wr_pad = jnp.zeros((n_moe, D_MODEL, LANES), F32).at[:, :, :N_EXPERTS].set(router_w)
    wr_head = wr_pad.astype(BF16)
    wr_t = jnp.concatenate([wr_head, (wr_pad - wr_head.astype(F32)).astype(BF16)], axis=-1)
    br_pad = jnp.zeros((n_moe, 1, LANES), F32).at[:, 0, :N_EXPERTS].set(router_b)

    norm1_3 = norm1.reshape(depth, 1, D_MODEL)
    norm2_3 = norm2.reshape(depth, 1, D_MODEL)
    qn3 = q_norm.reshape(depth, 1, HEAD_DIM)
    kn3 = k_norm.reshape(depth, 1, HEAD_DIM)
    conv_b3 = conv_b.reshape(depth, 1, D_RNN)
    ba4 = (0.5 * lru_ba).reshape(depth, 2, 1, D_RNN)
    bi4 = (0.5 * lru_bi).reshape(depth, 2, 1, D_RNN)
    lam4 = lru_lambda.reshape(depth, 2, 1, D_RNN)
    state5 = state_rnn.reshape(dec_batch, depth, 2, 1, D_RNN)
    cache_k4 = cache_k.reshape(dec_batch, depth, past, KV_DIM)
    cache_v4 = cache_v.reshape(dec_batch, depth, past, KV_DIM)
    rope_tabs = _rope_tables(dec_seq)
    final_g = final_norm.reshape(1, D_MODEL)

    n_steps = dec_batch * dec_seq // TQ
    assert n_steps == dec_batch * dec_seq // TM_ROPE

    def flat(w):
        return w.reshape(-1, w.shape[-1])

    def channel_names(l):
        return ("exp_gate_up", "exp_down") if l % 2 == 1 else ("ff_gate", "ff_up", "ff_down")

    stacked = dict(w_in=(w_in,), w_rnn_out=(w_rnn_out,), w_attn_out=(w_attn_out,), w_out=(w_out,),
                   ff_gate=(ff_gate,), ff_up=(ff_up,), ff_down=(ff_down,),
                   exp_gate_up=(exp_gate, exp_up), exp_down=(exp_down,))

    def layer_shape(name):
        parts = stacked[name]
        return (1,) + parts[0].shape[1:-1] + (sum(w.shape[-1] for w in parts),)

    def layer_index(name, l):
        return l if name.startswith("w_") else l // 2

    def rider(name, l, steps):
        parts = stacked[name]
        rows = flat(parts[0]).shape[0] // parts[0].shape[0] // steps
        return _Rider([flat(w) for w in parts], rows, layer_index(name, l) * steps)

    def names_of(l):
        return ("w_in", "w_rnn_out", "w_attn_out", "w_out") + channel_names(l)

    weights = {l: {} for l in range(depth)}

    def keep(l, names, arrays):
        for name, arr in zip(names, arrays):
            weights[l][name] = arr.reshape(layer_shape(name))

    mod_steps = _modulation_steps(depth)
    mods, w_in0 = _modulation(cond8, w_mod, b_mod, riders=[rider("w_in", 0, mod_steps)])
    mods = mods.reshape(depth, MOD_ROWS, 6, D_MODEL)
    keep(0, ("w_in",), [w_in0])
    ctx_steps = batch * seq // TM

    def run_layer(x, l, lat):
        seq_len = dec_seq if lat else seq
        wts = weights[l]
        convert_next = lat and l + 1 < depth
        if convert_next:
            proj_names = names_of(l + 1)[:4]
            attn_names = names_of(l + 1)[4:]
            proj_riders = [rider(name, l + 1, n_steps) for name in proj_names]
            attn_riders = [rider(name, l + 1, n_steps) for name in attn_names]
            target = l + 1
        elif l == 0 and not lat:
            proj_names, attn_names, attn_riders, target = names_of(0)[1:], (), [], 0
            proj_riders = [rider(name, 0, ctx_steps) for name in proj_names]
        else:
            proj_names, attn_names, proj_riders, attn_riders, target = (), (), [], [], None
        outs = _in_proj(x, mods, l, norm1_3, wts["w_in"], 0, qn3, kn3,
                        rope_tabs if lat else None, seq_len, riders=proj_riders,
                        caches=None if lat else new_kv)
        xr, yr, q, _, kr, v, gr, ga = outs[:8]
        if target is not None:
            keep(target, proj_names, outs[8:8 + len(proj_names)])
        if not lat:
            new_kv[:] = outs[8 + len(proj_names):]
        out_r, end_f, end_b = _rnn(xr, yr, l, conv_w, conv_b3, gate_w, ba4, bi4, lam4,
                                   state5 if lat else None, seq_len)
        if lat:
            outs = _attn_lat(q, kr, v, cache_k4, cache_v4, l, seq_len, riders=attn_riders)
            out_a = outs[0]
            if target is not None:
                keep(target, attn_names, outs[1:])
        else:
            out_a = _attn_ctx(q, kr, v, seq_len)
        mix_args = (x, out_r, out_a, gr, ga, mods, l, norm2_3, wts["w_rnn_out"],
                    wts["w_attn_out"], wts["w_out"], 0, seq_len, lat)
        if l % 2 == 1:
            x1, h2_rows, rec, counts = _mix_out(*mix_args, router=(l // 2, wr_t, br_pad))
            x = _moe(x1, h2_rows, rec, counts, mods, l, 0, wts["exp_gate_up"], wts["exp_down"],
                     seq_len, lat, final_g if l == depth - 1 else None)
        else:
            x1, h2 = _mix_out(*mix_args)
            x = _ffn(x1, h2, mods, l, 0, wts["ff_gate"], wts["ff_up"], wts["ff_down"],
                     seq_len, lat)
            if l == depth - 1:
                x = _final_norm(x, final_g)
        return x, jnp.stack([end_f, end_b], axis=1)

    new_kv = [jnp.zeros((batch, depth, seq, N_KV_HEADS, HEAD_DIM), F32) for _ in range(2)]
    y_ctx = x_prompt.reshape(batch * seq, D_MODEL)
    y_lat = x_sample.reshape(dec_batch * dec_seq, D_MODEL)
    ss = []
    for l in range(depth):
        y_ctx, st = run_layer(y_ctx, l, False)
        ss.append(st)
        y_lat, _ = run_layer(y_lat, l, True)

    new_state = jnp.stack(ss, axis=1)
    return (y_ctx.reshape(batch, seq, D_MODEL), y_lat.reshape(dec_batch, dec_seq, D_MODEL),
            new_kv[0], new_kv[1], new_state)
```

```python
import functools

import jax
import jax.numpy as jnp
from jax import lax
from jax.experimental import pallas as pl
from jax.experimental.pallas import tpu as pltpu

F32 = jnp.float32
BF16 = jnp.bfloat16

D_MODEL = 1024
D_RNN = 1024
N_RNN_BLOCKS = 16
RNN_BLOCK = D_RNN // N_RNN_BLOCKS
CONV_W = 4
CONV_LEFT = 2
LRU_C = 8.0
N_HEADS = 8
N_KV_HEADS = 2
HEAD_DIM = 128
AXIS_DIM = HEAD_DIM // 2
ROPE_THETA = 10000.0
GRID_W = 64
Q_DIM = N_HEADS * HEAD_DIM
KV_DIM = N_KV_HEADS * HEAD_DIM
N_EXPERTS = 8
EPS = 1e-6
LOG2_E = 1.4426950408889634
RSQRT_FLOOR = 1e-30
D_IN = 2 * D_RNN + Q_DIM + 2 * KV_DIM + 2 * D_MODEL
O_XR, O_YR, O_Q = 0, D_RNN, 2 * D_RNN
O_K = O_Q + Q_DIM
O_V = O_K + KV_DIM
O_GR = O_V + KV_DIM
O_GA = O_GR + D_MODEL

LANES = 128
SUBLANES = 8
MOD_ROWS = 8
VMEM_LIMIT = 56 * 1024 * 1024

TM = 512
TM_ROPE = 256
RESIDENT = pl.Buffered(1)
RNN_T = 256
RNN_SEG = SUBLANES
RNN_C = 512
RNN_TCH = 32
TQ = 256
ATTN_CTX_SEQS = 2
TM_FF = 512
TM_GROUP = 256
DMA_CHUNK = 256


def _cparams(sem):
    return pltpu.CompilerParams(dimension_semantics=sem, vmem_limit_bytes=VMEM_LIMIT)


def _sigmoid(x):
    return 0.5 * jnp.tanh(0.5 * x) + 0.5


def _silu(x):
    return x * _sigmoid(x)


def _gelu_tanh(x):
    return 0.5 * x * (1.0 + jnp.tanh(0.7978845608028654 * (x + 0.044715 * (x * x * x))))


def _rms(x, g):
    ms = jnp.mean(x * x, axis=-1, keepdims=True)
    return x * lax.rsqrt(ms + EPS) * g


def _dot(a, b):
    return jnp.dot(a, b, preferred_element_type=F32)


MOD_TN = 1536


def _mod_kernel(*refs, groups):
    n_rin = sum(groups)
    cond_ref, w_ref, b_ref = refs[:3]
    o_ref = refs[3 + n_rin]
    _convert_riders(refs[3:3 + n_rin], refs[4 + n_rin:], groups)
    s = _silu(cond_ref[...]).astype(BF16)
    o_ref[...] = _dot(s, w_ref[...].astype(BF16)) + b_ref[...]


def _modulation_steps(depth):
    return depth * (6 * D_MODEL // MOD_TN)


def _modulation(cond8, w_mod, b_mod, riders=()):
    depth = w_mod.shape[0]
    per_layer = 6 * D_MODEL // MOD_TN
    r_in, r_out, r_shape = _rider_specs(riders, depth * per_layer, lambda l, j: l * per_layer + j)
    return pl.pallas_call(
        functools.partial(_mod_kernel, groups=_rider_groups(riders)),
        grid=(depth, per_layer),
        in_specs=[
            pl.BlockSpec((MOD_ROWS, D_MODEL), lambda l, j: (0, 0)),
            pl.BlockSpec((None, D_MODEL, MOD_TN), lambda l, j: (l, 0, j)),
            pl.BlockSpec((None, 1, MOD_TN), lambda l, j: (l, 0, j)),
        ] + r_in,
        out_specs=[pl.BlockSpec((None, MOD_ROWS, MOD_TN), lambda l, j: (l, 0, j))] + r_out,
        out_shape=[jax.ShapeDtypeStruct((depth, MOD_ROWS, 6 * D_MODEL), F32)] + r_shape,
        compiler_params=_cparams(("arbitrary", "arbitrary")),
        name="adaln_mod",
    )(cond8, w_mod, b_mod.reshape(depth, 1, 6 * D_MODEL), *_rider_args(riders))


def _swap_halves(x):
    lane = lax.broadcasted_iota(jnp.int32, x.shape, 1)
    return jnp.where((lane % AXIS_DIM) < AXIS_DIM // 2,
                     pltpu.roll(x, LANES - AXIS_DIM // 2, axis=1),
                     pltpu.roll(x, AXIS_DIM // 2, axis=1))


class _Rider:
    def __init__(self, w2d, rows, first):
        self.parts = list(w2d) if isinstance(w2d, (list, tuple)) else [w2d]
        self.rows, self.first = rows, first


def _rider_specs(riders, n_steps, step_of):
    in_specs, out_specs, out_shape = [], [], []
    for r in riders:
        for w in r.parts:
            assert r.rows % 16 == 0 and (r.first + n_steps) * r.rows <= w.shape[0]
            in_specs.append(pl.BlockSpec((r.rows, w.shape[1]),
                                         lambda *g, r=r: (r.first + step_of(*g), 0)))
        cols = sum(w.shape[1] for w in r.parts)
        out_specs.append(pl.BlockSpec((r.rows, cols), lambda *g: (step_of(*g), 0)))
        out_shape.append(jax.ShapeDtypeStruct((r.rows * n_steps, cols), BF16))
    return in_specs, out_specs, out_shape


def _rider_groups(riders):
    return tuple(len(r.parts) for r in riders)


def _rider_args(riders):
    return [w for r in riders for w in r.parts]


def _convert_riders(in_refs, out_refs, groups):
    k = 0
    for o_ref, n_parts in zip(out_refs, groups):
        col = 0
        for w_ref in in_refs[k:k + n_parts]:
            o_ref[:, col:col + w_ref.shape[1]] = w_ref[...].astype(BF16)
            col += w_ref.shape[1]
        k += n_parts


def _in_proj_kernel(*refs, rope, groups, cache_layer, seq_len):
    n_in = 8 if rope else 6
    n_rin, n_rout = sum(groups), len(groups)
    n_cache = 0 if cache_layer is None else 2
    first_out = n_in + n_rin + n_cache
    _convert_riders(refs[n_in:n_in + n_rin], refs[first_out + 8:first_out + 8 + n_rout], groups)
    cache_refs = refs[first_out + 8 + n_rout:]
    refs = refs[:n_in] + refs[first_out:first_out + 8]
    if rope:
        (x_ref, mod_ref, n1_ref, w_ref, qn_ref, kn_ref, c_ref, s_ref,
         xr_ref, yr_ref, q_ref, kno_ref, kro_ref, v_ref, gr_ref, ga_ref) = refs
    else:
        (x_ref, mod_ref, n1_ref, w_ref, qn_ref, kn_ref,
         xr_ref, yr_ref, q_ref, kno_ref, kro_ref, v_ref, gr_ref, ga_ref) = refs
    x = x_ref[...]
    h = _rms(x, n1_ref[...]) * (1.0 + mod_ref[1:2, :]) + mod_ref[0:1, :]
    hb = h.astype(BF16)

    xr_ref[...] = _dot(hb, w_ref[:, O_XR:O_XR + D_RNN])
    yr_ref[...] = _dot(hb, w_ref[:, O_YR:O_YR + D_RNN])
    gr_ref[...] = _dot(hb, w_ref[:, O_GR:O_GR + D_MODEL]).astype(BF16)
    ga_ref[...] = _dot(hb, w_ref[:, O_GA:O_GA + D_MODEL]).astype(BF16)
    v_ref[...] = _dot(hb, w_ref[:, O_V:O_V + KV_DIM])

    if rope:
        cs, sn = c_ref[...], s_ref[...]
    zq = _dot(hb, w_ref[:, O_Q:O_Q + Q_DIM])
    for hd in range(N_HEADS):
        t = _rms(zq[:, hd * HEAD_DIM:(hd + 1) * HEAD_DIM], qn_ref[...])
        if rope:
            t = t * cs + _swap_halves(t) * sn
        q_ref[:, hd * HEAD_DIM:(hd + 1) * HEAD_DIM] = t.astype(BF16)
    zk = _dot(hb, w_ref[:, O_K:O_K + KV_DIM])
    for hd in range(N_KV_HEADS):
        t = _rms(zk[:, hd * HEAD_DIM:(hd + 1) * HEAD_DIM], kn_ref[...])
        kno_ref[:, hd * HEAD_DIM:(hd + 1) * HEAD_DIM] = t
        if rope:
            t = t * cs + _swap_halves(t) * sn
        kro_ref[:, hd * HEAD_DIM:(hd + 1) * HEAD_DIM] = t.astype(BF16)

    if cache_layer is None:
        return
    newk_ref, newv_ref, kbuf, vbuf, csem = cache_refs
    step = pl.program_id(0)
    slot = step % 2
    seqs_per_tile = x_ref.shape[0] // seq_len

    def cache_copies(k, s):
        cps = []
        for sq in range(seqs_per_tile):
            rows = slice(sq * seq_len, (sq + 1) * seq_len)
            for hd in range(N_KV_HEADS):
                cols = slice(hd * HEAD_DIM, (hd + 1) * HEAD_DIM)
                dst = (k * seqs_per_tile + sq, cache_layer, slice(None), hd, slice(None))
                cps.append(pltpu.make_async_copy(kbuf.at[s, rows, cols], newk_ref.at[dst], csem.at[s]))
                cps.append(pltpu.make_async_copy(vbuf.at[s, rows, cols], newv_ref.at[dst], csem.at[s]))
        return cps

    @pl.when(step >= 2)
    def _():
        for cp in cache_copies(step - 2, slot):
            cp.wait()

    kbuf[slot] = kno_ref[...]
    vbuf[slot] = v_ref[...]
    for cp in cache_copies(step, slot):
        cp.start()

    @pl.when(step == pl.num_programs(0) - 1)
    def _():
        for cp in cache_copies(step, slot):
            cp.wait()

        @pl.when(step >= 1)
        def _():
            for cp in cache_copies(step - 1, 1 - slot):
                cp.wait()


def _in_proj(x, mods, l, norm1, w_in_bf, wl, q_norm, k_norm, rope_tabs, seq_len, riders=(),
             caches=None):
    n = x.shape[0]
    rope = rope_tabs is not None
    tm = TM_ROPE if rope else TM
    tiles_per_seq = seq_len // tm
    if rope:
        mod_idx = lambda i: (l, 1 + i // tiles_per_seq, 0, 0)
    else:
        mod_idx = lambda i: (l, 0, 0, 0)
    row = lambda i: (i, 0)
    in_specs = [
        pl.BlockSpec((tm, D_MODEL), row),
        pl.BlockSpec((None, None, 6, D_MODEL), mod_idx),
        pl.BlockSpec((None, 1, D_MODEL), lambda i: (l, 0, 0)),
        pl.BlockSpec((None, D_MODEL, D_IN), lambda i: (wl, 0, 0), pipeline_mode=RESIDENT),
        pl.BlockSpec((None, 1, HEAD_DIM), lambda i: (l, 0, 0)),
        pl.BlockSpec((None, 1, HEAD_DIM), lambda i: (l, 0, 0)),
    ]
    args = [x, mods, norm1, w_in_bf, q_norm, k_norm]
    if rope:
        in_specs += [pl.BlockSpec((tm, HEAD_DIM), lambda i: (i % tiles_per_seq, 0))] * 2
        args += list(rope_tabs)
    widths = [(D_RNN, F32), (D_RNN, F32), (Q_DIM, BF16), (KV_DIM, F32), (KV_DIM, BF16),
              (KV_DIM, F32), (D_MODEL, BF16), (D_MODEL, BF16)]
    r_in, r_out, r_shape = _rider_specs(riders, n // tm, lambda i: i)
    in_specs = in_specs + r_in
    out_specs = [pl.BlockSpec((tm, w), row) for w, _ in widths] + r_out
    out_shape = [jax.ShapeDtypeStruct((n, w), dt) for w, dt in widths] + r_shape
    args = args + _rider_args(riders)
    scratch, aliases = [], {}
    if caches is not None:
        assert tm % seq_len == 0
        for c in caches:
            aliases[len(args)] = len(out_shape)
            args.append(c)
            in_specs.append(pl.BlockSpec(memory_space=pl.ANY))
            out_specs.append(pl.BlockSpec(memory_space=pl.ANY))
            out_shape.append(jax.ShapeDtypeStruct(c.shape, c.dtype))
        scratch = [pltpu.VMEM((2, tm, KV_DIM), F32), pltpu.VMEM((2, tm, KV_DIM), F32),
                   pltpu.SemaphoreType.DMA((2,))]
    return pl.pallas_call(
        functools.partial(_in_proj_kernel, rope=rope, groups=_rider_groups(riders),
                          cache_layer=None if caches is None else l, seq_len=seq_len),
        grid=(n // tm,),
        in_specs=in_specs,
        out_specs=out_specs,
        out_shape=out_shape,
        scratch_shapes=scratch,
        input_output_aliases=aliases,
        compiler_params=_cparams(("arbitrary",)),
        name="in_proj_lat" if rope else "in_proj_ctx",
    )(*args)


def _rnn_kernel(*refs, chained):
    refs = list(refs)
    xr_ref, yr_ref, cw_ref, cb_ref, gw_ref, ba_ref, bi_ref, lam_ref = refs[:8]
    refs = refs[8:]
    h0_ref = refs.pop(0) if chained else None
    out_ref, sf_ref, sb_ref, xext, ybuf, obuf, a_f, b_f, a_b, b_b, sem_in, sem_out = refs

    g = pl.program_id(0)
    cb = pl.program_id(1)
    n_cb = pl.num_programs(1)
    step = g * n_cb + cb
    n_steps = pl.num_programs(0) * n_cb
    slot = step % 2

    def hbm_window(kk, s):
        rows = pl.ds(pl.multiple_of(((kk // n_cb) * RNN_SEG + s) * RNN_T, RNN_T), RNN_T)
        cols = pl.ds(pl.multiple_of((kk % n_cb) * RNN_C, RNN_C), RNN_C)
        return rows, cols

    def in_copies(kk, sl_):
        cps = []
        for s in range(RNN_SEG):
            win = hbm_window(kk, s)
            cps.append(pltpu.make_async_copy(
                xr_ref.at[win], xext.at[sl_, pl.ds(CONV_LEFT, RNN_T), s], sem_in.at[sl_]))
            cps.append(pltpu.make_async_copy(yr_ref.at[win], ybuf.at[sl_, :, s], sem_in.at[sl_]))
        return cps

    def out_copies(kk, sl_):
        return [pltpu.make_async_copy(obuf.at[sl_, :, s], out_ref.at[hbm_window(kk, s)],
                                      sem_out.at[sl_]) for s in range(RNN_SEG)]

    @pl.when(step == 0)
    def _():
        for cp in in_copies(0, 0):
            cp.start()

    @pl.when(step + 1 < n_steps)
    def _():
        for cp in in_copies(step + 1, 1 - slot):
            cp.start()

    for cp in in_copies(step, slot):
        cp.wait()

    seg = lax.broadcasted_iota(jnp.int32, (RNN_SEG, RNN_C), 0)
    zero_row = jnp.zeros((RNN_SEG, RNN_C), F32)
    if chained:
        xext[slot, 0] = jnp.where(seg == 0, zero_row, pltpu.roll(xext[slot, RNN_T], 1, axis=0))
        xext[slot, 1] = jnp.where(seg == 0, zero_row, pltpu.roll(xext[slot, RNN_T + 1], 1, axis=0))
        xext[slot, RNN_T + CONV_LEFT] = jnp.where(
            seg == RNN_SEG - 1, zero_row, pltpu.roll(xext[slot, CONV_LEFT], RNN_SEG - 1, axis=0))
    else:
        xext[slot, 0] = zero_row
        xext[slot, 1] = zero_row
        xext[slot, RNN_T + CONV_LEFT] = zero_row

    scans = ((a_f, b_f), (a_b, b_b))
    rows2d = RNN_TCH * RNN_SEG

    def gate_chunk(tc, carry):
        t0 = pl.multiple_of(tc * RNN_TCH, RNN_TCH)
        for j in range(RNN_C // LANES):
            sl = slice(j * LANES, (j + 1) * LANES)
            xc = jnp.broadcast_to(cb_ref[:, sl].reshape(1, 1, LANES), (RNN_TCH, RNN_SEG, LANES))
            for k in range(CONV_W):
                xc = xc + (xext[slot, pl.ds(t0 + k, RNN_TCH), :, sl]
                           * cw_ref[k:k + 1, sl].reshape(1, 1, LANES))
            xc = xc.reshape(rows2d, LANES)
            pre = _dot(xc.astype(BF16), gw_ref[j])
            half_xc = 0.5 * xc
            for d, (a_s, b_s) in enumerate(scans):
                lam = lam_ref[d, :, sl]
                log2_decay = (-LRU_C * LOG2_E) * (jnp.maximum(-lam, 0.0)
                                                  + jnp.log(1.0 + jnp.exp(-jnp.abs(lam))))
                t_r = jnp.tanh(pre[:, 2 * d * LANES:(2 * d + 1) * LANES] + ba_ref[d, :, sl])
                t_i = jnp.tanh(pre[:, (2 * d + 1) * LANES:(2 * d + 2) * LANES]
                               + bi_ref[d, :, sl])
                half_decay = 0.5 * log2_decay
                a = jnp.exp2(t_r * half_decay + half_decay)
                y = 1.0 - a * a
                b = (y * lax.rsqrt(jnp.maximum(y, RSQRT_FLOOR))) * ((t_i + 1.0) * half_xc)
                a_s[pl.ds(t0, RNN_TCH), :, sl] = a.reshape(RNN_TCH, RNN_SEG, LANES)
                b_s[pl.ds(t0, RNN_TCH), :, sl] = b.reshape(RNN_TCH, RNN_SEG, LANES)
        return carry

    n_chunks = RNN_T // RNN_TCH
    ones = jnp.ones((RNN_SEG, RNN_C), F32)

    def fwd_scan_chunk(tc, hf, pf):
        for u in range(RNN_TCH):
            t = tc * RNN_TCH + u
            af_t = a_f[t]
            hf = af_t * hf + b_f[t]
            b_f[t] = hf
            if chained:
                pf = af_t * pf
                a_f[t] = pf
        return hf, pf

    def gate_and_scan(tc, carry):
        carry = fwd_scan_chunk(tc - 1, *carry)
        gate_chunk(tc, 0)
        return carry

    gate_chunk(0, 0)
    hf, pf = lax.fori_loop(1, n_chunks, gate_and_scan, (zero_row, ones))
    hf, pf = fwd_scan_chunk(n_chunks - 1, hf, pf)

    @pl.when(step >= 2)
    def _():
        for cp in out_copies(step - 2, slot):
            cp.wait()

    if not chained:
        def bwd_finish_chunk(i, hb):
            tc = n_chunks - 1 - i
            for u in reversed(range(RNN_TCH)):
                t = tc * RNN_TCH + u
                hb = a_b[t] * hb + b_b[t]
                obuf[slot, t] = (b_f[t] + hb) * _gelu_tanh(ybuf[slot, t])
            return hb

        hb = lax.fori_loop(0, n_chunks, bwd_finish_chunk, zero_row)
        sf_ref[...] = hf
        sb_ref[...] = hb
    else:
        st_f = jnp.broadcast_to(h0_ref[0], (RNN_SEG, RNN_C))
        st_b = jnp.broadcast_to(h0_ref[1], (RNN_SEG, RNN_C))
        in_f = st_f
        for _ in range(RNN_SEG - 1):
            in_f = jnp.where(seg == 0, st_f, pltpu.roll(pf * in_f + hf, 1, axis=0))
        sf_ref[...] = pf * in_f + hf

        def bwd_scan_chunk(i, carry):
            hb, pb = carry
            tc = n_chunks - 1 - i
            for u in reversed(range(RNN_TCH)):
                t = tc * RNN_TCH + u
                ab_t = a_b[t]
                hb = ab_t * hb + b_b[t]
                b_b[t] = hb
                pb = ab_t * pb
                a_b[t] = pb
                b_f[t] = a_f[t] * in_f + b_f[t]
            return hb, pb

        hb, pb = lax.fori_loop(0, n_chunks, bwd_scan_chunk, (zero_row, ones))
        in_b = st_b
        for _ in range(RNN_SEG - 1):
            in_b = jnp.where(seg == RNN_SEG - 1, st_b,
                             pltpu.roll(pb * in_b + hb, RNN_SEG - 1, axis=0))
        sb_ref[...] = pb * in_b + hb

        def finish_chunk(tc, carry):
            rows = pl.ds(pl.multiple_of(tc * RNN_TCH, RNN_TCH), RNN_TCH)
            h_b = a_b[rows] * in_b.reshape(1, RNN_SEG, RNN_C) + b_b[rows]
            obuf[slot, rows] = (b_f[rows] + h_b) * _gelu_tanh(ybuf[slot, rows])
            return carry

        lax.fori_loop(0, n_chunks, finish_chunk, 0)

    for cp in out_copies(step, slot):
        cp.start()

    @pl.when(step == n_steps - 1)
    def _():
        for cp in out_copies(step, slot):
            cp.wait()

        @pl.when(step >= 1)
        def _():
            for cp in out_copies(step - 1, 1 - slot):
                cp.wait()


def _rnn(xr, yr, l, conv_w, conv_b, gate_w, lru_ba, lru_bi, lru_lambda, state5, seq_len):
    n = xr.shape[0]
    chained = state5 is not None
    group_len = RNN_SEG * RNN_T
    assert n % group_len == 0 and seq_len in ((group_len,) if chained else (RNN_T,))
    n_groups = n // group_len
    lanes_per_step = RNN_C // LANES
    vec = pl.BlockSpec((None, 2, 1, RNN_C), lambda g, c: (l, 0, 0, c))
    in_specs = [
        pl.BlockSpec(memory_space=pl.ANY),
        pl.BlockSpec(memory_space=pl.ANY),
        pl.BlockSpec((None, CONV_W, RNN_C), lambda g, c: (l, 0, c)),
        pl.BlockSpec((None, 1, RNN_C), lambda g, c: (l, 0, c)),
        pl.BlockSpec((None, lanes_per_step, LANES, 4 * LANES), lambda g, c: (l, c, 0, 0)),
        vec, vec, vec,
    ]
    args = [xr, yr, conv_w, conv_b, gate_w, lru_ba, lru_bi, lru_lambda]
    if chained:
        in_specs.append(pl.BlockSpec((None, None, 2, 1, RNN_C), lambda g, c: (g, l, 0, 0, c)))
        args.append(state5)
    state_spec = pl.BlockSpec((RNN_SEG, RNN_C), lambda g, c: (g, c))
    state_shape = jax.ShapeDtypeStruct((n_groups * RNN_SEG, D_RNN), F32)
    work = lambda t: pltpu.VMEM((t, RNN_SEG, RNN_C), F32)
    return pl.pallas_call(
        functools.partial(_rnn_kernel, chained=chained),
        grid=(n_groups, D_RNN // RNN_C),
        in_specs=in_specs,
        out_specs=[pl.BlockSpec(memory_space=pl.ANY), state_spec, state_spec],
        out_shape=[jax.ShapeDtypeStruct((n, D_RNN), F32), state_shape, state_shape],
        scratch_shapes=[
            pltpu.VMEM((2, RNN_T + CONV_W - 1, RNN_SEG, RNN_C), F32),
            pltpu.VMEM((2, RNN_T, RNN_SEG, RNN_C), F32),
            pltpu.VMEM((2, RNN_T, RNN_SEG, RNN_C), F32),
            work(RNN_T), work(RNN_T), work(RNN_T), work(RNN_T),
            pltpu.SemaphoreType.DMA((2,)), pltpu.SemaphoreType.DMA((2,)),
        ],
        compiler_params=_cparams(("arbitrary", "arbitrary")),
        name="rnn_lat" if chained else "rnn_ctx",
    )(*args)


def _attend_heads(q_ref, k_all, v_all, o_ref, stack_heads):
    exp2_scale = (1.0 / (HEAD_DIM ** 0.5)) * LOG2_E
    group = N_HEADS // N_KV_HEADS
    tq = q_ref.shape[0]
    for kh in range(N_KV_HEADS):
        k = k_all[:, kh * HEAD_DIM:(kh + 1) * HEAD_DIM]
        v = v_all[:, kh * HEAD_DIM:(kh + 1) * HEAD_DIM]
        heads = [slice((kh * group + g) * HEAD_DIM, (kh * group + g + 1) * HEAD_DIM)
                 for g in range(group)]
        stacks = [heads] if stack_heads else [[sl] for sl in heads]
        for stack in stacks:
            q = jnp.concatenate([q_ref[:, sl] for sl in stack], axis=0)
            s = lax.dot_general(q, k, (((1,), (1,)), ((), ())), preferred_element_type=F32)
            m = jnp.max(s, axis=-1, keepdims=True)
            p = jnp.exp2((s - m) * exp2_scale)
            denom = jnp.sum(p, axis=-1, keepdims=True)
            o = (_dot(p.astype(BF16), v) / denom).astype(BF16)
            for g, sl in enumerate(stack):
                o_ref[:, sl] = o[g * tq:(g + 1) * tq]


def _attn_ctx_kernel(q_ref, k_ref, v_ref, o_ref, *, seq_len):
    for s in range(q_ref.shape[0] // seq_len):
        rows = pl.ds(s * seq_len, seq_len)
        _attend_heads(q_ref.at[rows], k_ref[rows, :], v_ref[rows, :].astype(BF16), o_ref.at[rows],
                      stack_heads=True)


def _attn_ctx(q, kr, v, seq_len):
    n = q.shape[0]
    row = lambda b: (b, 0)
    tile = ATTN_CTX_SEQS * seq_len
    return pl.pallas_call(
        functools.partial(_attn_ctx_kernel, seq_len=seq_len),
        grid=(n // tile,),
        in_specs=[pl.BlockSpec((tile, Q_DIM), row),
                  pl.BlockSpec((tile, KV_DIM), row),
                  pl.BlockSpec((tile, KV_DIM), row)],
        out_specs=pl.BlockSpec((tile, Q_DIM), row),
        out_shape=jax.ShapeDtypeStruct((n, Q_DIM), BF16),
        compiler_params=_cparams(("arbitrary",)),
        name="attn_ctx",
    )(q, kr, v)


def _attn_lat_kernel(*refs, past, groups):
    n_rin, n_rout = sum(groups), len(groups)
    _convert_riders(refs[5:5 + n_rin], refs[6 + n_rin:6 + n_rin + n_rout], groups)
    q_ref, kc_ref, vc_ref, kl_ref, vl_ref = refs[:5]
    o_ref = refs[5 + n_rin]
    k_all, v_all = refs[6 + n_rin + n_rout:]

    @pl.when(pl.program_id(1) == 0)
    def _():
        k_all[0:past, :] = kc_ref[...].astype(BF16)
        k_all[past:, :] = kl_ref[...]
        v_all[0:past, :] = vc_ref[...].astype(BF16)
        v_all[past:, :] = vl_ref[...].astype(BF16)

    _attend_heads(q_ref, k_all[...], v_all[...], o_ref, stack_heads=False)


def _attn_lat(q, kr, v, cache_k4, cache_v4, l, seq_len, riders=()):
    n = q.shape[0]
    past = cache_k4.shape[2]
    nq = seq_len // TQ
    step_of = lambda b, j: b * nq + j
    r_in, r_out, r_shape = _rider_specs(riders, (n // seq_len) * nq, step_of)
    return pl.pallas_call(
        functools.partial(_attn_lat_kernel, past=past, groups=_rider_groups(riders)),
        grid=(n // seq_len, nq),
        in_specs=[pl.BlockSpec((TQ, Q_DIM), lambda b, j: (b * nq + j, 0)),
                  pl.BlockSpec((None, None, past, KV_DIM), lambda b, j: (b, l, 0, 0)),
                  pl.BlockSpec((None, None, past, KV_DIM), lambda b, j: (b, l, 0, 0)),
                  pl.BlockSpec((seq_len, KV_DIM), lambda b, j: (b, 0)),
                  pl.BlockSpec((seq_len, KV_DIM), lambda b, j: (b, 0))] + r_in,
        out_specs=[pl.BlockSpec((TQ, Q_DIM), lambda b, j: (b * nq + j, 0))] + r_out,
        out_shape=[jax.ShapeDtypeStruct((n, Q_DIM), BF16)] + r_shape,
        scratch_shapes=[pltpu.VMEM((past + seq_len, KV_DIM), BF16),
                        pltpu.VMEM((past + seq_len, KV_DIM), BF16)],
        compiler_params=_cparams(("arbitrary", "arbitrary")),
        name="attn_lat",
    )(q, cache_k4, cache_v4, kr, v, *_rider_args(riders))


R_E1, R_E2, R_W1, R_W2, R_RANK1, R_RANK2 = range(6)


def _route(h2, wrt_ref, brt_ref, count_ref):
    shape = (h2.shape[0], LANES)
    lane = lax.broadcasted_iota(jnp.int32, shape, 1).astype(F32)
    neg = jnp.float32(-jnp.inf)
    h_hi = h2.astype(BF16)
    h_lo = (h2 - h_hi.astype(F32)).astype(BF16)
    z_hi = _dot(h_hi, wrt_ref[...])
    z_lo = _dot(h_lo, wrt_ref[...])
    logits = ((z_hi[:, :LANES] + z_hi[:, LANES:]) + (z_lo[:, :LANES] + z_lo[:, LANES:])
              + brt_ref[...])
    logits = jnp.where(lane < N_EXPERTS, logits, neg)
    no_lane = jnp.float32(LANES)
    v1 = jnp.max(logits, axis=-1, keepdims=True)
    i1 = jnp.min(jnp.where(logits == v1, lane, no_lane), axis=-1, keepdims=True)
    rest = jnp.where(lane == i1, neg, logits)
    v2 = jnp.max(rest, axis=-1, keepdims=True)
    i2 = jnp.min(jnp.where(rest == v2, lane, no_lane), axis=-1, keepdims=True)
    e2 = jnp.exp(v2 - v1)
    w1 = 1.0 / (1.0 + e2)
    w2 = e2 / (1.0 + e2)
    chosen = jnp.where((lane == i1) | (lane == i2), 1.0, 0.0)
    r_id = lax.broadcasted_iota(jnp.int32, (shape[0], shape[0]), 0)
    c_id = lax.broadcasted_iota(jnp.int32, (shape[0], shape[0]), 1)
    before = jnp.where(c_id < r_id, 1.0, 0.0).astype(BF16)
    rank = _dot(before, chosen.astype(BF16)) + count_ref[...]
    rank1 = jnp.sum(jnp.where(lane == i1, rank, 0.0), axis=-1, keepdims=True)
    rank2 = jnp.sum(jnp.where(lane == i2, rank, 0.0), axis=-1, keepdims=True)
    count_ref[...] += jnp.sum(chosen, axis=0, keepdims=True)
    rec = jnp.zeros(shape, F32)
    for k, val in ((R_E1, i1), (R_E2, i2), (R_W1, w1), (R_W2, w2),
                   (R_RANK1, rank1), (R_RANK2, rank2)):
        rec = jnp.where(lane == float(k), val, rec)
    return rec


def _mix_out_kernel(*refs, route):
    (x_ref, r_ref, a_ref, gr_ref, ga_ref, mod_ref, n2_ref, wr_ref, wa_ref, wo_ref) = refs[:10]
    if route:
        wrt_ref, brt_ref, x1_ref, h2_ref, rec_ref, cnt_ref, count_ref, hbuf, hsem = refs[10:]
    else:
        x1_ref, h2_ref = refs[10:]
    merged = (_sigmoid(gr_ref[...].astype(F32)) * _dot(r_ref[...].astype(BF16), wr_ref[...])
              + _sigmoid(ga_ref[...].astype(F32)) * _dot(a_ref[...], wa_ref[...]))
    mix = _dot(merged.astype(BF16), wo_ref[...])
    x1 = x_ref[...] + mod_ref[2:3, :] * mix
    x1_ref[...] = x1
    h2 = _rms(x1, n2_ref[...]) * (1.0 + mod_ref[4:5, :]) + mod_ref[3:4, :]
    if not route:
        h2_ref[...] = h2.astype(BF16)
        return

    @pl.when(pl.program_id(0) == 0)
    def _():
        count_ref[...] = jnp.zeros_like(count_ref)

    rec_ref[...] = _route(h2, wrt_ref, brt_ref, count_ref)
    cnt_ref[...] = count_ref[...]

    step = pl.program_id(0)
    slot = step % 2

    def h_copies(k, s):
        rows = pl.ds(k * TM, TM)
        return [pltpu.make_async_copy(hbuf.at[s, :, c * LANES:(c + 1) * LANES], h2_ref.at[rows, c],
                                      hsem.at[s]) for c in range(D_MODEL // LANES)]

    @pl.when(step >= 2)
    def _():
        for cp in h_copies(step - 2, slot):
            cp.wait()

    hbuf[slot] = h2
    for cp in h_copies(step, slot):
        cp.start()

    @pl.when(step == pl.num_programs(0) - 1)
    def _():
        for cp in h_copies(step, slot):
            cp.wait()

        @pl.when(step >= 1)
        def _():
            for cp in h_copies(step - 1, 1 - slot):
                cp.wait()


def _mix_out(x, out_r, out_a, gr, ga, mods, l, norm2, wr_bf, wa_bf, wo_bf, wl, seq_len, lat,
             router=None):
    n = x.shape[0]
    tiles_per_seq = seq_len // TM
    if lat:
        mod_idx = lambda i: (l, 1 + i // tiles_per_seq, 0, 0)
    else:
        mod_idx = lambda i: (l, 0, 0, 0)
    row = lambda i: (i, 0)
    wspec = pl.BlockSpec((None, D_MODEL, D_MODEL), lambda i: (wl, 0, 0), pipeline_mode=RESIDENT)
    in_specs = [pl.BlockSpec((TM, D_MODEL), row)] * 5 + [
        pl.BlockSpec((None, None, 6, D_MODEL), mod_idx),
        pl.BlockSpec((None, 1, D_MODEL), lambda i: (l, 0, 0)),
        wspec, wspec, wspec]
    args = [x, out_r, out_a, gr, ga, mods, norm2, wr_bf, wa_bf, wo_bf]
    out_specs = [pl.BlockSpec((TM, D_MODEL), row)]
    out_shape = [jax.ShapeDtypeStruct((n, D_MODEL), F32)]
    scratch = []
    if router is None:
        out_specs.append(pl.BlockSpec((TM, D_MODEL), row))
        out_shape.append(jax.ShapeDtypeStruct((n, D_MODEL), BF16))
    else:
        j, wr_t, br_pad = router
        in_specs += [pl.BlockSpec((None, D_MODEL, 2 * LANES), lambda i: (j, 0, 0)),
                     pl.BlockSpec((None, 1, LANES), lambda i: (j, 0, 0))]
        args += [wr_t, br_pad]
        out_specs += [pl.BlockSpec(memory_space=pl.ANY),
                      pl.BlockSpec((TM, LANES), row),
                      pl.BlockSpec((1, LANES), lambda i: (0, 0))]
        out_shape += [jax.ShapeDtypeStruct((n, D_MODEL // LANES, LANES), F32),
                      jax.ShapeDtypeStruct((n, LANES), F32),
                      jax.ShapeDtypeStruct((1, LANES), F32)]
        scratch = [pltpu.VMEM((1, LANES), F32), pltpu.VMEM((2, TM, D_MODEL), F32),
                   pltpu.SemaphoreType.DMA((2,))]
    return pl.pallas_call(
        functools.partial(_mix_out_kernel, route=router is not None),
        grid=(n // TM,),
        in_specs=in_specs,
        out_specs=out_specs,
        out_shape=out_shape,
        scratch_shapes=scratch,
        compiler_params=_cparams(("arbitrary",)),
        name=("mix_route" if router is not None else "mix_out") + ("_lat" if lat else "_ctx"),
    )(*args)


def _ffn_kernel(x_ref, h_ref, mod_ref, wg_ref, wu_ref, wd_ref, o_ref, *, f_chunk):
    h = h_ref[...]
    d_ff = wg_ref.shape[1]
    acc = jnp.zeros((h.shape[0], D_MODEL), F32)
    for c in range(d_ff // f_chunk):
        sl = slice(c * f_chunk, (c + 1) * f_chunk)
        act = _silu(_dot(h, wg_ref[:, sl])) * _dot(h, wu_ref[:, sl])
        acc = acc + _dot(act.astype(BF16), wd_ref[sl, :])
    o_ref[...] = x_ref[...] + mod_ref[5:6, :] * acc


def _ffn(x1, h2, mods, l, j, wg_bf, wu_bf, wd_bf, seq_len, lat):
    n = x1.shape[0]
    d_ff = wg_bf.shape[2]
    tiles_per_seq = seq_len // TM
    if lat:
        mod_idx = lambda i: (l, 1 + i // tiles_per_seq, 0, 0)
    else:
        mod_idx = lambda i: (l, 0, 0, 0)
    row = lambda i: (i, 0)
    return pl.pallas_call(
        functools.partial(_ffn_kernel, f_chunk=d_ff // 2),
        grid=(n // TM,),
        in_specs=[pl.BlockSpec((TM, D_MODEL), row), pl.BlockSpec((TM, D_MODEL), row),
                  pl.BlockSpec((None, None, 6, D_MODEL), mod_idx),
                  pl.BlockSpec((None, D_MODEL, d_ff), lambda i: (j, 0, 0), pipeline_mode=RESIDENT),
                  pl.BlockSpec((None, D_MODEL, d_ff), lambda i: (j, 0, 0), pipeline_mode=RESIDENT),
                  pl.BlockSpec((None, d_ff, D_MODEL), lambda i: (j, 0, 0), pipeline_mode=RESIDENT)],
        out_specs=pl.BlockSpec((TM, D_MODEL), row),
        out_shape=jax.ShapeDtypeStruct((n, D_MODEL), F32),
        compiler_params=_cparams(("arbitrary",)),
        name="ffn_lat" if lat else "ffn_ctx",
    )(x1, h2, mods, wg_bf, wu_bf, wd_bf)


def _group_layout(rec, counts, n_tiles):
    cnt = counts[0, :N_EXPERTS].astype(jnp.int32)
    tiles = (cnt + TM_GROUP - 1) // TM_GROUP
    cum = jnp.cumsum(tiles)
    start = (cum - tiles) * TM_GROUP
    experts = jnp.arange(N_EXPERTS, dtype=jnp.int32)

    assert (R_E2, R_RANK2) == (R_E1 + 1, R_RANK1 + 1)
    e12 = rec[:, R_E1:R_E2 + 1].astype(jnp.int32)
    rank12 = rec[:, R_RANK1:R_RANK2 + 1].astype(jnp.int32)
    group_start = jnp.sum(jnp.where(e12[:, :, None] == experts, start, 0), axis=2)
    pos2 = (group_start + rank12).T
    n_active = cum[N_EXPERTS - 1]
    tile_expert = jnp.sum(jnp.arange(n_tiles, dtype=jnp.int32)[:, None] >= cum[None, :], axis=1)
    tile_expert = jnp.minimum(tile_expert, tile_expert[n_active - 1]).astype(jnp.int32)
    tile_ids = jnp.arange(n_tiles, dtype=jnp.int32)
    group_last = jnp.any((tile_ids[:, None] == cum[None, :] - 1) & (tiles[None, :] > 0), axis=1)
    zero_tile = (group_last | (tile_ids >= n_active)).astype(jnp.int32)
    return (pos2.astype(jnp.int32), tile_expert, n_active.reshape(1).astype(jnp.int32), zero_tile)


def _row_copy(src, dst, sem):
    return pltpu.make_async_copy(src, dst, sem)


def _dispatch_kernel(pos_ref, zt_ref, h_ref, xs_ref, zbuf, sem, zsem):
    n_chunks = h_ref.shape[0] // DMA_CHUNK
    n_tiles = xs_ref.shape[0] // TM_GROUP

    zbuf[...] = jnp.zeros_like(zbuf)

    def zero_fill(j):
        return pltpu.make_async_copy(zbuf, xs_ref.at[pl.ds(j * TM_GROUP, TM_GROUP)], zsem)

    def zero_start(j, carry):
        @pl.when(zt_ref[j] != 0)
        def _():
            zero_fill(j).start()
        return carry

    def zero_wait(j, carry):
        @pl.when(zt_ref[j] != 0)
        def _():
            zero_fill(j).wait()
        return carry

    lax.fori_loop(0, n_tiles, zero_start, 0)
    lax.fori_loop(0, n_tiles, zero_wait, 0)

    def drain_chunk():
        rows = pl.ds(0, 2 * DMA_CHUNK)
        _row_copy(xs_ref.at[rows], xs_ref.at[rows], sem).wait()

    def chunk(k, carry):
        for r in range(DMA_CHUNK):
            t = k * DMA_CHUNK + r
            _row_copy(h_ref.at[t], xs_ref.at[pos_ref[0, t]], sem).start(priority=0)
            _row_copy(h_ref.at[t], xs_ref.at[pos_ref[1, t]], sem).start(priority=1)

        @pl.when(k > 0)
        def _():
            drain_chunk()

        return carry

    lax.fori_loop(0, n_chunks, chunk, 0)
    drain_chunk()


def _dispatch(pos2, zero_tile, h2_rows, n_rows):
    tile = h2_rows.shape[1:]
    return pl.pallas_call(
        _dispatch_kernel,
        grid_spec=pltpu.PrefetchScalarGridSpec(
            num_scalar_prefetch=2, grid=(1,),
            in_specs=[pl.BlockSpec(h2_rows.shape, lambda i, p, z: (0, 0, 0))],
            out_specs=pl.BlockSpec(memory_space=pl.ANY),
            scratch_shapes=[pltpu.VMEM((TM_GROUP,) + tile, F32),
                            pltpu.SemaphoreType.DMA(()), pltpu.SemaphoreType.DMA(())]),
        out_shape=jax.ShapeDtypeStruct((n_rows,) + tile, F32),
        compiler_params=_cparams(("arbitrary",)),
        name="moe_dispatch",
    )(pos2, zero_tile, h2_rows)


def _expert_kernel(te_ref, na_ref, xs_ref, wgu_ref, wd_ref, ys_ref,
                   xbuf, ybuf, sem_in, sem_out):
    del te_ref
    i = pl.program_id(0)
    last = pl.num_programs(0) - 1
    n_active = na_ref[0]
    slot = i % 2
    n_col = D_MODEL // LANES

    def in_copies(tile, s):
        rows = pl.ds(tile * TM_GROUP, TM_GROUP)
        return [pltpu.make_async_copy(xs_ref.at[rows, c], xbuf.at[s, :, c * LANES:(c + 1) * LANES],
                                      sem_in.at[s]) for c in range(n_col)]

    def out_copies(tile, s):
        rows = pl.ds(tile * TM_GROUP, TM_GROUP)
        return [pltpu.make_async_copy(ybuf.at[s, :, c * LANES:(c + 1) * LANES], ys_ref.at[rows, c],
                                      sem_out.at[s]) for c in range(n_col)]

    @pl.when(i == 0)
    def _():
        for cp in in_copies(0, 0):
            cp.start()

    @pl.when(i + 1 < n_active)
    def _():
        for cp in in_copies(i + 1, 1 - slot):
            cp.start()

    @pl.when(i >= 2)
    def _():
        for cp in out_copies(i - 2, slot):
            cp.wait()

    @pl.when(i < n_active)
    def _():
        for cp in in_copies(i, slot):
            cp.wait()
        xb = xbuf[slot].astype(BF16)
        z = _dot(xb, wgu_ref[...])
        d_e = wd_ref.shape[0]
        act = _silu(z[:, :d_e]) * z[:, d_e:]
        ybuf[slot] = _dot(act.astype(BF16), wd_ref[...])

    @pl.when(i >= n_active)
    def _():
        ybuf[slot] = jnp.zeros((TM_GROUP, D_MODEL), F32)

    for cp in out_copies(i, slot):
        cp.start()

    @pl.when(i == last)
    def _():
        for cp in out_copies(i, slot):
            cp.wait()

        @pl.when(i >= 1)
        def _():
            for cp in out_copies(i - 1, 1 - slot):
                cp.wait()


def _experts(tile_expert, n_active, xs, j, wgu_bf, wd_bf):
    n_rows = xs.shape[0]
    d_e = wd_bf.shape[2]
    wspec = lambda shape: pl.BlockSpec((None, None) + shape, lambda i, te, na: (j, te[i], 0, 0))
    return pl.pallas_call(
        _expert_kernel,
        grid_spec=pltpu.PrefetchScalarGridSpec(
            num_scalar_prefetch=2, grid=(n_rows // TM_GROUP,),
            in_specs=[pl.BlockSpec(memory_space=pl.ANY),
                      wspec((D_MODEL, 2 * d_e)), wspec((d_e, D_MODEL))],
            out_specs=pl.BlockSpec(memory_space=pl.ANY),
            scratch_shapes=[pltpu.VMEM((2, TM_GROUP, D_MODEL), F32),
                            pltpu.VMEM((2, TM_GROUP, D_MODEL), F32),
                            pltpu.SemaphoreType.DMA((2,)), pltpu.SemaphoreType.DMA((2,))]),
        out_shape=jax.ShapeDtypeStruct(xs.shape, F32),
        compiler_params=_cparams(("arbitrary",)),
        name="moe_experts",
    )(tile_expert, n_active, xs, wgu_bf, wd_bf)


def _combine_kernel(*refs, final):
    pos_ref, y_ref, x_ref, rec_ref, mod_ref = refs[:5]
    g_ref = refs[5] if final else None
    o_ref = refs[5 + int(final)]
    bufs = refs[6 + int(final):-1]
    sems = refs[-1]
    sets = [(bufs[2 * k], bufs[2 * k + 1], sems.at[k]) for k in range(4)]
    i = pl.program_id(0)
    last = pl.num_programs(0) - 1

    def issue(half_idx, buf_set):
        d1, d2, sem = buf_set
        base = half_idx * DMA_CHUNK

        for r in range(DMA_CHUNK):
            _row_copy(y_ref.at[pos_ref[0, base + r]], d1.at[r], sem).start(priority=1)
            _row_copy(y_ref.at[pos_ref[1, base + r]], d2.at[r], sem).start(priority=1)

    def drain(buf_set):
        d1, d2, sem = buf_set
        rows = pl.ds(0, DMA_CHUNK)
        _row_copy(y_ref.at[rows], d1, sem).wait()
        _row_copy(y_ref.at[rows], d2, sem).wait()

    def finish(half, buf_set):
        d1, d2, _ = buf_set
        rows = slice(half * DMA_CHUNK, (half + 1) * DMA_CHUNK)
        w1 = rec_ref[rows, R_W1:R_W1 + 1]
        w2 = rec_ref[rows, R_W2:R_W2 + 1]
        for c in range(D_MODEL // LANES):
            sl = slice(c * LANES, (c + 1) * LANES)
            y = w1 * d1[:, c, :] + w2 * d2[:, c, :]
            o_ref[rows, sl] = x_ref[rows, sl] + mod_ref[5:6, sl] * y
        if final:
            o_ref[rows, :] = _rms(o_ref[rows, :], g_ref[...])

    @pl.when(i == 0)
    def _():
        issue(0, sets[0])
        issue(1, sets[1])

    def half_block(half, cur, nxt):
        drain(cur)

        @pl.when(i < last)
        def _():
            issue(2 * i + 2 + half, nxt)
            finish(half, cur)

        @pl.when(i == last)
        def _():
            finish(half, cur)

    @pl.when(i % 2 == 0)
    def _():
        half_block(0, sets[0], sets[2])
        half_block(1, sets[1], sets[3])

    @pl.when(i % 2 == 1)
    def _():
        half_block(0, sets[2], sets[0])
        half_block(1, sets[3], sets[1])


def _combine(pos2, ys, x1, rec, mods, l, seq_len, lat, final_g):
    n = x1.shape[0]
    final = final_g is not None
    tile = ys.shape[1:]
    tm = 2 * DMA_CHUNK
    tiles_per_seq = max(seq_len // tm, 1)
    if lat:
        mod_idx = lambda i, p: (l, 1 + i // tiles_per_seq, 0, 0)
    else:
        mod_idx = lambda i, p: (l, 0, 0, 0)
    row = lambda i, p: (i, 0)
    slot = pltpu.VMEM((DMA_CHUNK,) + tile, F32)
    in_specs = [pl.BlockSpec(memory_space=pl.ANY),
                pl.BlockSpec((tm, D_MODEL), row),
                pl.BlockSpec((tm, LANES), row),
                pl.BlockSpec((None, None, 6, D_MODEL), mod_idx)]
    args = [pos2, ys, x1, rec, mods]
    if final:
        in_specs.append(pl.BlockSpec((1, D_MODEL), lambda i, p: (0, 0)))
        args.append(final_g)
    return pl.pallas_call(
        functools.partial(_combine_kernel, final=final),
        grid_spec=pltpu.PrefetchScalarGridSpec(
            num_scalar_prefetch=1, grid=(n // tm,),
            in_specs=in_specs,
            out_specs=pl.BlockSpec((tm, D_MODEL), row),
            scratch_shapes=[slot] * 8 + [pltpu.SemaphoreType.DMA((4,))]),
        out_shape=jax.ShapeDtypeStruct((n, D_MODEL), F32),
        compiler_params=_cparams(("arbitrary",)),
        name="moe_combine_final" if final else "moe_combine",
    )(*args)


def _moe(x1, h2_rows, rec, counts, mods, l, j, wgu_bf, wd_bf, seq_len, lat, final_g):
    n = x1.shape[0]
    n_tiles = 2 * n // TM_GROUP + N_EXPERTS
    pos2, tile_expert, n_active, zero_tile = _group_layout(rec, counts, n_tiles)
    xs = _dispatch(pos2, zero_tile, h2_rows, n_tiles * TM_GROUP)
    ys = _experts(tile_expert, n_active, xs, j, wgu_bf, wd_bf)
    return _combine(pos2, ys, x1, rec, mods, l, seq_len, lat, final_g)


def _final_norm_kernel(x_ref, g_ref, o_ref):
    o_ref[...] = _rms(x_ref[...], g_ref[...])


def _final_norm(x, g):
    n = x.shape[0]
    row = lambda i: (i, 0)
    return pl.pallas_call(
        _final_norm_kernel,
        grid=(n // TM_FF,),
        in_specs=[pl.BlockSpec((TM_FF, D_MODEL), row), pl.BlockSpec((1, D_MODEL), lambda i: (0, 0))],
        out_specs=pl.BlockSpec((TM_FF, D_MODEL), row),
        out_shape=jax.ShapeDtypeStruct((n, D_MODEL), F32),
        compiler_params=_cparams(("arbitrary",)),
        name="final_norm",
    )(x, g)


def _rope_tables(t_len):
    t = jnp.arange(t_len)
    pos = jnp.stack([t // GRID_W, t % GRID_W], axis=-1).astype(F32)
    inv = ROPE_THETA ** (-jnp.arange(0, AXIS_DIM, 2, dtype=F32) / AXIS_DIM)
    ang = pos[..., None] * inv
    cos, sin = jnp.cos(ang), jnp.sin(ang)
    c = jnp.concatenate([cos[:, 0], cos[:, 0], cos[:, 1], cos[:, 1]], axis=-1)
    s = jnp.concatenate([-sin[:, 0], sin[:, 0], -sin[:, 1], sin[:, 1]], axis=-1)
    return c, s


def _gate_weights(wa, wi):
    depth = wa.shape[0]
    groups = D_RNN // LANES

    def pair(w):
        w = w.reshape(depth, 2, groups, 2, RNN_BLOCK, RNN_BLOCK)
        z = jnp.zeros_like(w[:, :, :, 0])
        top = jnp.concatenate([w[:, :, :, 0], z], axis=-1)
        bot = jnp.concatenate([z, w[:, :, :, 1]], axis=-1)
        return jnp.concatenate([top, bot], axis=-2)

    w = jnp.concatenate([pair(wa), pair(wi)], axis=-1)
    return (0.5 * jnp.concatenate([w[:, 0], w[:, 1]], axis=-1)).astype(BF16)


def kernel(x_prompt, x_sample, cache_k, cache_v, state_rnn, c, c_ctx, w_mod, b_mod, norm1, norm2, w_in, conv_w, conv_b, lru_wa, lru_ba, lru_wi, lru_bi, lru_lambda, q_norm, k_norm, w_rnn_out, w_attn_out, w_out, ff_gate, ff_up, ff_down, router_w, router_b, exp_gate, exp_up, exp_down, final_norm):
    batch, seq, _ = x_prompt.shape
    dec_batch, dec_seq, _ = x_sample.shape
    depth = w_mod.shape[0]
    past = cache_k.shape[2]
    assert dec_batch + 1 <= MOD_ROWS

    cond8 = jnp.zeros((MOD_ROWS, D_MODEL), F32).at[0].set(c_ctx).at[1:1 + dec_batch].set(c)

    gate_w = _gate_weights(lru_wa, lru_wi)
    n_moe = router_w.shape[0]
    wr_pad = jnp.zeros((n_moe, D_MODEL, LANES), F32).at[:, :, :N_EXPERTS].set(router_w)
    wr_head = wr_pad.astype(BF16)
    wr_t = jnp.concatenate([wr_head, (wr_pad - wr_head.astype(F32)).astype(BF16)], axis=-1)
    br_pad = jnp.zeros((n_moe, 1, LANES), F32).at[:, 0, :N_EXPERTS].set(router_b)

    norm1_3 = norm1.reshape(depth, 1, D_MODEL)
    norm2_3 = norm2.reshape(depth, 1, D_MODEL)
    qn3 = q_norm.reshape(depth, 1, HEAD_DIM)
    kn3 = k_norm.reshape(depth, 1, HEAD_DIM)
    conv_b3 = conv_b.reshape(depth, 1, D_RNN)
    ba4 = (0.5 * lru_ba).reshape(depth, 2, 1, D_RNN)
    bi4 = (0.5 * lru_bi).reshape(depth, 2, 1, D_RNN)
    lam4 = lru_lambda.reshape(depth, 2, 1, D_RNN)
    state5 = state_rnn.reshape(dec_batch, depth, 2, 1, D_RNN)
    cache_k4 = cache_k.reshape(dec_batch, depth, past, KV_DIM)
    cache_v4 = cache_v.reshape(dec_batch, depth, past, KV_DIM)
    rope_tabs = _rope_tables(dec_seq)
    final_g = final_norm.reshape(1, D_MODEL)

    n_steps = dec_batch * dec_seq // TQ
    assert n_steps == dec_batch * dec_seq // TM_ROPE

    def flat(w):
        return w.reshape(-1, w.shape[-1])

    def channel_names(l):
        return ("exp_gate_up", "exp_down") if l % 2 == 1 else ("ff_gate", "ff_up", "ff_down")

    stacked = dict(w_in=(w_in,), w_rnn_out=(w_rnn_out,), w_attn_out=(w_attn_out,), w_out=(w_out,),
                   ff_gate=(ff_gate,), ff_up=(ff_up,), ff_down=(ff_down,),
                   exp_gate_up=(exp_gate, exp_up), exp_down=(exp_down,))

    def layer_shape(name):
        parts = stacked[name]
        return (1,) + parts[0].shape[1:-1] + (sum(w.shape[-1] for w in parts),)

    def layer_index(name, l):
        return l if name.startswith("w_") else l // 2

    def rider(name, l, steps):
        parts = stacked[name]
        rows = flat(parts[0]).shape[0] // parts[0].shape[0] // steps
        return _Rider([flat(w) for w in parts], rows, layer_index(name, l) * steps)

    def names_of(l):
        return ("w_in", "w_rnn_out", "w_attn_out", "w_out") + channel_names(l)

    weights = {l: {} for l in range(depth)}

    def keep(l, names, arrays):
        for name, arr in zip(names, arrays):
            weights[l][name] = arr.reshape(layer_shape(name))

    mod_steps = _modulation_steps(depth)
    mods, w_in0 = _modulation(cond8, w_mod, b_mod, riders=[rider("w_in", 0, mod_steps)])
    mods = mods.reshape(depth, MOD_ROWS, 6, D_MODEL)
    keep(0, ("w_in",), [w_in0])
    ctx_steps = batch * seq // TM

    def run_layer(x, l, lat):
        seq_len = dec_seq if lat else seq
        wts = weights[l]
        convert_next = lat and l + 1 < depth
        if convert_next:
            proj_names = names_of(l + 1)[:4]
            attn_names = names_of(l + 1)[4:]
            proj_riders = [rider(name, l + 1, n_steps) for name in proj_names]
            attn_riders = [rider(name, l + 1, n_steps) for name in attn_names]
            target = l + 1
        elif l == 0 and not lat:
            proj_names, attn_names, attn_riders, target = names_of(0)[1:], (), [], 0
            proj_riders = [rider(name, 0, ctx_steps) for name in proj_names]
        else:
            proj_names, attn_names, proj_riders, attn_riders, target = (), (), [], [], None
        outs = _in_proj(x, mods, l, norm1_3, wts["w_in"], 0, qn3, kn3,
                        rope_tabs if lat else None, seq_len, riders=proj_riders,
                        caches=None if lat else new_kv)
        xr, yr, q, _, kr, v, gr, ga = outs[:8]
        if target is not None:
            keep(target, proj_names, outs[8:8 + len(proj_names)])
        if not lat:
            new_kv[:] = outs[8 + len(proj_names):]
        out_r, end_f, end_b = _rnn(xr, yr, l, conv_w, conv_b3, gate_w, ba4, bi4, lam4,
                                   state5 if lat else None, seq_len)
        if lat:
            outs = _attn_lat(q, kr, v, cache_k4, cache_v4, l, seq_len, riders=attn_riders)
            out_a = outs[0]
            if target is not None:
                keep(target, attn_names, outs[1:])
        else:
            out_a = _attn_ctx(q, kr, v, seq_len)
        mix_args = (x, out_r, out_a, gr, ga, mods, l, norm2_3, wts["w_rnn_out"],
                    wts["w_attn_out"], wts["w_out"], 0, seq_len, lat)
        if l % 2 == 1:
            x1, h2_rows, rec, counts = _mix_out(*mix_args, router=(l // 2, wr_t, br_pad))
            x = _moe(x1, h2_rows, rec, counts, mods, l, 0, wts["exp_gate_up"], wts["exp_down"],
                     seq_len, lat, final_g if l == depth - 1 else None)
        else:
            x1, h2 = _mix_out(*mix_args)
            x = _ffn(x1, h2, mods, l, 0, wts["ff_gate"], wts["ff_up"], wts["ff_down"],
                     seq_len, lat)
            if l == depth - 1:
                x = _final_norm(x, final_g)
        return x, jnp.stack([end_f, end_b], axis=1)

    new_kv = [jnp.zeros((batch, depth, seq, N_KV_HEADS, HEAD_DIM), F32) for _ in range(2)]
    y_ctx = x_prompt.reshape(batch * seq, D_MODEL)
    y_lat = x_sample.reshape(dec_batch * dec_seq, D_MODEL)
    ss = []
    for l in range(depth):
        y_ctx, st = run_layer(y_ctx, l, False)
        ss.append(st)
        y_lat, _ = run_layer(y_lat, l, True)

    new_state = jnp.stack(ss, axis=1)
    return (y_ctx.reshape(batch, seq, D_MODEL), y_lat.reshape(dec_batch, dec_seq, D_MODEL),
            new_kv[0], new_kv[1], new_state)
```

```python
import functools

import jax
import jax.numpy as jnp
from jax import lax
from jax.experimental import pallas as pl
from jax.experimental.pallas import tpu as pltpu

F32 = jnp.float32
BF16 = jnp.bfloat16

D_MODEL = 1024
D_RNN = 1024
N_RNN_BLOCKS = 16
RNN_BLOCK = D_RNN // N_RNN_BLOCKS
CONV_W = 4
CONV_LEFT = 2
LRU_C = 8.0
N_HEADS = 8
N_KV_HEADS = 2
HEAD_DIM = 128
AXIS_DIM = HEAD_DIM // 2
ROPE_THETA = 10000.0
GRID_W = 64
Q_DIM = N_HEADS * HEAD_DIM
KV_DIM = N_KV_HEADS * HEAD_DIM
N_EXPERTS = 8
EPS = 1e-6
LOG2_E = 1.4426950408889634
RSQRT_FLOOR = 1e-30
D_IN = 2 * D_RNN + Q_DIM + 2 * KV_DIM + 2 * D_MODEL
O_XR, O_YR, O_Q = 0, D_RNN, 2 * D_RNN
O_K = O_Q + Q_DIM
O_V = O_K + KV_DIM
O_GR = O_V + KV_DIM
O_GA = O_GR + D_MODEL

LANES = 128
SUBLANES = 8
MOD_ROWS = 8
VMEM_LIMIT = 56 * 1024 * 1024

TM = 512
TM_ROPE = 256
RESIDENT = pl.Buffered(1)
RNN_T = 256
RNN_SEG = SUBLANES
RNN_C = 512
RNN_TCH = 64
TQ = 256
ATTN_CTX_SEQS = 2
TM_FF = 512
TM_GROUP = 256
DMA_CHUNK = 256


def _cparams(sem):
    return pltpu.CompilerParams(dimension_semantics=sem, vmem_limit_bytes=VMEM_LIMIT)


def _sigmoid(x):
    return 1.0 / (1.0 + jnp.exp(-x))


def _silu(x):
    return x * _sigmoid(x)


def _gelu_tanh(x):
    return 0.5 * x * (1.0 + jnp.tanh(0.7978845608028654 * (x + 0.044715 * (x * x * x))))


def _rms(x, g):
    ms = jnp.mean(x * x, axis=-1, keepdims=True)
    return x * lax.rsqrt(ms + EPS) * g


def _dot(a, b):
    return jnp.dot(a, b, preferred_element_type=F32)


MOD_TN = 1536


def _mod_kernel(*refs, groups):
    n_rin = sum(groups)
    cond_ref, w_ref, b_ref = refs[:3]
    o_ref = refs[3 + n_rin]
    _convert_riders(refs[3:3 + n_rin], refs[4 + n_rin:], groups)
    s = _silu(cond_ref[...]).astype(BF16)
    o_ref[...] = _dot(s, w_ref[...].astype(BF16)) + b_ref[...]


def _modulation_steps(depth):
    return depth * (6 * D_MODEL // MOD_TN)


def _modulation(cond8, w_mod, b_mod, riders=()):
    depth = w_mod.shape[0]
    per_layer = 6 * D_MODEL // MOD_TN
    r_in, r_out, r_shape = _rider_specs(riders, depth * per_layer, lambda l, j: l * per_layer + j)
    return pl.pallas_call(
        functools.partial(_mod_kernel, groups=_rider_groups(riders)),
        grid=(depth, per_layer),
        in_specs=[
            pl.BlockSpec((MOD_ROWS, D_MODEL), lambda l, j: (0, 0)),
            pl.BlockSpec((None, D_MODEL, MOD_TN), lambda l, j: (l, 0, j)),
            pl.BlockSpec((None, 1, MOD_TN), lambda l, j: (l, 0, j)),
        ] + r_in,
        out_specs=[pl.BlockSpec((None, MOD_ROWS, MOD_TN), lambda l, j: (l, 0, j))] + r_out,
        out_shape=[jax.ShapeDtypeStruct((depth, MOD_ROWS, 6 * D_MODEL), F32)] + r_shape,
        compiler_params=_cparams(("arbitrary", "arbitrary")),
        name="adaln_mod",
    )(cond8, w_mod, b_mod.reshape(depth, 1, 6 * D_MODEL), *_rider_args(riders))


def _swap_halves(x):
    lane = lax.broadcasted_iota(jnp.int32, x.shape, 1)
    return jnp.where((lane % AXIS_DIM) < AXIS_DIM // 2,
                     pltpu.roll(x, LANES - AXIS_DIM // 2, axis=1),
                     pltpu.roll(x, AXIS_DIM // 2, axis=1))


class _Rider:
    def __init__(self, w2d, rows, first):
        self.parts = list(w2d) if isinstance(w2d, (list, tuple)) else [w2d]
        self.rows, self.first = rows, first


def _rider_specs(riders, n_steps, step_of):
    in_specs, out_specs, out_shape = [], [], []
    for r in riders:
        for w in r.parts:
            assert r.rows % 16 == 0 and (r.first + n_steps) * r.rows <= w.shape[0]
            in_specs.append(pl.BlockSpec((r.rows, w.shape[1]),
                                         lambda *g, r=r: (r.first + step_of(*g), 0)))
        cols = sum(w.shape[1] for w in r.parts)
        out_specs.append(pl.BlockSpec((r.rows, cols), lambda *g: (step_of(*g), 0)))
        out_shape.append(jax.ShapeDtypeStruct((r.rows * n_steps, cols), BF16))
    return in_specs, out_specs, out_shape


def _rider_groups(riders):
    return tuple(len(r.parts) for r in riders)


def _rider_args(riders):
    return [w for r in riders for w in r.parts]


def _convert_riders(in_refs, out_refs, groups):
    k = 0
    for o_ref, n_parts in zip(out_refs, groups):
        col = 0
        for w_ref in in_refs[k:k + n_parts]:
            o_ref[:, col:col + w_ref.shape[1]] = w_ref[...].astype(BF16)
            col += w_ref.shape[1]
        k += n_parts


def _in_proj_kernel(*refs, rope, groups, cache_layer, seq_len):
    n_in = 8 if rope else 6
    n_rin, n_rout = sum(groups), len(groups)
    n_cache = 0 if cache_layer is None else 2
    first_out = n_in + n_rin + n_cache
    _convert_riders(refs[n_in:n_in + n_rin], refs[first_out + 8:first_out + 8 + n_rout], groups)
    cache_refs = refs[first_out + 8 + n_rout:]
    refs = refs[:n_in] + refs[first_out:first_out + 8]
    if rope:
        (x_ref, mod_ref, n1_ref, w_ref, qn_ref, kn_ref, c_ref, s_ref,
         xr_ref, yr_ref, q_ref, kno_ref, kro_ref, v_ref, gr_ref, ga_ref) = refs
    else:
        (x_ref, mod_ref, n1_ref, w_ref, qn_ref, kn_ref,
         xr_ref, yr_ref, q_ref, kno_ref, kro_ref, v_ref, gr_ref, ga_ref) = refs
    x = x_ref[...]
    h = _rms(x, n1_ref[...]) * (1.0 + mod_ref[1:2, :]) + mod_ref[0:1, :]
    hb = h.astype(BF16)

    xr_ref[...] = _dot(hb, w_ref[:, O_XR:O_XR + D_RNN])
    yr_ref[...] = _dot(hb, w_ref[:, O_YR:O_YR + D_RNN])
    gr_ref[...] = _dot(hb, w_ref[:, O_GR:O_GR + D_MODEL]).astype(BF16)
    ga_ref[...] = _dot(hb, w_ref[:, O_GA:O_GA + D_MODEL]).astype(BF16)
    v_ref[...] = _dot(hb, w_ref[:, O_V:O_V + KV_DIM])

    if rope:
        cs, sn = c_ref[...], s_ref[...]
    zq = _dot(hb, w_ref[:, O_Q:O_Q + Q_DIM])
    for hd in range(N_HEADS):
        t = _rms(zq[:, hd * HEAD_DIM:(hd + 1) * HEAD_DIM], qn_ref[...])
        if rope:
            t = t * cs + _swap_halves(t) * sn
        q_ref[:, hd * HEAD_DIM:(hd + 1) * HEAD_DIM] = t.astype(BF16)
    zk = _dot(hb, w_ref[:, O_K:O_K + KV_DIM])
    for hd in range(N_KV_HEADS):
        t = _rms(zk[:, hd * HEAD_DIM:(hd + 1) * HEAD_DIM], kn_ref[...])
        kno_ref[:, hd * HEAD_DIM:(hd + 1) * HEAD_DIM] = t
        if rope:
            t = t * cs + _swap_halves(t) * sn
        kro_ref[:, hd * HEAD_DIM:(hd + 1) * HEAD_DIM] = t.astype(BF16)

    if cache_layer is None:
        return
    newk_ref, newv_ref, kbuf, vbuf, csem = cache_refs
    step = pl.program_id(0)
    slot = step % 2
    seqs_per_tile = x_ref.shape[0] // seq_len

    def cache_copies(k, s):
        cps = []
        for sq in range(seqs_per_tile):
            rows = slice(sq * seq_len, (sq + 1) * seq_len)
            for hd in range(N_KV_HEADS):
                cols = slice(hd * HEAD_DIM, (hd + 1) * HEAD_DIM)
                dst = (k * seqs_per_tile + sq, cache_layer, slice(None), hd, slice(None))
                cps.append(pltpu.make_async_copy(kbuf.at[s, rows, cols], newk_ref.at[dst], csem.at[s]))
                cps.append(pltpu.make_async_copy(vbuf.at[s, rows, cols], newv_ref.at[dst], csem.at[s]))
        return cps

    @pl.when(step >= 2)
    def _():
        for cp in cache_copies(step - 2, slot):
            cp.wait()

    kbuf[slot] = kno_ref[...]
    vbuf[slot] = v_ref[...]
    for cp in cache_copies(step, slot):
        cp.start()

    @pl.when(step == pl.num_programs(0) - 1)
    def _():
        for cp in cache_copies(step, slot):
            cp.wait()

        @pl.when(step >= 1)
        def _():
            for cp in cache_copies(step - 1, 1 - slot):
                cp.wait()


def _in_proj(x, mods, l, norm1, w_in_bf, wl, q_norm, k_norm, rope_tabs, seq_len, riders=(),
             caches=None):
    n = x.shape[0]
    rope = rope_tabs is not None
    tm = TM_ROPE if rope else TM
    tiles_per_seq = seq_len // tm
    if rope:
        mod_idx = lambda i: (l, 1 + i // tiles_per_seq, 0, 0)
    else:
        mod_idx = lambda i: (l, 0, 0, 0)
    row = lambda i: (i, 0)
    in_specs = [
        pl.BlockSpec((tm, D_MODEL), row),
        pl.BlockSpec((None, None, 6, D_MODEL), mod_idx),
        pl.BlockSpec((None, 1, D_MODEL), lambda i: (l, 0, 0)),
        pl.BlockSpec((None, D_MODEL, D_IN), lambda i: (wl, 0, 0), pipeline_mode=RESIDENT),
        pl.BlockSpec((None, 1, HEAD_DIM), lambda i: (l, 0, 0)),
        pl.BlockSpec((None, 1, HEAD_DIM), lambda i: (l, 0, 0)),
    ]
    args = [x, mods, norm1, w_in_bf, q_norm, k_norm]
    if rope:
        in_specs += [pl.BlockSpec((tm, HEAD_DIM), lambda i: (i % tiles_per_seq, 0))] * 2
        args += list(rope_tabs)
    widths = [(D_RNN, F32), (D_RNN, F32), (Q_DIM, BF16), (KV_DIM, F32), (KV_DIM, BF16),
              (KV_DIM, F32), (D_MODEL, BF16), (D_MODEL, BF16)]
    r_in, r_out, r_shape = _rider_specs(riders, n // tm, lambda i: i)
    in_specs = in_specs + r_in
    out_specs = [pl.BlockSpec((tm, w), row) for w, _ in widths] + r_out
    out_shape = [jax.ShapeDtypeStruct((n, w), dt) for w, dt in widths] + r_shape
    args = args + _rider_args(riders)
    scratch, aliases = [], {}
    if caches is not None:
        assert tm % seq_len == 0
        for c in caches:
            aliases[len(args)] = len(out_shape)
            args.append(c)
            in_specs.append(pl.BlockSpec(memory_space=pl.ANY))
            out_specs.append(pl.BlockSpec(memory_space=pl.ANY))
            out_shape.append(jax.ShapeDtypeStruct(c.shape, c.dtype))
        scratch = [pltpu.VMEM((2, tm, KV_DIM), F32), pltpu.VMEM((2, tm, KV_DIM), F32),
                   pltpu.SemaphoreType.DMA((2,))]
    return pl.pallas_call(
        functools.partial(_in_proj_kernel, rope=rope, groups=_rider_groups(riders),
                          cache_layer=None if caches is None else l, seq_len=seq_len),
        grid=(n // tm,),
        in_specs=in_specs,
        out_specs=out_specs,
        out_shape=out_shape,
        scratch_shapes=scratch,
        input_output_aliases=aliases,
        compiler_params=_cparams(("arbitrary",)),
        name="in_proj_lat" if rope else "in_proj_ctx",
    )(*args)


def _rnn_kernel(*refs, chained):
    refs = list(refs)
    xr_ref, yr_ref, cw_ref, cb_ref, gw_ref, ba_ref, bi_ref, lam_ref = refs[:8]
    refs = refs[8:]
    h0_ref = refs.pop(0) if chained else None
    out_ref, sf_ref, sb_ref, xext, ybuf, obuf, a_f, b_f, a_b, b_b, sem_in, sem_out = refs

    g = pl.program_id(0)
    cb = pl.program_id(1)
    n_cb = pl.num_programs(1)
    step = g * n_cb + cb
    n_steps = pl.num_programs(0) * n_cb
    slot = step % 2

    def hbm_window(kk, s):
        rows = pl.ds(pl.multiple_of(((kk // n_cb) * RNN_SEG + s) * RNN_T, RNN_T), RNN_T)
        cols = pl.ds(pl.multiple_of((kk % n_cb) * RNN_C, RNN_C), RNN_C)
        return rows, cols

    def in_copies(kk, sl_):
        cps = []
        for s in range(RNN_SEG):
            win = hbm_window(kk, s)
            cps.append(pltpu.make_async_copy(
                xr_ref.at[win], xext.at[sl_, pl.ds(CONV_LEFT, RNN_T), s], sem_in.at[sl_]))
            cps.append(pltpu.make_async_copy(yr_ref.at[win], ybuf.at[sl_, :, s], sem_in.at[sl_]))
        return cps

    def out_copies(kk, sl_):
        return [pltpu.make_async_copy(obuf.at[sl_, :, s], out_ref.at[hbm_window(kk, s)],
                                      sem_out.at[sl_]) for s in range(RNN_SEG)]

    @pl.when(step == 0)
    def _():
        for cp in in_copies(0, 0):
            cp.start()

    @pl.when(step + 1 < n_steps)
    def _():
        for cp in in_copies(step + 1, 1 - slot):
            cp.start()

    for cp in in_copies(step, slot):
        cp.wait()

    seg = lax.broadcasted_iota(jnp.int32, (RNN_SEG, RNN_C), 0)
    zero_row = jnp.zeros((RNN_SEG, RNN_C), F32)
    if chained:
        xext[slot, 0] = jnp.where(seg == 0, zero_row, pltpu.roll(xext[slot, RNN_T], 1, axis=0))
        xext[slot, 1] = jnp.where(seg == 0, zero_row, pltpu.roll(xext[slot, RNN_T + 1], 1, axis=0))
        xext[slot, RNN_T + CONV_LEFT] = jnp.where(
            seg == RNN_SEG - 1, zero_row, pltpu.roll(xext[slot, CONV_LEFT], RNN_SEG - 1, axis=0))
    else:
        xext[slot, 0] = zero_row
        xext[slot, 1] = zero_row
        xext[slot, RNN_T + CONV_LEFT] = zero_row

    scans = ((a_f, b_f), (a_b, b_b))
    rows2d = RNN_TCH * RNN_SEG

    def gate_chunk(tc, carry):
        t0 = pl.multiple_of(tc * RNN_TCH, RNN_TCH)
        for j in range(RNN_C // LANES):
            sl = slice(j * LANES, (j + 1) * LANES)
            xc = jnp.broadcast_to(cb_ref[:, sl].reshape(1, 1, LANES), (RNN_TCH, RNN_SEG, LANES))
            for k in range(CONV_W):
                xc = xc + (xext[slot, pl.ds(t0 + k, RNN_TCH), :, sl]
                           * cw_ref[k:k + 1, sl].reshape(1, 1, LANES))
            xc = xc.reshape(rows2d, LANES)
            pre = _dot(xc.astype(BF16), gw_ref[j])
            half_xc = 0.5 * xc
            for d, (a_s, b_s) in enumerate(scans):
                lam = lam_ref[d, :, sl]
                log2_decay = (-LRU_C * LOG2_E) * (jnp.maximum(-lam, 0.0)
                                                  + jnp.log(1.0 + jnp.exp(-jnp.abs(lam))))
                t_r = jnp.tanh(pre[:, 2 * d * LANES:(2 * d + 1) * LANES] + ba_ref[d, :, sl])
                t_i = jnp.tanh(pre[:, (2 * d + 1) * LANES:(2 * d + 2) * LANES]
                               + bi_ref[d, :, sl])
                half_decay = 0.5 * log2_decay
                a = jnp.exp2(t_r * half_decay + half_decay)
                y = 1.0 - a * a
                b = (y * lax.rsqrt(jnp.maximum(y, RSQRT_FLOOR))) * ((t_i + 1.0) * half_xc)
                a_s[pl.ds(t0, RNN_TCH), :, sl] = a.reshape(RNN_TCH, RNN_SEG, LANES)
                b_s[pl.ds(t0, RNN_TCH), :, sl] = b.reshape(RNN_TCH, RNN_SEG, LANES)
        return carry

    n_chunks = RNN_T // RNN_TCH
    ones = jnp.ones((RNN_SEG, RNN_C), F32)

    def fwd_scan_chunk(tc, hf, pf):
        for u in range(RNN_TCH):
            t = tc * RNN_TCH + u
            af_t = a_f[t]
            hf = af_t * hf + b_f[t]
            b_f[t] = hf
            if chained:
                pf = af_t * pf
                a_f[t] = pf
        return hf, pf

    def gate_and_scan(tc, carry):
        carry = fwd_scan_chunk(tc - 1, *carry)
        gate_chunk(tc, 0)
        return carry

    gate_chunk(0, 0)
    hf, pf = lax.fori_loop(1, n_chunks, gate_and_scan, (zero_row, ones))
    hf, pf = fwd_scan_chunk(n_chunks - 1, hf, pf)

    @pl.when(step >= 2)
    def _():
        for cp in out_copies(step - 2, slot):
            cp.wait()

    if not chained:
        def bwd_finish_chunk(i, hb):
            tc = n_chunks - 1 - i
            for u in reversed(range(RNN_TCH)):
                t = tc * RNN_TCH + u
                hb = a_b[t] * hb + b_b[t]
                obuf[slot, t] = (b_f[t] + hb) * _gelu_tanh(ybuf[slot, t])
            return hb

        hb = lax.fori_loop(0, n_chunks, bwd_finish_chunk, zero_row)
        sf_ref[...] = hf
        sb_ref[...] = hb
    else:
        st_f = jnp.broadcast_to(h0_ref[0], (RNN_SEG, RNN_C))
        st_b = jnp.broadcast_to(h0_ref[1], (RNN_SEG, RNN_C))
        in_f = st_f
        for _ in range(RNN_SEG - 1):
            in_f = jnp.where(seg == 0, st_f, pltpu.roll(pf * in_f + hf, 1, axis=0))
        sf_ref[...] = pf * in_f + hf

        def bwd_scan_chunk(i, carry):
            hb, pb = carry
            tc = n_chunks - 1 - i
            for u in reversed(range(RNN_TCH)):
                t = tc * RNN_TCH + u
                ab_t = a_b[t]
                hb = ab_t * hb + b_b[t]
                b_b[t] = hb
                pb = ab_t * pb
                a_b[t] = pb
                b_f[t] = a_f[t] * in_f + b_f[t]
            return hb, pb

        hb, pb = lax.fori_loop(0, n_chunks, bwd_scan_chunk, (zero_row, ones))
        in_b = st_b
        for _ in range(RNN_SEG - 1):
            in_b = jnp.where(seg == RNN_SEG - 1, st_b,
                             pltpu.roll(pb * in_b + hb, RNN_SEG - 1, axis=0))
        sb_ref[...] = pb * in_b + hb

        def finish_chunk(tc, carry):
            rows = pl.ds(pl.multiple_of(tc * RNN_TCH, RNN_TCH), RNN_TCH)
            h_b = a_b[rows] * in_b.reshape(1, RNN_SEG, RNN_C) + b_b[rows]
            obuf[slot, rows] = (b_f[rows] + h_b) * _gelu_tanh(ybuf[slot, rows])
            return carry

        lax.fori_loop(0, n_chunks, finish_chunk, 0)

    for cp in out_copies(step, slot):
        cp.start()

    @pl.when(step == n_steps - 1)
    def _():
        for cp in out_copies(step, slot):
            cp.wait()

        @pl.when(step >= 1)
        def _():
            for cp in out_copies(step - 1, 1 - slot):
                cp.wait()


def _rnn(xr, yr, l, conv_w, conv_b, gate_w, lru_ba, lru_bi, lru_lambda, state5, seq_len):
    n = xr.shape[0]
    chained = state5 is not None
    group_len = RNN_SEG * RNN_T
    assert n % group_len == 0 and seq_len in ((group_len,) if chained else (RNN_T,))
    n_groups = n // group_len
    lanes_per_step = RNN_C // LANES
    vec = pl.BlockSpec((None, 2, 1, RNN_C), lambda g, c: (l, 0, 0, c))
    in_specs = [
        pl.BlockSpec(memory_space=pl.ANY),
        pl.BlockSpec(memory_space=pl.ANY),
        pl.BlockSpec((None, CONV_W, RNN_C), lambda g, c: (l, 0, c)),
        pl.BlockSpec((None, 1, RNN_C), lambda g, c: (l, 0, c)),
        pl.BlockSpec((None, lanes_per_step, LANES, 4 * LANES), lambda g, c: (l, c, 0, 0)),
        vec, vec, vec,
    ]
    args = [xr, yr, conv_w, conv_b, gate_w, lru_ba, lru_bi, lru_lambda]
    if chained:
        in_specs.append(pl.BlockSpec((None, None, 2, 1, RNN_C), lambda g, c: (g, l, 0, 0, c)))
        args.append(state5)
    state_spec = pl.BlockSpec((RNN_SEG, RNN_C), lambda g, c: (g, c))
    state_shape = jax.ShapeDtypeStruct((n_groups * RNN_SEG, D_RNN), F32)
    work = lambda t: pltpu.VMEM((t, RNN_SEG, RNN_C), F32)
    return pl.pallas_call(
        functools.partial(_rnn_kernel, chained=chained),
        grid=(n_groups, D_RNN // RNN_C),
        in_specs=in_specs,
        out_specs=[pl.BlockSpec(memory_space=pl.ANY), state_spec, state_spec],
        out_shape=[jax.ShapeDtypeStruct((n, D_RNN), F32), state_shape, state_shape],
        scratch_shapes=[
            pltpu.VMEM((2, RNN_T + CONV_W - 1, RNN_SEG, RNN_C), F32),
            pltpu.VMEM((2, RNN_T, RNN_SEG, RNN_C), F32),
            pltpu.VMEM((2, RNN_T, RNN_SEG, RNN_C), F32),
            work(RNN_T), work(RNN_T), work(RNN_T), work(RNN_T),
            pltpu.SemaphoreType.DMA((2,)), pltpu.SemaphoreType.DMA((2,)),
        ],
        compiler_params=_cparams(("arbitrary", "arbitrary")),
        name="rnn_lat" if chained else "rnn_ctx",
    )(*args)


def _attend_heads(q_ref, k_all, v_all, o_ref, stack_heads):
    exp2_scale = (1.0 / (HEAD_DIM ** 0.5)) * LOG2_E
    group = N_HEADS // N_KV_HEADS
    tq = q_ref.shape[0]
    for kh in range(N_KV_HEADS):
        k = k_all[:, kh * HEAD_DIM:(kh + 1) * HEAD_DIM]
        v = v_all[:, kh * HEAD_DIM:(kh + 1) * HEAD_DIM]
        heads = [slice((kh * group + g) * HEAD_DIM, (kh * group + g + 1) * HEAD_DIM)
                 for g in range(group)]
        stacks = [heads] if stack_heads else [[sl] for sl in heads]
        for stack in stacks:
            q = jnp.concatenate([q_ref[:, sl] for sl in stack], axis=0)
            s = lax.dot_general(q, k, (((1,), (1,)), ((), ())), preferred_element_type=F32)
            m = jnp.max(s, axis=-1, keepdims=True)
            p = jnp.exp2((s - m) * exp2_scale)
            denom = jnp.sum(p, axis=-1, keepdims=True)
            o = (_dot(p.astype(BF16), v) / denom).astype(BF16)
            for g, sl in enumerate(stack):
                o_ref[:, sl] = o[g * tq:(g + 1) * tq]


def _attn_ctx_kernel(q_ref, k_ref, v_ref, o_ref, *, seq_len):
    for s in range(q_ref.shape[0] // seq_len):
        rows = pl.ds(s * seq_len, seq_len)
        _attend_heads(q_ref.at[rows], k_ref[rows, :], v_ref[rows, :].astype(BF16), o_ref.at[rows],
                      stack_heads=True)


def _attn_ctx(q, kr, v, seq_len):
    n = q.shape[0]
    row = lambda b: (b, 0)
    tile = ATTN_CTX_SEQS * seq_len
    return pl.pallas_call(
        functools.partial(_attn_ctx_kernel, seq_len=seq_len),
        grid=(n // tile,),
        in_specs=[pl.BlockSpec((tile, Q_DIM), row),
                  pl.BlockSpec((tile, KV_DIM), row),
                  pl.BlockSpec((tile, KV_DIM), row)],
        out_specs=pl.BlockSpec((tile, Q_DIM), row),
        out_shape=jax.ShapeDtypeStruct((n, Q_DIM), BF16),
        compiler_params=_cparams(("arbitrary",)),
        name="attn_ctx",
    )(q, kr, v)


def _attn_lat_kernel(*refs, past, groups):
    n_rin, n_rout = sum(groups), len(groups)
    _convert_riders(refs[5:5 + n_rin], refs[6 + n_rin:6 + n_rin + n_rout], groups)
    q_ref, kc_ref, vc_ref, kl_ref, vl_ref = refs[:5]
    o_ref = refs[5 + n_rin]
    k_all, v_all = refs[6 + n_rin + n_rout:]

    @pl.when(pl.program_id(1) == 0)
    def _():
        k_all[0:past, :] = kc_ref[...].astype(BF16)
        k_all[past:, :] = kl_ref[...]
        v_all[0:past, :] = vc_ref[...].astype(BF16)
        v_all[past:, :] = vl_ref[...].astype(BF16)

    _attend_heads(q_ref, k_all[...], v_all[...], o_ref, stack_heads=False)


def _attn_lat(q, kr, v, cache_k4, cache_v4, l, seq_len, riders=()):
    n = q.shape[0]
    past = cache_k4.shape[2]
    nq = seq_len // TQ
    step_of = lambda b, j: b * nq + j
    r_in, r_out, r_shape = _rider_specs(riders, (n // seq_len) * nq, step_of)
    return pl.pallas_call(
        functools.partial(_attn_lat_kernel, past=past, groups=_rider_groups(riders)),
        grid=(n // seq_len, nq),
        in_specs=[pl.BlockSpec((TQ, Q_DIM), lambda b, j: (b * nq + j, 0)),
                  pl.BlockSpec((None, None, past, KV_DIM), lambda b, j: (b, l, 0, 0)),
                  pl.BlockSpec((None, None, past, KV_DIM), lambda b, j: (b, l, 0, 0)),
                  pl.BlockSpec((seq_len, KV_DIM), lambda b, j: (b, 0)),
                  pl.BlockSpec((seq_len, KV_DIM), lambda b, j: (b, 0))] + r_in,
        out_specs=[pl.BlockSpec((TQ, Q_DIM), lambda b, j: (b * nq + j, 0))] + r_out,
        out_shape=[jax.ShapeDtypeStruct((n, Q_DIM), BF16)] + r_shape,
        scratch_shapes=[pltpu.VMEM((past + seq_len, KV_DIM), BF16),
                        pltpu.VMEM((past + seq_len, KV_DIM), BF16)],
        compiler_params=_cparams(("arbitrary", "arbitrary")),
        name="attn_lat",
    )(q, cache_k4, cache_v4, kr, v, *_rider_args(riders))


R_E1, R_E2, R_W1, R_W2, R_RANK1, R_RANK2 = range(6)


def _route(h2, wrt_ref, brt_ref, count_ref):
    shape = (h2.shape[0], LANES)
    lane = lax.broadcasted_iota(jnp.int32, shape, 1).astype(F32)
    neg = jnp.float32(-jnp.inf)
    h_hi = h2.astype(BF16)
    h_lo = (h2 - h_hi.astype(F32)).astype(BF16)
    z_hi = _dot(h_hi, wrt_ref[...])
    z_lo = _dot(h_lo, wrt_ref[...])
    logits = ((z_hi[:, :LANES] + z_hi[:, LANES:]) + (z_lo[:, :LANES] + z_lo[:, LANES:])
              + brt_ref[...])
    logits = jnp.where(lane < N_EXPERTS, logits, neg)
    no_lane = jnp.float32(LANES)
    v1 = jnp.max(logits, axis=-1, keepdims=True)
    i1 = jnp.min(jnp.where(logits == v1, lane, no_lane), axis=-1, keepdims=True)
    rest = jnp.where(lane == i1, neg, logits)
    v2 = jnp.max(rest, axis=-1, keepdims=True)
    i2 = jnp.min(jnp.where(rest == v2, lane, no_lane), axis=-1, keepdims=True)
    e2 = jnp.exp(v2 - v1)
    w1 = 1.0 / (1.0 + e2)
    w2 = e2 / (1.0 + e2)
    chosen = jnp.where((lane == i1) | (lane == i2), 1.0, 0.0)
    r_id = lax.broadcasted_iota(jnp.int32, (shape[0], shape[0]), 0)
    c_id = lax.broadcasted_iota(jnp.int32, (shape[0], shape[0]), 1)
    before = jnp.where(c_id < r_id, 1.0, 0.0).astype(BF16)
    rank = _dot(before, chosen.astype(BF16)) + count_ref[...]
    rank1 = jnp.sum(jnp.where(lane == i1, rank, 0.0), axis=-1, keepdims=True)
    rank2 = jnp.sum(jnp.where(lane == i2, rank, 0.0), axis=-1, keepdims=True)
    count_ref[...] += jnp.sum(chosen, axis=0, keepdims=True)
    rec = jnp.zeros(shape, F32)
    for k, val in ((R_E1, i1), (R_E2, i2), (R_W1, w1), (R_W2, w2),
                   (R_RANK1, rank1), (R_RANK2, rank2)):
        rec = jnp.where(lane == float(k), val, rec)
    return rec


def _mix_out_kernel(*refs, route):
    (x_ref, r_ref, a_ref, gr_ref, ga_ref, mod_ref, n2_ref, wr_ref, wa_ref, wo_ref) = refs[:10]
    if route:
        wrt_ref, brt_ref, x1_ref, h2_ref, rec_ref, cnt_ref, count_ref, hbuf, hsem = refs[10:]
    else:
        x1_ref, h2_ref = refs[10:]
    merged = (_sigmoid(gr_ref[...].astype(F32)) * _dot(r_ref[...].astype(BF16), wr_ref[...])
              + _sigmoid(ga_ref[...].astype(F32)) * _dot(a_ref[...], wa_ref[...]))
    mix = _dot(merged.astype(BF16), wo_ref[...])
    x1 = x_ref[...] + mod_ref[2:3, :] * mix
    x1_ref[...] = x1
    h2 = _rms(x1, n2_ref[...]) * (1.0 + mod_ref[4:5, :]) + mod_ref[3:4, :]
    if not route:
        h2_ref[...] = h2.astype(BF16)
        return

    @pl.when(pl.program_id(0) == 0)
    def _():
        count_ref[...] = jnp.zeros_like(count_ref)

    rec_ref[...] = _route(h2, wrt_ref, brt_ref, count_ref)
    cnt_ref[...] = count_ref[...]

    step = pl.program_id(0)
    slot = step % 2

    def h_copies(k, s):
        rows = pl.ds(k * TM, TM)
        return [pltpu.make_async_copy(hbuf.at[s, :, c * LANES:(c + 1) * LANES], h2_ref.at[rows, c],
                                      hsem.at[s]) for c in range(D_MODEL // LANES)]

    @pl.when(step >= 2)
    def _():
        for cp in h_copies(step - 2, slot):
            cp.wait()

    hbuf[slot] = h2
    for cp in h_copies(step, slot):
        cp.start()

    @pl.when(step == pl.num_programs(0) - 1)
    def _():
        for cp in h_copies(step, slot):
            cp.wait()

        @pl.when(step >= 1)
        def _():
            for cp in h_copies(step - 1, 1 - slot):
                cp.wait()


def _mix_out(x, out_r, out_a, gr, ga, mods, l, norm2, wr_bf, wa_bf, wo_bf, wl, seq_len, lat,
             router=None):
    n = x.shape[0]
    tiles_per_seq = seq_len // TM
    if lat:
        mod_idx = lambda i: (l, 1 + i // tiles_per_seq, 0, 0)
    else:
        mod_idx = lambda i: (l, 0, 0, 0)
    row = lambda i: (i, 0)
    wspec = pl.BlockSpec((None, D_MODEL, D_MODEL), lambda i: (wl, 0, 0), pipeline_mode=RESIDENT)
    in_specs = [pl.BlockSpec((TM, D_MODEL), row)] * 5 + [
        pl.BlockSpec((None, None, 6, D_MODEL), mod_idx),
        pl.BlockSpec((None, 1, D_MODEL), lambda i: (l, 0, 0)),
        wspec, wspec, wspec]
    args = [x, out_r, out_a, gr, ga, mods, norm2, wr_bf, wa_bf, wo_bf]
    out_specs = [pl.BlockSpec((TM, D_MODEL), row)]
    out_shape = [jax.ShapeDtypeStruct((n, D_MODEL), F32)]
    scratch = []
    if router is None:
        out_specs.append(pl.BlockSpec((TM, D_MODEL), row))
        out_shape.append(jax.ShapeDtypeStruct((n, D_MODEL), BF16))
    else:
        j, wr_t, br_pad = router
        in_specs += [pl.BlockSpec((None, D_MODEL, 2 * LANES), lambda i: (j, 0, 0)),
                     pl.BlockSpec((None, 1, LANES), lambda i: (j, 0, 0))]
        args += [wr_t, br_pad]
        out_specs += [pl.BlockSpec(memory_space=pl.ANY),
                      pl.BlockSpec((TM, LANES), row),
                      pl.BlockSpec((1, LANES), lambda i: (0, 0))]
        out_shape += [jax.ShapeDtypeStruct((n, D_MODEL // LANES, LANES), F32),
                      jax.ShapeDtypeStruct((n, LANES), F32),
                      jax.ShapeDtypeStruct((1, LANES), F32)]
        scratch = [pltpu.VMEM((1, LANES), F32), pltpu.VMEM((2, TM, D_MODEL), F32),
                   pltpu.SemaphoreType.DMA((2,))]
    return pl.pallas_call(
        functools.partial(_mix_out_kernel, route=router is not None),
        grid=(n // TM,),
        in_specs=in_specs,
        out_specs=out_specs,
        out_shape=out_shape,
        scratch_shapes=scratch,
        compiler_params=_cparams(("arbitrary",)),
        name=("mix_route" if router is not None else "mix_out") + ("_lat" if lat else "_ctx"),
    )(*args)


def _ffn_kernel(x_ref, h_ref, mod_ref, wg_ref, wu_ref, wd_ref, o_ref, *, f_chunk):
    h = h_ref[...]
    d_ff = wg_ref.shape[1]
    acc = jnp.zeros((h.shape[0], D_MODEL), F32)
    for c in range(d_ff // f_chunk):
        sl = slice(c * f_chunk, (c + 1) * f_chunk)
        act = _silu(_dot(h, wg_ref[:, sl])) * _dot(h, wu_ref[:, sl])
        acc = acc + _dot(act.astype(BF16), wd_ref[sl, :])
    o_ref[...] = x_ref[...] + mod_ref[5:6, :] * acc


def _ffn(x1, h2, mods, l, j, wg_bf, wu_bf, wd_bf, seq_len, lat):
    n = x1.shape[0]
    d_ff = wg_bf.shape[2]
    tiles_per_seq = seq_len // TM
    if lat:
        mod_idx = lambda i: (l, 1 + i // tiles_per_seq, 0, 0)
    else:
        mod_idx = lambda i: (l, 0, 0, 0)
    row = lambda i: (i, 0)
    return pl.pallas_call(
        functools.partial(_ffn_kernel, f_chunk=d_ff // 2),
        grid=(n // TM,),
        in_specs=[pl.BlockSpec((TM, D_MODEL), row), pl.BlockSpec((TM, D_MODEL), row),
                  pl.BlockSpec((None, None, 6, D_MODEL), mod_idx),
                  pl.BlockSpec((None, D_MODEL, d_ff), lambda i: (j, 0, 0), pipeline_mode=RESIDENT),
                  pl.BlockSpec((None, D_MODEL, d_ff), lambda i: (j, 0, 0), pipeline_mode=RESIDENT),
                  pl.BlockSpec((None, d_ff, D_MODEL), lambda i: (j, 0, 0), pipeline_mode=RESIDENT)],
        out_specs=pl.BlockSpec((TM, D_MODEL), row),
        out_shape=jax.ShapeDtypeStruct((n, D_MODEL), F32),
        compiler_params=_cparams(("arbitrary",)),
        name="ffn_lat" if lat else "ffn_ctx",
    )(x1, h2, mods, wg_bf, wu_bf, wd_bf)


def _group_layout(rec, counts, n_tiles):
    cnt = counts[0, :N_EXPERTS].astype(jnp.int32)
    tiles = (cnt + TM_GROUP - 1) // TM_GROUP
    cum = jnp.cumsum(tiles)
    start = (cum - tiles) * TM_GROUP
    experts = jnp.arange(N_EXPERTS, dtype=jnp.int32)

    assert (R_E2, R_RANK2) == (R_E1 + 1, R_RANK1 + 1)
    e12 = rec[:, R_E1:R_E2 + 1].astype(jnp.int32)
    rank12 = rec[:, R_RANK1:R_RANK2 + 1].astype(jnp.int32)
    group_start = jnp.sum(jnp.where(e12[:, :, None] == experts, start, 0), axis=2)
    pos2 = (group_start + rank12).T
    n_active = cum[N_EXPERTS - 1]
    tile_expert = jnp.sum(jnp.arange(n_tiles, dtype=jnp.int32)[:, None] >= cum[None, :], axis=1)
    tile_expert = jnp.minimum(tile_expert, tile_expert[n_active - 1]).astype(jnp.int32)
    tile_ids = jnp.arange(n_tiles, dtype=jnp.int32)
    group_last = jnp.any((tile_ids[:, None] == cum[None, :] - 1) & (tiles[None, :] > 0), axis=1)
    zero_tile = (group_last | (tile_ids >= n_active)).astype(jnp.int32)
    return (pos2.astype(jnp.int32), tile_expert, n_active.reshape(1).astype(jnp.int32), zero_tile)


def _row_copy(src, dst, sem):
    return pltpu.make_async_copy(src, dst, sem)


def _dispatch_kernel(pos_ref, zt_ref, h_ref, xs_ref, zbuf, sem, zsem):
    n_chunks = h_ref.shape[0] // DMA_CHUNK
    n_tiles = xs_ref.shape[0] // TM_GROUP

    zbuf[...] = jnp.zeros_like(zbuf)

    def zero_fill(j):
        return pltpu.make_async_copy(zbuf, xs_ref.at[pl.ds(j * TM_GROUP, TM_GROUP)], zsem)

    def zero_start(j, carry):
        @pl.when(zt_ref[j] != 0)
        def _():
            zero_fill(j).start()
        return carry

    def zero_wait(j, carry):
        @pl.when(zt_ref[j] != 0)
        def _():
            zero_fill(j).wait()
        return carry

    lax.fori_loop(0, n_tiles, zero_start, 0)
    lax.fori_loop(0, n_tiles, zero_wait, 0)

    def drain_chunk():
        rows = pl.ds(0, 2 * DMA_CHUNK)
        _row_copy(xs_ref.at[rows], xs_ref.at[rows], sem).wait()

    def chunk(k, carry):
        for r in range(DMA_CHUNK):
            t = k * DMA_CHUNK + r
            _row_copy(h_ref.at[t], xs_ref.at[pos_ref[0, t]], sem).start(priority=0)
            _row_copy(h_ref.at[t], xs_ref.at[pos_ref[1, t]], sem).start(priority=1)

        @pl.when(k > 0)
        def _():
            drain_chunk()

        return carry

    lax.fori_loop(0, n_chunks, chunk, 0)
    drain_chunk()


def _dispatch(pos2, zero_tile, h2_rows, n_rows):
    tile = h2_rows.shape[1:]
    return pl.pallas_call(
        _dispatch_kernel,
        grid_spec=pltpu.PrefetchScalarGridSpec(
            num_scalar_prefetch=2, grid=(1,),
            in_specs=[pl.BlockSpec(h2_rows.shape, lambda i, p, z: (0, 0, 0))],
            out_specs=pl.BlockSpec(memory_space=pl.ANY),
            scratch_shapes=[pltpu.VMEM((TM_GROUP,) + tile, F32),
                            pltpu.SemaphoreType.DMA(()), pltpu.SemaphoreType.DMA(())]),
        out_shape=jax.ShapeDtypeStruct((n_rows,) + tile, F32),
        compiler_params=_cparams(("arbitrary",)),
        name="moe_dispatch",
    )(pos2, zero_tile, h2_rows)


def _expert_kernel(te_ref, na_ref, xs_ref, wgu_ref, wd_ref, ys_ref,
                   xbuf, ybuf, sem_in, sem_out):
    del te_ref
    i = pl.program_id(0)
    last = pl.num_programs(0) - 1
    n_active = na_ref[0]
    slot = i % 2
    n_col = D_MODEL // LANES

    def in_copies(tile, s):
        rows = pl.ds(tile * TM_GROUP, TM_GROUP)
        return [pltpu.make_async_copy(xs_ref.at[rows, c], xbuf.at[s, :, c * LANES:(c + 1) * LANES],
                                      sem_in.at[s]) for c in range(n_col)]

    def out_copies(tile, s):
        rows = pl.ds(tile * TM_GROUP, TM_GROUP)
        return [pltpu.make_async_copy(ybuf.at[s, :, c * LANES:(c + 1) * LANES], ys_ref.at[rows, c],
                                      sem_out.at[s]) for c in range(n_col)]

    @pl.when(i == 0)
    def _():
        for cp in in_copies(0, 0):
            cp.start()

    @pl.when(i + 1 < n_active)
    def _():
        for cp in in_copies(i + 1, 1 - slot):
            cp.start()

    @pl.when(i >= 2)
    def _():
        for cp in out_copies(i - 2, slot):
            cp.wait()

    @pl.when(i < n_active)
    def _():
        for cp in in_copies(i, slot):
            cp.wait()
        xb = xbuf[slot].astype(BF16)
        z = _dot(xb, wgu_ref[...])
        d_e = wd_ref.shape[0]
        act = _silu(z[:, :d_e]) * z[:, d_e:]
        ybuf[slot] = _dot(act.astype(BF16), wd_ref[...])

    @pl.when(i >= n_active)
    def _():
        ybuf[slot] = jnp.zeros((TM_GROUP, D_MODEL), F32)

    for cp in out_copies(i, slot):
        cp.start()

    @pl.when(i == last)
    def _():
        for cp in out_copies(i, slot):
            cp.wait()

        @pl.when(i >= 1)
        def _():
            for cp in out_copies(i - 1, 1 - slot):
                cp.wait()


def _experts(tile_expert, n_active, xs, j, wgu_bf, wd_bf):
    n_rows = xs.shape[0]
    d_e = wd_bf.shape[2]
    wspec = lambda shape: pl.BlockSpec((None, None) + shape, lambda i, te, na: (j, te[i], 0, 0))
    return pl.pallas_call(
        _expert_kernel,
        grid_spec=pltpu.PrefetchScalarGridSpec(
            num_scalar_prefetch=2, grid=(n_rows // TM_GROUP,),
            in_specs=[pl.BlockSpec(memory_space=pl.ANY),
                      wspec((D_MODEL, 2 * d_e)), wspec((d_e, D_MODEL))],
            out_specs=pl.BlockSpec(memory_space=pl.ANY),
            scratch_shapes=[pltpu.VMEM((2, TM_GROUP, D_MODEL), F32),
                            pltpu.VMEM((2, TM_GROUP, D_MODEL), F32),
                            pltpu.SemaphoreType.DMA((2,)), pltpu.SemaphoreType.DMA((2,))]),
        out_shape=jax.ShapeDtypeStruct(xs.shape, F32),
        compiler_params=_cparams(("arbitrary",)),
        name="moe_experts",
    )(tile_expert, n_active, xs, wgu_bf, wd_bf)


def _combine_kernel(*refs, final):
    pos_ref, y_ref, x_ref, rec_ref, mod_ref = refs[:5]
    g_ref = refs[5] if final else None
    o_ref = refs[5 + int(final)]
    bufs = refs[6 + int(final):-1]
    sems = refs[-1]
    sets = [(bufs[2 * k], bufs[2 * k + 1], sems.at[k]) for k in range(4)]
    i = pl.program_id(0)
    last = pl.num_programs(0) - 1

    def issue(half_idx, buf_set):
        d1, d2, sem = buf_set
        base = half_idx * DMA_CHUNK

        for r in range(DMA_CHUNK):
            _row_copy(y_ref.at[pos_ref[0, base + r]], d1.at[r], sem).start(priority=1)
            _row_copy(y_ref.at[pos_ref[1, base + r]], d2.at[r], sem).start(priority=1)

    def drain(buf_set):
        d1, d2, sem = buf_set
        rows = pl.ds(0, DMA_CHUNK)
        _row_copy(y_ref.at[rows], d1, sem).wait()
        _row_copy(y_ref.at[rows], d2, sem).wait()

    def finish(half, buf_set):
        d1, d2, _ = buf_set
        rows = slice(half * DMA_CHUNK, (half + 1) * DMA_CHUNK)
        w1 = rec_ref[rows, R_W1:R_W1 + 1]
        w2 = rec_ref[rows, R_W2:R_W2 + 1]
        for c in range(D_MODEL // LANES):
            sl = slice(c * LANES, (c + 1) * LANES)
            y = w1 * d1[:, c, :] + w2 * d2[:, c, :]
            o_ref[rows, sl] = x_ref[rows, sl] + mod_ref[5:6, sl] * y
        if final:
            o_ref[rows, :] = _rms(o_ref[rows, :], g_ref[...])

    @pl.when(i == 0)
    def _():
        issue(0, sets[0])
        issue(1, sets[1])

    def half_block(half, cur, nxt):
        drain(cur)

        @pl.when(i < last)
        def _():
            issue(2 * i + 2 + half, nxt)
            finish(half, cur)

        @pl.when(i == last)
        def _():
            finish(half, cur)

    @pl.when(i % 2 == 0)
    def _():
        half_block(0, sets[0], sets[2])
        half_block(1, sets[1], sets[3])

    @pl.when(i % 2 == 1)
    def _():
        half_block(0, sets[2], sets[0])
        half_block(1, sets[3], sets[1])


def _combine(pos2, ys, x1, rec, mods, l, seq_len, lat, final_g):
    n = x1.shape[0]
    final = final_g is not None
    tile = ys.shape[1:]
    tm = 2 * DMA_CHUNK
    tiles_per_seq = max(seq_len // tm, 1)
    if lat:
        mod_idx = lambda i, p: (l, 1 + i // tiles_per_seq, 0, 0)
    else:
        mod_idx = lambda i, p: (l, 0, 0, 0)
    row = lambda i, p: (i, 0)
    slot = pltpu.VMEM((DMA_CHUNK,) + tile, F32)
    in_specs = [pl.BlockSpec(memory_space=pl.ANY),
                pl.BlockSpec((tm, D_MODEL), row),
                pl.BlockSpec((tm, LANES), row),
                pl.BlockSpec((None, None, 6, D_MODEL), mod_idx)]
    args = [pos2, ys, x1, rec, mods]
    if final:
        in_specs.append(pl.BlockSpec((1, D_MODEL), lambda i, p: (0, 0)))
        args.append(final_g)
    return pl.pallas_call(
        functools.partial(_combine_kernel, final=final),
        grid_spec=pltpu.PrefetchScalarGridSpec(
            num_scalar_prefetch=1, grid=(n // tm,),
            in_specs=in_specs,
            out_specs=pl.BlockSpec((tm, D_MODEL), row),
            scratch_shapes=[slot] * 8 + [pltpu.SemaphoreType.DMA((4,))]),
        out_shape=jax.ShapeDtypeStruct((n, D_MODEL), F32),
        compiler_params=_cparams(("arbitrary",)),
        name="moe_combine_final" if final else "moe_combine",
    )(*args)


def _moe(x1, h2_rows, rec, counts, mods, l, j, wgu_bf, wd_bf, seq_len, lat, final_g):
    n = x1.shape[0]
    n_tiles = 2 * n // TM_GROUP + N_EXPERTS
    pos2, tile_expert, n_active, zero_tile = _group_layout(rec, counts, n_tiles)
    xs = _dispatch(pos2, zero_tile, h2_rows, n_tiles * TM_GROUP)
    ys = _experts(tile_expert, n_active, xs, j, wgu_bf, wd_bf)
    return _combine(pos2, ys, x1, rec, mods, l, seq_len, lat, final_g)


def _final_norm_kernel(x_ref, g_ref, o_ref):
    o_ref[...] = _rms(x_ref[...], g_ref[...])


def _final_norm(x, g):
    n = x.shape[0]
    row = lambda i: (i, 0)
    return pl.pallas_call(
        _final_norm_kernel,
        grid=(n // TM_FF,),
        in_specs=[pl.BlockSpec((TM_FF, D_MODEL), row), pl.BlockSpec((1, D_MODEL), lambda i: (0, 0))],
        out_specs=pl.BlockSpec((TM_FF, D_MODEL), row),
        out_shape=jax.ShapeDtypeStruct((n, D_MODEL), F32),
        compiler_params=_cparams(("arbitrary",)),
        name="final_norm",
    )(x, g)


def _rope_tables(t_len):
    t = jnp.arange(t_len)
    pos = jnp.stack([t // GRID_W, t % GRID_W], axis=-1).astype(F32)
    inv = ROPE_THETA ** (-jnp.arange(0, AXIS_DIM, 2, dtype=F32) / AXIS_DIM)
    ang = pos[..., None] * inv
    cos, sin = jnp.cos(ang), jnp.sin(ang)
    c = jnp.concatenate([cos[:, 0], cos[:, 0], cos[:, 1], cos[:, 1]], axis=-1)
    s = jnp.concatenate([-sin[:, 0], sin[:, 0], -sin[:, 1], sin[:, 1]], axis=-1)
    return c, s


def _gate_weights(wa, wi):
    depth = wa.shape[0]
    groups = D_RNN // LANES

    def pair(w):
        w = w.reshape(depth, 2, groups, 2, RNN_BLOCK, RNN_BLOCK)
        z = jnp.zeros_like(w[:, :, :, 0])
        top = jnp.concatenate([w[:, :, :, 0], z], axis=-1)
        bot = jnp.concatenate([z, w[:, :, :, 1]], axis=-1)
        return jnp.concatenate([top, bot], axis=-2)

    w = jnp.concatenate([pair(wa), pair(wi)], axis=-1)
    return (0.5 * jnp.concatenate([w[:, 0], w[:, 1]], axis=-1)).astype(BF16)


def kernel(x_prompt, x_sample, cache_k, cache_v, state_rnn, c, c_ctx, w_mod, b_mod, norm1, norm2, w_in, conv_w, conv_b, lru_wa, lru_ba, lru_wi, lru_bi, lru_lambda, q_norm, k_norm, w_rnn_out, w_attn_out, w_out, ff_gate, ff_up, ff_down, router_w, router_b, exp_gate, exp_up, exp_down, final_norm):
    batch, seq, _ = x_prompt.shape
    dec_batch, dec_seq, _ = x_sample.shape
    depth = w_mod.shape[0]
    past = cache_k.shape[2]
    assert dec_batch + 1 <= MOD_ROWS

    cond8 = jnp.zeros((MOD_ROWS, D_MODEL), F32).at[0].set(c_ctx).at[1:1 + dec_batch].set(c)

    gate_w = _gate_weights(lru_wa, lru_wi)
    n_moe = router_w.shape[0]
    wr_pad = jnp.zeros((n_moe, D_MODEL, LANES), F32).at[:, :, :N_EXPERTS].set(router_w)
    wr_head = wr_pad.astype(BF16)
    wr_t = jnp.concatenate([wr_head, (wr_pad - wr_head.astype(F32)).astype(BF16)], axis=-1)
    br_pad = jnp.zeros((n_moe, 1, LANES), F32).at[:, 0, :N_EXPERTS].set(router_b)

    norm1_3 = norm1.reshape(depth, 1, D_MODEL)
    norm2_3 = norm2.reshape(depth, 1, D_MODEL)
    qn3 = q_norm.reshape(depth, 1, HEAD_DIM)
    kn3 = k_norm.reshape(depth, 1, HEAD_DIM)
    conv_b3 = conv_b.reshape(depth, 1, D_RNN)
    ba4 = (0.5 * lru_ba).reshape(depth, 2, 1, D_RNN)
    bi4 = (0.5 * lru_bi).reshape(depth, 2, 1, D_RNN)
    lam4 = lru_lambda.reshape(depth, 2, 1, D_RNN)
    state5 = state_rnn.reshape(dec_batch, depth, 2, 1, D_RNN)
    cache_k4 = cache_k.reshape(dec_batch, depth, past, KV_DIM)
    cache_v4 = cache_v.reshape(dec_batch, depth, past, KV_DIM)
    rope_tabs = _rope_tables(dec_seq)
    final_g = final_norm.reshape(1, D_MODEL)

    n_steps = dec_batch * dec_seq // TQ
    assert n_steps == dec_batch * dec_seq // TM_ROPE

    def flat(w):
        return w.reshape(-1, w.shape[-1])

    def channel_names(l):
        return ("exp_gate_up", "exp_down") if l % 2 == 1 else ("ff_gate", "ff_up", "ff_down")

    stacked = dict(w_in=(w_in,), w_rnn_out=(w_rnn_out,), w_attn_out=(w_attn_out,), w_out=(w_out,),
                   ff_gate=(ff_gate,), ff_up=(ff_up,), ff_down=(ff_down,),
                   exp_gate_up=(exp_gate, exp_up), exp_down=(exp_down,))

    def layer_shape(name):
        parts = stacked[name]
        return (1,) + parts[0].shape[1:-1] + (sum(w.shape[-1] for w in parts),)

    def layer_index(name, l):
        return l if name.startswith("w_") else l // 2

    def rider(name, l, steps):
        parts = stacked[name]
        rows = flat(parts[0]).shape[0] // parts[0].shape[0] // steps
        return _Rider([flat(w) for w in parts], rows, layer_index(name, l) * steps)

    def names_of(l):
        return ("w_in", "w_rnn_out", "w_attn_out", "w_out") + channel_names(l)

    weights = {l: {} for l in range(depth)}

    def keep(l, names, arrays):
        for name, arr in zip(names, arrays):
            weights[l][name] = arr.reshape(layer_shape(name))

    mod_steps = _modulation_steps(depth)
    mods, w_in0 = _modulation(cond8, w_mod, b_mod, riders=[rider("w_in", 0, mod_steps)])
    mods = mods.reshape(depth, MOD_ROWS, 6, D_MODEL)
    keep(0, ("w_in",), [w_in0])
    ctx_steps = batch * seq // TM

    def run_layer(x, l, lat):
        seq_len = dec_seq if lat else seq
        wts = weights[l]
        convert_next = lat and l + 1 < depth
        if convert_next:
            proj_names = names_of(l + 1)[:4]
            attn_names = names_of(l + 1)[4:]
            proj_riders = [rider(name, l + 1, n_steps) for name in proj_names]
            attn_riders = [rider(name, l + 1, n_steps) for name in attn_names]
            target = l + 1
        elif l == 0 and not lat:
            proj_names, attn_names, attn_riders, target = names_of(0)[1:], (), [], 0
            proj_riders = [rider(name, 0, ctx_steps) for name in proj_names]
        else:
            proj_names, attn_names, proj_riders, attn_riders, target = (), (), [], [], None
        outs = _in_proj(x, mods, l, norm1_3, wts["w_in"], 0, qn3, kn3,
                        rope_tabs if lat else None, seq_len, riders=proj_riders,
                        caches=None if lat else new_kv)
        xr, yr, q, _, kr, v, gr, ga = outs[:8]
        if target is not None:
            keep(target, proj_names, outs[8:8 + len(proj_names)])
        if not lat:
            new_kv[:] = outs[8 + len(proj_names):]
        out_r, end_f, end_b = _rnn(xr, yr, l, conv_w, conv_b3, gate_w, ba4, bi4, lam4,
                                   state5 if lat else None, seq_len)
        if lat:
            outs = _attn_lat(q, kr, v, cache_k4, cache_v4, l, seq_len, riders=attn_riders)
            out_a = outs[0]
            if target is not None:
                keep(target, attn_names, outs[1:])
        else:
            out_a = _attn_ctx(q, kr, v, seq_len)
        mix_args = (x, out_r, out_a, gr, ga, mods, l, norm2_3, wts["w_rnn_out"],
                    wts["w_attn_out"], wts["w_out"], 0, seq_len, lat)
        if l % 2 == 1:
            x1, h2_rows, rec, counts = _mix_out(*mix_args, router=(l // 2, wr_t, br_pad))
            x = _moe(x1, h2_rows, rec, counts, mods, l, 0, wts["exp_gate_up"], wts["exp_down"],
                     seq_len, lat, final_g if l == depth - 1 else None)
        else:
            x1, h2 = _mix_out(*mix_args)
            x = _ffn(x1, h2, mods, l, 0, wts["ff_gate"], wts["ff_up"], wts["ff_down"],
                     seq_len, lat)
            if l == depth - 1:
                x = _final_norm(x, final_g)
        return x, jnp.stack([end_f, end_b], axis=1)

    new_kv = [jnp.zeros((batch, depth, seq, N_KV_HEADS, HEAD_DIM), F32) for _ in range(2)]
    y_ctx = x_prompt.reshape(batch * seq, D_MODEL)
    y_lat = x_sample.reshape(dec_batch * dec_seq, D_MODEL)
    ss = []
    for l in range(depth):
        y_ctx, st = run_layer(y_ctx, l, False)
        ss.append(st)
        y_lat, _ = run_layer(y_lat, l, True)

    new_state = jnp.stack(ss, axis=1)
    return (y_ctx.reshape(batch, seq, D_MODEL), y_lat.reshape(dec_batch, dec_seq, D_MODEL),
            new_kv[0], new_kv[1], new_state)
```
